```python
import jax, jax.numpy as jnp
from jax import lax
import numpy as np


D_MODEL = 1024
BATCH = 16
SEQ = 4096
DEPTH = 1

MIX_WIDTH = D_MODEL
GLA_HEADS = 4
GLA_WIDTH = MIX_WIDTH // 2
GLA_DV = GLA_WIDTH // GLA_HEADS
GLA_DK = GLA_DV // 2
GLA_QK = GLA_HEADS * GLA_DK
GLA_GATE_RANK = 16
GLA_GATE_NORM = 16.0
GDN_HEADS = 4
GDN_WIDTH = MIX_WIDTH - GLA_WIDTH
GDN_DK = GDN_WIDTH // GDN_HEADS
GDN_DV = GDN_DK
CONV_WIDTH = 4
CHUNK = 64
LN_EPS = 1e-5
RMS_EPS = 1e-6
ALPHA = (2.0 * DEPTH) ** 0.25
BETA_INIT = (8.0 * DEPTH) ** -0.25

IN_SIZES = (
    GLA_QK,
    GLA_QK,
    GLA_WIDTH,
    GLA_GATE_RANK,
    GLA_WIDTH,
    3 * GDN_WIDTH,
    GDN_HEADS,
    GDN_HEADS,
    GDN_WIDTH,
)
IN_COLS = sum(IN_SIZES)

kernel_name = 'hymba_gla_gdn_deepnorm_adaln'


def _split_cols(t, sizes):
    idx = []
    acc = 0
    for s in sizes[:-1]:
        acc += s
        idx.append(acc)
    return jnp.split(t, idx, axis=-1)


def layer_norm(u, w, b):
    u32 = u.astype(jnp.float32)
    mu = jnp.mean(u32, axis=-1, keepdims=True)
    var = jnp.mean(jnp.square(u32 - mu), axis=-1, keepdims=True)
    return ((u32 - mu) * lax.rsqrt(var + LN_EPS) * w + b).astype(u.dtype)


def rms_norm(u, w):
    u32 = u.astype(jnp.float32)
    return (u32 * lax.rsqrt(jnp.mean(jnp.square(u32), axis=-1, keepdims=True) + RMS_EPS) * w).astype(u.dtype)


def l2_norm(u):
    u32 = u.astype(jnp.float32)
    return (u32 * lax.rsqrt(jnp.sum(jnp.square(u32), axis=-1, keepdims=True) + RMS_EPS)).astype(u.dtype)


def causal_depthwise_conv(u, w):
    K, C = w.shape
    return lax.conv_general_dilated(u, w[:, None, :].astype(u.dtype), window_strides=(1,),
                                    padding=((K - 1, 0),), dimension_numbers=('NWC', 'WIO', 'NWC'),
                                    feature_group_count=C)


def _to_chunks(t, n_chunks):
    B, T, H = t.shape[:3]
    t = t.reshape((B, n_chunks, CHUNK, H) + t.shape[3:])
    return jnp.moveaxis(t, 3, 1)


def _from_chunks(t):
    B, H, N, C, d = t.shape
    return jnp.moveaxis(t, 1, 3).reshape(B, N * C, H, d)


def gla_chunked(q, k, v, g):
    out_dtype = v.dtype
    B, T, H, dk = q.shape
    dv = v.shape[-1]
    N = T // CHUNK
    q, k, v, g = (_to_chunks(t.astype(jnp.float32), N) for t in (q, k, v, g))
    b = jnp.cumsum(g, axis=3)
    b_ref = b[:, :, :, CHUNK // 2 - 1:CHUNK // 2, :]
    causal = jnp.tril(jnp.ones((CHUNK, CHUNK), dtype=bool))
    att = jnp.einsum('bhnid,bhnjd->bhnij', q * jnp.exp(b - b_ref), k * jnp.exp(b_ref - b))
    att = jnp.where(causal, att, 0.0)
    o_intra = jnp.einsum('bhnij,bhnjv->bhniv', att, v)
    b_last = b[:, :, :, -1, :]
    upd = jnp.einsum('bhncd,bhncv->bhndv', k * jnp.exp(b_last[:, :, :, None, :] - b), v)
    decay = jnp.exp(b_last)

    def step(S, inp):
        d_n, u_n = inp
        return d_n[..., None] * S + u_n, S

    S0 = jnp.zeros((B, H, dk, dv), jnp.float32)
    _, S_prev = lax.scan(step, S0, (jnp.moveaxis(decay, 2, 0), jnp.moveaxis(upd, 2, 0)))
    S_prev = jnp.moveaxis(S_prev, 0, 2)
    o_inter = jnp.einsum('bhncd,bhndv->bhncv', q * jnp.exp(b), S_prev)
    return _from_chunks(o_intra + o_inter).astype(out_dtype)


def gated_delta_chunked(q, k, v, g, beta):
    out_dtype = v.dtype
    B, T, H, dk = q.shape
    dv = v.shape[-1]
    N = T // CHUNK
    q, k, v = (_to_chunks(t.astype(jnp.float32), N) for t in (q, k, v))
    g, beta = (_to_chunks(t.astype(jnp.float32), N) for t in (g, beta))
    d = jnp.cumsum(g, axis=-1)
    causal = jnp.tril(jnp.ones((CHUNK, CHUNK), dtype=bool))
    strict = jnp.tril(jnp.ones((CHUNK, CHUNK), dtype=bool), k=-1)
    L = jnp.exp(jnp.where(causal, d[..., :, None] - d[..., None, :], -jnp.inf))
    k_beta = k * beta[..., None]
    A = jnp.where(strict, jnp.einsum('bhnid,bhnjd->bhnij', k_beta, k) * L, 0.0)
    eye = jnp.eye(CHUNK, dtype=jnp.float32)
    rhs = jnp.concatenate([v * beta[..., None], k_beta * jnp.exp(d)[..., None]], axis=-1)
    sol = lax.linalg.triangular_solve(A + eye, rhs, left_side=True, lower=True, unit_diagonal=True)
    u, w = sol[..., :dv], sol[..., dv:]
    qk = jnp.where(causal, jnp.einsum('bhnid,bhnjd->bhnij', q, k) * L, 0.0)
    q_dec = q * jnp.exp(d)[..., None]
    k_dec = k * jnp.exp(d[..., -1:] - d)[..., None]
    chunk_decay = jnp.exp(d[..., -1])

    def step(S, inp):
        qk_n, q_n, k_n, u_n, w_n, dec_n = inp
        v_new = u_n - jnp.einsum('bhcd,bhdv->bhcv', w_n, S)
        o_n = jnp.einsum('bhcd,bhdv->bhcv', q_n, S) + jnp.einsum('bhij,bhjv->bhiv', qk_n, v_new)
        S = dec_n[..., None, None] * S + jnp.einsum('bhcd,bhcv->bhdv', k_n, v_new)
        return S, o_n

    xs = tuple(jnp.moveaxis(t, 2, 0) for t in (qk, q_dec, k_dec, u, w, chunk_decay))
    S0 = jnp.zeros((B, H, dk, dv), jnp.float32)
    _, o = lax.scan(step, S0, xs)
    return _from_chunks(jnp.moveaxis(o, 0, 2)).astype(out_dtype)


def hybrid_layer(x, c, w_ada, b_ada, w_in, gla_w_gate_up, gla_b_gate, gla_norm_w,
                 gdn_conv_w, gdn_a_log, gdn_dt_bias, gdn_norm_w, w_out, ln_w, ln_b):
    B, T, _ = x.shape
    mod = (c @ w_ada + b_ada)[:, None, :]
    shift, scale, gate = jnp.split(mod, 3, axis=-1)
    h = x * (1.0 + scale) + shift
    proj = h @ w_in
    (gla_q, gla_k, gla_v, gla_lr, gla_og, gdn_qkv, gdn_a, gdn_b, gdn_og) = _split_cols(proj, IN_SIZES)

    q = gla_q.reshape(B, T, GLA_HEADS, GLA_DK) * (GLA_DK ** -0.5)
    k = gla_k.reshape(B, T, GLA_HEADS, GLA_DK)
    v = gla_v.reshape(B, T, GLA_HEADS, GLA_DV)
    z = (gla_lr @ gla_w_gate_up + gla_b_gate).astype(jnp.float32)
    g = (jax.nn.log_sigmoid(z) / GLA_GATE_NORM).reshape(B, T, GLA_HEADS, GLA_DK)
    o_a = rms_norm(gla_chunked(q, k, v, g), gla_norm_w)
    y_a = o_a.reshape(B, T, GLA_WIDTH) * jax.nn.silu(gla_og)

    qkv = jax.nn.silu(causal_depthwise_conv(gdn_qkv, gdn_conv_w))
    q, k, v = jnp.split(qkv, 3, axis=-1)
    q = l2_norm(q.reshape(B, T, GDN_HEADS, GDN_DK)) * (GDN_DK ** -0.5)
    k = l2_norm(k.reshape(B, T, GDN_HEADS, GDN_DK))
    v = v.reshape(B, T, GDN_HEADS, GDN_DV)
    g = -jnp.exp(gdn_a_log.astype(jnp.float32)) * jax.nn.softplus((gdn_a + gdn_dt_bias).astype(jnp.float32))
    beta = jax.nn.sigmoid(gdn_b.astype(jnp.float32))
    o_b = rms_norm(gated_delta_chunked(q, k, v, g, beta), gdn_norm_w)
    y_b = o_b.reshape(B, T, GDN_WIDTH) * jax.nn.silu(gdn_og)

    y = jnp.concatenate([y_a, y_b], axis=-1) @ w_out
    return layer_norm(ALPHA * x + (1.0 + gate) * y, ln_w, ln_b)


def _fwd_setup_inputs(seed: int = 0) -> dict:
    key = jax.random.key(seed)
    ks = jax.random.split(key, 16)
    f32 = jnp.float32
    x = jax.random.normal(ks[0], (BATCH, SEQ, D_MODEL), f32)
    c = jax.random.normal(ks[1], (BATCH, D_MODEL), f32)
    w_ada = jax.random.normal(ks[2], (DEPTH, D_MODEL, 3 * D_MODEL), f32) * (0.1 * D_MODEL ** -0.5)
    b_ada = jax.random.normal(ks[3], (DEPTH, 3 * D_MODEL), f32) * 0.01
    col_scale = jnp.concatenate([
        jnp.ones((2 * GLA_QK,), f32),
        jnp.full((GLA_WIDTH,), BETA_INIT, f32),
        jnp.ones((GLA_GATE_RANK + GLA_WIDTH + 2 * GDN_WIDTH,), f32),
        jnp.full((GDN_WIDTH,), BETA_INIT, f32),
        jnp.ones((2 * GDN_HEADS + GDN_WIDTH,), f32),
    ])
    w_in = jax.random.normal(ks[4], (DEPTH, D_MODEL, IN_COLS), f32) * (D_MODEL ** -0.5) * col_scale
    gla_w_gate_up = jax.random.normal(ks[5], (DEPTH, GLA_GATE_RANK, GLA_QK), f32) * (GLA_GATE_RANK ** -0.5)
    gla_b_gate = jax.random.normal(ks[6], (DEPTH, GLA_QK), f32) * 0.1
    gla_norm_w = 1.0 + 0.01 * jax.random.normal(ks[7], (DEPTH, GLA_DV), f32)
    gdn_conv_w = jax.random.normal(ks[8], (DEPTH, CONV_WIDTH, 3 * GDN_WIDTH), f32) * (CONV_WIDTH ** -0.5)
    gdn_a_log = jnp.log(jax.random.uniform(ks[9], (DEPTH, GDN_HEADS), f32, 1.0, 16.0))
    dt = jnp.exp(jax.random.uniform(ks[10], (DEPTH, GDN_HEADS), f32, np.log(1e-3), np.log(1e-1)))
    gdn_dt_bias = dt + jnp.log(-jnp.expm1(-dt))
    gdn_norm_w = 1.0 + 0.01 * jax.random.normal(ks[11], (DEPTH, GDN_DV), f32)
    w_out = jax.random.normal(ks[12], (DEPTH, MIX_WIDTH, D_MODEL), f32) * (MIX_WIDTH ** -0.5) * BETA_INIT
    ln_w = 1.0 + 0.01 * jax.random.normal(ks[13], (DEPTH, D_MODEL), f32)
    ln_b = 0.01 * jax.random.normal(ks[14], (DEPTH, D_MODEL), f32)
    return {'x': x, 'c': c, 'w_ada': w_ada, 'b_ada': b_ada, 'w_in': w_in,
            'gla_w_gate_up': gla_w_gate_up, 'gla_b_gate': gla_b_gate, 'gla_norm_w': gla_norm_w,
            'gdn_conv_w': gdn_conv_w, 'gdn_a_log': gdn_a_log, 'gdn_dt_bias': gdn_dt_bias,
            'gdn_norm_w': gdn_norm_w, 'w_out': w_out, 'ln_w': ln_w, 'ln_b': ln_b}


def _fwd_reference(x, c, w_ada, b_ada, w_in, gla_w_gate_up, gla_b_gate, gla_norm_w,
              gdn_conv_w, gdn_a_log, gdn_dt_bias, gdn_norm_w, w_out, ln_w, ln_b):
    for layer in range(DEPTH):
        x = hybrid_layer(x, c, w_ada[layer], b_ada[layer], w_in[layer], gla_w_gate_up[layer],
                         gla_b_gate[layer], gla_norm_w[layer], gdn_conv_w[layer], gdn_a_log[layer],
                         gdn_dt_bias[layer], gdn_norm_w[layer], w_out[layer], ln_w[layer], ln_b[layer])
    return x


import jax as _jax
import jax.numpy as _jnp

TWIN_FORMAT = 'train_step'
FWD_PARAMS = ['x', 'c', 'w_ada', 'b_ada', 'w_in', 'gla_w_gate_up', 'gla_b_gate', 'gla_norm_w', 'gdn_conv_w', 'gdn_a_log', 'gdn_dt_bias', 'gdn_norm_w', 'w_out', 'ln_w', 'ln_b']
TWIN_WEIGHTS = ['w_ada', 'b_ada', 'w_in', 'gla_w_gate_up', 'gla_b_gate', 'gla_norm_w', 'gdn_conv_w', 'gdn_a_log', 'gdn_dt_bias', 'gdn_norm_w', 'w_out', 'ln_w', 'ln_b']
TWIN_DIFF_INPUT = 'x'
TWIN_INPUTS = ['x', 'c', 'w_ada', 'b_ada', 'w_in', 'gla_w_gate_up', 'gla_b_gate', 'gla_norm_w', 'gdn_conv_w', 'gdn_a_log', 'gdn_dt_bias', 'gdn_norm_w', 'w_out', 'ln_w', 'ln_b', 'loss_target', 'm_w_ada', 'm_b_ada', 'm_w_in', 'm_gla_w_gate_up', 'm_gla_b_gate', 'm_gla_norm_w', 'm_gdn_conv_w', 'm_gdn_a_log', 'm_gdn_dt_bias', 'm_gdn_norm_w', 'm_w_out', 'm_ln_w', 'm_ln_b', 'v_w_ada', 'v_b_ada', 'v_w_in', 'v_gla_w_gate_up', 'v_gla_b_gate', 'v_gla_norm_w', 'v_gdn_conv_w', 'v_gdn_a_log', 'v_gdn_dt_bias', 'v_gdn_norm_w', 'v_w_out', 'v_ln_w', 'v_ln_b']
TWIN_OUTPUTS = ['loss', 'grad_x', 'grad_w_ada', 'grad_b_ada', 'grad_w_in', 'grad_gla_w_gate_up', 'grad_gla_b_gate', 'grad_gla_norm_w', 'grad_gdn_conv_w', 'grad_gdn_a_log', 'grad_gdn_dt_bias', 'grad_gdn_norm_w', 'grad_w_out', 'grad_ln_w', 'grad_ln_b', 'delta_w_ada', 'delta_b_ada', 'delta_w_in', 'delta_gla_w_gate_up', 'delta_gla_b_gate', 'delta_gla_norm_w', 'delta_gdn_conv_w', 'delta_gdn_a_log', 'delta_gdn_dt_bias', 'delta_gdn_norm_w', 'delta_w_out', 'delta_ln_w', 'delta_ln_b', 'new_m_w_ada', 'new_m_b_ada', 'new_m_w_in', 'new_m_gla_w_gate_up', 'new_m_gla_b_gate', 'new_m_gla_norm_w', 'new_m_gdn_conv_w', 'new_m_gdn_a_log', 'new_m_gdn_dt_bias', 'new_m_gdn_norm_w', 'new_m_w_out', 'new_m_ln_w', 'new_m_ln_b', 'new_v_w_ada', 'new_v_b_ada', 'new_v_w_in', 'new_v_gla_w_gate_up', 'new_v_gla_b_gate', 'new_v_gla_norm_w', 'new_v_gdn_conv_w', 'new_v_gdn_a_log', 'new_v_gdn_dt_bias', 'new_v_gdn_norm_w', 'new_v_w_out', 'new_v_ln_w', 'new_v_ln_b']
TWIN_LEAF_KINDS = {'loss': 'loss', 'grad_x': 'grad_x', 'grad_w_ada': 'grad_w', 'grad_b_ada': 'grad_w', 'grad_w_in': 'grad_w', 'grad_gla_w_gate_up': 'grad_w', 'grad_gla_b_gate': 'grad_w', 'grad_gla_norm_w': 'grad_w', 'grad_gdn_conv_w': 'grad_w', 'grad_gdn_a_log': 'grad_w', 'grad_gdn_dt_bias': 'grad_w', 'grad_gdn_norm_w': 'grad_w', 'grad_w_out': 'grad_w', 'grad_ln_w': 'grad_w', 'grad_ln_b': 'grad_w', 'delta_w_ada': 'delta_w', 'delta_b_ada': 'delta_w', 'delta_w_in': 'delta_w', 'delta_gla_w_gate_up': 'delta_w', 'delta_gla_b_gate': 'delta_w', 'delta_gla_norm_w': 'delta_w', 'delta_gdn_conv_w': 'delta_w', 'delta_gdn_a_log': 'delta_w', 'delta_gdn_dt_bias': 'delta_w', 'delta_gdn_norm_w': 'delta_w', 'delta_w_out': 'delta_w', 'delta_ln_w': 'delta_w', 'delta_ln_b': 'delta_w', 'new_m_w_ada': 'new_m', 'new_m_b_ada': 'new_m', 'new_m_w_in': 'new_m', 'new_m_gla_w_gate_up': 'new_m', 'new_m_gla_b_gate': 'new_m', 'new_m_gla_norm_w': 'new_m', 'new_m_gdn_conv_w': 'new_m', 'new_m_gdn_a_log': 'new_m', 'new_m_gdn_dt_bias': 'new_m', 'new_m_gdn_norm_w': 'new_m', 'new_m_w_out': 'new_m', 'new_m_ln_w': 'new_m', 'new_m_ln_b': 'new_m', 'new_v_w_ada': 'new_v', 'new_v_b_ada': 'new_v', 'new_v_w_in': 'new_v', 'new_v_gla_w_gate_up': 'new_v', 'new_v_gla_b_gate': 'new_v', 'new_v_gla_norm_w': 'new_v', 'new_v_gdn_conv_w': 'new_v', 'new_v_gdn_a_log': 'new_v', 'new_v_gdn_dt_bias': 'new_v', 'new_v_gdn_norm_w': 'new_v', 'new_v_w_out': 'new_v', 'new_v_ln_w': 'new_v', 'new_v_ln_b': 'new_v'}


def _forward(args):
    return _fwd_reference(*[args[k] for k in FWD_PARAMS])


def _output_shape():
    out = _jax.eval_shape(lambda: _forward(_fwd_setup_inputs(0)))
    return out.shape, out.dtype

N_MICROBATCH = 1
ADAM_LR = 0.001
ADAM_B1 = 0.9
ADAM_B2 = 0.999
ADAM_EPS = 1e-08
ADAM_WD = 0.01
ADAM_STEP = 10
PER_EXAMPLE_BATCH_AXIS = {'x': 0, 'c': 0, 'loss_target': 0}
SHARED_INPUTS = []
_WEIGHT_DTYPES = {'w_ada': _jnp.float32, 'b_ada': _jnp.float32, 'w_in': _jnp.float32, 'gla_w_gate_up': _jnp.float32, 'gla_b_gate': _jnp.float32, 'gla_norm_w': _jnp.float32, 'gdn_conv_w': _jnp.float32, 'gdn_a_log': _jnp.float32, 'gdn_dt_bias': _jnp.float32, 'gdn_norm_w': _jnp.float32, 'w_out': _jnp.float32, 'ln_w': _jnp.float32, 'ln_b': _jnp.float32}
MOMENT_SCALE = {'w_ada': 1.298588e-01, 'b_ada': 1.337972e-01, 'w_in': 8.814702e-02, 'gla_w_gate_up': 1.263009e-02, 'gla_b_gate': 4.931549e-02, 'gla_norm_w': 1.523498e-01, 'gdn_conv_w': 5.832889e-02, 'gdn_a_log': 2.762889e-01, 'gdn_dt_bias': 2.779198e-01, 'gdn_norm_w': 1.959214e-01, 'w_out': 1.218601e-01, 'ln_w': 6.388083e+01, 'ln_b': 1.740523e+00}


def _to_microbatches(a, axis):
    t = _jnp.moveaxis(a, axis, 0)
    t = t.reshape((N_MICROBATCH, t.shape[0] // N_MICROBATCH) + t.shape[1:])
    return _jnp.moveaxis(t, 1, axis + 1)


def setup_inputs(seed: int = 0) -> dict:
    inp = _fwd_setup_inputs(seed)
    key = _jax.random.fold_in(_jax.random.key(seed), 7919)
    shape, _ = _output_shape()
    out = dict(inp)
    out["loss_target"] = _jax.random.normal(_jax.random.fold_in(key, 0), shape, _jnp.float32)
    for i, name in enumerate(TWIN_WEIGHTS):
        w = inp[name].astype(_jnp.float32)
        if MOMENT_SCALE is None:
            s = _jnp.sqrt(_jnp.mean(_jnp.square(w)) + 1e-30)
        else:
            s = MOMENT_SCALE[name]
        km, kv = _jax.random.split(_jax.random.fold_in(key, i + 1))
        out[name] = w
        out["m_" + name] = s * _jax.random.normal(km, w.shape, _jnp.float32)
        out["v_" + name] = (s * s) * _jax.random.uniform(kv, w.shape, _jnp.float32, 0.5, 1.5)
    if N_MICROBATCH > 1:
        for name, axis in PER_EXAMPLE_BATCH_AXIS.items():
            out[name] = _to_microbatches(out[name], axis)
    return {'x': out['x'], 'c': out['c'], 'w_ada': out['w_ada'], 'b_ada': out['b_ada'], 'w_in': out['w_in'], 'gla_w_gate_up': out['gla_w_gate_up'], 'gla_b_gate': out['gla_b_gate'], 'gla_norm_w': out['gla_norm_w'], 'gdn_conv_w': out['gdn_conv_w'], 'gdn_a_log': out['gdn_a_log'], 'gdn_dt_bias': out['gdn_dt_bias'], 'gdn_norm_w': out['gdn_norm_w'], 'w_out': out['w_out'], 'ln_w': out['ln_w'], 'ln_b': out['ln_b'], 'loss_target': out['loss_target'], 'm_w_ada': out['m_w_ada'], 'm_b_ada': out['m_b_ada'], 'm_w_in': out['m_w_in'], 'm_gla_w_gate_up': out['m_gla_w_gate_up'], 'm_gla_b_gate': out['m_gla_b_gate'], 'm_gla_norm_w': out['m_gla_norm_w'], 'm_gdn_conv_w': out['m_gdn_conv_w'], 'm_gdn_a_log': out['m_gdn_a_log'], 'm_gdn_dt_bias': out['m_gdn_dt_bias'], 'm_gdn_norm_w': out['m_gdn_norm_w'], 'm_w_out': out['m_w_out'], 'm_ln_w': out['m_ln_w'], 'm_ln_b': out['m_ln_b'], 'v_w_ada': out['v_w_ada'], 'v_b_ada': out['v_b_ada'], 'v_w_in': out['v_w_in'], 'v_gla_w_gate_up': out['v_gla_w_gate_up'], 'v_gla_b_gate': out['v_gla_b_gate'], 'v_gla_norm_w': out['v_gla_norm_w'], 'v_gdn_conv_w': out['v_gdn_conv_w'], 'v_gdn_a_log': out['v_gdn_a_log'], 'v_gdn_dt_bias': out['v_gdn_dt_bias'], 'v_gdn_norm_w': out['v_gdn_norm_w'], 'v_w_out': out['v_w_out'], 'v_ln_w': out['v_ln_w'], 'v_ln_b': out['v_ln_b']}


def _loss(weights, diff, rest, loss_target):
    with _jax.named_scope("forward"):
        args = {**rest, TWIN_DIFF_INPUT: diff, **{k: w.astype(_WEIGHT_DTYPES[k]) for k, w in weights.items()}}
        y = _forward(args)
    with _jax.named_scope("loss_head"):
        err = _jnp.square(y.astype(_jnp.float32) - loss_target)
        return 0.5 * _jnp.sum(_jnp.mean(err, axis=-1)) if err.ndim else 0.5 * err


def _adamw(w, g, m, v):
    m = ADAM_B1 * m + (1.0 - ADAM_B1) * g
    v = ADAM_B2 * v + (1.0 - ADAM_B2) * _jnp.square(g)
    m_hat = m / (1.0 - ADAM_B1 ** ADAM_STEP)
    v_hat = v / (1.0 - ADAM_B2 ** ADAM_STEP)
    delta = -ADAM_LR * (m_hat / (_jnp.sqrt(v_hat) + ADAM_EPS) + ADAM_WD * w)
    return delta, m, v


def reference(x, c, w_ada, b_ada, w_in, gla_w_gate_up, gla_b_gate, gla_norm_w, gdn_conv_w, gdn_a_log, gdn_dt_bias, gdn_norm_w, w_out, ln_w, ln_b, loss_target, m_w_ada, m_b_ada, m_w_in, m_gla_w_gate_up, m_gla_b_gate, m_gla_norm_w, m_gdn_conv_w, m_gdn_a_log, m_gdn_dt_bias, m_gdn_norm_w, m_w_out, m_ln_w, m_ln_b, v_w_ada, v_b_ada, v_w_in, v_gla_w_gate_up, v_gla_b_gate, v_gla_norm_w, v_gdn_conv_w, v_gdn_a_log, v_gdn_dt_bias, v_gdn_norm_w, v_w_out, v_ln_w, v_ln_b):
    given = dict(x=x, c=c, w_ada=w_ada, b_ada=b_ada, w_in=w_in, gla_w_gate_up=gla_w_gate_up, gla_b_gate=gla_b_gate, gla_norm_w=gla_norm_w, gdn_conv_w=gdn_conv_w, gdn_a_log=gdn_a_log, gdn_dt_bias=gdn_dt_bias, gdn_norm_w=gdn_norm_w, w_out=w_out, ln_w=ln_w, ln_b=ln_b, loss_target=loss_target, m_w_ada=m_w_ada, m_b_ada=m_b_ada, m_w_in=m_w_in, m_gla_w_gate_up=m_gla_w_gate_up, m_gla_b_gate=m_gla_b_gate, m_gla_norm_w=m_gla_norm_w, m_gdn_conv_w=m_gdn_conv_w, m_gdn_a_log=m_gdn_a_log, m_gdn_dt_bias=m_gdn_dt_bias, m_gdn_norm_w=m_gdn_norm_w, m_w_out=m_w_out, m_ln_w=m_ln_w, m_ln_b=m_ln_b, v_w_ada=v_w_ada, v_b_ada=v_b_ada, v_w_in=v_w_in, v_gla_w_gate_up=v_gla_w_gate_up, v_gla_b_gate=v_gla_b_gate, v_gla_norm_w=v_gla_norm_w, v_gdn_conv_w=v_gdn_conv_w, v_gdn_a_log=v_gdn_a_log, v_gdn_dt_bias=v_gdn_dt_bias, v_gdn_norm_w=v_gdn_norm_w, v_w_out=v_w_out, v_ln_w=v_ln_w, v_ln_b=v_ln_b)
    weights = {n: given[n] for n in TWIN_WEIGHTS}
    shared = {n: given[n] for n in SHARED_INPUTS}
    per_example = {n: given[n] for n in ['x', 'c']}
    grad_fn = _jax.value_and_grad(_loss, argnums=(0, 1))

    def one_microbatch(ex, loss_target):
        ex = dict(ex)
        diff = ex.pop(TWIN_DIFF_INPUT)
        return grad_fn(weights, diff, {**shared, **ex}, loss_target)

    if N_MICROBATCH == 1:
        loss, (grad_w, grad_x) = one_microbatch(per_example, given["loss_target"])
    else:
        def body(carry, xs):
            loss_sum, grad_sum = carry
            l_k, (gw_k, gx_k) = one_microbatch(xs[0], xs[1])
            with _jax.named_scope("update"):
                return (loss_sum + l_k, _jax.tree.map(_jnp.add, grad_sum, gw_k)), gx_k

        init = (_jnp.zeros((), _jnp.float32), _jax.tree.map(_jnp.zeros_like, weights))
        (loss, grad_w), grad_x = _jax.lax.scan(body, init, (per_example, given["loss_target"]))
    with _jax.named_scope("update"):
        delta_w, new_m, new_v = {}, {}, {}
        for n in TWIN_WEIGHTS:
            delta_w[n], new_m[n], new_v[n] = _adamw(weights[n], grad_w[n], given["m_" + n], given["v_" + n])
    return (loss, grad_x, *[grad_w[n] for n in TWIN_WEIGHTS], *[delta_w[n] for n in TWIN_WEIGHTS],
            *[new_m[n] for n in TWIN_WEIGHTS], *[new_v[n] for n in TWIN_WEIGHTS])
```

```python
import functools

import jax
import jax.numpy as jnp
from jax import lax
from jax.experimental import pallas as pl
from jax.experimental.pallas import tpu as pltpu

F32 = jnp.float32
BF16 = jnp.bfloat16
HI = lax.Precision.HIGHEST
MESH = pl.DeviceIdType.MESH

D_MODEL = 1024
GLA_HEADS = 4
GLA_DK = 64
GLA_DV = 128
GLA_QK = 256
GLA_WIDTH = 512
GLA_RANK = 16
GLA_GATE_NORM = 16.0
GDN_HEADS = 4
GDN_DK = 128
GDN_WIDTH = 512
CONV_K = 4
CHUNK = 64
LN_EPS = 1e-5
RMS_EPS = 1e-6
ALPHA = 2.0 ** 0.25
IN_COLS = 3608

LANE_A = GLA_RANK
LANE_B = GLA_RANK + GDN_HEADS
SMALL_USED = GLA_RANK + 2 * GDN_HEADS

ADAM_LR = 0.001
ADAM_B1 = 0.9
ADAM_B2 = 0.999
ADAM_EPS = 1e-08
ADAM_WD = 0.01
ADAM_STEP = 10

VMEM_LIMIT = 56 * 1024 * 1024


def _iota(shape, dim):
    return lax.broadcasted_iota(jnp.int32, shape, dim)


def _dot(a, b, prec=None):
    return lax.dot_general(a, b, (((1,), (0,)), ((), ())), precision=prec, preferred_element_type=F32)


def _dot_nt(a, b, prec=None):
    return lax.dot_general(a, b, (((1,), (1,)), ((), ())), precision=prec, preferred_element_type=F32)


def _dot_tn(a, b, prec=None):
    return lax.dot_general(a, b, (((0,), (0,)), ((), ())), precision=prec, preferred_element_type=F32)


def _log_sigmoid(z):
    return jnp.minimum(z, 0.0) - jnp.log1p(jnp.exp(-jnp.abs(z)))


def _softplus(z):
    return jnp.maximum(z, 0.0) + jnp.log1p(jnp.exp(-jnp.abs(z)))


def _silu(z):
    return z * jax.nn.sigmoid(z)


def _rms_gate(o, nw, og):
    return o * lax.rsqrt(jnp.mean(o * o, axis=-1, keepdims=True) + RMS_EPS) * nw * _silu(og)


def _params(*sem):
    return pltpu.CompilerParams(dimension_semantics=sem, vmem_limit_bytes=VMEM_LIMIT)


def _gla_chunk(q, k, v0, v1, v2, v3, og0, og1, og2, og3, lr, s0, s1, s2, s3, wg, bg, nw):
    vs, ogs, ss = (v0, v1, v2, v3), (og0, og1, og2, og3), (s0, s1, s2, s3)
    c = q.shape[0]
    row, col = _iota((c, c), 0), _iota((c, c), 1)
    causal = row >= col
    first_half = (_iota((c, 1), 0) < c // 2).astype(F32)
    z = _dot(lr, wg) + bg
    g = _log_sigmoid(z) * (1.0 / GLA_GATE_NORM)
    b = _dot(causal.astype(F32), g, HI)
    b_ref = jnp.sum(g * first_half, axis=0, keepdims=True)
    b_last = jnp.sum(g, axis=0, keepdims=True)
    qs = q * (GLA_DK ** -0.5)
    qe = qs * jnp.exp(b - b_ref)
    ke = k * jnp.exp(b_ref - b)
    qb = qs * jnp.exp(b)
    kd = k * jnp.exp(b_last - b)
    decay = jnp.exp(b_last)
    lane = _iota((1, GLA_QK), 1)
    ys, s_new = [], []
    for h in range(GLA_HEADS):
        m = ((lane >= h * GLA_DK) & (lane < (h + 1) * GLA_DK)).astype(F32)
        att = jnp.where(causal, _dot_nt(qe * m, ke), 0.0)
        o = _dot(att, vs[h]) + _dot_nt(qb, ss[h])
        ys.append(_rms_gate(o, nw, ogs[h]))
        s_new.append(ss[h] * decay + m * _dot_tn(vs[h], kd))
    return tuple(ys) + tuple(s_new)


def _gdn_chunk(q0, q1, q2, q3, k0, k1, k2, k3, v0, v1, v2, v3, og0, og1, og2, og3, gb, s0, s1, s2, s3, nw):
    qs, ks, vs = (q0, q1, q2, q3), (k0, k1, k2, k3), (v0, v1, v2, v3)
    ogs, ss = (og0, og1, og2, og3), (s0, s1, s2, s3)
    c = q0.shape[0]
    row, col = _iota((c, c), 0), _iota((c, c), 1)
    causal, strict = row >= col, row > col
    eye = (row == col).astype(F32)
    lane = _iota((1, 128), 1)
    ys, s_new = [], []
    for h in range(GDN_HEADS):
        q, k, v, s = qs[h], ks[h], vs[h], ss[h]
        g_c = jnp.sum(gb * (lane == LANE_A + h).astype(F32), axis=-1, keepdims=True)
        beta_c = jnp.sum(gb * (lane == LANE_B + h).astype(F32), axis=-1, keepdims=True)
        d_b = _dot(causal.astype(F32), jnp.broadcast_to(g_c, (c, 128)), HI)
        d_c = jnp.sum(d_b * (lane == 0).astype(F32), axis=-1, keepdims=True)
        d_row = _dot_nt(jnp.full((c, 128), 1.0 / 128.0, F32), d_b, HI)
        d_last = jnp.sum(g_c, axis=0, keepdims=True)
        decay_mat = jnp.where(causal, jnp.exp(jnp.where(causal, d_c - d_row, 0.0)), 0.0)
        kb = k * beta_c
        a = jnp.where(strict, _dot_nt(kb, k) * decay_mat, 0.0)
        p = -a
        t = eye + p
        n_sq = max(c.bit_length() - 2, 0)
        for _ in range(n_sq):
            p = _dot(p, p, HI)
            t = t + _dot(t, p, HI)
        u = _dot(t, v * beta_c)
        w = _dot(t, kb * jnp.exp(d_c))
        qk = jnp.where(causal, _dot_nt(q, k) * decay_mat, 0.0)
        q_dec = q * jnp.exp(d_c)
        k_dec = k * jnp.exp(d_last - d_c)
        v_new = u - _dot(w, s)
        o = _dot(q_dec, s) + _dot(qk, v_new)
        ys.append(_rms_gate(o, nw, ogs[h]))
        s_new.append(s * jnp.exp(d_last) + _dot_tn(k_dec, v_new))
    return tuple(ys) + tuple(s_new)


def _gdn_pre_elem(ps, ab, alog_v, dtb_v):
    outs = []
    for j, p in enumerate(ps):
        s = _silu(p)
        if j < 2 * GDN_HEADS:
            s = s * lax.rsqrt(jnp.sum(s * s, axis=-1, keepdims=True) + RMS_EPS)
        if j < GDN_HEADS:
            s = s * (GDN_DK ** -0.5)
        outs.append(s)
    lane = _iota((1, 128), 1)
    is_a = (lane >= LANE_A) & (lane < LANE_A + GDN_HEADS)
    is_b = (lane >= LANE_B) & (lane < LANE_B + GDN_HEADS)
    g = -jnp.exp(alog_v) * _softplus(ab + dtb_v)
    gb = jnp.where(is_a, g, jnp.where(is_b, jax.nn.sigmoid(ab), 0.0))
    return tuple(outs) + (gb,)


def _proj_fwd(x2, sc3, sh3, ws, seq, tm=256):
    n = x2.shape[0]
    tpe = seq // tm
    nw = len(ws)

    def body(x_ref, sc_ref, sh_ref, *refs):
        h = (x_ref[...] * sc_ref[0] + sh_ref[0]).astype(ws[0].dtype)
        for w_ref, o_ref in zip(refs[:nw], refs[nw:]):
            o_ref[...] = _dot(h, w_ref[...])

    row = lambda i: (i, 0)
    per_ex = pl.BlockSpec((1, 1, D_MODEL), lambda i: (i // tpe, 0, 0))
    return pl.pallas_call(
        body, name="proj_fwd", grid=(n // tm,),
        in_specs=[pl.BlockSpec((tm, D_MODEL), row), per_ex, per_ex]
        + [pl.BlockSpec(w.shape, lambda i: (0, 0)) for w in ws],
        out_specs=[pl.BlockSpec((tm, w.shape[1]), row) for w in ws],
        out_shape=[jax.ShapeDtypeStruct((n, w.shape[1]), F32) for w in ws],
        compiler_params=_params("parallel"),
    )(x2, sc3, sh3, *ws)


def _gla_specs(nc, rev):
    def at(j):
        if rev:
            return lambda b, n: (b * nc + nc - 1 - n, j)
        return lambda b, n: (b * nc + n, j)
    return at


def _gla_fwd(pa, pd, wg, bg, nw, bl, seq):
    n = pa.shape[0]
    nc = seq // CHUNK
    at = _gla_specs(nc, False)

    def body(q_ref, k_ref, v_ref, og_ref, lr_ref, wg_ref, bg_ref, nw_ref, y_ref, st_ref, s_scr):
        @pl.when(pl.program_id(1) == 0)
        def _():
            s_scr[...] = jnp.zeros_like(s_scr)

        st_ref[0] = s_scr[...]
        vs = [v_ref[:, h * 128:(h + 1) * 128] for h in range(GLA_HEADS)]
        ogs = [og_ref[:, h * 128:(h + 1) * 128] for h in range(GLA_HEADS)]
        ss = [s_scr[h] for h in range(GLA_HEADS)]
        outs = _gla_chunk(q_ref[...], k_ref[...], *vs, *ogs, lr_ref[...], *ss,
                          wg_ref[...], bg_ref[...], nw_ref[...])
        for h in range(GLA_HEADS):
            y_ref[:, h * 128:(h + 1) * 128] = outs[h]
            s_scr[h] = outs[GLA_HEADS + h]

    const2 = lambda b, n: (0, 0)
    return pl.pallas_call(
        body, name="gla_fwd", grid=(bl, nc),
        in_specs=[pl.BlockSpec((CHUNK, 256), at(0)), pl.BlockSpec((CHUNK, 256), at(1)),
                  pl.BlockSpec((CHUNK, 512), at(1)), pl.BlockSpec((CHUNK, 512), at(2)),
                  pl.BlockSpec((CHUNK, 128), at(0)),
                  pl.BlockSpec(wg.shape, const2), pl.BlockSpec(bg.shape, const2), pl.BlockSpec(nw.shape, const2)],
        out_specs=[pl.BlockSpec((CHUNK, 512), at(0)),
                   pl.BlockSpec((1, GLA_HEADS, 128, 256), lambda b, n: (b * nc + n, 0, 0, 0))],
        out_shape=[jax.ShapeDtypeStruct((n, 512), F32),
                   jax.ShapeDtypeStruct((bl * nc, GLA_HEADS, 128, 256), F32)],
        scratch_shapes=[pltpu.VMEM((GLA_HEADS, 128, 256), F32)],
        compiler_params=_params("arbitrary", "arbitrary"),
    )(pa, pa, pa, pa, pd, wg, bg, nw)


def _gla_bwd(pa, pd, st, dya, wg, bg, nw, bl, seq):
    n = pa.shape[0]
    nc = seq // CHUNK
    at = _gla_specs(nc, True)

    def body(q_ref, k_ref, v_ref, og_ref, lr_ref, st_ref, dy_ref, wg_ref, bg_ref, nw_ref,
             da_ref, dd_ref, dwg_ref, dbg_ref, dnw_ref, ds_scr):
        first = (pl.program_id(0) == 0) & (pl.program_id(1) == 0)

        @pl.when(first)
        def _():
            dwg_ref[...] = jnp.zeros_like(dwg_ref)
            dbg_ref[...] = jnp.zeros_like(dbg_ref)
            dnw_ref[...] = jnp.zeros_like(dnw_ref)

        @pl.when(pl.program_id(1) == 0)
        def _():
            ds_scr[...] = jnp.zeros_like(ds_scr)

        vs = [v_ref[:, h * 128:(h + 1) * 128] for h in range(GLA_HEADS)]
        ogs = [og_ref[:, h * 128:(h + 1) * 128] for h in range(GLA_HEADS)]
        ss = [st_ref[0, h] for h in range(GLA_HEADS)]
        _, vjp = jax.vjp(_gla_chunk, q_ref[...], k_ref[...], *vs, *ogs, lr_ref[...], *ss,
                         wg_ref[...], bg_ref[...], nw_ref[...])
        cts = tuple(dy_ref[:, h * 128:(h + 1) * 128] for h in range(GLA_HEADS))
        cts += tuple(ds_scr[h] for h in range(GLA_HEADS))
        gr = vjp(cts)
        da_ref[:, 0:256] = gr[0]
        da_ref[:, 256:512] = gr[1]
        for h in range(GLA_HEADS):
            da_ref[:, 512 + h * 128:512 + (h + 1) * 128] = gr[2 + h]
            da_ref[:, 1024 + h * 128:1024 + (h + 1) * 128] = gr[6 + h]
            ds_scr[h] = gr[11 + h]
        dd_ref[...] = gr[10]
        dwg_ref[...] += gr[15]
        dbg_ref[...] += gr[16]
        dnw_ref[...] += gr[17]

    const2 = lambda b, n: (0, 0)
    return pl.pallas_call(
        body, name="gla_bwd", grid=(bl, nc),
        in_specs=[pl.BlockSpec((CHUNK, 256), at(0)), pl.BlockSpec((CHUNK, 256), at(1)),
                  pl.BlockSpec((CHUNK, 512), at(1)), pl.BlockSpec((CHUNK, 512), at(2)),
                  pl.BlockSpec((CHUNK, 128), at(0)),
                  pl.BlockSpec((1, GLA_HEADS, 128, 256), lambda b, n: (b * nc + nc - 1 - n, 0, 0, 0)),
                  pl.BlockSpec((CHUNK, 512), at(0)),
                  pl.BlockSpec(wg.shape, const2), pl.BlockSpec(bg.shape, const2), pl.BlockSpec(nw.shape, const2)],
        out_specs=[pl.BlockSpec((CHUNK, 1536), at(0)), pl.BlockSpec((CHUNK, 128), at(0)),
                   pl.BlockSpec(wg.shape, const2), pl.BlockSpec(bg.shape, const2), pl.BlockSpec(nw.shape, const2)],
        out_shape=[jax.ShapeDtypeStruct((n, 1536), F32), jax.ShapeDtypeStruct((n, 128), F32),
                   jax.ShapeDtypeStruct(wg.shape, F32), jax.ShapeDtypeStruct(bg.shape, F32),
                   jax.ShapeDtypeStruct(nw.shape, F32)],
        scratch_shapes=[pltpu.VMEM((GLA_HEADS, 128, 256), F32)],
        compiler_params=_params("arbitrary", "arbitrary"),
    )(pa, pa, pa, pa, pd, st, dya, wg, bg, nw)


def _conv_taps(buf_ref, w_ref, base, rows):
    acc = w_ref[0:1, :] * buf_ref[pl.ds(base, rows), :]
    for k in range(1, CONV_K):
        acc = acc + w_ref[k:k + 1, :] * buf_ref[pl.ds(base + k, rows), :]
    return acc


def _gdn_pre_fwd(pb, pd, cw8, alog_v, dtb_v, bl, seq, tm=256):
    n = pb.shape[0]
    tpe = seq // tm
    t8 = tm // 8

    def body(u_ref, prev_ref, ab_ref, w_ref, al_ref, dt_ref, qkv_ref, gb_ref, buf):
        i = pl.program_id(0)
        keep = (i % tpe != 0).astype(F32)
        buf[0:8, :] = prev_ref[...] * keep
        buf[8:8 + tm, :] = u_ref[...]
        p = _conv_taps(buf, w_ref, 8 - (CONV_K - 1), tm)
        ps = [p[:, j * 128:(j + 1) * 128] for j in range(12)]
        outs = _gdn_pre_elem(ps, ab_ref[...], al_ref[...], dt_ref[...])
        for j in range(12):
            qkv_ref[:, j * 128:(j + 1) * 128] = outs[j]
        gb_ref[...] = outs[12]

    row = lambda i: (i, 0)
    const = lambda i: (0, 0)
    return pl.pallas_call(
        body, name="gdn_pre_fwd", grid=(n // tm,),
        in_specs=[pl.BlockSpec((tm, 1536), row),
                  pl.BlockSpec((8, 1536), lambda i: (jnp.maximum(i * t8 - 1, 0), 0)),
                  pl.BlockSpec((tm, 128), row),
                  pl.BlockSpec((8, 1536), const), pl.BlockSpec((1, 128), const), pl.BlockSpec((1, 128), const)],
        out_specs=[pl.BlockSpec((tm, 1536), row), pl.BlockSpec((tm, 128), row)],
        out_shape=[jax.ShapeDtypeStruct((n, 1536), F32), jax.ShapeDtypeStruct((n, 128), F32)],
        scratch_shapes=[pltpu.VMEM((tm + 8, 1536), F32)],
        compiler_params=_params("parallel"),
    )(pb, pb, pd, cw8, alog_v, dtb_v)


def _gdn_pre_bwd(pb, pd, dqkv, dgb, cw8, alog_v, dtb_v, bl, seq, tm=256):
    n = pb.shape[0]
    tpe = seq // tm
    t8 = tm // 8
    nb8 = n // 8
    ext = tm + 8

    def body(u_ref, prev_ref, next_ref, ab_ref, abn_ref, dq_ref, dqn_ref, dgb_ref, w_ref, al_ref, dt_ref,
             du_ref, dab_ref, dw_ref, dal_ref, ddt_ref, buf, dpbuf):
        i = pl.program_id(0)

        @pl.when(i == 0)
        def _():
            dw_ref[...] = jnp.zeros_like(dw_ref)
            dal_ref[...] = jnp.zeros_like(dal_ref)
            ddt_ref[...] = jnp.zeros_like(ddt_ref)

        keep_prev = (i % tpe != 0).astype(F32)
        keep_next = (i % tpe != tpe - 1).astype(F32)
        buf[0:8, :] = prev_ref[...] * keep_prev
        buf[8:8 + tm, :] = u_ref[...]
        buf[8 + tm:16 + tm, :] = next_ref[...]
        p = _conv_taps(buf, w_ref, 8 - (CONV_K - 1), ext)
        ps = [p[:, j * 128:(j + 1) * 128] for j in range(12)]
        ab = jnp.concatenate([ab_ref[...], abn_ref[...]], axis=0)
        _, vjp = jax.vjp(_gdn_pre_elem, ps, ab, al_ref[...], dt_ref[...])
        zeros8 = jnp.zeros((8, 128), F32)
        cts = tuple(jnp.concatenate([dq_ref[:, j * 128:(j + 1) * 128],
                                     dqn_ref[:, j * 128:(j + 1) * 128] * keep_next], axis=0) for j in range(12))
        cts += (jnp.concatenate([dgb_ref[...], zeros8], axis=0),)
        dps, dab, dal, ddt = vjp(cts)
        for j in range(12):
            dpbuf[:, j * 128:(j + 1) * 128] = dps[j]
        dab_ref[...] = dab[0:tm, :]
        dal_ref[...] += dal
        ddt_ref[...] += ddt
        du = w_ref[0:1, :] * dpbuf[pl.ds(CONV_K - 1, tm), :]
        for k in range(1, CONV_K):
            du = du + w_ref[k:k + 1, :] * dpbuf[pl.ds(CONV_K - 1 - k, tm), :]
        du_ref[...] = du
        dp_own = dpbuf[0:tm, :]
        for k in range(CONV_K):
            dw_ref[k:k + 1, :] += jnp.sum(dp_own * buf[pl.ds(8 - (CONV_K - 1) + k, tm), :], axis=0, keepdims=True)

    row = lambda i: (i, 0)
    prev8 = lambda i: (jnp.maximum(i * t8 - 1, 0), 0)
    next8 = lambda i: (jnp.minimum((i + 1) * t8, nb8 - 1), 0)
    const = lambda i: (0, 0)
    return pl.pallas_call(
        body, name="gdn_pre_bwd", grid=(n // tm,),
        in_specs=[pl.BlockSpec((tm, 1536), row), pl.BlockSpec((8, 1536), prev8), pl.BlockSpec((8, 1536), next8),
                  pl.BlockSpec((tm, 128), row), pl.BlockSpec((8, 128), next8),
                  pl.BlockSpec((tm, 1536), row), pl.BlockSpec((8, 1536), next8),
                  pl.BlockSpec((tm, 128), row),
                  pl.BlockSpec((8, 1536), const), pl.BlockSpec((1, 128), const), pl.BlockSpec((1, 128), const)],
        out_specs=[pl.BlockSpec((tm, 1536), row), pl.BlockSpec((tm, 128), row),
                   pl.BlockSpec((8, 1536), const), pl.BlockSpec((1, 128), const), pl.BlockSpec((1, 128), const)],
        out_shape=[jax.ShapeDtypeStruct((n, 1536), F32), jax.ShapeDtypeStruct((n, 128), F32),
                   jax.ShapeDtypeStruct((8, 1536), F32), jax.ShapeDtypeStruct((1, 128), F32),
                   jax.ShapeDtypeStruct((1, 128), F32)],
        scratch_shapes=[pltpu.VMEM((tm + 16, 1536), F32), pltpu.VMEM((ext, 1536), F32)],
        compiler_params=_params("arbitrary"),
    )(pb, pb, pb, pd, pd, dqkv, dqkv, dgb, cw8, alog_v, dtb_v)


def _gdn_fwd(qkv, gb, pc, nw, bl, seq):
    n = qkv.shape[0]
    nc = seq // CHUNK
    at = _gla_specs(nc, False)

    def body(q_ref, k_ref, v_ref, og_ref, gb_ref, nw_ref, y_ref, st_ref, s_scr):
        @pl.when(pl.program_id(1) == 0)
        def _():
            s_scr[...] = jnp.zeros_like(s_scr)

        st_ref[0] = s_scr[...]
        sl = lambda r: [r[:, h * 128:(h + 1) * 128] for h in range(GDN_HEADS)]
        ss = [s_scr[h] for h in range(GDN_HEADS)]
        outs = _gdn_chunk(*sl(q_ref), *sl(k_ref), *sl(v_ref), *sl(og_ref), gb_ref[...], *ss, nw_ref[...])
        for h in range(GDN_HEADS):
            y_ref[:, h * 128:(h + 1) * 128] = outs[h]
            s_scr[h] = outs[GDN_HEADS + h]

    const2 = lambda b, n: (0, 0)
    return pl.pallas_call(
        body, name="gdn_fwd", grid=(bl, nc),
        in_specs=[pl.BlockSpec((CHUNK, 512), at(0)), pl.BlockSpec((CHUNK, 512), at(1)),
                  pl.BlockSpec((CHUNK, 512), at(2)), pl.BlockSpec((CHUNK, 512), at(0)),
                  pl.BlockSpec((CHUNK, 128), at(0)), pl.BlockSpec(nw.shape, const2)],
        out_specs=[pl.BlockSpec((CHUNK, 512), at(0)),
                   pl.BlockSpec((1, GDN_HEADS, 128, 128), lambda b, n: (b * nc + n, 0, 0, 0))],
        out_shape=[jax.ShapeDtypeStruct((n, 512), F32),
                   jax.ShapeDtypeStruct((bl * nc, GDN_HEADS, 128, 128), F32)],
        scratch_shapes=[pltpu.VMEM((GDN_HEADS, 128, 128), F32)],
        compiler_params=_params("arbitrary", "arbitrary"),
    )(qkv, qkv, qkv, pc, gb, nw)


def _gdn_bwd(qkv, gb, pc, st, dyb, nw, bl, seq):
    n = qkv.shape[0]
    nc = seq // CHUNK
    at = _gla_specs(nc, True)

    def body(q_ref, k_ref, v_ref, og_ref, gb_ref, st_ref, dy_ref, nw_ref,
             dqkv_ref, dog_ref, dgb_ref, dnw_ref, ds_scr):
        first = (pl.program_id(0) == 0) & (pl.program_id(1) == 0)

        @pl.when(first)
        def _():
            dnw_ref[...] = jnp.zeros_like(dnw_ref)

        @pl.when(pl.program_id(1) == 0)
        def _():
            ds_scr[...] = jnp.zeros_like(ds_scr)

        sl = lambda r: [r[:, h * 128:(h + 1) * 128] for h in range(GDN_HEADS)]
        ss = [st_ref[0, h] for h in range(GDN_HEADS)]
        _, vjp = jax.vjp(_gdn_chunk, *sl(q_ref), *sl(k_ref), *sl(v_ref), *sl(og_ref), gb_ref[...], *ss, nw_ref[...])
        cts = tuple(sl(dy_ref)) + tuple(ds_scr[h] for h in range(GDN_HEADS))
        gr = vjp(cts)
        for h in range(GDN_HEADS):
            for part in range(3):
                dqkv_ref[:, part * 512 + h * 128:part * 512 + (h + 1) * 128] = gr[part * 4 + h]
            dog_ref[:, h * 128:(h + 1) * 128] = gr[12 + h]
            ds_scr[h] = gr[17 + h]
        dgb_ref[...] = gr[16]
        dnw_ref[...] += gr[21]

    const2 = lambda b, n: (0, 0)
    return pl.pallas_call(
        body, name="gdn_bwd", grid=(bl, nc),
        in_specs=[pl.BlockSpec((CHUNK, 512), at(0)), pl.BlockSpec((CHUNK, 512), at(1)),
                  pl.BlockSpec((CHUNK, 512), at(2)), pl.BlockSpec((CHUNK, 512), at(0)),
                  pl.BlockSpec((CHUNK, 128), at(0)),
                  pl.BlockSpec((1, GDN_HEADS, 128, 128), lambda b, n: (b * nc + nc - 1 - n, 0, 0, 0)),
                  pl.BlockSpec((CHUNK, 512), at(0)), pl.BlockSpec(nw.shape, const2)],
        out_specs=[pl.BlockSpec((CHUNK, 1536), at(0)), pl.BlockSpec((CHUNK, 512), at(0)),
                   pl.BlockSpec((CHUNK, 128), at(0)), pl.BlockSpec(nw.shape, const2)],
        out_shape=[jax.ShapeDtypeStruct((n, 1536), F32), jax.ShapeDtypeStruct((n, 512), F32),
                   jax.ShapeDtypeStruct((n, 128), F32), jax.ShapeDtypeStruct(nw.shape, F32)],
        scratch_shapes=[pltpu.VMEM((GDN_HEADS, 128, 128), F32)],
        compiler_params=_params("arbitrary", "arbitrary"),
    )(qkv, qkv, qkv, pc, gb, st, dyb, nw)


def _out_block(x2, tgt2, ya, yb, g1p3, wo, lnw, lnb, seq, tm=256):
    n = x2.shape[0]
    tpe = seq // tm
    bl = n // seq

    def body(x_ref, t_ref, ya_ref, yb_ref, g_ref, wo_ref, lnw_ref, lnb_ref,
             dz_ref, dya_ref, dyb_ref, dwo_ref, dg_ref, glw_ref, glb_ref, loss_ref):
        i = pl.program_id(0)

        @pl.when(i == 0)
        def _():
            dwo_ref[...] = jnp.zeros_like(dwo_ref)
            glw_ref[...] = jnp.zeros_like(glw_ref)
            glb_ref[...] = jnp.zeros_like(glb_ref)
            loss_ref[...] = jnp.zeros_like(loss_ref)

        @pl.when(i % tpe == 0)
        def _():
            dg_ref[...] = jnp.zeros_like(dg_ref)

        ya16 = ya_ref[...].astype(wo.dtype)
        yb16 = yb_ref[...].astype(wo.dtype)
        wa = wo_ref[0:GLA_WIDTH, :]
        wb = wo_ref[GLA_WIDTH:, :]
        y = _dot(ya16, wa) + _dot(yb16, wb)
        g1p = g_ref[0]
        z = ALPHA * x_ref[...] + g1p * y
        mu = jnp.mean(z, axis=-1, keepdims=True)
        zc = z - mu
        rstd = lax.rsqrt(jnp.mean(zc * zc, axis=-1, keepdims=True) + LN_EPS)
        xhat = zc * rstd
        diff = xhat * lnw_ref[...] + lnb_ref[...] - t_ref[...]
        loss_ref[...] += (0.5 / D_MODEL) * jnp.sum(jnp.sum(diff * diff, axis=-1, keepdims=True), axis=0, keepdims=True)
        dout = diff * (1.0 / D_MODEL)
        glw_ref[...] += jnp.sum(dout * xhat, axis=0, keepdims=True)
        glb_ref[...] += jnp.sum(dout, axis=0, keepdims=True)
        dxh = dout * lnw_ref[...]
        dz = rstd * (dxh - jnp.mean(dxh, axis=-1, keepdims=True)
                     - xhat * jnp.mean(dxh * xhat, axis=-1, keepdims=True))
        dz_ref[...] = dz
        dg_ref[0] += jnp.sum(dz * y, axis=0, keepdims=True)
        dy = (g1p * dz).astype(wo.dtype)
        dya_ref[...] = _dot_nt(dy, wa)
        dyb_ref[...] = _dot_nt(dy, wb)
        dwo_ref[0:GLA_WIDTH, :] += _dot_tn(ya16, dy)
        dwo_ref[GLA_WIDTH:, :] += _dot_tn(yb16, dy)

    row = lambda i: (i, 0)
    const = lambda i: (0, 0)
    per_ex = pl.BlockSpec((1, 1, D_MODEL), lambda i: (i // tpe, 0, 0))
    return pl.pallas_call(
        body, name="out_block", grid=(n // tm,),
        in_specs=[pl.BlockSpec((tm, D_MODEL), row), pl.BlockSpec((tm, D_MODEL), row),
                  pl.BlockSpec((tm, 512), row), pl.BlockSpec((tm, 512), row), per_ex,
                  pl.BlockSpec((D_MODEL, D_MODEL), const), pl.BlockSpec((1, D_MODEL), const),
                  pl.BlockSpec((1, D_MODEL), const)],
        out_specs=[pl.BlockSpec((tm, D_MODEL), row), pl.BlockSpec((tm, 512), row), pl.BlockSpec((tm, 512), row),
                   pl.BlockSpec((D_MODEL, D_MODEL), const), per_ex,
                   pl.BlockSpec((1, D_MODEL), const), pl.BlockSpec((1, D_MODEL), const),
                   pl.BlockSpec((1, 1), const)],
        out_shape=[jax.ShapeDtypeStruct((n, D_MODEL), F32), jax.ShapeDtypeStruct((n, 512), F32),
                   jax.ShapeDtypeStruct((n, 512), F32), jax.ShapeDtypeStruct((D_MODEL, D_MODEL), F32),
                   jax.ShapeDtypeStruct((bl, 1, D_MODEL), F32), jax.ShapeDtypeStruct((1, D_MODEL), F32),
                   jax.ShapeDtypeStruct((1, D_MODEL), F32), jax.ShapeDtypeStruct((1, 1), F32)],
        compiler_params=_params("arbitrary"),
    )(x2, tgt2, ya, yb, g1p3, wo, lnw, lnb)


def _proj_bwd_x(ds, ws, x2, dz, sc3, seq, tm=256):
    n = x2.shape[0]
    tpe = seq // tm
    bl = n // seq

    def body(da_ref, db_ref, dc_ref, dd1_ref, dd2_ref, wa_ref, wb_ref, wc_ref, wd_ref, x_ref, dz_ref, sc_ref,
             gx_ref, dsh_ref, dsc_ref):
        i = pl.program_id(0)

        @pl.when(i % tpe == 0)
        def _():
            dsh_ref[...] = jnp.zeros_like(dsh_ref)
            dsc_ref[...] = jnp.zeros_like(dsc_ref)

        cdt = ws[0].dtype
        dh = _dot_nt(da_ref[...].astype(cdt), wa_ref[...])
        dh += _dot_nt(db_ref[...].astype(cdt), wb_ref[...])
        dh += _dot_nt(dc_ref[...].astype(cdt), wc_ref[...])
        dh += _dot_nt((dd1_ref[...] + dd2_ref[...]).astype(cdt), wd_ref[...])
        gx_ref[...] = dh * sc_ref[0] + ALPHA * dz_ref[...]
        dsh_ref[0] += jnp.sum(dh, axis=0, keepdims=True)
        dsc_ref[0] += jnp.sum(dh * x_ref[...], axis=0, keepdims=True)

    row = lambda i: (i, 0)
    const = lambda i: (0, 0)
    per_ex = pl.BlockSpec((1, 1, D_MODEL), lambda i: (i // tpe, 0, 0))
    da, db, dc, (dd1, dd2) = ds
    return pl.pallas_call(
        body, name="proj_bwd_x", grid=(n // tm,),
        in_specs=[pl.BlockSpec((tm, d.shape[1]), row) for d in (da, db, dc, dd1, dd2)]
        + [pl.BlockSpec(w.shape, const) for w in ws]
        + [pl.BlockSpec((tm, D_MODEL), row), pl.BlockSpec((tm, D_MODEL), row), per_ex],
        out_specs=[pl.BlockSpec((tm, D_MODEL), row), per_ex, per_ex],
        out_shape=[jax.ShapeDtypeStruct((n, D_MODEL), F32), jax.ShapeDtypeStruct((bl, 1, D_MODEL), F32),
                   jax.ShapeDtypeStruct((bl, 1, D_MODEL), F32)],
        compiler_params=_params("arbitrary"),
    )(da, db, dc, dd1, dd2, *ws, x2, dz, sc3)


def _proj_bwd_w(x2, sc3, sh3, ds, seq, cdt, name, tm=256):
    n = x2.shape[0]
    tpe = seq // tm
    flat, groups = [], []
    for d in ds:
        parts = d if isinstance(d, tuple) else (d,)
        groups.append(len(parts))
        flat.extend(parts)
    nin = len(flat)

    def body(x_ref, sc_ref, sh_ref, *refs):
        i = pl.program_id(0)
        outs = refs[nin:]

        @pl.when(i == 0)
        def _():
            for o in outs:
                o[...] = jnp.zeros_like(o)

        h = (x_ref[...] * sc_ref[0] + sh_ref[0]).astype(cdt)
        pos = 0
        for o, cnt in zip(outs, groups):
            d = refs[pos][...]
            for extra in refs[pos + 1:pos + cnt]:
                d = d + extra[...]
            pos += cnt
            o[...] += _dot_tn(h, d.astype(cdt))

    row = lambda i: (i, 0)
    const = lambda i: (0, 0)
    per_ex = pl.BlockSpec((1, 1, D_MODEL), lambda i: (i // tpe, 0, 0))
    widths = [(d[0] if isinstance(d, tuple) else d).shape[1] for d in ds]
    return pl.pallas_call(
        body, name=name, grid=(n // tm,),
        in_specs=[pl.BlockSpec((tm, D_MODEL), row), per_ex, per_ex]
        + [pl.BlockSpec((tm, d.shape[1]), row) for d in flat],
        out_specs=[pl.BlockSpec((D_MODEL, w), const) for w in widths],
        out_shape=[jax.ShapeDtypeStruct((D_MODEL, w), F32) for w in widths],
        compiler_params=_params("arbitrary"),
    )(x2, sc3, sh3, *flat)


def _mod_block(c_all, w_ada_sh, b_blk):
    def body(c_ref, w_ref, b_ref, o_ref):
        o_ref[...] = _dot(c_ref[...], w_ref[...]) + b_ref[...]

    return pl.pallas_call(
        body, name="mod_block",
        out_shape=jax.ShapeDtypeStruct((c_all.shape[0], w_ada_sh.shape[1]), F32),
        compiler_params=pltpu.CompilerParams(vmem_limit_bytes=VMEM_LIMIT),
    )(c_all, w_ada_sh, b_blk)


def _ada_grads(c_all, dmod_all, dmod_blk):
    def body(c_ref, da_ref, db_ref, gw_ref, gb_ref):
        gw_ref[...] = _dot_tn(c_ref[...], db_ref[...])
        gb_ref[...] = jnp.sum(da_ref[...], axis=0, keepdims=True)

    return pl.pallas_call(
        body, name="ada_grads",
        out_shape=[jax.ShapeDtypeStruct((c_all.shape[1], dmod_blk.shape[1]), F32),
                   jax.ShapeDtypeStruct((1, dmod_all.shape[1]), F32)],
        compiler_params=pltpu.CompilerParams(vmem_limit_bytes=VMEM_LIMIT),
    )(c_all, dmod_all, dmod_blk)


def _sum_leading(parts, name):
    def body(p_ref, o_ref):
        acc = p_ref[0]
        for d in range(1, parts.shape[0]):
            acc = acc + p_ref[d]
        o_ref[...] = acc

    return pl.pallas_call(
        body, name=name, out_shape=jax.ShapeDtypeStruct(parts.shape[1:], F32),
        compiler_params=pltpu.CompilerParams(vmem_limit_bytes=VMEM_LIMIT),
    )(parts)


def _row_tile(rows):
    for t in (256, 128, 64, 32, 16, 8):
        if rows % t == 0 and rows > t:
            return t
    return rows


def _add_n(arrs, name):
    rows, cols = arrs[0].shape
    tr = _row_tile(rows)

    def body(*refs):
        acc = refs[0][...]
        for r in refs[1:-1]:
            acc = acc + r[...]
        refs[-1][...] = acc

    spec = pl.BlockSpec((tr, cols), lambda i: (i, 0))
    return pl.pallas_call(
        body, name=name, grid=(rows // tr,), in_specs=[spec] * len(arrs), out_specs=spec,
        out_shape=jax.ShapeDtypeStruct((rows, cols), F32), compiler_params=_params("parallel"),
    )(*arrs)


def _adamw(w, g, m, v, name):
    rows, cols = w.shape
    tr = _row_tile(rows)
    c1 = 1.0 / (1.0 - ADAM_B1 ** ADAM_STEP)
    c2 = 1.0 / (1.0 - ADAM_B2 ** ADAM_STEP)

    def body(w_ref, g_ref, m_ref, v_ref, d_ref, nm_ref, nv_ref):
        gg = g_ref[...]
        nm = ADAM_B1 * m_ref[...] + (1.0 - ADAM_B1) * gg
        nv = ADAM_B2 * v_ref[...] + (1.0 - ADAM_B2) * (gg * gg)
        nm_ref[...] = nm
        nv_ref[...] = nv
        d_ref[...] = -ADAM_LR * ((nm * c1) / (jnp.sqrt(nv * c2) + ADAM_EPS) + ADAM_WD * w_ref[...])

    spec = pl.BlockSpec((tr, cols), lambda i: (i, 0))
    shp = jax.ShapeDtypeStruct((rows, cols), F32)
    return pl.pallas_call(
        body, name=name, grid=(rows // tr,), in_specs=[spec] * 4, out_specs=[spec] * 3,
        out_shape=[shp, shp, shp], compiler_params=_params("parallel"),
    )(w, g, m, v)


def _coords():
    return lax.axis_index("x"), lax.axis_index("y"), lax.axis_index("c")


def _all_gather8(blk, name):
    m_per, n = blk.shape

    def body(x_ref, out_ref, send_sems, recv_sems, local_sem):
        x, y, c = _coords()
        me, sibling = (x, y, c), (x, y, 1 - c)
        chips = [(1 - x, y), (x, 1 - y), (1 - x, 1 - y)]

        def rows(px, py, pc):
            return out_ref.at[pl.ds((4 * px + 2 * py + pc) * m_per, m_per), :]

        def copy(k, block, to, src=None):
            return pltpu.make_async_remote_copy(
                src_ref=rows(*block) if src is None else src, dst_ref=rows(*block),
                send_sem=send_sems.at[k], recv_sem=recv_sems.at[k], device_id=to, device_id_type=MESH)

        mine = pltpu.make_async_copy(x_ref, rows(*me), local_sem)
        mine.start()
        first = [copy(0, me, sibling, src=x_ref)]
        first += [copy(1 + j, me, (*chip, c), src=x_ref) for j, chip in enumerate(chips)]
        for cp in first:
            cp.start()
        passed = [copy(4 + j, (*chip, c), sibling) for j, chip in enumerate(chips)]
        for j, chip in enumerate(chips):
            copy(1 + j, (*chip, c), me).wait_recv()
            passed[j].start()
        copy(0, sibling, me).wait_recv()
        for j, chip in enumerate(chips):
            copy(4 + j, (*chip, 1 - c), me).wait_recv()
        for cp in first + passed:
            cp.wait_send()
        mine.wait()

    return pl.pallas_call(
        body, name=name,
        out_shape=jax.ShapeDtypeStruct((8 * m_per, n), blk.dtype),
        in_specs=[pl.BlockSpec(memory_space=pltpu.VMEM)],
        out_specs=pl.BlockSpec(memory_space=pltpu.VMEM),
        scratch_shapes=[pltpu.SemaphoreType.DMA((7,)), pltpu.SemaphoreType.DMA((7,)), pltpu.SemaphoreType.DMA],
        compiler_params=pltpu.CompilerParams(vmem_limit_bytes=VMEM_LIMIT),
    )(blk)


def _chip_gather(shards, name):
    k_arr = len(shards)

    def body(*refs):
        srcs, dsts = refs[:k_arr], refs[k_arr:2 * k_arr]
        send_sems, recv_sems, local_sems = refs[2 * k_arr:]
        x, y, c = _coords()
        peers = [(1 - x, y, c), (x, 1 - y, c), (1 - x, 1 - y, c)]
        me_chip = 2 * x + y
        local = [pltpu.make_async_copy(srcs[a], dsts[a].at[me_chip], local_sems.at[a]) for a in range(k_arr)]
        for cp in local:
            cp.start()
        sends = []
        for a in range(k_arr):
            for j, peer in enumerate(peers):
                sends.append(pltpu.make_async_remote_copy(
                    src_ref=srcs[a], dst_ref=dsts[a].at[me_chip],
                    send_sem=send_sems.at[a, j], recv_sem=recv_sems.at[a, j], device_id=peer, device_id_type=MESH))
        for cp in sends:
            cp.start()
        for a in range(k_arr):
            for j, peer in enumerate(peers):
                pltpu.make_async_remote_copy(
                    src_ref=srcs[a], dst_ref=dsts[a].at[2 * peer[0] + peer[1]],
                    send_sem=send_sems.at[a, j], recv_sem=recv_sems.at[a, j],
                    device_id=peer, device_id_type=MESH).wait_recv()
        for cp in sends:
            cp.wait_send()
        for cp in local:
            cp.wait()

    any_spec = pl.BlockSpec(memory_space=pl.ANY)
    return pl.pallas_call(
        body, name=name,
        out_shape=[jax.ShapeDtypeStruct((4,) + s.shape, s.dtype) for s in shards],
        in_specs=[any_spec] * k_arr, out_specs=[any_spec] * k_arr,
        scratch_shapes=[pltpu.SemaphoreType.DMA((k_arr, 3)), pltpu.SemaphoreType.DMA((k_arr, 3)),
                        pltpu.SemaphoreType.DMA((k_arr,))],
    )(*shards)


def _chip_scatter(pieces, name):
    k_arr = len(pieces)

    def body(*refs):
        srcs, dsts = refs[:k_arr], refs[k_arr:2 * k_arr]
        send_sems, recv_sems = refs[2 * k_arr:]
        x, y, c = _coords()
        peers = [(1 - x, y, c), (x, 1 - y, c), (1 - x, 1 - y, c)]
        copies = []
        for a in range(k_arr):
            for j, peer in enumerate(peers):
                copies.append(pltpu.make_async_remote_copy(
                    src_ref=srcs[a].at[2 * peer[0] + peer[1]], dst_ref=dsts[a].at[j],
                    send_sem=send_sems.at[a, j], recv_sem=recv_sems.at[a, j], device_id=peer, device_id_type=MESH))
        for cp in copies:
            cp.start()
        for cp in copies:
            cp.wait_recv()
        for cp in copies:
            cp.wait_send()

    any_spec = pl.BlockSpec(memory_space=pl.ANY)
    return pl.pallas_call(
        body, name=name,
        out_shape=[jax.ShapeDtypeStruct((3,) + p.shape[1:], p.dtype) for p in pieces],
        in_specs=[any_spec] * k_arr, out_specs=[any_spec] * k_arr,
        scratch_shapes=[pltpu.SemaphoreType.DMA((k_arr, 3)), pltpu.SemaphoreType.DMA((k_arr, 3))],
    )(*pieces)


def _sibling_swap(arrs, name):
    k_arr = len(arrs)

    def body(*refs):
        srcs, dsts = refs[:k_arr], refs[k_arr:2 * k_arr]
        send_sems, recv_sems = refs[2 * k_arr:]
        x, y, c = _coords()
        copies = [pltpu.make_async_remote_copy(
            src_ref=srcs[a], dst_ref=dsts[a], send_sem=send_sems.at[a], recv_sem=recv_sems.at[a],
            device_id=(x, y, 1 - c), device_id_type=MESH) for a in range(k_arr)]
        for cp in copies:
            cp.start()
        for cp in copies:
            cp.wait_recv()
        for cp in copies:
            cp.wait_send()

    any_spec = pl.BlockSpec(memory_space=pl.ANY)
    return pl.pallas_call(
        body, name=name,
        out_shape=[jax.ShapeDtypeStruct(a.shape, a.dtype) for a in arrs],
        in_specs=[any_spec] * k_arr, out_specs=[any_spec] * k_arr,
        scratch_shapes=[pltpu.SemaphoreType.DMA((k_arr,)), pltpu.SemaphoreType.DMA((k_arr,))],
    )(*arrs)


def _split_w_in(w_in):
    wa = jnp.concatenate([w_in[:, 0:1024], w_in[:, 1040:1552]], axis=1)
    wb = w_in[:, 1552:3088]
    wc = w_in[:, 3096:3608]
    wd = jnp.concatenate([w_in[:, 1024:1040], w_in[:, 3088:3096],
                          jnp.zeros((w_in.shape[0], 128 - SMALL_USED), w_in.dtype)], axis=1)
    return wa, wb, wc, wd


def _merge_dw_in(dwa, dwb, dwc, dwd):
    return jnp.concatenate([dwa[:, 0:1024], dwd[:, 0:GLA_RANK], dwa[:, 1024:1536], dwb,
                            dwd[:, GLA_RANK:SMALL_USED], dwc], axis=1)


def _local_step(x, mod, w_in16, w_out16, gla_wg, gla_bg, gla_nw, conv_w, a_log, dt_bias, gdn_nw, ln_w, ln_b, tgt):
    bl, seq, _ = x.shape
    n = bl * seq
    x2 = x.reshape(n, D_MODEL)
    tgt2 = tgt.reshape(n, D_MODEL)
    sh3 = mod[:, None, 0:D_MODEL]
    sc3 = 1.0 + mod[:, None, D_MODEL:2 * D_MODEL]
    g1p3 = 1.0 + mod[:, None, 2 * D_MODEL:]
    ws = _split_w_in(w_in16)
    wg = jnp.concatenate([gla_wg, jnp.zeros((128 - GLA_RANK, GLA_QK), F32)], axis=0)
    cw8 = jnp.concatenate([conv_w, jnp.zeros((8 - CONV_K, conv_w.shape[1]), F32)], axis=0)
    alog_v = jnp.zeros((1, 128), F32).at[:, LANE_A:LANE_A + GDN_HEADS].set(a_log)
    dtb_v = jnp.zeros((1, 128), F32).at[:, LANE_A:LANE_A + GDN_HEADS].set(dt_bias)

    pa, pb, pc, pd = _proj_fwd(x2, sc3, sh3, ws, seq)
    ya, st_a = _gla_fwd(pa, pd, wg, gla_bg, gla_nw, bl, seq)
    qkv, gb = _gdn_pre_fwd(pb, pd, cw8, alog_v, dtb_v, bl, seq)
    yb, st_b = _gdn_fwd(qkv, gb, pc, gdn_nw, bl, seq)
    dz, dya, dyb, d_wo, d_gate, d_lnw, d_lnb, loss = _out_block(x2, tgt2, ya, yb, g1p3, w_out16, ln_w, ln_b, seq)
    da, dd1, d_wg, d_bg, d_nwa = _gla_bwd(pa, pd, st_a, dya, wg, gla_bg, gla_nw, bl, seq)
    dqkv, dc, dgb, d_nwb = _gdn_bwd(qkv, gb, pc, st_b, dyb, gdn_nw, bl, seq)
    db, dd2, d_cw8, d_alog, d_dtb = _gdn_pre_bwd(pb, pd, dqkv, dgb, cw8, alog_v, dtb_v, bl, seq)
    gx, d_sh, d_sc = _proj_bwd_x((da, db, dc, (dd1, dd2)), ws, x2, dz, sc3, seq)
    (dwa,) = _proj_bwd_w(x2, sc3, sh3, [da], seq, w_in16.dtype, "proj_bwd_w_a")
    dwb, dwc, dwd = _proj_bwd_w(x2, sc3, sh3, [db, dc, (dd1, dd2)], seq, w_in16.dtype, "proj_bwd_w_bcd")
    grads = dict(
        w_in=_merge_dw_in(dwa, dwb, dwc, dwd),
        w_out=d_wo,
        gla_w_gate_up=d_wg[0:GLA_RANK, :],
        gla_b_gate=d_bg,
        gla_norm_w=d_nwa,
        gdn_conv_w=d_cw8[0:CONV_K, :],
        gdn_a_log=d_alog[:, LANE_A:LANE_A + GDN_HEADS],
        gdn_dt_bias=d_dtb[:, LANE_A:LANE_A + GDN_HEADS],
        gdn_norm_w=d_nwb,
        ln_w=d_lnw,
        ln_b=d_lnb,
        mod=jnp.concatenate([d_sh[:, 0, :], d_sc[:, 0, :], d_gate[:, 0, :]], axis=1),
    )
    return loss, gx.reshape(bl, seq, D_MODEL), grads


_SMALL = (("gla_b_gate", 256), ("gla_norm_w", 128), ("gdn_a_log", 4), ("gdn_dt_bias", 4), ("gdn_norm_w", 128),
          ("ln_w", 1024), ("ln_b", 1024), ("gla_w_gate_up", 16 * 256), ("gdn_conv_w", 4 * 1536), ("mod", 2 * 3072))


def _pack_small(grads):
    flat = jnp.concatenate([grads[k].reshape(-1) for k, _ in _SMALL])
    total = sum(sz for _, sz in _SMALL)
    rows = -(-total // 1024) * 8
    return jnp.concatenate([flat, jnp.zeros((rows * 128 - total,), F32)]).reshape(rows, 128)


def _unpack_small(flat):
    out, pos = {}, 0
    for k, sz in _SMALL:
        out[k] = flat[pos:pos + sz]
        pos += sz
    return out


def kernel(x, c, w_ada, b_ada, w_in, gla_w_gate_up, gla_b_gate, gla_norm_w, gdn_conv_w, gdn_a_log, gdn_dt_bias, gdn_norm_w, w_out, ln_w, ln_b, loss_target, m_w_ada, m_b_ada, m_w_in, m_gla_w_gate_up, m_gla_b_gate, m_gla_norm_w, m_gdn_conv_w, m_gdn_a_log, m_gdn_dt_bias, m_gdn_norm_w, m_w_out, m_ln_w, m_ln_b, v_w_ada, v_b_ada, v_w_in, v_gla_w_gate_up, v_gla_b_gate, v_gla_norm_w, v_gdn_conv_w, v_gdn_a_log, v_gdn_dt_bias, v_gdn_norm_w, v_w_out, v_ln_w, v_ln_b):
    ix, iy, ic = _coords()
    chip = 2 * ix + iy
    dev = 4 * ix + 2 * iy + ic
    bl = x.shape[0]
    ndev = 8

    c_all = _all_gather8(c.reshape(8, -1), "gather_c").reshape(ndev * bl, D_MODEL)
    ada_cols = w_ada.shape[2]
    b_blk = lax.dynamic_slice_in_dim(b_ada, chip * ada_cols, ada_cols, axis=1)
    mod_blk = _mod_block(c_all, w_ada[0], b_blk)
    mod_g = _all_gather8(mod_blk, "gather_mod").reshape(ndev, ndev * bl, ada_cols)
    mod_all = jnp.concatenate([mod_g[2 * j] for j in range(4)], axis=1)
    mod = lax.dynamic_slice_in_dim(mod_all, dev * bl, bl, axis=0)

    w_in_g, w_out_g, wg_g, cw_g = _chip_gather(
        [w_in[0].astype(BF16), w_out[0].astype(BF16), gla_w_gate_up[0], gdn_conv_w[0]], "gather_weights")
    w_in16 = jnp.concatenate([w_in_g[j] for j in range(4)], axis=1)
    w_out16 = w_out_g.reshape(D_MODEL, D_MODEL)
    gla_wg = jnp.concatenate([wg_g[j] for j in range(4)], axis=1)
    conv_w = jnp.concatenate([cw_g[j] for j in range(4)], axis=1)

    loss, grad_x, gr = _local_step(x, mod, w_in16, w_out16, gla_wg, gla_b_gate, gla_norm_w, conv_w,
                                   gdn_a_log, gdn_dt_bias, gdn_norm_w, ln_w, ln_b, loss_target)
    loss = lax.psum(loss[0, 0], ("x", "y", "c"))

    packed = _pack_small(gr)
    prow = packed.shape[0]
    gathered = _all_gather8(packed, "gather_small").reshape(ndev, prow, 128)
    small = _unpack_small(_sum_leading(gathered, "sum_small").reshape(-1))
    mod_rows = gathered.reshape(ndev, prow * 128)[:, sum(sz for _, sz in _SMALL[:-1]):][:, :bl * 3 * D_MODEL]
    dmod_all = mod_rows.reshape(ndev * bl, 3 * D_MODEL)
    dmod_blk = lax.dynamic_slice_in_dim(dmod_all, chip * ada_cols, ada_cols, axis=1)
    g_w_ada, g_b_ada = _ada_grads(c_all, dmod_all, dmod_blk)
    wg_cols = gla_w_gate_up.shape[2]
    g_wg = lax.dynamic_slice_in_dim(small["gla_w_gate_up"].reshape(GLA_RANK, GLA_QK), chip * wg_cols, wg_cols, axis=1)
    cw_cols = gdn_conv_w.shape[2]
    g_cw = lax.dynamic_slice_in_dim(small["gdn_conv_w"].reshape(CONV_K, 3 * GDN_WIDTH), chip * cw_cols, cw_cols, axis=1)

    in_cols = w_in.shape[2]
    out_rows = w_out.shape[1]
    p_in = jnp.stack([gr["w_in"][:, j * in_cols:(j + 1) * in_cols] for j in range(4)])
    p_out = gr["w_out"].reshape(4, out_rows, D_MODEL)
    h_in, h_out = D_MODEL // 2, out_rows // 2
    mine_in = lax.dynamic_slice_in_dim(p_in, ic * h_in, h_in, axis=1)
    mine_out = lax.dynamic_slice_in_dim(p_out, ic * h_out, h_out, axis=1)
    theirs_in = lax.dynamic_slice_in_dim(p_in, (1 - ic) * h_in, h_in, axis=1)
    theirs_out = lax.dynamic_slice_in_dim(p_out, (1 - ic) * h_out, h_out, axis=1)
    got_in, got_out = _sibling_swap([theirs_in, theirs_out], "swap_halves")
    chip_in = _add_n([mine_in.reshape(4 * h_in, in_cols), got_in.reshape(4 * h_in, in_cols)], "chip_sum_in")
    chip_out = _add_n([mine_out.reshape(4 * h_out, D_MODEL), got_out.reshape(4 * h_out, D_MODEL)], "chip_sum_out")
    chip_in = chip_in.reshape(4, h_in, in_cols)
    chip_out = chip_out.reshape(4, h_out, D_MODEL)
    rs_in, rs_out = _chip_scatter([chip_in, chip_out], "scatter_grads")
    own_in = lax.dynamic_index_in_dim(chip_in, chip, axis=0, keepdims=False)
    own_out = lax.dynamic_index_in_dim(chip_out, chip, axis=0, keepdims=False)
    half_in = _add_n([own_in, rs_in[0], rs_in[1], rs_in[2]], "reduce_in")
    half_out = _add_n([own_out, rs_out[0], rs_out[1], rs_out[2]], "reduce_out")
    sib_in, sib_out = _sibling_swap([half_in, half_out], "swap_result")
    g_w_in = jnp.where(ic == 0, jnp.concatenate([half_in, sib_in], axis=0), jnp.concatenate([sib_in, half_in], axis=0))
    g_w_out = jnp.where(ic == 0, jnp.concatenate([half_out, sib_out], axis=0),
                        jnp.concatenate([sib_out, half_out], axis=0))

    grads = dict(
        w_ada=g_w_ada[None], b_ada=g_b_ada, w_in=g_w_in[None], gla_w_gate_up=g_wg[None],
        gla_b_gate=small["gla_b_gate"].reshape(1, -1), gla_norm_w=small["gla_norm_w"].reshape(1, -1),
        gdn_conv_w=g_cw[None], gdn_a_log=small["gdn_a_log"].reshape(1, -1),
        gdn_dt_bias=small["gdn_dt_bias"].reshape(1, -1), gdn_norm_w=small["gdn_norm_w"].reshape(1, -1),
        w_out=g_w_out[None], ln_w=small["ln_w"].reshape(1, -1), ln_b=small["ln_b"].reshape(1, -1))
    weights = dict(w_ada=w_ada, b_ada=b_ada, w_in=w_in, gla_w_gate_up=gla_w_gate_up, gla_b_gate=gla_b_gate,
                   gla_norm_w=gla_norm_w, gdn_conv_w=gdn_conv_w, gdn_a_log=gdn_a_log, gdn_dt_bias=gdn_dt_bias,
                   gdn_norm_w=gdn_norm_w, w_out=w_out, ln_w=ln_w, ln_b=ln_b)
    m_in = dict(w_ada=m_w_ada, b_ada=m_b_ada, w_in=m_w_in, gla_w_gate_up=m_gla_w_gate_up, gla_b_gate=m_gla_b_gate,
                gla_norm_w=m_gla_norm_w, gdn_conv_w=m_gdn_conv_w, gdn_a_log=m_gdn_a_log, gdn_dt_bias=m_gdn_dt_bias,
                gdn_norm_w=m_gdn_norm_w, w_out=m_w_out, ln_w=m_ln_w, ln_b=m_ln_b)
    v_in = dict(w_ada=v_w_ada, b_ada=v_b_ada, w_in=v_w_in, gla_w_gate_up=v_gla_w_gate_up, gla_b_gate=v_gla_b_gate,
                gla_norm_w=v_gla_norm_w, gdn_conv_w=v_gdn_conv_w, gdn_a_log=v_gdn_a_log, gdn_dt_bias=v_gdn_dt_bias,
                gdn_norm_w=v_gdn_norm_w, w_out=v_w_out, ln_w=v_ln_w, ln_b=v_ln_b)
    names = list(weights)
    delta, new_m, new_v = {}, {}, {}
    for nm in names:
        shp = weights[nm].shape
        two_d = (-1, shp[-1])
        d, a, b = _adamw(weights[nm].reshape(two_d), grads[nm].reshape(two_d), m_in[nm].reshape(two_d),
                         v_in[nm].reshape(two_d), "adamw_" + nm)
        delta[nm], new_m[nm], new_v[nm] = d.reshape(shp), a.reshape(shp), b.reshape(shp)
        grads[nm] = grads[nm].reshape(shp)
    return (loss, grad_x, *[grads[k] for k in names], *[delta[k] for k in names],
            *[new_m[k] for k in names], *[new_v[k] for k in names])
```

```python
import functools

import jax
import jax.numpy as jnp
from jax import lax
from jax.experimental import pallas as pl
from jax.experimental.pallas import tpu as pltpu

F32 = jnp.float32
BF16 = jnp.bfloat16
HI = lax.Precision.HIGHEST
INV_PREC = lax.Precision.HIGH
MESH = pl.DeviceIdType.MESH

D_MODEL = 1024
GLA_HEADS = 4
GLA_DK = 64
GLA_DV = 128
GLA_QK = 256
GLA_WIDTH = 512
GLA_RANK = 16
GLA_GATE_NORM = 16.0
GDN_HEADS = 4
GDN_DK = 128
GDN_WIDTH = 512
CONV_K = 4
CHUNK = 64
LN_EPS = 1e-5
RMS_EPS = 1e-6
ALPHA = 2.0 ** 0.25
IN_COLS = 3608

LANE_A = GLA_RANK
LANE_B = GLA_RANK + GDN_HEADS
SMALL_USED = GLA_RANK + 2 * GDN_HEADS

ADAM_LR = 0.001
ADAM_B1 = 0.9
ADAM_B2 = 0.999
ADAM_EPS = 1e-08
ADAM_WD = 0.01
ADAM_STEP = 10

VMEM_LIMIT = 56 * 1024 * 1024


def _iota(shape, dim):
    return lax.broadcasted_iota(jnp.int32, shape, dim)


def _dot(a, b, prec=None):
    return lax.dot_general(a, b, (((1,), (0,)), ((), ())), precision=prec, preferred_element_type=F32)


def _dot_nt(a, b, prec=None):
    return lax.dot_general(a, b, (((1,), (1,)), ((), ())), precision=prec, preferred_element_type=F32)


def _dot_tn(a, b, prec=None):
    return lax.dot_general(a, b, (((0,), (0,)), ((), ())), precision=prec, preferred_element_type=F32)


def _log_sigmoid(z):
    return jnp.minimum(z, 0.0) - jnp.log1p(jnp.exp(-jnp.abs(z)))


def _softplus(z):
    return jnp.maximum(z, 0.0) + jnp.log1p(jnp.exp(-jnp.abs(z)))


def _silu(z):
    return z * jax.nn.sigmoid(z)


def _rms_gate(o, nw, og):
    return o * lax.rsqrt(jnp.mean(o * o, axis=-1, keepdims=True) + RMS_EPS) * nw * _silu(og)


def _params(*sem):
    return pltpu.CompilerParams(dimension_semantics=sem, vmem_limit_bytes=VMEM_LIMIT)


def _gla_chunk(qs, ks, lrs, vs, ogs, ss, wg, bg, nw):
    c = qs[0].shape[0]
    units = [divmod(i, GLA_HEADS) for i in range(len(vs))]
    row, col = _iota((c, c), 0), _iota((c, c), 1)
    causal = row >= col
    first_half = (_iota((c, 1), 0) < c // 2).astype(F32)
    lane = _iota((1, GLA_QK), 1)
    masks = [((lane >= h * GLA_DK) & (lane < (h + 1) * GLA_DK)).astype(F32) for h in range(GLA_HEADS)]
    gs = [_log_sigmoid(_dot(lr, wg) + bg) * (1.0 / GLA_GATE_NORM) for lr in lrs]
    bs = [_dot(causal.astype(F32), g, HI) for g in gs]
    b_ref = [jnp.sum(g * first_half, axis=0, keepdims=True) for g in gs]
    b_last = [jnp.sum(g, axis=0, keepdims=True) for g in gs]
    qsc = [q * (GLA_DK ** -0.5) for q in qs]
    qe = [q * jnp.exp(b - br) for q, b, br in zip(qsc, bs, b_ref)]
    ke = [k * jnp.exp(br - b) for k, b, br in zip(ks, bs, b_ref)]
    qb = [q * jnp.exp(b) for q, b in zip(qsc, bs)]
    kd = [k * jnp.exp(bl_ - b) for k, b, bl_ in zip(ks, bs, b_last)]
    decay = [jnp.exp(bl_) for bl_ in b_last]
    att = [jnp.where(causal, _dot_nt(qe[e] * masks[h], ke[e]), 0.0) for e, h in units]
    o_inter = [_dot_nt(qb[e], s) for (e, h), s in zip(units, ss)]
    os_ = [_dot(a, v) + oi for a, v, oi in zip(att, vs, o_inter)]
    upd = [_dot_tn(v, kd[e]) for (e, h), v in zip(units, vs)]
    s_new = [s * decay[e] + masks[h] * up for (e, h), s, up in zip(units, ss, upd)]
    ys = [_rms_gate(o, nw, og) for o, og in zip(os_, ogs)]
    return ys, s_new


def _unit_lower_inverse_chain(a_list):
    c = a_list[0].shape[0]
    eye = (_iota((c, c), 0) == _iota((c, c), 1)).astype(F32)
    ps = [-a for a in a_list]
    ts = [eye + p for p in ps]
    for _ in range(max(c.bit_length() - 2, 0)):
        ps = [_dot(p, p, INV_PREC) for p in ps]
        ts = [t + _dot(t, p, INV_PREC) for t, p in zip(ts, ps)]
    return ts


@jax.custom_vjp
def _unit_lower_inverse(a_list):
    return _unit_lower_inverse_chain(a_list)


def _unit_lower_inverse_fwd(a_list):
    ts = _unit_lower_inverse_chain(a_list)
    return ts, ts


def _unit_lower_inverse_bwd(ts, dts):
    xs = [_dot_nt(dt, t, INV_PREC) for dt, t in zip(dts, ts)]
    return ([-_dot_tn(t, x, INV_PREC) for t, x in zip(ts, xs)],)


_unit_lower_inverse.defvjp(_unit_lower_inverse_fwd, _unit_lower_inverse_bwd)


def _gdn_prep_units(qs, ks, vs, gbs):
    c = qs[0].shape[0]
    units = [divmod(i, GDN_HEADS) for i in range(len(qs))]
    row, col = _iota((c, c), 0), _iota((c, c), 1)
    causal, strict = row >= col, row > col
    lane = _iota((1, 128), 1)
    d_alls = [_dot(causal.astype(F32), gb, HI) for gb in gbs]
    g_c, beta_c, d_c = [], [], []
    for r, h in units:
        sel_a = (lane == LANE_A + h).astype(F32)
        g_c.append(jnp.sum(gbs[r] * sel_a, axis=-1, keepdims=True))
        beta_c.append(jnp.sum(gbs[r] * (lane == LANE_B + h).astype(F32), axis=-1, keepdims=True))
        d_c.append(jnp.sum(d_alls[r] * sel_a, axis=-1, keepdims=True))
    d_last = [jnp.sum(g, axis=0, keepdims=True) for g in g_c]
    d_diff = [jnp.broadcast_to(d, (c, c)) - jnp.broadcast_to(d, (c, c)).T for d in d_c]
    decay_mat = [jnp.where(causal, jnp.exp(jnp.where(causal, dd, 0.0)), 0.0) for dd in d_diff]
    kb = [k * b for k, b in zip(ks, beta_c)]
    a = [jnp.where(strict, _dot_nt(kbi, k) * dm, 0.0) for kbi, k, dm in zip(kb, ks, decay_mat)]
    t = _unit_lower_inverse(a)
    u = [_dot(ti, v * b) for ti, v, b in zip(t, vs, beta_c)]
    w = [_dot(ti, kbi * jnp.exp(d)) for ti, kbi, d in zip(t, kb, d_c)]
    qk = [jnp.where(causal, _dot_nt(q, k) * dm, 0.0) for q, k, dm in zip(qs, ks, decay_mat)]
    q_dec = [q * jnp.exp(d) for q, d in zip(qs, d_c)]
    k_dec = [k * jnp.exp(dl - d) for k, dl, d in zip(ks, d_last, d_c)]
    gamma = [jnp.exp(dl) for dl in d_last]
    return u, w, qk, q_dec, k_dec, gamma


def _sum_all(t):
    return jnp.sum(jnp.sum(t, axis=-1, keepdims=True), axis=0, keepdims=True)


def _gdn_pre_elem(ps, ab, alog_v, dtb_v):
    outs = []
    for j, p in enumerate(ps):
        s = _silu(p)
        if j < 2 * GDN_HEADS:
            s = s * lax.rsqrt(jnp.sum(s * s, axis=-1, keepdims=True) + RMS_EPS)
        if j < GDN_HEADS:
            s = s * (GDN_DK ** -0.5)
        outs.append(s)
    lane = _iota((1, 128), 1)
    is_a = (lane >= LANE_A) & (lane < LANE_A + GDN_HEADS)
    is_b = (lane >= LANE_B) & (lane < LANE_B + GDN_HEADS)
    g = -jnp.exp(alog_v) * _softplus(ab + dtb_v)
    gb = jnp.where(is_a, g, jnp.where(is_b, jax.nn.sigmoid(ab), 0.0))
    return tuple(outs) + (gb,)


def _proj_fwd(x2, sc3, sh3, ws, seq, tm=256):
    n = x2.shape[0]
    tpe = seq // tm
    nw = len(ws)

    def body(x_ref, sc_ref, sh_ref, *refs):
        h = (x_ref[...] * sc_ref[0] + sh_ref[0]).astype(ws[0].dtype)
        for w_ref, o_ref in zip(refs[:nw], refs[nw:]):
            o_ref[...] = _dot(h, w_ref[...])

    row = lambda i: (i, 0)
    per_ex = pl.BlockSpec((1, 1, D_MODEL), lambda i: (i // tpe, 0, 0))
    return pl.pallas_call(
        body, name="proj_fwd", grid=(n // tm,),
        in_specs=[pl.BlockSpec((tm, D_MODEL), row), per_ex, per_ex]
        + [pl.BlockSpec(w.shape, lambda i: (0, 0)) for w in ws],
        out_specs=[pl.BlockSpec((tm, w.shape[1]), row) for w in ws],
        out_shape=[jax.ShapeDtypeStruct((n, w.shape[1]), F32) for w in ws],
        compiler_params=_params("parallel"),
    )(x2, sc3, sh3, *ws)


def _gla_fwd(pa, pd, wg, bg, nw, bl, seq):
    n = pa.shape[0]
    nc = seq // CHUNK
    heads = [(e, h, slice(h * 128, (h + 1) * 128)) for e in range(bl) for h in range(GLA_HEADS)]

    def body(q_ref, k_ref, v_ref, og_ref, lr_ref, wg_ref, bg_ref, nw_ref, y_ref, st_ref, s_scr):
        @pl.when(pl.program_id(0) == 0)
        def _():
            s_scr[...] = jnp.zeros_like(s_scr)

        ss = [s_scr[e, h] for e, h, _ in heads]
        for (e, h, _), s in zip(heads, ss):
            st_ref[e, 0, h] = s
        ys, s_new = _gla_chunk([q_ref[e] for e in range(bl)], [k_ref[e] for e in range(bl)],
                               [lr_ref[e] for e in range(bl)],
                               [v_ref[e, :, cols] for e, _, cols in heads], [og_ref[e, :, cols] for e, _, cols in heads],
                               ss, wg_ref[...], bg_ref[...], nw_ref[...])
        for (e, h, cols), y, s in zip(heads, ys, s_new):
            y_ref[e, :, cols] = y
            s_scr[e, h] = s

    tok = lambda w, j: pl.BlockSpec((bl, CHUNK, w), lambda i: (0, i, j))
    const = lambda i: (0, 0)
    pa3 = pa.reshape(bl, seq, 1536)
    y, st = pl.pallas_call(
        body, name="gla_fwd", grid=(nc,),
        in_specs=[tok(256, 0), tok(256, 1), tok(512, 1), tok(512, 2), tok(128, 0),
                  pl.BlockSpec(wg.shape, const), pl.BlockSpec(bg.shape, const), pl.BlockSpec(nw.shape, const)],
        out_specs=[tok(512, 0), pl.BlockSpec((bl, 1, GLA_HEADS, 128, 256), lambda i: (0, i, 0, 0, 0))],
        out_shape=[jax.ShapeDtypeStruct((bl, seq, 512), F32),
                   jax.ShapeDtypeStruct((bl, nc, GLA_HEADS, 128, 256), F32)],
        scratch_shapes=[pltpu.VMEM((bl, GLA_HEADS, 128, 256), F32)],
        compiler_params=_params("arbitrary"),
    )(pa3, pa3, pa3, pa3, pd.reshape(bl, seq, 128), wg, bg, nw)
    return y.reshape(n, 512), st


def _gla_bwd(pa, pd, st, dya, wg, bg, nw, bl, seq):
    n = pa.shape[0]
    nc = seq // CHUNK
    heads = [(e, h, slice(h * 128, (h + 1) * 128)) for e in range(bl) for h in range(GLA_HEADS)]

    def body(q_ref, k_ref, v_ref, og_ref, lr_ref, st_ref, dy_ref, wg_ref, bg_ref, nw_ref,
             da_ref, dd_ref, dwg_ref, dbg_ref, dnw_ref, ds_scr):
        @pl.when(pl.program_id(0) == 0)
        def _():
            dwg_ref[...] = jnp.zeros_like(dwg_ref)
            dbg_ref[...] = jnp.zeros_like(dbg_ref)
            dnw_ref[...] = jnp.zeros_like(dnw_ref)
            ds_scr[...] = jnp.zeros_like(ds_scr)

        _, vjp = jax.vjp(_gla_chunk, [q_ref[e] for e in range(bl)], [k_ref[e] for e in range(bl)],
                         [lr_ref[e] for e in range(bl)],
                         [v_ref[e, :, cols] for e, _, cols in heads], [og_ref[e, :, cols] for e, _, cols in heads],
                         [st_ref[e, 0, h] for e, h, _ in heads], wg_ref[...], bg_ref[...], nw_ref[...])
        dq, dk, dlr, dv, dog, ds, dwg, dbg, dnw = vjp(([dy_ref[e, :, cols] for e, _, cols in heads],
                                                         [ds_scr[e, h] for e, h, _ in heads]))
        for e in range(bl):
            da_ref[e, :, 0:256] = dq[e]
            da_ref[e, :, 256:512] = dk[e]
            dd_ref[e] = dlr[e]
        for i, (e, h, _) in enumerate(heads):
            da_ref[e, :, 512 + h * 128:512 + (h + 1) * 128] = dv[i]
            da_ref[e, :, 1024 + h * 128:1024 + (h + 1) * 128] = dog[i]
            ds_scr[e, h] = ds[i]
        dwg_ref[...] += dwg
        dbg_ref[...] += dbg
        dnw_ref[...] += dnw

    tok = lambda w, j: pl.BlockSpec((bl, CHUNK, w), lambda i: (0, nc - 1 - i, j))
    const = lambda i: (0, 0)
    pa3 = pa.reshape(bl, seq, 1536)
    da, dd, dwg, dbg, dnw = pl.pallas_call(
        body, name="gla_bwd", grid=(nc,),
        in_specs=[tok(256, 0), tok(256, 1), tok(512, 1), tok(512, 2), tok(128, 0),
                  pl.BlockSpec((bl, 1, GLA_HEADS, 128, 256), lambda i: (0, nc - 1 - i, 0, 0, 0)), tok(512, 0),
                  pl.BlockSpec(wg.shape, const), pl.BlockSpec(bg.shape, const), pl.BlockSpec(nw.shape, const)],
        out_specs=[tok(1536, 0), tok(128, 0),
                   pl.BlockSpec(wg.shape, const), pl.BlockSpec(bg.shape, const), pl.BlockSpec(nw.shape, const)],
        out_shape=[jax.ShapeDtypeStruct((bl, seq, 1536), F32), jax.ShapeDtypeStruct((bl, seq, 128), F32),
                   jax.ShapeDtypeStruct(wg.shape, F32), jax.ShapeDtypeStruct(bg.shape, F32),
                   jax.ShapeDtypeStruct(nw.shape, F32)],
        scratch_shapes=[pltpu.VMEM((bl, GLA_HEADS, 128, 256), F32)],
        compiler_params=_params("arbitrary"),
    )(pa3, pa3, pa3, pa3, pd.reshape(bl, seq, 128), st, dya.reshape(bl, seq, 512), wg, bg, nw)
    return da.reshape(n, 1536), dd.reshape(n, 128), dwg, dbg, dnw


def _conv_taps(buf_ref, w_ref, base, rows):
    acc = w_ref[0:1, :] * buf_ref[pl.ds(base, rows), :]
    for k in range(1, CONV_K):
        acc = acc + w_ref[k:k + 1, :] * buf_ref[pl.ds(base + k, rows), :]
    return acc


def _gdn_pre_fwd(pb, pd, cw8, alog_v, dtb_v, bl, seq, tm=256):
    n = pb.shape[0]
    tpe = seq // tm
    t8 = tm // 8

    def body(u_ref, prev_ref, ab_ref, w_ref, al_ref, dt_ref, qkv_ref, gb_ref, buf):
        i = pl.program_id(0)
        keep = (i % tpe != 0).astype(F32)
        buf[0:8, :] = prev_ref[...] * keep
        buf[8:8 + tm, :] = u_ref[...]
        p = _conv_taps(buf, w_ref, 8 - (CONV_K - 1), tm)
        ps = [p[:, j * 128:(j + 1) * 128] for j in range(12)]
        outs = _gdn_pre_elem(ps, ab_ref[...], al_ref[...], dt_ref[...])
        for j in range(12):
            qkv_ref[:, j * 128:(j + 1) * 128] = outs[j]
        gb_ref[...] = outs[12]

    row = lambda i: (i, 0)
    const = lambda i: (0, 0)
    return pl.pallas_call(
        body, name="gdn_pre_fwd", grid=(n // tm,),
        in_specs=[pl.BlockSpec((tm, 1536), row),
                  pl.BlockSpec((8, 1536), lambda i: (jnp.maximum(i * t8 - 1, 0), 0)),
                  pl.BlockSpec((tm, 128), row),
                  pl.BlockSpec((8, 1536), const), pl.BlockSpec((1, 128), const), pl.BlockSpec((1, 128), const)],
        out_specs=[pl.BlockSpec((tm, 1536), row), pl.BlockSpec((tm, 128), row)],
        out_shape=[jax.ShapeDtypeStruct((n, 1536), F32), jax.ShapeDtypeStruct((n, 128), F32)],
        scratch_shapes=[pltpu.VMEM((tm + 8, 1536), F32)],
        compiler_params=_params("parallel"),
    )(pb, pb, pd, cw8, alog_v, dtb_v)


def _gdn_pre_bwd(pb, pd, dqkv, dgb, cw8, alog_v, dtb_v, bl, seq, tm=256):
    n = pb.shape[0]
    tpe = seq // tm
    t8 = tm // 8
    nb8 = n // 8
    ext = tm + 8

    def body(u_ref, prev_ref, next_ref, ab_ref, abn_ref, dq_ref, dqn_ref, dgb_ref, w_ref, al_ref, dt_ref,
             du_ref, dab_ref, dw_ref, dal_ref, ddt_ref, buf, dpbuf):
        i = pl.program_id(0)

        @pl.when(i == 0)
        def _():
            dw_ref[...] = jnp.zeros_like(dw_ref)
            dal_ref[...] = jnp.zeros_like(dal_ref)
            ddt_ref[...] = jnp.zeros_like(ddt_ref)

        keep_prev = (i % tpe != 0).astype(F32)
        keep_next = (i % tpe != tpe - 1).astype(F32)
        buf[0:8, :] = prev_ref[...] * keep_prev
        buf[8:8 + tm, :] = u_ref[...]
        buf[8 + tm:16 + tm, :] = next_ref[...]
        p = _conv_taps(buf, w_ref, 8 - (CONV_K - 1), ext)
        ps = [p[:, j * 128:(j + 1) * 128] for j in range(12)]
        ab = jnp.concatenate([ab_ref[...], abn_ref[...]], axis=0)
        _, vjp = jax.vjp(_gdn_pre_elem, ps, ab, al_ref[...], dt_ref[...])
        zeros8 = jnp.zeros((8, 128), F32)
        cts = tuple(jnp.concatenate([dq_ref[:, j * 128:(j + 1) * 128],
                                     dqn_ref[:, j * 128:(j + 1) * 128] * keep_next], axis=0) for j in range(12))
        cts += (jnp.concatenate([dgb_ref[...], zeros8], axis=0),)
        dps, dab, dal, ddt = vjp(cts)
        for j in range(12):
            dpbuf[:, j * 128:(j + 1) * 128] = dps[j]
        dab_ref[...] = dab[0:tm, :]
        dal_ref[...] += dal
        ddt_ref[...] += ddt
        du = w_ref[0:1, :] * dpbuf[pl.ds(CONV_K - 1, tm), :]
        for k in range(1, CONV_K):
            du = du + w_ref[k:k + 1, :] * dpbuf[pl.ds(CONV_K - 1 - k, tm), :]
        du_ref[...] = du
        dp_own = dpbuf[0:tm, :]
        for k in range(CONV_K):
            dw_ref[k:k + 1, :] += jnp.sum(dp_own * buf[pl.ds(8 - (CONV_K - 1) + k, tm), :], axis=0, keepdims=True)

    row = lambda i: (i, 0)
    prev8 = lambda i: (jnp.maximum(i * t8 - 1, 0), 0)
    next8 = lambda i: (jnp.minimum((i + 1) * t8, nb8 - 1), 0)
    const = lambda i: (0, 0)
    return pl.pallas_call(
        body, name="gdn_pre_bwd", grid=(n // tm,),
        in_specs=[pl.BlockSpec((tm, 1536), row), pl.BlockSpec((8, 1536), prev8), pl.BlockSpec((8, 1536), next8),
                  pl.BlockSpec((tm, 128), row), pl.BlockSpec((8, 128), next8),
                  pl.BlockSpec((tm, 1536), row), pl.BlockSpec((8, 1536), next8),
                  pl.BlockSpec((tm, 128), row),
                  pl.BlockSpec((8, 1536), const), pl.BlockSpec((1, 128), const), pl.BlockSpec((1, 128), const)],
        out_specs=[pl.BlockSpec((tm, 1536), row), pl.BlockSpec((tm, 128), row),
                   pl.BlockSpec((8, 1536), const), pl.BlockSpec((1, 128), const), pl.BlockSpec((1, 128), const)],
        out_shape=[jax.ShapeDtypeStruct((n, 1536), F32), jax.ShapeDtypeStruct((n, 128), F32),
                   jax.ShapeDtypeStruct((8, 1536), F32), jax.ShapeDtypeStruct((1, 128), F32),
                   jax.ShapeDtypeStruct((1, 128), F32)],
        scratch_shapes=[pltpu.VMEM((tm + 16, 1536), F32), pltpu.VMEM((ext, 1536), F32)],
        compiler_params=_params("arbitrary"),
    )(pb, pb, pb, pd, pd, dqkv, dqkv, dgb, cw8, alog_v, dtb_v)


GDN_PREP_CHUNKS = 2


def _head_cols(ref, rows, base=0):
    return [ref[rows, base + h * 128:base + (h + 1) * 128] for h in range(GDN_HEADS)]


def _gdn_prep(qkv, gb):
    n = qkv.shape[0]
    r_per = GDN_PREP_CHUNKS
    tm = r_per * CHUNK

    def body(q_ref, k_ref, v_ref, gb_ref, u_ref, w_ref, qd_ref, kd_ref, qk_ref, gam_ref):
        rowid = _iota((8, 128), 0)
        chunk_rows = [slice(r * CHUNK, (r + 1) * CHUNK) for r in range(r_per)]
        gather = lambda ref: [t for rows in chunk_rows for t in _head_cols(ref, rows)]
        u, w, qk, qd, kd, gamma = _gdn_prep_units(gather(q_ref), gather(k_ref), gather(v_ref),
                                                  [gb_ref[rows, :] for rows in chunk_rows])
        for r, rows in enumerate(chunk_rows):
            gam = jnp.zeros((8, 128), F32)
            for h in range(GDN_HEADS):
                i = r * GDN_HEADS + h
                cols = slice(h * 128, (h + 1) * 128)
                u_ref[rows, cols] = u[i]
                w_ref[rows, cols] = w[i]
                qd_ref[rows, cols] = qd[i]
                kd_ref[rows, cols] = kd[i]
                qk_ref[r, h] = qk[i]
                gam = jnp.where(rowid == h, gamma[i], gam)
            gam_ref[r] = gam

    tok = lambda j: pl.BlockSpec((tm, 512), lambda i: (i, j))
    return pl.pallas_call(
        body, name="gdn_prep", grid=(n // tm,),
        in_specs=[tok(0), tok(1), tok(2), pl.BlockSpec((tm, 128), lambda i: (i, 0))],
        out_specs=[tok(0)] * 4 + [pl.BlockSpec((r_per, GDN_HEADS, CHUNK, CHUNK), lambda i: (i, 0, 0, 0)),
                                  pl.BlockSpec((r_per, 8, 128), lambda i: (i, 0, 0))],
        out_shape=[jax.ShapeDtypeStruct((n, 512), F32)] * 4
        + [jax.ShapeDtypeStruct((n // CHUNK, GDN_HEADS, CHUNK, CHUNK), F32),
           jax.ShapeDtypeStruct((n // CHUNK, 8, 128), F32)],
        compiler_params=_params("parallel"),
    )(qkv, qkv, qkv, gb)


def _gdn_fwd(qkv, gb, pc, nw, bl, seq):
    n = qkv.shape[0]
    nc = seq // CHUNK
    u, w, qd, kd, qk, gam = _gdn_prep(qkv, gb)
    tok3 = lambda t: t.reshape(bl, seq, 512)
    qk5 = qk.reshape(bl, nc, GDN_HEADS, CHUNK, CHUNK)
    gam4 = gam.reshape(bl, nc, 8, 128)

    def body(u_ref, w_ref, qd_ref, kd_ref, qk_ref, gam_ref, og_ref, nw_ref, o_ref, y_ref, st_ref, s_scr):
        @pl.when(pl.program_id(0) == 0)
        def _():
            s_scr[...] = jnp.zeros_like(s_scr)

        units = [(b, h, slice(h * 128, (h + 1) * 128)) for b in range(bl) for h in range(GDN_HEADS)]
        ss = [s_scr[b, h] for b, h, _ in units]
        for (b, h, _), s in zip(units, ss):
            st_ref[b, 0, h] = s
        v_new = [u_ref[b, :, cols] - _dot(w_ref[b, :, cols], s) for (b, h, cols), s in zip(units, ss)]
        o_inter = [_dot(qd_ref[b, :, cols], s) for (b, h, cols), s in zip(units, ss)]
        os_ = [oi + _dot(qk_ref[b, 0, h], vn) for (b, h, cols), oi, vn in zip(units, o_inter, v_new)]
        for (b, h, cols), s, vn in zip(units, ss, v_new):
            s_scr[b, h] = s * gam_ref[b, 0, h:h + 1, :] + _dot_tn(kd_ref[b, :, cols], vn)
        for (b, h, cols), o in zip(units, os_):
            o_ref[b, :, cols] = o
            y_ref[b, :, cols] = _rms_gate(o, nw_ref[...], og_ref[b, :, cols])

    tok = pl.BlockSpec((bl, CHUNK, 512), lambda i: (0, i, 0))
    st_spec = pl.BlockSpec((bl, 1, GDN_HEADS, 128, 128), lambda i: (0, i, 0, 0, 0))
    o, y, st = pl.pallas_call(
        body, name="gdn_scan_fwd", grid=(nc,),
        in_specs=[tok, tok, tok, tok,
                  pl.BlockSpec((bl, 1, GDN_HEADS, CHUNK, CHUNK), lambda i: (0, i, 0, 0, 0)),
                  pl.BlockSpec((bl, 1, 8, 128), lambda i: (0, i, 0, 0)), tok,
                  pl.BlockSpec(nw.shape, lambda i: (0, 0))],
        out_specs=[tok, tok, st_spec],
        out_shape=[jax.ShapeDtypeStruct((bl, seq, 512), F32), jax.ShapeDtypeStruct((bl, seq, 512), F32),
                   jax.ShapeDtypeStruct((bl, nc, GDN_HEADS, 128, 128), F32)],
        scratch_shapes=[pltpu.VMEM((bl, GDN_HEADS, 128, 128), F32)],
        compiler_params=_params("arbitrary"),
    )(tok3(u), tok3(w), tok3(qd), tok3(kd), qk5, gam4, tok3(pc), nw)
    return y.reshape(n, 512), (o, st, w, qd, kd, qk5, gam4)


def _gdn_bwd(qkv, gb, pc, res, dyb, nw, bl, seq):
    n = qkv.shape[0]
    nc = seq // CHUNK
    o, st, w, qd, kd, qk5, gam4 = res
    tok3 = lambda t: t.reshape(bl, seq, 512)

    def scan_body(dy_ref, o_ref, og_ref, w_ref, qd_ref, kd_ref, qk_ref, gam_ref, nw_ref,
                  do_ref, dog_ref, dvn_ref, dst_ref, dnw_ref, ds_scr):
        @pl.when(pl.program_id(0) == 0)
        def _():
            ds_scr[...] = jnp.zeros_like(ds_scr)
            dnw_ref[...] = jnp.zeros_like(dnw_ref)

        units = [(b, h, slice(h * 128, (h + 1) * 128)) for b in range(bl) for h in range(GDN_HEADS)]
        dnw = jnp.zeros(nw.shape, F32)
        d_os = []
        for b, h, cols in units:
            _, vjp = jax.vjp(_rms_gate, o_ref[b, :, cols], nw_ref[...], og_ref[b, :, cols])
            d_o, dnw_h, dog = vjp(dy_ref[b, :, cols])
            do_ref[b, :, cols] = d_o
            dog_ref[b, :, cols] = dog
            dnw = dnw + dnw_h
            d_os.append(d_o)
        dnw_ref[...] += dnw
        dss = [ds_scr[b, h] for b, h, _ in units]
        for (b, h, _), ds in zip(units, dss):
            dst_ref[b, 0, h] = ds
        dvn_a = [_dot(kd_ref[b, :, cols], ds) for (b, h, cols), ds in zip(units, dss)]
        dvns = [a + _dot_tn(qk_ref[b, 0, h], d_o) for (b, h, cols), a, d_o in zip(units, dvn_a, d_os)]
        ds_a = [_dot_tn(qd_ref[b, :, cols], d_o) + ds * gam_ref[b, 0, h:h + 1, :]
                for (b, h, cols), d_o, ds in zip(units, d_os, dss)]
        for (b, h, cols), a, dvn in zip(units, ds_a, dvns):
            dvn_ref[b, :, cols] = dvn
            ds_scr[b, h] = a - _dot_tn(w_ref[b, :, cols], dvn)

    rev = lambda i: nc - 1 - i
    tok = pl.BlockSpec((bl, CHUNK, 512), lambda i: (0, rev(i), 0))
    st_spec = pl.BlockSpec((bl, 1, GDN_HEADS, 128, 128), lambda i: (0, rev(i), 0, 0, 0))
    tok_shape = jax.ShapeDtypeStruct((bl, seq, 512), F32)
    d_o, dog, dvn, dst, dnw = pl.pallas_call(
        scan_body, name="gdn_scan_bwd", grid=(nc,),
        in_specs=[tok] * 6 + [pl.BlockSpec((bl, 1, GDN_HEADS, CHUNK, CHUNK), lambda i: (0, rev(i), 0, 0, 0)),
                              pl.BlockSpec((bl, 1, 8, 128), lambda i: (0, rev(i), 0, 0)),
                              pl.BlockSpec(nw.shape, lambda i: (0, 0))],
        out_specs=[tok, tok, tok, st_spec, pl.BlockSpec(nw.shape, lambda i: (0, 0))],
        out_shape=[tok_shape, tok_shape, tok_shape, jax.ShapeDtypeStruct(st.shape, F32),
                   jax.ShapeDtypeStruct(nw.shape, F32)],
        scratch_shapes=[pltpu.VMEM((bl, GDN_HEADS, 128, 128), F32)],
        compiler_params=_params("arbitrary"),
    )(tok3(dyb), o, tok3(pc), tok3(w), tok3(qd), tok3(kd), qk5, gam4, nw)

    r_per = GDN_PREP_CHUNKS
    tm = r_per * CHUNK

    def prep_body(q_ref, k_ref, v_ref, gb_ref, st_ref, dst_ref, dvn_ref, do_ref, dqkv_ref, dgb_ref):
        chunk_rows = [slice(r * CHUNK, (r + 1) * CHUNK) for r in range(r_per)]
        gather = lambda ref: [t for rows in chunk_rows for t in _head_cols(ref, rows)]
        (u, w_, _, _, _, _), vjp = jax.vjp(_gdn_prep_units, gather(q_ref), gather(k_ref), gather(v_ref),
                                           [gb_ref[rows, :] for rows in chunk_rows])
        units = [(r, h) for r in range(r_per) for h in range(GDN_HEADS)]
        ss = [st_ref[r, h] for r, h in units]
        dss = [dst_ref[r, h] for r, h in units]
        dvns, d_os = gather(dvn_ref), gather(do_ref)
        v_new = [ui - _dot(wi, s) for ui, wi, s in zip(u, w_, ss)]
        d_w = [-_dot_nt(dvn, s) for dvn, s in zip(dvns, ss)]
        d_qk = [_dot_nt(d_o, vn) for d_o, vn in zip(d_os, v_new)]
        d_qd = [_dot_nt(d_o, s) for d_o, s in zip(d_os, ss)]
        d_kd = [_dot_nt(vn, ds) for vn, ds in zip(v_new, dss)]
        d_gam = [_sum_all(ds * s) for ds, s in zip(dss, ss)]
        dq, dk, dv, dgb = vjp((dvns, d_w, d_qk, d_qd, d_kd, d_gam))
        for i, (r, h) in enumerate(units):
            rows = chunk_rows[r]
            for part, d in enumerate((dq, dk, dv)):
                dqkv_ref[rows, part * 512 + h * 128:part * 512 + (h + 1) * 128] = d[i]
        for r, rows in enumerate(chunk_rows):
            dgb_ref[rows, :] = dgb[r]

    tokp = lambda j: pl.BlockSpec((tm, 512), lambda i: (i, j))
    st4 = pl.BlockSpec((r_per, GDN_HEADS, 128, 128), lambda i: (i, 0, 0, 0))
    dqkv, dgb = pl.pallas_call(
        prep_body, name="gdn_prep_bwd", grid=(n // tm,),
        in_specs=[tokp(0), tokp(1), tokp(2), pl.BlockSpec((tm, 128), lambda i: (i, 0)), st4, st4, tokp(0), tokp(0)],
        out_specs=[pl.BlockSpec((tm, 1536), lambda i: (i, 0)), pl.BlockSpec((tm, 128), lambda i: (i, 0))],
        out_shape=[jax.ShapeDtypeStruct((n, 1536), F32), jax.ShapeDtypeStruct((n, 128), F32)],
        compiler_params=_params("parallel"),
    )(qkv, qkv, qkv, gb, st.reshape(bl * nc, GDN_HEADS, 128, 128), dst.reshape(bl * nc, GDN_HEADS, 128, 128),
      dvn.reshape(n, 512), d_o.reshape(n, 512))
    return dqkv, dog.reshape(n, 512), dgb, dnw


def _out_block(x2, tgt2, ya, yb, g1p3, wo, lnw, lnb, seq, tm=256):
    n = x2.shape[0]
    tpe = seq // tm
    bl = n // seq

    def body(x_ref, t_ref, ya_ref, yb_ref, g_ref, wo_ref, lnw_ref, lnb_ref,
             dz_ref, dya_ref, dyb_ref, dwo_ref, dg_ref, glw_ref, glb_ref, loss_ref):
        i = pl.program_id(0)

        @pl.when(i == 0)
        def _():
            dwo_ref[...] = jnp.zeros_like(dwo_ref)
            glw_ref[...] = jnp.zeros_like(glw_ref)
            glb_ref[...] = jnp.zeros_like(glb_ref)
            loss_ref[...] = jnp.zeros_like(loss_ref)

        @pl.when(i % tpe == 0)
        def _():
            dg_ref[...] = jnp.zeros_like(dg_ref)

        ya16 = ya_ref[...].astype(wo.dtype)
        yb16 = yb_ref[...].astype(wo.dtype)
        wa = wo_ref[0:GLA_WIDTH, :]
        wb = wo_ref[GLA_WIDTH:, :]
        y = _dot(ya16, wa) + _dot(yb16, wb)
        g1p = g_ref[0]
        z = ALPHA * x_ref[...] + g1p * y
        mu = jnp.mean(z, axis=-1, keepdims=True)
        zc = z - mu
        rstd = lax.rsqrt(jnp.mean(zc * zc, axis=-1, keepdims=True) + LN_EPS)
        xhat = zc * rstd
        diff = xhat * lnw_ref[...] + lnb_ref[...] - t_ref[...]
        loss_ref[...] += (0.5 / D_MODEL) * jnp.sum(jnp.sum(diff * diff, axis=-1, keepdims=True), axis=0, keepdims=True)
        dout = diff * (1.0 / D_MODEL)
        glw_ref[...] += jnp.sum(dout * xhat, axis=0, keepdims=True)
        glb_ref[...] += jnp.sum(dout, axis=0, keepdims=True)
        dxh = dout * lnw_ref[...]
        dz = rstd * (dxh - jnp.mean(dxh, axis=-1, keepdims=True)
                     - xhat * jnp.mean(dxh * xhat, axis=-1, keepdims=True))
        dz_ref[...] = dz
        dg_ref[0] += jnp.sum(dz * y, axis=0, keepdims=True)
        dy = (g1p * dz).astype(wo.dtype)
        dya_ref[...] = _dot_nt(dy, wa)
        dyb_ref[...] = _dot_nt(dy, wb)
        dwo_ref[0:GLA_WIDTH, :] += _dot_tn(ya16, dy)
        dwo_ref[GLA_WIDTH:, :] += _dot_tn(yb16, dy)

    row = lambda i: (i, 0)
    const = lambda i: (0, 0)
    per_ex = pl.BlockSpec((1, 1, D_MODEL), lambda i: (i // tpe, 0, 0))
    return pl.pallas_call(
        body, name="out_block", grid=(n // tm,),
        in_specs=[pl.BlockSpec((tm, D_MODEL), row), pl.BlockSpec((tm, D_MODEL), row),
                  pl.BlockSpec((tm, 512), row), pl.BlockSpec((tm, 512), row), per_ex,
                  pl.BlockSpec((D_MODEL, D_MODEL), const), pl.BlockSpec((1, D_MODEL), const),
                  pl.BlockSpec((1, D_MODEL), const)],
        out_specs=[pl.BlockSpec((tm, D_MODEL), row), pl.BlockSpec((tm, 512), row), pl.BlockSpec((tm, 512), row),
                   pl.BlockSpec((D_MODEL, D_MODEL), const), per_ex,
                   pl.BlockSpec((1, D_MODEL), const), pl.BlockSpec((1, D_MODEL), const),
                   pl.BlockSpec((1, 1), const)],
        out_shape=[jax.ShapeDtypeStruct((n, D_MODEL), F32), jax.ShapeDtypeStruct((n, 512), F32),
                   jax.ShapeDtypeStruct((n, 512), F32), jax.ShapeDtypeStruct((D_MODEL, D_MODEL), F32),
                   jax.ShapeDtypeStruct((bl, 1, D_MODEL), F32), jax.ShapeDtypeStruct((1, D_MODEL), F32),
                   jax.ShapeDtypeStruct((1, D_MODEL), F32), jax.ShapeDtypeStruct((1, 1), F32)],
        compiler_params=_params("arbitrary"),
    )(x2, tgt2, ya, yb, g1p3, wo, lnw, lnb)


def _proj_bwd_x(ds, ws, x2, dz, sc3, seq, tm=256):
    n = x2.shape[0]
    tpe = seq // tm
    bl = n // seq

    def body(da_ref, db_ref, dc_ref, dd1_ref, dd2_ref, wa_ref, wb_ref, wc_ref, wd_ref, x_ref, dz_ref, sc_ref,
             gx_ref, dsh_ref, dsc_ref):
        i = pl.program_id(0)

        @pl.when(i % tpe == 0)
        def _():
            dsh_ref[...] = jnp.zeros_like(dsh_ref)
            dsc_ref[...] = jnp.zeros_like(dsc_ref)

        cdt = ws[0].dtype
        dh = _dot_nt(da_ref[...].astype(cdt), wa_ref[...])
        dh += _dot_nt(db_ref[...].astype(cdt), wb_ref[...])
        dh += _dot_nt(dc_ref[...].astype(cdt), wc_ref[...])
        dh += _dot_nt((dd1_ref[...] + dd2_ref[...]).astype(cdt), wd_ref[...])
        gx_ref[...] = dh * sc_ref[0] + ALPHA * dz_ref[...]
        dsh_ref[0] += jnp.sum(dh, axis=0, keepdims=True)
        dsc_ref[0] += jnp.sum(dh * x_ref[...], axis=0, keepdims=True)

    row = lambda i: (i, 0)
    const = lambda i: (0, 0)
    per_ex = pl.BlockSpec((1, 1, D_MODEL), lambda i: (i // tpe, 0, 0))
    da, db, dc, (dd1, dd2) = ds
    return pl.pallas_call(
        body, name="proj_bwd_x", grid=(n // tm,),
        in_specs=[pl.BlockSpec((tm, d.shape[1]), row) for d in (da, db, dc, dd1, dd2)]
        + [pl.BlockSpec(w.shape, const) for w in ws]
        + [pl.BlockSpec((tm, D_MODEL), row), pl.BlockSpec((tm, D_MODEL), row), per_ex],
        out_specs=[pl.BlockSpec((tm, D_MODEL), row), per_ex, per_ex],
        out_shape=[jax.ShapeDtypeStruct((n, D_MODEL), F32), jax.ShapeDtypeStruct((bl, 1, D_MODEL), F32),
                   jax.ShapeDtypeStruct((bl, 1, D_MODEL), F32)],
        compiler_params=_params("arbitrary"),
    )(da, db, dc, dd1, dd2, *ws, x2, dz, sc3)


def _proj_bwd_w(x2, sc3, sh3, ds, seq, cdt, name, tm=256):
    n = x2.shape[0]
    tpe = seq // tm
    flat, groups = [], []
    for d in ds:
        parts = d if isinstance(d, tuple) else (d,)
        groups.append(len(parts))
        flat.extend(parts)
    nin = len(flat)

    def body(x_ref, sc_ref, sh_ref, *refs):
        i = pl.program_id(0)
        outs = refs[nin:]

        @pl.when(i == 0)
        def _():
            for o in outs:
                o[...] = jnp.zeros_like(o)

        h = (x_ref[...] * sc_ref[0] + sh_ref[0]).astype(cdt)
        pos = 0
        for o, cnt in zip(outs, groups):
            d = refs[pos][...]
            for extra in refs[pos + 1:pos + cnt]:
                d = d + extra[...]
            pos += cnt
            o[...] += _dot_tn(h, d.astype(cdt))

    row = lambda i: (i, 0)
    const = lambda i: (0, 0)
    per_ex = pl.BlockSpec((1, 1, D_MODEL), lambda i: (i // tpe, 0, 0))
    widths = [(d[0] if isinstance(d, tuple) else d).shape[1] for d in ds]
    return pl.pallas_call(
        body, name=name, grid=(n // tm,),
        in_specs=[pl.BlockSpec((tm, D_MODEL), row), per_ex, per_ex]
        + [pl.BlockSpec((tm, d.shape[1]), row) for d in flat],
        out_specs=[pl.BlockSpec((D_MODEL, w), const) for w in widths],
        out_shape=[jax.ShapeDtypeStruct((D_MODEL, w), F32) for w in widths],
        compiler_params=_params("arbitrary"),
    )(x2, sc3, sh3, *flat)


def _mod_block(c_all, w_ada_sh, b_blk):
    def body(c_ref, w_ref, b_ref, o_ref):
        o_ref[...] = _dot(c_ref[...], w_ref[...]) + b_ref[...]

    return pl.pallas_call(
        body, name="mod_block",
        out_shape=jax.ShapeDtypeStruct((c_all.shape[0], w_ada_sh.shape[1]), F32),
        compiler_params=pltpu.CompilerParams(vmem_limit_bytes=VMEM_LIMIT),
    )(c_all, w_ada_sh, b_blk)


def _ada_grads(c_all, dmod_all, dmod_blk):
    def body(c_ref, da_ref, db_ref, gw_ref, gb_ref):
        gw_ref[...] = _dot_tn(c_ref[...], db_ref[...])
        gb_ref[...] = jnp.sum(da_ref[...], axis=0, keepdims=True)

    return pl.pallas_call(
        body, name="ada_grads",
        out_shape=[jax.ShapeDtypeStruct((c_all.shape[1], dmod_blk.shape[1]), F32),
                   jax.ShapeDtypeStruct((1, dmod_all.shape[1]), F32)],
        compiler_params=pltpu.CompilerParams(vmem_limit_bytes=VMEM_LIMIT),
    )(c_all, dmod_all, dmod_blk)


def _sum_leading(parts, name):
    def body(p_ref, o_ref):
        acc = p_ref[0]
        for d in range(1, parts.shape[0]):
            acc = acc + p_ref[d]
        o_ref[...] = acc

    return pl.pallas_call(
        body, name=name, out_shape=jax.ShapeDtypeStruct(parts.shape[1:], F32),
        compiler_params=pltpu.CompilerParams(vmem_limit_bytes=VMEM_LIMIT),
    )(parts)


def _row_tile(rows):
    for t in (256, 128, 64, 32, 16, 8):
        if rows % t == 0 and rows > t:
            return t
    return rows


def _add_n(arrs, name):
    rows, cols = arrs[0].shape
    tr = _row_tile(rows)

    def body(*refs):
        acc = refs[0][...]
        for r in refs[1:-1]:
            acc = acc + r[...]
        refs[-1][...] = acc

    spec = pl.BlockSpec((tr, cols), lambda i: (i, 0))
    return pl.pallas_call(
        body, name=name, grid=(rows // tr,), in_specs=[spec] * len(arrs), out_specs=spec,
        out_shape=jax.ShapeDtypeStruct((rows, cols), F32), compiler_params=_params("parallel"),
    )(*arrs)


def _adamw(w, g, m, v, name):
    rows, cols = w.shape
    tr = _row_tile(rows)
    c1 = 1.0 / (1.0 - ADAM_B1 ** ADAM_STEP)
    c2 = 1.0 / (1.0 - ADAM_B2 ** ADAM_STEP)

    def body(w_ref, g_ref, m_ref, v_ref, d_ref, nm_ref, nv_ref):
        gg = g_ref[...]
        nm = ADAM_B1 * m_ref[...] + (1.0 - ADAM_B1) * gg
        nv = ADAM_B2 * v_ref[...] + (1.0 - ADAM_B2) * (gg * gg)
        nm_ref[...] = nm
        nv_ref[...] = nv
        d_ref[...] = -ADAM_LR * ((nm * c1) / (jnp.sqrt(nv * c2) + ADAM_EPS) + ADAM_WD * w_ref[...])

    spec = pl.BlockSpec((tr, cols), lambda i: (i, 0))
    shp = jax.ShapeDtypeStruct((rows, cols), F32)
    return pl.pallas_call(
        body, name=name, grid=(rows // tr,), in_specs=[spec] * 4, out_specs=[spec] * 3,
        out_shape=[shp, shp, shp], compiler_params=_params("parallel"),
    )(w, g, m, v)


def _coords():
    return lax.axis_index("x"), lax.axis_index("y"), lax.axis_index("c")


def _all_gather8(blk, name):
    m_per, n = blk.shape

    def body(x_ref, out_ref, send_sems, recv_sems, local_sem):
        x, y, c = _coords()
        me, sibling = (x, y, c), (x, y, 1 - c)
        chips = [(1 - x, y), (x, 1 - y), (1 - x, 1 - y)]

        def rows(px, py, pc):
            return out_ref.at[pl.ds((4 * px + 2 * py + pc) * m_per, m_per), :]

        def copy(k, block, to, src=None):
            return pltpu.make_async_remote_copy(
                src_ref=rows(*block) if src is None else src, dst_ref=rows(*block),
                send_sem=send_sems.at[k], recv_sem=recv_sems.at[k], device_id=to, device_id_type=MESH)

        mine = pltpu.make_async_copy(x_ref, rows(*me), local_sem)
        mine.start()
        first = [copy(0, me, sibling, src=x_ref)]
        first += [copy(1 + j, me, (*chip, c), src=x_ref) for j, chip in enumerate(chips)]
        for cp in first:
            cp.start()
        passed = [copy(4 + j, (*chip, c), sibling) for j, chip in enumerate(chips)]
        for j, chip in enumerate(chips):
            copy(1 + j, (*chip, c), me).wait_recv()
            passed[j].start()
        copy(0, sibling, me).wait_recv()
        for j, chip in enumerate(chips):
            copy(4 + j, (*chip, 1 - c), me).wait_recv()
        for cp in first + passed:
            cp.wait_send()
        mine.wait()

    return pl.pallas_call(
        body, name=name,
        out_shape=jax.ShapeDtypeStruct((8 * m_per, n), blk.dtype),
        in_specs=[pl.BlockSpec(memory_space=pltpu.VMEM)],
        out_specs=pl.BlockSpec(memory_space=pltpu.VMEM),
        scratch_shapes=[pltpu.SemaphoreType.DMA((7,)), pltpu.SemaphoreType.DMA((7,)), pltpu.SemaphoreType.DMA],
        compiler_params=pltpu.CompilerParams(vmem_limit_bytes=VMEM_LIMIT),
    )(blk)


def _chip_gather(shards, name):
    k_arr = len(shards)

    def body(*refs):
        srcs, dsts = refs[:k_arr], refs[k_arr:2 * k_arr]
        send_sems, recv_sems, local_sems = refs[2 * k_arr:]
        x, y, c = _coords()
        peers = [(1 - x, y, c), (x, 1 - y, c), (1 - x, 1 - y, c)]
        me_chip = 2 * x + y
        local = [pltpu.make_async_copy(srcs[a], dsts[a].at[me_chip], local_sems.at[a]) for a in range(k_arr)]
        for cp in local:
            cp.start()
        sends = []
        for a in range(k_arr):
            for j, peer in enumerate(peers):
                sends.append(pltpu.make_async_remote_copy(
                    src_ref=srcs[a], dst_ref=dsts[a].at[me_chip],
                    send_sem=send_sems.at[a, j], recv_sem=recv_sems.at[a, j], device_id=peer, device_id_type=MESH))
        for cp in sends:
            cp.start()
        for a in range(k_arr):
            for j, peer in enumerate(peers):
                pltpu.make_async_remote_copy(
                    src_ref=srcs[a], dst_ref=dsts[a].at[2 * peer[0] + peer[1]],
                    send_sem=send_sems.at[a, j], recv_sem=recv_sems.at[a, j],
                    device_id=peer, device_id_type=MESH).wait_recv()
        for cp in sends:
            cp.wait_send()
        for cp in local:
            cp.wait()

    any_spec = pl.BlockSpec(memory_space=pl.ANY)
    return pl.pallas_call(
        body, name=name,
        out_shape=[jax.ShapeDtypeStruct((4,) + s.shape, s.dtype) for s in shards],
        in_specs=[any_spec] * k_arr, out_specs=[any_spec] * k_arr,
        scratch_shapes=[pltpu.SemaphoreType.DMA((k_arr, 3)), pltpu.SemaphoreType.DMA((k_arr, 3)),
                        pltpu.SemaphoreType.DMA((k_arr,))],
    )(*shards)


def _chip_scatter(pieces, name):
    k_arr = len(pieces)

    def body(*refs):
        srcs, dsts = refs[:k_arr], refs[k_arr:2 * k_arr]
        send_sems, recv_sems = refs[2 * k_arr:]
        x, y, c = _coords()
        peers = [(1 - x, y, c), (x, 1 - y, c), (1 - x, 1 - y, c)]
        copies = []
        for a in range(k_arr):
            for j, peer in enumerate(peers):
                copies.append(pltpu.make_async_remote_copy(
                    src_ref=srcs[a].at[2 * peer[0] + peer[1]], dst_ref=dsts[a].at[j],
                    send_sem=send_sems.at[a, j], recv_sem=recv_sems.at[a, j], device_id=peer, device_id_type=MESH))
        for cp in copies:
            cp.start()
        for cp in copies:
            cp.wait_recv()
        for cp in copies:
            cp.wait_send()

    any_spec = pl.BlockSpec(memory_space=pl.ANY)
    return pl.pallas_call(
        body, name=name,
        out_shape=[jax.ShapeDtypeStruct((3,) + p.shape[1:], p.dtype) for p in pieces],
        in_specs=[any_spec] * k_arr, out_specs=[any_spec] * k_arr,
        scratch_shapes=[pltpu.SemaphoreType.DMA((k_arr, 3)), pltpu.SemaphoreType.DMA((k_arr, 3))],
    )(*pieces)


def _sibling_swap(arrs, name):
    k_arr = len(arrs)

    def body(*refs):
        srcs, dsts = refs[:k_arr], refs[k_arr:2 * k_arr]
        send_sems, recv_sems = refs[2 * k_arr:]
        x, y, c = _coords()
        copies = [pltpu.make_async_remote_copy(
            src_ref=srcs[a], dst_ref=dsts[a], send_sem=send_sems.at[a], recv_sem=recv_sems.at[a],
            device_id=(x, y, 1 - c), device_id_type=MESH) for a in range(k_arr)]
        for cp in copies:
            cp.start()
        for cp in copies:
            cp.wait_recv()
        for cp in copies:
            cp.wait_send()

    any_spec = pl.BlockSpec(memory_space=pl.ANY)
    return pl.pallas_call(
        body, name=name,
        out_shape=[jax.ShapeDtypeStruct(a.shape, a.dtype) for a in arrs],
        in_specs=[any_spec] * k_arr, out_specs=[any_spec] * k_arr,
        scratch_shapes=[pltpu.SemaphoreType.DMA((k_arr,)), pltpu.SemaphoreType.DMA((k_arr,))],
    )(*arrs)


def _split_w_in(w_in):
    wa = jnp.concatenate([w_in[:, 0:1024], w_in[:, 1040:1552]], axis=1)
    wb = w_in[:, 1552:3088]
    wc = w_in[:, 3096:3608]
    wd = jnp.concatenate([w_in[:, 1024:1040], w_in[:, 3088:3096],
                          jnp.zeros((w_in.shape[0], 128 - SMALL_USED), w_in.dtype)], axis=1)
    return wa, wb, wc, wd


def _merge_dw_in(dwa, dwb, dwc, dwd):
    return jnp.concatenate([dwa[:, 0:1024], dwd[:, 0:GLA_RANK], dwa[:, 1024:1536], dwb,
                            dwd[:, GLA_RANK:SMALL_USED], dwc], axis=1)


def _local_step(x, mod, w_in16, w_out16, gla_wg, gla_bg, gla_nw, conv_w, a_log, dt_bias, gdn_nw, ln_w, ln_b, tgt):
    bl, seq, _ = x.shape
    n = bl * seq
    x2 = x.reshape(n, D_MODEL)
    tgt2 = tgt.reshape(n, D_MODEL)
    sh3 = mod[:, None, 0:D_MODEL]
    sc3 = 1.0 + mod[:, None, D_MODEL:2 * D_MODEL]
    g1p3 = 1.0 + mod[:, None, 2 * D_MODEL:]
    ws = _split_w_in(w_in16)
    wg = jnp.concatenate([gla_wg, jnp.zeros((128 - GLA_RANK, GLA_QK), F32)], axis=0)
    cw8 = jnp.concatenate([conv_w, jnp.zeros((8 - CONV_K, conv_w.shape[1]), F32)], axis=0)
    alog_v = jnp.zeros((1, 128), F32).at[:, LANE_A:LANE_A + GDN_HEADS].set(a_log)
    dtb_v = jnp.zeros((1, 128), F32).at[:, LANE_A:LANE_A + GDN_HEADS].set(dt_bias)

    pa, pb, pc, pd = _proj_fwd(x2, sc3, sh3, ws, seq)
    ya, st_a = _gla_fwd(pa, pd, wg, gla_bg, gla_nw, bl, seq)
    qkv, gb = _gdn_pre_fwd(pb, pd, cw8, alog_v, dtb_v, bl, seq)
    yb, st_b = _gdn_fwd(qkv, gb, pc, gdn_nw, bl, seq)
    dz, dya, dyb, d_wo, d_gate, d_lnw, d_lnb, loss = _out_block(x2, tgt2, ya, yb, g1p3, w_out16, ln_w, ln_b, seq)
    da, dd1, d_wg, d_bg, d_nwa = _gla_bwd(pa, pd, st_a, dya, wg, gla_bg, gla_nw, bl, seq)
    dqkv, dc, dgb, d_nwb = _gdn_bwd(qkv, gb, pc, st_b, dyb, gdn_nw, bl, seq)
    db, dd2, d_cw8, d_alog, d_dtb = _gdn_pre_bwd(pb, pd, dqkv, dgb, cw8, alog_v, dtb_v, bl, seq)
    gx, d_sh, d_sc = _proj_bwd_x((da, db, dc, (dd1, dd2)), ws, x2, dz, sc3, seq)
    (dwa,) = _proj_bwd_w(x2, sc3, sh3, [da], seq, w_in16.dtype, "proj_bwd_w_a")
    dwb, dwc, dwd = _proj_bwd_w(x2, sc3, sh3, [db, dc, (dd1, dd2)], seq, w_in16.dtype, "proj_bwd_w_bcd")
    grads = dict(
        w_in=_merge_dw_in(dwa, dwb, dwc, dwd),
        w_out=d_wo,
        gla_w_gate_up=d_wg[0:GLA_RANK, :],
        gla_b_gate=d_bg,
        gla_norm_w=d_nwa,
        gdn_conv_w=d_cw8[0:CONV_K, :],
        gdn_a_log=d_alog[:, LANE_A:LANE_A + GDN_HEADS],
        gdn_dt_bias=d_dtb[:, LANE_A:LANE_A + GDN_HEADS],
        gdn_norm_w=d_nwb,
        ln_w=d_lnw,
        ln_b=d_lnb,
        mod=jnp.concatenate([d_sh[:, 0, :], d_sc[:, 0, :], d_gate[:, 0, :]], axis=1),
    )
    return loss, gx.reshape(bl, seq, D_MODEL), grads


_SMALL = (("gla_b_gate", 256), ("gla_norm_w", 128), ("gdn_a_log", 4), ("gdn_dt_bias", 4), ("gdn_norm_w", 128),
          ("ln_w", 1024), ("ln_b", 1024), ("gla_w_gate_up", 16 * 256), ("gdn_conv_w", 4 * 1536), ("mod", 2 * 3072))


def _pack_small(grads):
    flat = jnp.concatenate([grads[k].reshape(-1) for k, _ in _SMALL])
    total = sum(sz for _, sz in _SMALL)
    rows = -(-total // 1024) * 8
    return jnp.concatenate([flat, jnp.zeros((rows * 128 - total,), F32)]).reshape(rows, 128)


def _unpack_small(flat):
    out, pos = {}, 0
    for k, sz in _SMALL:
        out[k] = flat[pos:pos + sz]
        pos += sz
    return out


def kernel(x, c, w_ada, b_ada, w_in, gla_w_gate_up, gla_b_gate, gla_norm_w, gdn_conv_w, gdn_a_log, gdn_dt_bias, gdn_norm_w, w_out, ln_w, ln_b, loss_target, m_w_ada, m_b_ada, m_w_in, m_gla_w_gate_up, m_gla_b_gate, m_gla_norm_w, m_gdn_conv_w, m_gdn_a_log, m_gdn_dt_bias, m_gdn_norm_w, m_w_out, m_ln_w, m_ln_b, v_w_ada, v_b_ada, v_w_in, v_gla_w_gate_up, v_gla_b_gate, v_gla_norm_w, v_gdn_conv_w, v_gdn_a_log, v_gdn_dt_bias, v_gdn_norm_w, v_w_out, v_ln_w, v_ln_b):
    ix, iy, ic = _coords()
    chip = 2 * ix + iy
    dev = 4 * ix + 2 * iy + ic
    bl = x.shape[0]
    ndev = 8

    c_all = _all_gather8(c.reshape(8, -1), "gather_c").reshape(ndev * bl, D_MODEL)
    ada_cols = w_ada.shape[2]
    b_blk = lax.dynamic_slice_in_dim(b_ada, chip * ada_cols, ada_cols, axis=1)
    mod_blk = _mod_block(c_all, w_ada[0], b_blk)
    mod_g = _all_gather8(mod_blk, "gather_mod").reshape(ndev, ndev * bl, ada_cols)
    mod_all = jnp.concatenate([mod_g[2 * j] for j in range(4)], axis=1)
    mod = lax.dynamic_slice_in_dim(mod_all, dev * bl, bl, axis=0)

    w_in_g, w_out_g, wg_g, cw_g = _chip_gather(
        [w_in[0].astype(BF16), w_out[0].astype(BF16), gla_w_gate_up[0], gdn_conv_w[0]], "gather_weights")
    w_in16 = jnp.concatenate([w_in_g[j] for j in range(4)], axis=1)
    w_out16 = w_out_g.reshape(D_MODEL, D_MODEL)
    gla_wg = jnp.concatenate([wg_g[j] for j in range(4)], axis=1)
    conv_w = jnp.concatenate([cw_g[j] for j in range(4)], axis=1)

    loss, grad_x, gr = _local_step(x, mod, w_in16, w_out16, gla_wg, gla_b_gate, gla_norm_w, conv_w,
                                   gdn_a_log, gdn_dt_bias, gdn_norm_w, ln_w, ln_b, loss_target)
    loss = lax.psum(loss[0, 0], ("x", "y", "c"))

    packed = _pack_small(gr)
    prow = packed.shape[0]
    gathered = _all_gather8(packed, "gather_small").reshape(ndev, prow, 128)
    small = _unpack_small(_sum_leading(gathered, "sum_small").reshape(-1))
    mod_rows = gathered.reshape(ndev, prow * 128)[:, sum(sz for _, sz in _SMALL[:-1]):][:, :bl * 3 * D_MODEL]
    dmod_all = mod_rows.reshape(ndev * bl, 3 * D_MODEL)
    dmod_blk = lax.dynamic_slice_in_dim(dmod_all, chip * ada_cols, ada_cols, axis=1)
    g_w_ada, g_b_ada = _ada_grads(c_all, dmod_all, dmod_blk)
    wg_cols = gla_w_gate_up.shape[2]
    g_wg = lax.dynamic_slice_in_dim(small["gla_w_gate_up"].reshape(GLA_RANK, GLA_QK), chip * wg_cols, wg_cols, axis=1)
    cw_cols = gdn_conv_w.shape[2]
    g_cw = lax.dynamic_slice_in_dim(small["gdn_conv_w"].reshape(CONV_K, 3 * GDN_WIDTH), chip * cw_cols, cw_cols, axis=1)

    in_cols = w_in.shape[2]
    out_rows = w_out.shape[1]
    p_in = jnp.stack([gr["w_in"][:, j * in_cols:(j + 1) * in_cols] for j in range(4)])
    p_out = gr["w_out"].reshape(4, out_rows, D_MODEL)
    h_in, h_out = D_MODEL // 2, out_rows // 2
    mine_in = lax.dynamic_slice_in_dim(p_in, ic * h_in, h_in, axis=1)
    mine_out = lax.dynamic_slice_in_dim(p_out, ic * h_out, h_out, axis=1)
    theirs_in = lax.dynamic_slice_in_dim(p_in, (1 - ic) * h_in, h_in, axis=1)
    theirs_out = lax.dynamic_slice_in_dim(p_out, (1 - ic) * h_out, h_out, axis=1)
    got_in, got_out = _sibling_swap([theirs_in, theirs_out], "swap_halves")
    chip_in = _add_n([mine_in.reshape(4 * h_in, in_cols), got_in.reshape(4 * h_in, in_cols)], "chip_sum_in")
    chip_out = _add_n([mine_out.reshape(4 * h_out, D_MODEL), got_out.reshape(4 * h_out, D_MODEL)], "chip_sum_out")
    chip_in = chip_in.reshape(4, h_in, in_cols)
    chip_out = chip_out.reshape(4, h_out, D_MODEL)
    rs_in, rs_out = _chip_scatter([chip_in, chip_out], "scatter_grads")
    own_in = lax.dynamic_index_in_dim(chip_in, chip, axis=0, keepdims=False)
    own_out = lax.dynamic_index_in_dim(chip_out, chip, axis=0, keepdims=False)
    half_in = _add_n([own_in, rs_in[0], rs_in[1], rs_in[2]], "reduce_in")
    half_out = _add_n([own_out, rs_out[0], rs_out[1], rs_out[2]], "reduce_out")
    sib_in, sib_out = _sibling_swap([half_in, half_out], "swap_result")
    g_w_in = jnp.where(ic == 0, jnp.concatenate([half_in, sib_in], axis=0), jnp.concatenate([sib_in, half_in], axis=0))
    g_w_out = jnp.where(ic == 0, jnp.concatenate([half_out, sib_out], axis=0),
                        jnp.concatenate([sib_out, half_out], axis=0))

    grads = dict(
        w_ada=g_w_ada[None], b_ada=g_b_ada, w_in=g_w_in[None], gla_w_gate_up=g_wg[None],
        gla_b_gate=small["gla_b_gate"].reshape(1, -1), gla_norm_w=small["gla_norm_w"].reshape(1, -1),
        gdn_conv_w=g_cw[None], gdn_a_log=small["gdn_a_log"].reshape(1, -1),
        gdn_dt_bias=small["gdn_dt_bias"].reshape(1, -1), gdn_norm_w=small["gdn_norm_w"].reshape(1, -1),
        w_out=g_w_out[None], ln_w=small["ln_w"].reshape(1, -1), ln_b=small["ln_b"].reshape(1, -1))
    weights = dict(w_ada=w_ada, b_ada=b_ada, w_in=w_in, gla_w_gate_up=gla_w_gate_up, gla_b_gate=gla_b_gate,
                   gla_norm_w=gla_norm_w, gdn_conv_w=gdn_conv_w, gdn_a_log=gdn_a_log, gdn_dt_bias=gdn_dt_bias,
                   gdn_norm_w=gdn_norm_w, w_out=w_out, ln_w=ln_w, ln_b=ln_b)
    m_in = dict(w_ada=m_w_ada, b_ada=m_b_ada, w_in=m_w_in, gla_w_gate_up=m_gla_w_gate_up, gla_b_gate=m_gla_b_gate,
                gla_norm_w=m_gla_norm_w, gdn_conv_w=m_gdn_conv_w, gdn_a_log=m_gdn_a_log, gdn_dt_bias=m_gdn_dt_bias,
                gdn_norm_w=m_gdn_norm_w, w_out=m_w_out, ln_w=m_ln_w, ln_b=m_ln_b)
    v_in = dict(w_ada=v_w_ada, b_ada=v_b_ada, w_in=v_w_in, gla_w_gate_up=v_gla_w_gate_up, gla_b_gate=v_gla_b_gate,
                gla_norm_w=v_gla_norm_w, gdn_conv_w=v_gdn_conv_w, gdn_a_log=v_gdn_a_log, gdn_dt_bias=v_gdn_dt_bias,
                gdn_norm_w=v_gdn_norm_w, w_out=v_w_out, ln_w=v_ln_w, ln_b=v_ln_b)
    names = list(weights)
    delta, new_m, new_v = {}, {}, {}
    for nm in names:
        shp = weights[nm].shape
        two_d = (-1, shp[-1])
        d, a, b = _adamw(weights[nm].reshape(two_d), grads[nm].reshape(two_d), m_in[nm].reshape(two_d),
                         v_in[nm].reshape(two_d), "adamw_" + nm)
        delta[nm], new_m[nm], new_v[nm] = d.reshape(shp), a.reshape(shp), b.reshape(shp)
        grads[nm] = grads[nm].reshape(shp)
    return (loss, grad_x, *[grads[k] for k in names], *[delta[k] for k in names],
            *[new_m[k] for k in names], *[new_v[k] for k in names])
```

```python
import functools

import jax
import jax.numpy as jnp
from jax import lax
from jax.experimental import pallas as pl
from jax.experimental.pallas import tpu as pltpu

F32 = jnp.float32
BF16 = jnp.bfloat16
HI = lax.Precision.HIGHEST
INV_PREC = lax.Precision.HIGH
MESH = pl.DeviceIdType.MESH

D_MODEL = 1024
GLA_HEADS = 4
GLA_DK = 64
GLA_DV = 128
GLA_QK = 256
GLA_WIDTH = 512
GLA_RANK = 16
GLA_GATE_NORM = 16.0
GDN_HEADS = 4
GDN_DK = 128
GDN_WIDTH = 512
CONV_K = 4
CHUNK = 64
LN_EPS = 1e-5
RMS_EPS = 1e-6
ALPHA = 2.0 ** 0.25
IN_COLS = 3608

LANE_A = GLA_RANK
LANE_B = GLA_RANK + GDN_HEADS
SMALL_USED = GLA_RANK + 2 * GDN_HEADS

ADAM_LR = 0.001
ADAM_B1 = 0.9
ADAM_B2 = 0.999
ADAM_EPS = 1e-08
ADAM_WD = 0.01
ADAM_STEP = 10

VMEM_LIMIT = 56 * 1024 * 1024


def _iota(shape, dim):
    return lax.broadcasted_iota(jnp.int32, shape, dim)


def _dot(a, b, prec=None):
    return lax.dot_general(a, b, (((1,), (0,)), ((), ())), precision=prec, preferred_element_type=F32)


def _dot_nt(a, b, prec=None):
    return lax.dot_general(a, b, (((1,), (1,)), ((), ())), precision=prec, preferred_element_type=F32)


def _dot_tn(a, b, prec=None):
    return lax.dot_general(a, b, (((0,), (0,)), ((), ())), precision=prec, preferred_element_type=F32)


def _log_sigmoid(z):
    return jnp.minimum(z, 0.0) - jnp.log1p(jnp.exp(-jnp.abs(z)))


def _softplus(z):
    return jnp.maximum(z, 0.0) + jnp.log1p(jnp.exp(-jnp.abs(z)))


def _silu(z):
    return z * jax.nn.sigmoid(z)


def _rms_gate(o, nw, og):
    return o * lax.rsqrt(jnp.mean(o * o, axis=-1, keepdims=True) + RMS_EPS) * nw * _silu(og)


def _params(*sem):
    return pltpu.CompilerParams(dimension_semantics=sem, vmem_limit_bytes=VMEM_LIMIT)


def _gla_chunk(qs, ks, lrs, vs, ogs, ss, wg, bg, nw):
    c = qs[0].shape[0]
    units = [divmod(i, GLA_HEADS) for i in range(len(vs))]
    row, col = _iota((c, c), 0), _iota((c, c), 1)
    causal = row >= col
    first_half = (_iota((c, 1), 0) < c // 2).astype(F32)
    lane = _iota((1, GLA_QK), 1)
    masks = [((lane >= h * GLA_DK) & (lane < (h + 1) * GLA_DK)).astype(F32) for h in range(GLA_HEADS)]
    gs = [_log_sigmoid(_dot(lr, wg) + bg) * (1.0 / GLA_GATE_NORM) for lr in lrs]
    bs = [_dot(causal.astype(F32), g, HI) for g in gs]
    b_ref = [jnp.sum(g * first_half, axis=0, keepdims=True) for g in gs]
    b_last = [jnp.sum(g, axis=0, keepdims=True) for g in gs]
    qsc = [q * (GLA_DK ** -0.5) for q in qs]
    qe = [q * jnp.exp(b - br) for q, b, br in zip(qsc, bs, b_ref)]
    ke = [k * jnp.exp(br - b) for k, b, br in zip(ks, bs, b_ref)]
    qb = [q * jnp.exp(b) for q, b in zip(qsc, bs)]
    kd = [k * jnp.exp(bl_ - b) for k, b, bl_ in zip(ks, bs, b_last)]
    decay = [jnp.exp(bl_) for bl_ in b_last]
    att = [jnp.where(causal, _dot_nt(qe[e] * masks[h], ke[e]), 0.0) for e, h in units]
    o_inter = [_dot_nt(qb[e], s) for (e, h), s in zip(units, ss)]
    os_ = [_dot(a, v) + oi for a, v, oi in zip(att, vs, o_inter)]
    upd = [_dot_tn(v, kd[e]) for (e, h), v in zip(units, vs)]
    s_new = [s * decay[e] + masks[h] * up for (e, h), s, up in zip(units, ss, upd)]
    ys = [_rms_gate(o, nw, og) for o, og in zip(os_, ogs)]
    return ys, s_new


def _unit_lower_inverse_chain(a_list):
    c = a_list[0].shape[0]
    eye = (_iota((c, c), 0) == _iota((c, c), 1)).astype(F32)
    ps = [-a for a in a_list]
    ts = [eye + p for p in ps]
    for _ in range(max(c.bit_length() - 2, 0)):
        ps = [_dot(p, p, INV_PREC) for p in ps]
        ts = [t + _dot(t, p, INV_PREC) for t, p in zip(ts, ps)]
    return ts


@jax.custom_vjp
def _unit_lower_inverse(a_list):
    return _unit_lower_inverse_chain(a_list)


def _unit_lower_inverse_fwd(a_list):
    ts = _unit_lower_inverse_chain(a_list)
    return ts, ts


def _unit_lower_inverse_bwd(ts, dts):
    xs = [_dot_nt(dt, t, INV_PREC) for dt, t in zip(dts, ts)]
    return ([-_dot_tn(t, x, INV_PREC) for t, x in zip(ts, xs)],)


_unit_lower_inverse.defvjp(_unit_lower_inverse_fwd, _unit_lower_inverse_bwd)


@jax.custom_vjp
def _unit_lower_inverse_known(a_list, ts):
    return ts


def _unit_lower_inverse_known_fwd(a_list, ts):
    return ts, ts


def _unit_lower_inverse_known_bwd(ts, dts):
    return _unit_lower_inverse_bwd(ts, dts) + ([jnp.zeros_like(t) for t in ts],)


_unit_lower_inverse_known.defvjp(_unit_lower_inverse_known_fwd, _unit_lower_inverse_known_bwd)


def _gdn_prep_units(qs, ks, vs, gbs, t_known=None):
    c = qs[0].shape[0]
    units = [divmod(i, GDN_HEADS) for i in range(len(qs))]
    row, col = _iota((c, c), 0), _iota((c, c), 1)
    causal, strict = row >= col, row > col
    lane = _iota((1, 128), 1)
    d_alls = [_dot(causal.astype(F32), gb, HI) for gb in gbs]
    g_c, beta_c, d_c = [], [], []
    for r, h in units:
        sel_a = (lane == LANE_A + h).astype(F32)
        g_c.append(jnp.sum(gbs[r] * sel_a, axis=-1, keepdims=True))
        beta_c.append(jnp.sum(gbs[r] * (lane == LANE_B + h).astype(F32), axis=-1, keepdims=True))
        d_c.append(jnp.sum(d_alls[r] * sel_a, axis=-1, keepdims=True))
    d_last = [jnp.sum(g, axis=0, keepdims=True) for g in g_c]
    d_diff = [jnp.broadcast_to(d, (c, c)) - jnp.broadcast_to(d, (c, c)).T for d in d_c]
    decay_mat = [jnp.where(causal, jnp.exp(jnp.where(causal, dd, 0.0)), 0.0) for dd in d_diff]
    kb = [k * b for k, b in zip(ks, beta_c)]
    a = [jnp.where(strict, _dot_nt(kbi, k) * dm, 0.0) for kbi, k, dm in zip(kb, ks, decay_mat)]
    t = _unit_lower_inverse(a) if t_known is None else _unit_lower_inverse_known(a, t_known)
    u = [_dot(ti, v * b) for ti, v, b in zip(t, vs, beta_c)]
    w = [_dot(ti, kbi * jnp.exp(d)) for ti, kbi, d in zip(t, kb, d_c)]
    qk = [jnp.where(causal, _dot_nt(q, k) * dm, 0.0) for q, k, dm in zip(qs, ks, decay_mat)]
    q_dec = [q * jnp.exp(d) for q, d in zip(qs, d_c)]
    k_dec = [k * jnp.exp(dl - d) for k, dl, d in zip(ks, d_last, d_c)]
    gamma = [jnp.exp(dl) for dl in d_last]
    return u, w, qk, q_dec, k_dec, gamma, t


def _sum_all(t):
    return jnp.sum(jnp.sum(t, axis=-1, keepdims=True), axis=0, keepdims=True)


def _gdn_pre_elem(ps, ab, alog_v, dtb_v):
    outs = []
    for j, p in enumerate(ps):
        s = _silu(p)
        if j < 2 * GDN_HEADS:
            s = s * lax.rsqrt(jnp.sum(s * s, axis=-1, keepdims=True) + RMS_EPS)
        if j < GDN_HEADS:
            s = s * (GDN_DK ** -0.5)
        outs.append(s)
    lane = _iota((1, 128), 1)
    is_a = (lane >= LANE_A) & (lane < LANE_A + GDN_HEADS)
    is_b = (lane >= LANE_B) & (lane < LANE_B + GDN_HEADS)
    g = -jnp.exp(alog_v) * _softplus(ab + dtb_v)
    gb = jnp.where(is_a, g, jnp.where(is_b, jax.nn.sigmoid(ab), 0.0))
    return tuple(outs) + (gb,)


def _proj_fwd(x2, sc3, sh3, ws, seq, tm=256):
    n = x2.shape[0]
    tpe = seq // tm
    nw = len(ws)

    def body(x_ref, sc_ref, sh_ref, *refs):
        h = (x_ref[...] * sc_ref[0] + sh_ref[0]).astype(ws[0].dtype)
        for w_ref, o_ref in zip(refs[:nw], refs[nw:]):
            o_ref[...] = _dot(h, w_ref[...])

    row = lambda i: (i, 0)
    per_ex = pl.BlockSpec((1, 1, D_MODEL), lambda i: (i // tpe, 0, 0))
    return pl.pallas_call(
        body, name="proj_fwd", grid=(n // tm,),
        in_specs=[pl.BlockSpec((tm, D_MODEL), row), per_ex, per_ex]
        + [pl.BlockSpec(w.shape, lambda i: (0, 0)) for w in ws],
        out_specs=[pl.BlockSpec((tm, w.shape[1]), row) for w in ws],
        out_shape=[jax.ShapeDtypeStruct((n, w.shape[1]), F32) for w in ws],
        compiler_params=_params("parallel"),
    )(x2, sc3, sh3, *ws)


def _gla_fwd(pa, pd, wg, bg, nw, bl, seq):
    n = pa.shape[0]
    nc = seq // CHUNK
    heads = [(e, h, slice(h * 128, (h + 1) * 128)) for e in range(bl) for h in range(GLA_HEADS)]

    def body(q_ref, k_ref, v_ref, og_ref, lr_ref, wg_ref, bg_ref, nw_ref, y_ref, st_ref, s_scr):
        @pl.when(pl.program_id(0) == 0)
        def _():
            s_scr[...] = jnp.zeros_like(s_scr)

        ss = [s_scr[e, h] for e, h, _ in heads]
        for (e, h, _), s in zip(heads, ss):
            st_ref[e, 0, h] = s
        ys, s_new = _gla_chunk([q_ref[e] for e in range(bl)], [k_ref[e] for e in range(bl)],
                               [lr_ref[e] for e in range(bl)],
                               [v_ref[e, :, cols] for e, _, cols in heads], [og_ref[e, :, cols] for e, _, cols in heads],
                               ss, wg_ref[...], bg_ref[...], nw_ref[...])
        for (e, h, cols), y, s in zip(heads, ys, s_new):
            y_ref[e, :, cols] = y
            s_scr[e, h] = s

    tok = lambda w, j: pl.BlockSpec((bl, CHUNK, w), lambda i: (0, i, j))
    const = lambda i: (0, 0)
    pa3 = pa.reshape(bl, seq, 1536)
    y, st = pl.pallas_call(
        body, name="gla_fwd", grid=(nc,),
        in_specs=[tok(256, 0), tok(256, 1), tok(512, 1), tok(512, 2), tok(128, 0),
                  pl.BlockSpec(wg.shape, const), pl.BlockSpec(bg.shape, const), pl.BlockSpec(nw.shape, const)],
        out_specs=[tok(512, 0), pl.BlockSpec((bl, 1, GLA_HEADS, 128, 256), lambda i: (0, i, 0, 0, 0))],
        out_shape=[jax.ShapeDtypeStruct((bl, seq, 512), F32),
                   jax.ShapeDtypeStruct((bl, nc, GLA_HEADS, 128, 256), F32)],
        scratch_shapes=[pltpu.VMEM((bl, GLA_HEADS, 128, 256), F32)],
        compiler_params=_params("arbitrary"),
    )(pa3, pa3, pa3, pa3, pd.reshape(bl, seq, 128), wg, bg, nw)
    return y.reshape(n, 512), st


def _gla_bwd(pa, pd, st, dya, wg, bg, nw, bl, seq):
    n = pa.shape[0]
    nc = seq // CHUNK
    heads = [(e, h, slice(h * 128, (h + 1) * 128)) for e in range(bl) for h in range(GLA_HEADS)]

    def body(q_ref, k_ref, v_ref, og_ref, lr_ref, st_ref, dy_ref, wg_ref, bg_ref, nw_ref,
             da_ref, dd_ref, dwg_ref, dbg_ref, dnw_ref, ds_scr):
        @pl.when(pl.program_id(0) == 0)
        def _():
            dwg_ref[...] = jnp.zeros_like(dwg_ref)
            dbg_ref[...] = jnp.zeros_like(dbg_ref)
            dnw_ref[...] = jnp.zeros_like(dnw_ref)
            ds_scr[...] = jnp.zeros_like(ds_scr)

        _, vjp = jax.vjp(_gla_chunk, [q_ref[e] for e in range(bl)], [k_ref[e] for e in range(bl)],
                         [lr_ref[e] for e in range(bl)],
                         [v_ref[e, :, cols] for e, _, cols in heads], [og_ref[e, :, cols] for e, _, cols in heads],
                         [st_ref[e, 0, h] for e, h, _ in heads], wg_ref[...], bg_ref[...], nw_ref[...])
        dq, dk, dlr, dv, dog, ds, dwg, dbg, dnw = vjp(([dy_ref[e, :, cols] for e, _, cols in heads],
                                                         [ds_scr[e, h] for e, h, _ in heads]))
        for e in range(bl):
            da_ref[e, :, 0:256] = dq[e]
            da_ref[e, :, 256:512] = dk[e]
            dd_ref[e] = dlr[e]
        for i, (e, h, _) in enumerate(heads):
            da_ref[e, :, 512 + h * 128:512 + (h + 1) * 128] = dv[i]
            da_ref[e, :, 1024 + h * 128:1024 + (h + 1) * 128] = dog[i]
            ds_scr[e, h] = ds[i]
        dwg_ref[...] += dwg
        dbg_ref[...] += dbg
        dnw_ref[...] += dnw

    tok = lambda w, j: pl.BlockSpec((bl, CHUNK, w), lambda i: (0, nc - 1 - i, j))
    const = lambda i: (0, 0)
    pa3 = pa.reshape(bl, seq, 1536)
    da, dd, dwg, dbg, dnw = pl.pallas_call(
        body, name="gla_bwd", grid=(nc,),
        in_specs=[tok(256, 0), tok(256, 1), tok(512, 1), tok(512, 2), tok(128, 0),
                  pl.BlockSpec((bl, 1, GLA_HEADS, 128, 256), lambda i: (0, nc - 1 - i, 0, 0, 0)), tok(512, 0),
                  pl.BlockSpec(wg.shape, const), pl.BlockSpec(bg.shape, const), pl.BlockSpec(nw.shape, const)],
        out_specs=[tok(1536, 0), tok(128, 0),
                   pl.BlockSpec(wg.shape, const), pl.BlockSpec(bg.shape, const), pl.BlockSpec(nw.shape, const)],
        out_shape=[jax.ShapeDtypeStruct((bl, seq, 1536), F32), jax.ShapeDtypeStruct((bl, seq, 128), F32),
                   jax.ShapeDtypeStruct(wg.shape, F32), jax.ShapeDtypeStruct(bg.shape, F32),
                   jax.ShapeDtypeStruct(nw.shape, F32)],
        scratch_shapes=[pltpu.VMEM((bl, GLA_HEADS, 128, 256), F32)],
        compiler_params=_params("arbitrary"),
    )(pa3, pa3, pa3, pa3, pd.reshape(bl, seq, 128), st, dya.reshape(bl, seq, 512), wg, bg, nw)
    return da.reshape(n, 1536), dd.reshape(n, 128), dwg, dbg, dnw


def _conv_taps(buf_ref, w_ref, base, rows):
    acc = w_ref[0:1, :] * buf_ref[pl.ds(base, rows), :]
    for k in range(1, CONV_K):
        acc = acc + w_ref[k:k + 1, :] * buf_ref[pl.ds(base + k, rows), :]
    return acc


def _gdn_pre_fwd(pb, pd, cw8, alog_v, dtb_v, bl, seq, tm=256):
    n = pb.shape[0]
    tpe = seq // tm
    t8 = tm // 8

    def body(u_ref, prev_ref, ab_ref, w_ref, al_ref, dt_ref, qkv_ref, gb_ref, buf):
        i = pl.program_id(0)
        keep = (i % tpe != 0).astype(F32)
        buf[0:8, :] = prev_ref[...] * keep
        buf[8:8 + tm, :] = u_ref[...]
        p = _conv_taps(buf, w_ref, 8 - (CONV_K - 1), tm)
        ps = [p[:, j * 128:(j + 1) * 128] for j in range(12)]
        outs = _gdn_pre_elem(ps, ab_ref[...], al_ref[...], dt_ref[...])
        for j in range(12):
            qkv_ref[:, j * 128:(j + 1) * 128] = outs[j]
        gb_ref[...] = outs[12]

    row = lambda i: (i, 0)
    const = lambda i: (0, 0)
    return pl.pallas_call(
        body, name="gdn_pre_fwd", grid=(n // tm,),
        in_specs=[pl.BlockSpec((tm, 1536), row),
                  pl.BlockSpec((8, 1536), lambda i: (jnp.maximum(i * t8 - 1, 0), 0)),
                  pl.BlockSpec((tm, 128), row),
                  pl.BlockSpec((8, 1536), const), pl.BlockSpec((1, 128), const), pl.BlockSpec((1, 128), const)],
        out_specs=[pl.BlockSpec((tm, 1536), row), pl.BlockSpec((tm, 128), row)],
        out_shape=[jax.ShapeDtypeStruct((n, 1536), F32), jax.ShapeDtypeStruct((n, 128), F32)],
        scratch_shapes=[pltpu.VMEM((tm + 8, 1536), F32)],
        compiler_params=_params("parallel"),
    )(pb, pb, pd, cw8, alog_v, dtb_v)


def _gdn_pre_bwd(pb, pd, dqkv, dgb, cw8, alog_v, dtb_v, bl, seq, tm=256):
    n = pb.shape[0]
    tpe = seq // tm
    t8 = tm // 8
    nb8 = n // 8
    ext = tm + 8

    def body(u_ref, prev_ref, next_ref, ab_ref, abn_ref, dq_ref, dqn_ref, dgb_ref, w_ref, al_ref, dt_ref,
             du_ref, dab_ref, dw_ref, dal_ref, ddt_ref, buf, dpbuf):
        i = pl.program_id(0)

        @pl.when(i == 0)
        def _():
            dw_ref[...] = jnp.zeros_like(dw_ref)
            dal_ref[...] = jnp.zeros_like(dal_ref)
            ddt_ref[...] = jnp.zeros_like(ddt_ref)

        keep_prev = (i % tpe != 0).astype(F32)
        keep_next = (i % tpe != tpe - 1).astype(F32)
        buf[0:8, :] = prev_ref[...] * keep_prev
        buf[8:8 + tm, :] = u_ref[...]
        buf[8 + tm:16 + tm, :] = next_ref[...]
        p = _conv_taps(buf, w_ref, 8 - (CONV_K - 1), ext)
        ps = [p[:, j * 128:(j + 1) * 128] for j in range(12)]
        ab = jnp.concatenate([ab_ref[...], abn_ref[...]], axis=0)
        _, vjp = jax.vjp(_gdn_pre_elem, ps, ab, al_ref[...], dt_ref[...])
        zeros8 = jnp.zeros((8, 128), F32)
        cts = tuple(jnp.concatenate([dq_ref[:, j * 128:(j + 1) * 128],
                                     dqn_ref[:, j * 128:(j + 1) * 128] * keep_next], axis=0) for j in range(12))
        cts += (jnp.concatenate([dgb_ref[...], zeros8], axis=0),)
        dps, dab, dal, ddt = vjp(cts)
        for j in range(12):
            dpbuf[:, j * 128:(j + 1) * 128] = dps[j]
        dab_ref[...] = dab[0:tm, :]
        dal_ref[...] += dal
        ddt_ref[...] += ddt
        du = w_ref[0:1, :] * dpbuf[pl.ds(CONV_K - 1, tm), :]
        for k in range(1, CONV_K):
            du = du + w_ref[k:k + 1, :] * dpbuf[pl.ds(CONV_K - 1 - k, tm), :]
        du_ref[...] = du
        dp_own = dpbuf[0:tm, :]
        for k in range(CONV_K):
            dw_ref[k:k + 1, :] += jnp.sum(dp_own * buf[pl.ds(8 - (CONV_K - 1) + k, tm), :], axis=0, keepdims=True)

    row = lambda i: (i, 0)
    prev8 = lambda i: (jnp.maximum(i * t8 - 1, 0), 0)
    next8 = lambda i: (jnp.minimum((i + 1) * t8, nb8 - 1), 0)
    const = lambda i: (0, 0)
    return pl.pallas_call(
        body, name="gdn_pre_bwd", grid=(n // tm,),
        in_specs=[pl.BlockSpec((tm, 1536), row), pl.BlockSpec((8, 1536), prev8), pl.BlockSpec((8, 1536), next8),
                  pl.BlockSpec((tm, 128), row), pl.BlockSpec((8, 128), next8),
                  pl.BlockSpec((tm, 1536), row), pl.BlockSpec((8, 1536), next8),
                  pl.BlockSpec((tm, 128), row),
                  pl.BlockSpec((8, 1536), const), pl.BlockSpec((1, 128), const), pl.BlockSpec((1, 128), const)],
        out_specs=[pl.BlockSpec((tm, 1536), row), pl.BlockSpec((tm, 128), row),
                   pl.BlockSpec((8, 1536), const), pl.BlockSpec((1, 128), const), pl.BlockSpec((1, 128), const)],
        out_shape=[jax.ShapeDtypeStruct((n, 1536), F32), jax.ShapeDtypeStruct((n, 128), F32),
                   jax.ShapeDtypeStruct((8, 1536), F32), jax.ShapeDtypeStruct((1, 128), F32),
                   jax.ShapeDtypeStruct((1, 128), F32)],
        scratch_shapes=[pltpu.VMEM((tm + 16, 1536), F32), pltpu.VMEM((ext, 1536), F32)],
        compiler_params=_params("arbitrary"),
    )(pb, pb, pb, pd, pd, dqkv, dqkv, dgb, cw8, alog_v, dtb_v)


GDN_PREP_CHUNKS = 2


def _head_cols(ref, rows, base=0):
    return [ref[rows, base + h * 128:base + (h + 1) * 128] for h in range(GDN_HEADS)]


def _gdn_prep(qkv, gb):
    n = qkv.shape[0]
    r_per = GDN_PREP_CHUNKS
    tm = r_per * CHUNK

    def body(q_ref, k_ref, v_ref, gb_ref, u_ref, w_ref, qd_ref, kd_ref, qk_ref, t_ref, gam_ref):
        rowid = _iota((8, 128), 0)
        chunk_rows = [slice(r * CHUNK, (r + 1) * CHUNK) for r in range(r_per)]
        gather = lambda ref: [t for rows in chunk_rows for t in _head_cols(ref, rows)]
        u, w, qk, qd, kd, gamma, tinv = _gdn_prep_units(gather(q_ref), gather(k_ref), gather(v_ref),
                                                        [gb_ref[rows, :] for rows in chunk_rows])
        for r, rows in enumerate(chunk_rows):
            gam = jnp.zeros((8, 128), F32)
            for h in range(GDN_HEADS):
                i = r * GDN_HEADS + h
                cols = slice(h * 128, (h + 1) * 128)
                u_ref[rows, cols] = u[i]
                w_ref[rows, cols] = w[i]
                qd_ref[rows, cols] = qd[i]
                kd_ref[rows, cols] = kd[i]
                qk_ref[r, h] = qk[i]
                t_ref[r, h] = tinv[i]
                gam = jnp.where(rowid == h, gamma[i], gam)
            gam_ref[r] = gam

    tok = lambda j: pl.BlockSpec((tm, 512), lambda i: (i, j))
    return pl.pallas_call(
        body, name="gdn_prep", grid=(n // tm,),
        in_specs=[tok(0), tok(1), tok(2), pl.BlockSpec((tm, 128), lambda i: (i, 0))],
        out_specs=[tok(0)] * 4 + [pl.BlockSpec((r_per, GDN_HEADS, CHUNK, CHUNK), lambda i: (i, 0, 0, 0))] * 2
        + [pl.BlockSpec((r_per, 8, 128), lambda i: (i, 0, 0))],
        out_shape=[jax.ShapeDtypeStruct((n, 512), F32)] * 4
        + [jax.ShapeDtypeStruct((n // CHUNK, GDN_HEADS, CHUNK, CHUNK), F32)] * 2
        + [jax.ShapeDtypeStruct((n // CHUNK, 8, 128), F32)],
        compiler_params=_params("parallel"),
    )(qkv, qkv, qkv, gb)


def _gdn_fwd(qkv, gb, pc, nw, bl, seq):
    n = qkv.shape[0]
    nc = seq // CHUNK
    u, w, qd, kd, qk, tinv, gam = _gdn_prep(qkv, gb)
    tok3 = lambda t: t.reshape(bl, seq, 512)
    qk5 = qk.reshape(bl, nc, GDN_HEADS, CHUNK, CHUNK)
    gam4 = gam.reshape(bl, nc, 8, 128)

    def body(u_ref, w_ref, qd_ref, kd_ref, qk_ref, gam_ref, og_ref, nw_ref, o_ref, y_ref, st_ref, s_scr):
        @pl.when(pl.program_id(0) == 0)
        def _():
            s_scr[...] = jnp.zeros_like(s_scr)

        units = [(b, h, slice(h * 128, (h + 1) * 128)) for b in range(bl) for h in range(GDN_HEADS)]
        ss = [s_scr[b, h] for b, h, _ in units]
        for (b, h, _), s in zip(units, ss):
            st_ref[b, 0, h] = s
        v_new = [u_ref[b, :, cols] - _dot(w_ref[b, :, cols], s) for (b, h, cols), s in zip(units, ss)]
        o_inter = [_dot(qd_ref[b, :, cols], s) for (b, h, cols), s in zip(units, ss)]
        os_ = [oi + _dot(qk_ref[b, 0, h], vn) for (b, h, cols), oi, vn in zip(units, o_inter, v_new)]
        for (b, h, cols), s, vn in zip(units, ss, v_new):
            s_scr[b, h] = s * gam_ref[b, 0, h:h + 1, :] + _dot_tn(kd_ref[b, :, cols], vn)
        for (b, h, cols), o in zip(units, os_):
            o_ref[b, :, cols] = o
            y_ref[b, :, cols] = _rms_gate(o, nw_ref[...], og_ref[b, :, cols])

    tok = pl.BlockSpec((bl, CHUNK, 512), lambda i: (0, i, 0))
    st_spec = pl.BlockSpec((bl, 1, GDN_HEADS, 128, 128), lambda i: (0, i, 0, 0, 0))
    o, y, st = pl.pallas_call(
        body, name="gdn_scan_fwd", grid=(nc,),
        in_specs=[tok, tok, tok, tok,
                  pl.BlockSpec((bl, 1, GDN_HEADS, CHUNK, CHUNK), lambda i: (0, i, 0, 0, 0)),
                  pl.BlockSpec((bl, 1, 8, 128), lambda i: (0, i, 0, 0)), tok,
                  pl.BlockSpec(nw.shape, lambda i: (0, 0))],
        out_specs=[tok, tok, st_spec],
        out_shape=[jax.ShapeDtypeStruct((bl, seq, 512), F32), jax.ShapeDtypeStruct((bl, seq, 512), F32),
                   jax.ShapeDtypeStruct((bl, nc, GDN_HEADS, 128, 128), F32)],
        scratch_shapes=[pltpu.VMEM((bl, GDN_HEADS, 128, 128), F32)],
        compiler_params=_params("arbitrary"),
    )(tok3(u), tok3(w), tok3(qd), tok3(kd), qk5, gam4, tok3(pc), nw)
    return y.reshape(n, 512), (o, st, w, qd, kd, qk5, gam4, tinv)


def _gdn_bwd(qkv, gb, pc, res, dyb, nw, bl, seq):
    n = qkv.shape[0]
    nc = seq // CHUNK
    o, st, w, qd, kd, qk5, gam4, tinv = res
    tok3 = lambda t: t.reshape(bl, seq, 512)

    def scan_body(dy_ref, o_ref, og_ref, w_ref, qd_ref, kd_ref, qk_ref, gam_ref, nw_ref,
                  do_ref, dog_ref, dvn_ref, dst_ref, dnw_ref, ds_scr):
        @pl.when(pl.program_id(0) == 0)
        def _():
            ds_scr[...] = jnp.zeros_like(ds_scr)
            dnw_ref[...] = jnp.zeros_like(dnw_ref)

        units = [(b, h, slice(h * 128, (h + 1) * 128)) for b in range(bl) for h in range(GDN_HEADS)]
        dnw = jnp.zeros(nw.shape, F32)
        d_os = []
        for b, h, cols in units:
            _, vjp = jax.vjp(_rms_gate, o_ref[b, :, cols], nw_ref[...], og_ref[b, :, cols])
            d_o, dnw_h, dog = vjp(dy_ref[b, :, cols])
            do_ref[b, :, cols] = d_o
            dog_ref[b, :, cols] = dog
            dnw = dnw + dnw_h
            d_os.append(d_o)
        dnw_ref[...] += dnw
        dss = [ds_scr[b, h] for b, h, _ in units]
        for (b, h, _), ds in zip(units, dss):
            dst_ref[b, 0, h] = ds
        dvn_a = [_dot(kd_ref[b, :, cols], ds) for (b, h, cols), ds in zip(units, dss)]
        dvns = [a + _dot_tn(qk_ref[b, 0, h], d_o) for (b, h, cols), a, d_o in zip(units, dvn_a, d_os)]
        ds_a = [_dot_tn(qd_ref[b, :, cols], d_o) + ds * gam_ref[b, 0, h:h + 1, :]
                for (b, h, cols), d_o, ds in zip(units, d_os, dss)]
        for (b, h, cols), a, dvn in zip(units, ds_a, dvns):
            dvn_ref[b, :, cols] = dvn
            ds_scr[b, h] = a - _dot_tn(w_ref[b, :, cols], dvn)

    rev = lambda i: nc - 1 - i
    tok = pl.BlockSpec((bl, CHUNK, 512), lambda i: (0, rev(i), 0))
    st_spec = pl.BlockSpec((bl, 1, GDN_HEADS, 128, 128), lambda i: (0, rev(i), 0, 0, 0))
    tok_shape = jax.ShapeDtypeStruct((bl, seq, 512), F32)
    d_o, dog, dvn, dst, dnw = pl.pallas_call(
        scan_body, name="gdn_scan_bwd", grid=(nc,),
        in_specs=[tok] * 6 + [pl.BlockSpec((bl, 1, GDN_HEADS, CHUNK, CHUNK), lambda i: (0, rev(i), 0, 0, 0)),
                              pl.BlockSpec((bl, 1, 8, 128), lambda i: (0, rev(i), 0, 0)),
                              pl.BlockSpec(nw.shape, lambda i: (0, 0))],
        out_specs=[tok, tok, tok, st_spec, pl.BlockSpec(nw.shape, lambda i: (0, 0))],
        out_shape=[tok_shape, tok_shape, tok_shape, jax.ShapeDtypeStruct(st.shape, F32),
                   jax.ShapeDtypeStruct(nw.shape, F32)],
        scratch_shapes=[pltpu.VMEM((bl, GDN_HEADS, 128, 128), F32)],
        compiler_params=_params("arbitrary"),
    )(tok3(dyb), o, tok3(pc), tok3(w), tok3(qd), tok3(kd), qk5, gam4, nw)

    r_per = GDN_PREP_CHUNKS
    tm = r_per * CHUNK

    def prep_body(q_ref, k_ref, v_ref, gb_ref, t_ref, st_ref, dst_ref, dvn_ref, do_ref, dqkv_ref, dgb_ref):
        chunk_rows = [slice(r * CHUNK, (r + 1) * CHUNK) for r in range(r_per)]
        gather = lambda ref: [t for rows in chunk_rows for t in _head_cols(ref, rows)]
        units = [(r, h) for r in range(r_per) for h in range(GDN_HEADS)]
        t_known = [t_ref[r, h] for r, h in units]
        prep = lambda q, k, v, g: _gdn_prep_units(q, k, v, g, t_known)[:6]
        (u, w_, _, _, _, _), vjp = jax.vjp(prep, gather(q_ref), gather(k_ref), gather(v_ref),
                                           [gb_ref[rows, :] for rows in chunk_rows])
        ss = [st_ref[r, h] for r, h in units]
        dss = [dst_ref[r, h] for r, h in units]
        dvns, d_os = gather(dvn_ref), gather(do_ref)
        v_new = [ui - _dot(wi, s) for ui, wi, s in zip(u, w_, ss)]
        d_w = [-_dot_nt(dvn, s) for dvn, s in zip(dvns, ss)]
        d_qk = [_dot_nt(d_o, vn) for d_o, vn in zip(d_os, v_new)]
        d_qd = [_dot_nt(d_o, s) for d_o, s in zip(d_os, ss)]
        d_kd = [_dot_nt(vn, ds) for vn, ds in zip(v_new, dss)]
        d_gam = [_sum_all(ds * s) for ds, s in zip(dss, ss)]
        dq, dk, dv, dgb = vjp((dvns, d_w, d_qk, d_qd, d_kd, d_gam))
        for i, (r, h) in enumerate(units):
            rows = chunk_rows[r]
            for part, d in enumerate((dq, dk, dv)):
                dqkv_ref[rows, part * 512 + h * 128:part * 512 + (h + 1) * 128] = d[i]
        for r, rows in enumerate(chunk_rows):
            dgb_ref[rows, :] = dgb[r]

    tokp = lambda j: pl.BlockSpec((tm, 512), lambda i: (i, j))
    st4 = pl.BlockSpec((r_per, GDN_HEADS, 128, 128), lambda i: (i, 0, 0, 0))
    dqkv, dgb = pl.pallas_call(
        prep_body, name="gdn_prep_bwd", grid=(n // tm,),
        in_specs=[tokp(0), tokp(1), tokp(2), pl.BlockSpec((tm, 128), lambda i: (i, 0)),
                  pl.BlockSpec((r_per, GDN_HEADS, CHUNK, CHUNK), lambda i: (i, 0, 0, 0)), st4, st4, tokp(0), tokp(0)],
        out_specs=[pl.BlockSpec((tm, 1536), lambda i: (i, 0)), pl.BlockSpec((tm, 128), lambda i: (i, 0))],
        out_shape=[jax.ShapeDtypeStruct((n, 1536), F32), jax.ShapeDtypeStruct((n, 128), F32)],
        compiler_params=_params("parallel"),
    )(qkv, qkv, qkv, gb, tinv, st.reshape(bl * nc, GDN_HEADS, 128, 128), dst.reshape(bl * nc, GDN_HEADS, 128, 128),
      dvn.reshape(n, 512), d_o.reshape(n, 512))
    return dqkv, dog.reshape(n, 512), dgb, dnw


def _out_block(x2, tgt2, ya, yb, g1p3, wo, lnw, lnb, seq, tm=256):
    n = x2.shape[0]
    tpe = seq // tm
    bl = n // seq

    def body(x_ref, t_ref, ya_ref, yb_ref, g_ref, wo_ref, lnw_ref, lnb_ref,
             dz_ref, dya_ref, dyb_ref, dwo_ref, dg_ref, glw_ref, glb_ref, loss_ref):
        i = pl.program_id(0)

        @pl.when(i == 0)
        def _():
            dwo_ref[...] = jnp.zeros_like(dwo_ref)
            glw_ref[...] = jnp.zeros_like(glw_ref)
            glb_ref[...] = jnp.zeros_like(glb_ref)
            loss_ref[...] = jnp.zeros_like(loss_ref)

        @pl.when(i % tpe == 0)
        def _():
            dg_ref[...] = jnp.zeros_like(dg_ref)

        ya16 = ya_ref[...].astype(wo.dtype)
        yb16 = yb_ref[...].astype(wo.dtype)
        wa = wo_ref[0:GLA_WIDTH, :]
        wb = wo_ref[GLA_WIDTH:, :]
        y = _dot(ya16, wa) + _dot(yb16, wb)
        g1p = g_ref[0]
        z = ALPHA * x_ref[...] + g1p * y
        mu = jnp.mean(z, axis=-1, keepdims=True)
        zc = z - mu
        rstd = lax.rsqrt(jnp.mean(zc * zc, axis=-1, keepdims=True) + LN_EPS)
        xhat = zc * rstd
        diff = xhat * lnw_ref[...] + lnb_ref[...] - t_ref[...]
        loss_ref[...] += (0.5 / D_MODEL) * jnp.sum(jnp.sum(diff * diff, axis=-1, keepdims=True), axis=0, keepdims=True)
        dout = diff * (1.0 / D_MODEL)
        glw_ref[...] += jnp.sum(dout * xhat, axis=0, keepdims=True)
        glb_ref[...] += jnp.sum(dout, axis=0, keepdims=True)
        dxh = dout * lnw_ref[...]
        dz = rstd * (dxh - jnp.mean(dxh, axis=-1, keepdims=True)
                     - xhat * jnp.mean(dxh * xhat, axis=-1, keepdims=True))
        dz_ref[...] = dz
        dg_ref[0] += jnp.sum(dz * y, axis=0, keepdims=True)
        dy = (g1p * dz).astype(wo.dtype)
        dya_ref[...] = _dot_nt(dy, wa)
        dyb_ref[...] = _dot_nt(dy, wb)
        dwo_ref[0:GLA_WIDTH, :] += _dot_tn(ya16, dy)
        dwo_ref[GLA_WIDTH:, :] += _dot_tn(yb16, dy)

    row = lambda i: (i, 0)
    const = lambda i: (0, 0)
    per_ex = pl.BlockSpec((1, 1, D_MODEL), lambda i: (i // tpe, 0, 0))
    return pl.pallas_call(
        body, name="out_block", grid=(n // tm,),
        in_specs=[pl.BlockSpec((tm, D_MODEL), row), pl.BlockSpec((tm, D_MODEL), row),
                  pl.BlockSpec((tm, 512), row), pl.BlockSpec((tm, 512), row), per_ex,
                  pl.BlockSpec((D_MODEL, D_MODEL), const), pl.BlockSpec((1, D_MODEL), const),
                  pl.BlockSpec((1, D_MODEL), const)],
        out_specs=[pl.BlockSpec((tm, D_MODEL), row), pl.BlockSpec((tm, 512), row), pl.BlockSpec((tm, 512), row),
                   pl.BlockSpec((D_MODEL, D_MODEL), const), per_ex,
                   pl.BlockSpec((1, D_MODEL), const), pl.BlockSpec((1, D_MODEL), const),
                   pl.BlockSpec((1, 1), const)],
        out_shape=[jax.ShapeDtypeStruct((n, D_MODEL), F32), jax.ShapeDtypeStruct((n, 512), F32),
                   jax.ShapeDtypeStruct((n, 512), F32), jax.ShapeDtypeStruct((D_MODEL, D_MODEL), F32),
                   jax.ShapeDtypeStruct((bl, 1, D_MODEL), F32), jax.ShapeDtypeStruct((1, D_MODEL), F32),
                   jax.ShapeDtypeStruct((1, D_MODEL), F32), jax.ShapeDtypeStruct((1, 1), F32)],
        compiler_params=_params("arbitrary"),
    )(x2, tgt2, ya, yb, g1p3, wo, lnw, lnb)


def _proj_bwd_x(ds, ws, x2, dz, sc3, seq, tm=256):
    n = x2.shape[0]
    tpe = seq // tm
    bl = n // seq

    def body(da_ref, db_ref, dc_ref, dd1_ref, dd2_ref, wa_ref, wb_ref, wc_ref, wd_ref, x_ref, dz_ref, sc_ref,
             gx_ref, dsh_ref, dsc_ref):
        i = pl.program_id(0)

        @pl.when(i % tpe == 0)
        def _():
            dsh_ref[...] = jnp.zeros_like(dsh_ref)
            dsc_ref[...] = jnp.zeros_like(dsc_ref)

        cdt = ws[0].dtype
        dh = _dot_nt(da_ref[...].astype(cdt), wa_ref[...])
        dh += _dot_nt(db_ref[...].astype(cdt), wb_ref[...])
        dh += _dot_nt(dc_ref[...].astype(cdt), wc_ref[...])
        dh += _dot_nt((dd1_ref[...] + dd2_ref[...]).astype(cdt), wd_ref[...])
        gx_ref[...] = dh * sc_ref[0] + ALPHA * dz_ref[...]
        dsh_ref[0] += jnp.sum(dh, axis=0, keepdims=True)
        dsc_ref[0] += jnp.sum(dh * x_ref[...], axis=0, keepdims=True)

    row = lambda i: (i, 0)
    const = lambda i: (0, 0)
    per_ex = pl.BlockSpec((1, 1, D_MODEL), lambda i: (i // tpe, 0, 0))
    da, db, dc, (dd1, dd2) = ds
    return pl.pallas_call(
        body, name="proj_bwd_x", grid=(n // tm,),
        in_specs=[pl.BlockSpec((tm, d.shape[1]), row) for d in (da, db, dc, dd1, dd2)]
        + [pl.BlockSpec(w.shape, const) for w in ws]
        + [pl.BlockSpec((tm, D_MODEL), row), pl.BlockSpec((tm, D_MODEL), row), per_ex],
        out_specs=[pl.BlockSpec((tm, D_MODEL), row), per_ex, per_ex],
        out_shape=[jax.ShapeDtypeStruct((n, D_MODEL), F32), jax.ShapeDtypeStruct((bl, 1, D_MODEL), F32),
                   jax.ShapeDtypeStruct((bl, 1, D_MODEL), F32)],
        compiler_params=_params("arbitrary"),
    )(da, db, dc, dd1, dd2, *ws, x2, dz, sc3)


def _proj_bwd_w(x2, sc3, sh3, ds, seq, cdt, name, tm=256):
    n = x2.shape[0]
    tpe = seq // tm
    flat, groups = [], []
    for d in ds:
        parts = d if isinstance(d, tuple) else (d,)
        groups.append(len(parts))
        flat.extend(parts)
    nin = len(flat)

    def body(x_ref, sc_ref, sh_ref, *refs):
        i = pl.program_id(0)
        outs = refs[nin:]

        @pl.when(i == 0)
        def _():
            for o in outs:
                o[...] = jnp.zeros_like(o)

        h = (x_ref[...] * sc_ref[0] + sh_ref[0]).astype(cdt)
        pos = 0
        for o, cnt in zip(outs, groups):
            d = refs[pos][...]
            for extra in refs[pos + 1:pos + cnt]:
                d = d + extra[...]
            pos += cnt
            o[...] += _dot_tn(h, d.astype(cdt))

    row = lambda i: (i, 0)
    const = lambda i: (0, 0)
    per_ex = pl.BlockSpec((1, 1, D_MODEL), lambda i: (i // tpe, 0, 0))
    widths = [(d[0] if isinstance(d, tuple) else d).shape[1] for d in ds]
    return pl.pallas_call(
        body, name=name, grid=(n // tm,),
        in_specs=[pl.BlockSpec((tm, D_MODEL), row), per_ex, per_ex]
        + [pl.BlockSpec((tm, d.shape[1]), row) for d in flat],
        out_specs=[pl.BlockSpec((D_MODEL, w), const) for w in widths],
        out_shape=[jax.ShapeDtypeStruct((D_MODEL, w), F32) for w in widths],
        compiler_params=_params("arbitrary"),
    )(x2, sc3, sh3, *flat)


def _mod_block(c_all, w_ada_sh, b_blk):
    def body(c_ref, w_ref, b_ref, o_ref):
        o_ref[...] = _dot(c_ref[...], w_ref[...]) + b_ref[...]

    return pl.pallas_call(
        body, name="mod_block",
        out_shape=jax.ShapeDtypeStruct((c_all.shape[0], w_ada_sh.shape[1]), F32),
        compiler_params=pltpu.CompilerParams(vmem_limit_bytes=VMEM_LIMIT),
    )(c_all, w_ada_sh, b_blk)


def _ada_grads(c_all, dmod_all, dmod_blk):
    def body(c_ref, da_ref, db_ref, gw_ref, gb_ref):
        gw_ref[...] = _dot_tn(c_ref[...], db_ref[...])
        gb_ref[...] = jnp.sum(da_ref[...], axis=0, keepdims=True)

    return pl.pallas_call(
        body, name="ada_grads",
        out_shape=[jax.ShapeDtypeStruct((c_all.shape[1], dmod_blk.shape[1]), F32),
                   jax.ShapeDtypeStruct((1, dmod_all.shape[1]), F32)],
        compiler_params=pltpu.CompilerParams(vmem_limit_bytes=VMEM_LIMIT),
    )(c_all, dmod_all, dmod_blk)


def _sum_leading(parts, name):
    def body(p_ref, o_ref):
        acc = p_ref[0]
        for d in range(1, parts.shape[0]):
            acc = acc + p_ref[d]
        o_ref[...] = acc

    return pl.pallas_call(
        body, name=name, out_shape=jax.ShapeDtypeStruct(parts.shape[1:], F32),
        compiler_params=pltpu.CompilerParams(vmem_limit_bytes=VMEM_LIMIT),
    )(parts)


def _row_tile(rows):
    for t in (256, 128, 64, 32, 16, 8):
        if rows % t == 0 and rows > t:
            return t
    return rows


def _add_n(arrs, name, out_dtypes=(F32,)):
    rows, cols = arrs[0].shape
    tr = _row_tile(rows)
    n_in = len(arrs)

    def body(*refs):
        acc = refs[0][...].astype(F32)
        for r in refs[1:n_in]:
            acc = acc + r[...].astype(F32)
        for o in refs[n_in:]:
            o[...] = acc.astype(o.dtype)

    spec = pl.BlockSpec((tr, cols), lambda i: (i, 0))
    return pl.pallas_call(
        body, name=name, grid=(rows // tr,), in_specs=[spec] * n_in, out_specs=[spec] * len(out_dtypes),
        out_shape=[jax.ShapeDtypeStruct((rows, cols), dt) for dt in out_dtypes], compiler_params=_params("parallel"),
    )(*arrs)


def _adamw(w, g, m, v, name):
    rows, cols = w.shape
    tr = _row_tile(rows)
    c1 = 1.0 / (1.0 - ADAM_B1 ** ADAM_STEP)
    c2 = 1.0 / (1.0 - ADAM_B2 ** ADAM_STEP)

    def body(w_ref, g_ref, m_ref, v_ref, d_ref, nm_ref, nv_ref):
        gg = g_ref[...]
        nm = ADAM_B1 * m_ref[...] + (1.0 - ADAM_B1) * gg
        nv = ADAM_B2 * v_ref[...] + (1.0 - ADAM_B2) * (gg * gg)
        nm_ref[...] = nm
        nv_ref[...] = nv
        d_ref[...] = -ADAM_LR * ((nm * c1) / (jnp.sqrt(nv * c2) + ADAM_EPS) + ADAM_WD * w_ref[...])

    spec = pl.BlockSpec((tr, cols), lambda i: (i, 0))
    shp = jax.ShapeDtypeStruct((rows, cols), F32)
    return pl.pallas_call(
        body, name=name, grid=(rows // tr,), in_specs=[spec] * 4, out_specs=[spec] * 3,
        out_shape=[shp, shp, shp], compiler_params=_params("parallel"),
    )(w, g, m, v)


def _coords():
    return lax.axis_index("x"), lax.axis_index("y"), lax.axis_index("c")


def _all_gather8(blk, name):
    m_per, n = blk.shape

    def body(x_ref, out_ref, send_sems, recv_sems, local_sem):
        x, y, c = _coords()
        me, sibling = (x, y, c), (x, y, 1 - c)
        chips = [(1 - x, y), (x, 1 - y), (1 - x, 1 - y)]

        def rows(px, py, pc):
            return out_ref.at[pl.ds((4 * px + 2 * py + pc) * m_per, m_per), :]

        def copy(k, block, to, src=None):
            return pltpu.make_async_remote_copy(
                src_ref=rows(*block) if src is None else src, dst_ref=rows(*block),
                send_sem=send_sems.at[k], recv_sem=recv_sems.at[k], device_id=to, device_id_type=MESH)

        mine = pltpu.make_async_copy(x_ref, rows(*me), local_sem)
        mine.start()
        first = [copy(0, me, sibling, src=x_ref)]
        first += [copy(1 + j, me, (*chip, c), src=x_ref) for j, chip in enumerate(chips)]
        for cp in first:
            cp.start()
        passed = [copy(4 + j, (*chip, c), sibling) for j, chip in enumerate(chips)]
        for j, chip in enumerate(chips):
            copy(1 + j, (*chip, c), me).wait_recv()
            passed[j].start()
        copy(0, sibling, me).wait_recv()
        for j, chip in enumerate(chips):
            copy(4 + j, (*chip, 1 - c), me).wait_recv()
        for cp in first + passed:
            cp.wait_send()
        mine.wait()

    return pl.pallas_call(
        body, name=name,
        out_shape=jax.ShapeDtypeStruct((8 * m_per, n), blk.dtype),
        in_specs=[pl.BlockSpec(memory_space=pltpu.VMEM)],
        out_specs=pl.BlockSpec(memory_space=pltpu.VMEM),
        scratch_shapes=[pltpu.SemaphoreType.DMA((7,)), pltpu.SemaphoreType.DMA((7,)), pltpu.SemaphoreType.DMA],
        compiler_params=pltpu.CompilerParams(vmem_limit_bytes=VMEM_LIMIT),
    )(blk)


def _chip_gather(shards, split, name):
    k_arr = len(shards)

    def body(*refs):
        srcs, dsts = refs[:k_arr], refs[k_arr:2 * k_arr]
        send_sems, recv_sems, fwd_send_sems, fwd_recv_sems, local_sems = refs[2 * k_arr:]
        x, y, c = _coords()
        peers = [(1 - x, y, c), (x, 1 - y, c), (1 - x, 1 - y, c)]
        sibling = (x, y, 1 - c)
        me_chip = 2 * x + y

        def part(ref, a, core):
            if not split[a]:
                return ref
            half = shards[a].shape[0] // 2
            return ref.at[pl.ds(core * half, half), :]

        def ici(a, j, src_chip, dst_dev):
            return pltpu.make_async_remote_copy(
                src_ref=part(srcs[a], a, c), dst_ref=part(dsts[a].at[src_chip], a, c),
                send_sem=send_sems.at[a, j], recv_sem=recv_sems.at[a, j], device_id=dst_dev, device_id_type=MESH)

        def d2d(a, j, src_chip, core):
            return pltpu.make_async_remote_copy(
                src_ref=part(dsts[a].at[src_chip], a, core), dst_ref=part(dsts[a].at[src_chip], a, core),
                send_sem=fwd_send_sems.at[a, j], recv_sem=fwd_recv_sems.at[a, j],
                device_id=sibling, device_id_type=MESH)

        local = [pltpu.make_async_copy(srcs[a], dsts[a].at[me_chip], local_sems.at[a]) for a in range(k_arr)]
        for cp in local:
            cp.start()
        sends = [ici(a, j, me_chip, peer) for a in range(k_arr) for j, peer in enumerate(peers)]
        for cp in sends:
            cp.start()
        forwards = []
        for a in range(k_arr):
            for j, peer in enumerate(peers):
                peer_chip = 2 * peer[0] + peer[1]
                ici(a, j, peer_chip, peer).wait_recv()
                if split[a]:
                    forwards.append(d2d(a, j, peer_chip, c))
                    forwards[-1].start()
        for a in range(k_arr):
            for j, peer in enumerate(peers):
                if split[a]:
                    d2d(a, j, 2 * peer[0] + peer[1], 1 - c).wait_recv()
        for cp in sends + forwards:
            cp.wait_send()
        for cp in local:
            cp.wait()

    any_spec = pl.BlockSpec(memory_space=pl.ANY)
    return pl.pallas_call(
        body, name=name,
        out_shape=[jax.ShapeDtypeStruct((4,) + s.shape, s.dtype) for s in shards],
        in_specs=[any_spec] * k_arr, out_specs=[any_spec] * k_arr,
        scratch_shapes=[pltpu.SemaphoreType.DMA((k_arr, 3))] * 4 + [pltpu.SemaphoreType.DMA((k_arr,))],
    )(*shards)


def _chip_scatter(pieces, name):
    k_arr = len(pieces)

    def body(*refs):
        srcs, dsts = refs[:k_arr], refs[k_arr:2 * k_arr]
        send_sems, recv_sems = refs[2 * k_arr:]
        x, y, c = _coords()
        peers = [(1 - x, y, c), (x, 1 - y, c), (1 - x, 1 - y, c)]
        copies = []
        for a in range(k_arr):
            for j, peer in enumerate(peers):
                copies.append(pltpu.make_async_remote_copy(
                    src_ref=srcs[a].at[2 * peer[0] + peer[1]], dst_ref=dsts[a].at[j],
                    send_sem=send_sems.at[a, j], recv_sem=recv_sems.at[a, j], device_id=peer, device_id_type=MESH))
        for cp in copies:
            cp.start()
        for cp in copies:
            cp.wait_recv()
        for cp in copies:
            cp.wait_send()

    any_spec = pl.BlockSpec(memory_space=pl.ANY)
    return pl.pallas_call(
        body, name=name,
        out_shape=[jax.ShapeDtypeStruct((3,) + p.shape[1:], p.dtype) for p in pieces],
        in_specs=[any_spec] * k_arr, out_specs=[any_spec] * k_arr,
        scratch_shapes=[pltpu.SemaphoreType.DMA((k_arr, 3)), pltpu.SemaphoreType.DMA((k_arr, 3))],
    )(*pieces)


def _sibling_swap(arrs, name):
    k_arr = len(arrs)

    def body(*refs):
        srcs, dsts = refs[:k_arr], refs[k_arr:2 * k_arr]
        send_sems, recv_sems = refs[2 * k_arr:]
        x, y, c = _coords()
        copies = [pltpu.make_async_remote_copy(
            src_ref=srcs[a], dst_ref=dsts[a], send_sem=send_sems.at[a], recv_sem=recv_sems.at[a],
            device_id=(x, y, 1 - c), device_id_type=MESH) for a in range(k_arr)]
        for cp in copies:
            cp.start()
        for cp in copies:
            cp.wait_recv()
        for cp in copies:
            cp.wait_send()

    any_spec = pl.BlockSpec(memory_space=pl.ANY)
    return pl.pallas_call(
        body, name=name,
        out_shape=[jax.ShapeDtypeStruct(a.shape, a.dtype) for a in arrs],
        in_specs=[any_spec] * k_arr, out_specs=[any_spec] * k_arr,
        scratch_shapes=[pltpu.SemaphoreType.DMA((k_arr,)), pltpu.SemaphoreType.DMA((k_arr,))],
    )(*arrs)


def _split_w_in(w_in):
    wa = jnp.concatenate([w_in[:, 0:1024], w_in[:, 1040:1552]], axis=1)
    wb = w_in[:, 1552:3088]
    wc = w_in[:, 3096:3608]
    wd = jnp.concatenate([w_in[:, 1024:1040], w_in[:, 3088:3096],
                          jnp.zeros((w_in.shape[0], 128 - SMALL_USED), w_in.dtype)], axis=1)
    return wa, wb, wc, wd


def _merge_dw_in(dwa, dwb, dwc, dwd):
    return jnp.concatenate([dwa[:, 0:1024], dwd[:, 0:GLA_RANK], dwa[:, 1024:1536], dwb,
                            dwd[:, GLA_RANK:SMALL_USED], dwc], axis=1)


def _local_step(x, mod, w_in16, w_out16, gla_wg, gla_bg, gla_nw, conv_w, a_log, dt_bias, gdn_nw, ln_w, ln_b, tgt):
    bl, seq, _ = x.shape
    n = bl * seq
    x2 = x.reshape(n, D_MODEL)
    tgt2 = tgt.reshape(n, D_MODEL)
    sh3 = mod[:, None, 0:D_MODEL]
    sc3 = 1.0 + mod[:, None, D_MODEL:2 * D_MODEL]
    g1p3 = 1.0 + mod[:, None, 2 * D_MODEL:]
    ws = _split_w_in(w_in16)
    wg = jnp.concatenate([gla_wg, jnp.zeros((128 - GLA_RANK, GLA_QK), F32)], axis=0)
    cw8 = jnp.concatenate([conv_w, jnp.zeros((8 - CONV_K, conv_w.shape[1]), F32)], axis=0)
    alog_v = jnp.zeros((1, 128), F32).at[:, LANE_A:LANE_A + GDN_HEADS].set(a_log)
    dtb_v = jnp.zeros((1, 128), F32).at[:, LANE_A:LANE_A + GDN_HEADS].set(dt_bias)

    pa, pb, pc, pd = _proj_fwd(x2, sc3, sh3, ws, seq)
    ya, st_a = _gla_fwd(pa, pd, wg, gla_bg, gla_nw, bl, seq)
    qkv, gb = _gdn_pre_fwd(pb, pd, cw8, alog_v, dtb_v, bl, seq)
    yb, st_b = _gdn_fwd(qkv, gb, pc, gdn_nw, bl, seq)
    dz, dya, dyb, d_wo, d_gate, d_lnw, d_lnb, loss = _out_block(x2, tgt2, ya, yb, g1p3, w_out16, ln_w, ln_b, seq)
    da, dd1, d_wg, d_bg, d_nwa = _gla_bwd(pa, pd, st_a, dya, wg, gla_bg, gla_nw, bl, seq)
    dqkv, dc, dgb, d_nwb = _gdn_bwd(qkv, gb, pc, st_b, dyb, gdn_nw, bl, seq)
    db, dd2, d_cw8, d_alog, d_dtb = _gdn_pre_bwd(pb, pd, dqkv, dgb, cw8, alog_v, dtb_v, bl, seq)
    gx, d_sh, d_sc = _proj_bwd_x((da, db, dc, (dd1, dd2)), ws, x2, dz, sc3, seq)
    (dwa,) = _proj_bwd_w(x2, sc3, sh3, [da], seq, w_in16.dtype, "proj_bwd_w_a")
    dwb, dwc, dwd = _proj_bwd_w(x2, sc3, sh3, [db, dc, (dd1, dd2)], seq, w_in16.dtype, "proj_bwd_w_bcd")
    grads = dict(
        w_in=_merge_dw_in(dwa, dwb, dwc, dwd),
        w_out=d_wo,
        gla_w_gate_up=d_wg[0:GLA_RANK, :],
        gla_b_gate=d_bg,
        gla_norm_w=d_nwa,
        gdn_conv_w=d_cw8[0:CONV_K, :],
        gdn_a_log=d_alog[:, LANE_A:LANE_A + GDN_HEADS],
        gdn_dt_bias=d_dtb[:, LANE_A:LANE_A + GDN_HEADS],
        gdn_norm_w=d_nwb,
        ln_w=d_lnw,
        ln_b=d_lnb,
        mod=jnp.concatenate([d_sh[:, 0, :], d_sc[:, 0, :], d_gate[:, 0, :]], axis=1),
    )
    return loss, gx.reshape(bl, seq, D_MODEL), grads


_SMALL = (("gla_b_gate", 256), ("gla_norm_w", 128), ("gdn_a_log", 4), ("gdn_dt_bias", 4), ("gdn_norm_w", 128),
          ("ln_w", 1024), ("ln_b", 1024), ("gla_w_gate_up", 16 * 256), ("gdn_conv_w", 4 * 1536), ("mod", 2 * 3072))


def _pack_small(grads):
    flat = jnp.concatenate([grads[k].reshape(-1) for k, _ in _SMALL])
    total = sum(sz for _, sz in _SMALL)
    rows = -(-total // 1024) * 8
    return jnp.concatenate([flat, jnp.zeros((rows * 128 - total,), F32)]).reshape(rows, 128)


def _unpack_small(flat):
    out, pos = {}, 0
    for k, sz in _SMALL:
        out[k] = flat[pos:pos + sz]
        pos += sz
    return out


def kernel(x, c, w_ada, b_ada, w_in, gla_w_gate_up, gla_b_gate, gla_norm_w, gdn_conv_w, gdn_a_log, gdn_dt_bias, gdn_norm_w, w_out, ln_w, ln_b, loss_target, m_w_ada, m_b_ada, m_w_in, m_gla_w_gate_up, m_gla_b_gate, m_gla_norm_w, m_gdn_conv_w, m_gdn_a_log, m_gdn_dt_bias, m_gdn_norm_w, m_w_out, m_ln_w, m_ln_b, v_w_ada, v_b_ada, v_w_in, v_gla_w_gate_up, v_gla_b_gate, v_gla_norm_w, v_gdn_conv_w, v_gdn_a_log, v_gdn_dt_bias, v_gdn_norm_w, v_w_out, v_ln_w, v_ln_b):
    ix, iy, ic = _coords()
    chip = 2 * ix + iy
    dev = 4 * ix + 2 * iy + ic
    bl = x.shape[0]
    ndev = 8

    c_all = _all_gather8(c.reshape(8, -1), "gather_c").reshape(ndev * bl, D_MODEL)
    ada_cols = w_ada.shape[2]
    b_blk = lax.dynamic_slice_in_dim(b_ada, chip * ada_cols, ada_cols, axis=1)
    mod_blk = _mod_block(c_all, w_ada[0], b_blk)
    mod_g = _all_gather8(mod_blk, "gather_mod").reshape(ndev, ndev * bl, ada_cols)
    mod_all = jnp.concatenate([mod_g[2 * j] for j in range(4)], axis=1)
    mod = lax.dynamic_slice_in_dim(mod_all, dev * bl, bl, axis=0)

    w_in_g, w_out_g, wg_g, cw_g = _chip_gather(
        [w_in[0].astype(BF16), w_out[0].astype(BF16), gla_w_gate_up[0], gdn_conv_w[0]],
        [True, True, False, False], "gather_weights")
    w_in16 = jnp.concatenate([w_in_g[j] for j in range(4)], axis=1)
    w_out16 = w_out_g.reshape(D_MODEL, D_MODEL)
    gla_wg = jnp.concatenate([wg_g[j] for j in range(4)], axis=1)
    conv_w = jnp.concatenate([cw_g[j] for j in range(4)], axis=1)

    loss, grad_x, gr = _local_step(x, mod, w_in16, w_out16, gla_wg, gla_b_gate, gla_norm_w, conv_w,
                                   gdn_a_log, gdn_dt_bias, gdn_norm_w, ln_w, ln_b, loss_target)
    loss = lax.psum(loss[0, 0], ("x", "y", "c"))

    packed = _pack_small(gr)
    prow = packed.shape[0]
    gathered = _all_gather8(packed, "gather_small").reshape(ndev, prow, 128)
    small = _unpack_small(_sum_leading(gathered, "sum_small").reshape(-1))
    mod_rows = gathered.reshape(ndev, prow * 128)[:, sum(sz for _, sz in _SMALL[:-1]):][:, :bl * 3 * D_MODEL]
    dmod_all = mod_rows.reshape(ndev * bl, 3 * D_MODEL)
    dmod_blk = lax.dynamic_slice_in_dim(dmod_all, chip * ada_cols, ada_cols, axis=1)
    g_w_ada, g_b_ada = _ada_grads(c_all, dmod_all, dmod_blk)
    wg_cols = gla_w_gate_up.shape[2]
    g_wg = lax.dynamic_slice_in_dim(small["gla_w_gate_up"].reshape(GLA_RANK, GLA_QK), chip * wg_cols, wg_cols, axis=1)
    cw_cols = gdn_conv_w.shape[2]
    g_cw = lax.dynamic_slice_in_dim(small["gdn_conv_w"].reshape(CONV_K, 3 * GDN_WIDTH), chip * cw_cols, cw_cols, axis=1)

    in_cols = w_in.shape[2]
    out_rows = w_out.shape[1]
    p_in = jnp.stack([gr["w_in"][:, j * in_cols:(j + 1) * in_cols] for j in range(4)])
    p_out = gr["w_out"].reshape(4, out_rows, D_MODEL)
    h_in, h_out = D_MODEL // 2, out_rows // 2
    mine_in = lax.dynamic_slice_in_dim(p_in, ic * h_in, h_in, axis=1)
    mine_out = lax.dynamic_slice_in_dim(p_out, ic * h_out, h_out, axis=1)
    theirs_in = lax.dynamic_slice_in_dim(p_in, (1 - ic) * h_in, h_in, axis=1)
    theirs_out = lax.dynamic_slice_in_dim(p_out, (1 - ic) * h_out, h_out, axis=1)
    got_in, got_out = _sibling_swap([theirs_in, theirs_out], "swap_halves")
    chip_in, chip_in16 = _add_n([mine_in.reshape(4 * h_in, in_cols), got_in.reshape(4 * h_in, in_cols)],
                                "chip_sum_in", (F32, BF16))
    chip_out, chip_out16 = _add_n([mine_out.reshape(4 * h_out, D_MODEL), got_out.reshape(4 * h_out, D_MODEL)],
                                  "chip_sum_out", (F32, BF16))
    chip_in = chip_in.reshape(4, h_in, in_cols)
    chip_out = chip_out.reshape(4, h_out, D_MODEL)
    rs_in, rs_out = _chip_scatter([chip_in16.reshape(4, h_in, in_cols), chip_out16.reshape(4, h_out, D_MODEL)],
                                  "scatter_grads")
    own_in = lax.dynamic_index_in_dim(chip_in, chip, axis=0, keepdims=False)
    own_out = lax.dynamic_index_in_dim(chip_out, chip, axis=0, keepdims=False)
    (half_in,) = _add_n([own_in, rs_in[0], rs_in[1], rs_in[2]], "reduce_in")
    (half_out,) = _add_n([own_out, rs_out[0], rs_out[1], rs_out[2]], "reduce_out")
    sib_in, sib_out = _sibling_swap([half_in, half_out], "swap_result")
    g_w_in = jnp.where(ic == 0, jnp.concatenate([half_in, sib_in], axis=0), jnp.concatenate([sib_in, half_in], axis=0))
    g_w_out = jnp.where(ic == 0, jnp.concatenate([half_out, sib_out], axis=0),
                        jnp.concatenate([sib_out, half_out], axis=0))

    grads = dict(
        w_ada=g_w_ada[None], b_ada=g_b_ada, w_in=g_w_in[None], gla_w_gate_up=g_wg[None],
        gla_b_gate=small["gla_b_gate"].reshape(1, -1), gla_norm_w=small["gla_norm_w"].reshape(1, -1),
        gdn_conv_w=g_cw[None], gdn_a_log=small["gdn_a_log"].reshape(1, -1),
        gdn_dt_bias=small["gdn_dt_bias"].reshape(1, -1), gdn_norm_w=small["gdn_norm_w"].reshape(1, -1),
        w_out=g_w_out[None], ln_w=small["ln_w"].reshape(1, -1), ln_b=small["ln_b"].reshape(1, -1))
    weights = dict(w_ada=w_ada, b_ada=b_ada, w_in=w_in, gla_w_gate_up=gla_w_gate_up, gla_b_gate=gla_b_gate,
                   gla_norm_w=gla_norm_w, gdn_conv_w=gdn_conv_w, gdn_a_log=gdn_a_log, gdn_dt_bias=gdn_dt_bias,
                   gdn_norm_w=gdn_norm_w, w_out=w_out, ln_w=ln_w, ln_b=ln_b)
    m_in = dict(w_ada=m_w_ada, b_ada=m_b_ada, w_in=m_w_in, gla_w_gate_up=m_gla_w_gate_up, gla_b_gate=m_gla_b_gate,
                gla_norm_w=m_gla_norm_w, gdn_conv_w=m_gdn_conv_w, gdn_a_log=m_gdn_a_log, gdn_dt_bias=m_gdn_dt_bias,
                gdn_norm_w=m_gdn_norm_w, w_out=m_w_out, ln_w=m_ln_w, ln_b=m_ln_b)
    v_in = dict(w_ada=v_w_ada, b_ada=v_b_ada, w_in=v_w_in, gla_w_gate_up=v_gla_w_gate_up, gla_b_gate=v_gla_b_gate,
                gla_norm_w=v_gla_norm_w, gdn_conv_w=v_gdn_conv_w, gdn_a_log=v_gdn_a_log, gdn_dt_bias=v_gdn_dt_bias,
                gdn_norm_w=v_gdn_norm_w, w_out=v_w_out, ln_w=v_ln_w, ln_b=v_ln_b)
    names = list(weights)
    delta, new_m, new_v = {}, {}, {}
    for nm in names:
        shp = weights[nm].shape
        two_d = (-1, shp[-1])
        d, a, b = _adamw(weights[nm].reshape(two_d), grads[nm].reshape(two_d), m_in[nm].reshape(two_d),
                         v_in[nm].reshape(two_d), "adamw_" + nm)
        delta[nm], new_m[nm], new_v[nm] = d.reshape(shp), a.reshape(shp), b.reshape(shp)
        grads[nm] = grads[nm].reshape(shp)
    return (loss, grad_x, *[grads[k] for k in names], *[delta[k] for k in names],
            *[new_m[k] for k in names], *[new_v[k] for k in names])
```

```python
import functools

import jax
import jax.numpy as jnp
from jax import lax
from jax.experimental import pallas as pl
from jax.experimental.pallas import tpu as pltpu

F32 = jnp.float32
BF16 = jnp.bfloat16
HI = lax.Precision.HIGHEST
INV_PREC = lax.Precision.HIGH
MESH = pl.DeviceIdType.MESH

D_MODEL = 1024
GLA_HEADS = 4
GLA_DK = 64
GLA_DV = 128
GLA_QK = 256
GLA_WIDTH = 512
GLA_RANK = 16
GLA_GATE_NORM = 16.0
GDN_HEADS = 4
GDN_DK = 128
GDN_WIDTH = 512
CONV_K = 4
CHUNK = 64
LN_EPS = 1e-5
RMS_EPS = 1e-6
ALPHA = 2.0 ** 0.25
IN_COLS = 3608

LANE_A = GLA_RANK
LANE_B = GLA_RANK + GDN_HEADS
SMALL_USED = GLA_RANK + 2 * GDN_HEADS

ADAM_LR = 0.001
ADAM_B1 = 0.9
ADAM_B2 = 0.999
ADAM_EPS = 1e-08
ADAM_WD = 0.01
ADAM_STEP = 10

VMEM_LIMIT = 56 * 1024 * 1024


def _iota(shape, dim):
    return lax.broadcasted_iota(jnp.int32, shape, dim)


def _dot(a, b, prec=None):
    return lax.dot_general(a, b, (((1,), (0,)), ((), ())), precision=prec, preferred_element_type=F32)


def _dot_nt(a, b, prec=None):
    return lax.dot_general(a, b, (((1,), (1,)), ((), ())), precision=prec, preferred_element_type=F32)


def _dot_tn(a, b, prec=None):
    return lax.dot_general(a, b, (((0,), (0,)), ((), ())), precision=prec, preferred_element_type=F32)


def _log_sigmoid(z):
    return jnp.minimum(z, 0.0) - jnp.log1p(jnp.exp(-jnp.abs(z)))


def _softplus(z):
    return jnp.maximum(z, 0.0) + jnp.log1p(jnp.exp(-jnp.abs(z)))


def _silu(z):
    return z * jax.nn.sigmoid(z)


def _rms_gate(o, nw, og):
    return o * lax.rsqrt(jnp.mean(o * o, axis=-1, keepdims=True) + RMS_EPS) * nw * _silu(og)


def _params(*sem):
    return pltpu.CompilerParams(dimension_semantics=sem, vmem_limit_bytes=VMEM_LIMIT)


def _gla_chunk(qs, ks, lrs, vs, ogs, ss, wg, bg, nw):
    c = qs[0].shape[0]
    units = [divmod(i, GLA_HEADS) for i in range(len(vs))]
    row, col = _iota((c, c), 0), _iota((c, c), 1)
    causal = row >= col
    first_half = (_iota((c, 1), 0) < c // 2).astype(F32)
    lane = _iota((1, GLA_QK), 1)
    masks = [((lane >= h * GLA_DK) & (lane < (h + 1) * GLA_DK)).astype(F32) for h in range(GLA_HEADS)]
    gs = [_log_sigmoid(_dot(lr, wg) + bg) * (1.0 / GLA_GATE_NORM) for lr in lrs]
    bs = [_dot(causal.astype(F32), g, HI) for g in gs]
    b_ref = [jnp.sum(g * first_half, axis=0, keepdims=True) for g in gs]
    b_last = [jnp.sum(g, axis=0, keepdims=True) for g in gs]
    qsc = [q * (GLA_DK ** -0.5) for q in qs]
    qe = [q * jnp.exp(b - br) for q, b, br in zip(qsc, bs, b_ref)]
    ke = [k * jnp.exp(br - b) for k, b, br in zip(ks, bs, b_ref)]
    qb = [q * jnp.exp(b) for q, b in zip(qsc, bs)]
    kd = [k * jnp.exp(bl_ - b) for k, b, bl_ in zip(ks, bs, b_last)]
    decay = [jnp.exp(bl_) for bl_ in b_last]
    att = [jnp.where(causal, _dot_nt(qe[e] * masks[h], ke[e]), 0.0) for e, h in units]
    o_inter = [_dot_nt(qb[e], s) for (e, h), s in zip(units, ss)]
    os_ = [_dot(a, v) + oi for a, v, oi in zip(att, vs, o_inter)]
    upd = [_dot_tn(v, kd[e]) for (e, h), v in zip(units, vs)]
    s_new = [s * decay[e] + masks[h] * up for (e, h), s, up in zip(units, ss, upd)]
    ys = [_rms_gate(o, nw, og) for o, og in zip(os_, ogs)]
    return ys, s_new


def _unit_lower_inverse_chain(a_list):
    c = a_list[0].shape[0]
    eye = (_iota((c, c), 0) == _iota((c, c), 1)).astype(F32)
    ps = [-a for a in a_list]
    ts = [eye + p for p in ps]
    for _ in range(max(c.bit_length() - 2, 0)):
        ps = [_dot(p, p, INV_PREC) for p in ps]
        ts = [t + _dot(t, p, INV_PREC) for t, p in zip(ts, ps)]
    return ts


@jax.custom_vjp
def _unit_lower_inverse(a_list):
    return _unit_lower_inverse_chain(a_list)


def _unit_lower_inverse_fwd(a_list):
    ts = _unit_lower_inverse_chain(a_list)
    return ts, ts


def _unit_lower_inverse_bwd(ts, dts):
    xs = [_dot_nt(dt, t, INV_PREC) for dt, t in zip(dts, ts)]
    return ([-_dot_tn(t, x, INV_PREC) for t, x in zip(ts, xs)],)


_unit_lower_inverse.defvjp(_unit_lower_inverse_fwd, _unit_lower_inverse_bwd)


@jax.custom_vjp
def _unit_lower_inverse_known(a_list, ts):
    return ts


def _unit_lower_inverse_known_fwd(a_list, ts):
    return ts, ts


def _unit_lower_inverse_known_bwd(ts, dts):
    return _unit_lower_inverse_bwd(ts, dts) + ([jnp.zeros_like(t) for t in ts],)


_unit_lower_inverse_known.defvjp(_unit_lower_inverse_known_fwd, _unit_lower_inverse_known_bwd)


def _gdn_prep_units(qs, ks, vs, gbs, t_known=None):
    c = qs[0].shape[0]
    units = [divmod(i, GDN_HEADS) for i in range(len(qs))]
    row, col = _iota((c, c), 0), _iota((c, c), 1)
    causal, strict = row >= col, row > col
    lane = _iota((1, 128), 1)
    d_alls = [_dot(causal.astype(F32), gb, HI) for gb in gbs]
    g_c, beta_c, d_c = [], [], []
    for r, h in units:
        sel_a = (lane == LANE_A + h).astype(F32)
        g_c.append(jnp.sum(gbs[r] * sel_a, axis=-1, keepdims=True))
        beta_c.append(jnp.sum(gbs[r] * (lane == LANE_B + h).astype(F32), axis=-1, keepdims=True))
        d_c.append(jnp.sum(d_alls[r] * sel_a, axis=-1, keepdims=True))
    d_last = [jnp.sum(g, axis=0, keepdims=True) for g in g_c]
    d_diff = [jnp.broadcast_to(d, (c, c)) - jnp.broadcast_to(d, (c, c)).T for d in d_c]
    decay_mat = [jnp.where(causal, jnp.exp(jnp.where(causal, dd, 0.0)), 0.0) for dd in d_diff]
    kb = [k * b for k, b in zip(ks, beta_c)]
    a = [jnp.where(strict, _dot_nt(kbi, k) * dm, 0.0) for kbi, k, dm in zip(kb, ks, decay_mat)]
    t = _unit_lower_inverse(a) if t_known is None else _unit_lower_inverse_known(a, t_known)
    u = [_dot(ti, v * b) for ti, v, b in zip(t, vs, beta_c)]
    w = [_dot(ti, kbi * jnp.exp(d)) for ti, kbi, d in zip(t, kb, d_c)]
    qk = [jnp.where(causal, _dot_nt(q, k) * dm, 0.0) for q, k, dm in zip(qs, ks, decay_mat)]
    q_dec = [q * jnp.exp(d) for q, d in zip(qs, d_c)]
    k_dec = [k * jnp.exp(dl - d) for k, dl, d in zip(ks, d_last, d_c)]
    gamma = [jnp.exp(dl) for dl in d_last]
    return u, w, qk, q_dec, k_dec, gamma, t


def _sum_all(t):
    return jnp.sum(jnp.sum(t, axis=-1, keepdims=True), axis=0, keepdims=True)


def _gdn_pre_elem(ps, ab, alog_v, dtb_v):
    outs = []
    for j, p in enumerate(ps):
        s = _silu(p)
        if j < 2 * GDN_HEADS:
            s = s * lax.rsqrt(jnp.sum(s * s, axis=-1, keepdims=True) + RMS_EPS)
        if j < GDN_HEADS:
            s = s * (GDN_DK ** -0.5)
        outs.append(s)
    lane = _iota((1, 128), 1)
    is_a = (lane >= LANE_A) & (lane < LANE_A + GDN_HEADS)
    is_b = (lane >= LANE_B) & (lane < LANE_B + GDN_HEADS)
    g = -jnp.exp(alog_v) * _softplus(ab + dtb_v)
    gb = jnp.where(is_a, g, jnp.where(is_b, jax.nn.sigmoid(ab), 0.0))
    return tuple(outs) + (gb,)


def _proj_fwd(x2, sc3, sh3, ws, seq, tm=256):
    n = x2.shape[0]
    tpe = seq // tm
    nw = len(ws)

    def body(x_ref, sc_ref, sh_ref, *refs):
        h = (x_ref[...] * sc_ref[0] + sh_ref[0]).astype(ws[0].dtype)
        for w_ref, o_ref in zip(refs[:nw], refs[nw:]):
            o_ref[...] = _dot_nt(h, w_ref[...])

    row = lambda i: (i, 0)
    per_ex = pl.BlockSpec((1, 1, D_MODEL), lambda i: (i // tpe, 0, 0))
    return pl.pallas_call(
        body, name="proj_fwd", grid=(n // tm,),
        in_specs=[pl.BlockSpec((tm, D_MODEL), row), per_ex, per_ex]
        + [pl.BlockSpec(w.shape, lambda i: (0, 0)) for w in ws],
        out_specs=[pl.BlockSpec((tm, w.shape[0]), row) for w in ws],
        out_shape=[jax.ShapeDtypeStruct((n, w.shape[0]), F32) for w in ws],
        compiler_params=_params("parallel"),
    )(x2, sc3, sh3, *ws)


def _gla_fwd(pa, pd, wg, bg, nw, bl, seq):
    n = pa.shape[0]
    nc = seq // CHUNK
    heads = [(e, h, slice(h * 128, (h + 1) * 128)) for e in range(bl) for h in range(GLA_HEADS)]

    def body(q_ref, k_ref, v_ref, og_ref, lr_ref, wg_ref, bg_ref, nw_ref, y_ref, st_ref, s_scr):
        @pl.when(pl.program_id(0) == 0)
        def _():
            s_scr[...] = jnp.zeros_like(s_scr)

        ss = [s_scr[e, h] for e, h, _ in heads]
        for (e, h, _), s in zip(heads, ss):
            st_ref[e, 0, h] = s
        ys, s_new = _gla_chunk([q_ref[e] for e in range(bl)], [k_ref[e] for e in range(bl)],
                               [lr_ref[e] for e in range(bl)],
                               [v_ref[e, :, cols] for e, _, cols in heads], [og_ref[e, :, cols] for e, _, cols in heads],
                               ss, wg_ref[...], bg_ref[...], nw_ref[...])
        for (e, h, cols), y, s in zip(heads, ys, s_new):
            y_ref[e, :, cols] = y
            s_scr[e, h] = s

    tok = lambda w, j: pl.BlockSpec((bl, CHUNK, w), lambda i: (0, i, j))
    const = lambda i: (0, 0)
    pa3 = pa.reshape(bl, seq, 1536)
    y, st = pl.pallas_call(
        body, name="gla_fwd", grid=(nc,),
        in_specs=[tok(256, 0), tok(256, 1), tok(512, 1), tok(512, 2), tok(128, 0),
                  pl.BlockSpec(wg.shape, const), pl.BlockSpec(bg.shape, const), pl.BlockSpec(nw.shape, const)],
        out_specs=[tok(512, 0), pl.BlockSpec((bl, 1, GLA_HEADS, 128, 256), lambda i: (0, i, 0, 0, 0))],
        out_shape=[jax.ShapeDtypeStruct((bl, seq, 512), F32),
                   jax.ShapeDtypeStruct((bl, nc, GLA_HEADS, 128, 256), F32)],
        scratch_shapes=[pltpu.VMEM((bl, GLA_HEADS, 128, 256), F32)],
        compiler_params=_params("arbitrary"),
    )(pa3, pa3, pa3, pa3, pd.reshape(bl, seq, 128), wg, bg, nw)
    return y.reshape(n, 512), st


def _gla_bwd(pa, pd, st, dya, wg, bg, nw, bl, seq):
    n = pa.shape[0]
    nc = seq // CHUNK
    heads = [(e, h, slice(h * 128, (h + 1) * 128)) for e in range(bl) for h in range(GLA_HEADS)]

    def body(q_ref, k_ref, v_ref, og_ref, lr_ref, st_ref, dy_ref, wg_ref, bg_ref, nw_ref,
             da_ref, dd_ref, dwg_ref, dbg_ref, dnw_ref, ds_scr):
        @pl.when(pl.program_id(0) == 0)
        def _():
            dwg_ref[...] = jnp.zeros_like(dwg_ref)
            dbg_ref[...] = jnp.zeros_like(dbg_ref)
            dnw_ref[...] = jnp.zeros_like(dnw_ref)
            ds_scr[...] = jnp.zeros_like(ds_scr)

        _, vjp = jax.vjp(_gla_chunk, [q_ref[e] for e in range(bl)], [k_ref[e] for e in range(bl)],
                         [lr_ref[e] for e in range(bl)],
                         [v_ref[e, :, cols] for e, _, cols in heads], [og_ref[e, :, cols] for e, _, cols in heads],
                         [st_ref[e, 0, h] for e, h, _ in heads], wg_ref[...], bg_ref[...], nw_ref[...])
        dq, dk, dlr, dv, dog, ds, dwg, dbg, dnw = vjp(([dy_ref[e, :, cols] for e, _, cols in heads],
                                                         [ds_scr[e, h] for e, h, _ in heads]))
        for e in range(bl):
            da_ref[e, :, 0:256] = dq[e]
            da_ref[e, :, 256:512] = dk[e]
            dd_ref[e] = dlr[e]
        for i, (e, h, _) in enumerate(heads):
            da_ref[e, :, 512 + h * 128:512 + (h + 1) * 128] = dv[i]
            da_ref[e, :, 1024 + h * 128:1024 + (h + 1) * 128] = dog[i]
            ds_scr[e, h] = ds[i]
        dwg_ref[...] += dwg
        dbg_ref[...] += dbg
        dnw_ref[...] += dnw

    tok = lambda w, j: pl.BlockSpec((bl, CHUNK, w), lambda i: (0, nc - 1 - i, j))
    const = lambda i: (0, 0)
    pa3 = pa.reshape(bl, seq, 1536)
    da, dd, dwg, dbg, dnw = pl.pallas_call(
        body, name="gla_bwd", grid=(nc,),
        in_specs=[tok(256, 0), tok(256, 1), tok(512, 1), tok(512, 2), tok(128, 0),
                  pl.BlockSpec((bl, 1, GLA_HEADS, 128, 256), lambda i: (0, nc - 1 - i, 0, 0, 0)), tok(512, 0),
                  pl.BlockSpec(wg.shape, const), pl.BlockSpec(bg.shape, const), pl.BlockSpec(nw.shape, const)],
        out_specs=[tok(1536, 0), tok(128, 0),
                   pl.BlockSpec(wg.shape, const), pl.BlockSpec(bg.shape, const), pl.BlockSpec(nw.shape, const)],
        out_shape=[jax.ShapeDtypeStruct((bl, seq, 1536), F32), jax.ShapeDtypeStruct((bl, seq, 128), F32),
                   jax.ShapeDtypeStruct(wg.shape, F32), jax.ShapeDtypeStruct(bg.shape, F32),
                   jax.ShapeDtypeStruct(nw.shape, F32)],
        scratch_shapes=[pltpu.VMEM((bl, GLA_HEADS, 128, 256), F32)],
        compiler_params=_params("arbitrary"),
    )(pa3, pa3, pa3, pa3, pd.reshape(bl, seq, 128), st, dya.reshape(bl, seq, 512), wg, bg, nw)
    return da.reshape(n, 1536), dd.reshape(n, 128), dwg, dbg, dnw


def _conv_taps(buf_ref, w_ref, base, rows):
    acc = w_ref[0:1, :] * buf_ref[pl.ds(base, rows), :]
    for k in range(1, CONV_K):
        acc = acc + w_ref[k:k + 1, :] * buf_ref[pl.ds(base + k, rows), :]
    return acc


def _gdn_pre_fwd(pb, pd, cw8, alog_v, dtb_v, bl, seq, tm=256):
    n = pb.shape[0]
    tpe = seq // tm
    t8 = tm // 8

    def body(u_ref, prev_ref, ab_ref, w_ref, al_ref, dt_ref, qkv_ref, gb_ref, p_ref, buf):
        i = pl.program_id(0)
        keep = (i % tpe != 0).astype(F32)
        buf[0:8, :] = prev_ref[...] * keep
        buf[8:8 + tm, :] = u_ref[...]
        p = _conv_taps(buf, w_ref, 8 - (CONV_K - 1), tm)
        p_ref[...] = p
        ps = [p[:, j * 128:(j + 1) * 128] for j in range(12)]
        outs = _gdn_pre_elem(ps, ab_ref[...], al_ref[...], dt_ref[...])
        for j in range(12):
            qkv_ref[:, j * 128:(j + 1) * 128] = outs[j]
        gb_ref[...] = outs[12]

    row = lambda i: (i, 0)
    const = lambda i: (0, 0)
    return pl.pallas_call(
        body, name="gdn_pre_fwd", grid=(n // tm,),
        in_specs=[pl.BlockSpec((tm, 1536), row),
                  pl.BlockSpec((8, 1536), lambda i: (jnp.maximum(i * t8 - 1, 0), 0)),
                  pl.BlockSpec((tm, 128), row),
                  pl.BlockSpec((8, 1536), const), pl.BlockSpec((1, 128), const), pl.BlockSpec((1, 128), const)],
        out_specs=[pl.BlockSpec((tm, 1536), row), pl.BlockSpec((tm, 128), row), pl.BlockSpec((tm, 1536), row)],
        out_shape=[jax.ShapeDtypeStruct((n, 1536), F32), jax.ShapeDtypeStruct((n, 128), F32),
                   jax.ShapeDtypeStruct((n, 1536), F32)],
        scratch_shapes=[pltpu.VMEM((tm + 8, 1536), F32)],
        compiler_params=_params("parallel"),
    )(pb, pb, pd, cw8, alog_v, dtb_v)


def _gdn_pre_bwd(pb, conv_out, pd, dqkv, dgb, cw8, alog_v, dtb_v, bl, seq, tm=256):
    n = pb.shape[0]
    tpe = seq // tm
    t8 = tm // 8
    nb8 = n // 8
    ext = tm + 8

    def body(u_ref, p_ref, pn_ref, ab_ref, abn_ref, dq_ref, dqn_ref, dgb_ref, w_ref, al_ref, dt_ref,
             du_ref, dab_ref, dw_ref, dal_ref, ddt_ref, dpbuf):
        i = pl.program_id(0)

        @pl.when(i == 0)
        def _():
            dw_ref[...] = jnp.zeros_like(dw_ref)
            dal_ref[...] = jnp.zeros_like(dal_ref)
            ddt_ref[...] = jnp.zeros_like(ddt_ref)

        keep_next = (i % tpe != tpe - 1).astype(F32)
        ps = [jnp.concatenate([p_ref[:, j * 128:(j + 1) * 128], pn_ref[:, j * 128:(j + 1) * 128]], axis=0)
              for j in range(12)]
        ab = jnp.concatenate([ab_ref[...], abn_ref[...]], axis=0)
        _, vjp = jax.vjp(_gdn_pre_elem, ps, ab, al_ref[...], dt_ref[...])
        zeros8 = jnp.zeros((8, 128), F32)
        cts = tuple(jnp.concatenate([dq_ref[:, j * 128:(j + 1) * 128],
                                     dqn_ref[:, j * 128:(j + 1) * 128] * keep_next], axis=0) for j in range(12))
        cts += (jnp.concatenate([dgb_ref[...], zeros8], axis=0),)
        dps, dab, dal, ddt = vjp(cts)
        for j in range(12):
            dpbuf[:, j * 128:(j + 1) * 128] = dps[j]
        dab_ref[...] = dab[0:tm, :]
        dal_ref[...] += dal
        ddt_ref[...] += ddt
        u = u_ref[...]
        du = None
        for k in range(CONV_K):
            dp_k = dpbuf[pl.ds(CONV_K - 1 - k, tm), :]
            term = w_ref[k:k + 1, :] * dp_k
            du = term if du is None else du + term
            dw_ref[k:k + 1, :] += jnp.sum(u * dp_k, axis=0, keepdims=True)
        du_ref[...] = du

    row = lambda i: (i, 0)
    next8 = lambda i: (jnp.minimum((i + 1) * t8, nb8 - 1), 0)
    const = lambda i: (0, 0)
    return pl.pallas_call(
        body, name="gdn_pre_bwd", grid=(n // tm,),
        in_specs=[pl.BlockSpec((tm, 1536), row), pl.BlockSpec((tm, 1536), row), pl.BlockSpec((8, 1536), next8),
                  pl.BlockSpec((tm, 128), row), pl.BlockSpec((8, 128), next8),
                  pl.BlockSpec((tm, 1536), row), pl.BlockSpec((8, 1536), next8),
                  pl.BlockSpec((tm, 128), row),
                  pl.BlockSpec((8, 1536), const), pl.BlockSpec((1, 128), const), pl.BlockSpec((1, 128), const)],
        out_specs=[pl.BlockSpec((tm, 1536), row), pl.BlockSpec((tm, 128), row),
                   pl.BlockSpec((8, 1536), const), pl.BlockSpec((1, 128), const), pl.BlockSpec((1, 128), const)],
        out_shape=[jax.ShapeDtypeStruct((n, 1536), F32), jax.ShapeDtypeStruct((n, 128), F32),
                   jax.ShapeDtypeStruct((8, 1536), F32), jax.ShapeDtypeStruct((1, 128), F32),
                   jax.ShapeDtypeStruct((1, 128), F32)],
        scratch_shapes=[pltpu.VMEM((ext, 1536), F32)],
        compiler_params=_params("arbitrary"),
    )(pb, conv_out, conv_out, pd, pd, dqkv, dqkv, dgb, cw8, alog_v, dtb_v)


GDN_PREP_CHUNKS = 2


def _head_cols(ref, rows, base=0):
    return [ref[rows, base + h * 128:base + (h + 1) * 128] for h in range(GDN_HEADS)]


def _gdn_prep(qkv, gb):
    n = qkv.shape[0]
    r_per = GDN_PREP_CHUNKS
    tm = r_per * CHUNK

    def body(q_ref, k_ref, v_ref, gb_ref, u_ref, w_ref, qd_ref, kd_ref, qk_ref, t_ref, gam_ref):
        rowid = _iota((8, 128), 0)
        chunk_rows = [slice(r * CHUNK, (r + 1) * CHUNK) for r in range(r_per)]
        gather = lambda ref: [t for rows in chunk_rows for t in _head_cols(ref, rows)]
        u, w, qk, qd, kd, gamma, tinv = _gdn_prep_units(gather(q_ref), gather(k_ref), gather(v_ref),
                                                        [gb_ref[rows, :] for rows in chunk_rows])
        for r, rows in enumerate(chunk_rows):
            gam = jnp.zeros((8, 128), F32)
            for h in range(GDN_HEADS):
                i = r * GDN_HEADS + h
                cols = slice(h * 128, (h + 1) * 128)
                u_ref[rows, cols] = u[i]
                w_ref[rows, cols] = w[i]
                qd_ref[rows, cols] = qd[i]
                kd_ref[rows, cols] = kd[i]
                qk_ref[r, h] = qk[i]
                t_ref[r, h] = tinv[i]
                gam = jnp.where(rowid == h, gamma[i], gam)
            gam_ref[r] = gam

    tok = lambda j: pl.BlockSpec((tm, 512), lambda i: (i, j))
    return pl.pallas_call(
        body, name="gdn_prep", grid=(n // tm,),
        in_specs=[tok(0), tok(1), tok(2), pl.BlockSpec((tm, 128), lambda i: (i, 0))],
        out_specs=[tok(0)] * 4 + [pl.BlockSpec((r_per, GDN_HEADS, CHUNK, CHUNK), lambda i: (i, 0, 0, 0))] * 2
        + [pl.BlockSpec((r_per, 8, 128), lambda i: (i, 0, 0))],
        out_shape=[jax.ShapeDtypeStruct((n, 512), F32)] * 4
        + [jax.ShapeDtypeStruct((n // CHUNK, GDN_HEADS, CHUNK, CHUNK), F32)] * 2
        + [jax.ShapeDtypeStruct((n // CHUNK, 8, 128), F32)],
        compiler_params=_params("parallel"),
    )(qkv, qkv, qkv, gb)


def _gdn_fwd(qkv, gb, pc, nw, bl, seq):
    n = qkv.shape[0]
    nc = seq // CHUNK
    u, w, qd, kd, qk, tinv, gam = _gdn_prep(qkv, gb)
    tok3 = lambda t: t.reshape(bl, seq, 512)
    qk5 = qk.reshape(bl, nc, GDN_HEADS, CHUNK, CHUNK)
    gam4 = gam.reshape(bl, nc, 8, 128)

    def body(u_ref, w_ref, qd_ref, kd_ref, qk_ref, gam_ref, og_ref, nw_ref, o_ref, y_ref, st_ref, s_scr):
        @pl.when(pl.program_id(0) == 0)
        def _():
            s_scr[...] = jnp.zeros_like(s_scr)

        units = [(b, h, slice(h * 128, (h + 1) * 128)) for b in range(bl) for h in range(GDN_HEADS)]
        ss = [s_scr[b, h] for b, h, _ in units]
        for (b, h, _), s in zip(units, ss):
            st_ref[b, 0, h] = s
        v_new = [u_ref[b, :, cols] - _dot(w_ref[b, :, cols], s) for (b, h, cols), s in zip(units, ss)]
        o_inter = [_dot(qd_ref[b, :, cols], s) for (b, h, cols), s in zip(units, ss)]
        os_ = [oi + _dot(qk_ref[b, 0, h], vn) for (b, h, cols), oi, vn in zip(units, o_inter, v_new)]
        for (b, h, cols), s, vn in zip(units, ss, v_new):
            s_scr[b, h] = s * gam_ref[b, 0, h:h + 1, :] + _dot_tn(kd_ref[b, :, cols], vn)
        for (b, h, cols), o in zip(units, os_):
            o_ref[b, :, cols] = o
            y_ref[b, :, cols] = _rms_gate(o, nw_ref[...], og_ref[b, :, cols])

    tok = pl.BlockSpec((bl, CHUNK, 512), lambda i: (0, i, 0))
    st_spec = pl.BlockSpec((bl, 1, GDN_HEADS, 128, 128), lambda i: (0, i, 0, 0, 0))
    o, y, st = pl.pallas_call(
        body, name="gdn_scan_fwd", grid=(nc,),
        in_specs=[tok, tok, tok, tok,
                  pl.BlockSpec((bl, 1, GDN_HEADS, CHUNK, CHUNK), lambda i: (0, i, 0, 0, 0)),
                  pl.BlockSpec((bl, 1, 8, 128), lambda i: (0, i, 0, 0)), tok,
                  pl.BlockSpec(nw.shape, lambda i: (0, 0))],
        out_specs=[tok, tok, st_spec],
        out_shape=[jax.ShapeDtypeStruct((bl, seq, 512), F32), jax.ShapeDtypeStruct((bl, seq, 512), F32),
                   jax.ShapeDtypeStruct((bl, nc, GDN_HEADS, 128, 128), F32)],
        scratch_shapes=[pltpu.VMEM((bl, GDN_HEADS, 128, 128), F32)],
        compiler_params=_params("arbitrary"),
    )(tok3(u), tok3(w), tok3(qd), tok3(kd), qk5, gam4, tok3(pc), nw)
    return y.reshape(n, 512), (o, st, w, qd, kd, qk5, gam4, tinv)


def _gdn_bwd(qkv, gb, pc, res, dyb, nw, bl, seq):
    n = qkv.shape[0]
    nc = seq // CHUNK
    o, st, w, qd, kd, qk5, gam4, tinv = res
    tok3 = lambda t: t.reshape(bl, seq, 512)

    def scan_body(dy_ref, o_ref, og_ref, w_ref, qd_ref, kd_ref, qk_ref, gam_ref, nw_ref,
                  do_ref, dog_ref, dvn_ref, dst_ref, dnw_ref, ds_scr):
        @pl.when(pl.program_id(0) == 0)
        def _():
            ds_scr[...] = jnp.zeros_like(ds_scr)
            dnw_ref[...] = jnp.zeros_like(dnw_ref)

        units = [(b, h, slice(h * 128, (h + 1) * 128)) for b in range(bl) for h in range(GDN_HEADS)]
        dnw = jnp.zeros(nw.shape, F32)
        d_os = []
        for b, h, cols in units:
            _, vjp = jax.vjp(_rms_gate, o_ref[b, :, cols], nw_ref[...], og_ref[b, :, cols])
            d_o, dnw_h, dog = vjp(dy_ref[b, :, cols])
            do_ref[b, :, cols] = d_o
            dog_ref[b, :, cols] = dog
            dnw = dnw + dnw_h
            d_os.append(d_o)
        dnw_ref[...] += dnw
        dss = [ds_scr[b, h] for b, h, _ in units]
        for (b, h, _), ds in zip(units, dss):
            dst_ref[b, 0, h] = ds
        dvn_a = [_dot(kd_ref[b, :, cols], ds) for (b, h, cols), ds in zip(units, dss)]
        dvns = [a + _dot_tn(qk_ref[b, 0, h], d_o) for (b, h, cols), a, d_o in zip(units, dvn_a, d_os)]
        ds_a = [_dot_tn(qd_ref[b, :, cols], d_o) + ds * gam_ref[b, 0, h:h + 1, :]
                for (b, h, cols), d_o, ds in zip(units, d_os, dss)]
        for (b, h, cols), a, dvn in zip(units, ds_a, dvns):
            dvn_ref[b, :, cols] = dvn
            ds_scr[b, h] = a - _dot_tn(w_ref[b, :, cols], dvn)

    rev = lambda i: nc - 1 - i
    tok = pl.BlockSpec((bl, CHUNK, 512), lambda i: (0, rev(i), 0))
    st_spec = pl.BlockSpec((bl, 1, GDN_HEADS, 128, 128), lambda i: (0, rev(i), 0, 0, 0))
    tok_shape = jax.ShapeDtypeStruct((bl, seq, 512), F32)
    d_o, dog, dvn, dst, dnw = pl.pallas_call(
        scan_body, name="gdn_scan_bwd", grid=(nc,),
        in_specs=[tok] * 6 + [pl.BlockSpec((bl, 1, GDN_HEADS, CHUNK, CHUNK), lambda i: (0, rev(i), 0, 0, 0)),
                              pl.BlockSpec((bl, 1, 8, 128), lambda i: (0, rev(i), 0, 0)),
                              pl.BlockSpec(nw.shape, lambda i: (0, 0))],
        out_specs=[tok, tok, tok, st_spec, pl.BlockSpec(nw.shape, lambda i: (0, 0))],
        out_shape=[tok_shape, tok_shape, tok_shape, jax.ShapeDtypeStruct(st.shape, F32),
                   jax.ShapeDtypeStruct(nw.shape, F32)],
        scratch_shapes=[pltpu.VMEM((bl, GDN_HEADS, 128, 128), F32)],
        compiler_params=_params("arbitrary"),
    )(tok3(dyb), o, tok3(pc), tok3(w), tok3(qd), tok3(kd), qk5, gam4, nw)

    r_per = GDN_PREP_CHUNKS
    tm = r_per * CHUNK

    def prep_body(q_ref, k_ref, v_ref, gb_ref, t_ref, st_ref, dst_ref, dvn_ref, do_ref, dqkv_ref, dgb_ref):
        chunk_rows = [slice(r * CHUNK, (r + 1) * CHUNK) for r in range(r_per)]
        gather = lambda ref: [t for rows in chunk_rows for t in _head_cols(ref, rows)]
        units = [(r, h) for r in range(r_per) for h in range(GDN_HEADS)]
        t_known = [t_ref[r, h] for r, h in units]
        prep = lambda q, k, v, g: _gdn_prep_units(q, k, v, g, t_known)[:6]
        (u, w_, _, _, _, _), vjp = jax.vjp(prep, gather(q_ref), gather(k_ref), gather(v_ref),
                                           [gb_ref[rows, :] for rows in chunk_rows])
        ss = [st_ref[r, h] for r, h in units]
        dss = [dst_ref[r, h] for r, h in units]
        dvns, d_os = gather(dvn_ref), gather(do_ref)
        v_new = [ui - _dot(wi, s) for ui, wi, s in zip(u, w_, ss)]
        d_w = [-_dot_nt(dvn, s) for dvn, s in zip(dvns, ss)]
        d_qk = [_dot_nt(d_o, vn) for d_o, vn in zip(d_os, v_new)]
        d_qd = [_dot_nt(d_o, s) for d_o, s in zip(d_os, ss)]
        d_kd = [_dot_nt(vn, ds) for vn, ds in zip(v_new, dss)]
        d_gam = [_sum_all(ds * s) for ds, s in zip(dss, ss)]
        dq, dk, dv, dgb = vjp((dvns, d_w, d_qk, d_qd, d_kd, d_gam))
        for i, (r, h) in enumerate(units):
            rows = chunk_rows[r]
            for part, d in enumerate((dq, dk, dv)):
                dqkv_ref[rows, part * 512 + h * 128:part * 512 + (h + 1) * 128] = d[i]
        for r, rows in enumerate(chunk_rows):
            dgb_ref[rows, :] = dgb[r]

    tokp = lambda j: pl.BlockSpec((tm, 512), lambda i: (i, j))
    st4 = pl.BlockSpec((r_per, GDN_HEADS, 128, 128), lambda i: (i, 0, 0, 0))
    dqkv, dgb = pl.pallas_call(
        prep_body, name="gdn_prep_bwd", grid=(n // tm,),
        in_specs=[tokp(0), tokp(1), tokp(2), pl.BlockSpec((tm, 128), lambda i: (i, 0)),
                  pl.BlockSpec((r_per, GDN_HEADS, CHUNK, CHUNK), lambda i: (i, 0, 0, 0)), st4, st4, tokp(0), tokp(0)],
        out_specs=[pl.BlockSpec((tm, 1536), lambda i: (i, 0)), pl.BlockSpec((tm, 128), lambda i: (i, 0))],
        out_shape=[jax.ShapeDtypeStruct((n, 1536), F32), jax.ShapeDtypeStruct((n, 128), F32)],
        compiler_params=_params("parallel"),
    )(qkv, qkv, qkv, gb, tinv, st.reshape(bl * nc, GDN_HEADS, 128, 128), dst.reshape(bl * nc, GDN_HEADS, 128, 128),
      dvn.reshape(n, 512), d_o.reshape(n, 512))
    return dqkv, dog.reshape(n, 512), dgb, dnw


def _out_block(x2, tgt2, ya, yb, g1p3, wo, lnw, lnb, seq, tm=256):
    n = x2.shape[0]
    tpe = seq // tm
    bl = n // seq

    def body(x_ref, t_ref, ya_ref, yb_ref, g_ref, wo_ref, lnw_ref, lnb_ref,
             dz_ref, dya_ref, dyb_ref, dwo_ref, dg_ref, glw_ref, glb_ref, loss_ref):
        i = pl.program_id(0)

        @pl.when(i == 0)
        def _():
            dwo_ref[...] = jnp.zeros_like(dwo_ref)
            glw_ref[...] = jnp.zeros_like(glw_ref)
            glb_ref[...] = jnp.zeros_like(glb_ref)
            loss_ref[...] = jnp.zeros_like(loss_ref)

        @pl.when(i % tpe == 0)
        def _():
            dg_ref[...] = jnp.zeros_like(dg_ref)

        ya16 = ya_ref[...].astype(wo.dtype)
        yb16 = yb_ref[...].astype(wo.dtype)
        wa = wo_ref[0:GLA_WIDTH, :]
        wb = wo_ref[GLA_WIDTH:, :]
        y = _dot(ya16, wa) + _dot(yb16, wb)
        g1p = g_ref[0]
        z = ALPHA * x_ref[...] + g1p * y
        mu = jnp.mean(z, axis=-1, keepdims=True)
        zc = z - mu
        rstd = lax.rsqrt(jnp.mean(zc * zc, axis=-1, keepdims=True) + LN_EPS)
        xhat = zc * rstd
        diff = xhat * lnw_ref[...] + lnb_ref[...] - t_ref[...]
        loss_ref[...] += (0.5 / D_MODEL) * jnp.sum(jnp.sum(diff * diff, axis=-1, keepdims=True), axis=0, keepdims=True)
        dout = diff * (1.0 / D_MODEL)
        glw_ref[...] += jnp.sum(dout * xhat, axis=0, keepdims=True)
        glb_ref[...] += jnp.sum(dout, axis=0, keepdims=True)
        dxh = dout * lnw_ref[...]
        dz = rstd * (dxh - jnp.mean(dxh, axis=-1, keepdims=True)
                     - xhat * jnp.mean(dxh * xhat, axis=-1, keepdims=True))
        dz_ref[...] = dz
        dg_ref[0] += jnp.sum(dz * y, axis=0, keepdims=True)
        dy = (g1p * dz).astype(wo.dtype)
        dya_ref[...] = _dot_nt(dy, wa)
        dyb_ref[...] = _dot_nt(dy, wb)
        dwo_ref[0:GLA_WIDTH, :] += _dot_tn(ya16, dy)
        dwo_ref[GLA_WIDTH:, :] += _dot_tn(yb16, dy)

    row = lambda i: (i, 0)
    const = lambda i: (0, 0)
    per_ex = pl.BlockSpec((1, 1, D_MODEL), lambda i: (i // tpe, 0, 0))
    return pl.pallas_call(
        body, name="out_block", grid=(n // tm,),
        in_specs=[pl.BlockSpec((tm, D_MODEL), row), pl.BlockSpec((tm, D_MODEL), row),
                  pl.BlockSpec((tm, 512), row), pl.BlockSpec((tm, 512), row), per_ex,
                  pl.BlockSpec((D_MODEL, D_MODEL), const), pl.BlockSpec((1, D_MODEL), const),
                  pl.BlockSpec((1, D_MODEL), const)],
        out_specs=[pl.BlockSpec((tm, D_MODEL), row), pl.BlockSpec((tm, 512), row), pl.BlockSpec((tm, 512), row),
                   pl.BlockSpec((D_MODEL, D_MODEL), const), per_ex,
                   pl.BlockSpec((1, D_MODEL), const), pl.BlockSpec((1, D_MODEL), const),
                   pl.BlockSpec((1, 1), const)],
        out_shape=[jax.ShapeDtypeStruct((n, D_MODEL), F32), jax.ShapeDtypeStruct((n, 512), F32),
                   jax.ShapeDtypeStruct((n, 512), F32), jax.ShapeDtypeStruct((D_MODEL, D_MODEL), F32),
                   jax.ShapeDtypeStruct((bl, 1, D_MODEL), F32), jax.ShapeDtypeStruct((1, D_MODEL), F32),
                   jax.ShapeDtypeStruct((1, D_MODEL), F32), jax.ShapeDtypeStruct((1, 1), F32)],
        compiler_params=_params("arbitrary"),
    )(x2, tgt2, ya, yb, g1p3, wo, lnw, lnb)


def _proj_bwd_x(ds, ws, x2, dz, sc3, seq, tm=256):
    n = x2.shape[0]
    tpe = seq // tm
    bl = n // seq

    def body(da_ref, db_ref, dc_ref, dd1_ref, dd2_ref, wa_ref, wb_ref, wc_ref, wd_ref, x_ref, dz_ref, sc_ref,
             gx_ref, dsh_ref, dsc_ref):
        i = pl.program_id(0)

        @pl.when(i % tpe == 0)
        def _():
            dsh_ref[...] = jnp.zeros_like(dsh_ref)
            dsc_ref[...] = jnp.zeros_like(dsc_ref)

        cdt = ws[0].dtype
        dh = _dot(da_ref[...].astype(cdt), wa_ref[...])
        dh += _dot(db_ref[...].astype(cdt), wb_ref[...])
        dh += _dot(dc_ref[...].astype(cdt), wc_ref[...])
        dh += _dot((dd1_ref[...] + dd2_ref[...]).astype(cdt), wd_ref[...])
        gx_ref[...] = dh * sc_ref[0] + ALPHA * dz_ref[...]
        dsh_ref[0] += jnp.sum(dh, axis=0, keepdims=True)
        dsc_ref[0] += jnp.sum(dh * x_ref[...], axis=0, keepdims=True)

    row = lambda i: (i, 0)
    const = lambda i: (0, 0)
    per_ex = pl.BlockSpec((1, 1, D_MODEL), lambda i: (i // tpe, 0, 0))
    da, db, dc, (dd1, dd2) = ds
    return pl.pallas_call(
        body, name="proj_bwd_x", grid=(n // tm,),
        in_specs=[pl.BlockSpec((tm, d.shape[1]), row) for d in (da, db, dc, dd1, dd2)]
        + [pl.BlockSpec(w.shape, const) for w in ws]
        + [pl.BlockSpec((tm, D_MODEL), row), pl.BlockSpec((tm, D_MODEL), row), per_ex],
        out_specs=[pl.BlockSpec((tm, D_MODEL), row), per_ex, per_ex],
        out_shape=[jax.ShapeDtypeStruct((n, D_MODEL), F32), jax.ShapeDtypeStruct((bl, 1, D_MODEL), F32),
                   jax.ShapeDtypeStruct((bl, 1, D_MODEL), F32)],
        compiler_params=_params("arbitrary"),
    )(da, db, dc, dd1, dd2, *ws, x2, dz, sc3)


def _proj_bwd_w(x2, sc3, sh3, ds, seq, cdt, name, tm=256):
    n = x2.shape[0]
    tpe = seq // tm
    flat, groups = [], []
    for d in ds:
        parts = d if isinstance(d, tuple) else (d,)
        groups.append(len(parts))
        flat.extend(parts)
    nin = len(flat)

    def body(x_ref, sc_ref, sh_ref, *refs):
        i = pl.program_id(0)
        outs = refs[nin:]

        @pl.when(i == 0)
        def _():
            for o in outs:
                o[...] = jnp.zeros_like(o)

        h = (x_ref[...] * sc_ref[0] + sh_ref[0]).astype(cdt)
        pos = 0
        for o, cnt in zip(outs, groups):
            d = refs[pos][...]
            for extra in refs[pos + 1:pos + cnt]:
                d = d + extra[...]
            pos += cnt
            o[...] += _dot_tn(d.astype(cdt), h)

    row = lambda i: (i, 0)
    const = lambda i: (0, 0)
    per_ex = pl.BlockSpec((1, 1, D_MODEL), lambda i: (i // tpe, 0, 0))
    widths = [(d[0] if isinstance(d, tuple) else d).shape[1] for d in ds]
    return pl.pallas_call(
        body, name=name, grid=(n // tm,),
        in_specs=[pl.BlockSpec((tm, D_MODEL), row), per_ex, per_ex]
        + [pl.BlockSpec((tm, d.shape[1]), row) for d in flat],
        out_specs=[pl.BlockSpec((w, D_MODEL), const) for w in widths],
        out_shape=[jax.ShapeDtypeStruct((w, D_MODEL), F32) for w in widths],
        compiler_params=_params("arbitrary"),
    )(x2, sc3, sh3, *flat)


def _mod_block(c_all, w_ada_sh, b_blk):
    def body(c_ref, w_ref, b_ref, o_ref):
        o_ref[...] = _dot(c_ref[...], w_ref[...]) + b_ref[...]

    return pl.pallas_call(
        body, name="mod_block",
        out_shape=jax.ShapeDtypeStruct((c_all.shape[0], w_ada_sh.shape[1]), F32),
        compiler_params=pltpu.CompilerParams(vmem_limit_bytes=VMEM_LIMIT),
    )(c_all, w_ada_sh, b_blk)


def _ada_grads(c_all, dmod_all, dmod_blk):
    def body(c_ref, da_ref, db_ref, gw_ref, gb_ref):
        gw_ref[...] = _dot_tn(c_ref[...], db_ref[...])
        gb_ref[...] = jnp.sum(da_ref[...], axis=0, keepdims=True)

    return pl.pallas_call(
        body, name="ada_grads",
        out_shape=[jax.ShapeDtypeStruct((c_all.shape[1], dmod_blk.shape[1]), F32),
                   jax.ShapeDtypeStruct((1, dmod_all.shape[1]), F32)],
        compiler_params=pltpu.CompilerParams(vmem_limit_bytes=VMEM_LIMIT),
    )(c_all, dmod_all, dmod_blk)


def _sum_leading(parts, name):
    def body(p_ref, o_ref):
        acc = p_ref[0]
        for d in range(1, parts.shape[0]):
            acc = acc + p_ref[d]
        o_ref[...] = acc

    return pl.pallas_call(
        body, name=name, out_shape=jax.ShapeDtypeStruct(parts.shape[1:], F32),
        compiler_params=pltpu.CompilerParams(vmem_limit_bytes=VMEM_LIMIT),
    )(parts)


ELEMENTWISE_BLOCK_BYTES = 2 * 1024 * 1024


def _tile2d(rows, cols, row_align=8):
    if rows * cols * 4 <= ELEMENTWISE_BLOCK_BYTES:
        return rows, cols
    fits = [t for t in range(row_align, rows, row_align) if rows % t == 0 and t * cols * 4 <= ELEMENTWISE_BLOCK_BYTES]
    if fits:
        return fits[-1], cols
    fits = [t for t in range(128, cols, 128) if cols % t == 0 and rows * t * 4 <= ELEMENTWISE_BLOCK_BYTES]
    assert fits, (rows, cols)
    return rows, fits[-1]


def _add_n(arrs, name, out_dtypes=(F32,)):
    rows, cols = arrs[0].shape
    narrow = any(jnp.dtype(dt).itemsize < 4 for dt in tuple(out_dtypes) + tuple(a.dtype for a in arrs))
    tr, tc = _tile2d(rows, cols, 16 if narrow else 8)
    n_in = len(arrs)

    def body(*refs):
        acc = refs[0][...].astype(F32)
        for r in refs[1:n_in]:
            acc = acc + r[...].astype(F32)
        for o in refs[n_in:]:
            o[...] = acc.astype(o.dtype)

    spec = pl.BlockSpec((tr, tc), lambda i, j: (i, j))
    return pl.pallas_call(
        body, name=name, grid=(rows // tr, cols // tc), in_specs=[spec] * n_in, out_specs=[spec] * len(out_dtypes),
        out_shape=[jax.ShapeDtypeStruct((rows, cols), dt) for dt in out_dtypes],
        compiler_params=_params("parallel", "parallel"),
    )(*arrs)


def _adamw(w, g, m, v, name):
    rows, cols = w.shape
    tr, tc = _tile2d(rows, cols)
    c1 = 1.0 / (1.0 - ADAM_B1 ** ADAM_STEP)
    c2 = 1.0 / (1.0 - ADAM_B2 ** ADAM_STEP)

    def body(w_ref, g_ref, m_ref, v_ref, d_ref, nm_ref, nv_ref):
        gg = g_ref[...]
        nm = ADAM_B1 * m_ref[...] + (1.0 - ADAM_B1) * gg
        nv = ADAM_B2 * v_ref[...] + (1.0 - ADAM_B2) * (gg * gg)
        nm_ref[...] = nm
        nv_ref[...] = nv
        d_ref[...] = -ADAM_LR * ((nm * c1) / (jnp.sqrt(nv * c2) + ADAM_EPS) + ADAM_WD * w_ref[...])

    spec = pl.BlockSpec((tr, tc), lambda i, j: (i, j))
    shp = jax.ShapeDtypeStruct((rows, cols), F32)
    return pl.pallas_call(
        body, name=name, grid=(rows // tr, cols // tc), in_specs=[spec] * 4, out_specs=[spec] * 3,
        out_shape=[shp, shp, shp], compiler_params=_params("parallel", "parallel"),
    )(w, g, m, v)


def _coords():
    return lax.axis_index("x"), lax.axis_index("y"), lax.axis_index("c")


def _all_gather8(blk, name):
    m_per, n = blk.shape

    def body(x_ref, out_ref, send_sems, recv_sems, local_sem):
        x, y, c = _coords()
        me, sibling = (x, y, c), (x, y, 1 - c)
        chips = [(1 - x, y), (x, 1 - y), (1 - x, 1 - y)]

        def rows(px, py, pc):
            return out_ref.at[pl.ds((4 * px + 2 * py + pc) * m_per, m_per), :]

        def copy(k, block, to, src=None):
            return pltpu.make_async_remote_copy(
                src_ref=rows(*block) if src is None else src, dst_ref=rows(*block),
                send_sem=send_sems.at[k], recv_sem=recv_sems.at[k], device_id=to, device_id_type=MESH)

        mine = pltpu.make_async_copy(x_ref, rows(*me), local_sem)
        mine.start()
        first = [copy(0, me, sibling, src=x_ref)]
        first += [copy(1 + j, me, (*chip, c), src=x_ref) for j, chip in enumerate(chips)]
        for cp in first:
            cp.start()
        passed = [copy(4 + j, (*chip, c), sibling) for j, chip in enumerate(chips)]
        for j, chip in enumerate(chips):
            copy(1 + j, (*chip, c), me).wait_recv()
            passed[j].start()
        copy(0, sibling, me).wait_recv()
        for j, chip in enumerate(chips):
            copy(4 + j, (*chip, 1 - c), me).wait_recv()
        for cp in first + passed:
            cp.wait_send()
        mine.wait()

    return pl.pallas_call(
        body, name=name,
        out_shape=jax.ShapeDtypeStruct((8 * m_per, n), blk.dtype),
        in_specs=[pl.BlockSpec(memory_space=pltpu.VMEM)],
        out_specs=pl.BlockSpec(memory_space=pltpu.VMEM),
        scratch_shapes=[pltpu.SemaphoreType.DMA((7,)), pltpu.SemaphoreType.DMA((7,)), pltpu.SemaphoreType.DMA],
        compiler_params=pltpu.CompilerParams(vmem_limit_bytes=VMEM_LIMIT),
    )(blk)


def _chip_gather(shards, split, name):
    k_arr = len(shards)

    def body(*refs):
        srcs, dsts = refs[:k_arr], refs[k_arr:2 * k_arr]
        send_sems, recv_sems, fwd_send_sems, fwd_recv_sems, local_sems = refs[2 * k_arr:]
        x, y, c = _coords()
        peers = [(1 - x, y, c), (x, 1 - y, c), (1 - x, 1 - y, c)]
        sibling = (x, y, 1 - c)
        me_chip = 2 * x + y

        def part(ref, a, core):
            if not split[a]:
                return ref
            half = shards[a].shape[1] // 2
            return ref.at[:, pl.ds(core * half, half)]

        def ici(a, j, src_chip, dst_dev):
            return pltpu.make_async_remote_copy(
                src_ref=part(srcs[a], a, c), dst_ref=part(dsts[a].at[src_chip], a, c),
                send_sem=send_sems.at[a, j], recv_sem=recv_sems.at[a, j], device_id=dst_dev, device_id_type=MESH)

        def d2d(a, j, src_chip, core):
            return pltpu.make_async_remote_copy(
                src_ref=part(dsts[a].at[src_chip], a, core), dst_ref=part(dsts[a].at[src_chip], a, core),
                send_sem=fwd_send_sems.at[a, j], recv_sem=fwd_recv_sems.at[a, j],
                device_id=sibling, device_id_type=MESH)

        local = [pltpu.make_async_copy(srcs[a], dsts[a].at[me_chip], local_sems.at[a]) for a in range(k_arr)]
        for cp in local:
            cp.start()
        sends = [ici(a, j, me_chip, peer) for a in range(k_arr) for j, peer in enumerate(peers)]
        for cp in sends:
            cp.start()
        forwards = []
        for a in range(k_arr):
            for j, peer in enumerate(peers):
                peer_chip = 2 * peer[0] + peer[1]
                ici(a, j, peer_chip, peer).wait_recv()
                if split[a]:
                    forwards.append(d2d(a, j, peer_chip, c))
                    forwards[-1].start()
        for a in range(k_arr):
            for j, peer in enumerate(peers):
                if split[a]:
                    d2d(a, j, 2 * peer[0] + peer[1], 1 - c).wait_recv()
        for cp in sends + forwards:
            cp.wait_send()
        for cp in local:
            cp.wait()

    any_spec = pl.BlockSpec(memory_space=pl.ANY)
    return pl.pallas_call(
        body, name=name,
        out_shape=[jax.ShapeDtypeStruct((4,) + s.shape, s.dtype) for s in shards],
        in_specs=[any_spec] * k_arr, out_specs=[any_spec] * k_arr,
        scratch_shapes=[pltpu.SemaphoreType.DMA((k_arr, 3))] * 4 + [pltpu.SemaphoreType.DMA((k_arr,))],
    )(*shards)


def _chip_scatter(pieces, name):
    k_arr = len(pieces)

    def body(*refs):
        srcs, dsts = refs[:k_arr], refs[k_arr:2 * k_arr]
        send_sems, recv_sems = refs[2 * k_arr:]
        x, y, c = _coords()
        peers = [(1 - x, y, c), (x, 1 - y, c), (1 - x, 1 - y, c)]
        copies = []
        for a in range(k_arr):
            for j, peer in enumerate(peers):
                copies.append(pltpu.make_async_remote_copy(
                    src_ref=srcs[a].at[2 * peer[0] + peer[1]], dst_ref=dsts[a].at[j],
                    send_sem=send_sems.at[a, j], recv_sem=recv_sems.at[a, j], device_id=peer, device_id_type=MESH))
        for cp in copies:
            cp.start()
        for cp in copies:
            cp.wait_recv()
        for cp in copies:
            cp.wait_send()

    any_spec = pl.BlockSpec(memory_space=pl.ANY)
    return pl.pallas_call(
        body, name=name,
        out_shape=[jax.ShapeDtypeStruct((3,) + p.shape[1:], p.dtype) for p in pieces],
        in_specs=[any_spec] * k_arr, out_specs=[any_spec] * k_arr,
        scratch_shapes=[pltpu.SemaphoreType.DMA((k_arr, 3)), pltpu.SemaphoreType.DMA((k_arr, 3))],
    )(*pieces)


def _sibling_swap(arrs, name):
    k_arr = len(arrs)

    def body(*refs):
        srcs, dsts = refs[:k_arr], refs[k_arr:2 * k_arr]
        send_sems, recv_sems = refs[2 * k_arr:]
        x, y, c = _coords()
        copies = [pltpu.make_async_remote_copy(
            src_ref=srcs[a], dst_ref=dsts[a], send_sem=send_sems.at[a], recv_sem=recv_sems.at[a],
            device_id=(x, y, 1 - c), device_id_type=MESH) for a in range(k_arr)]
        for cp in copies:
            cp.start()
        for cp in copies:
            cp.wait_recv()
        for cp in copies:
            cp.wait_send()

    any_spec = pl.BlockSpec(memory_space=pl.ANY)
    return pl.pallas_call(
        body, name=name,
        out_shape=[jax.ShapeDtypeStruct(a.shape, a.dtype) for a in arrs],
        in_specs=[any_spec] * k_arr, out_specs=[any_spec] * k_arr,
        scratch_shapes=[pltpu.SemaphoreType.DMA((k_arr,)), pltpu.SemaphoreType.DMA((k_arr,))],
    )(*arrs)


def _split_w_in(w_in_t):
    wa = jnp.concatenate([w_in_t[0:1024], w_in_t[1040:1552]], axis=0)
    wb = w_in_t[1552:3088]
    wc = w_in_t[3096:3608]
    wd = jnp.concatenate([w_in_t[1024:1040], w_in_t[3088:3096],
                          jnp.zeros((128 - SMALL_USED, w_in_t.shape[1]), w_in_t.dtype)], axis=0)
    return wa, wb, wc, wd


def _merge_dw_in(dwa, dwb, dwc, dwd):
    return jnp.concatenate([dwa[0:1024], dwd[0:GLA_RANK], dwa[1024:1536], dwb, dwd[GLA_RANK:SMALL_USED], dwc], axis=0)


def _local_step(x, mod, w_in16, w_out16, gla_wg, gla_bg, gla_nw, conv_w, a_log, dt_bias, gdn_nw, ln_w, ln_b, tgt):
    bl, seq, _ = x.shape
    n = bl * seq
    x2 = x.reshape(n, D_MODEL)
    tgt2 = tgt.reshape(n, D_MODEL)
    sh3 = mod[:, None, 0:D_MODEL]
    sc3 = 1.0 + mod[:, None, D_MODEL:2 * D_MODEL]
    g1p3 = 1.0 + mod[:, None, 2 * D_MODEL:]
    ws = _split_w_in(w_in16)
    wg = jnp.concatenate([gla_wg, jnp.zeros((128 - GLA_RANK, GLA_QK), F32)], axis=0)
    cw8 = jnp.concatenate([conv_w, jnp.zeros((8 - CONV_K, conv_w.shape[1]), F32)], axis=0)
    alog_v = jnp.zeros((1, 128), F32).at[:, LANE_A:LANE_A + GDN_HEADS].set(a_log)
    dtb_v = jnp.zeros((1, 128), F32).at[:, LANE_A:LANE_A + GDN_HEADS].set(dt_bias)

    pa, pb, pc, pd = _proj_fwd(x2, sc3, sh3, ws, seq)
    ya, st_a = _gla_fwd(pa, pd, wg, gla_bg, gla_nw, bl, seq)
    qkv, gb, conv_out = _gdn_pre_fwd(pb, pd, cw8, alog_v, dtb_v, bl, seq)
    yb, st_b = _gdn_fwd(qkv, gb, pc, gdn_nw, bl, seq)
    dz, dya, dyb, d_wo, d_gate, d_lnw, d_lnb, loss = _out_block(x2, tgt2, ya, yb, g1p3, w_out16, ln_w, ln_b, seq)
    da, dd1, d_wg, d_bg, d_nwa = _gla_bwd(pa, pd, st_a, dya, wg, gla_bg, gla_nw, bl, seq)
    dqkv, dc, dgb, d_nwb = _gdn_bwd(qkv, gb, pc, st_b, dyb, gdn_nw, bl, seq)
    db, dd2, d_cw8, d_alog, d_dtb = _gdn_pre_bwd(pb, conv_out, pd, dqkv, dgb, cw8, alog_v, dtb_v, bl, seq)
    gx, d_sh, d_sc = _proj_bwd_x((da, db, dc, (dd1, dd2)), ws, x2, dz, sc3, seq)
    (dwa,) = _proj_bwd_w(x2, sc3, sh3, [da], seq, w_in16.dtype, "proj_bwd_w_a")
    dwb, dwc, dwd = _proj_bwd_w(x2, sc3, sh3, [db, dc, (dd1, dd2)], seq, w_in16.dtype, "proj_bwd_w_bcd")
    grads = dict(
        w_in=_merge_dw_in(dwa, dwb, dwc, dwd),
        w_out=d_wo,
        gla_w_gate_up=d_wg[0:GLA_RANK, :],
        gla_b_gate=d_bg,
        gla_norm_w=d_nwa,
        gdn_conv_w=d_cw8[0:CONV_K, :],
        gdn_a_log=d_alog[:, LANE_A:LANE_A + GDN_HEADS],
        gdn_dt_bias=d_dtb[:, LANE_A:LANE_A + GDN_HEADS],
        gdn_norm_w=d_nwb,
        ln_w=d_lnw,
        ln_b=d_lnb,
        mod=jnp.concatenate([d_sh[:, 0, :], d_sc[:, 0, :], d_gate[:, 0, :]], axis=1),
    )
    return loss, gx.reshape(bl, seq, D_MODEL), grads


_SMALL = (("gla_b_gate", 256), ("gla_norm_w", 128), ("gdn_a_log", 4), ("gdn_dt_bias", 4), ("gdn_norm_w", 128),
          ("ln_w", 1024), ("ln_b", 1024), ("gla_w_gate_up", 16 * 256), ("gdn_conv_w", 4 * 1536), ("mod", 2 * 3072))


def _pack_small(grads):
    flat = jnp.concatenate([grads[k].reshape(-1) for k, _ in _SMALL])
    total = sum(sz for _, sz in _SMALL)
    rows = -(-total // 1024) * 8
    return jnp.concatenate([flat, jnp.zeros((rows * 128 - total,), F32)]).reshape(rows, 128)


def _unpack_small(flat):
    out, pos = {}, 0
    for k, sz in _SMALL:
        out[k] = flat[pos:pos + sz]
        pos += sz
    return out


def kernel(x, c, w_ada, b_ada, w_in, gla_w_gate_up, gla_b_gate, gla_norm_w, gdn_conv_w, gdn_a_log, gdn_dt_bias, gdn_norm_w, w_out, ln_w, ln_b, loss_target, m_w_ada, m_b_ada, m_w_in, m_gla_w_gate_up, m_gla_b_gate, m_gla_norm_w, m_gdn_conv_w, m_gdn_a_log, m_gdn_dt_bias, m_gdn_norm_w, m_w_out, m_ln_w, m_ln_b, v_w_ada, v_b_ada, v_w_in, v_gla_w_gate_up, v_gla_b_gate, v_gla_norm_w, v_gdn_conv_w, v_gdn_a_log, v_gdn_dt_bias, v_gdn_norm_w, v_w_out, v_ln_w, v_ln_b):
    ix, iy, ic = _coords()
    chip = 2 * ix + iy
    dev = 4 * ix + 2 * iy + ic
    bl = x.shape[0]
    ndev = 8

    c_all = _all_gather8(c.reshape(8, -1), "gather_c").reshape(ndev * bl, D_MODEL)
    ada_cols = w_ada.shape[2]
    b_blk = lax.dynamic_slice_in_dim(b_ada, chip * ada_cols, ada_cols, axis=1)
    mod_blk = _mod_block(c_all, w_ada[0], b_blk)
    mod_g = _all_gather8(mod_blk, "gather_mod").reshape(ndev, ndev * bl, ada_cols)
    mod_all = jnp.concatenate([mod_g[2 * j] for j in range(4)], axis=1)
    mod = lax.dynamic_slice_in_dim(mod_all, dev * bl, bl, axis=0)

    w_in_g, w_out_g, wg_g, cw_g = _chip_gather(
        [jnp.transpose(w_in[0]).astype(BF16), w_out[0].astype(BF16), gla_w_gate_up[0], gdn_conv_w[0]],
        [True, True, False, False], "gather_weights")
    w_in16 = w_in_g.reshape(IN_COLS, D_MODEL)
    w_out16 = w_out_g.reshape(D_MODEL, D_MODEL)
    gla_wg = jnp.concatenate([wg_g[j] for j in range(4)], axis=1)
    conv_w = jnp.concatenate([cw_g[j] for j in range(4)], axis=1)

    loss, grad_x, gr = _local_step(x, mod, w_in16, w_out16, gla_wg, gla_b_gate, gla_norm_w, conv_w,
                                   gdn_a_log, gdn_dt_bias, gdn_norm_w, ln_w, ln_b, loss_target)
    loss = lax.psum(loss[0, 0], ("x", "y", "c"))

    packed = _pack_small(gr)
    prow = packed.shape[0]
    gathered = _all_gather8(packed, "gather_small").reshape(ndev, prow, 128)
    small = _unpack_small(_sum_leading(gathered, "sum_small").reshape(-1))
    mod_rows = gathered.reshape(ndev, prow * 128)[:, sum(sz for _, sz in _SMALL[:-1]):][:, :bl * 3 * D_MODEL]
    dmod_all = mod_rows.reshape(ndev * bl, 3 * D_MODEL)
    dmod_blk = lax.dynamic_slice_in_dim(dmod_all, chip * ada_cols, ada_cols, axis=1)
    g_w_ada, g_b_ada = _ada_grads(c_all, dmod_all, dmod_blk)
    wg_cols = gla_w_gate_up.shape[2]
    g_wg = lax.dynamic_slice_in_dim(small["gla_w_gate_up"].reshape(GLA_RANK, GLA_QK), chip * wg_cols, wg_cols, axis=1)
    cw_cols = gdn_conv_w.shape[2]
    g_cw = lax.dynamic_slice_in_dim(small["gdn_conv_w"].reshape(CONV_K, 3 * GDN_WIDTH), chip * cw_cols, cw_cols, axis=1)

    in_rows = w_in.shape[2]
    out_rows = w_out.shape[1]
    p_in = gr["w_in"].reshape(4, in_rows, D_MODEL)
    p_out = gr["w_out"].reshape(4, out_rows, D_MODEL)
    h_in, h_out = D_MODEL // 2, out_rows // 2
    mine_in = lax.dynamic_slice_in_dim(p_in, ic * h_in, h_in, axis=2)
    mine_out = lax.dynamic_slice_in_dim(p_out, ic * h_out, h_out, axis=1)
    theirs_in = lax.dynamic_slice_in_dim(p_in, (1 - ic) * h_in, h_in, axis=2)
    theirs_out = lax.dynamic_slice_in_dim(p_out, (1 - ic) * h_out, h_out, axis=1)
    got_in, got_out = _sibling_swap([theirs_in, theirs_out], "swap_halves")
    (chip_in,) = _add_n([mine_in.reshape(4 * in_rows, h_in), got_in.reshape(4 * in_rows, h_in)], "chip_sum_in")
    chip_out, chip_out16 = _add_n([mine_out.reshape(4 * h_out, D_MODEL), got_out.reshape(4 * h_out, D_MODEL)],
                                  "chip_sum_out", (F32, BF16))
    chip_in = chip_in.reshape(4, in_rows, h_in)
    chip_out = chip_out.reshape(4, h_out, D_MODEL)
    rs_in, rs_out = _chip_scatter([chip_in.astype(BF16), chip_out16.reshape(4, h_out, D_MODEL)], "scatter_grads")
    own_in = lax.dynamic_index_in_dim(chip_in, chip, axis=0, keepdims=False)
    own_out = lax.dynamic_index_in_dim(chip_out, chip, axis=0, keepdims=False)
    (half_in,) = _add_n([own_in, rs_in[0], rs_in[1], rs_in[2]], "reduce_in")
    (half_out,) = _add_n([own_out, rs_out[0], rs_out[1], rs_out[2]], "reduce_out")
    sib_in, sib_out = _sibling_swap([half_in, half_out], "swap_result")
    g_w_in_t = jnp.where(ic == 0, jnp.concatenate([half_in, sib_in], axis=1),
                         jnp.concatenate([sib_in, half_in], axis=1))
    g_w_out = jnp.where(ic == 0, jnp.concatenate([half_out, sib_out], axis=0),
                        jnp.concatenate([sib_out, half_out], axis=0))

    grads = dict(
        w_ada=g_w_ada[None], b_ada=g_b_ada, w_in=g_w_in_t, gla_w_gate_up=g_wg[None],
        gla_b_gate=small["gla_b_gate"].reshape(1, -1), gla_norm_w=small["gla_norm_w"].reshape(1, -1),
        gdn_conv_w=g_cw[None], gdn_a_log=small["gdn_a_log"].reshape(1, -1),
        gdn_dt_bias=small["gdn_dt_bias"].reshape(1, -1), gdn_norm_w=small["gdn_norm_w"].reshape(1, -1),
        w_out=g_w_out[None], ln_w=small["ln_w"].reshape(1, -1), ln_b=small["ln_b"].reshape(1, -1))
    weights = dict(w_ada=w_ada, b_ada=b_ada, w_in=w_in, gla_w_gate_up=gla_w_gate_up, gla_b_gate=gla_b_gate,
                   gla_norm_w=gla_norm_w, gdn_conv_w=gdn_conv_w, gdn_a_log=gdn_a_log, gdn_dt_bias=gdn_dt_bias,
                   gdn_norm_w=gdn_norm_w, w_out=w_out, ln_w=ln_w, ln_b=ln_b)
    m_in = dict(w_ada=m_w_ada, b_ada=m_b_ada, w_in=m_w_in, gla_w_gate_up=m_gla_w_gate_up, gla_b_gate=m_gla_b_gate,
                gla_norm_w=m_gla_norm_w, gdn_conv_w=m_gdn_conv_w, gdn_a_log=m_gdn_a_log, gdn_dt_bias=m_gdn_dt_bias,
                gdn_norm_w=m_gdn_norm_w, w_out=m_w_out, ln_w=m_ln_w, ln_b=m_ln_b)
    v_in = dict(w_ada=v_w_ada, b_ada=v_b_ada, w_in=v_w_in, gla_w_gate_up=v_gla_w_gate_up, gla_b_gate=v_gla_b_gate,
                gla_norm_w=v_gla_norm_w, gdn_conv_w=v_gdn_conv_w, gdn_a_log=v_gdn_a_log, gdn_dt_bias=v_gdn_dt_bias,
                gdn_norm_w=v_gdn_norm_w, w_out=v_w_out, ln_w=v_ln_w, ln_b=v_ln_b)
    names = list(weights)
    delta, new_m, new_v = {}, {}, {}
    for nm in names:
        shp = weights[nm].shape
        if nm == "w_in":
            to2d = lambda t: jnp.transpose(t[0])
            from2d = lambda t: jnp.transpose(t)[None]
            g2d = grads[nm]
        else:
            to2d = lambda t: t.reshape(-1, shp[-1])
            from2d = lambda t: t.reshape(shp)
            g2d = to2d(grads[nm])
        d, a, b = _adamw(to2d(weights[nm]), g2d, to2d(m_in[nm]), to2d(v_in[nm]), "adamw_" + nm)
        delta[nm], new_m[nm], new_v[nm] = from2d(d), from2d(a), from2d(b)
        grads[nm] = from2d(g2d)
    return (loss, grad_x, *[grads[k] for k in names], *[delta[k] for k in names],
            *[new_m[k] for k in names], *[new_v[k] for k in names])
```

```python
import functools

import jax
import jax.numpy as jnp
from jax import lax
from jax.experimental import pallas as pl
from jax.experimental.pallas import tpu as pltpu

F32 = jnp.float32
BF16 = jnp.bfloat16
HI = lax.Precision.HIGHEST
INV_PREC = None
MESH = pl.DeviceIdType.MESH

D_MODEL = 1024
GLA_HEADS = 4
GLA_DK = 64
GLA_DV = 128
GLA_QK = 256
GLA_WIDTH = 512
GLA_RANK = 16
GLA_GATE_NORM = 16.0
GDN_HEADS = 4
GDN_DK = 128
GDN_WIDTH = 512
CONV_K = 4
CHUNK = 64
LN_EPS = 1e-5
RMS_EPS = 1e-6
ALPHA = 2.0 ** 0.25
IN_COLS = 3608

LANE_A = GLA_RANK
LANE_B = GLA_RANK + GDN_HEADS
SMALL_USED = GLA_RANK + 2 * GDN_HEADS

ADAM_LR = 0.001
ADAM_B1 = 0.9
ADAM_B2 = 0.999
ADAM_EPS = 1e-08
ADAM_WD = 0.01
ADAM_STEP = 10

VMEM_LIMIT = 56 * 1024 * 1024


def _iota(shape, dim):
    return lax.broadcasted_iota(jnp.int32, shape, dim)


def _dot(a, b, prec=None):
    return lax.dot_general(a, b, (((1,), (0,)), ((), ())), precision=prec, preferred_element_type=F32)


def _dot_nt(a, b, prec=None):
    return lax.dot_general(a, b, (((1,), (1,)), ((), ())), precision=prec, preferred_element_type=F32)


def _dot_tn(a, b, prec=None):
    return lax.dot_general(a, b, (((0,), (0,)), ((), ())), precision=prec, preferred_element_type=F32)


def _log_sigmoid(z):
    return jnp.minimum(z, 0.0) - jnp.log1p(jnp.exp(-jnp.abs(z)))


def _softplus(z):
    return jnp.maximum(z, 0.0) + jnp.log1p(jnp.exp(-jnp.abs(z)))


def _silu(z):
    return z * jax.nn.sigmoid(z)


def _rms_gate(o, nw, og):
    return o * lax.rsqrt(jnp.mean(o * o, axis=-1, keepdims=True) + RMS_EPS) * nw * _silu(og)


def _params(*sem):
    return pltpu.CompilerParams(dimension_semantics=sem, vmem_limit_bytes=VMEM_LIMIT)


def _gla_chunk(qs, ks, lrs, vs, ogs, ss, wg, bg, nw):
    c = qs[0].shape[0]
    units = [divmod(i, GLA_HEADS) for i in range(len(vs))]
    row, col = _iota((c, c), 0), _iota((c, c), 1)
    causal = row >= col
    first_half = (_iota((c, 1), 0) < c // 2).astype(F32)
    lane = _iota((1, GLA_QK), 1)
    masks = [((lane >= h * GLA_DK) & (lane < (h + 1) * GLA_DK)).astype(F32) for h in range(GLA_HEADS)]
    gs = [_log_sigmoid(_dot(lr, wg) + bg) * (1.0 / GLA_GATE_NORM) for lr in lrs]
    bs = [_dot(causal.astype(F32), g, HI) for g in gs]
    b_ref = [jnp.sum(g * first_half, axis=0, keepdims=True) for g in gs]
    b_last = [jnp.sum(g, axis=0, keepdims=True) for g in gs]
    qsc = [q * (GLA_DK ** -0.5) for q in qs]
    qe = [q * jnp.exp(b - br) for q, b, br in zip(qsc, bs, b_ref)]
    ke = [k * jnp.exp(br - b) for k, b, br in zip(ks, bs, b_ref)]
    qb = [q * jnp.exp(b) for q, b in zip(qsc, bs)]
    kd = [k * jnp.exp(bl_ - b) for k, b, bl_ in zip(ks, bs, b_last)]
    decay = [jnp.exp(bl_) for bl_ in b_last]
    att = [jnp.where(causal, _dot_nt(qe[e] * masks[h], ke[e]), 0.0) for e, h in units]
    o_inter = [_dot_nt(qb[e], s) for (e, h), s in zip(units, ss)]
    os_ = [_dot(a, v) + oi for a, v, oi in zip(att, vs, o_inter)]
    upd = [_dot_tn(v, kd[e]) for (e, h), v in zip(units, vs)]
    s_new = [s * decay[e] + masks[h] * up for (e, h), s, up in zip(units, ss, upd)]
    ys = [_rms_gate(o, nw, og) for o, og in zip(os_, ogs)]
    return ys, s_new


def _unit_lower_inverse_chain(a_list):
    c = a_list[0].shape[0]
    eye = (_iota((c, c), 0) == _iota((c, c), 1)).astype(F32)
    ps = [-a for a in a_list]
    ts = [eye + p for p in ps]
    for _ in range(max(c.bit_length() - 2, 0)):
        ps = [_dot(p, p, INV_PREC) for p in ps]
        ts = [t + _dot(t, p, INV_PREC) for t, p in zip(ts, ps)]
    return ts


@jax.custom_vjp
def _unit_lower_inverse(a_list):
    return _unit_lower_inverse_chain(a_list)


def _unit_lower_inverse_fwd(a_list):
    ts = _unit_lower_inverse_chain(a_list)
    return ts, ts


def _unit_lower_inverse_bwd(ts, dts):
    xs = [_dot_nt(dt, t, INV_PREC) for dt, t in zip(dts, ts)]
    return ([-_dot_tn(t, x, INV_PREC) for t, x in zip(ts, xs)],)


_unit_lower_inverse.defvjp(_unit_lower_inverse_fwd, _unit_lower_inverse_bwd)


@jax.custom_vjp
def _unit_lower_inverse_known(a_list, ts):
    return ts


def _unit_lower_inverse_known_fwd(a_list, ts):
    return ts, ts


def _unit_lower_inverse_known_bwd(ts, dts):
    return _unit_lower_inverse_bwd(ts, dts) + ([jnp.zeros_like(t) for t in ts],)


_unit_lower_inverse_known.defvjp(_unit_lower_inverse_known_fwd, _unit_lower_inverse_known_bwd)


def _gdn_prep_units(qs, ks, vs, gbs, t_known=None):
    c = qs[0].shape[0]
    units = [divmod(i, GDN_HEADS) for i in range(len(qs))]
    row, col = _iota((c, c), 0), _iota((c, c), 1)
    causal, strict = row >= col, row > col
    lane = _iota((1, 128), 1)
    d_alls = [_dot(causal.astype(F32), gb, HI) for gb in gbs]
    g_c, beta_c, d_c = [], [], []
    for r, h in units:
        sel_a = (lane == LANE_A + h).astype(F32)
        g_c.append(jnp.sum(gbs[r] * sel_a, axis=-1, keepdims=True))
        beta_c.append(jnp.sum(gbs[r] * (lane == LANE_B + h).astype(F32), axis=-1, keepdims=True))
        d_c.append(jnp.sum(d_alls[r] * sel_a, axis=-1, keepdims=True))
    d_last = [jnp.sum(g, axis=0, keepdims=True) for g in g_c]
    d_diff = [jnp.broadcast_to(d, (c, c)) - jnp.broadcast_to(d, (c, c)).T for d in d_c]
    decay_mat = [jnp.where(causal, jnp.exp(jnp.where(causal, dd, 0.0)), 0.0) for dd in d_diff]
    kb = [k * b for k, b in zip(ks, beta_c)]
    a = [jnp.where(strict, _dot_nt(kbi, k) * dm, 0.0) for kbi, k, dm in zip(kb, ks, decay_mat)]
    t = _unit_lower_inverse(a) if t_known is None else _unit_lower_inverse_known(a, t_known)
    u = [_dot(ti, v * b) for ti, v, b in zip(t, vs, beta_c)]
    w = [_dot(ti, kbi * jnp.exp(d)) for ti, kbi, d in zip(t, kb, d_c)]
    qk = [jnp.where(causal, _dot_nt(q, k) * dm, 0.0) for q, k, dm in zip(qs, ks, decay_mat)]
    q_dec = [q * jnp.exp(d) for q, d in zip(qs, d_c)]
    k_dec = [k * jnp.exp(dl - d) for k, dl, d in zip(ks, d_last, d_c)]
    gamma = [jnp.exp(dl) for dl in d_last]
    return u, w, qk, q_dec, k_dec, gamma, t


def _sum_all(t):
    return jnp.sum(jnp.sum(t, axis=-1, keepdims=True), axis=0, keepdims=True)


def _gdn_pre_elem(ps, ab, alog_v, dtb_v):
    outs = []
    for j, p in enumerate(ps):
        s = _silu(p)
        if j < 2 * GDN_HEADS:
            s = s * lax.rsqrt(jnp.sum(s * s, axis=-1, keepdims=True) + RMS_EPS)
        if j < GDN_HEADS:
            s = s * (GDN_DK ** -0.5)
        outs.append(s)
    lane = _iota((1, 128), 1)
    is_a = (lane >= LANE_A) & (lane < LANE_A + GDN_HEADS)
    is_b = (lane >= LANE_B) & (lane < LANE_B + GDN_HEADS)
    g = -jnp.exp(alog_v) * _softplus(ab + dtb_v)
    gb = jnp.where(is_a, g, jnp.where(is_b, jax.nn.sigmoid(ab), 0.0))
    return tuple(outs) + (gb,)


def _proj_fwd(x2, sc3, sh3, ws, seq, tm=256):
    n = x2.shape[0]
    tpe = seq // tm
    nw = len(ws)

    def body(x_ref, sc_ref, sh_ref, *refs):
        h = (x_ref[...] * sc_ref[0] + sh_ref[0]).astype(ws[0].dtype)
        for w_ref, o_ref in zip(refs[:nw], refs[nw:]):
            o_ref[...] = _dot_nt(h, w_ref[...])

    row = lambda i: (i, 0)
    per_ex = pl.BlockSpec((1, 1, D_MODEL), lambda i: (i // tpe, 0, 0))
    return pl.pallas_call(
        body, name="proj_fwd", grid=(n // tm,),
        in_specs=[pl.BlockSpec((tm, D_MODEL), row), per_ex, per_ex]
        + [pl.BlockSpec(w.shape, lambda i: (0, 0)) for w in ws],
        out_specs=[pl.BlockSpec((tm, w.shape[0]), row) for w in ws],
        out_shape=[jax.ShapeDtypeStruct((n, w.shape[0]), F32) for w in ws],
        compiler_params=_params("parallel"),
    )(x2, sc3, sh3, *ws)


def _gla_fwd(pa, pd, wg, bg, nw, bl, seq):
    n = pa.shape[0]
    nc = seq // CHUNK
    heads = [(e, h, slice(h * 128, (h + 1) * 128)) for e in range(bl) for h in range(GLA_HEADS)]

    def body(q_ref, k_ref, v_ref, og_ref, lr_ref, wg_ref, bg_ref, nw_ref, y_ref, st_ref, s_scr):
        @pl.when(pl.program_id(0) == 0)
        def _():
            s_scr[...] = jnp.zeros_like(s_scr)

        ss = [s_scr[e, h] for e, h, _ in heads]
        for (e, h, _), s in zip(heads, ss):
            st_ref[e, 0, h] = s
        ys, s_new = _gla_chunk([q_ref[e] for e in range(bl)], [k_ref[e] for e in range(bl)],
                               [lr_ref[e] for e in range(bl)],
                               [v_ref[e, :, cols] for e, _, cols in heads], [og_ref[e, :, cols] for e, _, cols in heads],
                               ss, wg_ref[...], bg_ref[...], nw_ref[...])
        for (e, h, cols), y, s in zip(heads, ys, s_new):
            y_ref[e, :, cols] = y
            s_scr[e, h] = s

    tok = lambda w, j: pl.BlockSpec((bl, CHUNK, w), lambda i: (0, i, j))
    const = lambda i: (0, 0)
    pa3 = pa.reshape(bl, seq, 1536)
    y, st = pl.pallas_call(
        body, name="gla_fwd", grid=(nc,),
        in_specs=[tok(256, 0), tok(256, 1), tok(512, 1), tok(512, 2), tok(128, 0),
                  pl.BlockSpec(wg.shape, const), pl.BlockSpec(bg.shape, const), pl.BlockSpec(nw.shape, const)],
        out_specs=[tok(512, 0), pl.BlockSpec((bl, 1, GLA_HEADS, 128, 256), lambda i: (0, i, 0, 0, 0))],
        out_shape=[jax.ShapeDtypeStruct((bl, seq, 512), F32),
                   jax.ShapeDtypeStruct((bl, nc, GLA_HEADS, 128, 256), F32)],
        scratch_shapes=[pltpu.VMEM((bl, GLA_HEADS, 128, 256), F32)],
        compiler_params=_params("arbitrary"),
    )(pa3, pa3, pa3, pa3, pd.reshape(bl, seq, 128), wg, bg, nw)
    return y.reshape(n, 512), st


def _gla_bwd(pa, pd, st, dya, wg, bg, nw, bl, seq):
    n = pa.shape[0]
    nc = seq // CHUNK
    heads = [(e, h, slice(h * 128, (h + 1) * 128)) for e in range(bl) for h in range(GLA_HEADS)]

    def body(q_ref, k_ref, v_ref, og_ref, lr_ref, st_ref, dy_ref, wg_ref, bg_ref, nw_ref,
             da_ref, dd_ref, dwg_ref, dbg_ref, dnw_ref, ds_scr):
        @pl.when(pl.program_id(0) == 0)
        def _():
            dwg_ref[...] = jnp.zeros_like(dwg_ref)
            dbg_ref[...] = jnp.zeros_like(dbg_ref)
            dnw_ref[...] = jnp.zeros_like(dnw_ref)
            ds_scr[...] = jnp.zeros_like(ds_scr)

        _, vjp = jax.vjp(_gla_chunk, [q_ref[e] for e in range(bl)], [k_ref[e] for e in range(bl)],
                         [lr_ref[e] for e in range(bl)],
                         [v_ref[e, :, cols] for e, _, cols in heads], [og_ref[e, :, cols] for e, _, cols in heads],
                         [st_ref[e, 0, h] for e, h, _ in heads], wg_ref[...], bg_ref[...], nw_ref[...])
        dq, dk, dlr, dv, dog, ds, dwg, dbg, dnw = vjp(([dy_ref[e, :, cols] for e, _, cols in heads],
                                                         [ds_scr[e, h] for e, h, _ in heads]))
        for e in range(bl):
            da_ref[e, :, 0:256] = dq[e]
            da_ref[e, :, 256:512] = dk[e]
            dd_ref[e] = dlr[e]
        for i, (e, h, _) in enumerate(heads):
            da_ref[e, :, 512 + h * 128:512 + (h + 1) * 128] = dv[i]
            da_ref[e, :, 1024 + h * 128:1024 + (h + 1) * 128] = dog[i]
            ds_scr[e, h] = ds[i]
        dwg_ref[...] += dwg
        dbg_ref[...] += dbg
        dnw_ref[...] += dnw

    tok = lambda w, j: pl.BlockSpec((bl, CHUNK, w), lambda i: (0, nc - 1 - i, j))
    const = lambda i: (0, 0)
    pa3 = pa.reshape(bl, seq, 1536)
    da, dd, dwg, dbg, dnw = pl.pallas_call(
        body, name="gla_bwd", grid=(nc,),
        in_specs=[tok(256, 0), tok(256, 1), tok(512, 1), tok(512, 2), tok(128, 0),
                  pl.BlockSpec((bl, 1, GLA_HEADS, 128, 256), lambda i: (0, nc - 1 - i, 0, 0, 0)), tok(512, 0),
                  pl.BlockSpec(wg.shape, const), pl.BlockSpec(bg.shape, const), pl.BlockSpec(nw.shape, const)],
        out_specs=[tok(1536, 0), tok(128, 0),
                   pl.BlockSpec(wg.shape, const), pl.BlockSpec(bg.shape, const), pl.BlockSpec(nw.shape, const)],
        out_shape=[jax.ShapeDtypeStruct((bl, seq, 1536), F32), jax.ShapeDtypeStruct((bl, seq, 128), F32),
                   jax.ShapeDtypeStruct(wg.shape, F32), jax.ShapeDtypeStruct(bg.shape, F32),
                   jax.ShapeDtypeStruct(nw.shape, F32)],
        scratch_shapes=[pltpu.VMEM((bl, GLA_HEADS, 128, 256), F32)],
        compiler_params=_params("arbitrary"),
    )(pa3, pa3, pa3, pa3, pd.reshape(bl, seq, 128), st, dya.reshape(bl, seq, 512), wg, bg, nw)
    return da.reshape(n, 1536), dd.reshape(n, 128), dwg, dbg, dnw


def _conv_taps(buf_ref, w_ref, base, rows):
    acc = w_ref[0:1, :] * buf_ref[pl.ds(base, rows), :]
    for k in range(1, CONV_K):
        acc = acc + w_ref[k:k + 1, :] * buf_ref[pl.ds(base + k, rows), :]
    return acc


def _gdn_pre_fwd(pb, pd, cw8, alog_v, dtb_v, bl, seq, tm=256):
    n = pb.shape[0]
    tpe = seq // tm
    t8 = tm // 8

    def body(u_ref, prev_ref, ab_ref, w_ref, al_ref, dt_ref, qkv_ref, gb_ref, p_ref, buf):
        i = pl.program_id(0)
        keep = (i % tpe != 0).astype(F32)
        buf[0:8, :] = prev_ref[...] * keep
        buf[8:8 + tm, :] = u_ref[...]
        p = _conv_taps(buf, w_ref, 8 - (CONV_K - 1), tm)
        p_ref[...] = p
        ps = [p[:, j * 128:(j + 1) * 128] for j in range(12)]
        outs = _gdn_pre_elem(ps, ab_ref[...], al_ref[...], dt_ref[...])
        for j in range(12):
            qkv_ref[:, j * 128:(j + 1) * 128] = outs[j]
        gb_ref[...] = outs[12]

    row = lambda i: (i, 0)
    const = lambda i: (0, 0)
    return pl.pallas_call(
        body, name="gdn_pre_fwd", grid=(n // tm,),
        in_specs=[pl.BlockSpec((tm, 1536), row),
                  pl.BlockSpec((8, 1536), lambda i: (jnp.maximum(i * t8 - 1, 0), 0)),
                  pl.BlockSpec((tm, 128), row),
                  pl.BlockSpec((8, 1536), const), pl.BlockSpec((1, 128), const), pl.BlockSpec((1, 128), const)],
        out_specs=[pl.BlockSpec((tm, 1536), row), pl.BlockSpec((tm, 128), row), pl.BlockSpec((tm, 1536), row)],
        out_shape=[jax.ShapeDtypeStruct((n, 1536), F32), jax.ShapeDtypeStruct((n, 128), F32),
                   jax.ShapeDtypeStruct((n, 1536), F32)],
        scratch_shapes=[pltpu.VMEM((tm + 8, 1536), F32)],
        compiler_params=_params("parallel"),
    )(pb, pb, pd, cw8, alog_v, dtb_v)


def _gdn_pre_bwd(pb, conv_out, pd, dqkv, dgb, cw8, alog_v, dtb_v, bl, seq, tm=256):
    n = pb.shape[0]
    tpe = seq // tm
    t8 = tm // 8
    nb8 = n // 8
    ext = tm + 8

    def body(u_ref, p_ref, pn_ref, ab_ref, abn_ref, dq_ref, dqn_ref, dgb_ref, w_ref, al_ref, dt_ref,
             du_ref, dab_ref, dw_ref, dal_ref, ddt_ref, dpbuf):
        i = pl.program_id(0)

        @pl.when(i == 0)
        def _():
            dw_ref[...] = jnp.zeros_like(dw_ref)
            dal_ref[...] = jnp.zeros_like(dal_ref)
            ddt_ref[...] = jnp.zeros_like(ddt_ref)

        keep_next = (i % tpe != tpe - 1).astype(F32)
        ps = [jnp.concatenate([p_ref[:, j * 128:(j + 1) * 128], pn_ref[:, j * 128:(j + 1) * 128]], axis=0)
              for j in range(12)]
        ab = jnp.concatenate([ab_ref[...], abn_ref[...]], axis=0)
        _, vjp = jax.vjp(_gdn_pre_elem, ps, ab, al_ref[...], dt_ref[...])
        zeros8 = jnp.zeros((8, 128), F32)
        cts = tuple(jnp.concatenate([dq_ref[:, j * 128:(j + 1) * 128],
                                     dqn_ref[:, j * 128:(j + 1) * 128] * keep_next], axis=0) for j in range(12))
        cts += (jnp.concatenate([dgb_ref[...], zeros8], axis=0),)
        dps, dab, dal, ddt = vjp(cts)
        for j in range(12):
            dpbuf[:, j * 128:(j + 1) * 128] = dps[j]
        dab_ref[...] = dab[0:tm, :]
        dal_ref[...] += dal
        ddt_ref[...] += ddt
        u = u_ref[...]
        du = None
        for k in range(CONV_K):
            dp_k = dpbuf[pl.ds(CONV_K - 1 - k, tm), :]
            term = w_ref[k:k + 1, :] * dp_k
            du = term if du is None else du + term
            dw_ref[k:k + 1, :] += jnp.sum(u * dp_k, axis=0, keepdims=True)
        du_ref[...] = du

    row = lambda i: (i, 0)
    next8 = lambda i: (jnp.minimum((i + 1) * t8, nb8 - 1), 0)
    const = lambda i: (0, 0)
    return pl.pallas_call(
        body, name="gdn_pre_bwd", grid=(n // tm,),
        in_specs=[pl.BlockSpec((tm, 1536), row), pl.BlockSpec((tm, 1536), row), pl.BlockSpec((8, 1536), next8),
                  pl.BlockSpec((tm, 128), row), pl.BlockSpec((8, 128), next8),
                  pl.BlockSpec((tm, 1536), row), pl.BlockSpec((8, 1536), next8),
                  pl.BlockSpec((tm, 128), row),
                  pl.BlockSpec((8, 1536), const), pl.BlockSpec((1, 128), const), pl.BlockSpec((1, 128), const)],
        out_specs=[pl.BlockSpec((tm, 1536), row), pl.BlockSpec((tm, 128), row),
                   pl.BlockSpec((8, 1536), const), pl.BlockSpec((1, 128), const), pl.BlockSpec((1, 128), const)],
        out_shape=[jax.ShapeDtypeStruct((n, 1536), F32), jax.ShapeDtypeStruct((n, 128), F32),
                   jax.ShapeDtypeStruct((8, 1536), F32), jax.ShapeDtypeStruct((1, 128), F32),
                   jax.ShapeDtypeStruct((1, 128), F32)],
        scratch_shapes=[pltpu.VMEM((ext, 1536), F32)],
        compiler_params=_params("arbitrary"),
    )(pb, conv_out, conv_out, pd, pd, dqkv, dqkv, dgb, cw8, alog_v, dtb_v)


GDN_PREP_CHUNKS = 2


def _head_cols(ref, rows, base=0):
    return [ref[rows, base + h * 128:base + (h + 1) * 128] for h in range(GDN_HEADS)]


def _gdn_prep(qkv, gb):
    n = qkv.shape[0]
    r_per = GDN_PREP_CHUNKS
    tm = r_per * CHUNK

    def body(q_ref, k_ref, v_ref, gb_ref, u_ref, w_ref, qd_ref, kd_ref, qk_ref, t_ref, gam_ref):
        rowid = _iota((8, 128), 0)
        chunk_rows = [slice(r * CHUNK, (r + 1) * CHUNK) for r in range(r_per)]
        gather = lambda ref: [t for rows in chunk_rows for t in _head_cols(ref, rows)]
        u, w, qk, qd, kd, gamma, tinv = _gdn_prep_units(gather(q_ref), gather(k_ref), gather(v_ref),
                                                        [gb_ref[rows, :] for rows in chunk_rows])
        for r, rows in enumerate(chunk_rows):
            gam = jnp.zeros((8, 128), F32)
            for h in range(GDN_HEADS):
                i = r * GDN_HEADS + h
                cols = slice(h * 128, (h + 1) * 128)
                u_ref[rows, cols] = u[i]
                w_ref[rows, cols] = w[i]
                qd_ref[rows, cols] = qd[i]
                kd_ref[rows, cols] = kd[i]
                qk_ref[r, h] = qk[i]
                t_ref[r, h] = tinv[i]
                gam = jnp.where(rowid == h, gamma[i], gam)
            gam_ref[r] = gam

    tok = lambda j: pl.BlockSpec((tm, 512), lambda i: (i, j))
    return pl.pallas_call(
        body, name="gdn_prep", grid=(n // tm,),
        in_specs=[tok(0), tok(1), tok(2), pl.BlockSpec((tm, 128), lambda i: (i, 0))],
        out_specs=[tok(0)] * 4 + [pl.BlockSpec((r_per, GDN_HEADS, CHUNK, CHUNK), lambda i: (i, 0, 0, 0))] * 2
        + [pl.BlockSpec((r_per, 8, 128), lambda i: (i, 0, 0))],
        out_shape=[jax.ShapeDtypeStruct((n, 512), F32)] * 4
        + [jax.ShapeDtypeStruct((n // CHUNK, GDN_HEADS, CHUNK, CHUNK), F32)] * 2
        + [jax.ShapeDtypeStruct((n // CHUNK, 8, 128), F32)],
        compiler_params=_params("parallel"),
    )(qkv, qkv, qkv, gb)


def _gdn_fwd(qkv, gb, pc, nw, bl, seq):
    n = qkv.shape[0]
    nc = seq // CHUNK
    u, w, qd, kd, qk, tinv, gam = _gdn_prep(qkv, gb)
    tok3 = lambda t: t.reshape(bl, seq, 512)
    qk5 = qk.reshape(bl, nc, GDN_HEADS, CHUNK, CHUNK)
    gam4 = gam.reshape(bl, nc, 8, 128)

    def body(u_ref, w_ref, qd_ref, kd_ref, qk_ref, gam_ref, og_ref, nw_ref, o_ref, y_ref, st_ref, s_scr):
        @pl.when(pl.program_id(0) == 0)
        def _():
            s_scr[...] = jnp.zeros_like(s_scr)

        units = [(b, h, slice(h * 128, (h + 1) * 128)) for b in range(bl) for h in range(GDN_HEADS)]
        ss = [s_scr[b, h] for b, h, _ in units]
        for (b, h, _), s in zip(units, ss):
            st_ref[b, 0, h] = s
        v_new = [u_ref[b, :, cols] - _dot(w_ref[b, :, cols], s) for (b, h, cols), s in zip(units, ss)]
        o_inter = [_dot(qd_ref[b, :, cols], s) for (b, h, cols), s in zip(units, ss)]
        os_ = [oi + _dot(qk_ref[b, 0, h], vn) for (b, h, cols), oi, vn in zip(units, o_inter, v_new)]
        for (b, h, cols), s, vn in zip(units, ss, v_new):
            s_scr[b, h] = s * gam_ref[b, 0, h:h + 1, :] + _dot_tn(kd_ref[b, :, cols], vn)
        for (b, h, cols), o in zip(units, os_):
            o_ref[b, :, cols] = o
            y_ref[b, :, cols] = _rms_gate(o, nw_ref[...], og_ref[b, :, cols])

    tok = pl.BlockSpec((bl, CHUNK, 512), lambda i: (0, i, 0))
    st_spec = pl.BlockSpec((bl, 1, GDN_HEADS, 128, 128), lambda i: (0, i, 0, 0, 0))
    o, y, st = pl.pallas_call(
        body, name="gdn_scan_fwd", grid=(nc,),
        in_specs=[tok, tok, tok, tok,
                  pl.BlockSpec((bl, 1, GDN_HEADS, CHUNK, CHUNK), lambda i: (0, i, 0, 0, 0)),
                  pl.BlockSpec((bl, 1, 8, 128), lambda i: (0, i, 0, 0)), tok,
                  pl.BlockSpec(nw.shape, lambda i: (0, 0))],
        out_specs=[tok, tok, st_spec],
        out_shape=[jax.ShapeDtypeStruct((bl, seq, 512), F32), jax.ShapeDtypeStruct((bl, seq, 512), F32),
                   jax.ShapeDtypeStruct((bl, nc, GDN_HEADS, 128, 128), F32)],
        scratch_shapes=[pltpu.VMEM((bl, GDN_HEADS, 128, 128), F32)],
        compiler_params=_params("arbitrary"),
    )(tok3(u), tok3(w), tok3(qd), tok3(kd), qk5, gam4, tok3(pc), nw)
    return y.reshape(n, 512), (o, st, w, qd, kd, qk5, gam4, tinv)


def _gdn_bwd(qkv, gb, pc, res, dyb, nw, bl, seq):
    n = qkv.shape[0]
    nc = seq // CHUNK
    o, st, w, qd, kd, qk5, gam4, tinv = res
    tok3 = lambda t: t.reshape(bl, seq, 512)

    def scan_body(dy_ref, o_ref, og_ref, w_ref, qd_ref, kd_ref, qk_ref, gam_ref, nw_ref,
                  do_ref, dog_ref, dvn_ref, dst_ref, dnw_ref, ds_scr):
        @pl.when(pl.program_id(0) == 0)
        def _():
            ds_scr[...] = jnp.zeros_like(ds_scr)
            dnw_ref[...] = jnp.zeros_like(dnw_ref)

        units = [(b, h, slice(h * 128, (h + 1) * 128)) for b in range(bl) for h in range(GDN_HEADS)]
        dnw = jnp.zeros(nw.shape, F32)
        d_os = []
        for b, h, cols in units:
            _, vjp = jax.vjp(_rms_gate, o_ref[b, :, cols], nw_ref[...], og_ref[b, :, cols])
            d_o, dnw_h, dog = vjp(dy_ref[b, :, cols])
            do_ref[b, :, cols] = d_o
            dog_ref[b, :, cols] = dog
            dnw = dnw + dnw_h
            d_os.append(d_o)
        dnw_ref[...] += dnw
        dss = [ds_scr[b, h] for b, h, _ in units]
        for (b, h, _), ds in zip(units, dss):
            dst_ref[b, 0, h] = ds
        dvn_a = [_dot(kd_ref[b, :, cols], ds) for (b, h, cols), ds in zip(units, dss)]
        dvns = [a + _dot_tn(qk_ref[b, 0, h], d_o) for (b, h, cols), a, d_o in zip(units, dvn_a, d_os)]
        ds_a = [_dot_tn(qd_ref[b, :, cols], d_o) + ds * gam_ref[b, 0, h:h + 1, :]
                for (b, h, cols), d_o, ds in zip(units, d_os, dss)]
        for (b, h, cols), a, dvn in zip(units, ds_a, dvns):
            dvn_ref[b, :, cols] = dvn
            ds_scr[b, h] = a - _dot_tn(w_ref[b, :, cols], dvn)

    rev = lambda i: nc - 1 - i
    tok = pl.BlockSpec((bl, CHUNK, 512), lambda i: (0, rev(i), 0))
    st_spec = pl.BlockSpec((bl, 1, GDN_HEADS, 128, 128), lambda i: (0, rev(i), 0, 0, 0))
    tok_shape = jax.ShapeDtypeStruct((bl, seq, 512), F32)
    d_o, dog, dvn, dst, dnw = pl.pallas_call(
        scan_body, name="gdn_scan_bwd", grid=(nc,),
        in_specs=[tok] * 6 + [pl.BlockSpec((bl, 1, GDN_HEADS, CHUNK, CHUNK), lambda i: (0, rev(i), 0, 0, 0)),
                              pl.BlockSpec((bl, 1, 8, 128), lambda i: (0, rev(i), 0, 0)),
                              pl.BlockSpec(nw.shape, lambda i: (0, 0))],
        out_specs=[tok, tok, tok, st_spec, pl.BlockSpec(nw.shape, lambda i: (0, 0))],
        out_shape=[tok_shape, tok_shape, tok_shape, jax.ShapeDtypeStruct(st.shape, F32),
                   jax.ShapeDtypeStruct(nw.shape, F32)],
        scratch_shapes=[pltpu.VMEM((bl, GDN_HEADS, 128, 128), F32)],
        compiler_params=_params("arbitrary"),
    )(tok3(dyb), o, tok3(pc), tok3(w), tok3(qd), tok3(kd), qk5, gam4, nw)

    r_per = GDN_PREP_CHUNKS
    tm = r_per * CHUNK

    def prep_body(q_ref, k_ref, v_ref, gb_ref, t_ref, st_ref, dst_ref, dvn_ref, do_ref, dqkv_ref, dgb_ref):
        chunk_rows = [slice(r * CHUNK, (r + 1) * CHUNK) for r in range(r_per)]
        gather = lambda ref: [t for rows in chunk_rows for t in _head_cols(ref, rows)]
        units = [(r, h) for r in range(r_per) for h in range(GDN_HEADS)]
        t_known = [t_ref[r, h] for r, h in units]
        prep = lambda q, k, v, g: _gdn_prep_units(q, k, v, g, t_known)[:6]
        (u, w_, _, _, _, _), vjp = jax.vjp(prep, gather(q_ref), gather(k_ref), gather(v_ref),
                                           [gb_ref[rows, :] for rows in chunk_rows])
        ss = [st_ref[r, h] for r, h in units]
        dss = [dst_ref[r, h] for r, h in units]
        dvns, d_os = gather(dvn_ref), gather(do_ref)
        v_new = [ui - _dot(wi, s) for ui, wi, s in zip(u, w_, ss)]
        d_w = [-_dot_nt(dvn, s) for dvn, s in zip(dvns, ss)]
        d_qk = [_dot_nt(d_o, vn) for d_o, vn in zip(d_os, v_new)]
        d_qd = [_dot_nt(d_o, s) for d_o, s in zip(d_os, ss)]
        d_kd = [_dot_nt(vn, ds) for vn, ds in zip(v_new, dss)]
        d_gam = [_sum_all(ds * s) for ds, s in zip(dss, ss)]
        dq, dk, dv, dgb = vjp((dvns, d_w, d_qk, d_qd, d_kd, d_gam))
        for i, (r, h) in enumerate(units):
            rows = chunk_rows[r]
            for part, d in enumerate((dq, dk, dv)):
                dqkv_ref[rows, part * 512 + h * 128:part * 512 + (h + 1) * 128] = d[i]
        for r, rows in enumerate(chunk_rows):
            dgb_ref[rows, :] = dgb[r]

    tokp = lambda j: pl.BlockSpec((tm, 512), lambda i: (i, j))
    st4 = pl.BlockSpec((r_per, GDN_HEADS, 128, 128), lambda i: (i, 0, 0, 0))
    dqkv, dgb = pl.pallas_call(
        prep_body, name="gdn_prep_bwd", grid=(n // tm,),
        in_specs=[tokp(0), tokp(1), tokp(2), pl.BlockSpec((tm, 128), lambda i: (i, 0)),
                  pl.BlockSpec((r_per, GDN_HEADS, CHUNK, CHUNK), lambda i: (i, 0, 0, 0)), st4, st4, tokp(0), tokp(0)],
        out_specs=[pl.BlockSpec((tm, 1536), lambda i: (i, 0)), pl.BlockSpec((tm, 128), lambda i: (i, 0))],
        out_shape=[jax.ShapeDtypeStruct((n, 1536), F32), jax.ShapeDtypeStruct((n, 128), F32)],
        compiler_params=_params("parallel"),
    )(qkv, qkv, qkv, gb, tinv, st.reshape(bl * nc, GDN_HEADS, 128, 128), dst.reshape(bl * nc, GDN_HEADS, 128, 128),
      dvn.reshape(n, 512), d_o.reshape(n, 512))
    return dqkv, dog.reshape(n, 512), dgb, dnw


def _out_block(x2, tgt2, ya, yb, g1p3, wo, lnw, lnb, seq, tm=256):
    n = x2.shape[0]
    tpe = seq // tm
    bl = n // seq

    def body(x_ref, t_ref, ya_ref, yb_ref, g_ref, wo_ref, lnw_ref, lnb_ref,
             dz_ref, dya_ref, dyb_ref, dwo_ref, dg_ref, glw_ref, glb_ref, loss_ref):
        i = pl.program_id(0)

        @pl.when(i == 0)
        def _():
            dwo_ref[...] = jnp.zeros_like(dwo_ref)
            glw_ref[...] = jnp.zeros_like(glw_ref)
            glb_ref[...] = jnp.zeros_like(glb_ref)
            loss_ref[...] = jnp.zeros_like(loss_ref)

        @pl.when(i % tpe == 0)
        def _():
            dg_ref[...] = jnp.zeros_like(dg_ref)

        ya16 = ya_ref[...].astype(wo.dtype)
        yb16 = yb_ref[...].astype(wo.dtype)
        wa = wo_ref[0:GLA_WIDTH, :]
        wb = wo_ref[GLA_WIDTH:, :]
        y = _dot(ya16, wa) + _dot(yb16, wb)
        g1p = g_ref[0]
        z = ALPHA * x_ref[...] + g1p * y
        mu = jnp.mean(z, axis=-1, keepdims=True)
        zc = z - mu
        rstd = lax.rsqrt(jnp.mean(zc * zc, axis=-1, keepdims=True) + LN_EPS)
        xhat = zc * rstd
        diff = xhat * lnw_ref[...] + lnb_ref[...] - t_ref[...]
        loss_ref[...] += (0.5 / D_MODEL) * jnp.sum(jnp.sum(diff * diff, axis=-1, keepdims=True), axis=0, keepdims=True)
        dout = diff * (1.0 / D_MODEL)
        glw_ref[...] += jnp.sum(dout * xhat, axis=0, keepdims=True)
        glb_ref[...] += jnp.sum(dout, axis=0, keepdims=True)
        dxh = dout * lnw_ref[...]
        dz = rstd * (dxh - jnp.mean(dxh, axis=-1, keepdims=True)
                     - xhat * jnp.mean(dxh * xhat, axis=-1, keepdims=True))
        dz_ref[...] = dz
        dg_ref[0] += jnp.sum(dz * y, axis=0, keepdims=True)
        dy = (g1p * dz).astype(wo.dtype)
        dya_ref[...] = _dot_nt(dy, wa)
        dyb_ref[...] = _dot_nt(dy, wb)
        dwo_ref[0:GLA_WIDTH, :] += _dot_tn(ya16, dy)
        dwo_ref[GLA_WIDTH:, :] += _dot_tn(yb16, dy)

    row = lambda i: (i, 0)
    const = lambda i: (0, 0)
    per_ex = pl.BlockSpec((1, 1, D_MODEL), lambda i: (i // tpe, 0, 0))
    return pl.pallas_call(
        body, name="out_block", grid=(n // tm,),
        in_specs=[pl.BlockSpec((tm, D_MODEL), row), pl.BlockSpec((tm, D_MODEL), row),
                  pl.BlockSpec((tm, 512), row), pl.BlockSpec((tm, 512), row), per_ex,
                  pl.BlockSpec((D_MODEL, D_MODEL), const), pl.BlockSpec((1, D_MODEL), const),
                  pl.BlockSpec((1, D_MODEL), const)],
        out_specs=[pl.BlockSpec((tm, D_MODEL), row), pl.BlockSpec((tm, 512), row), pl.BlockSpec((tm, 512), row),
                   pl.BlockSpec((D_MODEL, D_MODEL), const), per_ex,
                   pl.BlockSpec((1, D_MODEL), const), pl.BlockSpec((1, D_MODEL), const),
                   pl.BlockSpec((1, 1), const)],
        out_shape=[jax.ShapeDtypeStruct((n, D_MODEL), F32), jax.ShapeDtypeStruct((n, 512), F32),
                   jax.ShapeDtypeStruct((n, 512), F32), jax.ShapeDtypeStruct((D_MODEL, D_MODEL), F32),
                   jax.ShapeDtypeStruct((bl, 1, D_MODEL), F32), jax.ShapeDtypeStruct((1, D_MODEL), F32),
                   jax.ShapeDtypeStruct((1, D_MODEL), F32), jax.ShapeDtypeStruct((1, 1), F32)],
        compiler_params=_params("arbitrary"),
    )(x2, tgt2, ya, yb, g1p3, wo, lnw, lnb)


def _proj_bwd_x(ds, ws, x2, dz, sc3, seq, tm=256):
    n = x2.shape[0]
    tpe = seq // tm
    bl = n // seq

    def body(da_ref, db_ref, dc_ref, dd1_ref, dd2_ref, wa_ref, wb_ref, wc_ref, wd_ref, x_ref, dz_ref, sc_ref,
             gx_ref, dsh_ref, dsc_ref):
        i = pl.program_id(0)

        @pl.when(i % tpe == 0)
        def _():
            dsh_ref[...] = jnp.zeros_like(dsh_ref)
            dsc_ref[...] = jnp.zeros_like(dsc_ref)

        cdt = ws[0].dtype
        dh = _dot(da_ref[...].astype(cdt), wa_ref[...])
        dh += _dot(db_ref[...].astype(cdt), wb_ref[...])
        dh += _dot(dc_ref[...].astype(cdt), wc_ref[...])
        dh += _dot((dd1_ref[...] + dd2_ref[...]).astype(cdt), wd_ref[...])
        gx_ref[...] = dh * sc_ref[0] + ALPHA * dz_ref[...]
        dsh_ref[0] += jnp.sum(dh, axis=0, keepdims=True)
        dsc_ref[0] += jnp.sum(dh * x_ref[...], axis=0, keepdims=True)

    row = lambda i: (i, 0)
    const = lambda i: (0, 0)
    per_ex = pl.BlockSpec((1, 1, D_MODEL), lambda i: (i // tpe, 0, 0))
    da, db, dc, (dd1, dd2) = ds
    return pl.pallas_call(
        body, name="proj_bwd_x", grid=(n // tm,),
        in_specs=[pl.BlockSpec((tm, d.shape[1]), row) for d in (da, db, dc, dd1, dd2)]
        + [pl.BlockSpec(w.shape, const) for w in ws]
        + [pl.BlockSpec((tm, D_MODEL), row), pl.BlockSpec((tm, D_MODEL), row), per_ex],
        out_specs=[pl.BlockSpec((tm, D_MODEL), row), per_ex, per_ex],
        out_shape=[jax.ShapeDtypeStruct((n, D_MODEL), F32), jax.ShapeDtypeStruct((bl, 1, D_MODEL), F32),
                   jax.ShapeDtypeStruct((bl, 1, D_MODEL), F32)],
        compiler_params=_params("arbitrary"),
    )(da, db, dc, dd1, dd2, *ws, x2, dz, sc3)


def _proj_bwd_w(x2, sc3, sh3, ds, seq, cdt, name, tm=256):
    n = x2.shape[0]
    tpe = seq // tm
    flat, groups = [], []
    for d in ds:
        parts = d if isinstance(d, tuple) else (d,)
        groups.append(len(parts))
        flat.extend(parts)
    nin = len(flat)

    def body(x_ref, sc_ref, sh_ref, *refs):
        i = pl.program_id(0)
        outs = refs[nin:]

        @pl.when(i == 0)
        def _():
            for o in outs:
                o[...] = jnp.zeros_like(o)

        h = (x_ref[...] * sc_ref[0] + sh_ref[0]).astype(cdt)
        pos = 0
        for o, cnt in zip(outs, groups):
            d = refs[pos][...]
            for extra in refs[pos + 1:pos + cnt]:
                d = d + extra[...]
            pos += cnt
            o[...] += _dot_tn(d.astype(cdt), h)

    row = lambda i: (i, 0)
    const = lambda i: (0, 0)
    per_ex = pl.BlockSpec((1, 1, D_MODEL), lambda i: (i // tpe, 0, 0))
    widths = [(d[0] if isinstance(d, tuple) else d).shape[1] for d in ds]
    return pl.pallas_call(
        body, name=name, grid=(n // tm,),
        in_specs=[pl.BlockSpec((tm, D_MODEL), row), per_ex, per_ex]
        + [pl.BlockSpec((tm, d.shape[1]), row) for d in flat],
        out_specs=[pl.BlockSpec((w, D_MODEL), const) for w in widths],
        out_shape=[jax.ShapeDtypeStruct((w, D_MODEL), F32) for w in widths],
        compiler_params=_params("arbitrary"),
    )(x2, sc3, sh3, *flat)


def _mod_block(c_all, w_ada_sh, b_blk):
    def body(c_ref, w_ref, b_ref, o_ref):
        o_ref[...] = _dot(c_ref[...], w_ref[...]) + b_ref[...]

    return pl.pallas_call(
        body, name="mod_block",
        out_shape=jax.ShapeDtypeStruct((c_all.shape[0], w_ada_sh.shape[1]), F32),
        compiler_params=pltpu.CompilerParams(vmem_limit_bytes=VMEM_LIMIT),
    )(c_all, w_ada_sh, b_blk)


def _ada_grads(c_all, dmod_all, dmod_blk):
    def body(c_ref, da_ref, db_ref, gw_ref, gb_ref):
        gw_ref[...] = _dot_tn(c_ref[...], db_ref[...])
        gb_ref[...] = jnp.sum(da_ref[...], axis=0, keepdims=True)

    return pl.pallas_call(
        body, name="ada_grads",
        out_shape=[jax.ShapeDtypeStruct((c_all.shape[1], dmod_blk.shape[1]), F32),
                   jax.ShapeDtypeStruct((1, dmod_all.shape[1]), F32)],
        compiler_params=pltpu.CompilerParams(vmem_limit_bytes=VMEM_LIMIT),
    )(c_all, dmod_all, dmod_blk)


def _sum_leading(parts, name):
    def body(p_ref, o_ref):
        acc = p_ref[0]
        for d in range(1, parts.shape[0]):
            acc = acc + p_ref[d]
        o_ref[...] = acc

    return pl.pallas_call(
        body, name=name, out_shape=jax.ShapeDtypeStruct(parts.shape[1:], F32),
        compiler_params=pltpu.CompilerParams(vmem_limit_bytes=VMEM_LIMIT),
    )(parts)


ELEMENTWISE_BLOCK_BYTES = 2 * 1024 * 1024


def _tile2d(rows, cols, row_align=8):
    if rows * cols * 4 <= ELEMENTWISE_BLOCK_BYTES:
        return rows, cols
    fits = [t for t in range(row_align, rows, row_align) if rows % t == 0 and t * cols * 4 <= ELEMENTWISE_BLOCK_BYTES]
    if fits:
        return fits[-1], cols
    fits = [t for t in range(128, cols, 128) if cols % t == 0 and rows * t * 4 <= ELEMENTWISE_BLOCK_BYTES]
    assert fits, (rows, cols)
    return rows, fits[-1]


def _add_n(arrs, name, out_dtypes=(F32,)):
    rows, cols = arrs[0].shape
    narrow = any(jnp.dtype(dt).itemsize < 4 for dt in tuple(out_dtypes) + tuple(a.dtype for a in arrs))
    tr, tc = _tile2d(rows, cols, 16 if narrow else 8)
    n_in = len(arrs)

    def body(*refs):
        acc = refs[0][...].astype(F32)
        for r in refs[1:n_in]:
            acc = acc + r[...].astype(F32)
        for o in refs[n_in:]:
            o[...] = acc.astype(o.dtype)

    spec = pl.BlockSpec((tr, tc), lambda i, j: (i, j))
    return pl.pallas_call(
        body, name=name, grid=(rows // tr, cols // tc), in_specs=[spec] * n_in, out_specs=[spec] * len(out_dtypes),
        out_shape=[jax.ShapeDtypeStruct((rows, cols), dt) for dt in out_dtypes],
        compiler_params=_params("parallel", "parallel"),
    )(*arrs)


def _chip_sum_blocks(a, b, per, blocks, name, chunk=128):
    rows, cols = a.shape
    padded = -(-per // 16) * 16
    assert rows >= (blocks - 1) * per + padded, (rows, per, blocks)

    def body(a_ref, b_ref, o_ref, o16_ref):
        for j in range(blocks):
            for r0 in range(0, padded, chunk):
                n_rows = min(chunk, padded - r0)
                src = pl.ds(j * per + r0, n_rows)
                s = a_ref[src, :] + b_ref[src, :]
                if per - r0 < n_rows:
                    s = jnp.where(_iota((n_rows, 1), 0) < per - r0, s, 0.0)
                o_ref[j, r0:r0 + n_rows, :] = s
                o16_ref[j, r0:r0 + n_rows, :] = s.astype(BF16)

    return pl.pallas_call(
        body, name=name,
        out_shape=[jax.ShapeDtypeStruct((blocks, padded, cols), F32), jax.ShapeDtypeStruct((blocks, padded, cols), BF16)],
        compiler_params=pltpu.CompilerParams(vmem_limit_bytes=VMEM_LIMIT),
    )(a, b)


GRAD_PAD_ROWS = 16


def _adamw(w, g, m, v, name):
    rows, cols = w.shape
    tr, tc = _tile2d(rows, cols)
    c1 = 1.0 / (1.0 - ADAM_B1 ** ADAM_STEP)
    c2 = 1.0 / (1.0 - ADAM_B2 ** ADAM_STEP)

    def body(w_ref, g_ref, m_ref, v_ref, d_ref, nm_ref, nv_ref):
        gg = g_ref[...]
        nm = ADAM_B1 * m_ref[...] + (1.0 - ADAM_B1) * gg
        nv = ADAM_B2 * v_ref[...] + (1.0 - ADAM_B2) * (gg * gg)
        nm_ref[...] = nm
        nv_ref[...] = nv
        d_ref[...] = -ADAM_LR * ((nm * c1) / (jnp.sqrt(nv * c2) + ADAM_EPS) + ADAM_WD * w_ref[...])

    spec = pl.BlockSpec((tr, tc), lambda i, j: (i, j))
    shp = jax.ShapeDtypeStruct((rows, cols), F32)
    return pl.pallas_call(
        body, name=name, grid=(rows // tr, cols // tc), in_specs=[spec] * 4, out_specs=[spec] * 3,
        out_shape=[shp, shp, shp], compiler_params=_params("parallel", "parallel"),
    )(w, g, m, v)


def _coords():
    return lax.axis_index("x"), lax.axis_index("y"), lax.axis_index("c")


def _all_gather8(blk, name):
    m_per, n = blk.shape

    def body(x_ref, out_ref, send_sems, recv_sems, local_sem):
        x, y, c = _coords()
        me, sibling = (x, y, c), (x, y, 1 - c)
        chips = [(1 - x, y), (x, 1 - y), (1 - x, 1 - y)]

        def rows(px, py, pc):
            return out_ref.at[pl.ds((4 * px + 2 * py + pc) * m_per, m_per), :]

        def copy(k, block, to, src=None):
            return pltpu.make_async_remote_copy(
                src_ref=rows(*block) if src is None else src, dst_ref=rows(*block),
                send_sem=send_sems.at[k], recv_sem=recv_sems.at[k], device_id=to, device_id_type=MESH)

        mine = pltpu.make_async_copy(x_ref, rows(*me), local_sem)
        mine.start()
        first = [copy(0, me, sibling, src=x_ref)]
        first += [copy(1 + j, me, (*chip, c), src=x_ref) for j, chip in enumerate(chips)]
        for cp in first:
            cp.start()
        passed = [copy(4 + j, (*chip, c), sibling) for j, chip in enumerate(chips)]
        for j, chip in enumerate(chips):
            copy(1 + j, (*chip, c), me).wait_recv()
            passed[j].start()
        copy(0, sibling, me).wait_recv()
        for j, chip in enumerate(chips):
            copy(4 + j, (*chip, 1 - c), me).wait_recv()
        for cp in first + passed:
            cp.wait_send()
        mine.wait()

    return pl.pallas_call(
        body, name=name,
        out_shape=jax.ShapeDtypeStruct((8 * m_per, n), blk.dtype),
        in_specs=[pl.BlockSpec(memory_space=pltpu.VMEM)],
        out_specs=pl.BlockSpec(memory_space=pltpu.VMEM),
        scratch_shapes=[pltpu.SemaphoreType.DMA((7,)), pltpu.SemaphoreType.DMA((7,)), pltpu.SemaphoreType.DMA],
        compiler_params=pltpu.CompilerParams(vmem_limit_bytes=VMEM_LIMIT),
    )(blk)


def _chip_gather(shards, split, name):
    k_arr = len(shards)

    def body(*refs):
        srcs, dsts = refs[:k_arr], refs[k_arr:2 * k_arr]
        send_sems, recv_sems, fwd_send_sems, fwd_recv_sems, local_sems = refs[2 * k_arr:]
        x, y, c = _coords()
        peers = [(1 - x, y, c), (x, 1 - y, c), (1 - x, 1 - y, c)]
        sibling = (x, y, 1 - c)
        me_chip = 2 * x + y

        def part(ref, a, core):
            if not split[a]:
                return ref
            half = shards[a].shape[1] // 2
            return ref.at[:, pl.ds(core * half, half)]

        def ici(a, j, src_chip, dst_dev):
            return pltpu.make_async_remote_copy(
                src_ref=part(srcs[a], a, c), dst_ref=part(dsts[a].at[src_chip], a, c),
                send_sem=send_sems.at[a, j], recv_sem=recv_sems.at[a, j], device_id=dst_dev, device_id_type=MESH)

        def d2d(a, j, src_chip, core):
            return pltpu.make_async_remote_copy(
                src_ref=part(dsts[a].at[src_chip], a, core), dst_ref=part(dsts[a].at[src_chip], a, core),
                send_sem=fwd_send_sems.at[a, j], recv_sem=fwd_recv_sems.at[a, j],
                device_id=sibling, device_id_type=MESH)

        local = [pltpu.make_async_copy(srcs[a], dsts[a].at[me_chip], local_sems.at[a]) for a in range(k_arr)]
        for cp in local:
            cp.start()
        sends = [ici(a, j, me_chip, peer) for a in range(k_arr) for j, peer in enumerate(peers)]
        for cp in sends:
            cp.start()
        forwards = []
        for a in range(k_arr):
            for j, peer in enumerate(peers):
                peer_chip = 2 * peer[0] + peer[1]
                ici(a, j, peer_chip, peer).wait_recv()
                if split[a]:
                    forwards.append(d2d(a, j, peer_chip, c))
                    forwards[-1].start()
        for a in range(k_arr):
            for j, peer in enumerate(peers):
                if split[a]:
                    d2d(a, j, 2 * peer[0] + peer[1], 1 - c).wait_recv()
        for cp in sends + forwards:
            cp.wait_send()
        for cp in local:
            cp.wait()

    any_spec = pl.BlockSpec(memory_space=pl.ANY)
    return pl.pallas_call(
        body, name=name,
        out_shape=[jax.ShapeDtypeStruct((4,) + s.shape, s.dtype) for s in shards],
        in_specs=[any_spec] * k_arr, out_specs=[any_spec] * k_arr,
        scratch_shapes=[pltpu.SemaphoreType.DMA((k_arr, 3))] * 4 + [pltpu.SemaphoreType.DMA((k_arr,))],
    )(*shards)


def _chip_scatter(pieces, name):
    k_arr = len(pieces)

    def body(*refs):
        srcs, dsts = refs[:k_arr], refs[k_arr:2 * k_arr]
        send_sems, recv_sems = refs[2 * k_arr:]
        x, y, c = _coords()
        peers = [(1 - x, y, c), (x, 1 - y, c), (1 - x, 1 - y, c)]
        copies = []
        for a in range(k_arr):
            for j, peer in enumerate(peers):
                copies.append(pltpu.make_async_remote_copy(
                    src_ref=srcs[a].at[2 * peer[0] + peer[1]], dst_ref=dsts[a].at[j],
                    send_sem=send_sems.at[a, j], recv_sem=recv_sems.at[a, j], device_id=peer, device_id_type=MESH))
        for cp in copies:
            cp.start()
        for cp in copies:
            cp.wait_recv()
        for cp in copies:
            cp.wait_send()

    any_spec = pl.BlockSpec(memory_space=pl.ANY)
    return pl.pallas_call(
        body, name=name,
        out_shape=[jax.ShapeDtypeStruct((3,) + p.shape[1:], p.dtype) for p in pieces],
        in_specs=[any_spec] * k_arr, out_specs=[any_spec] * k_arr,
        scratch_shapes=[pltpu.SemaphoreType.DMA((k_arr, 3)), pltpu.SemaphoreType.DMA((k_arr, 3))],
    )(*pieces)


def _sibling_swap(arrs, name):
    k_arr = len(arrs)

    def body(*refs):
        srcs, dsts = refs[:k_arr], refs[k_arr:2 * k_arr]
        send_sems, recv_sems = refs[2 * k_arr:]
        x, y, c = _coords()
        copies = [pltpu.make_async_remote_copy(
            src_ref=srcs[a], dst_ref=dsts[a], send_sem=send_sems.at[a], recv_sem=recv_sems.at[a],
            device_id=(x, y, 1 - c), device_id_type=MESH) for a in range(k_arr)]
        for cp in copies:
            cp.start()
        for cp in copies:
            cp.wait_recv()
        for cp in copies:
            cp.wait_send()

    any_spec = pl.BlockSpec(memory_space=pl.ANY)
    return pl.pallas_call(
        body, name=name,
        out_shape=[jax.ShapeDtypeStruct(a.shape, a.dtype) for a in arrs],
        in_specs=[any_spec] * k_arr, out_specs=[any_spec] * k_arr,
        scratch_shapes=[pltpu.SemaphoreType.DMA((k_arr,)), pltpu.SemaphoreType.DMA((k_arr,))],
    )(*arrs)


def _split_w_in(w_in_t):
    wa = jnp.concatenate([w_in_t[0:1024], w_in_t[1040:1552]], axis=0)
    wb = w_in_t[1552:3088]
    wc = w_in_t[3096:3608]
    wd = jnp.concatenate([w_in_t[1024:1040], w_in_t[3088:3096],
                          jnp.zeros((128 - SMALL_USED, w_in_t.shape[1]), w_in_t.dtype)], axis=0)
    return wa, wb, wc, wd


def _merge_dw_in(dwa, dwb, dwc, dwd):
    return jnp.concatenate([dwa[0:1024], dwd[0:GLA_RANK], dwa[1024:1536], dwb, dwd[GLA_RANK:SMALL_USED], dwc,
                            jnp.zeros((GRAD_PAD_ROWS, dwa.shape[1]), dwa.dtype)], axis=0)


def _local_step(x, mod, w_in16, w_out16, gla_wg, gla_bg, gla_nw, conv_w, a_log, dt_bias, gdn_nw, ln_w, ln_b, tgt):
    bl, seq, _ = x.shape
    n = bl * seq
    x2 = x.reshape(n, D_MODEL)
    tgt2 = tgt.reshape(n, D_MODEL)
    sh3 = mod[:, None, 0:D_MODEL]
    sc3 = 1.0 + mod[:, None, D_MODEL:2 * D_MODEL]
    g1p3 = 1.0 + mod[:, None, 2 * D_MODEL:]
    ws = _split_w_in(w_in16)
    wg = jnp.concatenate([gla_wg, jnp.zeros((128 - GLA_RANK, GLA_QK), F32)], axis=0)
    cw8 = jnp.concatenate([conv_w, jnp.zeros((8 - CONV_K, conv_w.shape[1]), F32)], axis=0)
    alog_v = jnp.zeros((1, 128), F32).at[:, LANE_A:LANE_A + GDN_HEADS].set(a_log)
    dtb_v = jnp.zeros((1, 128), F32).at[:, LANE_A:LANE_A + GDN_HEADS].set(dt_bias)

    pa, pb, pc, pd = _proj_fwd(x2, sc3, sh3, ws, seq)
    ya, st_a = _gla_fwd(pa, pd, wg, gla_bg, gla_nw, bl, seq)
    qkv, gb, conv_out = _gdn_pre_fwd(pb, pd, cw8, alog_v, dtb_v, bl, seq)
    yb, st_b = _gdn_fwd(qkv, gb, pc, gdn_nw, bl, seq)
    dz, dya, dyb, d_wo, d_gate, d_lnw, d_lnb, loss = _out_block(x2, tgt2, ya, yb, g1p3, w_out16, ln_w, ln_b, seq)
    da, dd1, d_wg, d_bg, d_nwa = _gla_bwd(pa, pd, st_a, dya, wg, gla_bg, gla_nw, bl, seq)
    dqkv, dc, dgb, d_nwb = _gdn_bwd(qkv, gb, pc, st_b, dyb, gdn_nw, bl, seq)
    db, dd2, d_cw8, d_alog, d_dtb = _gdn_pre_bwd(pb, conv_out, pd, dqkv, dgb, cw8, alog_v, dtb_v, bl, seq)
    gx, d_sh, d_sc = _proj_bwd_x((da, db, dc, (dd1, dd2)), ws, x2, dz, sc3, seq)
    (dwa,) = _proj_bwd_w(x2, sc3, sh3, [da], seq, w_in16.dtype, "proj_bwd_w_a")
    dwb, dwc, dwd = _proj_bwd_w(x2, sc3, sh3, [db, dc, (dd1, dd2)], seq, w_in16.dtype, "proj_bwd_w_bcd")
    grads = dict(
        w_in=_merge_dw_in(dwa, dwb, dwc, dwd),
        w_out=d_wo,
        gla_w_gate_up=d_wg[0:GLA_RANK, :],
        gla_b_gate=d_bg,
        gla_norm_w=d_nwa,
        gdn_conv_w=d_cw8[0:CONV_K, :],
        gdn_a_log=d_alog[:, LANE_A:LANE_A + GDN_HEADS],
        gdn_dt_bias=d_dtb[:, LANE_A:LANE_A + GDN_HEADS],
        gdn_norm_w=d_nwb,
        ln_w=d_lnw,
        ln_b=d_lnb,
        mod=jnp.concatenate([d_sh[:, 0, :], d_sc[:, 0, :], d_gate[:, 0, :]], axis=1),
    )
    return loss, gx.reshape(bl, seq, D_MODEL), grads


_SMALL = (("gla_b_gate", 256), ("gla_norm_w", 128), ("gdn_a_log", 4), ("gdn_dt_bias", 4), ("gdn_norm_w", 128),
          ("ln_w", 1024), ("ln_b", 1024), ("gla_w_gate_up", 16 * 256), ("gdn_conv_w", 4 * 1536), ("mod", 2 * 3072))


def _pack_small(grads):
    flat = jnp.concatenate([grads[k].reshape(-1) for k, _ in _SMALL])
    total = sum(sz for _, sz in _SMALL)
    rows = -(-total // 1024) * 8
    return jnp.concatenate([flat, jnp.zeros((rows * 128 - total,), F32)]).reshape(rows, 128)


def _unpack_small(flat):
    out, pos = {}, 0
    for k, sz in _SMALL:
        out[k] = flat[pos:pos + sz]
        pos += sz
    return out


def kernel(x, c, w_ada, b_ada, w_in, gla_w_gate_up, gla_b_gate, gla_norm_w, gdn_conv_w, gdn_a_log, gdn_dt_bias, gdn_norm_w, w_out, ln_w, ln_b, loss_target, m_w_ada, m_b_ada, m_w_in, m_gla_w_gate_up, m_gla_b_gate, m_gla_norm_w, m_gdn_conv_w, m_gdn_a_log, m_gdn_dt_bias, m_gdn_norm_w, m_w_out, m_ln_w, m_ln_b, v_w_ada, v_b_ada, v_w_in, v_gla_w_gate_up, v_gla_b_gate, v_gla_norm_w, v_gdn_conv_w, v_gdn_a_log, v_gdn_dt_bias, v_gdn_norm_w, v_w_out, v_ln_w, v_ln_b):
    ix, iy, ic = _coords()
    chip = 2 * ix + iy
    dev = 4 * ix + 2 * iy + ic
    bl = x.shape[0]
    ndev = 8

    c_all = _all_gather8(c.reshape(8, -1), "gather_c").reshape(ndev * bl, D_MODEL)
    ada_cols = w_ada.shape[2]
    b_blk = lax.dynamic_slice_in_dim(b_ada, chip * ada_cols, ada_cols, axis=1)
    mod_blk = _mod_block(c_all, w_ada[0], b_blk)
    mod_g = _all_gather8(mod_blk, "gather_mod").reshape(ndev, ndev * bl, ada_cols)
    mod_all = jnp.concatenate([mod_g[2 * j] for j in range(4)], axis=1)
    mod = lax.dynamic_slice_in_dim(mod_all, dev * bl, bl, axis=0)

    w_in_g, w_out_g, wg_g, cw_g = _chip_gather(
        [jnp.transpose(w_in[0]).astype(BF16), w_out[0].astype(BF16), gla_w_gate_up[0], gdn_conv_w[0]],
        [True, True, False, False], "gather_weights")
    w_in16 = w_in_g.reshape(IN_COLS, D_MODEL)
    w_out16 = w_out_g.reshape(D_MODEL, D_MODEL)
    gla_wg = jnp.concatenate([wg_g[j] for j in range(4)], axis=1)
    conv_w = jnp.concatenate([cw_g[j] for j in range(4)], axis=1)

    loss, grad_x, gr = _local_step(x, mod, w_in16, w_out16, gla_wg, gla_b_gate, gla_norm_w, conv_w,
                                   gdn_a_log, gdn_dt_bias, gdn_norm_w, ln_w, ln_b, loss_target)
    loss = lax.psum(loss[0, 0], ("x", "y", "c"))

    packed = _pack_small(gr)
    prow = packed.shape[0]
    gathered = _all_gather8(packed, "gather_small").reshape(ndev, prow, 128)
    small = _unpack_small(_sum_leading(gathered, "sum_small").reshape(-1))
    mod_rows = gathered.reshape(ndev, prow * 128)[:, sum(sz for _, sz in _SMALL[:-1]):][:, :bl * 3 * D_MODEL]
    dmod_all = mod_rows.reshape(ndev * bl, 3 * D_MODEL)
    dmod_blk = lax.dynamic_slice_in_dim(dmod_all, chip * ada_cols, ada_cols, axis=1)
    g_w_ada, g_b_ada = _ada_grads(c_all, dmod_all, dmod_blk)
    wg_cols = gla_w_gate_up.shape[2]
    g_wg = lax.dynamic_slice_in_dim(small["gla_w_gate_up"].reshape(GLA_RANK, GLA_QK), chip * wg_cols, wg_cols, axis=1)
    cw_cols = gdn_conv_w.shape[2]
    g_cw = lax.dynamic_slice_in_dim(small["gdn_conv_w"].reshape(CONV_K, 3 * GDN_WIDTH), chip * cw_cols, cw_cols, axis=1)

    in_feats = w_in.shape[2]
    out_rows = w_out.shape[1]
    p_in = gr["w_in"]
    p_out = gr["w_out"].reshape(4, out_rows, D_MODEL)
    h_in, h_out = D_MODEL // 2, out_rows // 2
    mine_in = lax.dynamic_slice_in_dim(p_in, ic * h_in, h_in, axis=1)
    mine_out = lax.dynamic_slice_in_dim(p_out, ic * h_out, h_out, axis=1)
    theirs_in = lax.dynamic_slice_in_dim(p_in, (1 - ic) * h_in, h_in, axis=1)
    theirs_out = lax.dynamic_slice_in_dim(p_out, (1 - ic) * h_out, h_out, axis=1)
    got_in, got_out = _sibling_swap([theirs_in, theirs_out], "swap_halves")
    chip_in, chip_in16 = _chip_sum_blocks(mine_in, got_in, in_feats, 4, "chip_sum_in")
    chip_out, chip_out16 = _add_n([mine_out.reshape(4 * h_out, D_MODEL), got_out.reshape(4 * h_out, D_MODEL)],
                                  "chip_sum_out", (F32, BF16))
    chip_out = chip_out.reshape(4, h_out, D_MODEL)
    rs_in, rs_out = _chip_scatter([chip_in16, chip_out16.reshape(4, h_out, D_MODEL)], "scatter_grads")
    own_in = lax.dynamic_index_in_dim(chip_in, chip, axis=0, keepdims=False)
    own_out = lax.dynamic_index_in_dim(chip_out, chip, axis=0, keepdims=False)
    (half_in,) = _add_n([own_in, rs_in[0], rs_in[1], rs_in[2]], "reduce_in")
    (half_out,) = _add_n([own_out, rs_out[0], rs_out[1], rs_out[2]], "reduce_out")
    sib_in, sib_out = _sibling_swap([half_in, half_out], "swap_result")
    g_w_in_t = jnp.where(ic == 0, jnp.concatenate([half_in, sib_in], axis=1),
                         jnp.concatenate([sib_in, half_in], axis=1))[0:in_feats]
    g_w_out = jnp.where(ic == 0, jnp.concatenate([half_out, sib_out], axis=0),
                        jnp.concatenate([sib_out, half_out], axis=0))

    grads = dict(
        w_ada=g_w_ada[None], b_ada=g_b_ada, w_in=g_w_in_t, gla_w_gate_up=g_wg[None],
        gla_b_gate=small["gla_b_gate"].reshape(1, -1), gla_norm_w=small["gla_norm_w"].reshape(1, -1),
        gdn_conv_w=g_cw[None], gdn_a_log=small["gdn_a_log"].reshape(1, -1),
        gdn_dt_bias=small["gdn_dt_bias"].reshape(1, -1), gdn_norm_w=small["gdn_norm_w"].reshape(1, -1),
        w_out=g_w_out[None], ln_w=small["ln_w"].reshape(1, -1), ln_b=small["ln_b"].reshape(1, -1))
    weights = dict(w_ada=w_ada, b_ada=b_ada, w_in=w_in, gla_w_gate_up=gla_w_gate_up, gla_b_gate=gla_b_gate,
                   gla_norm_w=gla_norm_w, gdn_conv_w=gdn_conv_w, gdn_a_log=gdn_a_log, gdn_dt_bias=gdn_dt_bias,
                   gdn_norm_w=gdn_norm_w, w_out=w_out, ln_w=ln_w, ln_b=ln_b)
    m_in = dict(w_ada=m_w_ada, b_ada=m_b_ada, w_in=m_w_in, gla_w_gate_up=m_gla_w_gate_up, gla_b_gate=m_gla_b_gate,
                gla_norm_w=m_gla_norm_w, gdn_conv_w=m_gdn_conv_w, gdn_a_log=m_gdn_a_log, gdn_dt_bias=m_gdn_dt_bias,
                gdn_norm_w=m_gdn_norm_w, w_out=m_w_out, ln_w=m_ln_w, ln_b=m_ln_b)
    v_in = dict(w_ada=v_w_ada, b_ada=v_b_ada, w_in=v_w_in, gla_w_gate_up=v_gla_w_gate_up, gla_b_gate=v_gla_b_gate,
                gla_norm_w=v_gla_norm_w, gdn_conv_w=v_gdn_conv_w, gdn_a_log=v_gdn_a_log, gdn_dt_bias=v_gdn_dt_bias,
                gdn_norm_w=v_gdn_norm_w, w_out=v_w_out, ln_w=v_ln_w, ln_b=v_ln_b)
    names = list(weights)
    delta, new_m, new_v = {}, {}, {}
    for nm in names:
        shp = weights[nm].shape
        if nm == "w_in":
            to2d = lambda t: jnp.transpose(t[0])
            from2d = lambda t: jnp.transpose(t)[None]
            g2d = grads[nm]
        else:
            to2d = lambda t: t.reshape(-1, shp[-1])
            from2d = lambda t: t.reshape(shp)
            g2d = to2d(grads[nm])
        d, a, b = _adamw(to2d(weights[nm]), g2d, to2d(m_in[nm]), to2d(v_in[nm]), "adamw_" + nm)
        delta[nm], new_m[nm], new_v[nm] = from2d(d), from2d(a), from2d(b)
        grads[nm] = from2d(g2d)
    return (loss, grad_x, *[grads[k] for k in names], *[delta[k] for k in names],
            *[new_m[k] for k in names], *[new_v[k] for k in names])
```

```python
import functools

import jax
import jax.numpy as jnp
from jax import lax
from jax.experimental import pallas as pl
from jax.experimental.pallas import tpu as pltpu

F32 = jnp.float32
BF16 = jnp.bfloat16
HI = lax.Precision.HIGH
INV_PREC = None
MESH = pl.DeviceIdType.MESH

D_MODEL = 1024
GLA_HEADS = 4
GLA_DK = 64
GLA_DV = 128
GLA_QK = 256
GLA_WIDTH = 512
GLA_RANK = 16
GLA_GATE_NORM = 16.0
GDN_HEADS = 4
GDN_DK = 128
GDN_WIDTH = 512
CONV_K = 4
CHUNK = 64
LN_EPS = 1e-5
RMS_EPS = 1e-6
ALPHA = 2.0 ** 0.25
IN_COLS = 3608

LANE_A = GLA_RANK
LANE_B = GLA_RANK + GDN_HEADS
SMALL_USED = GLA_RANK + 2 * GDN_HEADS

ADAM_LR = 0.001
ADAM_B1 = 0.9
ADAM_B2 = 0.999
ADAM_EPS = 1e-08
ADAM_WD = 0.01
ADAM_STEP = 10

VMEM_LIMIT = 56 * 1024 * 1024


def _iota(shape, dim):
    return lax.broadcasted_iota(jnp.int32, shape, dim)


def _dot(a, b, prec=None):
    return lax.dot_general(a, b, (((1,), (0,)), ((), ())), precision=prec, preferred_element_type=F32)


def _dot_nt(a, b, prec=None):
    return lax.dot_general(a, b, (((1,), (1,)), ((), ())), precision=prec, preferred_element_type=F32)


def _dot_tn(a, b, prec=None):
    return lax.dot_general(a, b, (((0,), (0,)), ((), ())), precision=prec, preferred_element_type=F32)


def _log_sigmoid(z):
    return jnp.minimum(z, 0.0) - jnp.log1p(jnp.exp(-jnp.abs(z)))


def _softplus(z):
    return jnp.maximum(z, 0.0) + jnp.log1p(jnp.exp(-jnp.abs(z)))


def _silu(z):
    return z * jax.nn.sigmoid(z)


def _rms_gate(o, nw, og):
    return o * lax.rsqrt(jnp.mean(o * o, axis=-1, keepdims=True) + RMS_EPS) * nw * _silu(og)


def _params(*sem):
    return pltpu.CompilerParams(dimension_semantics=sem, vmem_limit_bytes=VMEM_LIMIT)


GLA_PAIRS = GLA_HEADS // 2


def _gla_chunk(qs, ks, lrs, vs, ogs, ss, wgs, bgs, nw):
    c = qs[0].shape[0]
    pair_units = [divmod(i, GLA_PAIRS) for i in range(len(qs))]
    head_units = [(i // GLA_HEADS, i // GLA_HEADS * GLA_PAIRS + (i % GLA_HEADS) // 2, i % 2) for i in range(len(vs))]
    row, col = _iota((c, c), 0), _iota((c, c), 1)
    causal = row >= col
    first_half = (_iota((c, 1), 0) < c // 2).astype(F32)
    lane = _iota((1, 128), 1)
    masks = [(lane < GLA_DK).astype(F32), (lane >= GLA_DK).astype(F32)]
    gs = [_log_sigmoid(_dot(lrs[e], wgs[p]) + bgs[p]) * (1.0 / GLA_GATE_NORM) for e, p in pair_units]
    bs = [_dot(causal.astype(F32), g, HI) for g in gs]
    b_ref = [jnp.sum(g * first_half, axis=0, keepdims=True) for g in gs]
    b_last = [jnp.sum(g, axis=0, keepdims=True) for g in gs]
    qsc = [q * (GLA_DK ** -0.5) for q in qs]
    qe = [q * jnp.exp(b - br) for q, b, br in zip(qsc, bs, b_ref)]
    ke = [k * jnp.exp(br - b) for k, b, br in zip(ks, bs, b_ref)]
    qb = [q * jnp.exp(b) for q, b in zip(qsc, bs)]
    kd = [k * jnp.exp(bl_ - b) for k, b, bl_ in zip(ks, bs, b_last)]
    att = [jnp.where(causal, _dot_nt(qe[u] * masks[half], ke[u]), 0.0) for _, u, half in head_units]
    o_inter = [_dot_nt(qb[u] * masks[half], ss[u]) for _, u, half in head_units]
    os_ = [_dot(a, v) + oi for a, v, oi in zip(att, vs, o_inter)]
    upd = [_dot_tn(v, kd[u] * masks[half]) for (_, u, half), v in zip(head_units, vs)]
    s_new = [s * jnp.exp(bl_) + upd[2 * u] + upd[2 * u + 1] for u, (s, bl_) in enumerate(zip(ss, b_last))]
    ys = [_rms_gate(o, nw, og) for o, og in zip(os_, ogs)]
    return ys, s_new


def _unit_lower_inverse_chain(a_list):
    c = a_list[0].shape[0]
    eye = (_iota((c, c), 0) == _iota((c, c), 1)).astype(F32)
    ps = [-a for a in a_list]
    ts = [eye + p for p in ps]
    for _ in range(max(c.bit_length() - 2, 0)):
        ps = [_dot(p, p, INV_PREC) for p in ps]
        ts = [t + _dot(t, p, INV_PREC) for t, p in zip(ts, ps)]
    return ts


@jax.custom_vjp
def _unit_lower_inverse(a_list):
    return _unit_lower_inverse_chain(a_list)


def _unit_lower_inverse_fwd(a_list):
    ts = _unit_lower_inverse_chain(a_list)
    return ts, ts


def _unit_lower_inverse_bwd(ts, dts):
    xs = [_dot_nt(dt, t, INV_PREC) for dt, t in zip(dts, ts)]
    return ([-_dot_tn(t, x, INV_PREC) for t, x in zip(ts, xs)],)


_unit_lower_inverse.defvjp(_unit_lower_inverse_fwd, _unit_lower_inverse_bwd)


@jax.custom_vjp
def _unit_lower_inverse_known(a_list, ts):
    return ts


def _unit_lower_inverse_known_fwd(a_list, ts):
    return ts, ts


def _unit_lower_inverse_known_bwd(ts, dts):
    return _unit_lower_inverse_bwd(ts, dts) + ([jnp.zeros_like(t) for t in ts],)


_unit_lower_inverse_known.defvjp(_unit_lower_inverse_known_fwd, _unit_lower_inverse_known_bwd)


def _gdn_prep_units(qs, ks, vs, gbs, t_known=None):
    c = qs[0].shape[0]
    units = [divmod(i, GDN_HEADS) for i in range(len(qs))]
    row, col = _iota((c, c), 0), _iota((c, c), 1)
    causal, strict = row >= col, row > col
    lane = _iota((1, 128), 1)
    d_alls = [_dot(causal.astype(F32), gb, HI) for gb in gbs]
    g_c, beta_c, d_c = [], [], []
    for r, h in units:
        sel_a = (lane == LANE_A + h).astype(F32)
        g_c.append(jnp.sum(gbs[r] * sel_a, axis=-1, keepdims=True))
        beta_c.append(jnp.sum(gbs[r] * (lane == LANE_B + h).astype(F32), axis=-1, keepdims=True))
        d_c.append(jnp.sum(d_alls[r] * sel_a, axis=-1, keepdims=True))
    d_last = [jnp.sum(g, axis=0, keepdims=True) for g in g_c]
    d_diff = [jnp.broadcast_to(d, (c, c)) - jnp.broadcast_to(d, (c, c)).T for d in d_c]
    decay_mat = [jnp.where(causal, jnp.exp(jnp.where(causal, dd, 0.0)), 0.0) for dd in d_diff]
    kb = [k * b for k, b in zip(ks, beta_c)]
    a = [jnp.where(strict, _dot_nt(kbi, k) * dm, 0.0) for kbi, k, dm in zip(kb, ks, decay_mat)]
    t = _unit_lower_inverse(a) if t_known is None else _unit_lower_inverse_known(a, t_known)
    u = [_dot(ti, v * b) for ti, v, b in zip(t, vs, beta_c)]
    w = [_dot(ti, kbi * jnp.exp(d)) for ti, kbi, d in zip(t, kb, d_c)]
    qk = [jnp.where(causal, _dot_nt(q, k) * dm, 0.0) for q, k, dm in zip(qs, ks, decay_mat)]
    q_dec = [q * jnp.exp(d) for q, d in zip(qs, d_c)]
    k_dec = [k * jnp.exp(dl - d) for k, dl, d in zip(ks, d_last, d_c)]
    gamma = [jnp.exp(dl) for dl in d_last]
    return u, w, qk, q_dec, k_dec, gamma, t


def _sum_all(t):
    return jnp.sum(jnp.sum(t, axis=-1, keepdims=True), axis=0, keepdims=True)


def _gdn_pre_elem(ps, ab, alog_v, dtb_v):
    outs = []
    for j, p in enumerate(ps):
        s = _silu(p)
        if j < 2 * GDN_HEADS:
            s = s * lax.rsqrt(jnp.sum(s * s, axis=-1, keepdims=True) + RMS_EPS)
        if j < GDN_HEADS:
            s = s * (GDN_DK ** -0.5)
        outs.append(s)
    lane = _iota((1, 128), 1)
    is_a = (lane >= LANE_A) & (lane < LANE_A + GDN_HEADS)
    is_b = (lane >= LANE_B) & (lane < LANE_B + GDN_HEADS)
    g = -jnp.exp(alog_v) * _softplus(ab + dtb_v)
    gb = jnp.where(is_a, g, jnp.where(is_b, jax.nn.sigmoid(ab), 0.0))
    return tuple(outs) + (gb,)


def _proj_fwd(x2, sc3, sh3, ws, seq, tm=256):
    n = x2.shape[0]
    tpe = seq // tm
    nw = len(ws)

    def body(x_ref, sc_ref, sh_ref, *refs):
        h = (x_ref[...] * sc_ref[0] + sh_ref[0]).astype(ws[0].dtype)
        for w_ref, o_ref in zip(refs[:nw], refs[nw:]):
            o_ref[...] = _dot_nt(h, w_ref[...])

    row = lambda i: (i, 0)
    per_ex = pl.BlockSpec((1, 1, D_MODEL), lambda i: (i // tpe, 0, 0))
    return pl.pallas_call(
        body, name="proj_fwd", grid=(n // tm,),
        in_specs=[pl.BlockSpec((tm, D_MODEL), row), per_ex, per_ex]
        + [pl.BlockSpec(w.shape, lambda i: (0, 0)) for w in ws],
        out_specs=[pl.BlockSpec((tm, w.shape[0]), row) for w in ws],
        out_shape=[jax.ShapeDtypeStruct((n, w.shape[0]), F32) for w in ws],
        compiler_params=_params("parallel"),
    )(x2, sc3, sh3, *ws)


def _gla_fwd(pa, pd, wg, bg, nw, bl, seq):
    n = pa.shape[0]
    nc = seq // CHUNK
    heads = [(e, h, slice(h * 128, (h + 1) * 128)) for e in range(bl) for h in range(GLA_HEADS)]
    pair_cols = [slice(p * 128, (p + 1) * 128) for p in range(GLA_PAIRS)]
    pairs = [(e, p, pair_cols[p]) for e in range(bl) for p in range(GLA_PAIRS)]

    def body(q_ref, k_ref, v_ref, og_ref, lr_ref, wg_ref, bg_ref, nw_ref, y_ref, st_ref, s_scr):
        @pl.when(pl.program_id(0) == 0)
        def _():
            s_scr[...] = jnp.zeros_like(s_scr)

        ss = [s_scr[e, p] for e, p, _ in pairs]
        for (e, p, _), s in zip(pairs, ss):
            st_ref[e, 0, p] = s
        ys, s_new = _gla_chunk([q_ref[e, :, cols] for e, _, cols in pairs], [k_ref[e, :, cols] for e, _, cols in pairs],
                               [lr_ref[e] for e in range(bl)],
                               [v_ref[e, :, cols] for e, _, cols in heads], [og_ref[e, :, cols] for e, _, cols in heads],
                               ss, [wg_ref[:, cols] for cols in pair_cols], [bg_ref[:, cols] for cols in pair_cols],
                               nw_ref[...])
        for (e, h, cols), y in zip(heads, ys):
            y_ref[e, :, cols] = y
        for (e, p, _), s in zip(pairs, s_new):
            s_scr[e, p] = s

    tok = lambda w, j: pl.BlockSpec((bl, CHUNK, w), lambda i: (0, i, j))
    const = lambda i: (0, 0)
    pa3 = pa.reshape(bl, seq, 1536)
    y, st = pl.pallas_call(
        body, name="gla_fwd", grid=(nc,),
        in_specs=[tok(256, 0), tok(256, 1), tok(512, 1), tok(512, 2), tok(128, 0),
                  pl.BlockSpec(wg.shape, const), pl.BlockSpec(bg.shape, const), pl.BlockSpec(nw.shape, const)],
        out_specs=[tok(512, 0), pl.BlockSpec((bl, 1, GLA_PAIRS, 128, 128), lambda i: (0, i, 0, 0, 0))],
        out_shape=[jax.ShapeDtypeStruct((bl, seq, 512), F32),
                   jax.ShapeDtypeStruct((bl, nc, GLA_PAIRS, 128, 128), F32)],
        scratch_shapes=[pltpu.VMEM((bl, GLA_PAIRS, 128, 128), F32)],
        compiler_params=_params("arbitrary"),
    )(pa3, pa3, pa3, pa3, pd.reshape(bl, seq, 128), wg, bg, nw)
    return y.reshape(n, 512), st


def _gla_bwd(pa, pd, st, dya, wg, bg, nw, bl, seq):
    n = pa.shape[0]
    nc = seq // CHUNK
    heads = [(e, h, slice(h * 128, (h + 1) * 128)) for e in range(bl) for h in range(GLA_HEADS)]
    pair_cols = [slice(p * 128, (p + 1) * 128) for p in range(GLA_PAIRS)]
    pairs = [(e, p, pair_cols[p]) for e in range(bl) for p in range(GLA_PAIRS)]

    def body(q_ref, k_ref, v_ref, og_ref, lr_ref, st_ref, dy_ref, wg_ref, bg_ref, nw_ref,
             da_ref, dd_ref, dwg_ref, dbg_ref, dnw_ref, ds_scr):
        @pl.when(pl.program_id(0) == 0)
        def _():
            dwg_ref[...] = jnp.zeros_like(dwg_ref)
            dbg_ref[...] = jnp.zeros_like(dbg_ref)
            dnw_ref[...] = jnp.zeros_like(dnw_ref)
            ds_scr[...] = jnp.zeros_like(ds_scr)

        _, vjp = jax.vjp(_gla_chunk, [q_ref[e, :, cols] for e, _, cols in pairs],
                         [k_ref[e, :, cols] for e, _, cols in pairs], [lr_ref[e] for e in range(bl)],
                         [v_ref[e, :, cols] for e, _, cols in heads], [og_ref[e, :, cols] for e, _, cols in heads],
                         [st_ref[e, 0, p] for e, p, _ in pairs],
                         [wg_ref[:, cols] for cols in pair_cols], [bg_ref[:, cols] for cols in pair_cols], nw_ref[...])
        dq, dk, dlr, dv, dog, ds, dwg, dbg, dnw = vjp(([dy_ref[e, :, cols] for e, _, cols in heads],
                                                         [ds_scr[e, p] for e, p, _ in pairs]))
        for e in range(bl):
            dd_ref[e] = dlr[e]
        for i, (e, p, cols) in enumerate(pairs):
            da_ref[e, :, cols] = dq[i]
            da_ref[e, :, GLA_QK + p * 128:GLA_QK + (p + 1) * 128] = dk[i]
            ds_scr[e, p] = ds[i]
        for i, (e, h, _) in enumerate(heads):
            da_ref[e, :, 512 + h * 128:512 + (h + 1) * 128] = dv[i]
            da_ref[e, :, 1024 + h * 128:1024 + (h + 1) * 128] = dog[i]
        for p, cols in enumerate(pair_cols):
            dwg_ref[:, cols] += dwg[p]
            dbg_ref[:, cols] += dbg[p]
        dnw_ref[...] += dnw

    tok = lambda w, j: pl.BlockSpec((bl, CHUNK, w), lambda i: (0, nc - 1 - i, j))
    const = lambda i: (0, 0)
    pa3 = pa.reshape(bl, seq, 1536)
    da, dd, dwg, dbg, dnw = pl.pallas_call(
        body, name="gla_bwd", grid=(nc,),
        in_specs=[tok(256, 0), tok(256, 1), tok(512, 1), tok(512, 2), tok(128, 0),
                  pl.BlockSpec((bl, 1, GLA_PAIRS, 128, 128), lambda i: (0, nc - 1 - i, 0, 0, 0)), tok(512, 0),
                  pl.BlockSpec(wg.shape, const), pl.BlockSpec(bg.shape, const), pl.BlockSpec(nw.shape, const)],
        out_specs=[tok(1536, 0), tok(128, 0),
                   pl.BlockSpec(wg.shape, const), pl.BlockSpec(bg.shape, const), pl.BlockSpec(nw.shape, const)],
        out_shape=[jax.ShapeDtypeStruct((bl, seq, 1536), F32), jax.ShapeDtypeStruct((bl, seq, 128), F32),
                   jax.ShapeDtypeStruct(wg.shape, F32), jax.ShapeDtypeStruct(bg.shape, F32),
                   jax.ShapeDtypeStruct(nw.shape, F32)],
        scratch_shapes=[pltpu.VMEM((bl, GLA_PAIRS, 128, 128), F32)],
        compiler_params=_params("arbitrary"),
    )(pa3, pa3, pa3, pa3, pd.reshape(bl, seq, 128), st, dya.reshape(bl, seq, 512), wg, bg, nw)
    return da.reshape(n, 1536), dd.reshape(n, 128), dwg, dbg, dnw


def _conv_taps(buf_ref, w_ref, base, rows):
    acc = w_ref[0:1, :] * buf_ref[pl.ds(base, rows), :]
    for k in range(1, CONV_K):
        acc = acc + w_ref[k:k + 1, :] * buf_ref[pl.ds(base + k, rows), :]
    return acc


def _gdn_pre_fwd(pb, pd, cw8, alog_v, dtb_v, bl, seq, tm=256):
    n = pb.shape[0]
    tpe = seq // tm
    t8 = tm // 8

    def body(u_ref, prev_ref, ab_ref, w_ref, al_ref, dt_ref, qkv_ref, gb_ref, p_ref, buf):
        i = pl.program_id(0)
        keep = (i % tpe != 0).astype(F32)
        buf[0:8, :] = prev_ref[...] * keep
        buf[8:8 + tm, :] = u_ref[...]
        p = _conv_taps(buf, w_ref, 8 - (CONV_K - 1), tm)
        p_ref[...] = p
        ps = [p[:, j * 128:(j + 1) * 128] for j in range(12)]
        outs = _gdn_pre_elem(ps, ab_ref[...], al_ref[...], dt_ref[...])
        for j in range(12):
            qkv_ref[:, j * 128:(j + 1) * 128] = outs[j]
        gb_ref[...] = outs[12]

    row = lambda i: (i, 0)
    const = lambda i: (0, 0)
    return pl.pallas_call(
        body, name="gdn_pre_fwd", grid=(n // tm,),
        in_specs=[pl.BlockSpec((tm, 1536), row),
                  pl.BlockSpec((8, 1536), lambda i: (jnp.maximum(i * t8 - 1, 0), 0)),
                  pl.BlockSpec((tm, 128), row),
                  pl.BlockSpec((8, 1536), const), pl.BlockSpec((1, 128), const), pl.BlockSpec((1, 128), const)],
        out_specs=[pl.BlockSpec((tm, 1536), row), pl.BlockSpec((tm, 128), row), pl.BlockSpec((tm, 1536), row)],
        out_shape=[jax.ShapeDtypeStruct((n, 1536), F32), jax.ShapeDtypeStruct((n, 128), F32),
                   jax.ShapeDtypeStruct((n, 1536), F32)],
        scratch_shapes=[pltpu.VMEM((tm + 8, 1536), F32)],
        compiler_params=_params("parallel"),
    )(pb, pb, pd, cw8, alog_v, dtb_v)


def _gdn_pre_bwd(pb, conv_out, pd, dqkv, dgb, cw8, alog_v, dtb_v, bl, seq, tm=256):
    n = pb.shape[0]
    tpe = seq // tm
    t8 = tm // 8
    nb8 = n // 8
    ext = tm + 8

    def body(u_ref, p_ref, pn_ref, ab_ref, abn_ref, dq_ref, dqn_ref, dgb_ref, w_ref, al_ref, dt_ref,
             du_ref, dab_ref, dw_ref, dal_ref, ddt_ref, dpbuf):
        i = pl.program_id(0)

        @pl.when(i == 0)
        def _():
            dw_ref[...] = jnp.zeros_like(dw_ref)
            dal_ref[...] = jnp.zeros_like(dal_ref)
            ddt_ref[...] = jnp.zeros_like(ddt_ref)

        keep_next = (i % tpe != tpe - 1).astype(F32)
        ps = [jnp.concatenate([p_ref[:, j * 128:(j + 1) * 128], pn_ref[:, j * 128:(j + 1) * 128]], axis=0)
              for j in range(12)]
        ab = jnp.concatenate([ab_ref[...], abn_ref[...]], axis=0)
        _, vjp = jax.vjp(_gdn_pre_elem, ps, ab, al_ref[...], dt_ref[...])
        zeros8 = jnp.zeros((8, 128), F32)
        cts = tuple(jnp.concatenate([dq_ref[:, j * 128:(j + 1) * 128],
                                     dqn_ref[:, j * 128:(j + 1) * 128] * keep_next], axis=0) for j in range(12))
        cts += (jnp.concatenate([dgb_ref[...], zeros8], axis=0),)
        dps, dab, dal, ddt = vjp(cts)
        for j in range(12):
            dpbuf[:, j * 128:(j + 1) * 128] = dps[j]
        dab_ref[...] = dab[0:tm, :]
        dal_ref[...] += dal
        ddt_ref[...] += ddt
        u = u_ref[...]
        du = None
        for k in range(CONV_K):
            dp_k = dpbuf[pl.ds(CONV_K - 1 - k, tm), :]
            term = w_ref[k:k + 1, :] * dp_k
            du = term if du is None else du + term
            dw_ref[k:k + 1, :] += jnp.sum(u * dp_k, axis=0, keepdims=True)
        du_ref[...] = du

    row = lambda i: (i, 0)
    next8 = lambda i: (jnp.minimum((i + 1) * t8, nb8 - 1), 0)
    const = lambda i: (0, 0)
    return pl.pallas_call(
        body, name="gdn_pre_bwd", grid=(n // tm,),
        in_specs=[pl.BlockSpec((tm, 1536), row), pl.BlockSpec((tm, 1536), row), pl.BlockSpec((8, 1536), next8),
                  pl.BlockSpec((tm, 128), row), pl.BlockSpec((8, 128), next8),
                  pl.BlockSpec((tm, 1536), row), pl.BlockSpec((8, 1536), next8),
                  pl.BlockSpec((tm, 128), row),
                  pl.BlockSpec((8, 1536), const), pl.BlockSpec((1, 128), const), pl.BlockSpec((1, 128), const)],
        out_specs=[pl.BlockSpec((tm, 1536), row), pl.BlockSpec((tm, 128), row),
                   pl.BlockSpec((8, 1536), const), pl.BlockSpec((1, 128), const), pl.BlockSpec((1, 128), const)],
        out_shape=[jax.ShapeDtypeStruct((n, 1536), F32), jax.ShapeDtypeStruct((n, 128), F32),
                   jax.ShapeDtypeStruct((8, 1536), F32), jax.ShapeDtypeStruct((1, 128), F32),
                   jax.ShapeDtypeStruct((1, 128), F32)],
        scratch_shapes=[pltpu.VMEM((ext, 1536), F32)],
        compiler_params=_params("arbitrary"),
    )(pb, conv_out, conv_out, pd, pd, dqkv, dqkv, dgb, cw8, alog_v, dtb_v)


GDN_PREP_CHUNKS = 2


def _head_cols(ref, rows, base=0):
    return [ref[rows, base + h * 128:base + (h + 1) * 128] for h in range(GDN_HEADS)]


def _gdn_prep(qkv, gb):
    n = qkv.shape[0]
    r_per = GDN_PREP_CHUNKS
    tm = r_per * CHUNK

    def body(q_ref, k_ref, v_ref, gb_ref, u_ref, w_ref, qd_ref, kd_ref, qk_ref, t_ref, gam_ref):
        rowid = _iota((8, 128), 0)
        chunk_rows = [slice(r * CHUNK, (r + 1) * CHUNK) for r in range(r_per)]
        gather = lambda ref: [t for rows in chunk_rows for t in _head_cols(ref, rows)]
        u, w, qk, qd, kd, gamma, tinv = _gdn_prep_units(gather(q_ref), gather(k_ref), gather(v_ref),
                                                        [gb_ref[rows, :] for rows in chunk_rows])
        for r, rows in enumerate(chunk_rows):
            gam = jnp.zeros((8, 128), F32)
            for h in range(GDN_HEADS):
                i = r * GDN_HEADS + h
                cols = slice(h * 128, (h + 1) * 128)
                u_ref[rows, cols] = u[i]
                w_ref[rows, cols] = w[i]
                qd_ref[rows, cols] = qd[i]
                kd_ref[rows, cols] = kd[i]
                qk_ref[r, h] = qk[i]
                t_ref[r, h] = tinv[i]
                gam = jnp.where(rowid == h, gamma[i], gam)
            gam_ref[r] = gam

    tok = lambda j: pl.BlockSpec((tm, 512), lambda i: (i, j))
    return pl.pallas_call(
        body, name="gdn_prep", grid=(n // tm,),
        in_specs=[tok(0), tok(1), tok(2), pl.BlockSpec((tm, 128), lambda i: (i, 0))],
        out_specs=[tok(0)] * 4 + [pl.BlockSpec((r_per, GDN_HEADS, CHUNK, CHUNK), lambda i: (i, 0, 0, 0))] * 2
        + [pl.BlockSpec((r_per, 8, 128), lambda i: (i, 0, 0))],
        out_shape=[jax.ShapeDtypeStruct((n, 512), F32)] * 4
        + [jax.ShapeDtypeStruct((n // CHUNK, GDN_HEADS, CHUNK, CHUNK), F32)] * 2
        + [jax.ShapeDtypeStruct((n // CHUNK, 8, 128), F32)],
        compiler_params=_params("parallel"),
    )(qkv, qkv, qkv, gb)


def _gdn_fwd(qkv, gb, pc, nw, bl, seq):
    n = qkv.shape[0]
    nc = seq // CHUNK
    u, w, qd, kd, qk, tinv, gam = _gdn_prep(qkv, gb)
    tok3 = lambda t: t.reshape(bl, seq, 512)
    qk5 = qk.reshape(bl, nc, GDN_HEADS, CHUNK, CHUNK)
    gam4 = gam.reshape(bl, nc, 8, 128)

    def body(u_ref, w_ref, qd_ref, kd_ref, qk_ref, gam_ref, og_ref, nw_ref, o_ref, y_ref, vn_ref, st_ref, s_scr):
        @pl.when(pl.program_id(0) == 0)
        def _():
            s_scr[...] = jnp.zeros_like(s_scr)

        units = [(b, h, slice(h * 128, (h + 1) * 128)) for b in range(bl) for h in range(GDN_HEADS)]
        ss = [s_scr[b, h] for b, h, _ in units]
        for (b, h, _), s in zip(units, ss):
            st_ref[b, 0, h] = s
        v_new = [u_ref[b, :, cols] - _dot(w_ref[b, :, cols], s) for (b, h, cols), s in zip(units, ss)]
        o_inter = [_dot(qd_ref[b, :, cols], s) for (b, h, cols), s in zip(units, ss)]
        os_ = [oi + _dot(qk_ref[b, 0, h], vn) for (b, h, cols), oi, vn in zip(units, o_inter, v_new)]
        for (b, h, cols), s, vn in zip(units, ss, v_new):
            s_scr[b, h] = s * gam_ref[b, 0, h:h + 1, :] + _dot_tn(kd_ref[b, :, cols], vn)
            vn_ref[b, :, cols] = vn
        for (b, h, cols), o in zip(units, os_):
            o_ref[b, :, cols] = o
            y_ref[b, :, cols] = _rms_gate(o, nw_ref[...], og_ref[b, :, cols])

    tok = pl.BlockSpec((bl, CHUNK, 512), lambda i: (0, i, 0))
    st_spec = pl.BlockSpec((bl, 1, GDN_HEADS, 128, 128), lambda i: (0, i, 0, 0, 0))
    o, y, vn, st = pl.pallas_call(
        body, name="gdn_scan_fwd", grid=(nc,),
        in_specs=[tok, tok, tok, tok,
                  pl.BlockSpec((bl, 1, GDN_HEADS, CHUNK, CHUNK), lambda i: (0, i, 0, 0, 0)),
                  pl.BlockSpec((bl, 1, 8, 128), lambda i: (0, i, 0, 0)), tok,
                  pl.BlockSpec(nw.shape, lambda i: (0, 0))],
        out_specs=[tok, tok, tok, st_spec],
        out_shape=[jax.ShapeDtypeStruct((bl, seq, 512), F32)] * 3
        + [jax.ShapeDtypeStruct((bl, nc, GDN_HEADS, 128, 128), F32)],
        scratch_shapes=[pltpu.VMEM((bl, GDN_HEADS, 128, 128), F32)],
        compiler_params=_params("arbitrary"),
    )(tok3(u), tok3(w), tok3(qd), tok3(kd), qk5, gam4, tok3(pc), nw)
    return y.reshape(n, 512), (o, st, w, qd, kd, qk5, gam4, tinv, vn)


def _gdn_bwd(qkv, gb, pc, res, dyb, nw, bl, seq):
    n = qkv.shape[0]
    nc = seq // CHUNK
    o, st, w, qd, kd, qk5, gam4, tinv, vn = res
    tok3 = lambda t: t.reshape(bl, seq, 512)

    def scan_body(dy_ref, o_ref, og_ref, w_ref, qd_ref, kd_ref, qk_ref, gam_ref, nw_ref,
                  do_ref, dog_ref, dvn_ref, dst_ref, dnw_ref, ds_scr):
        @pl.when(pl.program_id(0) == 0)
        def _():
            ds_scr[...] = jnp.zeros_like(ds_scr)
            dnw_ref[...] = jnp.zeros_like(dnw_ref)

        units = [(b, h, slice(h * 128, (h + 1) * 128)) for b in range(bl) for h in range(GDN_HEADS)]
        dnw = jnp.zeros(nw.shape, F32)
        d_os = []
        for b, h, cols in units:
            _, vjp = jax.vjp(_rms_gate, o_ref[b, :, cols], nw_ref[...], og_ref[b, :, cols])
            d_o, dnw_h, dog = vjp(dy_ref[b, :, cols])
            do_ref[b, :, cols] = d_o
            dog_ref[b, :, cols] = dog
            dnw = dnw + dnw_h
            d_os.append(d_o)
        dnw_ref[...] += dnw
        dss = [ds_scr[b, h] for b, h, _ in units]
        for (b, h, _), ds in zip(units, dss):
            dst_ref[b, 0, h] = ds
        dvn_a = [_dot(kd_ref[b, :, cols], ds) for (b, h, cols), ds in zip(units, dss)]
        dvns = [a + _dot_tn(qk_ref[b, 0, h], d_o) for (b, h, cols), a, d_o in zip(units, dvn_a, d_os)]
        ds_a = [_dot_tn(qd_ref[b, :, cols], d_o) + ds * gam_ref[b, 0, h:h + 1, :]
                for (b, h, cols), d_o, ds in zip(units, d_os, dss)]
        for (b, h, cols), a, dvn in zip(units, ds_a, dvns):
            dvn_ref[b, :, cols] = dvn
            ds_scr[b, h] = a - _dot_tn(w_ref[b, :, cols], dvn)

    rev = lambda i: nc - 1 - i
    tok = pl.BlockSpec((bl, CHUNK, 512), lambda i: (0, rev(i), 0))
    st_spec = pl.BlockSpec((bl, 1, GDN_HEADS, 128, 128), lambda i: (0, rev(i), 0, 0, 0))
    tok_shape = jax.ShapeDtypeStruct((bl, seq, 512), F32)
    d_o, dog, dvn, dst, dnw = pl.pallas_call(
        scan_body, name="gdn_scan_bwd", grid=(nc,),
        in_specs=[tok] * 6 + [pl.BlockSpec((bl, 1, GDN_HEADS, CHUNK, CHUNK), lambda i: (0, rev(i), 0, 0, 0)),
                              pl.BlockSpec((bl, 1, 8, 128), lambda i: (0, rev(i), 0, 0)),
                              pl.BlockSpec(nw.shape, lambda i: (0, 0))],
        out_specs=[tok, tok, tok, st_spec, pl.BlockSpec(nw.shape, lambda i: (0, 0))],
        out_shape=[tok_shape, tok_shape, tok_shape, jax.ShapeDtypeStruct(st.shape, F32),
                   jax.ShapeDtypeStruct(nw.shape, F32)],
        scratch_shapes=[pltpu.VMEM((bl, GDN_HEADS, 128, 128), F32)],
        compiler_params=_params("arbitrary"),
    )(tok3(dyb), o, tok3(pc), tok3(w), tok3(qd), tok3(kd), qk5, gam4, nw)

    r_per = GDN_PREP_CHUNKS
    tm = r_per * CHUNK

    def prep_body(q_ref, k_ref, v_ref, gb_ref, t_ref, st_ref, dst_ref, dvn_ref, do_ref, vn_ref, dqkv_ref, dgb_ref):
        chunk_rows = [slice(r * CHUNK, (r + 1) * CHUNK) for r in range(r_per)]
        gather = lambda ref: [t for rows in chunk_rows for t in _head_cols(ref, rows)]
        units = [(r, h) for r in range(r_per) for h in range(GDN_HEADS)]
        t_known = [t_ref[r, h] for r, h in units]
        prep = lambda q, k, v, g: _gdn_prep_units(q, k, v, g, t_known)[:6]
        _, vjp = jax.vjp(prep, gather(q_ref), gather(k_ref), gather(v_ref), [gb_ref[rows, :] for rows in chunk_rows])
        ss = [st_ref[r, h] for r, h in units]
        dss = [dst_ref[r, h] for r, h in units]
        dvns, d_os, v_new = gather(dvn_ref), gather(do_ref), gather(vn_ref)
        d_w = [-_dot_nt(dvn, s) for dvn, s in zip(dvns, ss)]
        d_qk = [_dot_nt(d_o, vn) for d_o, vn in zip(d_os, v_new)]
        d_qd = [_dot_nt(d_o, s) for d_o, s in zip(d_os, ss)]
        d_kd = [_dot_nt(vn, ds) for vn, ds in zip(v_new, dss)]
        d_gam = [_sum_all(ds * s) for ds, s in zip(dss, ss)]
        dq, dk, dv, dgb = vjp((dvns, d_w, d_qk, d_qd, d_kd, d_gam))
        for i, (r, h) in enumerate(units):
            rows = chunk_rows[r]
            for part, d in enumerate((dq, dk, dv)):
                dqkv_ref[rows, part * 512 + h * 128:part * 512 + (h + 1) * 128] = d[i]
        for r, rows in enumerate(chunk_rows):
            dgb_ref[rows, :] = dgb[r]

    tokp = lambda j: pl.BlockSpec((tm, 512), lambda i: (i, j))
    st4 = pl.BlockSpec((r_per, GDN_HEADS, 128, 128), lambda i: (i, 0, 0, 0))
    dqkv, dgb = pl.pallas_call(
        prep_body, name="gdn_prep_bwd", grid=(n // tm,),
        in_specs=[tokp(0), tokp(1), tokp(2), pl.BlockSpec((tm, 128), lambda i: (i, 0)),
                  pl.BlockSpec((r_per, GDN_HEADS, CHUNK, CHUNK), lambda i: (i, 0, 0, 0)), st4, st4,
                  tokp(0), tokp(0), tokp(0)],
        out_specs=[pl.BlockSpec((tm, 1536), lambda i: (i, 0)), pl.BlockSpec((tm, 128), lambda i: (i, 0))],
        out_shape=[jax.ShapeDtypeStruct((n, 1536), F32), jax.ShapeDtypeStruct((n, 128), F32)],
        compiler_params=_params("parallel"),
    )(qkv, qkv, qkv, gb, tinv, st.reshape(bl * nc, GDN_HEADS, 128, 128), dst.reshape(bl * nc, GDN_HEADS, 128, 128),
      dvn.reshape(n, 512), d_o.reshape(n, 512), vn.reshape(n, 512))
    return dqkv, dog.reshape(n, 512), dgb, dnw


def _out_block(x2, tgt2, ya, yb, g1p3, wo, lnw, lnb, seq, tm=256):
    n = x2.shape[0]
    tpe = seq // tm
    bl = n // seq

    def body(x_ref, t_ref, ya_ref, yb_ref, g_ref, wo_ref, lnw_ref, lnb_ref,
             dz_ref, dya_ref, dyb_ref, dwo_ref, dg_ref, glw_ref, glb_ref, loss_ref):
        i = pl.program_id(0)

        @pl.when(i == 0)
        def _():
            dwo_ref[...] = jnp.zeros_like(dwo_ref)
            glw_ref[...] = jnp.zeros_like(glw_ref)
            glb_ref[...] = jnp.zeros_like(glb_ref)
            loss_ref[...] = jnp.zeros_like(loss_ref)

        @pl.when(i % tpe == 0)
        def _():
            dg_ref[...] = jnp.zeros_like(dg_ref)

        ya16 = ya_ref[...].astype(wo.dtype)
        yb16 = yb_ref[...].astype(wo.dtype)
        wa = wo_ref[0:GLA_WIDTH, :]
        wb = wo_ref[GLA_WIDTH:, :]
        y = _dot(ya16, wa) + _dot(yb16, wb)
        g1p = g_ref[0]
        z = ALPHA * x_ref[...] + g1p * y
        mu = jnp.mean(z, axis=-1, keepdims=True)
        zc = z - mu
        rstd = lax.rsqrt(jnp.mean(zc * zc, axis=-1, keepdims=True) + LN_EPS)
        xhat = zc * rstd
        diff = xhat * lnw_ref[...] + lnb_ref[...] - t_ref[...]
        loss_ref[...] += (0.5 / D_MODEL) * jnp.sum(jnp.sum(diff * diff, axis=-1, keepdims=True), axis=0, keepdims=True)
        dout = diff * (1.0 / D_MODEL)
        glw_ref[...] += jnp.sum(dout * xhat, axis=0, keepdims=True)
        glb_ref[...] += jnp.sum(dout, axis=0, keepdims=True)
        dxh = dout * lnw_ref[...]
        dz = rstd * (dxh - jnp.mean(dxh, axis=-1, keepdims=True)
                     - xhat * jnp.mean(dxh * xhat, axis=-1, keepdims=True))
        dz_ref[...] = dz
        dg_ref[0] += jnp.sum(dz * y, axis=0, keepdims=True)
        dy = (g1p * dz).astype(wo.dtype)
        dya_ref[...] = _dot_nt(dy, wa)
        dyb_ref[...] = _dot_nt(dy, wb)
        dwo_ref[0:GLA_WIDTH, :] += _dot_tn(ya16, dy)
        dwo_ref[GLA_WIDTH:, :] += _dot_tn(yb16, dy)

    row = lambda i: (i, 0)
    const = lambda i: (0, 0)
    per_ex = pl.BlockSpec((1, 1, D_MODEL), lambda i: (i // tpe, 0, 0))
    return pl.pallas_call(
        body, name="out_block", grid=(n // tm,),
        in_specs=[pl.BlockSpec((tm, D_MODEL), row), pl.BlockSpec((tm, D_MODEL), row),
                  pl.BlockSpec((tm, 512), row), pl.BlockSpec((tm, 512), row), per_ex,
                  pl.BlockSpec((D_MODEL, D_MODEL), const), pl.BlockSpec((1, D_MODEL), const),
                  pl.BlockSpec((1, D_MODEL), const)],
        out_specs=[pl.BlockSpec((tm, D_MODEL), row), pl.BlockSpec((tm, 512), row), pl.BlockSpec((tm, 512), row),
                   pl.BlockSpec((D_MODEL, D_MODEL), const), per_ex,
                   pl.BlockSpec((1, D_MODEL), const), pl.BlockSpec((1, D_MODEL), const),
                   pl.BlockSpec((1, 1), const)],
        out_shape=[jax.ShapeDtypeStruct((n, D_MODEL), F32), jax.ShapeDtypeStruct((n, 512), F32),
                   jax.ShapeDtypeStruct((n, 512), F32), jax.ShapeDtypeStruct((D_MODEL, D_MODEL), F32),
                   jax.ShapeDtypeStruct((bl, 1, D_MODEL), F32), jax.ShapeDtypeStruct((1, D_MODEL), F32),
                   jax.ShapeDtypeStruct((1, D_MODEL), F32), jax.ShapeDtypeStruct((1, 1), F32)],
        compiler_params=_params("arbitrary"),
    )(x2, tgt2, ya, yb, g1p3, wo, lnw, lnb)


def _proj_bwd_x(ds, ws, x2, dz, sc3, seq, tm=256):
    n = x2.shape[0]
    tpe = seq // tm
    bl = n // seq

    def body(da_ref, db_ref, dc_ref, dd1_ref, dd2_ref, wa_ref, wb_ref, wc_ref, wd_ref, x_ref, dz_ref, sc_ref,
             gx_ref, dsh_ref, dsc_ref):
        i = pl.program_id(0)

        @pl.when(i % tpe == 0)
        def _():
            dsh_ref[...] = jnp.zeros_like(dsh_ref)
            dsc_ref[...] = jnp.zeros_like(dsc_ref)

        cdt = ws[0].dtype
        dh = _dot(da_ref[...].astype(cdt), wa_ref[...])
        dh += _dot(db_ref[...].astype(cdt), wb_ref[...])
        dh += _dot(dc_ref[...].astype(cdt), wc_ref[...])
        dh += _dot((dd1_ref[...] + dd2_ref[...]).astype(cdt), wd_ref[...])
        gx_ref[...] = dh * sc_ref[0] + ALPHA * dz_ref[...]
        dsh_ref[0] += jnp.sum(dh, axis=0, keepdims=True)
        dsc_ref[0] += jnp.sum(dh * x_ref[...], axis=0, keepdims=True)

    row = lambda i: (i, 0)
    const = lambda i: (0, 0)
    per_ex = pl.BlockSpec((1, 1, D_MODEL), lambda i: (i // tpe, 0, 0))
    da, db, dc, (dd1, dd2) = ds
    return pl.pallas_call(
        body, name="proj_bwd_x", grid=(n // tm,),
        in_specs=[pl.BlockSpec((tm, d.shape[1]), row) for d in (da, db, dc, dd1, dd2)]
        + [pl.BlockSpec(w.shape, const) for w in ws]
        + [pl.BlockSpec((tm, D_MODEL), row), pl.BlockSpec((tm, D_MODEL), row), per_ex],
        out_specs=[pl.BlockSpec((tm, D_MODEL), row), per_ex, per_ex],
        out_shape=[jax.ShapeDtypeStruct((n, D_MODEL), F32), jax.ShapeDtypeStruct((bl, 1, D_MODEL), F32),
                   jax.ShapeDtypeStruct((bl, 1, D_MODEL), F32)],
        compiler_params=_params("arbitrary"),
    )(da, db, dc, dd1, dd2, *ws, x2, dz, sc3)


def _proj_bwd_w(x2, sc3, sh3, ds, seq, cdt, name, tm=256):
    n = x2.shape[0]
    tpe = seq // tm
    flat, groups = [], []
    for d in ds:
        parts = d if isinstance(d, tuple) else (d,)
        groups.append(len(parts))
        flat.extend(parts)
    nin = len(flat)

    def body(x_ref, sc_ref, sh_ref, *refs):
        i = pl.program_id(0)
        outs = refs[nin:]

        @pl.when(i == 0)
        def _():
            for o in outs:
                o[...] = jnp.zeros_like(o)

        h = (x_ref[...] * sc_ref[0] + sh_ref[0]).astype(cdt)
        pos = 0
        for o, cnt in zip(outs, groups):
            d = refs[pos][...]
            for extra in refs[pos + 1:pos + cnt]:
                d = d + extra[...]
            pos += cnt
            o[...] += _dot_tn(d.astype(cdt), h)

    row = lambda i: (i, 0)
    const = lambda i: (0, 0)
    per_ex = pl.BlockSpec((1, 1, D_MODEL), lambda i: (i // tpe, 0, 0))
    widths = [(d[0] if isinstance(d, tuple) else d).shape[1] for d in ds]
    return pl.pallas_call(
        body, name=name, grid=(n // tm,),
        in_specs=[pl.BlockSpec((tm, D_MODEL), row), per_ex, per_ex]
        + [pl.BlockSpec((tm, d.shape[1]), row) for d in flat],
        out_specs=[pl.BlockSpec((w, D_MODEL), const) for w in widths],
        out_shape=[jax.ShapeDtypeStruct((w, D_MODEL), F32) for w in widths],
        compiler_params=_params("arbitrary"),
    )(x2, sc3, sh3, *flat)


def _mod_block(c_all, w_ada_sh, b_blk):
    def body(c_ref, w_ref, b_ref, o_ref):
        o_ref[...] = _dot(c_ref[...], w_ref[...]) + b_ref[...]

    return pl.pallas_call(
        body, name="mod_block",
        out_shape=jax.ShapeDtypeStruct((c_all.shape[0], w_ada_sh.shape[1]), F32),
        compiler_params=pltpu.CompilerParams(vmem_limit_bytes=VMEM_LIMIT),
    )(c_all, w_ada_sh, b_blk)


def _ada_grads(c_all, dmod_all, dmod_blk):
    def body(c_ref, da_ref, db_ref, gw_ref, gb_ref):
        gw_ref[...] = _dot_tn(c_ref[...], db_ref[...])
        gb_ref[...] = jnp.sum(da_ref[...], axis=0, keepdims=True)

    return pl.pallas_call(
        body, name="ada_grads",
        out_shape=[jax.ShapeDtypeStruct((c_all.shape[1], dmod_blk.shape[1]), F32),
                   jax.ShapeDtypeStruct((1, dmod_all.shape[1]), F32)],
        compiler_params=pltpu.CompilerParams(vmem_limit_bytes=VMEM_LIMIT),
    )(c_all, dmod_all, dmod_blk)


def _sum_leading(parts, name):
    def body(p_ref, o_ref):
        acc = p_ref[0]
        for d in range(1, parts.shape[0]):
            acc = acc + p_ref[d]
        o_ref[...] = acc

    return pl.pallas_call(
        body, name=name, out_shape=jax.ShapeDtypeStruct(parts.shape[1:], F32),
        compiler_params=pltpu.CompilerParams(vmem_limit_bytes=VMEM_LIMIT),
    )(parts)


ELEMENTWISE_BLOCK_BYTES = 2 * 1024 * 1024


def _tile2d(rows, cols, row_align=8):
    if rows * cols * 4 <= ELEMENTWISE_BLOCK_BYTES:
        return rows, cols
    fits = [t for t in range(row_align, rows, row_align) if rows % t == 0 and t * cols * 4 <= ELEMENTWISE_BLOCK_BYTES]
    if fits:
        return fits[-1], cols
    fits = [t for t in range(128, cols, 128) if cols % t == 0 and rows * t * 4 <= ELEMENTWISE_BLOCK_BYTES]
    assert fits, (rows, cols)
    return rows, fits[-1]


def _add_n(arrs, name, out_dtypes=(F32,)):
    rows, cols = arrs[0].shape
    narrow = any(jnp.dtype(dt).itemsize < 4 for dt in tuple(out_dtypes) + tuple(a.dtype for a in arrs))
    tr, tc = _tile2d(rows, cols, 16 if narrow else 8)
    n_in = len(arrs)

    def body(*refs):
        acc = refs[0][...].astype(F32)
        for r in refs[1:n_in]:
            acc = acc + r[...].astype(F32)
        for o in refs[n_in:]:
            o[...] = acc.astype(o.dtype)

    spec = pl.BlockSpec((tr, tc), lambda i, j: (i, j))
    return pl.pallas_call(
        body, name=name, grid=(rows // tr, cols // tc), in_specs=[spec] * n_in, out_specs=[spec] * len(out_dtypes),
        out_shape=[jax.ShapeDtypeStruct((rows, cols), dt) for dt in out_dtypes],
        compiler_params=_params("parallel", "parallel"),
    )(*arrs)


def _chip_sum_blocks(a, b, per, blocks, name, chunk=128):
    rows, cols = a.shape
    padded = -(-per // 16) * 16
    assert rows >= (blocks - 1) * per + padded, (rows, per, blocks)

    def body(a_ref, b_ref, o_ref, o16_ref):
        for j in range(blocks):
            for r0 in range(0, padded, chunk):
                n_rows = min(chunk, padded - r0)
                src = pl.ds(j * per + r0, n_rows)
                s = a_ref[src, :] + b_ref[src, :]
                if per - r0 < n_rows:
                    s = jnp.where(_iota((n_rows, 1), 0) < per - r0, s, 0.0)
                o_ref[j, r0:r0 + n_rows, :] = s
                o16_ref[j, r0:r0 + n_rows, :] = s.astype(BF16)

    return pl.pallas_call(
        body, name=name,
        out_shape=[jax.ShapeDtypeStruct((blocks, padded, cols), F32), jax.ShapeDtypeStruct((blocks, padded, cols), BF16)],
        compiler_params=pltpu.CompilerParams(vmem_limit_bytes=VMEM_LIMIT),
    )(a, b)


GRAD_PAD_ROWS = 16


def _adamw(w, g, m, v, name):
    rows, cols = w.shape
    tr, tc = _tile2d(rows, cols)
    c1 = 1.0 / (1.0 - ADAM_B1 ** ADAM_STEP)
    c2 = 1.0 / (1.0 - ADAM_B2 ** ADAM_STEP)

    def body(w_ref, g_ref, m_ref, v_ref, d_ref, nm_ref, nv_ref):
        gg = g_ref[...]
        nm = ADAM_B1 * m_ref[...] + (1.0 - ADAM_B1) * gg
        nv = ADAM_B2 * v_ref[...] + (1.0 - ADAM_B2) * (gg * gg)
        nm_ref[...] = nm
        nv_ref[...] = nv
        d_ref[...] = -ADAM_LR * ((nm * c1) / (jnp.sqrt(nv * c2) + ADAM_EPS) + ADAM_WD * w_ref[...])

    spec = pl.BlockSpec((tr, tc), lambda i, j: (i, j))
    shp = jax.ShapeDtypeStruct((rows, cols), F32)
    return pl.pallas_call(
        body, name=name, grid=(rows // tr, cols // tc), in_specs=[spec] * 4, out_specs=[spec] * 3,
        out_shape=[shp, shp, shp], compiler_params=_params("parallel", "parallel"),
    )(w, g, m, v)


def _coords():
    return lax.axis_index("x"), lax.axis_index("y"), lax.axis_index("c")


def _all_gather8(blk, name):
    m_per, n = blk.shape

    def body(x_ref, out_ref, send_sems, recv_sems, local_sem):
        x, y, c = _coords()
        me, sibling = (x, y, c), (x, y, 1 - c)
        chips = [(1 - x, y), (x, 1 - y), (1 - x, 1 - y)]

        def rows(px, py, pc):
            return out_ref.at[pl.ds((4 * px + 2 * py + pc) * m_per, m_per), :]

        def copy(k, block, to, src=None):
            return pltpu.make_async_remote_copy(
                src_ref=rows(*block) if src is None else src, dst_ref=rows(*block),
                send_sem=send_sems.at[k], recv_sem=recv_sems.at[k], device_id=to, device_id_type=MESH)

        mine = pltpu.make_async_copy(x_ref, rows(*me), local_sem)
        mine.start()
        first = [copy(0, me, sibling, src=x_ref)]
        first += [copy(1 + j, me, (*chip, c), src=x_ref) for j, chip in enumerate(chips)]
        for cp in first:
            cp.start()
        passed = [copy(4 + j, (*chip, c), sibling) for j, chip in enumerate(chips)]
        for j, chip in enumerate(chips):
            copy(1 + j, (*chip, c), me).wait_recv()
            passed[j].start()
        copy(0, sibling, me).wait_recv()
        for j, chip in enumerate(chips):
            copy(4 + j, (*chip, 1 - c), me).wait_recv()
        for cp in first + passed:
            cp.wait_send()
        mine.wait()

    return pl.pallas_call(
        body, name=name,
        out_shape=jax.ShapeDtypeStruct((8 * m_per, n), blk.dtype),
        in_specs=[pl.BlockSpec(memory_space=pltpu.VMEM)],
        out_specs=pl.BlockSpec(memory_space=pltpu.VMEM),
        scratch_shapes=[pltpu.SemaphoreType.DMA((7,)), pltpu.SemaphoreType.DMA((7,)), pltpu.SemaphoreType.DMA],
        compiler_params=pltpu.CompilerParams(vmem_limit_bytes=VMEM_LIMIT),
    )(blk)


def _chip_gather(shards, split, name):
    k_arr = len(shards)

    def body(*refs):
        srcs, dsts = refs[:k_arr], refs[k_arr:2 * k_arr]
        send_sems, recv_sems, fwd_send_sems, fwd_recv_sems, local_sems = refs[2 * k_arr:]
        x, y, c = _coords()
        peers = [(1 - x, y, c), (x, 1 - y, c), (1 - x, 1 - y, c)]
        sibling = (x, y, 1 - c)
        me_chip = 2 * x + y

        def part(ref, a, core):
            if not split[a]:
                return ref
            half = shards[a].shape[1] // 2
            return ref.at[:, pl.ds(core * half, half)]

        def ici(a, j, src_chip, dst_dev):
            return pltpu.make_async_remote_copy(
                src_ref=part(srcs[a], a, c), dst_ref=part(dsts[a].at[src_chip], a, c),
                send_sem=send_sems.at[a, j], recv_sem=recv_sems.at[a, j], device_id=dst_dev, device_id_type=MESH)

        def d2d(a, j, src_chip, core):
            return pltpu.make_async_remote_copy(
                src_ref=part(dsts[a].at[src_chip], a, core), dst_ref=part(dsts[a].at[src_chip], a, core),
                send_sem=fwd_send_sems.at[a, j], recv_sem=fwd_recv_sems.at[a, j],
                device_id=sibling, device_id_type=MESH)

        local = [pltpu.make_async_copy(srcs[a], dsts[a].at[me_chip], local_sems.at[a]) for a in range(k_arr)]
        for cp in local:
            cp.start()
        sends = [ici(a, j, me_chip, peer) for a in range(k_arr) for j, peer in enumerate(peers)]
        for cp in sends:
            cp.start()
        forwards = []
        for a in range(k_arr):
            for j, peer in enumerate(peers):
                peer_chip = 2 * peer[0] + peer[1]
                ici(a, j, peer_chip, peer).wait_recv()
                if split[a]:
                    forwards.append(d2d(a, j, peer_chip, c))
                    forwards[-1].start()
        for a in range(k_arr):
            for j, peer in enumerate(peers):
                if split[a]:
                    d2d(a, j, 2 * peer[0] + peer[1], 1 - c).wait_recv()
        for cp in sends + forwards:
            cp.wait_send()
        for cp in local:
            cp.wait()

    any_spec = pl.BlockSpec(memory_space=pl.ANY)
    return pl.pallas_call(
        body, name=name,
        out_shape=[jax.ShapeDtypeStruct((4,) + s.shape, s.dtype) for s in shards],
        in_specs=[any_spec] * k_arr, out_specs=[any_spec] * k_arr,
        scratch_shapes=[pltpu.SemaphoreType.DMA((k_arr, 3))] * 4 + [pltpu.SemaphoreType.DMA((k_arr,))],
    )(*shards)


def _chip_scatter(pieces, name):
    k_arr = len(pieces)

    def body(*refs):
        srcs, dsts = refs[:k_arr], refs[k_arr:2 * k_arr]
        send_sems, recv_sems = refs[2 * k_arr:]
        x, y, c = _coords()
        peers = [(1 - x, y, c), (x, 1 - y, c), (1 - x, 1 - y, c)]
        copies = []
        for a in range(k_arr):
            for j, peer in enumerate(peers):
                copies.append(pltpu.make_async_remote_copy(
                    src_ref=srcs[a].at[2 * peer[0] + peer[1]], dst_ref=dsts[a].at[j],
                    send_sem=send_sems.at[a, j], recv_sem=recv_sems.at[a, j], device_id=peer, device_id_type=MESH))
        for cp in copies:
            cp.start()
        for cp in copies:
            cp.wait_recv()
        for cp in copies:
            cp.wait_send()

    any_spec = pl.BlockSpec(memory_space=pl.ANY)
    return pl.pallas_call(
        body, name=name,
        out_shape=[jax.ShapeDtypeStruct((3,) + p.shape[1:], p.dtype) for p in pieces],
        in_specs=[any_spec] * k_arr, out_specs=[any_spec] * k_arr,
        scratch_shapes=[pltpu.SemaphoreType.DMA((k_arr, 3)), pltpu.SemaphoreType.DMA((k_arr, 3))],
    )(*pieces)


def _sibling_swap(arrs, name):
    k_arr = len(arrs)

    def body(*refs):
        srcs, dsts = refs[:k_arr], refs[k_arr:2 * k_arr]
        send_sems, recv_sems = refs[2 * k_arr:]
        x, y, c = _coords()
        copies = [pltpu.make_async_remote_copy(
            src_ref=srcs[a], dst_ref=dsts[a], send_sem=send_sems.at[a], recv_sem=recv_sems.at[a],
            device_id=(x, y, 1 - c), device_id_type=MESH) for a in range(k_arr)]
        for cp in copies:
            cp.start()
        for cp in copies:
            cp.wait_recv()
        for cp in copies:
            cp.wait_send()

    any_spec = pl.BlockSpec(memory_space=pl.ANY)
    return pl.pallas_call(
        body, name=name,
        out_shape=[jax.ShapeDtypeStruct(a.shape, a.dtype) for a in arrs],
        in_specs=[any_spec] * k_arr, out_specs=[any_spec] * k_arr,
        scratch_shapes=[pltpu.SemaphoreType.DMA((k_arr,)), pltpu.SemaphoreType.DMA((k_arr,))],
    )(*arrs)


def _split_w_in(w_in_t):
    wa = jnp.concatenate([w_in_t[0:1024], w_in_t[1040:1552]], axis=0)
    wb = w_in_t[1552:3088]
    wc = w_in_t[3096:3608]
    wd = jnp.concatenate([w_in_t[1024:1040], w_in_t[3088:3096],
                          jnp.zeros((128 - SMALL_USED, w_in_t.shape[1]), w_in_t.dtype)], axis=0)
    return wa, wb, wc, wd


def _merge_dw_in(dwa, dwb, dwc, dwd):
    return jnp.concatenate([dwa[0:1024], dwd[0:GLA_RANK], dwa[1024:1536], dwb, dwd[GLA_RANK:SMALL_USED], dwc,
                            jnp.zeros((GRAD_PAD_ROWS, dwa.shape[1]), dwa.dtype)], axis=0)


def _local_step(x, mod, w_in16, w_out16, gla_wg, gla_bg, gla_nw, conv_w, a_log, dt_bias, gdn_nw, ln_w, ln_b, tgt):
    bl, seq, _ = x.shape
    n = bl * seq
    x2 = x.reshape(n, D_MODEL)
    tgt2 = tgt.reshape(n, D_MODEL)
    sh3 = mod[:, None, 0:D_MODEL]
    sc3 = 1.0 + mod[:, None, D_MODEL:2 * D_MODEL]
    g1p3 = 1.0 + mod[:, None, 2 * D_MODEL:]
    ws = _split_w_in(w_in16)
    wg = jnp.concatenate([gla_wg, jnp.zeros((128 - GLA_RANK, GLA_QK), F32)], axis=0)
    cw8 = jnp.concatenate([conv_w, jnp.zeros((8 - CONV_K, conv_w.shape[1]), F32)], axis=0)
    alog_v = jnp.zeros((1, 128), F32).at[:, LANE_A:LANE_A + GDN_HEADS].set(a_log)
    dtb_v = jnp.zeros((1, 128), F32).at[:, LANE_A:LANE_A + GDN_HEADS].set(dt_bias)

    pa, pb, pc, pd = _proj_fwd(x2, sc3, sh3, ws, seq)
    ya, st_a = _gla_fwd(pa, pd, wg, gla_bg, gla_nw, bl, seq)
    qkv, gb, conv_out = _gdn_pre_fwd(pb, pd, cw8, alog_v, dtb_v, bl, seq)
    yb, st_b = _gdn_fwd(qkv, gb, pc, gdn_nw, bl, seq)
    dz, dya, dyb, d_wo, d_gate, d_lnw, d_lnb, loss = _out_block(x2, tgt2, ya, yb, g1p3, w_out16, ln_w, ln_b, seq)
    da, dd1, d_wg, d_bg, d_nwa = _gla_bwd(pa, pd, st_a, dya, wg, gla_bg, gla_nw, bl, seq)
    dqkv, dc, dgb, d_nwb = _gdn_bwd(qkv, gb, pc, st_b, dyb, gdn_nw, bl, seq)
    db, dd2, d_cw8, d_alog, d_dtb = _gdn_pre_bwd(pb, conv_out, pd, dqkv, dgb, cw8, alog_v, dtb_v, bl, seq)
    gx, d_sh, d_sc = _proj_bwd_x((da, db, dc, (dd1, dd2)), ws, x2, dz, sc3, seq)
    (dwa,) = _proj_bwd_w(x2, sc3, sh3, [da], seq, w_in16.dtype, "proj_bwd_w_a")
    dwb, dwc, dwd = _proj_bwd_w(x2, sc3, sh3, [db, dc, (dd1, dd2)], seq, w_in16.dtype, "proj_bwd_w_bcd")
    grads = dict(
        w_in=_merge_dw_in(dwa, dwb, dwc, dwd),
        w_out=d_wo,
        gla_w_gate_up=d_wg[0:GLA_RANK, :],
        gla_b_gate=d_bg,
        gla_norm_w=d_nwa,
        gdn_conv_w=d_cw8[0:CONV_K, :],
        gdn_a_log=d_alog[:, LANE_A:LANE_A + GDN_HEADS],
        gdn_dt_bias=d_dtb[:, LANE_A:LANE_A + GDN_HEADS],
        gdn_norm_w=d_nwb,
        ln_w=d_lnw,
        ln_b=d_lnb,
        mod=jnp.concatenate([d_sh[:, 0, :], d_sc[:, 0, :], d_gate[:, 0, :]], axis=1),
    )
    return loss, gx.reshape(bl, seq, D_MODEL), grads


_SMALL = (("gla_b_gate", 256), ("gla_norm_w", 128), ("gdn_a_log", 4), ("gdn_dt_bias", 4), ("gdn_norm_w", 128),
          ("ln_w", 1024), ("ln_b", 1024), ("gla_w_gate_up", 16 * 256), ("gdn_conv_w", 4 * 1536), ("loss", 1),
          ("mod", 2 * 3072))


def _pack_small(grads):
    flat = jnp.concatenate([grads[k].reshape(-1) for k, _ in _SMALL])
    total = sum(sz for _, sz in _SMALL)
    rows = -(-total // 1024) * 8
    return jnp.concatenate([flat, jnp.zeros((rows * 128 - total,), F32)]).reshape(rows, 128)


def _unpack_small(flat):
    out, pos = {}, 0
    for k, sz in _SMALL:
        out[k] = flat[pos:pos + sz]
        pos += sz
    return out


def kernel(x, c, w_ada, b_ada, w_in, gla_w_gate_up, gla_b_gate, gla_norm_w, gdn_conv_w, gdn_a_log, gdn_dt_bias, gdn_norm_w, w_out, ln_w, ln_b, loss_target, m_w_ada, m_b_ada, m_w_in, m_gla_w_gate_up, m_gla_b_gate, m_gla_norm_w, m_gdn_conv_w, m_gdn_a_log, m_gdn_dt_bias, m_gdn_norm_w, m_w_out, m_ln_w, m_ln_b, v_w_ada, v_b_ada, v_w_in, v_gla_w_gate_up, v_gla_b_gate, v_gla_norm_w, v_gdn_conv_w, v_gdn_a_log, v_gdn_dt_bias, v_gdn_norm_w, v_w_out, v_ln_w, v_ln_b):
    ix, iy, ic = _coords()
    chip = 2 * ix + iy
    dev = 4 * ix + 2 * iy + ic
    bl = x.shape[0]
    ndev = 8

    c_all = _all_gather8(c.reshape(8, -1), "gather_c").reshape(ndev * bl, D_MODEL)
    ada_cols = w_ada.shape[2]
    b_blk = lax.dynamic_slice_in_dim(b_ada, chip * ada_cols, ada_cols, axis=1)
    mod_blk = _mod_block(c_all, w_ada[0], b_blk)
    mod_g = _all_gather8(mod_blk, "gather_mod").reshape(ndev, ndev * bl, ada_cols)
    mod_all = jnp.concatenate([mod_g[2 * j] for j in range(4)], axis=1)
    mod = lax.dynamic_slice_in_dim(mod_all, dev * bl, bl, axis=0)

    w_in_g, w_out_g, wg_g, cw_g = _chip_gather(
        [jnp.transpose(w_in[0]).astype(BF16), w_out[0].astype(BF16), gla_w_gate_up[0], gdn_conv_w[0]],
        [True, True, False, False], "gather_weights")
    w_in16 = w_in_g.reshape(IN_COLS, D_MODEL)
    w_out16 = w_out_g.reshape(D_MODEL, D_MODEL)
    gla_wg = jnp.concatenate([wg_g[j] for j in range(4)], axis=1)
    conv_w = jnp.concatenate([cw_g[j] for j in range(4)], axis=1)

    loss, grad_x, gr = _local_step(x, mod, w_in16, w_out16, gla_wg, gla_b_gate, gla_norm_w, conv_w,
                                   gdn_a_log, gdn_dt_bias, gdn_norm_w, ln_w, ln_b, loss_target)

    gr["loss"] = loss
    packed = _pack_small(gr)
    prow = packed.shape[0]
    gathered = _all_gather8(packed, "gather_small").reshape(ndev, prow, 128)
    small = _unpack_small(_sum_leading(gathered, "sum_small").reshape(-1))
    loss = small["loss"][0]
    mod_rows = gathered.reshape(ndev, prow * 128)[:, sum(sz for _, sz in _SMALL[:-1]):][:, :bl * 3 * D_MODEL]
    dmod_all = mod_rows.reshape(ndev * bl, 3 * D_MODEL)
    dmod_blk = lax.dynamic_slice_in_dim(dmod_all, chip * ada_cols, ada_cols, axis=1)
    g_w_ada, g_b_ada = _ada_grads(c_all, dmod_all, dmod_blk)
    wg_cols = gla_w_gate_up.shape[2]
    g_wg = lax.dynamic_slice_in_dim(small["gla_w_gate_up"].reshape(GLA_RANK, GLA_QK), chip * wg_cols, wg_cols, axis=1)
    cw_cols = gdn_conv_w.shape[2]
    g_cw = lax.dynamic_slice_in_dim(small["gdn_conv_w"].reshape(CONV_K, 3 * GDN_WIDTH), chip * cw_cols, cw_cols, axis=1)

    in_feats = w_in.shape[2]
    out_rows = w_out.shape[1]
    p_in = gr["w_in"]
    p_out = gr["w_out"].reshape(4, out_rows, D_MODEL)
    h_in, h_out = D_MODEL // 2, out_rows // 2
    mine_in = lax.dynamic_slice_in_dim(p_in, ic * h_in, h_in, axis=1)
    mine_out = lax.dynamic_slice_in_dim(p_out, ic * h_out, h_out, axis=1)
    theirs_in = lax.dynamic_slice_in_dim(p_in, (1 - ic) * h_in, h_in, axis=1)
    theirs_out = lax.dynamic_slice_in_dim(p_out, (1 - ic) * h_out, h_out, axis=1)
    got_in, got_out = _sibling_swap([theirs_in, theirs_out], "swap_halves")
    chip_in, chip_in16 = _chip_sum_blocks(mine_in, got_in, in_feats, 4, "chip_sum_in")
    chip_out, chip_out16 = _add_n([mine_out.reshape(4 * h_out, D_MODEL), got_out.reshape(4 * h_out, D_MODEL)],
                                  "chip_sum_out", (F32, BF16))
    chip_out = chip_out.reshape(4, h_out, D_MODEL)
    rs_in, rs_out = _chip_scatter([chip_in16, chip_out16.reshape(4, h_out, D_MODEL)], "scatter_grads")
    own_in = lax.dynamic_index_in_dim(chip_in, chip, axis=0, keepdims=False)
    own_out = lax.dynamic_index_in_dim(chip_out, chip, axis=0, keepdims=False)
    (half_in,) = _add_n([own_in, rs_in[0], rs_in[1], rs_in[2]], "reduce_in")
    (half_out,) = _add_n([own_out, rs_out[0], rs_out[1], rs_out[2]], "reduce_out")
    sib_in, sib_out = _sibling_swap([half_in, half_out], "swap_result")
    g_w_in_t = jnp.where(ic == 0, jnp.concatenate([half_in, sib_in], axis=1),
                         jnp.concatenate([sib_in, half_in], axis=1))[0:in_feats]
    g_w_out = jnp.where(ic == 0, jnp.concatenate([half_out, sib_out], axis=0),
                        jnp.concatenate([sib_out, half_out], axis=0))

    grads = dict(
        w_ada=g_w_ada[None], b_ada=g_b_ada, w_in=g_w_in_t, gla_w_gate_up=g_wg[None],
        gla_b_gate=small["gla_b_gate"].reshape(1, -1), gla_norm_w=small["gla_norm_w"].reshape(1, -1),
        gdn_conv_w=g_cw[None], gdn_a_log=small["gdn_a_log"].reshape(1, -1),
        gdn_dt_bias=small["gdn_dt_bias"].reshape(1, -1), gdn_norm_w=small["gdn_norm_w"].reshape(1, -1),
        w_out=g_w_out[None], ln_w=small["ln_w"].reshape(1, -1), ln_b=small["ln_b"].reshape(1, -1))
    weights = dict(w_ada=w_ada, b_ada=b_ada, w_in=w_in, gla_w_gate_up=gla_w_gate_up, gla_b_gate=gla_b_gate,
                   gla_norm_w=gla_norm_w, gdn_conv_w=gdn_conv_w, gdn_a_log=gdn_a_log, gdn_dt_bias=gdn_dt_bias,
                   gdn_norm_w=gdn_norm_w, w_out=w_out, ln_w=ln_w, ln_b=ln_b)
    m_in = dict(w_ada=m_w_ada, b_ada=m_b_ada, w_in=m_w_in, gla_w_gate_up=m_gla_w_gate_up, gla_b_gate=m_gla_b_gate,
                gla_norm_w=m_gla_norm_w, gdn_conv_w=m_gdn_conv_w, gdn_a_log=m_gdn_a_log, gdn_dt_bias=m_gdn_dt_bias,
                gdn_norm_w=m_gdn_norm_w, w_out=m_w_out, ln_w=m_ln_w, ln_b=m_ln_b)
    v_in = dict(w_ada=v_w_ada, b_ada=v_b_ada, w_in=v_w_in, gla_w_gate_up=v_gla_w_gate_up, gla_b_gate=v_gla_b_gate,
                gla_norm_w=v_gla_norm_w, gdn_conv_w=v_gdn_conv_w, gdn_a_log=v_gdn_a_log, gdn_dt_bias=v_gdn_dt_bias,
                gdn_norm_w=v_gdn_norm_w, w_out=v_w_out, ln_w=v_ln_w, ln_b=v_ln_b)
    names = list(weights)
    delta, new_m, new_v = {}, {}, {}
    for nm in names:
        shp = weights[nm].shape
        if nm == "w_in":
            to2d = lambda t: jnp.transpose(t[0])
            from2d = lambda t: jnp.transpose(t)[None]
            g2d = grads[nm]
        else:
            to2d = lambda t: t.reshape(-1, shp[-1])
            from2d = lambda t: t.reshape(shp)
            g2d = to2d(grads[nm])
        d, a, b = _adamw(to2d(weights[nm]), g2d, to2d(m_in[nm]), to2d(v_in[nm]), "adamw_" + nm)
        delta[nm], new_m[nm], new_v[nm] = from2d(d), from2d(a), from2d(b)
        grads[nm] = from2d(g2d)
    return (loss, grad_x, *[grads[k] for k in names], *[delta[k] for k in names],
            *[new_m[k] for k in names], *[new_v[k] for k in names])
```

```python
import functools

import jax
import jax.numpy as jnp
from jax import lax
from jax.experimental import pallas as pl
from jax.experimental.pallas import tpu as pltpu

F32 = jnp.float32
BF16 = jnp.bfloat16
HI = lax.Precision.HIGH
INV_PREC = None
MESH = pl.DeviceIdType.MESH

D_MODEL = 1024
GLA_HEADS = 4
GLA_DK = 64
GLA_DV = 128
GLA_QK = 256
GLA_WIDTH = 512
GLA_RANK = 16
GLA_GATE_NORM = 16.0
GDN_HEADS = 4
GDN_DK = 128
GDN_WIDTH = 512
CONV_K = 4
CHUNK = 64
LN_EPS = 1e-5
RMS_EPS = 1e-6
ALPHA = 2.0 ** 0.25
IN_COLS = 3608

LANE_A = GLA_RANK
LANE_B = GLA_RANK + GDN_HEADS
SMALL_USED = GLA_RANK + 2 * GDN_HEADS

ADAM_LR = 0.001
ADAM_B1 = 0.9
ADAM_B2 = 0.999
ADAM_EPS = 1e-08
ADAM_WD = 0.01
ADAM_STEP = 10

VMEM_LIMIT = 56 * 1024 * 1024


def _iota(shape, dim):
    return lax.broadcasted_iota(jnp.int32, shape, dim)


def _dot(a, b, prec=None):
    return lax.dot_general(a, b, (((1,), (0,)), ((), ())), precision=prec, preferred_element_type=F32)


def _dot_nt(a, b, prec=None):
    return lax.dot_general(a, b, (((1,), (1,)), ((), ())), precision=prec, preferred_element_type=F32)


def _dot_tn(a, b, prec=None):
    return lax.dot_general(a, b, (((0,), (0,)), ((), ())), precision=prec, preferred_element_type=F32)


def _log_sigmoid(z):
    return jnp.minimum(z, 0.0) - jnp.log1p(jnp.exp(-jnp.abs(z)))


def _softplus(z):
    return jnp.maximum(z, 0.0) + jnp.log1p(jnp.exp(-jnp.abs(z)))


def _silu(z):
    return z * jax.nn.sigmoid(z)


def _rms_gate(o, nw, og):
    return o * lax.rsqrt(jnp.mean(o * o, axis=-1, keepdims=True) + RMS_EPS) * nw * _silu(og)


def _params(*sem):
    return pltpu.CompilerParams(dimension_semantics=sem, vmem_limit_bytes=VMEM_LIMIT)


GLA_PAIRS = GLA_HEADS // 2


def _gla_chunk(qs, ks, lrs, vs, ogs, ss, wgs, bgs, nw):
    c = qs[0].shape[0]
    pair_units = [divmod(i, GLA_PAIRS) for i in range(len(qs))]
    head_units = [(i // GLA_HEADS, i // GLA_HEADS * GLA_PAIRS + (i % GLA_HEADS) // 2, i % 2) for i in range(len(vs))]
    row, col = _iota((c, c), 0), _iota((c, c), 1)
    causal = row >= col
    first_half = (_iota((c, 1), 0) < c // 2).astype(F32)
    lane = _iota((1, 128), 1)
    masks = [(lane < GLA_DK).astype(F32), (lane >= GLA_DK).astype(F32)]
    gs = [_log_sigmoid(_dot(lrs[e], wgs[p]) + bgs[p]) * (1.0 / GLA_GATE_NORM) for e, p in pair_units]
    bs = [_dot(causal.astype(F32), g, HI) for g in gs]
    b_ref = [jnp.sum(g * first_half, axis=0, keepdims=True) for g in gs]
    b_last = [jnp.sum(g, axis=0, keepdims=True) for g in gs]
    qsc = [q * (GLA_DK ** -0.5) for q in qs]
    qe = [q * jnp.exp(b - br) for q, b, br in zip(qsc, bs, b_ref)]
    ke = [k * jnp.exp(br - b) for k, b, br in zip(ks, bs, b_ref)]
    qb = [q * jnp.exp(b) for q, b in zip(qsc, bs)]
    kd = [k * jnp.exp(bl_ - b) for k, b, bl_ in zip(ks, bs, b_last)]
    att = [jnp.where(causal, _dot_nt(qe[u] * masks[half], ke[u]), 0.0) for _, u, half in head_units]
    o_inter = [_dot_nt(qb[u] * masks[half], ss[u]) for _, u, half in head_units]
    os_ = [_dot(a, v) + oi for a, v, oi in zip(att, vs, o_inter)]
    upd = [_dot_tn(v, kd[u] * masks[half]) for (_, u, half), v in zip(head_units, vs)]
    s_new = [s * jnp.exp(bl_) + upd[2 * u] + upd[2 * u + 1] for u, (s, bl_) in enumerate(zip(ss, b_last))]
    ys = [_rms_gate(o, nw, og) for o, og in zip(os_, ogs)]
    return ys, s_new


def _unit_lower_inverse_chain(a_list):
    c = a_list[0].shape[0]
    eye = (_iota((c, c), 0) == _iota((c, c), 1)).astype(F32)
    ps = [-a for a in a_list]
    ts = [eye + p for p in ps]
    levels = max(c.bit_length() - 2, 0)
    if levels:
        ps = [_dot(p, p, INV_PREC) for p in ps]
    for level in range(levels):
        last = level == levels - 1
        both = [_dot(t if last else jnp.concatenate([t, p], axis=0), p, INV_PREC) for t, p in zip(ts, ps)]
        ts = [t + m[0:c] for t, m in zip(ts, both)]
        if not last:
            ps = [m[c:2 * c] for m in both]
    return ts


@jax.custom_vjp
def _unit_lower_inverse(a_list):
    return _unit_lower_inverse_chain(a_list)


def _unit_lower_inverse_fwd(a_list):
    ts = _unit_lower_inverse_chain(a_list)
    return ts, ts


def _unit_lower_inverse_bwd(ts, dts):
    xs = [_dot_nt(dt, t, INV_PREC) for dt, t in zip(dts, ts)]
    return ([-_dot_tn(t, x, INV_PREC) for t, x in zip(ts, xs)],)


_unit_lower_inverse.defvjp(_unit_lower_inverse_fwd, _unit_lower_inverse_bwd)


@jax.custom_vjp
def _unit_lower_inverse_known(a_list, ts):
    return ts


def _unit_lower_inverse_known_fwd(a_list, ts):
    return ts, ts


def _unit_lower_inverse_known_bwd(ts, dts):
    return _unit_lower_inverse_bwd(ts, dts) + ([jnp.zeros_like(t) for t in ts],)


_unit_lower_inverse_known.defvjp(_unit_lower_inverse_known_fwd, _unit_lower_inverse_known_bwd)


def _gdn_prep_units(qs, ks, vs, gbs, t_known=None):
    c = qs[0].shape[0]
    units = [divmod(i, GDN_HEADS) for i in range(len(qs))]
    row, col = _iota((c, c), 0), _iota((c, c), 1)
    causal, strict = row >= col, row > col
    lane = _iota((1, 128), 1)
    d_alls = [_dot(causal.astype(F32), gb, HI) for gb in gbs]
    g_c, beta_c, d_c = [], [], []
    for r, h in units:
        sel_a = (lane == LANE_A + h).astype(F32)
        g_c.append(jnp.sum(gbs[r] * sel_a, axis=-1, keepdims=True))
        beta_c.append(jnp.sum(gbs[r] * (lane == LANE_B + h).astype(F32), axis=-1, keepdims=True))
        d_c.append(jnp.sum(d_alls[r] * sel_a, axis=-1, keepdims=True))
    d_last = [jnp.sum(g, axis=0, keepdims=True) for g in g_c]
    d_diff = [jnp.broadcast_to(d, (c, c)) - jnp.broadcast_to(d, (c, c)).T for d in d_c]
    decay_mat = [jnp.where(causal, jnp.exp(jnp.where(causal, dd, 0.0)), 0.0) for dd in d_diff]
    kb = [k * b for k, b in zip(ks, beta_c)]
    kbk_qk = [_dot_nt(jnp.concatenate([kbi, q], axis=0), k) for kbi, q, k in zip(kb, qs, ks)]
    a = [jnp.where(strict, m[0:c] * dm, 0.0) for m, dm in zip(kbk_qk, decay_mat)]
    qk = [jnp.where(causal, m[c:2 * c] * dm, 0.0) for m, dm in zip(kbk_qk, decay_mat)]
    t = _unit_lower_inverse(a) if t_known is None else _unit_lower_inverse_known(a, t_known)
    uw = [_dot(ti, jnp.concatenate([v * b, kbi * jnp.exp(d)], axis=1))
          for ti, v, b, kbi, d in zip(t, vs, beta_c, kb, d_c)]
    u = [m[:, 0:128] for m in uw]
    w = [m[:, 128:256] for m in uw]
    q_dec = [q * jnp.exp(d) for q, d in zip(qs, d_c)]
    k_dec = [k * jnp.exp(dl - d) for k, dl, d in zip(ks, d_last, d_c)]
    gamma = [jnp.exp(dl) for dl in d_last]
    return u, w, qk, q_dec, k_dec, gamma, t


def _sum_all(t):
    return jnp.sum(jnp.sum(t, axis=-1, keepdims=True), axis=0, keepdims=True)


def _gdn_pre_elem(ps, ab, alog_v, dtb_v):
    outs = []
    for j, p in enumerate(ps):
        s = _silu(p)
        if j < 2 * GDN_HEADS:
            s = s * lax.rsqrt(jnp.sum(s * s, axis=-1, keepdims=True) + RMS_EPS)
        if j < GDN_HEADS:
            s = s * (GDN_DK ** -0.5)
        outs.append(s)
    lane = _iota((1, 128), 1)
    is_a = (lane >= LANE_A) & (lane < LANE_A + GDN_HEADS)
    is_b = (lane >= LANE_B) & (lane < LANE_B + GDN_HEADS)
    g = -jnp.exp(alog_v) * _softplus(ab + dtb_v)
    gb = jnp.where(is_a, g, jnp.where(is_b, jax.nn.sigmoid(ab), 0.0))
    return tuple(outs) + (gb,)


def _proj_fwd(x2, sc3, sh3, ws, seq, tm=256):
    n = x2.shape[0]
    tpe = seq // tm
    nw = len(ws)

    def body(x_ref, sc_ref, sh_ref, *refs):
        h = (x_ref[...] * sc_ref[0] + sh_ref[0]).astype(ws[0].dtype)
        for w_ref, o_ref in zip(refs[:nw], refs[nw:]):
            o_ref[...] = _dot_nt(h, w_ref[...])

    row = lambda i: (i, 0)
    per_ex = pl.BlockSpec((1, 1, D_MODEL), lambda i: (i // tpe, 0, 0))
    return pl.pallas_call(
        body, name="proj_fwd", grid=(n // tm,),
        in_specs=[pl.BlockSpec((tm, D_MODEL), row), per_ex, per_ex]
        + [pl.BlockSpec(w.shape, lambda i: (0, 0)) for w in ws],
        out_specs=[pl.BlockSpec((tm, w.shape[0]), row) for w in ws],
        out_shape=[jax.ShapeDtypeStruct((n, w.shape[0]), F32) for w in ws],
        compiler_params=_params("parallel"),
    )(x2, sc3, sh3, *ws)


def _gla_fwd(pa, pd, wg, bg, nw, bl, seq):
    n = pa.shape[0]
    nc = seq // CHUNK
    heads = [(e, h, slice(h * 128, (h + 1) * 128)) for e in range(bl) for h in range(GLA_HEADS)]
    pair_cols = [slice(p * 128, (p + 1) * 128) for p in range(GLA_PAIRS)]
    pairs = [(e, p, pair_cols[p]) for e in range(bl) for p in range(GLA_PAIRS)]

    def body(q_ref, k_ref, v_ref, og_ref, lr_ref, wg_ref, bg_ref, nw_ref, y_ref, st_ref, s_scr):
        @pl.when(pl.program_id(0) == 0)
        def _():
            s_scr[...] = jnp.zeros_like(s_scr)

        ss = [s_scr[e, p] for e, p, _ in pairs]
        for (e, p, _), s in zip(pairs, ss):
            st_ref[e, 0, p] = s
        ys, s_new = _gla_chunk([q_ref[e, :, cols] for e, _, cols in pairs], [k_ref[e, :, cols] for e, _, cols in pairs],
                               [lr_ref[e] for e in range(bl)],
                               [v_ref[e, :, cols] for e, _, cols in heads], [og_ref[e, :, cols] for e, _, cols in heads],
                               ss, [wg_ref[:, cols] for cols in pair_cols], [bg_ref[:, cols] for cols in pair_cols],
                               nw_ref[...])
        for (e, h, cols), y in zip(heads, ys):
            y_ref[e, :, cols] = y
        for (e, p, _), s in zip(pairs, s_new):
            s_scr[e, p] = s

    tok = lambda w, j: pl.BlockSpec((bl, CHUNK, w), lambda i: (0, i, j))
    const = lambda i: (0, 0)
    pa3 = pa.reshape(bl, seq, 1536)
    y, st = pl.pallas_call(
        body, name="gla_fwd", grid=(nc,),
        in_specs=[tok(256, 0), tok(256, 1), tok(512, 1), tok(512, 2), tok(128, 0),
                  pl.BlockSpec(wg.shape, const), pl.BlockSpec(bg.shape, const), pl.BlockSpec(nw.shape, const)],
        out_specs=[tok(512, 0), pl.BlockSpec((bl, 1, GLA_PAIRS, 128, 128), lambda i: (0, i, 0, 0, 0))],
        out_shape=[jax.ShapeDtypeStruct((bl, seq, 512), F32),
                   jax.ShapeDtypeStruct((bl, nc, GLA_PAIRS, 128, 128), F32)],
        scratch_shapes=[pltpu.VMEM((bl, GLA_PAIRS, 128, 128), F32)],
        compiler_params=_params("arbitrary"),
    )(pa3, pa3, pa3, pa3, pd.reshape(bl, seq, 128), wg, bg, nw)
    return y.reshape(n, 512), st


def _gla_bwd(pa, pd, st, dya, wg, bg, nw, bl, seq):
    n = pa.shape[0]
    nc = seq // CHUNK
    heads = [(e, h, slice(h * 128, (h + 1) * 128)) for e in range(bl) for h in range(GLA_HEADS)]
    pair_cols = [slice(p * 128, (p + 1) * 128) for p in range(GLA_PAIRS)]
    pairs = [(e, p, pair_cols[p]) for e in range(bl) for p in range(GLA_PAIRS)]

    def body(q_ref, k_ref, v_ref, og_ref, lr_ref, st_ref, dy_ref, wg_ref, bg_ref, nw_ref,
             da_ref, dd_ref, dwg_ref, dbg_ref, dnw_ref, ds_scr):
        @pl.when(pl.program_id(0) == 0)
        def _():
            dwg_ref[...] = jnp.zeros_like(dwg_ref)
            dbg_ref[...] = jnp.zeros_like(dbg_ref)
            dnw_ref[...] = jnp.zeros_like(dnw_ref)
            ds_scr[...] = jnp.zeros_like(ds_scr)

        _, vjp = jax.vjp(_gla_chunk, [q_ref[e, :, cols] for e, _, cols in pairs],
                         [k_ref[e, :, cols] for e, _, cols in pairs], [lr_ref[e] for e in range(bl)],
                         [v_ref[e, :, cols] for e, _, cols in heads], [og_ref[e, :, cols] for e, _, cols in heads],
                         [st_ref[e, 0, p] for e, p, _ in pairs],
                         [wg_ref[:, cols] for cols in pair_cols], [bg_ref[:, cols] for cols in pair_cols], nw_ref[...])
        dq, dk, dlr, dv, dog, ds, dwg, dbg, dnw = vjp(([dy_ref[e, :, cols] for e, _, cols in heads],
                                                         [ds_scr[e, p] for e, p, _ in pairs]))
        for e in range(bl):
            dd_ref[e] = dlr[e]
        for i, (e, p, cols) in enumerate(pairs):
            da_ref[e, :, cols] = dq[i]
            da_ref[e, :, GLA_QK + p * 128:GLA_QK + (p + 1) * 128] = dk[i]
            ds_scr[e, p] = ds[i]
        for i, (e, h, _) in enumerate(heads):
            da_ref[e, :, 512 + h * 128:512 + (h + 1) * 128] = dv[i]
            da_ref[e, :, 1024 + h * 128:1024 + (h + 1) * 128] = dog[i]
        for p, cols in enumerate(pair_cols):
            dwg_ref[:, cols] += dwg[p]
            dbg_ref[:, cols] += dbg[p]
        dnw_ref[...] += dnw

    tok = lambda w, j: pl.BlockSpec((bl, CHUNK, w), lambda i: (0, nc - 1 - i, j))
    const = lambda i: (0, 0)
    pa3 = pa.reshape(bl, seq, 1536)
    da, dd, dwg, dbg, dnw = pl.pallas_call(
        body, name="gla_bwd", grid=(nc,),
        in_specs=[tok(256, 0), tok(256, 1), tok(512, 1), tok(512, 2), tok(128, 0),
                  pl.BlockSpec((bl, 1, GLA_PAIRS, 128, 128), lambda i: (0, nc - 1 - i, 0, 0, 0)), tok(512, 0),
                  pl.BlockSpec(wg.shape, const), pl.BlockSpec(bg.shape, const), pl.BlockSpec(nw.shape, const)],
        out_specs=[tok(1536, 0), tok(128, 0),
                   pl.BlockSpec(wg.shape, const), pl.BlockSpec(bg.shape, const), pl.BlockSpec(nw.shape, const)],
        out_shape=[jax.ShapeDtypeStruct((bl, seq, 1536), F32), jax.ShapeDtypeStruct((bl, seq, 128), F32),
                   jax.ShapeDtypeStruct(wg.shape, F32), jax.ShapeDtypeStruct(bg.shape, F32),
                   jax.ShapeDtypeStruct(nw.shape, F32)],
        scratch_shapes=[pltpu.VMEM((bl, GLA_PAIRS, 128, 128), F32)],
        compiler_params=_params("arbitrary"),
    )(pa3, pa3, pa3, pa3, pd.reshape(bl, seq, 128), st, dya.reshape(bl, seq, 512), wg, bg, nw)
    return da.reshape(n, 1536), dd.reshape(n, 128), dwg, dbg, dnw


def _conv_taps(buf_ref, w_ref, base, rows):
    acc = w_ref[0:1, :] * buf_ref[pl.ds(base, rows), :]
    for k in range(1, CONV_K):
        acc = acc + w_ref[k:k + 1, :] * buf_ref[pl.ds(base + k, rows), :]
    return acc


def _gdn_pre_fwd(pb, pd, cw8, alog_v, dtb_v, bl, seq, tm=256):
    n = pb.shape[0]
    tpe = seq // tm
    t8 = tm // 8

    def body(u_ref, prev_ref, ab_ref, w_ref, al_ref, dt_ref, qkv_ref, gb_ref, p_ref, buf):
        i = pl.program_id(0)
        keep = (i % tpe != 0).astype(F32)
        buf[0:8, :] = prev_ref[...] * keep
        buf[8:8 + tm, :] = u_ref[...]
        p = _conv_taps(buf, w_ref, 8 - (CONV_K - 1), tm)
        p_ref[...] = p
        ps = [p[:, j * 128:(j + 1) * 128] for j in range(12)]
        outs = _gdn_pre_elem(ps, ab_ref[...], al_ref[...], dt_ref[...])
        for j in range(12):
            qkv_ref[:, j * 128:(j + 1) * 128] = outs[j]
        gb_ref[...] = outs[12]

    row = lambda i: (i, 0)
    const = lambda i: (0, 0)
    return pl.pallas_call(
        body, name="gdn_pre_fwd", grid=(n // tm,),
        in_specs=[pl.BlockSpec((tm, 1536), row),
                  pl.BlockSpec((8, 1536), lambda i: (jnp.maximum(i * t8 - 1, 0), 0)),
                  pl.BlockSpec((tm, 128), row),
                  pl.BlockSpec((8, 1536), const), pl.BlockSpec((1, 128), const), pl.BlockSpec((1, 128), const)],
        out_specs=[pl.BlockSpec((tm, 1536), row), pl.BlockSpec((tm, 128), row), pl.BlockSpec((tm, 1536), row)],
        out_shape=[jax.ShapeDtypeStruct((n, 1536), F32), jax.ShapeDtypeStruct((n, 128), F32),
                   jax.ShapeDtypeStruct((n, 1536), F32)],
        scratch_shapes=[pltpu.VMEM((tm + 8, 1536), F32)],
        compiler_params=_params("parallel"),
    )(pb, pb, pd, cw8, alog_v, dtb_v)


def _gdn_pre_bwd(pb, conv_out, pd, dqkv, dgb, cw8, alog_v, dtb_v, bl, seq, tm=256):
    n = pb.shape[0]
    tpe = seq // tm
    t8 = tm // 8
    nb8 = n // 8
    ext = tm + 8

    def body(u_ref, p_ref, pn_ref, ab_ref, abn_ref, dq_ref, dqn_ref, dgb_ref, w_ref, al_ref, dt_ref,
             du_ref, dab_ref, dw_ref, dal_ref, ddt_ref, dpbuf):
        i = pl.program_id(0)

        @pl.when(i == 0)
        def _():
            dw_ref[...] = jnp.zeros_like(dw_ref)
            dal_ref[...] = jnp.zeros_like(dal_ref)
            ddt_ref[...] = jnp.zeros_like(ddt_ref)

        keep_next = (i % tpe != tpe - 1).astype(F32)
        ps = [jnp.concatenate([p_ref[:, j * 128:(j + 1) * 128], pn_ref[:, j * 128:(j + 1) * 128]], axis=0)
              for j in range(12)]
        ab = jnp.concatenate([ab_ref[...], abn_ref[...]], axis=0)
        _, vjp = jax.vjp(_gdn_pre_elem, ps, ab, al_ref[...], dt_ref[...])
        zeros8 = jnp.zeros((8, 128), F32)
        cts = tuple(jnp.concatenate([dq_ref[:, j * 128:(j + 1) * 128],
                                     dqn_ref[:, j * 128:(j + 1) * 128] * keep_next], axis=0) for j in range(12))
        cts += (jnp.concatenate([dgb_ref[...], zeros8], axis=0),)
        dps, dab, dal, ddt = vjp(cts)
        for j in range(12):
            dpbuf[:, j * 128:(j + 1) * 128] = dps[j]
        dab_ref[...] = dab[0:tm, :]
        dal_ref[...] += dal
        ddt_ref[...] += ddt
        u = u_ref[...]
        du = None
        for k in range(CONV_K):
            dp_k = dpbuf[pl.ds(CONV_K - 1 - k, tm), :]
            term = w_ref[k:k + 1, :] * dp_k
            du = term if du is None else du + term
            dw_ref[k:k + 1, :] += jnp.sum(u * dp_k, axis=0, keepdims=True)
        du_ref[...] = du

    row = lambda i: (i, 0)
    next8 = lambda i: (jnp.minimum((i + 1) * t8, nb8 - 1), 0)
    const = lambda i: (0, 0)
    return pl.pallas_call(
        body, name="gdn_pre_bwd", grid=(n // tm,),
        in_specs=[pl.BlockSpec((tm, 1536), row), pl.BlockSpec((tm, 1536), row), pl.BlockSpec((8, 1536), next8),
                  pl.BlockSpec((tm, 128), row), pl.BlockSpec((8, 128), next8),
                  pl.BlockSpec((tm, 1536), row), pl.BlockSpec((8, 1536), next8),
                  pl.BlockSpec((tm, 128), row),
                  pl.BlockSpec((8, 1536), const), pl.BlockSpec((1, 128), const), pl.BlockSpec((1, 128), const)],
        out_specs=[pl.BlockSpec((tm, 1536), row), pl.BlockSpec((tm, 128), row),
                   pl.BlockSpec((8, 1536), const), pl.BlockSpec((1, 128), const), pl.BlockSpec((1, 128), const)],
        out_shape=[jax.ShapeDtypeStruct((n, 1536), F32), jax.ShapeDtypeStruct((n, 128), F32),
                   jax.ShapeDtypeStruct((8, 1536), F32), jax.ShapeDtypeStruct((1, 128), F32),
                   jax.ShapeDtypeStruct((1, 128), F32)],
        scratch_shapes=[pltpu.VMEM((ext, 1536), F32)],
        compiler_params=_params("arbitrary"),
    )(pb, conv_out, conv_out, pd, pd, dqkv, dqkv, dgb, cw8, alog_v, dtb_v)


GDN_PREP_CHUNKS = 2
GDN_SCAN_CHUNKS = 2
MM_DTYPE = BF16


def _head_cols(ref, rows, base=0):
    return [ref[rows, base + h * 128:base + (h + 1) * 128] for h in range(GDN_HEADS)]


def _gdn_prep(qkv, gb):
    n = qkv.shape[0]
    r_per = GDN_PREP_CHUNKS
    tm = r_per * CHUNK

    def body(q_ref, k_ref, v_ref, gb_ref, u_ref, w_ref, qd_ref, kd_ref, qk_ref, t_ref, gam_ref):
        rowid = _iota((8, 128), 0)
        chunk_rows = [slice(r * CHUNK, (r + 1) * CHUNK) for r in range(r_per)]
        gather = lambda ref: [t for rows in chunk_rows for t in _head_cols(ref, rows)]
        u, w, qk, qd, kd, gamma, tinv = _gdn_prep_units(gather(q_ref), gather(k_ref), gather(v_ref),
                                                        [gb_ref[rows, :] for rows in chunk_rows])
        for r, rows in enumerate(chunk_rows):
            gam = jnp.zeros((8, 128), F32)
            for h in range(GDN_HEADS):
                i = r * GDN_HEADS + h
                cols = slice(h * 128, (h + 1) * 128)
                u_ref[rows, cols] = u[i]
                w_ref[rows, cols] = w[i].astype(MM_DTYPE)
                qd_ref[rows, cols] = qd[i].astype(MM_DTYPE)
                kd_ref[rows, cols] = kd[i].astype(MM_DTYPE)
                qk_ref[r, h] = qk[i].astype(MM_DTYPE)
                t_ref[r, h] = tinv[i].astype(MM_DTYPE)
                gam = jnp.where(rowid == h, gamma[i], gam)
            gam_ref[r] = gam

    tok = lambda j: pl.BlockSpec((tm, 512), lambda i: (i, j))
    return pl.pallas_call(
        body, name="gdn_prep", grid=(n // tm,),
        in_specs=[tok(0), tok(1), tok(2), pl.BlockSpec((tm, 128), lambda i: (i, 0))],
        out_specs=[tok(0)] * 4 + [pl.BlockSpec((r_per, GDN_HEADS, CHUNK, CHUNK), lambda i: (i, 0, 0, 0))] * 2
        + [pl.BlockSpec((r_per, 8, 128), lambda i: (i, 0, 0))],
        out_shape=[jax.ShapeDtypeStruct((n, 512), F32)] + [jax.ShapeDtypeStruct((n, 512), MM_DTYPE)] * 3
        + [jax.ShapeDtypeStruct((n // CHUNK, GDN_HEADS, CHUNK, CHUNK), MM_DTYPE)] * 2
        + [jax.ShapeDtypeStruct((n // CHUNK, 8, 128), F32)],
        compiler_params=_params("parallel"),
    )(qkv, qkv, qkv, gb)


def _gdn_fwd(qkv, gb, pc, nw, bl, seq):
    n = qkv.shape[0]
    nc = seq // CHUNK
    u, w, qd, kd, qk, tinv, gam = _gdn_prep(qkv, gb)
    tok3 = lambda t: t.reshape(bl, seq, 512)
    qk5 = qk.reshape(bl, nc, GDN_HEADS, CHUNK, CHUNK)
    gam4 = gam.reshape(bl, nc, 8, 128)

    r_per = GDN_SCAN_CHUNKS
    mm = lambda t: t.astype(MM_DTYPE)

    def body(u_ref, w_ref, qd_ref, kd_ref, qk_ref, gam_ref, og_ref, nw_ref, o_ref, y_ref, vn_ref, st_ref, s_scr):
        @pl.when(pl.program_id(0) == 0)
        def _():
            s_scr[...] = jnp.zeros_like(s_scr)

        units = [(b, h, slice(h * 128, (h + 1) * 128)) for b in range(bl) for h in range(GDN_HEADS)]
        ss = [s_scr[b, h] for b, h, _ in units]
        for r in range(r_per):
            rows = slice(r * CHUNK, (r + 1) * CHUNK)
            for (b, h, _), s in zip(units, ss):
                st_ref[b, r, h] = s
            ws_qs = [_dot(jnp.concatenate([w_ref[b, rows, cols], qd_ref[b, rows, cols]], axis=0), mm(s))
                     for (b, h, cols), s in zip(units, ss)]
            v_new = [u_ref[b, rows, cols] - m[0:CHUNK] for (b, h, cols), m in zip(units, ws_qs)]
            os_ = [m[CHUNK:2 * CHUNK] + _dot(qk_ref[b, r, h], mm(vn))
                   for (b, h, cols), m, vn in zip(units, ws_qs, v_new)]
            ss = [s * gam_ref[b, r, h:h + 1, :] + _dot_tn(kd_ref[b, rows, cols], mm(vn))
                  for (b, h, cols), s, vn in zip(units, ss, v_new)]
            for (b, h, cols), vn, o in zip(units, v_new, os_):
                vn_ref[b, rows, cols] = mm(vn)
                o_ref[b, rows, cols] = o
                y_ref[b, rows, cols] = _rms_gate(o, nw_ref[...], og_ref[b, rows, cols])
        for (b, h, _), s in zip(units, ss):
            s_scr[b, h] = s

    tok = pl.BlockSpec((bl, r_per * CHUNK, 512), lambda i: (0, i, 0))
    st_spec = pl.BlockSpec((bl, r_per, GDN_HEADS, 128, 128), lambda i: (0, i, 0, 0, 0))
    tok_shape = jax.ShapeDtypeStruct((bl, seq, 512), F32)
    o, y, vn, st = pl.pallas_call(
        body, name="gdn_scan_fwd", grid=(nc // r_per,),
        in_specs=[tok, tok, tok, tok,
                  pl.BlockSpec((bl, r_per, GDN_HEADS, CHUNK, CHUNK), lambda i: (0, i, 0, 0, 0)),
                  pl.BlockSpec((bl, r_per, 8, 128), lambda i: (0, i, 0, 0)), tok,
                  pl.BlockSpec(nw.shape, lambda i: (0, 0))],
        out_specs=[tok, tok, tok, st_spec],
        out_shape=[tok_shape, tok_shape, jax.ShapeDtypeStruct((bl, seq, 512), MM_DTYPE),
                   jax.ShapeDtypeStruct((bl, nc, GDN_HEADS, 128, 128), F32)],
        scratch_shapes=[pltpu.VMEM((bl, GDN_HEADS, 128, 128), F32)],
        compiler_params=_params("arbitrary"),
    )(tok3(u), tok3(w), tok3(qd), tok3(kd), qk5, gam4, tok3(pc), nw)
    return y.reshape(n, 512), (o, st, w, qd, kd, qk5, gam4, tinv, vn)


def _gdn_bwd(qkv, gb, pc, res, dyb, nw, bl, seq):
    n = qkv.shape[0]
    nc = seq // CHUNK
    o, st, w, qd, kd, qk5, gam4, tinv, vn = res
    tok3 = lambda t: t.reshape(bl, seq, 512)

    def scan_body(dy_ref, o_ref, og_ref, w_ref, qd_ref, kd_ref, qk_ref, gam_ref, nw_ref,
                  do_ref, dog_ref, dvn_ref, dst_ref, dnw_ref, ds_scr):
        @pl.when(pl.program_id(0) == 0)
        def _():
            ds_scr[...] = jnp.zeros_like(ds_scr)
            dnw_ref[...] = jnp.zeros_like(dnw_ref)

        units = [(b, h, slice(h * 128, (h + 1) * 128)) for b in range(bl) for h in range(GDN_HEADS)]
        dnw = jnp.zeros(nw.shape, F32)
        dss = [ds_scr[b, h] for b, h, _ in units]
        for r in reversed(range(r_scan)):
            rows = slice(r * CHUNK, (r + 1) * CHUNK)
            d_os = []
            for b, h, cols in units:
                _, vjp = jax.vjp(_rms_gate, o_ref[b, rows, cols], nw_ref[...], og_ref[b, rows, cols])
                d_o, dnw_h, dog = vjp(dy_ref[b, rows, cols])
                do_ref[b, rows, cols] = mm(d_o)
                dog_ref[b, rows, cols] = dog
                dnw = dnw + dnw_h
                d_os.append(mm(d_o))
            for (b, h, _), ds in zip(units, dss):
                dst_ref[b, r, h] = ds
            dvn_a = [_dot(kd_ref[b, rows, cols], mm(ds)) for (b, h, cols), ds in zip(units, dss)]
            dvns = [a + _dot_tn(qk_ref[b, r, h], d_o) for (b, h, cols), a, d_o in zip(units, dvn_a, d_os)]
            for (b, h, cols), dvn in zip(units, dvns):
                dvn_ref[b, rows, cols] = mm(dvn)
            dss = [ds * gam_ref[b, r, h:h + 1, :] + _dot_tn(
                jnp.concatenate([qd_ref[b, rows, cols], w_ref[b, rows, cols]], axis=0),
                jnp.concatenate([d_o, mm(-dvn)], axis=0))
                for (b, h, cols), d_o, ds, dvn in zip(units, d_os, dss, dvns)]
        dnw_ref[...] += dnw
        for (b, h, _), ds in zip(units, dss):
            ds_scr[b, h] = ds

    r_scan = GDN_SCAN_CHUNKS
    mm = lambda t: t.astype(MM_DTYPE)
    rev = lambda i: nc // r_scan - 1 - i
    tok = pl.BlockSpec((bl, r_scan * CHUNK, 512), lambda i: (0, rev(i), 0))
    st_spec = pl.BlockSpec((bl, r_scan, GDN_HEADS, 128, 128), lambda i: (0, rev(i), 0, 0, 0))
    tok_shape = jax.ShapeDtypeStruct((bl, seq, 512), F32)
    tok_mm = jax.ShapeDtypeStruct((bl, seq, 512), MM_DTYPE)
    d_o, dog, dvn, dst, dnw = pl.pallas_call(
        scan_body, name="gdn_scan_bwd", grid=(nc // r_scan,),
        in_specs=[tok] * 6 + [pl.BlockSpec((bl, r_scan, GDN_HEADS, CHUNK, CHUNK), lambda i: (0, rev(i), 0, 0, 0)),
                              pl.BlockSpec((bl, r_scan, 8, 128), lambda i: (0, rev(i), 0, 0)),
                              pl.BlockSpec(nw.shape, lambda i: (0, 0))],
        out_specs=[tok, tok, tok, st_spec, pl.BlockSpec(nw.shape, lambda i: (0, 0))],
        out_shape=[tok_mm, tok_shape, tok_mm, jax.ShapeDtypeStruct(st.shape, F32),
                   jax.ShapeDtypeStruct(nw.shape, F32)],
        scratch_shapes=[pltpu.VMEM((bl, GDN_HEADS, 128, 128), F32)],
        compiler_params=_params("arbitrary"),
    )(tok3(dyb), o, tok3(pc), tok3(w), tok3(qd), tok3(kd), qk5, gam4, nw)

    r_per = GDN_PREP_CHUNKS
    tm = r_per * CHUNK

    def prep_body(q_ref, k_ref, v_ref, gb_ref, t_ref, st_ref, dst_ref, dvn_ref, do_ref, vn_ref, dqkv_ref, dgb_ref):
        chunk_rows = [slice(r * CHUNK, (r + 1) * CHUNK) for r in range(r_per)]
        gather = lambda ref: [t for rows in chunk_rows for t in _head_cols(ref, rows)]
        units = [(r, h) for r in range(r_per) for h in range(GDN_HEADS)]
        t_known = [t_ref[r, h].astype(F32) for r, h in units]
        prep = lambda q, k, v, g: _gdn_prep_units(q, k, v, g, t_known)[:6]
        _, vjp = jax.vjp(prep, gather(q_ref), gather(k_ref), gather(v_ref), [gb_ref[rows, :] for rows in chunk_rows])
        ss = [st_ref[r, h] for r, h in units]
        dss = [dst_ref[r, h] for r, h in units]
        dvns, d_os, v_new = gather(dvn_ref), gather(do_ref), gather(vn_ref)
        both = [_dot_nt(jnp.concatenate([dvn, d_o], axis=0), s.astype(MM_DTYPE)) for dvn, d_o, s in zip(dvns, d_os, ss)]
        d_w = [-m[0:CHUNK] for m in both]
        d_qd = [m[CHUNK:2 * CHUNK] for m in both]
        d_qk = [_dot_nt(d_o, vn) for d_o, vn in zip(d_os, v_new)]
        d_kd = [_dot_nt(vn, ds.astype(MM_DTYPE)) for vn, ds in zip(v_new, dss)]
        d_gam = [_sum_all(ds * s) for ds, s in zip(dss, ss)]
        dq, dk, dv, dgb = vjp(([d.astype(F32) for d in dvns], d_w, d_qk, d_qd, d_kd, d_gam))
        for i, (r, h) in enumerate(units):
            rows = chunk_rows[r]
            for part, d in enumerate((dq, dk, dv)):
                dqkv_ref[rows, part * 512 + h * 128:part * 512 + (h + 1) * 128] = d[i]
        for r, rows in enumerate(chunk_rows):
            dgb_ref[rows, :] = dgb[r]

    tokp = lambda j: pl.BlockSpec((tm, 512), lambda i: (i, j))
    st4 = pl.BlockSpec((r_per, GDN_HEADS, 128, 128), lambda i: (i, 0, 0, 0))
    dqkv, dgb = pl.pallas_call(
        prep_body, name="gdn_prep_bwd", grid=(n // tm,),
        in_specs=[tokp(0), tokp(1), tokp(2), pl.BlockSpec((tm, 128), lambda i: (i, 0)),
                  pl.BlockSpec((r_per, GDN_HEADS, CHUNK, CHUNK), lambda i: (i, 0, 0, 0)), st4, st4,
                  tokp(0), tokp(0), tokp(0)],
        out_specs=[pl.BlockSpec((tm, 1536), lambda i: (i, 0)), pl.BlockSpec((tm, 128), lambda i: (i, 0))],
        out_shape=[jax.ShapeDtypeStruct((n, 1536), F32), jax.ShapeDtypeStruct((n, 128), F32)],
        compiler_params=_params("parallel"),
    )(qkv, qkv, qkv, gb, tinv, st.reshape(bl * nc, GDN_HEADS, 128, 128), dst.reshape(bl * nc, GDN_HEADS, 128, 128),
      dvn.reshape(n, 512), d_o.reshape(n, 512), vn.reshape(n, 512))
    return dqkv, dog.reshape(n, 512), dgb, dnw


def _out_block(x2, tgt2, ya, yb, g1p3, wo, lnw, lnb, seq, tm=256):
    n = x2.shape[0]
    tpe = seq // tm
    bl = n // seq

    def body(x_ref, t_ref, ya_ref, yb_ref, g_ref, wo_ref, lnw_ref, lnb_ref,
             dz_ref, dya_ref, dyb_ref, dwo_ref, dg_ref, glw_ref, glb_ref, loss_ref):
        i = pl.program_id(0)

        @pl.when(i == 0)
        def _():
            dwo_ref[...] = jnp.zeros_like(dwo_ref)
            glw_ref[...] = jnp.zeros_like(glw_ref)
            glb_ref[...] = jnp.zeros_like(glb_ref)
            loss_ref[...] = jnp.zeros_like(loss_ref)

        @pl.when(i % tpe == 0)
        def _():
            dg_ref[...] = jnp.zeros_like(dg_ref)

        ya16 = ya_ref[...].astype(wo.dtype)
        yb16 = yb_ref[...].astype(wo.dtype)
        wa = wo_ref[0:GLA_WIDTH, :]
        wb = wo_ref[GLA_WIDTH:, :]
        y = _dot(ya16, wa) + _dot(yb16, wb)
        g1p = g_ref[0]
        z = ALPHA * x_ref[...] + g1p * y
        mu = jnp.mean(z, axis=-1, keepdims=True)
        zc = z - mu
        rstd = lax.rsqrt(jnp.mean(zc * zc, axis=-1, keepdims=True) + LN_EPS)
        xhat = zc * rstd
        diff = xhat * lnw_ref[...] + lnb_ref[...] - t_ref[...]
        loss_ref[...] += (0.5 / D_MODEL) * jnp.sum(jnp.sum(diff * diff, axis=-1, keepdims=True), axis=0, keepdims=True)
        dout = diff * (1.0 / D_MODEL)
        glw_ref[...] += jnp.sum(dout * xhat, axis=0, keepdims=True)
        glb_ref[...] += jnp.sum(dout, axis=0, keepdims=True)
        dxh = dout * lnw_ref[...]
        dz = rstd * (dxh - jnp.mean(dxh, axis=-1, keepdims=True)
                     - xhat * jnp.mean(dxh * xhat, axis=-1, keepdims=True))
        dz_ref[...] = dz
        dg_ref[0] += jnp.sum(dz * y, axis=0, keepdims=True)
        dy = (g1p * dz).astype(wo.dtype)
        dya_ref[...] = _dot_nt(dy, wa)
        dyb_ref[...] = _dot_nt(dy, wb)
        dwo_ref[0:GLA_WIDTH, :] += _dot_tn(ya16, dy)
        dwo_ref[GLA_WIDTH:, :] += _dot_tn(yb16, dy)

    row = lambda i: (i, 0)
    const = lambda i: (0, 0)
    per_ex = pl.BlockSpec((1, 1, D_MODEL), lambda i: (i // tpe, 0, 0))
    return pl.pallas_call(
        body, name="out_block", grid=(n // tm,),
        in_specs=[pl.BlockSpec((tm, D_MODEL), row), pl.BlockSpec((tm, D_MODEL), row),
                  pl.BlockSpec((tm, 512), row), pl.BlockSpec((tm, 512), row), per_ex,
                  pl.BlockSpec((D_MODEL, D_MODEL), const), pl.BlockSpec((1, D_MODEL), const),
                  pl.BlockSpec((1, D_MODEL), const)],
        out_specs=[pl.BlockSpec((tm, D_MODEL), row), pl.BlockSpec((tm, 512), row), pl.BlockSpec((tm, 512), row),
                   pl.BlockSpec((D_MODEL, D_MODEL), const), per_ex,
                   pl.BlockSpec((1, D_MODEL), const), pl.BlockSpec((1, D_MODEL), const),
                   pl.BlockSpec((1, 1), const)],
        out_shape=[jax.ShapeDtypeStruct((n, D_MODEL), F32), jax.ShapeDtypeStruct((n, 512), F32),
                   jax.ShapeDtypeStruct((n, 512), F32), jax.ShapeDtypeStruct((D_MODEL, D_MODEL), F32),
                   jax.ShapeDtypeStruct((bl, 1, D_MODEL), F32), jax.ShapeDtypeStruct((1, D_MODEL), F32),
                   jax.ShapeDtypeStruct((1, D_MODEL), F32), jax.ShapeDtypeStruct((1, 1), F32)],
        compiler_params=_params("arbitrary"),
    )(x2, tgt2, ya, yb, g1p3, wo, lnw, lnb)


def _proj_bwd_x(ds, ws, x2, dz, sc3, seq, tm=256):
    n = x2.shape[0]
    tpe = seq // tm
    bl = n // seq

    def body(da_ref, db_ref, dc_ref, dd1_ref, dd2_ref, wa_ref, wb_ref, wc_ref, wd_ref, x_ref, dz_ref, sc_ref,
             gx_ref, dsh_ref, dsc_ref):
        i = pl.program_id(0)

        @pl.when(i % tpe == 0)
        def _():
            dsh_ref[...] = jnp.zeros_like(dsh_ref)
            dsc_ref[...] = jnp.zeros_like(dsc_ref)

        cdt = ws[0].dtype
        dh = _dot(da_ref[...].astype(cdt), wa_ref[...])
        dh += _dot(db_ref[...].astype(cdt), wb_ref[...])
        dh += _dot(dc_ref[...].astype(cdt), wc_ref[...])
        dh += _dot((dd1_ref[...] + dd2_ref[...]).astype(cdt), wd_ref[...])
        gx_ref[...] = dh * sc_ref[0] + ALPHA * dz_ref[...]
        dsh_ref[0] += jnp.sum(dh, axis=0, keepdims=True)
        dsc_ref[0] += jnp.sum(dh * x_ref[...], axis=0, keepdims=True)

    row = lambda i: (i, 0)
    const = lambda i: (0, 0)
    per_ex = pl.BlockSpec((1, 1, D_MODEL), lambda i: (i // tpe, 0, 0))
    da, db, dc, (dd1, dd2) = ds
    return pl.pallas_call(
        body, name="proj_bwd_x", grid=(n // tm,),
        in_specs=[pl.BlockSpec((tm, d.shape[1]), row) for d in (da, db, dc, dd1, dd2)]
        + [pl.BlockSpec(w.shape, const) for w in ws]
        + [pl.BlockSpec((tm, D_MODEL), row), pl.BlockSpec((tm, D_MODEL), row), per_ex],
        out_specs=[pl.BlockSpec((tm, D_MODEL), row), per_ex, per_ex],
        out_shape=[jax.ShapeDtypeStruct((n, D_MODEL), F32), jax.ShapeDtypeStruct((bl, 1, D_MODEL), F32),
                   jax.ShapeDtypeStruct((bl, 1, D_MODEL), F32)],
        compiler_params=_params("arbitrary"),
    )(da, db, dc, dd1, dd2, *ws, x2, dz, sc3)


def _proj_bwd_w(x2, sc3, sh3, ds, seq, cdt, name, tm=256):
    n = x2.shape[0]
    tpe = seq // tm
    flat, groups = [], []
    for d in ds:
        parts = d if isinstance(d, tuple) else (d,)
        groups.append(len(parts))
        flat.extend(parts)
    nin = len(flat)

    def body(x_ref, sc_ref, sh_ref, *refs):
        i = pl.program_id(0)
        outs = refs[nin:]

        @pl.when(i == 0)
        def _():
            for o in outs:
                o[...] = jnp.zeros_like(o)

        h = (x_ref[...] * sc_ref[0] + sh_ref[0]).astype(cdt)
        pos = 0
        for o, cnt in zip(outs, groups):
            d = refs[pos][...]
            for extra in refs[pos + 1:pos + cnt]:
                d = d + extra[...]
            pos += cnt
            o[...] += _dot_tn(d.astype(cdt), h)

    row = lambda i: (i, 0)
    const = lambda i: (0, 0)
    per_ex = pl.BlockSpec((1, 1, D_MODEL), lambda i: (i // tpe, 0, 0))
    widths = [(d[0] if isinstance(d, tuple) else d).shape[1] for d in ds]
    return pl.pallas_call(
        body, name=name, grid=(n // tm,),
        in_specs=[pl.BlockSpec((tm, D_MODEL), row), per_ex, per_ex]
        + [pl.BlockSpec((tm, d.shape[1]), row) for d in flat],
        out_specs=[pl.BlockSpec((w, D_MODEL), const) for w in widths],
        out_shape=[jax.ShapeDtypeStruct((w, D_MODEL), F32) for w in widths],
        compiler_params=_params("arbitrary"),
    )(x2, sc3, sh3, *flat)


def _mod_block(c_all, w_ada_sh, b_blk):
    def body(c_ref, w_ref, b_ref, o_ref):
        o_ref[...] = _dot(c_ref[...], w_ref[...]) + b_ref[...]

    return pl.pallas_call(
        body, name="mod_block",
        out_shape=jax.ShapeDtypeStruct((c_all.shape[0], w_ada_sh.shape[1]), F32),
        compiler_params=pltpu.CompilerParams(vmem_limit_bytes=VMEM_LIMIT),
    )(c_all, w_ada_sh, b_blk)


def _ada_grads(c_all, dmod_all, dmod_blk):
    def body(c_ref, da_ref, db_ref, gw_ref, gb_ref):
        gw_ref[...] = _dot_tn(c_ref[...], db_ref[...])
        gb_ref[...] = jnp.sum(da_ref[...], axis=0, keepdims=True)

    return pl.pallas_call(
        body, name="ada_grads",
        out_shape=[jax.ShapeDtypeStruct((c_all.shape[1], dmod_blk.shape[1]), F32),
                   jax.ShapeDtypeStruct((1, dmod_all.shape[1]), F32)],
        compiler_params=pltpu.CompilerParams(vmem_limit_bytes=VMEM_LIMIT),
    )(c_all, dmod_all, dmod_blk)


def _sum_leading(parts, name):
    def body(p_ref, o_ref):
        acc = p_ref[0]
        for d in range(1, parts.shape[0]):
            acc = acc + p_ref[d]
        o_ref[...] = acc

    return pl.pallas_call(
        body, name=name, out_shape=jax.ShapeDtypeStruct(parts.shape[1:], F32),
        compiler_params=pltpu.CompilerParams(vmem_limit_bytes=VMEM_LIMIT),
    )(parts)


ELEMENTWISE_BLOCK_BYTES = 2 * 1024 * 1024


def _tile2d(rows, cols, row_align=8):
    if rows * cols * 4 <= ELEMENTWISE_BLOCK_BYTES:
        return rows, cols
    fits = [t for t in range(row_align, rows, row_align) if rows % t == 0 and t * cols * 4 <= ELEMENTWISE_BLOCK_BYTES]
    if fits:
        return fits[-1], cols
    fits = [t for t in range(128, cols, 128) if cols % t == 0 and rows * t * 4 <= ELEMENTWISE_BLOCK_BYTES]
    assert fits, (rows, cols)
    return rows, fits[-1]


def _add_n(arrs, name, out_dtypes=(F32,)):
    rows, cols = arrs[0].shape
    narrow = any(jnp.dtype(dt).itemsize < 4 for dt in tuple(out_dtypes) + tuple(a.dtype for a in arrs))
    tr, tc = _tile2d(rows, cols, 16 if narrow else 8)
    n_in = len(arrs)

    def body(*refs):
        acc = refs[0][...].astype(F32)
        for r in refs[1:n_in]:
            acc = acc + r[...].astype(F32)
        for o in refs[n_in:]:
            o[...] = acc.astype(o.dtype)

    spec = pl.BlockSpec((tr, tc), lambda i, j: (i, j))
    return pl.pallas_call(
        body, name=name, grid=(rows // tr, cols // tc), in_specs=[spec] * n_in, out_specs=[spec] * len(out_dtypes),
        out_shape=[jax.ShapeDtypeStruct((rows, cols), dt) for dt in out_dtypes],
        compiler_params=_params("parallel", "parallel"),
    )(*arrs)


def _chip_sum_blocks(a, b, per, blocks, name, chunk=128):
    rows, cols = a.shape
    padded = -(-per // 16) * 16
    assert rows >= (blocks - 1) * per + padded, (rows, per, blocks)

    def body(a_ref, b_ref, o_ref, o16_ref):
        for j in range(blocks):
            for r0 in range(0, padded, chunk):
                n_rows = min(chunk, padded - r0)
                src = pl.ds(j * per + r0, n_rows)
                s = a_ref[src, :] + b_ref[src, :]
                if per - r0 < n_rows:
                    s = jnp.where(_iota((n_rows, 1), 0) < per - r0, s, 0.0)
                o_ref[j, r0:r0 + n_rows, :] = s
                o16_ref[j, r0:r0 + n_rows, :] = s.astype(BF16)

    return pl.pallas_call(
        body, name=name,
        out_shape=[jax.ShapeDtypeStruct((blocks, padded, cols), F32), jax.ShapeDtypeStruct((blocks, padded, cols), BF16)],
        compiler_params=pltpu.CompilerParams(vmem_limit_bytes=VMEM_LIMIT),
    )(a, b)


GRAD_PAD_ROWS = 16


def _adamw(w, g, m, v, name):
    rows, cols = w.shape
    tr, tc = _tile2d(rows, cols)
    c1 = 1.0 / (1.0 - ADAM_B1 ** ADAM_STEP)
    c2 = 1.0 / (1.0 - ADAM_B2 ** ADAM_STEP)

    def body(w_ref, g_ref, m_ref, v_ref, d_ref, nm_ref, nv_ref):
        gg = g_ref[...]
        nm = ADAM_B1 * m_ref[...] + (1.0 - ADAM_B1) * gg
        nv = ADAM_B2 * v_ref[...] + (1.0 - ADAM_B2) * (gg * gg)
        nm_ref[...] = nm
        nv_ref[...] = nv
        d_ref[...] = -ADAM_LR * ((nm * c1) / (jnp.sqrt(nv * c2) + ADAM_EPS) + ADAM_WD * w_ref[...])

    spec = pl.BlockSpec((tr, tc), lambda i, j: (i, j))
    shp = jax.ShapeDtypeStruct((rows, cols), F32)
    return pl.pallas_call(
        body, name=name, grid=(rows // tr, cols // tc), in_specs=[spec] * 4, out_specs=[spec] * 3,
        out_shape=[shp, shp, shp], compiler_params=_params("parallel", "parallel"),
    )(w, g, m, v)


def _coords():
    return lax.axis_index("x"), lax.axis_index("y"), lax.axis_index("c")


def _all_gather8(blk, name):
    m_per, n = blk.shape

    def body(x_ref, out_ref, send_sems, recv_sems, local_sem):
        x, y, c = _coords()
        me, sibling = (x, y, c), (x, y, 1 - c)
        chips = [(1 - x, y), (x, 1 - y), (1 - x, 1 - y)]

        def rows(px, py, pc):
            return out_ref.at[pl.ds((4 * px + 2 * py + pc) * m_per, m_per), :]

        def copy(k, block, to, src=None):
            return pltpu.make_async_remote_copy(
                src_ref=rows(*block) if src is None else src, dst_ref=rows(*block),
                send_sem=send_sems.at[k], recv_sem=recv_sems.at[k], device_id=to, device_id_type=MESH)

        mine = pltpu.make_async_copy(x_ref, rows(*me), local_sem)
        mine.start()
        first = [copy(0, me, sibling, src=x_ref)]
        first += [copy(1 + j, me, (*chip, c), src=x_ref) for j, chip in enumerate(chips)]
        for cp in first:
            cp.start()
        passed = [copy(4 + j, (*chip, c), sibling) for j, chip in enumerate(chips)]
        for j, chip in enumerate(chips):
            copy(1 + j, (*chip, c), me).wait_recv()
            passed[j].start()
        copy(0, sibling, me).wait_recv()
        for j, chip in enumerate(chips):
            copy(4 + j, (*chip, 1 - c), me).wait_recv()
        for cp in first + passed:
            cp.wait_send()
        mine.wait()

    return pl.pallas_call(
        body, name=name,
        out_shape=jax.ShapeDtypeStruct((8 * m_per, n), blk.dtype),
        in_specs=[pl.BlockSpec(memory_space=pltpu.VMEM)],
        out_specs=pl.BlockSpec(memory_space=pltpu.VMEM),
        scratch_shapes=[pltpu.SemaphoreType.DMA((7,)), pltpu.SemaphoreType.DMA((7,)), pltpu.SemaphoreType.DMA],
        compiler_params=pltpu.CompilerParams(vmem_limit_bytes=VMEM_LIMIT),
    )(blk)


def _chip_gather(shards, split, name):
    k_arr = len(shards)

    def body(*refs):
        srcs, dsts = refs[:k_arr], refs[k_arr:2 * k_arr]
        send_sems, recv_sems, fwd_send_sems, fwd_recv_sems, local_sems = refs[2 * k_arr:]
        x, y, c = _coords()
        peers = [(1 - x, y, c), (x, 1 - y, c), (1 - x, 1 - y, c)]
        sibling = (x, y, 1 - c)
        me_chip = 2 * x + y

        def part(ref, a, core):
            if not split[a]:
                return ref
            half = shards[a].shape[1] // 2
            return ref.at[:, pl.ds(core * half, half)]

        def ici(a, j, src_chip, dst_dev):
            return pltpu.make_async_remote_copy(
                src_ref=part(srcs[a], a, c), dst_ref=part(dsts[a].at[src_chip], a, c),
                send_sem=send_sems.at[a, j], recv_sem=recv_sems.at[a, j], device_id=dst_dev, device_id_type=MESH)

        def d2d(a, j, src_chip, core):
            return pltpu.make_async_remote_copy(
                src_ref=part(dsts[a].at[src_chip], a, core), dst_ref=part(dsts[a].at[src_chip], a, core),
                send_sem=fwd_send_sems.at[a, j], recv_sem=fwd_recv_sems.at[a, j],
                device_id=sibling, device_id_type=MESH)

        local = [pltpu.make_async_copy(srcs[a], dsts[a].at[me_chip], local_sems.at[a]) for a in range(k_arr)]
        for cp in local:
            cp.start()
        sends = [ici(a, j, me_chip, peer) for a in range(k_arr) for j, peer in enumerate(peers)]
        for cp in sends:
            cp.start()
        forwards = []
        for a in range(k_arr):
            for j, peer in enumerate(peers):
                peer_chip = 2 * peer[0] + peer[1]
                ici(a, j, peer_chip, peer).wait_recv()
                if split[a]:
                    forwards.append(d2d(a, j, peer_chip, c))
                    forwards[-1].start()
        for a in range(k_arr):
            for j, peer in enumerate(peers):
                if split[a]:
                    d2d(a, j, 2 * peer[0] + peer[1], 1 - c).wait_recv()
        for cp in sends + forwards:
            cp.wait_send()
        for cp in local:
            cp.wait()

    any_spec = pl.BlockSpec(memory_space=pl.ANY)
    return pl.pallas_call(
        body, name=name,
        out_shape=[jax.ShapeDtypeStruct((4,) + s.shape, s.dtype) for s in shards],
        in_specs=[any_spec] * k_arr, out_specs=[any_spec] * k_arr,
        scratch_shapes=[pltpu.SemaphoreType.DMA((k_arr, 3))] * 4 + [pltpu.SemaphoreType.DMA((k_arr,))],
    )(*shards)


def _chip_scatter(pieces, name):
    k_arr = len(pieces)

    def body(*refs):
        srcs, dsts = refs[:k_arr], refs[k_arr:2 * k_arr]
        send_sems, recv_sems = refs[2 * k_arr:]
        x, y, c = _coords()
        peers = [(1 - x, y, c), (x, 1 - y, c), (1 - x, 1 - y, c)]
        copies = []
        for a in range(k_arr):
            for j, peer in enumerate(peers):
                copies.append(pltpu.make_async_remote_copy(
                    src_ref=srcs[a].at[2 * peer[0] + peer[1]], dst_ref=dsts[a].at[j],
                    send_sem=send_sems.at[a, j], recv_sem=recv_sems.at[a, j], device_id=peer, device_id_type=MESH))
        for cp in copies:
            cp.start()
        for cp in copies:
            cp.wait_recv()
        for cp in copies:
            cp.wait_send()

    any_spec = pl.BlockSpec(memory_space=pl.ANY)
    return pl.pallas_call(
        body, name=name,
        out_shape=[jax.ShapeDtypeStruct((3,) + p.shape[1:], p.dtype) for p in pieces],
        in_specs=[any_spec] * k_arr, out_specs=[any_spec] * k_arr,
        scratch_shapes=[pltpu.SemaphoreType.DMA((k_arr, 3)), pltpu.SemaphoreType.DMA((k_arr, 3))],
    )(*pieces)


def _sibling_swap(arrs, name):
    k_arr = len(arrs)

    def body(*refs):
        srcs, dsts = refs[:k_arr], refs[k_arr:2 * k_arr]
        send_sems, recv_sems = refs[2 * k_arr:]
        x, y, c = _coords()
        copies = [pltpu.make_async_remote_copy(
            src_ref=srcs[a], dst_ref=dsts[a], send_sem=send_sems.at[a], recv_sem=recv_sems.at[a],
            device_id=(x, y, 1 - c), device_id_type=MESH) for a in range(k_arr)]
        for cp in copies:
            cp.start()
        for cp in copies:
            cp.wait_recv()
        for cp in copies:
            cp.wait_send()

    any_spec = pl.BlockSpec(memory_space=pl.ANY)
    return pl.pallas_call(
        body, name=name,
        out_shape=[jax.ShapeDtypeStruct(a.shape, a.dtype) for a in arrs],
        in_specs=[any_spec] * k_arr, out_specs=[any_spec] * k_arr,
        scratch_shapes=[pltpu.SemaphoreType.DMA((k_arr,)), pltpu.SemaphoreType.DMA((k_arr,))],
    )(*arrs)


def _split_w_in(w_in_t):
    wa = jnp.concatenate([w_in_t[0:1024], w_in_t[1040:1552]], axis=0)
    wb = w_in_t[1552:3088]
    wc = w_in_t[3096:3608]
    wd = jnp.concatenate([w_in_t[1024:1040], w_in_t[3088:3096],
                          jnp.zeros((128 - SMALL_USED, w_in_t.shape[1]), w_in_t.dtype)], axis=0)
    return wa, wb, wc, wd


def _merge_dw_in(dwa, dwb, dwc, dwd):
    return jnp.concatenate([dwa[0:1024], dwd[0:GLA_RANK], dwa[1024:1536], dwb, dwd[GLA_RANK:SMALL_USED], dwc,
                            jnp.zeros((GRAD_PAD_ROWS, dwa.shape[1]), dwa.dtype)], axis=0)


def _local_step(x, mod, w_in16, w_out16, gla_wg, gla_bg, gla_nw, conv_w, a_log, dt_bias, gdn_nw, ln_w, ln_b, tgt):
    bl, seq, _ = x.shape
    n = bl * seq
    x2 = x.reshape(n, D_MODEL)
    tgt2 = tgt.reshape(n, D_MODEL)
    sh3 = mod[:, None, 0:D_MODEL]
    sc3 = 1.0 + mod[:, None, D_MODEL:2 * D_MODEL]
    g1p3 = 1.0 + mod[:, None, 2 * D_MODEL:]
    ws = _split_w_in(w_in16)
    wg = jnp.concatenate([gla_wg, jnp.zeros((128 - GLA_RANK, GLA_QK), F32)], axis=0)
    cw8 = jnp.concatenate([conv_w, jnp.zeros((8 - CONV_K, conv_w.shape[1]), F32)], axis=0)
    alog_v = jnp.zeros((1, 128), F32).at[:, LANE_A:LANE_A + GDN_HEADS].set(a_log)
    dtb_v = jnp.zeros((1, 128), F32).at[:, LANE_A:LANE_A + GDN_HEADS].set(dt_bias)

    pa, pb, pc, pd = _proj_fwd(x2, sc3, sh3, ws, seq)
    ya, st_a = _gla_fwd(pa, pd, wg, gla_bg, gla_nw, bl, seq)
    qkv, gb, conv_out = _gdn_pre_fwd(pb, pd, cw8, alog_v, dtb_v, bl, seq)
    yb, st_b = _gdn_fwd(qkv, gb, pc, gdn_nw, bl, seq)
    dz, dya, dyb, d_wo, d_gate, d_lnw, d_lnb, loss = _out_block(x2, tgt2, ya, yb, g1p3, w_out16, ln_w, ln_b, seq)
    da, dd1, d_wg, d_bg, d_nwa = _gla_bwd(pa, pd, st_a, dya, wg, gla_bg, gla_nw, bl, seq)
    dqkv, dc, dgb, d_nwb = _gdn_bwd(qkv, gb, pc, st_b, dyb, gdn_nw, bl, seq)
    db, dd2, d_cw8, d_alog, d_dtb = _gdn_pre_bwd(pb, conv_out, pd, dqkv, dgb, cw8, alog_v, dtb_v, bl, seq)
    gx, d_sh, d_sc = _proj_bwd_x((da, db, dc, (dd1, dd2)), ws, x2, dz, sc3, seq)
    (dwa,) = _proj_bwd_w(x2, sc3, sh3, [da], seq, w_in16.dtype, "proj_bwd_w_a")
    dwb, dwc, dwd = _proj_bwd_w(x2, sc3, sh3, [db, dc, (dd1, dd2)], seq, w_in16.dtype, "proj_bwd_w_bcd")
    grads = dict(
        w_in=_merge_dw_in(dwa, dwb, dwc, dwd),
        w_out=d_wo,
        gla_w_gate_up=d_wg[0:GLA_RANK, :],
        gla_b_gate=d_bg,
        gla_norm_w=d_nwa,
        gdn_conv_w=d_cw8[0:CONV_K, :],
        gdn_a_log=d_alog[:, LANE_A:LANE_A + GDN_HEADS],
        gdn_dt_bias=d_dtb[:, LANE_A:LANE_A + GDN_HEADS],
        gdn_norm_w=d_nwb,
        ln_w=d_lnw,
        ln_b=d_lnb,
        mod=jnp.concatenate([d_sh[:, 0, :], d_sc[:, 0, :], d_gate[:, 0, :]], axis=1),
    )
    return loss, gx.reshape(bl, seq, D_MODEL), grads


_SMALL = (("gla_b_gate", 256), ("gla_norm_w", 128), ("gdn_a_log", 4), ("gdn_dt_bias", 4), ("gdn_norm_w", 128),
          ("ln_w", 1024), ("ln_b", 1024), ("gla_w_gate_up", 16 * 256), ("gdn_conv_w", 4 * 1536), ("loss", 1),
          ("mod", 2 * 3072))


def _pack_small(grads):
    flat = jnp.concatenate([grads[k].reshape(-1) for k, _ in _SMALL])
    total = sum(sz for _, sz in _SMALL)
    rows = -(-total // 1024) * 8
    return jnp.concatenate([flat, jnp.zeros((rows * 128 - total,), F32)]).reshape(rows, 128)


def _unpack_small(flat):
    out, pos = {}, 0
    for k, sz in _SMALL:
        out[k] = flat[pos:pos + sz]
        pos += sz
    return out


def kernel(x, c, w_ada, b_ada, w_in, gla_w_gate_up, gla_b_gate, gla_norm_w, gdn_conv_w, gdn_a_log, gdn_dt_bias, gdn_norm_w, w_out, ln_w, ln_b, loss_target, m_w_ada, m_b_ada, m_w_in, m_gla_w_gate_up, m_gla_b_gate, m_gla_norm_w, m_gdn_conv_w, m_gdn_a_log, m_gdn_dt_bias, m_gdn_norm_w, m_w_out, m_ln_w, m_ln_b, v_w_ada, v_b_ada, v_w_in, v_gla_w_gate_up, v_gla_b_gate, v_gla_norm_w, v_gdn_conv_w, v_gdn_a_log, v_gdn_dt_bias, v_gdn_norm_w, v_w_out, v_ln_w, v_ln_b):
    ix, iy, ic = _coords()
    chip = 2 * ix + iy
    dev = 4 * ix + 2 * iy + ic
    bl = x.shape[0]
    ndev = 8

    c_all = _all_gather8(c.reshape(8, -1), "gather_c").reshape(ndev * bl, D_MODEL)
    ada_cols = w_ada.shape[2]
    b_blk = lax.dynamic_slice_in_dim(b_ada, chip * ada_cols, ada_cols, axis=1)
    mod_blk = _mod_block(c_all, w_ada[0], b_blk)
    mod_g = _all_gather8(mod_blk, "gather_mod").reshape(ndev, ndev * bl, ada_cols)
    mod_all = jnp.concatenate([mod_g[2 * j] for j in range(4)], axis=1)
    mod = lax.dynamic_slice_in_dim(mod_all, dev * bl, bl, axis=0)

    w_in_g, w_out_g, wg_g, cw_g = _chip_gather(
        [jnp.transpose(w_in[0]).astype(BF16), w_out[0].astype(BF16), gla_w_gate_up[0], gdn_conv_w[0]],
        [True, True, False, False], "gather_weights")
    w_in16 = w_in_g.reshape(IN_COLS, D_MODEL)
    w_out16 = w_out_g.reshape(D_MODEL, D_MODEL)
    gla_wg = jnp.concatenate([wg_g[j] for j in range(4)], axis=1)
    conv_w = jnp.concatenate([cw_g[j] for j in range(4)], axis=1)

    loss, grad_x, gr = _local_step(x, mod, w_in16, w_out16, gla_wg, gla_b_gate, gla_norm_w, conv_w,
                                   gdn_a_log, gdn_dt_bias, gdn_norm_w, ln_w, ln_b, loss_target)

    gr["loss"] = loss
    packed = _pack_small(gr)
    prow = packed.shape[0]
    gathered = _all_gather8(packed, "gather_small").reshape(ndev, prow, 128)
    small = _unpack_small(_sum_leading(gathered, "sum_small").reshape(-1))
    loss = small["loss"][0]
    mod_rows = gathered.reshape(ndev, prow * 128)[:, sum(sz for _, sz in _SMALL[:-1]):][:, :bl * 3 * D_MODEL]
    dmod_all = mod_rows.reshape(ndev * bl, 3 * D_MODEL)
    dmod_blk = lax.dynamic_slice_in_dim(dmod_all, chip * ada_cols, ada_cols, axis=1)
    g_w_ada, g_b_ada = _ada_grads(c_all, dmod_all, dmod_blk)
    wg_cols = gla_w_gate_up.shape[2]
    g_wg = lax.dynamic_slice_in_dim(small["gla_w_gate_up"].reshape(GLA_RANK, GLA_QK), chip * wg_cols, wg_cols, axis=1)
    cw_cols = gdn_conv_w.shape[2]
    g_cw = lax.dynamic_slice_in_dim(small["gdn_conv_w"].reshape(CONV_K, 3 * GDN_WIDTH), chip * cw_cols, cw_cols, axis=1)

    in_feats = w_in.shape[2]
    out_rows = w_out.shape[1]
    p_in = gr["w_in"]
    p_out = gr["w_out"].reshape(4, out_rows, D_MODEL)
    h_in, h_out = D_MODEL // 2, out_rows // 2
    mine_in = lax.dynamic_slice_in_dim(p_in, ic * h_in, h_in, axis=1)
    mine_out = lax.dynamic_slice_in_dim(p_out, ic * h_out, h_out, axis=1)
    theirs_in = lax.dynamic_slice_in_dim(p_in, (1 - ic) * h_in, h_in, axis=1)
    theirs_out = lax.dynamic_slice_in_dim(p_out, (1 - ic) * h_out, h_out, axis=1)
    got_in, got_out = _sibling_swap([theirs_in, theirs_out], "swap_halves")
    chip_in, chip_in16 = _chip_sum_blocks(mine_in, got_in, in_feats, 4, "chip_sum_in")
    chip_out, chip_out16 = _add_n([mine_out.reshape(4 * h_out, D_MODEL), got_out.reshape(4 * h_out, D_MODEL)],
                                  "chip_sum_out", (F32, BF16))
    chip_out = chip_out.reshape(4, h_out, D_MODEL)
    rs_in, rs_out = _chip_scatter([chip_in16, chip_out16.reshape(4, h_out, D_MODEL)], "scatter_grads")
    own_in = lax.dynamic_index_in_dim(chip_in, chip, axis=0, keepdims=False)
    own_out = lax.dynamic_index_in_dim(chip_out, chip, axis=0, keepdims=False)
    (half_in,) = _add_n([own_in, rs_in[0], rs_in[1], rs_in[2]], "reduce_in")
    (half_out,) = _add_n([own_out, rs_out[0], rs_out[1], rs_out[2]], "reduce_out")
    sib_in, sib_out = _sibling_swap([half_in, half_out], "swap_result")
    g_w_in_t = jnp.where(ic == 0, jnp.concatenate([half_in, sib_in], axis=1),
                         jnp.concatenate([sib_in, half_in], axis=1))[0:in_feats]
    g_w_out = jnp.where(ic == 0, jnp.concatenate([half_out, sib_out], axis=0),
                        jnp.concatenate([sib_out, half_out], axis=0))

    grads = dict(
        w_ada=g_w_ada[None], b_ada=g_b_ada, w_in=g_w_in_t, gla_w_gate_up=g_wg[None],
        gla_b_gate=small["gla_b_gate"].reshape(1, -1), gla_norm_w=small["gla_norm_w"].reshape(1, -1),
        gdn_conv_w=g_cw[None], gdn_a_log=small["gdn_a_log"].reshape(1, -1),
        gdn_dt_bias=small["gdn_dt_bias"].reshape(1, -1), gdn_norm_w=small["gdn_norm_w"].reshape(1, -1),
        w_out=g_w_out[None], ln_w=small["ln_w"].reshape(1, -1), ln_b=small["ln_b"].reshape(1, -1))
    weights = dict(w_ada=w_ada, b_ada=b_ada, w_in=w_in, gla_w_gate_up=gla_w_gate_up, gla_b_gate=gla_b_gate,
                   gla_norm_w=gla_norm_w, gdn_conv_w=gdn_conv_w, gdn_a_log=gdn_a_log, gdn_dt_bias=gdn_dt_bias,
                   gdn_norm_w=gdn_norm_w, w_out=w_out, ln_w=ln_w, ln_b=ln_b)
    m_in = dict(w_ada=m_w_ada, b_ada=m_b_ada, w_in=m_w_in, gla_w_gate_up=m_gla_w_gate_up, gla_b_gate=m_gla_b_gate,
                gla_norm_w=m_gla_norm_w, gdn_conv_w=m_gdn_conv_w, gdn_a_log=m_gdn_a_log, gdn_dt_bias=m_gdn_dt_bias,
                gdn_norm_w=m_gdn_norm_w, w_out=m_w_out, ln_w=m_ln_w, ln_b=m_ln_b)
    v_in = dict(w_ada=v_w_ada, b_ada=v_b_ada, w_in=v_w_in, gla_w_gate_up=v_gla_w_gate_up, gla_b_gate=v_gla_b_gate,
                gla_norm_w=v_gla_norm_w, gdn_conv_w=v_gdn_conv_w, gdn_a_log=v_gdn_a_log, gdn_dt_bias=v_gdn_dt_bias,
                gdn_norm_w=v_gdn_norm_w, w_out=v_w_out, ln_w=v_ln_w, ln_b=v_ln_b)
    names = list(weights)
    delta, new_m, new_v = {}, {}, {}
    for nm in names:
        shp = weights[nm].shape
        if nm == "w_in":
            to2d = lambda t: jnp.transpose(t[0])
            from2d = lambda t: jnp.transpose(t)[None]
            g2d = grads[nm]
        else:
            to2d = lambda t: t.reshape(-1, shp[-1])
            from2d = lambda t: t.reshape(shp)
            g2d = to2d(grads[nm])
        d, a, b = _adamw(to2d(weights[nm]), g2d, to2d(m_in[nm]), to2d(v_in[nm]), "adamw_" + nm)
        delta[nm], new_m[nm], new_v[nm] = from2d(d), from2d(a), from2d(b)
        grads[nm] = from2d(g2d)
    return (loss, grad_x, *[grads[k] for k in names], *[delta[k] for k in names],
            *[new_m[k] for k in names], *[new_v[k] for k in names])
```

```python
import functools

import jax
import jax.numpy as jnp
from jax import lax
from jax.experimental import pallas as pl
from jax.experimental.pallas import tpu as pltpu

F32 = jnp.float32
BF16 = jnp.bfloat16
HI = lax.Precision.HIGH
INV_PREC = None
MESH = pl.DeviceIdType.MESH

D_MODEL = 1024
GLA_HEADS = 4
GLA_DK = 64
GLA_DV = 128
GLA_QK = 256
GLA_WIDTH = 512
GLA_RANK = 16
GLA_GATE_NORM = 16.0
GDN_HEADS = 4
GDN_DK = 128
GDN_WIDTH = 512
CONV_K = 4
CHUNK = 64
LN_EPS = 1e-5
RMS_EPS = 1e-6
ALPHA = 2.0 ** 0.25
IN_COLS = 3608

LANE_A = GLA_RANK
LANE_B = GLA_RANK + GDN_HEADS
SMALL_USED = GLA_RANK + 2 * GDN_HEADS

ADAM_LR = 0.001
ADAM_B1 = 0.9
ADAM_B2 = 0.999
ADAM_EPS = 1e-08
ADAM_WD = 0.01
ADAM_STEP = 10

VMEM_LIMIT = 56 * 1024 * 1024


def _iota(shape, dim):
    return lax.broadcasted_iota(jnp.int32, shape, dim)


def _dot(a, b, prec=None):
    return lax.dot_general(a, b, (((1,), (0,)), ((), ())), precision=prec, preferred_element_type=F32)


def _dot_nt(a, b, prec=None):
    return lax.dot_general(a, b, (((1,), (1,)), ((), ())), precision=prec, preferred_element_type=F32)


def _dot_tn(a, b, prec=None):
    return lax.dot_general(a, b, (((0,), (0,)), ((), ())), precision=prec, preferred_element_type=F32)


def _log_sigmoid(z):
    return jnp.minimum(z, 0.0) - jnp.log1p(jnp.exp(-jnp.abs(z)))


def _softplus(z):
    return jnp.maximum(z, 0.0) + jnp.log1p(jnp.exp(-jnp.abs(z)))


def _silu(z):
    return z * jax.nn.sigmoid(z)


def _rms_gate(o, nw, og):
    return o * lax.rsqrt(jnp.mean(o * o, axis=-1, keepdims=True) + RMS_EPS) * nw * _silu(og)


def _params(*sem):
    return pltpu.CompilerParams(dimension_semantics=sem, vmem_limit_bytes=VMEM_LIMIT)


GLA_PAIRS = GLA_HEADS // 2


def _gla_chunk(qs, ks, lrs, vs, ogs, ss, wgs, bgs, nw):
    c = qs[0].shape[0]
    pair_units = [divmod(i, GLA_PAIRS) for i in range(len(qs))]
    head_units = [(i // GLA_HEADS, i // GLA_HEADS * GLA_PAIRS + (i % GLA_HEADS) // 2, i % 2) for i in range(len(vs))]
    row, col = _iota((c, c), 0), _iota((c, c), 1)
    causal = row >= col
    first_half = (_iota((c, 1), 0) < c // 2).astype(F32)
    lane = _iota((1, 128), 1)
    masks = [(lane < GLA_DK).astype(F32), (lane >= GLA_DK).astype(F32)]
    gs = [_log_sigmoid(_dot(lrs[e], wgs[p]) + bgs[p]) * (1.0 / GLA_GATE_NORM) for e, p in pair_units]
    bs = [_dot(causal.astype(F32), g, HI) for g in gs]
    b_ref = [jnp.sum(g * first_half, axis=0, keepdims=True) for g in gs]
    b_last = [jnp.sum(g, axis=0, keepdims=True) for g in gs]
    qsc = [q * (GLA_DK ** -0.5) for q in qs]
    qe = [q * jnp.exp(b - br) for q, b, br in zip(qsc, bs, b_ref)]
    ke = [k * jnp.exp(br - b) for k, b, br in zip(ks, bs, b_ref)]
    qb = [q * jnp.exp(b) for q, b in zip(qsc, bs)]
    kd = [k * jnp.exp(bl_ - b) for k, b, bl_ in zip(ks, bs, b_last)]
    att = [jnp.where(causal, _dot_nt(qe[u] * masks[half], ke[u]), 0.0) for _, u, half in head_units]
    o_inter = [_dot_nt(qb[u] * masks[half], ss[u]) for _, u, half in head_units]
    os_ = [_dot(a, v) + oi for a, v, oi in zip(att, vs, o_inter)]
    upd = [_dot_tn(v, kd[u] * masks[half]) for (_, u, half), v in zip(head_units, vs)]
    s_new = [s * jnp.exp(bl_) + upd[2 * u] + upd[2 * u + 1] for u, (s, bl_) in enumerate(zip(ss, b_last))]
    ys = [_rms_gate(o, nw, og) for o, og in zip(os_, ogs)]
    return ys, s_new


def _unit_lower_inverse_chain(a_list):
    c = a_list[0].shape[0]
    eye = (_iota((c, c), 0) == _iota((c, c), 1)).astype(F32)
    ps = [-a for a in a_list]
    ts = [eye + p for p in ps]
    levels = max(c.bit_length() - 2, 0)
    if levels:
        ps = [_dot(p, p, INV_PREC) for p in ps]
    for level in range(levels):
        last = level == levels - 1
        both = [_dot(t if last else jnp.concatenate([t, p], axis=0), p, INV_PREC) for t, p in zip(ts, ps)]
        ts = [t + m[0:c] for t, m in zip(ts, both)]
        if not last:
            ps = [m[c:2 * c] for m in both]
    return ts


@jax.custom_vjp
def _unit_lower_inverse(a_list):
    return _unit_lower_inverse_chain(a_list)


def _unit_lower_inverse_fwd(a_list):
    ts = _unit_lower_inverse_chain(a_list)
    return ts, ts


def _unit_lower_inverse_bwd(ts, dts):
    xs = [_dot_nt(dt, t, INV_PREC) for dt, t in zip(dts, ts)]
    return ([-_dot_tn(t, x, INV_PREC) for t, x in zip(ts, xs)],)


_unit_lower_inverse.defvjp(_unit_lower_inverse_fwd, _unit_lower_inverse_bwd)


@jax.custom_vjp
def _unit_lower_inverse_known(a_list, ts):
    return ts


def _unit_lower_inverse_known_fwd(a_list, ts):
    return ts, ts


def _unit_lower_inverse_known_bwd(ts, dts):
    return _unit_lower_inverse_bwd(ts, dts) + ([jnp.zeros_like(t) for t in ts],)


_unit_lower_inverse_known.defvjp(_unit_lower_inverse_known_fwd, _unit_lower_inverse_known_bwd)


def _gdn_prep_units(qs, ks, vs, gbs, t_known=None):
    c = qs[0].shape[0]
    units = [divmod(i, GDN_HEADS) for i in range(len(qs))]
    row, col = _iota((c, c), 0), _iota((c, c), 1)
    causal, strict = row >= col, row > col
    lane = _iota((1, 128), 1)
    d_alls = [_dot(causal.astype(F32), gb, HI) for gb in gbs]
    g_c, beta_c, d_c = [], [], []
    for r, h in units:
        sel_a = (lane == LANE_A + h).astype(F32)
        g_c.append(jnp.sum(gbs[r] * sel_a, axis=-1, keepdims=True))
        beta_c.append(jnp.sum(gbs[r] * (lane == LANE_B + h).astype(F32), axis=-1, keepdims=True))
        d_c.append(jnp.sum(d_alls[r] * sel_a, axis=-1, keepdims=True))
    d_last = [jnp.sum(g, axis=0, keepdims=True) for g in g_c]
    d_diff = [jnp.broadcast_to(d, (c, c)) - jnp.broadcast_to(d, (c, c)).T for d in d_c]
    decay_mat = [jnp.where(causal, jnp.exp(jnp.where(causal, dd, 0.0)), 0.0) for dd in d_diff]
    kb = [k * b for k, b in zip(ks, beta_c)]
    kbk_qk = [_dot_nt(jnp.concatenate([kbi, q], axis=0), k) for kbi, q, k in zip(kb, qs, ks)]
    a = [jnp.where(strict, m[0:c] * dm, 0.0) for m, dm in zip(kbk_qk, decay_mat)]
    qk = [jnp.where(causal, m[c:2 * c] * dm, 0.0) for m, dm in zip(kbk_qk, decay_mat)]
    t = _unit_lower_inverse(a) if t_known is None else _unit_lower_inverse_known(a, t_known)
    uw = [_dot(ti, jnp.concatenate([v * b, kbi * jnp.exp(d)], axis=1))
          for ti, v, b, kbi, d in zip(t, vs, beta_c, kb, d_c)]
    u = [m[:, 0:128] for m in uw]
    w = [m[:, 128:256] for m in uw]
    q_dec = [q * jnp.exp(d) for q, d in zip(qs, d_c)]
    k_dec = [k * jnp.exp(dl - d) for k, dl, d in zip(ks, d_last, d_c)]
    gamma = [jnp.exp(dl) for dl in d_last]
    return u, w, qk, q_dec, k_dec, gamma, t


def _sum_all(t):
    return jnp.sum(jnp.sum(t, axis=-1, keepdims=True), axis=0, keepdims=True)


def _gdn_pre_elem(ps, ab, alog_v, dtb_v):
    outs = []
    for j, p in enumerate(ps):
        s = _silu(p)
        if j < 2 * GDN_HEADS:
            s = s * lax.rsqrt(jnp.sum(s * s, axis=-1, keepdims=True) + RMS_EPS)
        if j < GDN_HEADS:
            s = s * (GDN_DK ** -0.5)
        outs.append(s)
    lane = _iota((1, 128), 1)
    is_a = (lane >= LANE_A) & (lane < LANE_A + GDN_HEADS)
    is_b = (lane >= LANE_B) & (lane < LANE_B + GDN_HEADS)
    g = -jnp.exp(alog_v) * _softplus(ab + dtb_v)
    gb = jnp.where(is_a, g, jnp.where(is_b, jax.nn.sigmoid(ab), 0.0))
    return tuple(outs) + (gb,)


def _proj_fwd(x2, sc3, sh3, ws, seq, tm=256):
    n = x2.shape[0]
    tpe = seq // tm
    nw = len(ws)

    def body(x_ref, sc_ref, sh_ref, *refs):
        h = (x_ref[...] * sc_ref[0] + sh_ref[0]).astype(ws[0].dtype)
        for w_ref, o_ref in zip(refs[:nw], refs[nw:]):
            o_ref[...] = _dot_nt(h, w_ref[...])

    row = lambda i: (i, 0)
    per_ex = pl.BlockSpec((1, 1, D_MODEL), lambda i: (i // tpe, 0, 0))
    return pl.pallas_call(
        body, name="proj_fwd", grid=(n // tm,),
        in_specs=[pl.BlockSpec((tm, D_MODEL), row), per_ex, per_ex]
        + [pl.BlockSpec(w.shape, lambda i: (0, 0)) for w in ws],
        out_specs=[pl.BlockSpec((tm, w.shape[0]), row) for w in ws],
        out_shape=[jax.ShapeDtypeStruct((n, w.shape[0]), F32) for w in ws],
        compiler_params=_params("parallel"),
    )(x2, sc3, sh3, *ws)


def _gla_fwd(pa, pd, wg, bg, nw, bl, seq):
    n = pa.shape[0]
    nc = seq // CHUNK
    heads = [(e, h, slice(h * 128, (h + 1) * 128)) for e in range(bl) for h in range(GLA_HEADS)]
    pair_cols = [slice(p * 128, (p + 1) * 128) for p in range(GLA_PAIRS)]
    pairs = [(e, p, pair_cols[p]) for e in range(bl) for p in range(GLA_PAIRS)]

    def body(q_ref, k_ref, v_ref, og_ref, lr_ref, wg_ref, bg_ref, nw_ref, y_ref, st_ref, s_scr):
        @pl.when(pl.program_id(0) == 0)
        def _():
            s_scr[...] = jnp.zeros_like(s_scr)

        ss = [s_scr[e, p] for e, p, _ in pairs]
        for (e, p, _), s in zip(pairs, ss):
            st_ref[e, 0, p] = s
        ys, s_new = _gla_chunk([q_ref[e, :, cols] for e, _, cols in pairs], [k_ref[e, :, cols] for e, _, cols in pairs],
                               [lr_ref[e] for e in range(bl)],
                               [v_ref[e, :, cols] for e, _, cols in heads], [og_ref[e, :, cols] for e, _, cols in heads],
                               ss, [wg_ref[:, cols] for cols in pair_cols], [bg_ref[:, cols] for cols in pair_cols],
                               nw_ref[...])
        for (e, h, cols), y in zip(heads, ys):
            y_ref[e, :, cols] = y.astype(y_ref.dtype)
        for (e, p, _), s in zip(pairs, s_new):
            s_scr[e, p] = s

    tok = lambda w, j: pl.BlockSpec((bl, CHUNK, w), lambda i: (0, i, j))
    const = lambda i: (0, 0)
    pa3 = pa.reshape(bl, seq, 1536)
    y, st = pl.pallas_call(
        body, name="gla_fwd", grid=(nc,),
        in_specs=[tok(256, 0), tok(256, 1), tok(512, 1), tok(512, 2), tok(128, 0),
                  pl.BlockSpec(wg.shape, const), pl.BlockSpec(bg.shape, const), pl.BlockSpec(nw.shape, const)],
        out_specs=[tok(512, 0), pl.BlockSpec((bl, 1, GLA_PAIRS, 128, 128), lambda i: (0, i, 0, 0, 0))],
        out_shape=[jax.ShapeDtypeStruct((bl, seq, 512), MM_DTYPE),
                   jax.ShapeDtypeStruct((bl, nc, GLA_PAIRS, 128, 128), F32)],
        scratch_shapes=[pltpu.VMEM((bl, GLA_PAIRS, 128, 128), F32)],
        compiler_params=_params("arbitrary"),
    )(pa3, pa3, pa3, pa3, pd.reshape(bl, seq, 128), wg, bg, nw)
    return y.reshape(n, 512), st


def _gla_bwd(pa, pd, st, dya, wg, bg, nw, bl, seq):
    n = pa.shape[0]
    nc = seq // CHUNK
    heads = [(e, h, slice(h * 128, (h + 1) * 128)) for e in range(bl) for h in range(GLA_HEADS)]
    pair_cols = [slice(p * 128, (p + 1) * 128) for p in range(GLA_PAIRS)]
    pairs = [(e, p, pair_cols[p]) for e in range(bl) for p in range(GLA_PAIRS)]

    def body(q_ref, k_ref, v_ref, og_ref, lr_ref, st_ref, dy_ref, wg_ref, bg_ref, nw_ref,
             da_ref, dd_ref, dwg_ref, dbg_ref, dnw_ref, ds_scr):
        @pl.when(pl.program_id(0) == 0)
        def _():
            dwg_ref[...] = jnp.zeros_like(dwg_ref)
            dbg_ref[...] = jnp.zeros_like(dbg_ref)
            dnw_ref[...] = jnp.zeros_like(dnw_ref)
            ds_scr[...] = jnp.zeros_like(ds_scr)

        _, vjp = jax.vjp(_gla_chunk, [q_ref[e, :, cols] for e, _, cols in pairs],
                         [k_ref[e, :, cols] for e, _, cols in pairs], [lr_ref[e] for e in range(bl)],
                         [v_ref[e, :, cols] for e, _, cols in heads], [og_ref[e, :, cols] for e, _, cols in heads],
                         [st_ref[e, 0, p] for e, p, _ in pairs],
                         [wg_ref[:, cols] for cols in pair_cols], [bg_ref[:, cols] for cols in pair_cols], nw_ref[...])
        dq, dk, dlr, dv, dog, ds, dwg, dbg, dnw = vjp(([dy_ref[e, :, cols] for e, _, cols in heads],
                                                         [ds_scr[e, p] for e, p, _ in pairs]))
        for e in range(bl):
            dd_ref[e] = dlr[e]
        for i, (e, p, cols) in enumerate(pairs):
            da_ref[e, :, cols] = dq[i].astype(da_ref.dtype)
            da_ref[e, :, GLA_QK + p * 128:GLA_QK + (p + 1) * 128] = dk[i].astype(da_ref.dtype)
            ds_scr[e, p] = ds[i]
        for i, (e, h, _) in enumerate(heads):
            da_ref[e, :, 512 + h * 128:512 + (h + 1) * 128] = dv[i].astype(da_ref.dtype)
            da_ref[e, :, 1024 + h * 128:1024 + (h + 1) * 128] = dog[i].astype(da_ref.dtype)
        for p, cols in enumerate(pair_cols):
            dwg_ref[:, cols] += dwg[p]
            dbg_ref[:, cols] += dbg[p]
        dnw_ref[...] += dnw

    tok = lambda w, j: pl.BlockSpec((bl, CHUNK, w), lambda i: (0, nc - 1 - i, j))
    const = lambda i: (0, 0)
    pa3 = pa.reshape(bl, seq, 1536)
    da, dd, dwg, dbg, dnw = pl.pallas_call(
        body, name="gla_bwd", grid=(nc,),
        in_specs=[tok(256, 0), tok(256, 1), tok(512, 1), tok(512, 2), tok(128, 0),
                  pl.BlockSpec((bl, 1, GLA_PAIRS, 128, 128), lambda i: (0, nc - 1 - i, 0, 0, 0)), tok(512, 0),
                  pl.BlockSpec(wg.shape, const), pl.BlockSpec(bg.shape, const), pl.BlockSpec(nw.shape, const)],
        out_specs=[tok(1536, 0), tok(128, 0),
                   pl.BlockSpec(wg.shape, const), pl.BlockSpec(bg.shape, const), pl.BlockSpec(nw.shape, const)],
        out_shape=[jax.ShapeDtypeStruct((bl, seq, 1536), MM_DTYPE), jax.ShapeDtypeStruct((bl, seq, 128), F32),
                   jax.ShapeDtypeStruct(wg.shape, F32), jax.ShapeDtypeStruct(bg.shape, F32),
                   jax.ShapeDtypeStruct(nw.shape, F32)],
        scratch_shapes=[pltpu.VMEM((bl, GLA_PAIRS, 128, 128), F32)],
        compiler_params=_params("arbitrary"),
    )(pa3, pa3, pa3, pa3, pd.reshape(bl, seq, 128), st, dya.reshape(bl, seq, 512), wg, bg, nw)
    return da.reshape(n, 1536), dd.reshape(n, 128), dwg, dbg, dnw


def _conv_taps(buf_ref, w_ref, base, rows):
    acc = w_ref[0:1, :] * buf_ref[pl.ds(base, rows), :]
    for k in range(1, CONV_K):
        acc = acc + w_ref[k:k + 1, :] * buf_ref[pl.ds(base + k, rows), :]
    return acc


def _gdn_pre_fwd(pb, pd, cw8, alog_v, dtb_v, bl, seq, tm=256):
    n = pb.shape[0]
    tpe = seq // tm
    t8 = tm // 8

    def body(u_ref, prev_ref, ab_ref, w_ref, al_ref, dt_ref, qkv_ref, gb_ref, p_ref, buf):
        i = pl.program_id(0)
        keep = (i % tpe != 0).astype(F32)
        buf[0:8, :] = prev_ref[...] * keep
        buf[8:8 + tm, :] = u_ref[...]
        p = _conv_taps(buf, w_ref, 8 - (CONV_K - 1), tm)
        p_ref[...] = p
        ps = [p[:, j * 128:(j + 1) * 128] for j in range(12)]
        outs = _gdn_pre_elem(ps, ab_ref[...], al_ref[...], dt_ref[...])
        for j in range(12):
            qkv_ref[:, j * 128:(j + 1) * 128] = outs[j]
        gb_ref[...] = outs[12]

    row = lambda i: (i, 0)
    const = lambda i: (0, 0)
    return pl.pallas_call(
        body, name="gdn_pre_fwd", grid=(n // tm,),
        in_specs=[pl.BlockSpec((tm, 1536), row),
                  pl.BlockSpec((8, 1536), lambda i: (jnp.maximum(i * t8 - 1, 0), 0)),
                  pl.BlockSpec((tm, 128), row),
                  pl.BlockSpec((8, 1536), const), pl.BlockSpec((1, 128), const), pl.BlockSpec((1, 128), const)],
        out_specs=[pl.BlockSpec((tm, 1536), row), pl.BlockSpec((tm, 128), row), pl.BlockSpec((tm, 1536), row)],
        out_shape=[jax.ShapeDtypeStruct((n, 1536), F32), jax.ShapeDtypeStruct((n, 128), F32),
                   jax.ShapeDtypeStruct((n, 1536), F32)],
        scratch_shapes=[pltpu.VMEM((tm + 8, 1536), F32)],
        compiler_params=_params("parallel"),
    )(pb, pb, pd, cw8, alog_v, dtb_v)


def _gdn_pre_bwd(pb, conv_out, pd, dqkv, dgb, cw8, alog_v, dtb_v, bl, seq, tm=256):
    n = pb.shape[0]
    tpe = seq // tm
    t8 = tm // 8
    nb8 = n // 8
    ext = tm + 8

    def body(u_ref, p_ref, pn_ref, ab_ref, abn_ref, dq_ref, dqn_ref, dgb_ref, w_ref, al_ref, dt_ref,
             du_ref, dab_ref, dw_ref, dal_ref, ddt_ref, dpbuf):
        i = pl.program_id(0)

        @pl.when(i == 0)
        def _():
            dw_ref[...] = jnp.zeros_like(dw_ref)
            dal_ref[...] = jnp.zeros_like(dal_ref)
            ddt_ref[...] = jnp.zeros_like(ddt_ref)

        keep_next = (i % tpe != tpe - 1).astype(F32)
        ps = [jnp.concatenate([p_ref[:, j * 128:(j + 1) * 128], pn_ref[:, j * 128:(j + 1) * 128]], axis=0)
              for j in range(12)]
        ab = jnp.concatenate([ab_ref[...], abn_ref[...]], axis=0)
        _, vjp = jax.vjp(_gdn_pre_elem, ps, ab, al_ref[...], dt_ref[...])
        zeros8 = jnp.zeros((8, 128), F32)
        cts = tuple(jnp.concatenate([dq_ref[:, j * 128:(j + 1) * 128],
                                     dqn_ref[:, j * 128:(j + 1) * 128] * keep_next], axis=0) for j in range(12))
        cts += (jnp.concatenate([dgb_ref[...], zeros8], axis=0),)
        dps, dab, dal, ddt = vjp(cts)
        for j in range(12):
            dpbuf[:, j * 128:(j + 1) * 128] = dps[j]
        dab_ref[...] = dab[0:tm, :]
        dal_ref[...] += dal
        ddt_ref[...] += ddt
        u = u_ref[...]
        du = None
        for k in range(CONV_K):
            dp_k = dpbuf[pl.ds(CONV_K - 1 - k, tm), :]
            term = w_ref[k:k + 1, :] * dp_k
            du = term if du is None else du + term
            dw_ref[k:k + 1, :] += jnp.sum(u * dp_k, axis=0, keepdims=True)
        du_ref[...] = du.astype(du_ref.dtype)

    row = lambda i: (i, 0)
    next8 = lambda i: (jnp.minimum((i + 1) * t8, nb8 - 1), 0)
    const = lambda i: (0, 0)
    return pl.pallas_call(
        body, name="gdn_pre_bwd", grid=(n // tm,),
        in_specs=[pl.BlockSpec((tm, 1536), row), pl.BlockSpec((tm, 1536), row), pl.BlockSpec((8, 1536), next8),
                  pl.BlockSpec((tm, 128), row), pl.BlockSpec((8, 128), next8),
                  pl.BlockSpec((tm, 1536), row), pl.BlockSpec((8, 1536), next8),
                  pl.BlockSpec((tm, 128), row),
                  pl.BlockSpec((8, 1536), const), pl.BlockSpec((1, 128), const), pl.BlockSpec((1, 128), const)],
        out_specs=[pl.BlockSpec((tm, 1536), row), pl.BlockSpec((tm, 128), row),
                   pl.BlockSpec((8, 1536), const), pl.BlockSpec((1, 128), const), pl.BlockSpec((1, 128), const)],
        out_shape=[jax.ShapeDtypeStruct((n, 1536), MM_DTYPE), jax.ShapeDtypeStruct((n, 128), F32),
                   jax.ShapeDtypeStruct((8, 1536), F32), jax.ShapeDtypeStruct((1, 128), F32),
                   jax.ShapeDtypeStruct((1, 128), F32)],
        scratch_shapes=[pltpu.VMEM((ext, 1536), F32)],
        compiler_params=_params("arbitrary"),
    )(pb, conv_out, conv_out, pd, pd, dqkv, dqkv, dgb, cw8, alog_v, dtb_v)


GDN_PREP_CHUNKS = 2
GDN_SCAN_CHUNKS = 2
MM_DTYPE = BF16


def _head_cols(ref, rows, base=0):
    return [ref[rows, base + h * 128:base + (h + 1) * 128] for h in range(GDN_HEADS)]


def _gdn_prep(qkv, gb):
    n = qkv.shape[0]
    r_per = GDN_PREP_CHUNKS
    tm = r_per * CHUNK

    def body(q_ref, k_ref, v_ref, gb_ref, u_ref, w_ref, qd_ref, kd_ref, qk_ref, t_ref, gam_ref):
        rowid = _iota((8, 128), 0)
        chunk_rows = [slice(r * CHUNK, (r + 1) * CHUNK) for r in range(r_per)]
        gather = lambda ref: [t for rows in chunk_rows for t in _head_cols(ref, rows)]
        u, w, qk, qd, kd, gamma, tinv = _gdn_prep_units(gather(q_ref), gather(k_ref), gather(v_ref),
                                                        [gb_ref[rows, :] for rows in chunk_rows])
        for r, rows in enumerate(chunk_rows):
            gam = jnp.zeros((8, 128), F32)
            for h in range(GDN_HEADS):
                i = r * GDN_HEADS + h
                cols = slice(h * 128, (h + 1) * 128)
                u_ref[rows, cols] = u[i]
                w_ref[rows, cols] = w[i].astype(MM_DTYPE)
                qd_ref[rows, cols] = qd[i].astype(MM_DTYPE)
                kd_ref[rows, cols] = kd[i].astype(MM_DTYPE)
                qk_ref[r, h] = qk[i].astype(MM_DTYPE)
                t_ref[r, h] = tinv[i].astype(MM_DTYPE)
                gam = jnp.where(rowid == h, gamma[i], gam)
            gam_ref[r] = gam

    tok = lambda j: pl.BlockSpec((tm, 512), lambda i: (i, j))
    return pl.pallas_call(
        body, name="gdn_prep", grid=(n // tm,),
        in_specs=[tok(0), tok(1), tok(2), pl.BlockSpec((tm, 128), lambda i: (i, 0))],
        out_specs=[tok(0)] * 4 + [pl.BlockSpec((r_per, GDN_HEADS, CHUNK, CHUNK), lambda i: (i, 0, 0, 0))] * 2
        + [pl.BlockSpec((r_per, 8, 128), lambda i: (i, 0, 0))],
        out_shape=[jax.ShapeDtypeStruct((n, 512), F32)] + [jax.ShapeDtypeStruct((n, 512), MM_DTYPE)] * 3
        + [jax.ShapeDtypeStruct((n // CHUNK, GDN_HEADS, CHUNK, CHUNK), MM_DTYPE)] * 2
        + [jax.ShapeDtypeStruct((n // CHUNK, 8, 128), F32)],
        compiler_params=_params("parallel"),
    )(qkv, qkv, qkv, gb)


def _gdn_fwd(qkv, gb, pc, nw, bl, seq):
    n = qkv.shape[0]
    nc = seq // CHUNK
    u, w, qd, kd, qk, tinv, gam = _gdn_prep(qkv, gb)
    tok3 = lambda t: t.reshape(bl, seq, 512)
    qk5 = qk.reshape(bl, nc, GDN_HEADS, CHUNK, CHUNK)
    gam4 = gam.reshape(bl, nc, 8, 128)

    r_per = GDN_SCAN_CHUNKS
    mm = lambda t: t.astype(MM_DTYPE)

    def body(u_ref, w_ref, qd_ref, kd_ref, qk_ref, gam_ref, og_ref, nw_ref, o_ref, y_ref, vn_ref, st_ref, s_scr):
        @pl.when(pl.program_id(0) == 0)
        def _():
            s_scr[...] = jnp.zeros_like(s_scr)

        units = [(b, h, slice(h * 128, (h + 1) * 128)) for b in range(bl) for h in range(GDN_HEADS)]
        ss = [s_scr[b, h] for b, h, _ in units]
        for r in range(r_per):
            rows = slice(r * CHUNK, (r + 1) * CHUNK)
            for (b, h, _), s in zip(units, ss):
                st_ref[b, r, h] = s
            ws_qs = [_dot(jnp.concatenate([w_ref[b, rows, cols], qd_ref[b, rows, cols]], axis=0), mm(s))
                     for (b, h, cols), s in zip(units, ss)]
            v_new = [u_ref[b, rows, cols] - m[0:CHUNK] for (b, h, cols), m in zip(units, ws_qs)]
            os_ = [m[CHUNK:2 * CHUNK] + _dot(qk_ref[b, r, h], mm(vn))
                   for (b, h, cols), m, vn in zip(units, ws_qs, v_new)]
            ss = [s * gam_ref[b, r, h:h + 1, :] + _dot_tn(kd_ref[b, rows, cols], mm(vn))
                  for (b, h, cols), s, vn in zip(units, ss, v_new)]
            for (b, h, cols), vn, o in zip(units, v_new, os_):
                vn_ref[b, rows, cols] = mm(vn)
                o_ref[b, rows, cols] = o
                y_ref[b, rows, cols] = mm(_rms_gate(o, nw_ref[...], og_ref[b, rows, cols]))
        for (b, h, _), s in zip(units, ss):
            s_scr[b, h] = s

    tok = pl.BlockSpec((bl, r_per * CHUNK, 512), lambda i: (0, i, 0))
    st_spec = pl.BlockSpec((bl, r_per, GDN_HEADS, 128, 128), lambda i: (0, i, 0, 0, 0))
    tok_shape = jax.ShapeDtypeStruct((bl, seq, 512), F32)
    o, y, vn, st = pl.pallas_call(
        body, name="gdn_scan_fwd", grid=(nc // r_per,),
        in_specs=[tok, tok, tok, tok,
                  pl.BlockSpec((bl, r_per, GDN_HEADS, CHUNK, CHUNK), lambda i: (0, i, 0, 0, 0)),
                  pl.BlockSpec((bl, r_per, 8, 128), lambda i: (0, i, 0, 0)), tok,
                  pl.BlockSpec(nw.shape, lambda i: (0, 0))],
        out_specs=[tok, tok, tok, st_spec],
        out_shape=[tok_shape, jax.ShapeDtypeStruct((bl, seq, 512), MM_DTYPE), jax.ShapeDtypeStruct((bl, seq, 512), MM_DTYPE),
                   jax.ShapeDtypeStruct((bl, nc, GDN_HEADS, 128, 128), F32)],
        scratch_shapes=[pltpu.VMEM((bl, GDN_HEADS, 128, 128), F32)],
        compiler_params=_params("arbitrary"),
    )(tok3(u), tok3(w), tok3(qd), tok3(kd), qk5, gam4, tok3(pc), nw)
    return y.reshape(n, 512), (o, st, w, qd, kd, qk5, gam4, tinv, vn)


def _gdn_bwd(qkv, gb, pc, res, dyb, nw, bl, seq):
    n = qkv.shape[0]
    nc = seq // CHUNK
    o, st, w, qd, kd, qk5, gam4, tinv, vn = res
    tok3 = lambda t: t.reshape(bl, seq, 512)

    def scan_body(dy_ref, o_ref, og_ref, w_ref, qd_ref, kd_ref, qk_ref, gam_ref, nw_ref,
                  do_ref, dog_ref, dvn_ref, dst_ref, dnw_ref, ds_scr):
        @pl.when(pl.program_id(0) == 0)
        def _():
            ds_scr[...] = jnp.zeros_like(ds_scr)
            dnw_ref[...] = jnp.zeros_like(dnw_ref)

        units = [(b, h, slice(h * 128, (h + 1) * 128)) for b in range(bl) for h in range(GDN_HEADS)]
        dnw = jnp.zeros(nw.shape, F32)
        dss = [ds_scr[b, h] for b, h, _ in units]
        for r in reversed(range(r_scan)):
            rows = slice(r * CHUNK, (r + 1) * CHUNK)
            d_os = []
            for b, h, cols in units:
                _, vjp = jax.vjp(_rms_gate, o_ref[b, rows, cols], nw_ref[...], og_ref[b, rows, cols])
                d_o, dnw_h, dog = vjp(dy_ref[b, rows, cols])
                do_ref[b, rows, cols] = mm(d_o)
                dog_ref[b, rows, cols] = mm(dog)
                dnw = dnw + dnw_h
                d_os.append(mm(d_o))
            for (b, h, _), ds in zip(units, dss):
                dst_ref[b, r, h] = ds
            dvn_a = [_dot(kd_ref[b, rows, cols], mm(ds)) for (b, h, cols), ds in zip(units, dss)]
            dvns = [a + _dot_tn(qk_ref[b, r, h], d_o) for (b, h, cols), a, d_o in zip(units, dvn_a, d_os)]
            for (b, h, cols), dvn in zip(units, dvns):
                dvn_ref[b, rows, cols] = mm(dvn)
            dss = [ds * gam_ref[b, r, h:h + 1, :] + _dot_tn(
                jnp.concatenate([qd_ref[b, rows, cols], w_ref[b, rows, cols]], axis=0),
                jnp.concatenate([d_o, mm(-dvn)], axis=0))
                for (b, h, cols), d_o, ds, dvn in zip(units, d_os, dss, dvns)]
        dnw_ref[...] += dnw
        for (b, h, _), ds in zip(units, dss):
            ds_scr[b, h] = ds

    r_scan = GDN_SCAN_CHUNKS
    mm = lambda t: t.astype(MM_DTYPE)
    rev = lambda i: nc // r_scan - 1 - i
    tok = pl.BlockSpec((bl, r_scan * CHUNK, 512), lambda i: (0, rev(i), 0))
    st_spec = pl.BlockSpec((bl, r_scan, GDN_HEADS, 128, 128), lambda i: (0, rev(i), 0, 0, 0))
    tok_shape = jax.ShapeDtypeStruct((bl, seq, 512), F32)
    tok_mm = jax.ShapeDtypeStruct((bl, seq, 512), MM_DTYPE)
    d_o, dog, dvn, dst, dnw = pl.pallas_call(
        scan_body, name="gdn_scan_bwd", grid=(nc // r_scan,),
        in_specs=[tok] * 6 + [pl.BlockSpec((bl, r_scan, GDN_HEADS, CHUNK, CHUNK), lambda i: (0, rev(i), 0, 0, 0)),
                              pl.BlockSpec((bl, r_scan, 8, 128), lambda i: (0, rev(i), 0, 0)),
                              pl.BlockSpec(nw.shape, lambda i: (0, 0))],
        out_specs=[tok, tok, tok, st_spec, pl.BlockSpec(nw.shape, lambda i: (0, 0))],
        out_shape=[tok_mm, tok_mm, tok_mm, jax.ShapeDtypeStruct(st.shape, F32),
                   jax.ShapeDtypeStruct(nw.shape, F32)],
        scratch_shapes=[pltpu.VMEM((bl, GDN_HEADS, 128, 128), F32)],
        compiler_params=_params("arbitrary"),
    )(tok3(dyb), o, tok3(pc), tok3(w), tok3(qd), tok3(kd), qk5, gam4, nw)

    r_per = GDN_PREP_CHUNKS
    tm = r_per * CHUNK

    def prep_body(q_ref, k_ref, v_ref, gb_ref, t_ref, st_ref, dst_ref, dvn_ref, do_ref, vn_ref, dqkv_ref, dgb_ref):
        chunk_rows = [slice(r * CHUNK, (r + 1) * CHUNK) for r in range(r_per)]
        gather = lambda ref: [t for rows in chunk_rows for t in _head_cols(ref, rows)]
        units = [(r, h) for r in range(r_per) for h in range(GDN_HEADS)]
        t_known = [t_ref[r, h].astype(F32) for r, h in units]
        prep = lambda q, k, v, g: _gdn_prep_units(q, k, v, g, t_known)[:6]
        _, vjp = jax.vjp(prep, gather(q_ref), gather(k_ref), gather(v_ref), [gb_ref[rows, :] for rows in chunk_rows])
        ss = [st_ref[r, h] for r, h in units]
        dss = [dst_ref[r, h] for r, h in units]
        dvns, d_os, v_new = gather(dvn_ref), gather(do_ref), gather(vn_ref)
        both = [_dot_nt(jnp.concatenate([dvn, d_o], axis=0), s.astype(MM_DTYPE)) for dvn, d_o, s in zip(dvns, d_os, ss)]
        d_w = [-m[0:CHUNK] for m in both]
        d_qd = [m[CHUNK:2 * CHUNK] for m in both]
        d_qk = [_dot_nt(d_o, vn) for d_o, vn in zip(d_os, v_new)]
        d_kd = [_dot_nt(vn, ds.astype(MM_DTYPE)) for vn, ds in zip(v_new, dss)]
        d_gam = [_sum_all(ds * s) for ds, s in zip(dss, ss)]
        dq, dk, dv, dgb = vjp(([d.astype(F32) for d in dvns], d_w, d_qk, d_qd, d_kd, d_gam))
        for i, (r, h) in enumerate(units):
            rows = chunk_rows[r]
            for part, d in enumerate((dq, dk, dv)):
                dqkv_ref[rows, part * 512 + h * 128:part * 512 + (h + 1) * 128] = d[i]
        for r, rows in enumerate(chunk_rows):
            dgb_ref[rows, :] = dgb[r]

    tokp = lambda j: pl.BlockSpec((tm, 512), lambda i: (i, j))
    st4 = pl.BlockSpec((r_per, GDN_HEADS, 128, 128), lambda i: (i, 0, 0, 0))
    dqkv, dgb = pl.pallas_call(
        prep_body, name="gdn_prep_bwd", grid=(n // tm,),
        in_specs=[tokp(0), tokp(1), tokp(2), pl.BlockSpec((tm, 128), lambda i: (i, 0)),
                  pl.BlockSpec((r_per, GDN_HEADS, CHUNK, CHUNK), lambda i: (i, 0, 0, 0)), st4, st4,
                  tokp(0), tokp(0), tokp(0)],
        out_specs=[pl.BlockSpec((tm, 1536), lambda i: (i, 0)), pl.BlockSpec((tm, 128), lambda i: (i, 0))],
        out_shape=[jax.ShapeDtypeStruct((n, 1536), F32), jax.ShapeDtypeStruct((n, 128), F32)],
        compiler_params=_params("parallel"),
    )(qkv, qkv, qkv, gb, tinv, st.reshape(bl * nc, GDN_HEADS, 128, 128), dst.reshape(bl * nc, GDN_HEADS, 128, 128),
      dvn.reshape(n, 512), d_o.reshape(n, 512), vn.reshape(n, 512))
    return dqkv, dog.reshape(n, 512), dgb, dnw


def _out_block(x2, tgt2, ya, yb, g1p3, wo, lnw, lnb, seq, tm=256):
    n = x2.shape[0]
    tpe = seq // tm
    bl = n // seq

    def body(x_ref, t_ref, ya_ref, yb_ref, g_ref, wo_ref, lnw_ref, lnb_ref,
             dz_ref, dya_ref, dyb_ref, dwo_ref, dg_ref, glw_ref, glb_ref, loss_ref):
        i = pl.program_id(0)

        @pl.when(i == 0)
        def _():
            dwo_ref[...] = jnp.zeros_like(dwo_ref)
            glw_ref[...] = jnp.zeros_like(glw_ref)
            glb_ref[...] = jnp.zeros_like(glb_ref)
            loss_ref[...] = jnp.zeros_like(loss_ref)

        @pl.when(i % tpe == 0)
        def _():
            dg_ref[...] = jnp.zeros_like(dg_ref)

        ya16 = ya_ref[...].astype(wo.dtype)
        yb16 = yb_ref[...].astype(wo.dtype)
        wa = wo_ref[0:GLA_WIDTH, :]
        wb = wo_ref[GLA_WIDTH:, :]
        y = _dot(ya16, wa) + _dot(yb16, wb)
        g1p = g_ref[0]
        z = ALPHA * x_ref[...] + g1p * y
        mu = jnp.mean(z, axis=-1, keepdims=True)
        zc = z - mu
        rstd = lax.rsqrt(jnp.mean(zc * zc, axis=-1, keepdims=True) + LN_EPS)
        xhat = zc * rstd
        diff = xhat * lnw_ref[...] + lnb_ref[...] - t_ref[...]
        loss_ref[...] += (0.5 / D_MODEL) * jnp.sum(jnp.sum(diff * diff, axis=-1, keepdims=True), axis=0, keepdims=True)
        dout = diff * (1.0 / D_MODEL)
        glw_ref[...] += jnp.sum(dout * xhat, axis=0, keepdims=True)
        glb_ref[...] += jnp.sum(dout, axis=0, keepdims=True)
        dxh = dout * lnw_ref[...]
        dz = rstd * (dxh - jnp.mean(dxh, axis=-1, keepdims=True)
                     - xhat * jnp.mean(dxh * xhat, axis=-1, keepdims=True))
        dz_ref[...] = dz
        dg_ref[0] += jnp.sum(dz * y, axis=0, keepdims=True)
        dy = (g1p * dz).astype(wo.dtype)
        dya_ref[...] = _dot_nt(dy, wa)
        dyb_ref[...] = _dot_nt(dy, wb)
        dwo_ref[0:GLA_WIDTH, :] += _dot_tn(ya16, dy)
        dwo_ref[GLA_WIDTH:, :] += _dot_tn(yb16, dy)

    row = lambda i: (i, 0)
    const = lambda i: (0, 0)
    per_ex = pl.BlockSpec((1, 1, D_MODEL), lambda i: (i // tpe, 0, 0))
    return pl.pallas_call(
        body, name="out_block", grid=(n // tm,),
        in_specs=[pl.BlockSpec((tm, D_MODEL), row), pl.BlockSpec((tm, D_MODEL), row),
                  pl.BlockSpec((tm, 512), row), pl.BlockSpec((tm, 512), row), per_ex,
                  pl.BlockSpec((D_MODEL, D_MODEL), const), pl.BlockSpec((1, D_MODEL), const),
                  pl.BlockSpec((1, D_MODEL), const)],
        out_specs=[pl.BlockSpec((tm, D_MODEL), row), pl.BlockSpec((tm, 512), row), pl.BlockSpec((tm, 512), row),
                   pl.BlockSpec((D_MODEL, D_MODEL), const), per_ex,
                   pl.BlockSpec((1, D_MODEL), const), pl.BlockSpec((1, D_MODEL), const),
                   pl.BlockSpec((1, 1), const)],
        out_shape=[jax.ShapeDtypeStruct((n, D_MODEL), F32), jax.ShapeDtypeStruct((n, 512), F32),
                   jax.ShapeDtypeStruct((n, 512), F32), jax.ShapeDtypeStruct((D_MODEL, D_MODEL), F32),
                   jax.ShapeDtypeStruct((bl, 1, D_MODEL), F32), jax.ShapeDtypeStruct((1, D_MODEL), F32),
                   jax.ShapeDtypeStruct((1, D_MODEL), F32), jax.ShapeDtypeStruct((1, 1), F32)],
        compiler_params=_params("arbitrary"),
    )(x2, tgt2, ya, yb, g1p3, wo, lnw, lnb)


def _proj_bwd_x(ds, ws, x2, dz, sc3, seq, tm=256):
    n = x2.shape[0]
    tpe = seq // tm
    bl = n // seq

    def body(da_ref, db_ref, dc_ref, dd1_ref, dd2_ref, wa_ref, wb_ref, wc_ref, wd_ref, x_ref, dz_ref, sc_ref,
             gx_ref, dsh_ref, dsc_ref):
        i = pl.program_id(0)

        @pl.when(i % tpe == 0)
        def _():
            dsh_ref[...] = jnp.zeros_like(dsh_ref)
            dsc_ref[...] = jnp.zeros_like(dsc_ref)

        cdt = ws[0].dtype
        dh = _dot(da_ref[...].astype(cdt), wa_ref[...])
        dh += _dot(db_ref[...].astype(cdt), wb_ref[...])
        dh += _dot(dc_ref[...].astype(cdt), wc_ref[...])
        dh += _dot((dd1_ref[...] + dd2_ref[...]).astype(cdt), wd_ref[...])
        gx_ref[...] = dh * sc_ref[0] + ALPHA * dz_ref[...]
        dsh_ref[0] += jnp.sum(dh, axis=0, keepdims=True)
        dsc_ref[0] += jnp.sum(dh * x_ref[...], axis=0, keepdims=True)

    row = lambda i: (i, 0)
    const = lambda i: (0, 0)
    per_ex = pl.BlockSpec((1, 1, D_MODEL), lambda i: (i // tpe, 0, 0))
    da, db, dc, (dd1, dd2) = ds
    return pl.pallas_call(
        body, name="proj_bwd_x", grid=(n // tm,),
        in_specs=[pl.BlockSpec((tm, d.shape[1]), row) for d in (da, db, dc, dd1, dd2)]
        + [pl.BlockSpec(w.shape, const) for w in ws]
        + [pl.BlockSpec((tm, D_MODEL), row), pl.BlockSpec((tm, D_MODEL), row), per_ex],
        out_specs=[pl.BlockSpec((tm, D_MODEL), row), per_ex, per_ex],
        out_shape=[jax.ShapeDtypeStruct((n, D_MODEL), F32), jax.ShapeDtypeStruct((bl, 1, D_MODEL), F32),
                   jax.ShapeDtypeStruct((bl, 1, D_MODEL), F32)],
        compiler_params=_params("arbitrary"),
    )(da, db, dc, dd1, dd2, *ws, x2, dz, sc3)


def _proj_bwd_w(x2, sc3, sh3, ds, seq, cdt, name, tm=256):
    n = x2.shape[0]
    tpe = seq // tm
    flat, groups = [], []
    for d in ds:
        parts = d if isinstance(d, tuple) else (d,)
        groups.append(len(parts))
        flat.extend(parts)
    nin = len(flat)

    def body(x_ref, sc_ref, sh_ref, *refs):
        i = pl.program_id(0)
        outs = refs[nin:]

        @pl.when(i == 0)
        def _():
            for o in outs:
                o[...] = jnp.zeros_like(o)

        h = (x_ref[...] * sc_ref[0] + sh_ref[0]).astype(cdt)
        pos = 0
        for o, cnt in zip(outs, groups):
            d = refs[pos][...]
            for extra in refs[pos + 1:pos + cnt]:
                d = d + extra[...]
            pos += cnt
            o[...] += _dot_tn(d.astype(cdt), h)

    row = lambda i: (i, 0)
    const = lambda i: (0, 0)
    per_ex = pl.BlockSpec((1, 1, D_MODEL), lambda i: (i // tpe, 0, 0))
    widths = [(d[0] if isinstance(d, tuple) else d).shape[1] for d in ds]
    return pl.pallas_call(
        body, name=name, grid=(n // tm,),
        in_specs=[pl.BlockSpec((tm, D_MODEL), row), per_ex, per_ex]
        + [pl.BlockSpec((tm, d.shape[1]), row) for d in flat],
        out_specs=[pl.BlockSpec((w, D_MODEL), const) for w in widths],
        out_shape=[jax.ShapeDtypeStruct((w, D_MODEL), F32) for w in widths],
        compiler_params=_params("arbitrary"),
    )(x2, sc3, sh3, *flat)


def _mod_block(c_all, w_ada_sh, b_blk):
    def body(c_ref, w_ref, b_ref, o_ref):
        o_ref[...] = _dot(c_ref[...], w_ref[...]) + b_ref[...]

    return pl.pallas_call(
        body, name="mod_block",
        out_shape=jax.ShapeDtypeStruct((c_all.shape[0], w_ada_sh.shape[1]), F32),
        compiler_params=pltpu.CompilerParams(vmem_limit_bytes=VMEM_LIMIT),
    )(c_all, w_ada_sh, b_blk)


def _ada_grads(c_all, dmod_all, dmod_blk):
    def body(c_ref, da_ref, db_ref, gw_ref, gb_ref):
        gw_ref[...] = _dot_tn(c_ref[...], db_ref[...])
        gb_ref[...] = jnp.sum(da_ref[...], axis=0, keepdims=True)

    return pl.pallas_call(
        body, name="ada_grads",
        out_shape=[jax.ShapeDtypeStruct((c_all.shape[1], dmod_blk.shape[1]), F32),
                   jax.ShapeDtypeStruct((1, dmod_all.shape[1]), F32)],
        compiler_params=pltpu.CompilerParams(vmem_limit_bytes=VMEM_LIMIT),
    )(c_all, dmod_all, dmod_blk)


def _sum_leading(parts, name):
    def body(p_ref, o_ref):
        acc = p_ref[0]
        for d in range(1, parts.shape[0]):
            acc = acc + p_ref[d]
        o_ref[...] = acc

    return pl.pallas_call(
        body, name=name, out_shape=jax.ShapeDtypeStruct(parts.shape[1:], F32),
        compiler_params=pltpu.CompilerParams(vmem_limit_bytes=VMEM_LIMIT),
    )(parts)


ELEMENTWISE_BLOCK_BYTES = 2 * 1024 * 1024


def _tile2d(rows, cols, row_align=8):
    if rows * cols * 4 <= ELEMENTWISE_BLOCK_BYTES:
        return rows, cols
    fits = [t for t in range(row_align, rows, row_align) if rows % t == 0 and t * cols * 4 <= ELEMENTWISE_BLOCK_BYTES]
    if fits:
        return fits[-1], cols
    fits = [t for t in range(128, cols, 128) if cols % t == 0 and rows * t * 4 <= ELEMENTWISE_BLOCK_BYTES]
    assert fits, (rows, cols)
    return rows, fits[-1]


def _add_n(arrs, name, out_dtypes=(F32,)):
    rows, cols = arrs[0].shape
    narrow = any(jnp.dtype(dt).itemsize < 4 for dt in tuple(out_dtypes) + tuple(a.dtype for a in arrs))
    tr, tc = _tile2d(rows, cols, 16 if narrow else 8)
    n_in = len(arrs)

    def body(*refs):
        acc = refs[0][...].astype(F32)
        for r in refs[1:n_in]:
            acc = acc + r[...].astype(F32)
        for o in refs[n_in:]:
            o[...] = acc.astype(o.dtype)

    spec = pl.BlockSpec((tr, tc), lambda i, j: (i, j))
    return pl.pallas_call(
        body, name=name, grid=(rows // tr, cols // tc), in_specs=[spec] * n_in, out_specs=[spec] * len(out_dtypes),
        out_shape=[jax.ShapeDtypeStruct((rows, cols), dt) for dt in out_dtypes],
        compiler_params=_params("parallel", "parallel"),
    )(*arrs)


def _chip_sum_blocks(a, b, per, blocks, name, chunk=128):
    rows, cols = a.shape
    padded = -(-per // 16) * 16
    assert rows >= (blocks - 1) * per + padded, (rows, per, blocks)

    def body(a_ref, b_ref, o_ref, o16_ref):
        for j in range(blocks):
            for r0 in range(0, padded, chunk):
                n_rows = min(chunk, padded - r0)
                src = pl.ds(j * per + r0, n_rows)
                s = a_ref[src, :] + b_ref[src, :]
                if per - r0 < n_rows:
                    s = jnp.where(_iota((n_rows, 1), 0) < per - r0, s, 0.0)
                o_ref[j, r0:r0 + n_rows, :] = s
                o16_ref[j, r0:r0 + n_rows, :] = s.astype(BF16)

    return pl.pallas_call(
        body, name=name,
        out_shape=[jax.ShapeDtypeStruct((blocks, padded, cols), F32), jax.ShapeDtypeStruct((blocks, padded, cols), BF16)],
        compiler_params=pltpu.CompilerParams(vmem_limit_bytes=VMEM_LIMIT),
    )(a, b)


GRAD_PAD_ROWS = 16


def _adamw(w, g, m, v, name):
    rows, cols = w.shape
    tr, tc = _tile2d(rows, cols)
    c1 = 1.0 / (1.0 - ADAM_B1 ** ADAM_STEP)
    c2 = 1.0 / (1.0 - ADAM_B2 ** ADAM_STEP)

    def body(w_ref, g_ref, m_ref, v_ref, d_ref, nm_ref, nv_ref):
        gg = g_ref[...]
        nm = ADAM_B1 * m_ref[...] + (1.0 - ADAM_B1) * gg
        nv = ADAM_B2 * v_ref[...] + (1.0 - ADAM_B2) * (gg * gg)
        nm_ref[...] = nm
        nv_ref[...] = nv
        d_ref[...] = -ADAM_LR * ((nm * c1) / (jnp.sqrt(nv * c2) + ADAM_EPS) + ADAM_WD * w_ref[...])

    spec = pl.BlockSpec((tr, tc), lambda i, j: (i, j))
    shp = jax.ShapeDtypeStruct((rows, cols), F32)
    return pl.pallas_call(
        body, name=name, grid=(rows // tr, cols // tc), in_specs=[spec] * 4, out_specs=[spec] * 3,
        out_shape=[shp, shp, shp], compiler_params=_params("parallel", "parallel"),
    )(w, g, m, v)


def _coords():
    return lax.axis_index("x"), lax.axis_index("y"), lax.axis_index("c")


def _all_gather8(blk, name):
    m_per, n = blk.shape

    def body(x_ref, out_ref, send_sems, recv_sems, local_sem):
        x, y, c = _coords()
        me, sibling = (x, y, c), (x, y, 1 - c)
        chips = [(1 - x, y), (x, 1 - y), (1 - x, 1 - y)]

        def rows(px, py, pc):
            return out_ref.at[pl.ds((4 * px + 2 * py + pc) * m_per, m_per), :]

        def copy(k, block, to, src=None):
            return pltpu.make_async_remote_copy(
                src_ref=rows(*block) if src is None else src, dst_ref=rows(*block),
                send_sem=send_sems.at[k], recv_sem=recv_sems.at[k], device_id=to, device_id_type=MESH)

        mine = pltpu.make_async_copy(x_ref, rows(*me), local_sem)
        mine.start()
        first = [copy(0, me, sibling, src=x_ref)]
        first += [copy(1 + j, me, (*chip, c), src=x_ref) for j, chip in enumerate(chips)]
        for cp in first:
            cp.start()
        passed = [copy(4 + j, (*chip, c), sibling) for j, chip in enumerate(chips)]
        for j, chip in enumerate(chips):
            copy(1 + j, (*chip, c), me).wait_recv()
            passed[j].start()
        copy(0, sibling, me).wait_recv()
        for j, chip in enumerate(chips):
            copy(4 + j, (*chip, 1 - c), me).wait_recv()
        for cp in first + passed:
            cp.wait_send()
        mine.wait()

    return pl.pallas_call(
        body, name=name,
        out_shape=jax.ShapeDtypeStruct((8 * m_per, n), blk.dtype),
        in_specs=[pl.BlockSpec(memory_space=pltpu.VMEM)],
        out_specs=pl.BlockSpec(memory_space=pltpu.VMEM),
        scratch_shapes=[pltpu.SemaphoreType.DMA((7,)), pltpu.SemaphoreType.DMA((7,)), pltpu.SemaphoreType.DMA],
        compiler_params=pltpu.CompilerParams(vmem_limit_bytes=VMEM_LIMIT),
    )(blk)


def _chip_gather(shards, split, name):
    k_arr = len(shards)

    def body(*refs):
        srcs, dsts = refs[:k_arr], refs[k_arr:2 * k_arr]
        send_sems, recv_sems, fwd_send_sems, fwd_recv_sems, local_sems = refs[2 * k_arr:]
        x, y, c = _coords()
        peers = [(1 - x, y, c), (x, 1 - y, c), (1 - x, 1 - y, c)]
        sibling = (x, y, 1 - c)
        me_chip = 2 * x + y

        def part(ref, a, core):
            if not split[a]:
                return ref
            half = shards[a].shape[1] // 2
            return ref.at[:, pl.ds(core * half, half)]

        def ici(a, j, src_chip, dst_dev):
            return pltpu.make_async_remote_copy(
                src_ref=part(srcs[a], a, c), dst_ref=part(dsts[a].at[src_chip], a, c),
                send_sem=send_sems.at[a, j], recv_sem=recv_sems.at[a, j], device_id=dst_dev, device_id_type=MESH)

        def d2d(a, j, src_chip, core):
            return pltpu.make_async_remote_copy(
                src_ref=part(dsts[a].at[src_chip], a, core), dst_ref=part(dsts[a].at[src_chip], a, core),
                send_sem=fwd_send_sems.at[a, j], recv_sem=fwd_recv_sems.at[a, j],
                device_id=sibling, device_id_type=MESH)

        local = [pltpu.make_async_copy(srcs[a], dsts[a].at[me_chip], local_sems.at[a]) for a in range(k_arr)]
        for cp in local:
            cp.start()
        sends = [ici(a, j, me_chip, peer) for a in range(k_arr) for j, peer in enumerate(peers)]
        for cp in sends:
            cp.start()
        forwards = []
        for a in range(k_arr):
            for j, peer in enumerate(peers):
                peer_chip = 2 * peer[0] + peer[1]
                ici(a, j, peer_chip, peer).wait_recv()
                if split[a]:
                    forwards.append(d2d(a, j, peer_chip, c))
                    forwards[-1].start()
        for a in range(k_arr):
            for j, peer in enumerate(peers):
                if split[a]:
                    d2d(a, j, 2 * peer[0] + peer[1], 1 - c).wait_recv()
        for cp in sends + forwards:
            cp.wait_send()
        for cp in local:
            cp.wait()

    any_spec = pl.BlockSpec(memory_space=pl.ANY)
    return pl.pallas_call(
        body, name=name,
        out_shape=[jax.ShapeDtypeStruct((4,) + s.shape, s.dtype) for s in shards],
        in_specs=[any_spec] * k_arr, out_specs=[any_spec] * k_arr,
        scratch_shapes=[pltpu.SemaphoreType.DMA((k_arr, 3))] * 4 + [pltpu.SemaphoreType.DMA((k_arr,))],
    )(*shards)


def _chip_scatter(pieces, name):
    k_arr = len(pieces)

    def body(*refs):
        srcs, dsts = refs[:k_arr], refs[k_arr:2 * k_arr]
        send_sems, recv_sems = refs[2 * k_arr:]
        x, y, c = _coords()
        peers = [(1 - x, y, c), (x, 1 - y, c), (1 - x, 1 - y, c)]
        copies = []
        for a in range(k_arr):
            for j, peer in enumerate(peers):
                copies.append(pltpu.make_async_remote_copy(
                    src_ref=srcs[a].at[2 * peer[0] + peer[1]], dst_ref=dsts[a].at[j],
                    send_sem=send_sems.at[a, j], recv_sem=recv_sems.at[a, j], device_id=peer, device_id_type=MESH))
        for cp in copies:
            cp.start()
        for cp in copies:
            cp.wait_recv()
        for cp in copies:
            cp.wait_send()

    any_spec = pl.BlockSpec(memory_space=pl.ANY)
    return pl.pallas_call(
        body, name=name,
        out_shape=[jax.ShapeDtypeStruct((3,) + p.shape[1:], p.dtype) for p in pieces],
        in_specs=[any_spec] * k_arr, out_specs=[any_spec] * k_arr,
        scratch_shapes=[pltpu.SemaphoreType.DMA((k_arr, 3)), pltpu.SemaphoreType.DMA((k_arr, 3))],
    )(*pieces)


def _sibling_swap(arrs, name):
    k_arr = len(arrs)

    def body(*refs):
        srcs, dsts = refs[:k_arr], refs[k_arr:2 * k_arr]
        send_sems, recv_sems = refs[2 * k_arr:]
        x, y, c = _coords()
        copies = [pltpu.make_async_remote_copy(
            src_ref=srcs[a], dst_ref=dsts[a], send_sem=send_sems.at[a], recv_sem=recv_sems.at[a],
            device_id=(x, y, 1 - c), device_id_type=MESH) for a in range(k_arr)]
        for cp in copies:
            cp.start()
        for cp in copies:
            cp.wait_recv()
        for cp in copies:
            cp.wait_send()

    any_spec = pl.BlockSpec(memory_space=pl.ANY)
    return pl.pallas_call(
        body, name=name,
        out_shape=[jax.ShapeDtypeStruct(a.shape, a.dtype) for a in arrs],
        in_specs=[any_spec] * k_arr, out_specs=[any_spec] * k_arr,
        scratch_shapes=[pltpu.SemaphoreType.DMA((k_arr,)), pltpu.SemaphoreType.DMA((k_arr,))],
    )(*arrs)


def _split_w_in(w_in_t):
    wa = jnp.concatenate([w_in_t[0:1024], w_in_t[1040:1552]], axis=0)
    wb = w_in_t[1552:3088]
    wc = w_in_t[3096:3608]
    wd = jnp.concatenate([w_in_t[1024:1040], w_in_t[3088:3096],
                          jnp.zeros((128 - SMALL_USED, w_in_t.shape[1]), w_in_t.dtype)], axis=0)
    return wa, wb, wc, wd


def _merge_dw_in(dwa, dwb, dwc, dwd):
    return jnp.concatenate([dwa[0:1024], dwd[0:GLA_RANK], dwa[1024:1536], dwb, dwd[GLA_RANK:SMALL_USED], dwc,
                            jnp.zeros((GRAD_PAD_ROWS, dwa.shape[1]), dwa.dtype)], axis=0)


def _local_step(x, mod, w_in16, w_out16, gla_wg, gla_bg, gla_nw, conv_w, a_log, dt_bias, gdn_nw, ln_w, ln_b, tgt):
    bl, seq, _ = x.shape
    n = bl * seq
    x2 = x.reshape(n, D_MODEL)
    tgt2 = tgt.reshape(n, D_MODEL)
    sh3 = mod[:, None, 0:D_MODEL]
    sc3 = 1.0 + mod[:, None, D_MODEL:2 * D_MODEL]
    g1p3 = 1.0 + mod[:, None, 2 * D_MODEL:]
    ws = _split_w_in(w_in16)
    wg = jnp.concatenate([gla_wg, jnp.zeros((128 - GLA_RANK, GLA_QK), F32)], axis=0)
    cw8 = jnp.concatenate([conv_w, jnp.zeros((8 - CONV_K, conv_w.shape[1]), F32)], axis=0)
    alog_v = jnp.zeros((1, 128), F32).at[:, LANE_A:LANE_A + GDN_HEADS].set(a_log)
    dtb_v = jnp.zeros((1, 128), F32).at[:, LANE_A:LANE_A + GDN_HEADS].set(dt_bias)

    pa, pb, pc, pd = _proj_fwd(x2, sc3, sh3, ws, seq)
    ya, st_a = _gla_fwd(pa, pd, wg, gla_bg, gla_nw, bl, seq)
    qkv, gb, conv_out = _gdn_pre_fwd(pb, pd, cw8, alog_v, dtb_v, bl, seq)
    yb, st_b = _gdn_fwd(qkv, gb, pc, gdn_nw, bl, seq)
    dz, dya, dyb, d_wo, d_gate, d_lnw, d_lnb, loss = _out_block(x2, tgt2, ya, yb, g1p3, w_out16, ln_w, ln_b, seq)
    da, dd1, d_wg, d_bg, d_nwa = _gla_bwd(pa, pd, st_a, dya, wg, gla_bg, gla_nw, bl, seq)
    dqkv, dc, dgb, d_nwb = _gdn_bwd(qkv, gb, pc, st_b, dyb, gdn_nw, bl, seq)
    db, dd2, d_cw8, d_alog, d_dtb = _gdn_pre_bwd(pb, conv_out, pd, dqkv, dgb, cw8, alog_v, dtb_v, bl, seq)
    gx, d_sh, d_sc = _proj_bwd_x((da, db, dc, (dd1, dd2)), ws, x2, dz, sc3, seq)
    (dwa,) = _proj_bwd_w(x2, sc3, sh3, [da], seq, w_in16.dtype, "proj_bwd_w_a")
    dwb, dwc, dwd = _proj_bwd_w(x2, sc3, sh3, [db, dc, (dd1, dd2)], seq, w_in16.dtype, "proj_bwd_w_bcd")
    grads = dict(
        w_in=_merge_dw_in(dwa, dwb, dwc, dwd),
        w_out=d_wo,
        gla_w_gate_up=d_wg[0:GLA_RANK, :],
        gla_b_gate=d_bg,
        gla_norm_w=d_nwa,
        gdn_conv_w=d_cw8[0:CONV_K, :],
        gdn_a_log=d_alog[:, LANE_A:LANE_A + GDN_HEADS],
        gdn_dt_bias=d_dtb[:, LANE_A:LANE_A + GDN_HEADS],
        gdn_norm_w=d_nwb,
        ln_w=d_lnw,
        ln_b=d_lnb,
        mod=jnp.concatenate([d_sh[:, 0, :], d_sc[:, 0, :], d_gate[:, 0, :]], axis=1),
    )
    return loss, gx.reshape(bl, seq, D_MODEL), grads


_SMALL = (("gla_b_gate", 256), ("gla_norm_w", 128), ("gdn_a_log", 4), ("gdn_dt_bias", 4), ("gdn_norm_w", 128),
          ("ln_w", 1024), ("ln_b", 1024), ("gla_w_gate_up", 16 * 256), ("gdn_conv_w", 4 * 1536), ("loss", 1),
          ("mod", 2 * 3072))


def _pack_small(grads):
    flat = jnp.concatenate([grads[k].reshape(-1) for k, _ in _SMALL])
    total = sum(sz for _, sz in _SMALL)
    rows = -(-total // 1024) * 8
    return jnp.concatenate([flat, jnp.zeros((rows * 128 - total,), F32)]).reshape(rows, 128)


def _unpack_small(flat):
    out, pos = {}, 0
    for k, sz in _SMALL:
        out[k] = flat[pos:pos + sz]
        pos += sz
    return out


def kernel(x, c, w_ada, b_ada, w_in, gla_w_gate_up, gla_b_gate, gla_norm_w, gdn_conv_w, gdn_a_log, gdn_dt_bias, gdn_norm_w, w_out, ln_w, ln_b, loss_target, m_w_ada, m_b_ada, m_w_in, m_gla_w_gate_up, m_gla_b_gate, m_gla_norm_w, m_gdn_conv_w, m_gdn_a_log, m_gdn_dt_bias, m_gdn_norm_w, m_w_out, m_ln_w, m_ln_b, v_w_ada, v_b_ada, v_w_in, v_gla_w_gate_up, v_gla_b_gate, v_gla_norm_w, v_gdn_conv_w, v_gdn_a_log, v_gdn_dt_bias, v_gdn_norm_w, v_w_out, v_ln_w, v_ln_b):
    ix, iy, ic = _coords()
    chip = 2 * ix + iy
    dev = 4 * ix + 2 * iy + ic
    bl = x.shape[0]
    ndev = 8

    c_all = _all_gather8(c.reshape(8, -1), "gather_c").reshape(ndev * bl, D_MODEL)
    ada_cols = w_ada.shape[2]
    b_blk = lax.dynamic_slice_in_dim(b_ada, chip * ada_cols, ada_cols, axis=1)
    mod_blk = _mod_block(c_all, w_ada[0], b_blk)
    mod_g = _all_gather8(mod_blk, "gather_mod").reshape(ndev, ndev * bl, ada_cols)
    mod_all = jnp.concatenate([mod_g[2 * j] for j in range(4)], axis=1)
    mod = lax.dynamic_slice_in_dim(mod_all, dev * bl, bl, axis=0)

    w_in_g, w_out_g, wg_g, cw_g = _chip_gather(
        [jnp.transpose(w_in[0]).astype(BF16), w_out[0].astype(BF16), gla_w_gate_up[0], gdn_conv_w[0]],
        [True, True, False, False], "gather_weights")
    w_in16 = w_in_g.reshape(IN_COLS, D_MODEL)
    w_out16 = w_out_g.reshape(D_MODEL, D_MODEL)
    gla_wg = jnp.concatenate([wg_g[j] for j in range(4)], axis=1)
    conv_w = jnp.concatenate([cw_g[j] for j in range(4)], axis=1)

    loss, grad_x, gr = _local_step(x, mod, w_in16, w_out16, gla_wg, gla_b_gate, gla_norm_w, conv_w,
                                   gdn_a_log, gdn_dt_bias, gdn_norm_w, ln_w, ln_b, loss_target)

    gr["loss"] = loss
    packed = _pack_small(gr)
    prow = packed.shape[0]
    gathered = _all_gather8(packed, "gather_small").reshape(ndev, prow, 128)
    small = _unpack_small(_sum_leading(gathered, "sum_small").reshape(-1))
    loss = small["loss"][0]
    mod_rows = gathered.reshape(ndev, prow * 128)[:, sum(sz for _, sz in _SMALL[:-1]):][:, :bl * 3 * D_MODEL]
    dmod_all = mod_rows.reshape(ndev * bl, 3 * D_MODEL)
    dmod_blk = lax.dynamic_slice_in_dim(dmod_all, chip * ada_cols, ada_cols, axis=1)
    g_w_ada, g_b_ada = _ada_grads(c_all, dmod_all, dmod_blk)
    wg_cols = gla_w_gate_up.shape[2]
    g_wg = lax.dynamic_slice_in_dim(small["gla_w_gate_up"].reshape(GLA_RANK, GLA_QK), chip * wg_cols, wg_cols, axis=1)
    cw_cols = gdn_conv_w.shape[2]
    g_cw = lax.dynamic_slice_in_dim(small["gdn_conv_w"].reshape(CONV_K, 3 * GDN_WIDTH), chip * cw_cols, cw_cols, axis=1)

    in_feats = w_in.shape[2]
    out_rows = w_out.shape[1]
    p_in = gr["w_in"]
    p_out = gr["w_out"].reshape(4, out_rows, D_MODEL)
    h_in, h_out = D_MODEL // 2, out_rows // 2
    mine_in = lax.dynamic_slice_in_dim(p_in, ic * h_in, h_in, axis=1)
    mine_out = lax.dynamic_slice_in_dim(p_out, ic * h_out, h_out, axis=1)
    theirs_in = lax.dynamic_slice_in_dim(p_in, (1 - ic) * h_in, h_in, axis=1)
    theirs_out = lax.dynamic_slice_in_dim(p_out, (1 - ic) * h_out, h_out, axis=1)
    got_in, got_out = _sibling_swap([theirs_in, theirs_out], "swap_halves")
    chip_in, chip_in16 = _chip_sum_blocks(mine_in, got_in, in_feats, 4, "chip_sum_in")
    chip_out, chip_out16 = _add_n([mine_out.reshape(4 * h_out, D_MODEL), got_out.reshape(4 * h_out, D_MODEL)],
                                  "chip_sum_out", (F32, BF16))
    chip_out = chip_out.reshape(4, h_out, D_MODEL)
    rs_in, rs_out = _chip_scatter([chip_in16, chip_out16.reshape(4, h_out, D_MODEL)], "scatter_grads")
    own_in = lax.dynamic_index_in_dim(chip_in, chip, axis=0, keepdims=False)
    own_out = lax.dynamic_index_in_dim(chip_out, chip, axis=0, keepdims=False)
    (half_in,) = _add_n([own_in, rs_in[0], rs_in[1], rs_in[2]], "reduce_in")
    (half_out,) = _add_n([own_out, rs_out[0], rs_out[1], rs_out[2]], "reduce_out")
    sib_in, sib_out = _sibling_swap([half_in, half_out], "swap_result")
    g_w_in_t = jnp.where(ic == 0, jnp.concatenate([half_in, sib_in], axis=1),
                         jnp.concatenate([sib_in, half_in], axis=1))[0:in_feats]
    g_w_out = jnp.where(ic == 0, jnp.concatenate([half_out, sib_out], axis=0),
                        jnp.concatenate([sib_out, half_out], axis=0))

    grads = dict(
        w_ada=g_w_ada[None], b_ada=g_b_ada, w_in=g_w_in_t, gla_w_gate_up=g_wg[None],
        gla_b_gate=small["gla_b_gate"].reshape(1, -1), gla_norm_w=small["gla_norm_w"].reshape(1, -1),
        gdn_conv_w=g_cw[None], gdn_a_log=small["gdn_a_log"].reshape(1, -1),
        gdn_dt_bias=small["gdn_dt_bias"].reshape(1, -1), gdn_norm_w=small["gdn_norm_w"].reshape(1, -1),
        w_out=g_w_out[None], ln_w=small["ln_w"].reshape(1, -1), ln_b=small["ln_b"].reshape(1, -1))
    weights = dict(w_ada=w_ada, b_ada=b_ada, w_in=w_in, gla_w_gate_up=gla_w_gate_up, gla_b_gate=gla_b_gate,
                   gla_norm_w=gla_norm_w, gdn_conv_w=gdn_conv_w, gdn_a_log=gdn_a_log, gdn_dt_bias=gdn_dt_bias,
                   gdn_norm_w=gdn_norm_w, w_out=w_out, ln_w=ln_w, ln_b=ln_b)
    m_in = dict(w_ada=m_w_ada, b_ada=m_b_ada, w_in=m_w_in, gla_w_gate_up=m_gla_w_gate_up, gla_b_gate=m_gla_b_gate,
                gla_norm_w=m_gla_norm_w, gdn_conv_w=m_gdn_conv_w, gdn_a_log=m_gdn_a_log, gdn_dt_bias=m_gdn_dt_bias,
                gdn_norm_w=m_gdn_norm_w, w_out=m_w_out, ln_w=m_ln_w, ln_b=m_ln_b)
    v_in = dict(w_ada=v_w_ada, b_ada=v_b_ada, w_in=v_w_in, gla_w_gate_up=v_gla_w_gate_up, gla_b_gate=v_gla_b_gate,
                gla_norm_w=v_gla_norm_w, gdn_conv_w=v_gdn_conv_w, gdn_a_log=v_gdn_a_log, gdn_dt_bias=v_gdn_dt_bias,
                gdn_norm_w=v_gdn_norm_w, w_out=v_w_out, ln_w=v_ln_w, ln_b=v_ln_b)
    names = list(weights)
    delta, new_m, new_v = {}, {}, {}
    for nm in names:
        shp = weights[nm].shape
        if nm == "w_in":
            to2d = lambda t: jnp.transpose(t[0])
            from2d = lambda t: jnp.transpose(t)[None]
            g2d = grads[nm]
        else:
            to2d = lambda t: t.reshape(-1, shp[-1])
            from2d = lambda t: t.reshape(shp)
            g2d = to2d(grads[nm])
        d, a, b = _adamw(to2d(weights[nm]), g2d, to2d(m_in[nm]), to2d(v_in[nm]), "adamw_" + nm)
        delta[nm], new_m[nm], new_v[nm] = from2d(d), from2d(a), from2d(b)
        grads[nm] = from2d(g2d)
    return (loss, grad_x, *[grads[k] for k in names], *[delta[k] for k in names],
            *[new_m[k] for k in names], *[new_v[k] for k in names])
```

```python
import functools

import jax
import jax.numpy as jnp
from jax import lax
from jax.experimental import pallas as pl
from jax.experimental.pallas import tpu as pltpu

F32 = jnp.float32
BF16 = jnp.bfloat16
HI = lax.Precision.HIGH
INV_PREC = None
MESH = pl.DeviceIdType.MESH

D_MODEL = 1024
GLA_HEADS = 4
GLA_DK = 64
GLA_DV = 128
GLA_QK = 256
GLA_WIDTH = 512
GLA_RANK = 16
GLA_GATE_NORM = 16.0
GDN_HEADS = 4
GDN_DK = 128
GDN_WIDTH = 512
CONV_K = 4
CHUNK = 64
LN_EPS = 1e-5
RMS_EPS = 1e-6
ALPHA = 2.0 ** 0.25
IN_COLS = 3608

LANE_A = GLA_RANK
LANE_B = GLA_RANK + GDN_HEADS
SMALL_USED = GLA_RANK + 2 * GDN_HEADS

ADAM_LR = 0.001
ADAM_B1 = 0.9
ADAM_B2 = 0.999
ADAM_EPS = 1e-08
ADAM_WD = 0.01
ADAM_STEP = 10

VMEM_LIMIT = 56 * 1024 * 1024


def _iota(shape, dim):
    return lax.broadcasted_iota(jnp.int32, shape, dim)


def _dot(a, b, prec=None):
    return lax.dot_general(a, b, (((1,), (0,)), ((), ())), precision=prec, preferred_element_type=F32)


def _dot_nt(a, b, prec=None):
    return lax.dot_general(a, b, (((1,), (1,)), ((), ())), precision=prec, preferred_element_type=F32)


def _dot_tn(a, b, prec=None):
    return lax.dot_general(a, b, (((0,), (0,)), ((), ())), precision=prec, preferred_element_type=F32)


def _log_sigmoid(z):
    return jnp.minimum(z, 0.0) - jnp.log1p(jnp.exp(-jnp.abs(z)))


def _softplus(z):
    return jnp.maximum(z, 0.0) + jnp.log1p(jnp.exp(-jnp.abs(z)))


def _silu(z):
    return z * jax.nn.sigmoid(z)


def _rms_gate(o, nw, og):
    return o * lax.rsqrt(jnp.mean(o * o, axis=-1, keepdims=True) + RMS_EPS) * nw * _silu(og)


def _params(*sem):
    return pltpu.CompilerParams(dimension_semantics=sem, vmem_limit_bytes=VMEM_LIMIT)


GLA_PAIRS = GLA_HEADS // 2


def _gla_chunk(qs, ks, lrs, vs, ogs, ss, wgs, bgs, nw):
    c = qs[0].shape[0]
    pair_units = [divmod(i, GLA_PAIRS) for i in range(len(qs))]
    head_units = [(i // GLA_HEADS, i // GLA_HEADS * GLA_PAIRS + (i % GLA_HEADS) // 2, i % 2) for i in range(len(vs))]
    row, col = _iota((c, c), 0), _iota((c, c), 1)
    causal = row >= col
    first_half = (_iota((c, 1), 0) < c // 2).astype(F32)
    lane = _iota((1, 128), 1)
    masks = [(lane < GLA_DK).astype(F32), (lane >= GLA_DK).astype(F32)]
    gs = [_log_sigmoid(_dot(lrs[e], wgs[p]) + bgs[p]) * (1.0 / GLA_GATE_NORM) for e, p in pair_units]
    bs = [_dot(causal.astype(F32), g, HI) for g in gs]
    b_ref = [jnp.sum(g * first_half, axis=0, keepdims=True) for g in gs]
    b_last = [jnp.sum(g, axis=0, keepdims=True) for g in gs]
    qsc = [q * (GLA_DK ** -0.5) for q in qs]
    qe = [q * jnp.exp(b - br) for q, b, br in zip(qsc, bs, b_ref)]
    ke = [k * jnp.exp(br - b) for k, b, br in zip(ks, bs, b_ref)]
    qb = [q * jnp.exp(b) for q, b in zip(qsc, bs)]
    kd = [k * jnp.exp(bl_ - b) for k, b, bl_ in zip(ks, bs, b_last)]
    att = [jnp.where(causal, _dot_nt(qe[u] * masks[half], ke[u]), 0.0) for _, u, half in head_units]
    o_inter = [_dot_nt(qb[u] * masks[half], ss[u]) for _, u, half in head_units]
    os_ = [_dot(a, v) + oi for a, v, oi in zip(att, vs, o_inter)]
    upd = [_dot_tn(v, kd[u] * masks[half]) for (_, u, half), v in zip(head_units, vs)]
    s_new = [s * jnp.exp(bl_) + upd[2 * u] + upd[2 * u + 1] for u, (s, bl_) in enumerate(zip(ss, b_last))]
    ys = [_rms_gate(o, nw, og) for o, og in zip(os_, ogs)]
    return ys, s_new


def _unit_lower_inverse_chain(a_list):
    c = a_list[0].shape[0]
    eye = (_iota((c, c), 0) == _iota((c, c), 1)).astype(F32)
    ps = [-a for a in a_list]
    ts = [eye + p for p in ps]
    levels = max(c.bit_length() - 2, 0)
    if levels:
        ps = [_dot(p, p, INV_PREC) for p in ps]
    for level in range(levels):
        last = level == levels - 1
        both = [_dot(t if last else jnp.concatenate([t, p], axis=0), p, INV_PREC) for t, p in zip(ts, ps)]
        ts = [t + m[0:c] for t, m in zip(ts, both)]
        if not last:
            ps = [m[c:2 * c] for m in both]
    return ts


@jax.custom_vjp
def _unit_lower_inverse(a_list):
    return _unit_lower_inverse_chain(a_list)


def _unit_lower_inverse_fwd(a_list):
    ts = _unit_lower_inverse_chain(a_list)
    return ts, ts


def _unit_lower_inverse_bwd(ts, dts):
    xs = [_dot_nt(dt, t, INV_PREC) for dt, t in zip(dts, ts)]
    return ([-_dot_tn(t, x, INV_PREC) for t, x in zip(ts, xs)],)


_unit_lower_inverse.defvjp(_unit_lower_inverse_fwd, _unit_lower_inverse_bwd)


@jax.custom_vjp
def _unit_lower_inverse_known(a_list, ts):
    return ts


def _unit_lower_inverse_known_fwd(a_list, ts):
    return ts, ts


def _unit_lower_inverse_known_bwd(ts, dts):
    return _unit_lower_inverse_bwd(ts, dts) + ([jnp.zeros_like(t) for t in ts],)


_unit_lower_inverse_known.defvjp(_unit_lower_inverse_known_fwd, _unit_lower_inverse_known_bwd)


def _gdn_prep_units(qs, ks, vs, gbs, t_known=None):
    c = qs[0].shape[0]
    units = [divmod(i, GDN_HEADS) for i in range(len(qs))]
    row, col = _iota((c, c), 0), _iota((c, c), 1)
    causal, strict = row >= col, row > col
    lane = _iota((1, 128), 1)
    d_alls = [_dot(causal.astype(F32), gb, HI) for gb in gbs]
    g_c, beta_c, d_c = [], [], []
    for r, h in units:
        sel_a = (lane == LANE_A + h).astype(F32)
        g_c.append(jnp.sum(gbs[r] * sel_a, axis=-1, keepdims=True))
        beta_c.append(jnp.sum(gbs[r] * (lane == LANE_B + h).astype(F32), axis=-1, keepdims=True))
        d_c.append(jnp.sum(d_alls[r] * sel_a, axis=-1, keepdims=True))
    d_last = [jnp.sum(g, axis=0, keepdims=True) for g in g_c]
    d_diff = [jnp.broadcast_to(d, (c, c)) - jnp.broadcast_to(d, (c, c)).T for d in d_c]
    decay_mat = [jnp.where(causal, jnp.exp(jnp.where(causal, dd, 0.0)), 0.0) for dd in d_diff]
    kb = [k * b for k, b in zip(ks, beta_c)]
    kbk_qk = [_dot_nt(jnp.concatenate([kbi, q], axis=0), k) for kbi, q, k in zip(kb, qs, ks)]
    a = [jnp.where(strict, m[0:c] * dm, 0.0) for m, dm in zip(kbk_qk, decay_mat)]
    qk = [jnp.where(causal, m[c:2 * c] * dm, 0.0) for m, dm in zip(kbk_qk, decay_mat)]
    t = _unit_lower_inverse(a) if t_known is None else _unit_lower_inverse_known(a, t_known)
    uw = [_dot(ti, jnp.concatenate([v * b, kbi * jnp.exp(d)], axis=1))
          for ti, v, b, kbi, d in zip(t, vs, beta_c, kb, d_c)]
    u = [m[:, 0:128] for m in uw]
    w = [m[:, 128:256] for m in uw]
    q_dec = [q * jnp.exp(d) for q, d in zip(qs, d_c)]
    k_dec = [k * jnp.exp(dl - d) for k, dl, d in zip(ks, d_last, d_c)]
    gamma = [jnp.exp(dl) for dl in d_last]
    return u, w, qk, q_dec, k_dec, gamma, t


def _sum_all(t):
    return jnp.sum(jnp.sum(t, axis=-1, keepdims=True), axis=0, keepdims=True)


def _gdn_pre_elem(ps, ab, alog_v, dtb_v):
    outs = []
    for j, p in enumerate(ps):
        s = _silu(p)
        if j < 2 * GDN_HEADS:
            s = s * lax.rsqrt(jnp.sum(s * s, axis=-1, keepdims=True) + RMS_EPS)
        if j < GDN_HEADS:
            s = s * (GDN_DK ** -0.5)
        outs.append(s)
    lane = _iota((1, 128), 1)
    is_a = (lane >= LANE_A) & (lane < LANE_A + GDN_HEADS)
    is_b = (lane >= LANE_B) & (lane < LANE_B + GDN_HEADS)
    g = -jnp.exp(alog_v) * _softplus(ab + dtb_v)
    gb = jnp.where(is_a, g, jnp.where(is_b, jax.nn.sigmoid(ab), 0.0))
    return tuple(outs) + (gb,)


def _proj_fwd(x2, sc3, sh3, ws, seq, tm=256):
    n = x2.shape[0]
    tpe = seq // tm
    nw = len(ws)

    def body(x_ref, sc_ref, sh_ref, *refs):
        h = (x_ref[...] * sc_ref[0] + sh_ref[0]).astype(ws[0].dtype)
        for w_ref, o_ref in zip(refs[:nw], refs[nw:]):
            o_ref[...] = _dot_nt(h, w_ref[...])

    row = lambda i: (i, 0)
    per_ex = pl.BlockSpec((1, 1, D_MODEL), lambda i: (i // tpe, 0, 0))
    return pl.pallas_call(
        body, name="proj_fwd", grid=(n // tm,),
        in_specs=[pl.BlockSpec((tm, D_MODEL), row), per_ex, per_ex]
        + [pl.BlockSpec(w.shape, lambda i: (0, 0)) for w in ws],
        out_specs=[pl.BlockSpec((tm, w.shape[0]), row) for w in ws],
        out_shape=[jax.ShapeDtypeStruct((n, w.shape[0]), F32) for w in ws],
        compiler_params=_params("parallel"),
    )(x2, sc3, sh3, *ws)


def _gla_fwd(pa, pd, wg, bg, nw, bl, seq):
    n = pa.shape[0]
    nc = seq // CHUNK
    heads = [(e, h, slice(h * 128, (h + 1) * 128)) for e in range(bl) for h in range(GLA_HEADS)]
    pair_cols = [slice(p * 128, (p + 1) * 128) for p in range(GLA_PAIRS)]
    pairs = [(e, p, pair_cols[p]) for e in range(bl) for p in range(GLA_PAIRS)]

    def body(q_ref, k_ref, v_ref, og_ref, lr_ref, wg_ref, bg_ref, nw_ref, y_ref, st_ref, s_scr):
        @pl.when(pl.program_id(0) == 0)
        def _():
            s_scr[...] = jnp.zeros_like(s_scr)

        ss = [s_scr[e, p] for e, p, _ in pairs]
        for (e, p, _), s in zip(pairs, ss):
            st_ref[e, 0, p] = s
        ys, s_new = _gla_chunk([q_ref[e, :, cols] for e, _, cols in pairs], [k_ref[e, :, cols] for e, _, cols in pairs],
                               [lr_ref[e] for e in range(bl)],
                               [v_ref[e, :, cols] for e, _, cols in heads], [og_ref[e, :, cols] for e, _, cols in heads],
                               ss, [wg_ref[:, cols] for cols in pair_cols], [bg_ref[:, cols] for cols in pair_cols],
                               nw_ref[...])
        for (e, h, cols), y in zip(heads, ys):
            y_ref[e, :, cols] = y.astype(y_ref.dtype)
        for (e, p, _), s in zip(pairs, s_new):
            s_scr[e, p] = s

    tok = lambda w, j: pl.BlockSpec((bl, CHUNK, w), lambda i: (0, i, j))
    const = lambda i: (0, 0)
    pa3 = pa.reshape(bl, seq, 1536)
    y, st = pl.pallas_call(
        body, name="gla_fwd", grid=(nc,),
        in_specs=[tok(256, 0), tok(256, 1), tok(512, 1), tok(512, 2), tok(128, 0),
                  pl.BlockSpec(wg.shape, const), pl.BlockSpec(bg.shape, const), pl.BlockSpec(nw.shape, const)],
        out_specs=[tok(512, 0), pl.BlockSpec((bl, 1, GLA_PAIRS, 128, 128), lambda i: (0, i, 0, 0, 0))],
        out_shape=[jax.ShapeDtypeStruct((bl, seq, 512), MM_DTYPE),
                   jax.ShapeDtypeStruct((bl, nc, GLA_PAIRS, 128, 128), F32)],
        scratch_shapes=[pltpu.VMEM((bl, GLA_PAIRS, 128, 128), F32)],
        compiler_params=_params("arbitrary"),
    )(pa3, pa3, pa3, pa3, pd.reshape(bl, seq, 128), wg, bg, nw)
    return y.reshape(n, 512), st


def _gla_bwd(pa, pd, st, dya, wg, bg, nw, bl, seq):
    n = pa.shape[0]
    nc = seq // CHUNK
    heads = [(e, h, slice(h * 128, (h + 1) * 128)) for e in range(bl) for h in range(GLA_HEADS)]
    pair_cols = [slice(p * 128, (p + 1) * 128) for p in range(GLA_PAIRS)]
    pairs = [(e, p, pair_cols[p]) for e in range(bl) for p in range(GLA_PAIRS)]

    def body(q_ref, k_ref, v_ref, og_ref, lr_ref, st_ref, dy_ref, wg_ref, bg_ref, nw_ref,
             da_ref, dd_ref, dwg_ref, dbg_ref, dnw_ref, ds_scr):
        @pl.when(pl.program_id(0) == 0)
        def _():
            dwg_ref[...] = jnp.zeros_like(dwg_ref)
            dbg_ref[...] = jnp.zeros_like(dbg_ref)
            dnw_ref[...] = jnp.zeros_like(dnw_ref)
            ds_scr[...] = jnp.zeros_like(ds_scr)

        _, vjp = jax.vjp(_gla_chunk, [q_ref[e, :, cols] for e, _, cols in pairs],
                         [k_ref[e, :, cols] for e, _, cols in pairs], [lr_ref[e] for e in range(bl)],
                         [v_ref[e, :, cols] for e, _, cols in heads], [og_ref[e, :, cols] for e, _, cols in heads],
                         [st_ref[e, 0, p] for e, p, _ in pairs],
                         [wg_ref[:, cols] for cols in pair_cols], [bg_ref[:, cols] for cols in pair_cols], nw_ref[...])
        dq, dk, dlr, dv, dog, ds, dwg, dbg, dnw = vjp(([dy_ref[e, :, cols] for e, _, cols in heads],
                                                         [ds_scr[e, p] for e, p, _ in pairs]))
        for e in range(bl):
            dd_ref[e] = dlr[e]
        for i, (e, p, cols) in enumerate(pairs):
            da_ref[e, :, cols] = dq[i].astype(da_ref.dtype)
            da_ref[e, :, GLA_QK + p * 128:GLA_QK + (p + 1) * 128] = dk[i].astype(da_ref.dtype)
            ds_scr[e, p] = ds[i]
        for i, (e, h, _) in enumerate(heads):
            da_ref[e, :, 512 + h * 128:512 + (h + 1) * 128] = dv[i].astype(da_ref.dtype)
            da_ref[e, :, 1024 + h * 128:1024 + (h + 1) * 128] = dog[i].astype(da_ref.dtype)
        for p, cols in enumerate(pair_cols):
            dwg_ref[:, cols] += dwg[p]
            dbg_ref[:, cols] += dbg[p]
        dnw_ref[...] += dnw

    tok = lambda w, j: pl.BlockSpec((bl, CHUNK, w), lambda i: (0, nc - 1 - i, j))
    const = lambda i: (0, 0)
    pa3 = pa.reshape(bl, seq, 1536)
    da, dd, dwg, dbg, dnw = pl.pallas_call(
        body, name="gla_bwd", grid=(nc,),
        in_specs=[tok(256, 0), tok(256, 1), tok(512, 1), tok(512, 2), tok(128, 0),
                  pl.BlockSpec((bl, 1, GLA_PAIRS, 128, 128), lambda i: (0, nc - 1 - i, 0, 0, 0)), tok(512, 0),
                  pl.BlockSpec(wg.shape, const), pl.BlockSpec(bg.shape, const), pl.BlockSpec(nw.shape, const)],
        out_specs=[tok(1536, 0), tok(128, 0),
                   pl.BlockSpec(wg.shape, const), pl.BlockSpec(bg.shape, const), pl.BlockSpec(nw.shape, const)],
        out_shape=[jax.ShapeDtypeStruct((bl, seq, 1536), MM_DTYPE), jax.ShapeDtypeStruct((bl, seq, 128), F32),
                   jax.ShapeDtypeStruct(wg.shape, F32), jax.ShapeDtypeStruct(bg.shape, F32),
                   jax.ShapeDtypeStruct(nw.shape, F32)],
        scratch_shapes=[pltpu.VMEM((bl, GLA_PAIRS, 128, 128), F32)],
        compiler_params=_params("arbitrary"),
    )(pa3, pa3, pa3, pa3, pd.reshape(bl, seq, 128), st, dya.reshape(bl, seq, 512), wg, bg, nw)
    return da.reshape(n, 1536), dd.reshape(n, 128), dwg, dbg, dnw


def _conv_taps(buf_ref, w_ref, base, rows):
    acc = w_ref[0:1, :] * buf_ref[pl.ds(base, rows), :]
    for k in range(1, CONV_K):
        acc = acc + w_ref[k:k + 1, :] * buf_ref[pl.ds(base + k, rows), :]
    return acc


def _gdn_pre_fwd(pb, pd, cw8, alog_v, dtb_v, bl, seq, tm=256):
    n = pb.shape[0]
    tpe = seq // tm
    t8 = tm // 8

    def body(u_ref, prev_ref, ab_ref, w_ref, al_ref, dt_ref, qkv_ref, gb_ref, p_ref, buf):
        i = pl.program_id(0)
        keep = (i % tpe != 0).astype(F32)
        buf[0:8, :] = prev_ref[...] * keep
        buf[8:8 + tm, :] = u_ref[...]
        p = _conv_taps(buf, w_ref, 8 - (CONV_K - 1), tm)
        p_ref[...] = p
        ps = [p[:, j * 128:(j + 1) * 128] for j in range(12)]
        outs = _gdn_pre_elem(ps, ab_ref[...], al_ref[...], dt_ref[...])
        for j in range(12):
            qkv_ref[:, j * 128:(j + 1) * 128] = outs[j]
        gb_ref[...] = outs[12]

    row = lambda i: (i, 0)
    const = lambda i: (0, 0)
    return pl.pallas_call(
        body, name="gdn_pre_fwd", grid=(n // tm,),
        in_specs=[pl.BlockSpec((tm, 1536), row),
                  pl.BlockSpec((8, 1536), lambda i: (jnp.maximum(i * t8 - 1, 0), 0)),
                  pl.BlockSpec((tm, 128), row),
                  pl.BlockSpec((8, 1536), const), pl.BlockSpec((1, 128), const), pl.BlockSpec((1, 128), const)],
        out_specs=[pl.BlockSpec((tm, 1536), row), pl.BlockSpec((tm, 128), row), pl.BlockSpec((tm, 1536), row)],
        out_shape=[jax.ShapeDtypeStruct((n, 1536), F32), jax.ShapeDtypeStruct((n, 128), F32),
                   jax.ShapeDtypeStruct((n, 1536), F32)],
        scratch_shapes=[pltpu.VMEM((tm + 8, 1536), F32)],
        compiler_params=_params("parallel"),
    )(pb, pb, pd, cw8, alog_v, dtb_v)


def _gdn_pre_bwd(pb, conv_out, pd, dqkv, dgb, cw8, alog_v, dtb_v, bl, seq, tm=256):
    n = pb.shape[0]
    tpe = seq // tm
    t8 = tm // 8
    nb8 = n // 8
    ext = tm + 8

    def body(u_ref, p_ref, pn_ref, ab_ref, abn_ref, dq_ref, dqn_ref, dgb_ref, w_ref, al_ref, dt_ref,
             du_ref, dab_ref, dw_ref, dal_ref, ddt_ref, dpbuf):
        i = pl.program_id(0)

        @pl.when(i == 0)
        def _():
            dw_ref[...] = jnp.zeros_like(dw_ref)
            dal_ref[...] = jnp.zeros_like(dal_ref)
            ddt_ref[...] = jnp.zeros_like(ddt_ref)

        keep_next = (i % tpe != tpe - 1).astype(F32)
        ps = [jnp.concatenate([p_ref[:, j * 128:(j + 1) * 128], pn_ref[:, j * 128:(j + 1) * 128]], axis=0)
              for j in range(12)]
        ab = jnp.concatenate([ab_ref[...], abn_ref[...]], axis=0)
        _, vjp = jax.vjp(_gdn_pre_elem, ps, ab, al_ref[...], dt_ref[...])
        zeros8 = jnp.zeros((8, 128), F32)
        cts = tuple(jnp.concatenate([dq_ref[:, j * 128:(j + 1) * 128],
                                     dqn_ref[:, j * 128:(j + 1) * 128] * keep_next], axis=0) for j in range(12))
        cts += (jnp.concatenate([dgb_ref[...], zeros8], axis=0),)
        dps, dab, dal, ddt = vjp(cts)
        for j in range(12):
            dpbuf[:, j * 128:(j + 1) * 128] = dps[j]
        dab_ref[...] = dab[0:tm, :]
        dal_ref[...] += dal
        ddt_ref[...] += ddt
        u = u_ref[...]
        du = None
        for k in range(CONV_K):
            dp_k = dpbuf[pl.ds(CONV_K - 1 - k, tm), :]
            term = w_ref[k:k + 1, :] * dp_k
            du = term if du is None else du + term
            dw_ref[k:k + 1, :] += jnp.sum(u * dp_k, axis=0, keepdims=True)
        du_ref[...] = du.astype(du_ref.dtype)

    row = lambda i: (i, 0)
    next8 = lambda i: (jnp.minimum((i + 1) * t8, nb8 - 1), 0)
    const = lambda i: (0, 0)
    return pl.pallas_call(
        body, name="gdn_pre_bwd", grid=(n // tm,),
        in_specs=[pl.BlockSpec((tm, 1536), row), pl.BlockSpec((tm, 1536), row), pl.BlockSpec((8, 1536), next8),
                  pl.BlockSpec((tm, 128), row), pl.BlockSpec((8, 128), next8),
                  pl.BlockSpec((tm, 1536), row), pl.BlockSpec((8, 1536), next8),
                  pl.BlockSpec((tm, 128), row),
                  pl.BlockSpec((8, 1536), const), pl.BlockSpec((1, 128), const), pl.BlockSpec((1, 128), const)],
        out_specs=[pl.BlockSpec((tm, 1536), row), pl.BlockSpec((tm, 128), row),
                   pl.BlockSpec((8, 1536), const), pl.BlockSpec((1, 128), const), pl.BlockSpec((1, 128), const)],
        out_shape=[jax.ShapeDtypeStruct((n, 1536), MM_DTYPE), jax.ShapeDtypeStruct((n, 128), F32),
                   jax.ShapeDtypeStruct((8, 1536), F32), jax.ShapeDtypeStruct((1, 128), F32),
                   jax.ShapeDtypeStruct((1, 128), F32)],
        scratch_shapes=[pltpu.VMEM((ext, 1536), F32)],
        compiler_params=_params("arbitrary"),
    )(pb, conv_out, conv_out, pd, pd, dqkv, dqkv, dgb, cw8, alog_v, dtb_v)


GDN_PREP_CHUNKS = 4
GDN_PREP_BWD_CHUNKS = 2
GDN_SCAN_CHUNKS = 2
MM_DTYPE = BF16


def _head_cols(ref, rows, base=0):
    return [ref[rows, base + h * 128:base + (h + 1) * 128] for h in range(GDN_HEADS)]


def _gdn_prep(qkv, gb):
    n = qkv.shape[0]
    r_per = GDN_PREP_CHUNKS
    tm = r_per * CHUNK

    def body(q_ref, k_ref, v_ref, gb_ref, u_ref, w_ref, qd_ref, kd_ref, qk_ref, t_ref, gam_ref):
        rowid = _iota((8, 128), 0)
        chunk_rows = [slice(r * CHUNK, (r + 1) * CHUNK) for r in range(r_per)]
        gather = lambda ref: [t for rows in chunk_rows for t in _head_cols(ref, rows)]
        u, w, qk, qd, kd, gamma, tinv = _gdn_prep_units(gather(q_ref), gather(k_ref), gather(v_ref),
                                                        [gb_ref[rows, :] for rows in chunk_rows])
        for r, rows in enumerate(chunk_rows):
            gam = jnp.zeros((8, 128), F32)
            for h in range(GDN_HEADS):
                i = r * GDN_HEADS + h
                cols = slice(h * 128, (h + 1) * 128)
                u_ref[rows, cols] = u[i]
                w_ref[rows, cols] = w[i].astype(MM_DTYPE)
                qd_ref[rows, cols] = qd[i].astype(MM_DTYPE)
                kd_ref[rows, cols] = kd[i].astype(MM_DTYPE)
                qk_ref[r, h] = qk[i].astype(MM_DTYPE)
                t_ref[r, h] = tinv[i].astype(MM_DTYPE)
                gam = jnp.where(rowid == h, gamma[i], gam)
            gam_ref[r] = gam

    tok = lambda j: pl.BlockSpec((tm, 512), lambda i: (i, j))
    return pl.pallas_call(
        body, name="gdn_prep", grid=(n // tm,),
        in_specs=[tok(0), tok(1), tok(2), pl.BlockSpec((tm, 128), lambda i: (i, 0))],
        out_specs=[tok(0)] * 4 + [pl.BlockSpec((r_per, GDN_HEADS, CHUNK, CHUNK), lambda i: (i, 0, 0, 0))] * 2
        + [pl.BlockSpec((r_per, 8, 128), lambda i: (i, 0, 0))],
        out_shape=[jax.ShapeDtypeStruct((n, 512), F32)] + [jax.ShapeDtypeStruct((n, 512), MM_DTYPE)] * 3
        + [jax.ShapeDtypeStruct((n // CHUNK, GDN_HEADS, CHUNK, CHUNK), MM_DTYPE)] * 2
        + [jax.ShapeDtypeStruct((n // CHUNK, 8, 128), F32)],
        compiler_params=_params("parallel"),
    )(qkv, qkv, qkv, gb)


def _gdn_fwd(qkv, gb, pc, nw, bl, seq):
    n = qkv.shape[0]
    nc = seq // CHUNK
    u, w, qd, kd, qk, tinv, gam = _gdn_prep(qkv, gb)
    tok3 = lambda t: t.reshape(bl, seq, 512)
    qk5 = qk.reshape(bl, nc, GDN_HEADS, CHUNK, CHUNK)
    gam4 = gam.reshape(bl, nc, 8, 128)

    r_per = GDN_SCAN_CHUNKS
    mm = lambda t: t.astype(MM_DTYPE)

    def body(u_ref, w_ref, qd_ref, kd_ref, qk_ref, gam_ref, og_ref, nw_ref, o_ref, y_ref, vn_ref, st_ref, s_scr):
        @pl.when(pl.program_id(0) == 0)
        def _():
            s_scr[...] = jnp.zeros_like(s_scr)

        units = [(b, h, slice(h * 128, (h + 1) * 128)) for b in range(bl) for h in range(GDN_HEADS)]
        ss = [s_scr[b, h] for b, h, _ in units]
        for r in range(r_per):
            rows = slice(r * CHUNK, (r + 1) * CHUNK)
            for (b, h, _), s in zip(units, ss):
                st_ref[b, r, h] = s
            ws_qs = [_dot(jnp.concatenate([w_ref[b, rows, cols], qd_ref[b, rows, cols]], axis=0), mm(s))
                     for (b, h, cols), s in zip(units, ss)]
            v_new = [u_ref[b, rows, cols] - m[0:CHUNK] for (b, h, cols), m in zip(units, ws_qs)]
            os_ = [m[CHUNK:2 * CHUNK] + _dot(qk_ref[b, r, h], mm(vn))
                   for (b, h, cols), m, vn in zip(units, ws_qs, v_new)]
            ss = [s * gam_ref[b, r, h:h + 1, :] + _dot_tn(kd_ref[b, rows, cols], mm(vn))
                  for (b, h, cols), s, vn in zip(units, ss, v_new)]
            for (b, h, cols), vn, o in zip(units, v_new, os_):
                vn_ref[b, rows, cols] = mm(vn)
                o_ref[b, rows, cols] = o
                y_ref[b, rows, cols] = mm(_rms_gate(o, nw_ref[...], og_ref[b, rows, cols]))
        for (b, h, _), s in zip(units, ss):
            s_scr[b, h] = s

    tok = pl.BlockSpec((bl, r_per * CHUNK, 512), lambda i: (0, i, 0))
    st_spec = pl.BlockSpec((bl, r_per, GDN_HEADS, 128, 128), lambda i: (0, i, 0, 0, 0))
    tok_shape = jax.ShapeDtypeStruct((bl, seq, 512), F32)
    o, y, vn, st = pl.pallas_call(
        body, name="gdn_scan_fwd", grid=(nc // r_per,),
        in_specs=[tok, tok, tok, tok,
                  pl.BlockSpec((bl, r_per, GDN_HEADS, CHUNK, CHUNK), lambda i: (0, i, 0, 0, 0)),
                  pl.BlockSpec((bl, r_per, 8, 128), lambda i: (0, i, 0, 0)), tok,
                  pl.BlockSpec(nw.shape, lambda i: (0, 0))],
        out_specs=[tok, tok, tok, st_spec],
        out_shape=[tok_shape, jax.ShapeDtypeStruct((bl, seq, 512), MM_DTYPE), jax.ShapeDtypeStruct((bl, seq, 512), MM_DTYPE),
                   jax.ShapeDtypeStruct((bl, nc, GDN_HEADS, 128, 128), F32)],
        scratch_shapes=[pltpu.VMEM((bl, GDN_HEADS, 128, 128), F32)],
        compiler_params=_params("arbitrary"),
    )(tok3(u), tok3(w), tok3(qd), tok3(kd), qk5, gam4, tok3(pc), nw)
    return y.reshape(n, 512), (o, st, w, qd, kd, qk5, gam4, tinv, vn)


def _gdn_bwd(qkv, gb, pc, res, dyb, nw, bl, seq):
    n = qkv.shape[0]
    nc = seq // CHUNK
    o, st, w, qd, kd, qk5, gam4, tinv, vn = res
    tok3 = lambda t: t.reshape(bl, seq, 512)

    def scan_body(dy_ref, o_ref, og_ref, w_ref, qd_ref, kd_ref, qk_ref, gam_ref, nw_ref,
                  do_ref, dog_ref, dvn_ref, dst_ref, dnw_ref, ds_scr):
        @pl.when(pl.program_id(0) == 0)
        def _():
            ds_scr[...] = jnp.zeros_like(ds_scr)
            dnw_ref[...] = jnp.zeros_like(dnw_ref)

        units = [(b, h, slice(h * 128, (h + 1) * 128)) for b in range(bl) for h in range(GDN_HEADS)]
        dnw = jnp.zeros(nw.shape, F32)
        dss = [ds_scr[b, h] for b, h, _ in units]
        for r in reversed(range(r_scan)):
            rows = slice(r * CHUNK, (r + 1) * CHUNK)
            d_os = []
            for b, h, cols in units:
                _, vjp = jax.vjp(_rms_gate, o_ref[b, rows, cols], nw_ref[...], og_ref[b, rows, cols])
                d_o, dnw_h, dog = vjp(dy_ref[b, rows, cols])
                do_ref[b, rows, cols] = mm(d_o)
                dog_ref[b, rows, cols] = mm(dog)
                dnw = dnw + dnw_h
                d_os.append(mm(d_o))
            for (b, h, _), ds in zip(units, dss):
                dst_ref[b, r, h] = ds
            dvn_a = [_dot(kd_ref[b, rows, cols], mm(ds)) for (b, h, cols), ds in zip(units, dss)]
            dvns = [a + _dot_tn(qk_ref[b, r, h], d_o) for (b, h, cols), a, d_o in zip(units, dvn_a, d_os)]
            for (b, h, cols), dvn in zip(units, dvns):
                dvn_ref[b, rows, cols] = mm(dvn)
            dss = [ds * gam_ref[b, r, h:h + 1, :] + _dot_tn(
                jnp.concatenate([qd_ref[b, rows, cols], w_ref[b, rows, cols]], axis=0),
                jnp.concatenate([d_o, mm(-dvn)], axis=0))
                for (b, h, cols), d_o, ds, dvn in zip(units, d_os, dss, dvns)]
        dnw_ref[...] += dnw
        for (b, h, _), ds in zip(units, dss):
            ds_scr[b, h] = ds

    r_scan = GDN_SCAN_CHUNKS
    mm = lambda t: t.astype(MM_DTYPE)
    rev = lambda i: nc // r_scan - 1 - i
    tok = pl.BlockSpec((bl, r_scan * CHUNK, 512), lambda i: (0, rev(i), 0))
    st_spec = pl.BlockSpec((bl, r_scan, GDN_HEADS, 128, 128), lambda i: (0, rev(i), 0, 0, 0))
    tok_shape = jax.ShapeDtypeStruct((bl, seq, 512), F32)
    tok_mm = jax.ShapeDtypeStruct((bl, seq, 512), MM_DTYPE)
    d_o, dog, dvn, dst, dnw = pl.pallas_call(
        scan_body, name="gdn_scan_bwd", grid=(nc // r_scan,),
        in_specs=[tok] * 6 + [pl.BlockSpec((bl, r_scan, GDN_HEADS, CHUNK, CHUNK), lambda i: (0, rev(i), 0, 0, 0)),
                              pl.BlockSpec((bl, r_scan, 8, 128), lambda i: (0, rev(i), 0, 0)),
                              pl.BlockSpec(nw.shape, lambda i: (0, 0))],
        out_specs=[tok, tok, tok, st_spec, pl.BlockSpec(nw.shape, lambda i: (0, 0))],
        out_shape=[tok_mm, tok_mm, tok_mm, jax.ShapeDtypeStruct(st.shape, F32),
                   jax.ShapeDtypeStruct(nw.shape, F32)],
        scratch_shapes=[pltpu.VMEM((bl, GDN_HEADS, 128, 128), F32)],
        compiler_params=_params("arbitrary"),
    )(tok3(dyb), o, tok3(pc), tok3(w), tok3(qd), tok3(kd), qk5, gam4, nw)

    r_per = GDN_PREP_BWD_CHUNKS
    tm = r_per * CHUNK

    def prep_body(q_ref, k_ref, v_ref, gb_ref, t_ref, st_ref, dst_ref, dvn_ref, do_ref, vn_ref, dqkv_ref, dgb_ref):
        chunk_rows = [slice(r * CHUNK, (r + 1) * CHUNK) for r in range(r_per)]
        gather = lambda ref: [t for rows in chunk_rows for t in _head_cols(ref, rows)]
        units = [(r, h) for r in range(r_per) for h in range(GDN_HEADS)]
        t_known = [t_ref[r, h].astype(F32) for r, h in units]
        prep = lambda q, k, v, g: _gdn_prep_units(q, k, v, g, t_known)[:6]
        _, vjp = jax.vjp(prep, gather(q_ref), gather(k_ref), gather(v_ref), [gb_ref[rows, :] for rows in chunk_rows])
        ss = [st_ref[r, h] for r, h in units]
        dss = [dst_ref[r, h] for r, h in units]
        dvns, d_os, v_new = gather(dvn_ref), gather(do_ref), gather(vn_ref)
        both = [_dot_nt(jnp.concatenate([dvn, d_o], axis=0), s.astype(MM_DTYPE)) for dvn, d_o, s in zip(dvns, d_os, ss)]
        d_w = [-m[0:CHUNK] for m in both]
        d_qd = [m[CHUNK:2 * CHUNK] for m in both]
        d_qk = [_dot_nt(d_o, vn) for d_o, vn in zip(d_os, v_new)]
        d_kd = [_dot_nt(vn, ds.astype(MM_DTYPE)) for vn, ds in zip(v_new, dss)]
        d_gam = [_sum_all(ds * s) for ds, s in zip(dss, ss)]
        dq, dk, dv, dgb = vjp(([d.astype(F32) for d in dvns], d_w, d_qk, d_qd, d_kd, d_gam))
        for i, (r, h) in enumerate(units):
            rows = chunk_rows[r]
            for part, d in enumerate((dq, dk, dv)):
                dqkv_ref[rows, part * 512 + h * 128:part * 512 + (h + 1) * 128] = d[i]
        for r, rows in enumerate(chunk_rows):
            dgb_ref[rows, :] = dgb[r]

    tokp = lambda j: pl.BlockSpec((tm, 512), lambda i: (i, j))
    st4 = pl.BlockSpec((r_per, GDN_HEADS, 128, 128), lambda i: (i, 0, 0, 0))
    dqkv, dgb = pl.pallas_call(
        prep_body, name="gdn_prep_bwd", grid=(n // tm,),
        in_specs=[tokp(0), tokp(1), tokp(2), pl.BlockSpec((tm, 128), lambda i: (i, 0)),
                  pl.BlockSpec((r_per, GDN_HEADS, CHUNK, CHUNK), lambda i: (i, 0, 0, 0)), st4, st4,
                  tokp(0), tokp(0), tokp(0)],
        out_specs=[pl.BlockSpec((tm, 1536), lambda i: (i, 0)), pl.BlockSpec((tm, 128), lambda i: (i, 0))],
        out_shape=[jax.ShapeDtypeStruct((n, 1536), F32), jax.ShapeDtypeStruct((n, 128), F32)],
        compiler_params=_params("parallel"),
    )(qkv, qkv, qkv, gb, tinv, st.reshape(bl * nc, GDN_HEADS, 128, 128), dst.reshape(bl * nc, GDN_HEADS, 128, 128),
      dvn.reshape(n, 512), d_o.reshape(n, 512), vn.reshape(n, 512))
    return dqkv, dog.reshape(n, 512), dgb, dnw


def _out_block(x2, tgt2, ya, yb, g1p3, wo, lnw, lnb, seq, tm=256):
    n = x2.shape[0]
    tpe = seq // tm
    bl = n // seq

    def body(x_ref, t_ref, ya_ref, yb_ref, g_ref, wo_ref, lnw_ref, lnb_ref,
             dz_ref, dya_ref, dyb_ref, dwo_ref, dg_ref, glw_ref, glb_ref, loss_ref):
        i = pl.program_id(0)

        @pl.when(i == 0)
        def _():
            dwo_ref[...] = jnp.zeros_like(dwo_ref)
            glw_ref[...] = jnp.zeros_like(glw_ref)
            glb_ref[...] = jnp.zeros_like(glb_ref)
            loss_ref[...] = jnp.zeros_like(loss_ref)

        @pl.when(i % tpe == 0)
        def _():
            dg_ref[...] = jnp.zeros_like(dg_ref)

        ya16 = ya_ref[...].astype(wo.dtype)
        yb16 = yb_ref[...].astype(wo.dtype)
        wa = wo_ref[0:GLA_WIDTH, :]
        wb = wo_ref[GLA_WIDTH:, :]
        y = _dot(ya16, wa) + _dot(yb16, wb)
        g1p = g_ref[0]
        z = ALPHA * x_ref[...] + g1p * y
        mu = jnp.mean(z, axis=-1, keepdims=True)
        zc = z - mu
        rstd = lax.rsqrt(jnp.mean(zc * zc, axis=-1, keepdims=True) + LN_EPS)
        xhat = zc * rstd
        diff = xhat * lnw_ref[...] + lnb_ref[...] - t_ref[...]
        loss_ref[...] += (0.5 / D_MODEL) * jnp.sum(jnp.sum(diff * diff, axis=-1, keepdims=True), axis=0, keepdims=True)
        dout = diff * (1.0 / D_MODEL)
        glw_ref[...] += jnp.sum(dout * xhat, axis=0, keepdims=True)
        glb_ref[...] += jnp.sum(dout, axis=0, keepdims=True)
        dxh = dout * lnw_ref[...]
        dz = rstd * (dxh - jnp.mean(dxh, axis=-1, keepdims=True)
                     - xhat * jnp.mean(dxh * xhat, axis=-1, keepdims=True))
        dz_ref[...] = dz
        dg_ref[0] += jnp.sum(dz * y, axis=0, keepdims=True)
        dy = (g1p * dz).astype(wo.dtype)
        dya_ref[...] = _dot_nt(dy, wa)
        dyb_ref[...] = _dot_nt(dy, wb)
        dwo_ref[0:GLA_WIDTH, :] += _dot_tn(ya16, dy)
        dwo_ref[GLA_WIDTH:, :] += _dot_tn(yb16, dy)

    row = lambda i: (i, 0)
    const = lambda i: (0, 0)
    per_ex = pl.BlockSpec((1, 1, D_MODEL), lambda i: (i // tpe, 0, 0))
    return pl.pallas_call(
        body, name="out_block", grid=(n // tm,),
        in_specs=[pl.BlockSpec((tm, D_MODEL), row), pl.BlockSpec((tm, D_MODEL), row),
                  pl.BlockSpec((tm, 512), row), pl.BlockSpec((tm, 512), row), per_ex,
                  pl.BlockSpec((D_MODEL, D_MODEL), const), pl.BlockSpec((1, D_MODEL), const),
                  pl.BlockSpec((1, D_MODEL), const)],
        out_specs=[pl.BlockSpec((tm, D_MODEL), row), pl.BlockSpec((tm, 512), row), pl.BlockSpec((tm, 512), row),
                   pl.BlockSpec((D_MODEL, D_MODEL), const), per_ex,
                   pl.BlockSpec((1, D_MODEL), const), pl.BlockSpec((1, D_MODEL), const),
                   pl.BlockSpec((1, 1), const)],
        out_shape=[jax.ShapeDtypeStruct((n, D_MODEL), F32), jax.ShapeDtypeStruct((n, 512), F32),
                   jax.ShapeDtypeStruct((n, 512), F32), jax.ShapeDtypeStruct((D_MODEL, D_MODEL), F32),
                   jax.ShapeDtypeStruct((bl, 1, D_MODEL), F32), jax.ShapeDtypeStruct((1, D_MODEL), F32),
                   jax.ShapeDtypeStruct((1, D_MODEL), F32), jax.ShapeDtypeStruct((1, 1), F32)],
        compiler_params=_params("arbitrary"),
    )(x2, tgt2, ya, yb, g1p3, wo, lnw, lnb)


def _proj_bwd_x(ds, ws, x2, dz, sc3, seq, tm=256):
    n = x2.shape[0]
    tpe = seq // tm
    bl = n // seq

    def body(da_ref, db_ref, dc_ref, dd1_ref, dd2_ref, wa_ref, wb_ref, wc_ref, wd_ref, x_ref, dz_ref, sc_ref,
             gx_ref, dsh_ref, dsc_ref):
        i = pl.program_id(0)

        @pl.when(i % tpe == 0)
        def _():
            dsh_ref[...] = jnp.zeros_like(dsh_ref)
            dsc_ref[...] = jnp.zeros_like(dsc_ref)

        cdt = ws[0].dtype
        dh = _dot(da_ref[...].astype(cdt), wa_ref[...])
        dh += _dot(db_ref[...].astype(cdt), wb_ref[...])
        dh += _dot(dc_ref[...].astype(cdt), wc_ref[...])
        dh += _dot((dd1_ref[...] + dd2_ref[...]).astype(cdt), wd_ref[...])
        gx_ref[...] = dh * sc_ref[0] + ALPHA * dz_ref[...]
        dsh_ref[0] += jnp.sum(dh, axis=0, keepdims=True)
        dsc_ref[0] += jnp.sum(dh * x_ref[...], axis=0, keepdims=True)

    row = lambda i: (i, 0)
    const = lambda i: (0, 0)
    per_ex = pl.BlockSpec((1, 1, D_MODEL), lambda i: (i // tpe, 0, 0))
    da, db, dc, (dd1, dd2) = ds
    return pl.pallas_call(
        body, name="proj_bwd_x", grid=(n // tm,),
        in_specs=[pl.BlockSpec((tm, d.shape[1]), row) for d in (da, db, dc, dd1, dd2)]
        + [pl.BlockSpec(w.shape, const) for w in ws]
        + [pl.BlockSpec((tm, D_MODEL), row), pl.BlockSpec((tm, D_MODEL), row), per_ex],
        out_specs=[pl.BlockSpec((tm, D_MODEL), row), per_ex, per_ex],
        out_shape=[jax.ShapeDtypeStruct((n, D_MODEL), F32), jax.ShapeDtypeStruct((bl, 1, D_MODEL), F32),
                   jax.ShapeDtypeStruct((bl, 1, D_MODEL), F32)],
        compiler_params=_params("arbitrary"),
    )(da, db, dc, dd1, dd2, *ws, x2, dz, sc3)


def _proj_bwd_w(x2, sc3, sh3, ds, seq, cdt, name, tm=256):
    n = x2.shape[0]
    tpe = seq // tm
    flat, groups = [], []
    for d in ds:
        parts = d if isinstance(d, tuple) else (d,)
        groups.append(len(parts))
        flat.extend(parts)
    nin = len(flat)

    def body(x_ref, sc_ref, sh_ref, *refs):
        i = pl.program_id(0)
        outs = refs[nin:]

        @pl.when(i == 0)
        def _():
            for o in outs:
                o[...] = jnp.zeros_like(o)

        h = (x_ref[...] * sc_ref[0] + sh_ref[0]).astype(cdt)
        pos = 0
        for o, cnt in zip(outs, groups):
            d = refs[pos][...]
            for extra in refs[pos + 1:pos + cnt]:
                d = d + extra[...]
            pos += cnt
            o[...] += _dot_tn(d.astype(cdt), h)

    row = lambda i: (i, 0)
    const = lambda i: (0, 0)
    per_ex = pl.BlockSpec((1, 1, D_MODEL), lambda i: (i // tpe, 0, 0))
    widths = [(d[0] if isinstance(d, tuple) else d).shape[1] for d in ds]
    return pl.pallas_call(
        body, name=name, grid=(n // tm,),
        in_specs=[pl.BlockSpec((tm, D_MODEL), row), per_ex, per_ex]
        + [pl.BlockSpec((tm, d.shape[1]), row) for d in flat],
        out_specs=[pl.BlockSpec((w, D_MODEL), const) for w in widths],
        out_shape=[jax.ShapeDtypeStruct((w, D_MODEL), F32) for w in widths],
        compiler_params=_params("arbitrary"),
    )(x2, sc3, sh3, *flat)


def _mod_block(c_all, w_ada_sh, b_blk):
    def body(c_ref, w_ref, b_ref, o_ref):
        o_ref[...] = _dot(c_ref[...], w_ref[...]) + b_ref[...]

    return pl.pallas_call(
        body, name="mod_block",
        out_shape=jax.ShapeDtypeStruct((c_all.shape[0], w_ada_sh.shape[1]), F32),
        compiler_params=pltpu.CompilerParams(vmem_limit_bytes=VMEM_LIMIT),
    )(c_all, w_ada_sh, b_blk)


def _ada_grads(c_all, dmod_all, dmod_blk):
    def body(c_ref, da_ref, db_ref, gw_ref, gb_ref):
        gw_ref[...] = _dot_tn(c_ref[...], db_ref[...])
        gb_ref[...] = jnp.sum(da_ref[...], axis=0, keepdims=True)

    return pl.pallas_call(
        body, name="ada_grads",
        out_shape=[jax.ShapeDtypeStruct((c_all.shape[1], dmod_blk.shape[1]), F32),
                   jax.ShapeDtypeStruct((1, dmod_all.shape[1]), F32)],
        compiler_params=pltpu.CompilerParams(vmem_limit_bytes=VMEM_LIMIT),
    )(c_all, dmod_all, dmod_blk)


def _sum_leading(parts, name):
    def body(p_ref, o_ref):
        acc = p_ref[0]
        for d in range(1, parts.shape[0]):
            acc = acc + p_ref[d]
        o_ref[...] = acc

    return pl.pallas_call(
        body, name=name, out_shape=jax.ShapeDtypeStruct(parts.shape[1:], F32),
        compiler_params=pltpu.CompilerParams(vmem_limit_bytes=VMEM_LIMIT),
    )(parts)


ELEMENTWISE_BLOCK_BYTES = 2 * 1024 * 1024


def _tile2d(rows, cols, row_align=8):
    if rows * cols * 4 <= ELEMENTWISE_BLOCK_BYTES:
        return rows, cols
    fits = [t for t in range(row_align, rows, row_align) if rows % t == 0 and t * cols * 4 <= ELEMENTWISE_BLOCK_BYTES]
    if fits:
        return fits[-1], cols
    fits = [t for t in range(128, cols, 128) if cols % t == 0 and rows * t * 4 <= ELEMENTWISE_BLOCK_BYTES]
    assert fits, (rows, cols)
    return rows, fits[-1]


def _add_n(arrs, name, out_dtypes=(F32,)):
    rows, cols = arrs[0].shape
    narrow = any(jnp.dtype(dt).itemsize < 4 for dt in tuple(out_dtypes) + tuple(a.dtype for a in arrs))
    tr, tc = _tile2d(rows, cols, 16 if narrow else 8)
    n_in = len(arrs)

    def body(*refs):
        acc = refs[0][...].astype(F32)
        for r in refs[1:n_in]:
            acc = acc + r[...].astype(F32)
        for o in refs[n_in:]:
            o[...] = acc.astype(o.dtype)

    spec = pl.BlockSpec((tr, tc), lambda i, j: (i, j))
    return pl.pallas_call(
        body, name=name, grid=(rows // tr, cols // tc), in_specs=[spec] * n_in, out_specs=[spec] * len(out_dtypes),
        out_shape=[jax.ShapeDtypeStruct((rows, cols), dt) for dt in out_dtypes],
        compiler_params=_params("parallel", "parallel"),
    )(*arrs)


def _chip_sum_blocks(a, b, per, blocks, name, chunk=128):
    rows, cols = a.shape
    padded = -(-per // 16) * 16
    assert rows >= (blocks - 1) * per + padded, (rows, per, blocks)

    def body(a_ref, b_ref, o_ref, o16_ref):
        for j in range(blocks):
            for r0 in range(0, padded, chunk):
                n_rows = min(chunk, padded - r0)
                src = pl.ds(j * per + r0, n_rows)
                s = a_ref[src, :] + b_ref[src, :]
                if per - r0 < n_rows:
                    s = jnp.where(_iota((n_rows, 1), 0) < per - r0, s, 0.0)
                o_ref[j, r0:r0 + n_rows, :] = s
                o16_ref[j, r0:r0 + n_rows, :] = s.astype(BF16)

    return pl.pallas_call(
        body, name=name,
        out_shape=[jax.ShapeDtypeStruct((blocks, padded, cols), F32), jax.ShapeDtypeStruct((blocks, padded, cols), BF16)],
        compiler_params=pltpu.CompilerParams(vmem_limit_bytes=VMEM_LIMIT),
    )(a, b)


GRAD_PAD_ROWS = 16


def _adamw(w, g, m, v, name):
    rows, cols = w.shape
    tr, tc = _tile2d(rows, cols)
    c1 = 1.0 / (1.0 - ADAM_B1 ** ADAM_STEP)
    c2 = 1.0 / (1.0 - ADAM_B2 ** ADAM_STEP)

    def body(w_ref, g_ref, m_ref, v_ref, d_ref, nm_ref, nv_ref):
        gg = g_ref[...]
        nm = ADAM_B1 * m_ref[...] + (1.0 - ADAM_B1) * gg
        nv = ADAM_B2 * v_ref[...] + (1.0 - ADAM_B2) * (gg * gg)
        nm_ref[...] = nm
        nv_ref[...] = nv
        d_ref[...] = -ADAM_LR * ((nm * c1) / (jnp.sqrt(nv * c2) + ADAM_EPS) + ADAM_WD * w_ref[...])

    spec = pl.BlockSpec((tr, tc), lambda i, j: (i, j))
    shp = jax.ShapeDtypeStruct((rows, cols), F32)
    return pl.pallas_call(
        body, name=name, grid=(rows // tr, cols // tc), in_specs=[spec] * 4, out_specs=[spec] * 3,
        out_shape=[shp, shp, shp], compiler_params=_params("parallel", "parallel"),
    )(w, g, m, v)


def _coords():
    return lax.axis_index("x"), lax.axis_index("y"), lax.axis_index("c")


def _all_gather8(blk, name):
    m_per, n = blk.shape

    def body(x_ref, out_ref, send_sems, recv_sems, local_sem):
        x, y, c = _coords()
        me, sibling = (x, y, c), (x, y, 1 - c)
        chips = [(1 - x, y), (x, 1 - y), (1 - x, 1 - y)]

        def rows(px, py, pc):
            return out_ref.at[pl.ds((4 * px + 2 * py + pc) * m_per, m_per), :]

        def copy(k, block, to, src=None):
            return pltpu.make_async_remote_copy(
                src_ref=rows(*block) if src is None else src, dst_ref=rows(*block),
                send_sem=send_sems.at[k], recv_sem=recv_sems.at[k], device_id=to, device_id_type=MESH)

        mine = pltpu.make_async_copy(x_ref, rows(*me), local_sem)
        mine.start()
        first = [copy(0, me, sibling, src=x_ref)]
        first += [copy(1 + j, me, (*chip, c), src=x_ref) for j, chip in enumerate(chips)]
        for cp in first:
            cp.start()
        passed = [copy(4 + j, (*chip, c), sibling) for j, chip in enumerate(chips)]
        for j, chip in enumerate(chips):
            copy(1 + j, (*chip, c), me).wait_recv()
            passed[j].start()
        copy(0, sibling, me).wait_recv()
        for j, chip in enumerate(chips):
            copy(4 + j, (*chip, 1 - c), me).wait_recv()
        for cp in first + passed:
            cp.wait_send()
        mine.wait()

    return pl.pallas_call(
        body, name=name,
        out_shape=jax.ShapeDtypeStruct((8 * m_per, n), blk.dtype),
        in_specs=[pl.BlockSpec(memory_space=pltpu.VMEM)],
        out_specs=pl.BlockSpec(memory_space=pltpu.VMEM),
        scratch_shapes=[pltpu.SemaphoreType.DMA((7,)), pltpu.SemaphoreType.DMA((7,)), pltpu.SemaphoreType.DMA],
        compiler_params=pltpu.CompilerParams(vmem_limit_bytes=VMEM_LIMIT),
    )(blk)


def _chip_gather(shards, split, name):
    k_arr = len(shards)

    def body(*refs):
        srcs, dsts = refs[:k_arr], refs[k_arr:2 * k_arr]
        send_sems, recv_sems, fwd_send_sems, fwd_recv_sems, local_sems = refs[2 * k_arr:]
        x, y, c = _coords()
        peers = [(1 - x, y, c), (x, 1 - y, c), (1 - x, 1 - y, c)]
        sibling = (x, y, 1 - c)
        me_chip = 2 * x + y

        def part(ref, a, core):
            if not split[a]:
                return ref
            half = shards[a].shape[1] // 2
            return ref.at[:, pl.ds(core * half, half)]

        def ici(a, j, src_chip, dst_dev):
            return pltpu.make_async_remote_copy(
                src_ref=part(srcs[a], a, c), dst_ref=part(dsts[a].at[src_chip], a, c),
                send_sem=send_sems.at[a, j], recv_sem=recv_sems.at[a, j], device_id=dst_dev, device_id_type=MESH)

        def d2d(a, j, src_chip, core):
            return pltpu.make_async_remote_copy(
                src_ref=part(dsts[a].at[src_chip], a, core), dst_ref=part(dsts[a].at[src_chip], a, core),
                send_sem=fwd_send_sems.at[a, j], recv_sem=fwd_recv_sems.at[a, j],
                device_id=sibling, device_id_type=MESH)

        local = [pltpu.make_async_copy(srcs[a], dsts[a].at[me_chip], local_sems.at[a]) for a in range(k_arr)]
        for cp in local:
            cp.start()
        sends = [ici(a, j, me_chip, peer) for a in range(k_arr) for j, peer in enumerate(peers)]
        for cp in sends:
            cp.start()
        forwards = []
        for a in range(k_arr):
            for j, peer in enumerate(peers):
                peer_chip = 2 * peer[0] + peer[1]
                ici(a, j, peer_chip, peer).wait_recv()
                if split[a]:
                    forwards.append(d2d(a, j, peer_chip, c))
                    forwards[-1].start()
        for a in range(k_arr):
            for j, peer in enumerate(peers):
                if split[a]:
                    d2d(a, j, 2 * peer[0] + peer[1], 1 - c).wait_recv()
        for cp in sends + forwards:
            cp.wait_send()
        for cp in local:
            cp.wait()

    any_spec = pl.BlockSpec(memory_space=pl.ANY)
    return pl.pallas_call(
        body, name=name,
        out_shape=[jax.ShapeDtypeStruct((4,) + s.shape, s.dtype) for s in shards],
        in_specs=[any_spec] * k_arr, out_specs=[any_spec] * k_arr,
        scratch_shapes=[pltpu.SemaphoreType.DMA((k_arr, 3))] * 4 + [pltpu.SemaphoreType.DMA((k_arr,))],
    )(*shards)


def _chip_scatter(pieces, name):
    k_arr = len(pieces)

    def body(*refs):
        srcs, dsts = refs[:k_arr], refs[k_arr:2 * k_arr]
        send_sems, recv_sems = refs[2 * k_arr:]
        x, y, c = _coords()
        peers = [(1 - x, y, c), (x, 1 - y, c), (1 - x, 1 - y, c)]
        copies = []
        for a in range(k_arr):
            for j, peer in enumerate(peers):
                copies.append(pltpu.make_async_remote_copy(
                    src_ref=srcs[a].at[2 * peer[0] + peer[1]], dst_ref=dsts[a].at[j],
                    send_sem=send_sems.at[a, j], recv_sem=recv_sems.at[a, j], device_id=peer, device_id_type=MESH))
        for cp in copies:
            cp.start()
        for cp in copies:
            cp.wait_recv()
        for cp in copies:
            cp.wait_send()

    any_spec = pl.BlockSpec(memory_space=pl.ANY)
    return pl.pallas_call(
        body, name=name,
        out_shape=[jax.ShapeDtypeStruct((3,) + p.shape[1:], p.dtype) for p in pieces],
        in_specs=[any_spec] * k_arr, out_specs=[any_spec] * k_arr,
        scratch_shapes=[pltpu.SemaphoreType.DMA((k_arr, 3)), pltpu.SemaphoreType.DMA((k_arr, 3))],
    )(*pieces)


def _sibling_swap(arrs, name):
    k_arr = len(arrs)

    def body(*refs):
        srcs, dsts = refs[:k_arr], refs[k_arr:2 * k_arr]
        send_sems, recv_sems = refs[2 * k_arr:]
        x, y, c = _coords()
        copies = [pltpu.make_async_remote_copy(
            src_ref=srcs[a], dst_ref=dsts[a], send_sem=send_sems.at[a], recv_sem=recv_sems.at[a],
            device_id=(x, y, 1 - c), device_id_type=MESH) for a in range(k_arr)]
        for cp in copies:
            cp.start()
        for cp in copies:
            cp.wait_recv()
        for cp in copies:
            cp.wait_send()

    any_spec = pl.BlockSpec(memory_space=pl.ANY)
    return pl.pallas_call(
        body, name=name,
        out_shape=[jax.ShapeDtypeStruct(a.shape, a.dtype) for a in arrs],
        in_specs=[any_spec] * k_arr, out_specs=[any_spec] * k_arr,
        scratch_shapes=[pltpu.SemaphoreType.DMA((k_arr,)), pltpu.SemaphoreType.DMA((k_arr,))],
    )(*arrs)


def _split_w_in(w_in_t):
    wa = jnp.concatenate([w_in_t[0:1024], w_in_t[1040:1552]], axis=0)
    wb = w_in_t[1552:3088]
    wc = w_in_t[3096:3608]
    wd = jnp.concatenate([w_in_t[1024:1040], w_in_t[3088:3096],
                          jnp.zeros((128 - SMALL_USED, w_in_t.shape[1]), w_in_t.dtype)], axis=0)
    return wa, wb, wc, wd


def _merge_dw_in(dwa, dwb, dwc, dwd):
    return jnp.concatenate([dwa[0:1024], dwd[0:GLA_RANK], dwa[1024:1536], dwb, dwd[GLA_RANK:SMALL_USED], dwc,
                            jnp.zeros((GRAD_PAD_ROWS, dwa.shape[1]), dwa.dtype)], axis=0)


def _local_step(x, mod, w_in16, w_out16, gla_wg, gla_bg, gla_nw, conv_w, a_log, dt_bias, gdn_nw, ln_w, ln_b, tgt):
    bl, seq, _ = x.shape
    n = bl * seq
    x2 = x.reshape(n, D_MODEL)
    tgt2 = tgt.reshape(n, D_MODEL)
    sh3 = mod[:, None, 0:D_MODEL]
    sc3 = 1.0 + mod[:, None, D_MODEL:2 * D_MODEL]
    g1p3 = 1.0 + mod[:, None, 2 * D_MODEL:]
    ws = _split_w_in(w_in16)
    wg = jnp.concatenate([gla_wg, jnp.zeros((128 - GLA_RANK, GLA_QK), F32)], axis=0)
    cw8 = jnp.concatenate([conv_w, jnp.zeros((8 - CONV_K, conv_w.shape[1]), F32)], axis=0)
    alog_v = jnp.zeros((1, 128), F32).at[:, LANE_A:LANE_A + GDN_HEADS].set(a_log)
    dtb_v = jnp.zeros((1, 128), F32).at[:, LANE_A:LANE_A + GDN_HEADS].set(dt_bias)

    pa, pb, pc, pd = _proj_fwd(x2, sc3, sh3, ws, seq)
    ya, st_a = _gla_fwd(pa, pd, wg, gla_bg, gla_nw, bl, seq)
    qkv, gb, conv_out = _gdn_pre_fwd(pb, pd, cw8, alog_v, dtb_v, bl, seq)
    yb, st_b = _gdn_fwd(qkv, gb, pc, gdn_nw, bl, seq)
    dz, dya, dyb, d_wo, d_gate, d_lnw, d_lnb, loss = _out_block(x2, tgt2, ya, yb, g1p3, w_out16, ln_w, ln_b, seq)
    da, dd1, d_wg, d_bg, d_nwa = _gla_bwd(pa, pd, st_a, dya, wg, gla_bg, gla_nw, bl, seq)
    dqkv, dc, dgb, d_nwb = _gdn_bwd(qkv, gb, pc, st_b, dyb, gdn_nw, bl, seq)
    db, dd2, d_cw8, d_alog, d_dtb = _gdn_pre_bwd(pb, conv_out, pd, dqkv, dgb, cw8, alog_v, dtb_v, bl, seq)
    gx, d_sh, d_sc = _proj_bwd_x((da, db, dc, (dd1, dd2)), ws, x2, dz, sc3, seq)
    (dwa,) = _proj_bwd_w(x2, sc3, sh3, [da], seq, w_in16.dtype, "proj_bwd_w_a")
    dwb, dwc, dwd = _proj_bwd_w(x2, sc3, sh3, [db, dc, (dd1, dd2)], seq, w_in16.dtype, "proj_bwd_w_bcd")
    grads = dict(
        w_in=_merge_dw_in(dwa, dwb, dwc, dwd),
        w_out=d_wo,
        gla_w_gate_up=d_wg[0:GLA_RANK, :],
        gla_b_gate=d_bg,
        gla_norm_w=d_nwa,
        gdn_conv_w=d_cw8[0:CONV_K, :],
        gdn_a_log=d_alog[:, LANE_A:LANE_A + GDN_HEADS],
        gdn_dt_bias=d_dtb[:, LANE_A:LANE_A + GDN_HEADS],
        gdn_norm_w=d_nwb,
        ln_w=d_lnw,
        ln_b=d_lnb,
        mod=jnp.concatenate([d_sh[:, 0, :], d_sc[:, 0, :], d_gate[:, 0, :]], axis=1),
    )
    return loss, gx.reshape(bl, seq, D_MODEL), grads


_SMALL = (("gla_b_gate", 256), ("gla_norm_w", 128), ("gdn_a_log", 4), ("gdn_dt_bias", 4), ("gdn_norm_w", 128),
          ("ln_w", 1024), ("ln_b", 1024), ("gla_w_gate_up", 16 * 256), ("gdn_conv_w", 4 * 1536), ("loss", 1),
          ("mod", 2 * 3072))


def _pack_small(grads):
    flat = jnp.concatenate([grads[k].reshape(-1) for k, _ in _SMALL])
    total = sum(sz for _, sz in _SMALL)
    rows = -(-total // 1024) * 8
    return jnp.concatenate([flat, jnp.zeros((rows * 128 - total,), F32)]).reshape(rows, 128)


def _unpack_small(flat):
    out, pos = {}, 0
    for k, sz in _SMALL:
        out[k] = flat[pos:pos + sz]
        pos += sz
    return out


def kernel(x, c, w_ada, b_ada, w_in, gla_w_gate_up, gla_b_gate, gla_norm_w, gdn_conv_w, gdn_a_log, gdn_dt_bias, gdn_norm_w, w_out, ln_w, ln_b, loss_target, m_w_ada, m_b_ada, m_w_in, m_gla_w_gate_up, m_gla_b_gate, m_gla_norm_w, m_gdn_conv_w, m_gdn_a_log, m_gdn_dt_bias, m_gdn_norm_w, m_w_out, m_ln_w, m_ln_b, v_w_ada, v_b_ada, v_w_in, v_gla_w_gate_up, v_gla_b_gate, v_gla_norm_w, v_gdn_conv_w, v_gdn_a_log, v_gdn_dt_bias, v_gdn_norm_w, v_w_out, v_ln_w, v_ln_b):
    ix, iy, ic = _coords()
    chip = 2 * ix + iy
    dev = 4 * ix + 2 * iy + ic
    bl = x.shape[0]
    ndev = 8

    c_all = _all_gather8(c.reshape(8, -1), "gather_c").reshape(ndev * bl, D_MODEL)
    ada_cols = w_ada.shape[2]
    b_blk = lax.dynamic_slice_in_dim(b_ada, chip * ada_cols, ada_cols, axis=1)
    mod_blk = _mod_block(c_all, w_ada[0], b_blk)
    mod_g = _all_gather8(mod_blk, "gather_mod").reshape(ndev, ndev * bl, ada_cols)
    mod_all = jnp.concatenate([mod_g[2 * j] for j in range(4)], axis=1)
    mod = lax.dynamic_slice_in_dim(mod_all, dev * bl, bl, axis=0)

    w_in_g, w_out_g, wg_g, cw_g = _chip_gather(
        [jnp.transpose(w_in[0]).astype(BF16), w_out[0].astype(BF16), gla_w_gate_up[0], gdn_conv_w[0]],
        [True, True, False, False], "gather_weights")
    w_in16 = w_in_g.reshape(IN_COLS, D_MODEL)
    w_out16 = w_out_g.reshape(D_MODEL, D_MODEL)
    gla_wg = jnp.concatenate([wg_g[j] for j in range(4)], axis=1)
    conv_w = jnp.concatenate([cw_g[j] for j in range(4)], axis=1)

    loss, grad_x, gr = _local_step(x, mod, w_in16, w_out16, gla_wg, gla_b_gate, gla_norm_w, conv_w,
                                   gdn_a_log, gdn_dt_bias, gdn_norm_w, ln_w, ln_b, loss_target)

    gr["loss"] = loss
    packed = _pack_small(gr)
    prow = packed.shape[0]
    gathered = _all_gather8(packed, "gather_small").reshape(ndev, prow, 128)
    small = _unpack_small(_sum_leading(gathered, "sum_small").reshape(-1))
    loss = small["loss"][0]
    mod_rows = gathered.reshape(ndev, prow * 128)[:, sum(sz for _, sz in _SMALL[:-1]):][:, :bl * 3 * D_MODEL]
    dmod_all = mod_rows.reshape(ndev * bl, 3 * D_MODEL)
    dmod_blk = lax.dynamic_slice_in_dim(dmod_all, chip * ada_cols, ada_cols, axis=1)
    g_w_ada, g_b_ada = _ada_grads(c_all, dmod_all, dmod_blk)
    wg_cols = gla_w_gate_up.shape[2]
    g_wg = lax.dynamic_slice_in_dim(small["gla_w_gate_up"].reshape(GLA_RANK, GLA_QK), chip * wg_cols, wg_cols, axis=1)
    cw_cols = gdn_conv_w.shape[2]
    g_cw = lax.dynamic_slice_in_dim(small["gdn_conv_w"].reshape(CONV_K, 3 * GDN_WIDTH), chip * cw_cols, cw_cols, axis=1)

    in_feats = w_in.shape[2]
    out_rows = w_out.shape[1]
    p_in = gr["w_in"]
    p_out = gr["w_out"].reshape(4, out_rows, D_MODEL)
    h_in, h_out = D_MODEL // 2, out_rows // 2
    mine_in = lax.dynamic_slice_in_dim(p_in, ic * h_in, h_in, axis=1)
    mine_out = lax.dynamic_slice_in_dim(p_out, ic * h_out, h_out, axis=1)
    theirs_in = lax.dynamic_slice_in_dim(p_in, (1 - ic) * h_in, h_in, axis=1)
    theirs_out = lax.dynamic_slice_in_dim(p_out, (1 - ic) * h_out, h_out, axis=1)
    got_in, got_out = _sibling_swap([theirs_in, theirs_out], "swap_halves")
    chip_in, chip_in16 = _chip_sum_blocks(mine_in, got_in, in_feats, 4, "chip_sum_in")
    chip_out, chip_out16 = _add_n([mine_out.reshape(4 * h_out, D_MODEL), got_out.reshape(4 * h_out, D_MODEL)],
                                  "chip_sum_out", (F32, BF16))
    chip_out = chip_out.reshape(4, h_out, D_MODEL)
    rs_in, rs_out = _chip_scatter([chip_in16, chip_out16.reshape(4, h_out, D_MODEL)], "scatter_grads")
    own_in = lax.dynamic_index_in_dim(chip_in, chip, axis=0, keepdims=False)
    own_out = lax.dynamic_index_in_dim(chip_out, chip, axis=0, keepdims=False)
    (half_in,) = _add_n([own_in, rs_in[0], rs_in[1], rs_in[2]], "reduce_in")
    (half_out,) = _add_n([own_out, rs_out[0], rs_out[1], rs_out[2]], "reduce_out")
    sib_in, sib_out = _sibling_swap([half_in, half_out], "swap_result")
    g_w_in_t = jnp.where(ic == 0, jnp.concatenate([half_in, sib_in], axis=1),
                         jnp.concatenate([sib_in, half_in], axis=1))[0:in_feats]
    g_w_out = jnp.where(ic == 0, jnp.concatenate([half_out, sib_out], axis=0),
                        jnp.concatenate([sib_out, half_out], axis=0))

    grads = dict(
        w_ada=g_w_ada[None], b_ada=g_b_ada, w_in=g_w_in_t, gla_w_gate_up=g_wg[None],
        gla_b_gate=small["gla_b_gate"].reshape(1, -1), gla_norm_w=small["gla_norm_w"].reshape(1, -1),
        gdn_conv_w=g_cw[None], gdn_a_log=small["gdn_a_log"].reshape(1, -1),
        gdn_dt_bias=small["gdn_dt_bias"].reshape(1, -1), gdn_norm_w=small["gdn_norm_w"].reshape(1, -1),
        w_out=g_w_out[None], ln_w=small["ln_w"].reshape(1, -1), ln_b=small["ln_b"].reshape(1, -1))
    weights = dict(w_ada=w_ada, b_ada=b_ada, w_in=w_in, gla_w_gate_up=gla_w_gate_up, gla_b_gate=gla_b_gate,
                   gla_norm_w=gla_norm_w, gdn_conv_w=gdn_conv_w, gdn_a_log=gdn_a_log, gdn_dt_bias=gdn_dt_bias,
                   gdn_norm_w=gdn_norm_w, w_out=w_out, ln_w=ln_w, ln_b=ln_b)
    m_in = dict(w_ada=m_w_ada, b_ada=m_b_ada, w_in=m_w_in, gla_w_gate_up=m_gla_w_gate_up, gla_b_gate=m_gla_b_gate,
                gla_norm_w=m_gla_norm_w, gdn_conv_w=m_gdn_conv_w, gdn_a_log=m_gdn_a_log, gdn_dt_bias=m_gdn_dt_bias,
                gdn_norm_w=m_gdn_norm_w, w_out=m_w_out, ln_w=m_ln_w, ln_b=m_ln_b)
    v_in = dict(w_ada=v_w_ada, b_ada=v_b_ada, w_in=v_w_in, gla_w_gate_up=v_gla_w_gate_up, gla_b_gate=v_gla_b_gate,
                gla_norm_w=v_gla_norm_w, gdn_conv_w=v_gdn_conv_w, gdn_a_log=v_gdn_a_log, gdn_dt_bias=v_gdn_dt_bias,
                gdn_norm_w=v_gdn_norm_w, w_out=v_w_out, ln_w=v_ln_w, ln_b=v_ln_b)
    names = list(weights)
    delta, new_m, new_v = {}, {}, {}
    for nm in names:
        shp = weights[nm].shape
        if nm == "w_in":
            to2d = lambda t: jnp.transpose(t[0])
            from2d = lambda t: jnp.transpose(t)[None]
            g2d = grads[nm]
        else:
            to2d = lambda t: t.reshape(-1, shp[-1])
            from2d = lambda t: t.reshape(shp)
            g2d = to2d(grads[nm])
        d, a, b = _adamw(to2d(weights[nm]), g2d, to2d(m_in[nm]), to2d(v_in[nm]), "adamw_" + nm)
        delta[nm], new_m[nm], new_v[nm] = from2d(d), from2d(a), from2d(b)
        grads[nm] = from2d(g2d)
    return (loss, grad_x, *[grads[k] for k in names], *[delta[k] for k in names],
            *[new_m[k] for k in names], *[new_v[k] for k in names])
```

```python
import functools

import jax
import jax.numpy as jnp
from jax import lax
from jax.experimental import pallas as pl
from jax.experimental.pallas import tpu as pltpu

F32 = jnp.float32
BF16 = jnp.bfloat16
HI = lax.Precision.HIGH
INV_PREC = None
MESH = pl.DeviceIdType.MESH

D_MODEL = 1024
GLA_HEADS = 4
GLA_DK = 64
GLA_DV = 128
GLA_QK = 256
GLA_WIDTH = 512
GLA_RANK = 16
GLA_GATE_NORM = 16.0
GDN_HEADS = 4
GDN_DK = 128
GDN_WIDTH = 512
CONV_K = 4
CHUNK = 64
LN_EPS = 1e-5
RMS_EPS = 1e-6
ALPHA = 2.0 ** 0.25
IN_COLS = 3608

LANE_A = GLA_RANK
LANE_B = GLA_RANK + GDN_HEADS
SMALL_USED = GLA_RANK + 2 * GDN_HEADS

ADAM_LR = 0.001
ADAM_B1 = 0.9
ADAM_B2 = 0.999
ADAM_EPS = 1e-08
ADAM_WD = 0.01
ADAM_STEP = 10

VMEM_LIMIT = 56 * 1024 * 1024


def _iota(shape, dim):
    return lax.broadcasted_iota(jnp.int32, shape, dim)


def _dot(a, b, prec=None):
    return lax.dot_general(a, b, (((1,), (0,)), ((), ())), precision=prec, preferred_element_type=F32)


def _dot_nt(a, b, prec=None):
    return lax.dot_general(a, b, (((1,), (1,)), ((), ())), precision=prec, preferred_element_type=F32)


def _dot_tn(a, b, prec=None):
    return lax.dot_general(a, b, (((0,), (0,)), ((), ())), precision=prec, preferred_element_type=F32)


def _log_sigmoid(z):
    return jnp.minimum(z, 0.0) - jnp.log1p(jnp.exp(-jnp.abs(z)))


def _softplus(z):
    return jnp.maximum(z, 0.0) + jnp.log1p(jnp.exp(-jnp.abs(z)))


def _silu(z):
    return z * jax.nn.sigmoid(z)


def _rms_gate(o, nw, og):
    return o * lax.rsqrt(jnp.mean(o * o, axis=-1, keepdims=True) + RMS_EPS) * nw * _silu(og)


def _params(*sem):
    return pltpu.CompilerParams(dimension_semantics=sem, vmem_limit_bytes=VMEM_LIMIT)


GLA_PAIRS = GLA_HEADS // 2


def _gla_chunk(qs, ks, lrs, vs, ogs, ss, wgs, bgs, nw):
    c = qs[0].shape[0]
    n_ep = len(ss)
    n_ex = n_ep // GLA_PAIRS
    n_chunks = len(qs) // n_ep
    pair_units = [(i // n_ep * n_ex + i % n_ep // GLA_PAIRS, i % GLA_PAIRS) for i in range(len(qs))]
    head_units = [(i // GLA_HEADS * GLA_PAIRS + i % GLA_HEADS // 2, i % 2) for i in range(len(vs))]
    row, col = _iota((c, c), 0), _iota((c, c), 1)
    causal = row >= col
    first_half = (_iota((c, 1), 0) < c // 2).astype(F32)
    lane = _iota((1, 128), 1)
    masks = [(lane < GLA_DK).astype(F32), (lane >= GLA_DK).astype(F32)]
    gs = [_log_sigmoid(_dot(lrs[ce], wgs[p]) + bgs[p]) * (1.0 / GLA_GATE_NORM) for ce, p in pair_units]
    bs = [_dot(causal.astype(F32), g, HI) for g in gs]
    b_ref = [jnp.sum(g * first_half, axis=0, keepdims=True) for g in gs]
    b_last = [jnp.sum(g, axis=0, keepdims=True) for g in gs]
    qsc = [q * (GLA_DK ** -0.5) for q in qs]
    qe = [q * jnp.exp(b - br) for q, b, br in zip(qsc, bs, b_ref)]
    ke = [k * jnp.exp(br - b) for k, b, br in zip(ks, bs, b_ref)]
    qb = [q * jnp.exp(b) for q, b in zip(qsc, bs)]
    kd = [k * jnp.exp(bl_ - b) for k, b, bl_ in zip(ks, bs, b_last)]
    decay = [jnp.exp(bl_) for bl_ in b_last]
    att = [jnp.where(causal, _dot_nt(qe[u] * masks[half], ke[u]), 0.0) for u, half in head_units]
    o_intra = [_dot(a, v) for a, v in zip(att, vs)]
    qbm = [qb[u] * masks[half] for u, half in head_units]
    kdm = [kd[u] * masks[half] for u, half in head_units]
    ys = []
    for r in range(n_chunks):
        heads_r = range(r * n_ex * GLA_HEADS, (r + 1) * n_ex * GLA_HEADS)
        o_inter = [_dot_nt(qbm[i], ss[head_units[i][0] - r * n_ep]) for i in heads_r]
        upd = [_dot_tn(vs[i], kdm[i]) for i in heads_r]
        ss = [s * decay[r * n_ep + j] + upd[2 * j] + upd[2 * j + 1] for j, s in enumerate(ss)]
        ys += [_rms_gate(o_intra[i] + oi, nw, ogs[i]) for i, oi in zip(heads_r, o_inter)]
    return ys, ss


def _unit_lower_inverse_chain(a_list):
    c = a_list[0].shape[0]
    eye = (_iota((c, c), 0) == _iota((c, c), 1)).astype(F32)
    ps = [-a for a in a_list]
    ts = [eye + p for p in ps]
    levels = max(c.bit_length() - 2, 0)
    if levels:
        ps = [_dot(p, p, INV_PREC) for p in ps]
    for level in range(levels):
        last = level == levels - 1
        both = [_dot(t if last else jnp.concatenate([t, p], axis=0), p, INV_PREC) for t, p in zip(ts, ps)]
        ts = [t + m[0:c] for t, m in zip(ts, both)]
        if not last:
            ps = [m[c:2 * c] for m in both]
    return ts


@jax.custom_vjp
def _unit_lower_inverse(a_list):
    return _unit_lower_inverse_chain(a_list)


def _unit_lower_inverse_fwd(a_list):
    ts = _unit_lower_inverse_chain(a_list)
    return ts, ts


def _unit_lower_inverse_bwd(ts, dts):
    xs = [_dot_nt(dt, t, INV_PREC) for dt, t in zip(dts, ts)]
    return ([-_dot_tn(t, x, INV_PREC) for t, x in zip(ts, xs)],)


_unit_lower_inverse.defvjp(_unit_lower_inverse_fwd, _unit_lower_inverse_bwd)


@jax.custom_vjp
def _unit_lower_inverse_known(a_list, ts):
    return ts


def _unit_lower_inverse_known_fwd(a_list, ts):
    return ts, ts


def _unit_lower_inverse_known_bwd(ts, dts):
    return _unit_lower_inverse_bwd(ts, dts) + ([jnp.zeros_like(t) for t in ts],)


_unit_lower_inverse_known.defvjp(_unit_lower_inverse_known_fwd, _unit_lower_inverse_known_bwd)


def _gdn_prep_units(qs, ks, vs, gbs, t_known=None):
    c = qs[0].shape[0]
    units = [divmod(i, GDN_HEADS) for i in range(len(qs))]
    row, col = _iota((c, c), 0), _iota((c, c), 1)
    causal, strict = row >= col, row > col
    lane = _iota((1, 128), 1)
    d_alls = [_dot(causal.astype(F32), gb, HI) for gb in gbs]
    g_c, beta_c, d_c = [], [], []
    for r, h in units:
        sel_a = (lane == LANE_A + h).astype(F32)
        g_c.append(jnp.sum(gbs[r] * sel_a, axis=-1, keepdims=True))
        beta_c.append(jnp.sum(gbs[r] * (lane == LANE_B + h).astype(F32), axis=-1, keepdims=True))
        d_c.append(jnp.sum(d_alls[r] * sel_a, axis=-1, keepdims=True))
    d_last = [jnp.sum(g, axis=0, keepdims=True) for g in g_c]
    d_diff = [jnp.broadcast_to(d, (c, c)) - jnp.broadcast_to(d, (c, c)).T for d in d_c]
    decay_mat = [jnp.where(causal, jnp.exp(jnp.where(causal, dd, 0.0)), 0.0) for dd in d_diff]
    kb = [k * b for k, b in zip(ks, beta_c)]
    kbk_qk = [_dot_nt(jnp.concatenate([kbi, q], axis=0), k) for kbi, q, k in zip(kb, qs, ks)]
    a = [jnp.where(strict, m[0:c] * dm, 0.0) for m, dm in zip(kbk_qk, decay_mat)]
    qk = [jnp.where(causal, m[c:2 * c] * dm, 0.0) for m, dm in zip(kbk_qk, decay_mat)]
    t = _unit_lower_inverse(a) if t_known is None else _unit_lower_inverse_known(a, t_known)
    uw = [_dot(ti, jnp.concatenate([v * b, kbi * jnp.exp(d)], axis=1))
          for ti, v, b, kbi, d in zip(t, vs, beta_c, kb, d_c)]
    u = [m[:, 0:128] for m in uw]
    w = [m[:, 128:256] for m in uw]
    q_dec = [q * jnp.exp(d) for q, d in zip(qs, d_c)]
    k_dec = [k * jnp.exp(dl - d) for k, dl, d in zip(ks, d_last, d_c)]
    gamma = [jnp.exp(dl) for dl in d_last]
    return u, w, qk, q_dec, k_dec, gamma, t


def _sum_all(t):
    return jnp.sum(jnp.sum(t, axis=-1, keepdims=True), axis=0, keepdims=True)


def _gdn_pre_elem(ps, ab, alog_v, dtb_v):
    outs = []
    for j, p in enumerate(ps):
        s = _silu(p)
        if j < 2 * GDN_HEADS:
            s = s * lax.rsqrt(jnp.sum(s * s, axis=-1, keepdims=True) + RMS_EPS)
        if j < GDN_HEADS:
            s = s * (GDN_DK ** -0.5)
        outs.append(s)
    lane = _iota((1, 128), 1)
    is_a = (lane >= LANE_A) & (lane < LANE_A + GDN_HEADS)
    is_b = (lane >= LANE_B) & (lane < LANE_B + GDN_HEADS)
    g = -jnp.exp(alog_v) * _softplus(ab + dtb_v)
    gb = jnp.where(is_a, g, jnp.where(is_b, jax.nn.sigmoid(ab), 0.0))
    return tuple(outs) + (gb,)


def _proj_fwd(x2, sc3, sh3, ws, seq, tm=256):
    n = x2.shape[0]
    tpe = seq // tm
    nw = len(ws)

    def body(x_ref, sc_ref, sh_ref, *refs):
        h = (x_ref[...] * sc_ref[0] + sh_ref[0]).astype(ws[0].dtype)
        for w_ref, o_ref in zip(refs[:nw], refs[nw:]):
            o_ref[...] = _dot_nt(h, w_ref[...])

    row = lambda i: (i, 0)
    per_ex = pl.BlockSpec((1, 1, D_MODEL), lambda i: (i // tpe, 0, 0))
    return pl.pallas_call(
        body, name="proj_fwd", grid=(n // tm,),
        in_specs=[pl.BlockSpec((tm, D_MODEL), row), per_ex, per_ex]
        + [pl.BlockSpec(w.shape, lambda i: (0, 0)) for w in ws],
        out_specs=[pl.BlockSpec((tm, w.shape[0]), row) for w in ws],
        out_shape=[jax.ShapeDtypeStruct((n, w.shape[0]), F32) for w in ws],
        compiler_params=_params("parallel"),
    )(x2, sc3, sh3, *ws)


GLA_SCAN_CHUNKS = 4


def _gla_operands(q_ref, k_ref, v_ref, og_ref, lr_ref, wg_ref, bg_ref, bl, r_per):
    chunks = [slice(r * CHUNK, (r + 1) * CHUNK) for r in range(r_per)]
    pair_cols = [slice(p * 128, (p + 1) * 128) for p in range(GLA_PAIRS)]
    head_cols = [slice(h * 128, (h + 1) * 128) for h in range(GLA_HEADS)]
    per_pair = lambda ref: [ref[e, rows, cols] for rows in chunks for e in range(bl) for cols in pair_cols]
    per_head = lambda ref: [ref[e, rows, cols] for rows in chunks for e in range(bl) for cols in head_cols]
    return (per_pair(q_ref), per_pair(k_ref), [lr_ref[e, rows, :] for rows in chunks for e in range(bl)],
            per_head(v_ref), per_head(og_ref)), ([wg_ref[:, cols] for cols in pair_cols],
                                                 [bg_ref[:, cols] for cols in pair_cols])


def _gla_fwd(pa, pd, wg, bg, nw, bl, seq):
    n = pa.shape[0]
    nc = seq // CHUNK
    r_per = GLA_SCAN_CHUNKS
    pairs = [(e, p) for e in range(bl) for p in range(GLA_PAIRS)]
    head_slots = [(slice(r * CHUNK, (r + 1) * CHUNK), e, slice(h * 128, (h + 1) * 128))
                  for r in range(r_per) for e in range(bl) for h in range(GLA_HEADS)]

    def body(q_ref, k_ref, v_ref, og_ref, lr_ref, wg_ref, bg_ref, nw_ref, y_ref, st_ref, s_scr):
        @pl.when(pl.program_id(0) == 0)
        def _():
            s_scr[...] = jnp.zeros_like(s_scr)

        ss = [s_scr[e, p] for e, p in pairs]
        for (e, p), s in zip(pairs, ss):
            st_ref[e, 0, p] = s
        acts, gate = _gla_operands(q_ref, k_ref, v_ref, og_ref, lr_ref, wg_ref, bg_ref, bl, r_per)
        ys, s_new = _gla_chunk(*acts, ss, *gate, nw_ref[...])
        for (rows, e, cols), y in zip(head_slots, ys):
            y_ref[e, rows, cols] = y.astype(y_ref.dtype)
        for (e, p), s in zip(pairs, s_new):
            s_scr[e, p] = s

    tok = lambda w, j: pl.BlockSpec((bl, r_per * CHUNK, w), lambda i: (0, i, j))
    const = lambda i: (0, 0)
    pa3 = pa.reshape(bl, seq, 1536)
    y, st = pl.pallas_call(
        body, name="gla_fwd", grid=(nc // r_per,),
        in_specs=[tok(256, 0), tok(256, 1), tok(512, 1), tok(512, 2), tok(128, 0),
                  pl.BlockSpec(wg.shape, const), pl.BlockSpec(bg.shape, const), pl.BlockSpec(nw.shape, const)],
        out_specs=[tok(512, 0), pl.BlockSpec((bl, 1, GLA_PAIRS, 128, 128), lambda i: (0, i, 0, 0, 0))],
        out_shape=[jax.ShapeDtypeStruct((bl, seq, 512), MM_DTYPE),
                   jax.ShapeDtypeStruct((bl, nc // r_per, GLA_PAIRS, 128, 128), F32)],
        scratch_shapes=[pltpu.VMEM((bl, GLA_PAIRS, 128, 128), F32)],
        compiler_params=_params("arbitrary"),
    )(pa3, pa3, pa3, pa3, pd.reshape(bl, seq, 128), wg, bg, nw)
    return y.reshape(n, 512), st


def _gla_bwd(pa, pd, st, dya, wg, bg, nw, bl, seq):
    n = pa.shape[0]
    nc = seq // CHUNK
    r_per = GLA_SCAN_CHUNKS
    steps = nc // r_per
    pairs = [(e, p) for e in range(bl) for p in range(GLA_PAIRS)]
    pair_cols = [slice(p * 128, (p + 1) * 128) for p in range(GLA_PAIRS)]
    chunks = [slice(r * CHUNK, (r + 1) * CHUNK) for r in range(r_per)]
    pair_slots = [(rows, e, p) for rows in chunks for e in range(bl) for p in range(GLA_PAIRS)]
    head_slots = [(rows, e, h) for rows in chunks for e in range(bl) for h in range(GLA_HEADS)]

    def body(q_ref, k_ref, v_ref, og_ref, lr_ref, st_ref, dy_ref, wg_ref, bg_ref, nw_ref,
             da_ref, dd_ref, dwg_ref, dbg_ref, dnw_ref, ds_scr):
        @pl.when(pl.program_id(0) == 0)
        def _():
            dwg_ref[...] = jnp.zeros_like(dwg_ref)
            dbg_ref[...] = jnp.zeros_like(dbg_ref)
            dnw_ref[...] = jnp.zeros_like(dnw_ref)
            ds_scr[...] = jnp.zeros_like(ds_scr)

        acts, gate = _gla_operands(q_ref, k_ref, v_ref, og_ref, lr_ref, wg_ref, bg_ref, bl, r_per)
        _, vjp = jax.vjp(_gla_chunk, *acts, [st_ref[e, 0, p] for e, p in pairs], *gate, nw_ref[...])
        dq, dk, dlr, dv, dog, ds, dwg, dbg, dnw = vjp(
            ([dy_ref[e, rows, h * 128:(h + 1) * 128] for rows, e, h in head_slots], [ds_scr[e, p] for e, p in pairs]))
        for i, (rows, e) in enumerate((rows, e) for rows in chunks for e in range(bl)):
            dd_ref[e, rows, :] = dlr[i]
        for i, (rows, e, p) in enumerate(pair_slots):
            da_ref[e, rows, pair_cols[p]] = dq[i].astype(da_ref.dtype)
            da_ref[e, rows, GLA_QK + p * 128:GLA_QK + (p + 1) * 128] = dk[i].astype(da_ref.dtype)
        for i, (rows, e, h) in enumerate(head_slots):
            da_ref[e, rows, 512 + h * 128:512 + (h + 1) * 128] = dv[i].astype(da_ref.dtype)
            da_ref[e, rows, 1024 + h * 128:1024 + (h + 1) * 128] = dog[i].astype(da_ref.dtype)
        for (e, p), d in zip(pairs, ds):
            ds_scr[e, p] = d
        for p, cols in enumerate(pair_cols):
            dwg_ref[:, cols] += dwg[p]
            dbg_ref[:, cols] += dbg[p]
        dnw_ref[...] += dnw

    tok = lambda w, j: pl.BlockSpec((bl, r_per * CHUNK, w), lambda i: (0, steps - 1 - i, j))
    const = lambda i: (0, 0)
    pa3 = pa.reshape(bl, seq, 1536)
    da, dd, dwg, dbg, dnw = pl.pallas_call(
        body, name="gla_bwd", grid=(steps,),
        in_specs=[tok(256, 0), tok(256, 1), tok(512, 1), tok(512, 2), tok(128, 0),
                  pl.BlockSpec((bl, 1, GLA_PAIRS, 128, 128), lambda i: (0, steps - 1 - i, 0, 0, 0)), tok(512, 0),
                  pl.BlockSpec(wg.shape, const), pl.BlockSpec(bg.shape, const), pl.BlockSpec(nw.shape, const)],
        out_specs=[tok(1536, 0), tok(128, 0),
                   pl.BlockSpec(wg.shape, const), pl.BlockSpec(bg.shape, const), pl.BlockSpec(nw.shape, const)],
        out_shape=[jax.ShapeDtypeStruct((bl, seq, 1536), MM_DTYPE), jax.ShapeDtypeStruct((bl, seq, 128), F32),
                   jax.ShapeDtypeStruct(wg.shape, F32), jax.ShapeDtypeStruct(bg.shape, F32),
                   jax.ShapeDtypeStruct(nw.shape, F32)],
        scratch_shapes=[pltpu.VMEM((bl, GLA_PAIRS, 128, 128), F32)],
        compiler_params=_params("arbitrary"),
    )(pa3, pa3, pa3, pa3, pd.reshape(bl, seq, 128), st, dya.reshape(bl, seq, 512), wg, bg, nw)
    return da.reshape(n, 1536), dd.reshape(n, 128), dwg, dbg, dnw


def _conv_taps(buf_ref, w_ref, base, rows):
    acc = w_ref[0:1, :] * buf_ref[pl.ds(base, rows), :]
    for k in range(1, CONV_K):
        acc = acc + w_ref[k:k + 1, :] * buf_ref[pl.ds(base + k, rows), :]
    return acc


def _gdn_pre_fwd(pb, pd, cw8, alog_v, dtb_v, bl, seq, tm=256):
    n = pb.shape[0]
    tpe = seq // tm
    t8 = tm // 8

    def body(u_ref, prev_ref, ab_ref, w_ref, al_ref, dt_ref, qkv_ref, gb_ref, p_ref, buf):
        i = pl.program_id(0)
        keep = (i % tpe != 0).astype(F32)
        buf[0:8, :] = prev_ref[...] * keep
        buf[8:8 + tm, :] = u_ref[...]
        p = _conv_taps(buf, w_ref, 8 - (CONV_K - 1), tm)
        p_ref[...] = p
        ps = [p[:, j * 128:(j + 1) * 128] for j in range(12)]
        outs = _gdn_pre_elem(ps, ab_ref[...], al_ref[...], dt_ref[...])
        for j in range(12):
            qkv_ref[:, j * 128:(j + 1) * 128] = outs[j]
        gb_ref[...] = outs[12]

    row = lambda i: (i, 0)
    const = lambda i: (0, 0)
    return pl.pallas_call(
        body, name="gdn_pre_fwd", grid=(n // tm,),
        in_specs=[pl.BlockSpec((tm, 1536), row),
                  pl.BlockSpec((8, 1536), lambda i: (jnp.maximum(i * t8 - 1, 0), 0)),
                  pl.BlockSpec((tm, 128), row),
                  pl.BlockSpec((8, 1536), const), pl.BlockSpec((1, 128), const), pl.BlockSpec((1, 128), const)],
        out_specs=[pl.BlockSpec((tm, 1536), row), pl.BlockSpec((tm, 128), row), pl.BlockSpec((tm, 1536), row)],
        out_shape=[jax.ShapeDtypeStruct((n, 1536), F32), jax.ShapeDtypeStruct((n, 128), F32),
                   jax.ShapeDtypeStruct((n, 1536), F32)],
        scratch_shapes=[pltpu.VMEM((tm + 8, 1536), F32)],
        compiler_params=_params("parallel"),
    )(pb, pb, pd, cw8, alog_v, dtb_v)


def _gdn_pre_bwd(pb, conv_out, pd, dqkv, dgb, cw8, alog_v, dtb_v, bl, seq, tm=256):
    n = pb.shape[0]
    tpe = seq // tm
    t8 = tm // 8
    nb8 = n // 8
    ext = tm + 8

    def body(u_ref, p_ref, pn_ref, ab_ref, abn_ref, dq_ref, dqn_ref, dgb_ref, w_ref, al_ref, dt_ref,
             du_ref, dab_ref, dw_ref, dal_ref, ddt_ref, dpbuf):
        i = pl.program_id(0)

        @pl.when(i == 0)
        def _():
            dw_ref[...] = jnp.zeros_like(dw_ref)
            dal_ref[...] = jnp.zeros_like(dal_ref)
            ddt_ref[...] = jnp.zeros_like(ddt_ref)

        keep_next = (i % tpe != tpe - 1).astype(F32)
        ps = [jnp.concatenate([p_ref[:, j * 128:(j + 1) * 128], pn_ref[:, j * 128:(j + 1) * 128]], axis=0)
              for j in range(12)]
        ab = jnp.concatenate([ab_ref[...], abn_ref[...]], axis=0)
        _, vjp = jax.vjp(_gdn_pre_elem, ps, ab, al_ref[...], dt_ref[...])
        zeros8 = jnp.zeros((8, 128), F32)
        cts = tuple(jnp.concatenate([dq_ref[:, j * 128:(j + 1) * 128],
                                     dqn_ref[:, j * 128:(j + 1) * 128] * keep_next], axis=0) for j in range(12))
        cts += (jnp.concatenate([dgb_ref[...], zeros8], axis=0),)
        dps, dab, dal, ddt = vjp(cts)
        for j in range(12):
            dpbuf[:, j * 128:(j + 1) * 128] = dps[j]
        dab_ref[...] = dab[0:tm, :]
        dal_ref[...] += dal
        ddt_ref[...] += ddt
        u = u_ref[...]
        du = None
        for k in range(CONV_K):
            dp_k = dpbuf[pl.ds(CONV_K - 1 - k, tm), :]
            term = w_ref[k:k + 1, :] * dp_k
            du = term if du is None else du + term
            dw_ref[k:k + 1, :] += jnp.sum(u * dp_k, axis=0, keepdims=True)
        du_ref[...] = du.astype(du_ref.dtype)

    row = lambda i: (i, 0)
    next8 = lambda i: (jnp.minimum((i + 1) * t8, nb8 - 1), 0)
    const = lambda i: (0, 0)
    return pl.pallas_call(
        body, name="gdn_pre_bwd", grid=(n // tm,),
        in_specs=[pl.BlockSpec((tm, 1536), row), pl.BlockSpec((tm, 1536), row), pl.BlockSpec((8, 1536), next8),
                  pl.BlockSpec((tm, 128), row), pl.BlockSpec((8, 128), next8),
                  pl.BlockSpec((tm, 1536), row), pl.BlockSpec((8, 1536), next8),
                  pl.BlockSpec((tm, 128), row),
                  pl.BlockSpec((8, 1536), const), pl.BlockSpec((1, 128), const), pl.BlockSpec((1, 128), const)],
        out_specs=[pl.BlockSpec((tm, 1536), row), pl.BlockSpec((tm, 128), row),
                   pl.BlockSpec((8, 1536), const), pl.BlockSpec((1, 128), const), pl.BlockSpec((1, 128), const)],
        out_shape=[jax.ShapeDtypeStruct((n, 1536), MM_DTYPE), jax.ShapeDtypeStruct((n, 128), F32),
                   jax.ShapeDtypeStruct((8, 1536), F32), jax.ShapeDtypeStruct((1, 128), F32),
                   jax.ShapeDtypeStruct((1, 128), F32)],
        scratch_shapes=[pltpu.VMEM((ext, 1536), F32)],
        compiler_params=_params("arbitrary"),
    )(pb, conv_out, conv_out, pd, pd, dqkv, dqkv, dgb, cw8, alog_v, dtb_v)


GDN_PREP_CHUNKS = 4
GDN_PREP_BWD_CHUNKS = 2
GDN_SCAN_CHUNKS = 2
MM_DTYPE = BF16


def _head_cols(ref, rows, base=0):
    return [ref[rows, base + h * 128:base + (h + 1) * 128] for h in range(GDN_HEADS)]


def _gdn_prep(qkv, gb):
    n = qkv.shape[0]
    r_per = GDN_PREP_CHUNKS
    tm = r_per * CHUNK

    def body(q_ref, k_ref, v_ref, gb_ref, u_ref, w_ref, qd_ref, kd_ref, qk_ref, t_ref, gam_ref):
        rowid = _iota((8, 128), 0)
        chunk_rows = [slice(r * CHUNK, (r + 1) * CHUNK) for r in range(r_per)]
        gather = lambda ref: [t for rows in chunk_rows for t in _head_cols(ref, rows)]
        u, w, qk, qd, kd, gamma, tinv = _gdn_prep_units(gather(q_ref), gather(k_ref), gather(v_ref),
                                                        [gb_ref[rows, :] for rows in chunk_rows])
        for r, rows in enumerate(chunk_rows):
            gam = jnp.zeros((8, 128), F32)
            for h in range(GDN_HEADS):
                i = r * GDN_HEADS + h
                cols = slice(h * 128, (h + 1) * 128)
                u_ref[rows, cols] = u[i]
                w_ref[rows, cols] = w[i].astype(MM_DTYPE)
                qd_ref[rows, cols] = qd[i].astype(MM_DTYPE)
                kd_ref[rows, cols] = kd[i].astype(MM_DTYPE)
                qk_ref[r, h] = qk[i].astype(MM_DTYPE)
                t_ref[r, h] = tinv[i].astype(MM_DTYPE)
                gam = jnp.where(rowid == h, gamma[i], gam)
            gam_ref[r] = gam

    tok = lambda j: pl.BlockSpec((tm, 512), lambda i: (i, j))
    return pl.pallas_call(
        body, name="gdn_prep", grid=(n // tm,),
        in_specs=[tok(0), tok(1), tok(2), pl.BlockSpec((tm, 128), lambda i: (i, 0))],
        out_specs=[tok(0)] * 4 + [pl.BlockSpec((r_per, GDN_HEADS, CHUNK, CHUNK), lambda i: (i, 0, 0, 0))] * 2
        + [pl.BlockSpec((r_per, 8, 128), lambda i: (i, 0, 0))],
        out_shape=[jax.ShapeDtypeStruct((n, 512), F32)] + [jax.ShapeDtypeStruct((n, 512), MM_DTYPE)] * 3
        + [jax.ShapeDtypeStruct((n // CHUNK, GDN_HEADS, CHUNK, CHUNK), MM_DTYPE)] * 2
        + [jax.ShapeDtypeStruct((n // CHUNK, 8, 128), F32)],
        compiler_params=_params("parallel"),
    )(qkv, qkv, qkv, gb)


def _gdn_fwd(qkv, gb, pc, nw, bl, seq):
    n = qkv.shape[0]
    nc = seq // CHUNK
    u, w, qd, kd, qk, tinv, gam = _gdn_prep(qkv, gb)
    tok3 = lambda t: t.reshape(bl, seq, 512)
    qk5 = qk.reshape(bl, nc, GDN_HEADS, CHUNK, CHUNK)
    gam4 = gam.reshape(bl, nc, 8, 128)

    r_per = GDN_SCAN_CHUNKS
    mm = lambda t: t.astype(MM_DTYPE)

    def body(u_ref, w_ref, qd_ref, kd_ref, qk_ref, gam_ref, og_ref, nw_ref, o_ref, y_ref, vn_ref, st_ref, s_scr):
        @pl.when(pl.program_id(0) == 0)
        def _():
            s_scr[...] = jnp.zeros_like(s_scr)

        units = [(b, h, slice(h * 128, (h + 1) * 128)) for b in range(bl) for h in range(GDN_HEADS)]
        ss = [s_scr[b, h] for b, h, _ in units]
        for r in range(r_per):
            rows = slice(r * CHUNK, (r + 1) * CHUNK)
            for (b, h, _), s in zip(units, ss):
                st_ref[b, r, h] = s
            ws_qs = [_dot(jnp.concatenate([w_ref[b, rows, cols], qd_ref[b, rows, cols]], axis=0), mm(s))
                     for (b, h, cols), s in zip(units, ss)]
            v_new = [u_ref[b, rows, cols] - m[0:CHUNK] for (b, h, cols), m in zip(units, ws_qs)]
            os_ = [m[CHUNK:2 * CHUNK] + _dot(qk_ref[b, r, h], mm(vn))
                   for (b, h, cols), m, vn in zip(units, ws_qs, v_new)]
            ss = [s * gam_ref[b, r, h:h + 1, :] + _dot_tn(kd_ref[b, rows, cols], mm(vn))
                  for (b, h, cols), s, vn in zip(units, ss, v_new)]
            for (b, h, cols), vn, o in zip(units, v_new, os_):
                vn_ref[b, rows, cols] = mm(vn)
                o_ref[b, rows, cols] = o
                y_ref[b, rows, cols] = mm(_rms_gate(o, nw_ref[...], og_ref[b, rows, cols]))
        for (b, h, _), s in zip(units, ss):
            s_scr[b, h] = s

    tok = pl.BlockSpec((bl, r_per * CHUNK, 512), lambda i: (0, i, 0))
    st_spec = pl.BlockSpec((bl, r_per, GDN_HEADS, 128, 128), lambda i: (0, i, 0, 0, 0))
    tok_shape = jax.ShapeDtypeStruct((bl, seq, 512), F32)
    o, y, vn, st = pl.pallas_call(
        body, name="gdn_scan_fwd", grid=(nc // r_per,),
        in_specs=[tok, tok, tok, tok,
                  pl.BlockSpec((bl, r_per, GDN_HEADS, CHUNK, CHUNK), lambda i: (0, i, 0, 0, 0)),
                  pl.BlockSpec((bl, r_per, 8, 128), lambda i: (0, i, 0, 0)), tok,
                  pl.BlockSpec(nw.shape, lambda i: (0, 0))],
        out_specs=[tok, tok, tok, st_spec],
        out_shape=[tok_shape, jax.ShapeDtypeStruct((bl, seq, 512), MM_DTYPE), jax.ShapeDtypeStruct((bl, seq, 512), MM_DTYPE),
                   jax.ShapeDtypeStruct((bl, nc, GDN_HEADS, 128, 128), F32)],
        scratch_shapes=[pltpu.VMEM((bl, GDN_HEADS, 128, 128), F32)],
        compiler_params=_params("arbitrary"),
    )(tok3(u), tok3(w), tok3(qd), tok3(kd), qk5, gam4, tok3(pc), nw)
    return y.reshape(n, 512), (o, st, w, qd, kd, qk5, gam4, tinv, vn)


def _gdn_bwd(qkv, gb, pc, res, dyb, nw, bl, seq):
    n = qkv.shape[0]
    nc = seq // CHUNK
    o, st, w, qd, kd, qk5, gam4, tinv, vn = res
    tok3 = lambda t: t.reshape(bl, seq, 512)

    def scan_body(dy_ref, o_ref, og_ref, w_ref, qd_ref, kd_ref, qk_ref, gam_ref, nw_ref,
                  do_ref, dog_ref, dvn_ref, dst_ref, dnw_ref, ds_scr):
        @pl.when(pl.program_id(0) == 0)
        def _():
            ds_scr[...] = jnp.zeros_like(ds_scr)
            dnw_ref[...] = jnp.zeros_like(dnw_ref)

        units = [(b, h, slice(h * 128, (h + 1) * 128)) for b in range(bl) for h in range(GDN_HEADS)]
        dnw = jnp.zeros(nw.shape, F32)
        dss = [ds_scr[b, h] for b, h, _ in units]
        for r in reversed(range(r_scan)):
            rows = slice(r * CHUNK, (r + 1) * CHUNK)
            d_os = []
            for b, h, cols in units:
                _, vjp = jax.vjp(_rms_gate, o_ref[b, rows, cols], nw_ref[...], og_ref[b, rows, cols])
                d_o, dnw_h, dog = vjp(dy_ref[b, rows, cols])
                do_ref[b, rows, cols] = mm(d_o)
                dog_ref[b, rows, cols] = mm(dog)
                dnw = dnw + dnw_h
                d_os.append(mm(d_o))
            for (b, h, _), ds in zip(units, dss):
                dst_ref[b, r, h] = ds
            dvn_a = [_dot(kd_ref[b, rows, cols], mm(ds)) for (b, h, cols), ds in zip(units, dss)]
            dvns = [a + _dot_tn(qk_ref[b, r, h], d_o) for (b, h, cols), a, d_o in zip(units, dvn_a, d_os)]
            for (b, h, cols), dvn in zip(units, dvns):
                dvn_ref[b, rows, cols] = mm(dvn)
            dss = [ds * gam_ref[b, r, h:h + 1, :] + _dot_tn(
                jnp.concatenate([qd_ref[b, rows, cols], w_ref[b, rows, cols]], axis=0),
                jnp.concatenate([d_o, mm(-dvn)], axis=0))
                for (b, h, cols), d_o, ds, dvn in zip(units, d_os, dss, dvns)]
        dnw_ref[...] += dnw
        for (b, h, _), ds in zip(units, dss):
            ds_scr[b, h] = ds

    r_scan = GDN_SCAN_CHUNKS
    mm = lambda t: t.astype(MM_DTYPE)
    rev = lambda i: nc // r_scan - 1 - i
    tok = pl.BlockSpec((bl, r_scan * CHUNK, 512), lambda i: (0, rev(i), 0))
    st_spec = pl.BlockSpec((bl, r_scan, GDN_HEADS, 128, 128), lambda i: (0, rev(i), 0, 0, 0))
    tok_shape = jax.ShapeDtypeStruct((bl, seq, 512), F32)
    tok_mm = jax.ShapeDtypeStruct((bl, seq, 512), MM_DTYPE)
    d_o, dog, dvn, dst, dnw = pl.pallas_call(
        scan_body, name="gdn_scan_bwd", grid=(nc // r_scan,),
        in_specs=[tok] * 6 + [pl.BlockSpec((bl, r_scan, GDN_HEADS, CHUNK, CHUNK), lambda i: (0, rev(i), 0, 0, 0)),
                              pl.BlockSpec((bl, r_scan, 8, 128), lambda i: (0, rev(i), 0, 0)),
                              pl.BlockSpec(nw.shape, lambda i: (0, 0))],
        out_specs=[tok, tok, tok, st_spec, pl.BlockSpec(nw.shape, lambda i: (0, 0))],
        out_shape=[tok_mm, tok_mm, tok_mm, jax.ShapeDtypeStruct(st.shape, F32),
                   jax.ShapeDtypeStruct(nw.shape, F32)],
        scratch_shapes=[pltpu.VMEM((bl, GDN_HEADS, 128, 128), F32)],
        compiler_params=_params("arbitrary"),
    )(tok3(dyb), o, tok3(pc), tok3(w), tok3(qd), tok3(kd), qk5, gam4, nw)

    r_per = GDN_PREP_BWD_CHUNKS
    tm = r_per * CHUNK

    def prep_body(q_ref, k_ref, v_ref, gb_ref, t_ref, st_ref, dst_ref, dvn_ref, do_ref, vn_ref, dqkv_ref, dgb_ref):
        chunk_rows = [slice(r * CHUNK, (r + 1) * CHUNK) for r in range(r_per)]
        gather = lambda ref: [t for rows in chunk_rows for t in _head_cols(ref, rows)]
        units = [(r, h) for r in range(r_per) for h in range(GDN_HEADS)]
        t_known = [t_ref[r, h].astype(F32) for r, h in units]
        prep = lambda q, k, v, g: _gdn_prep_units(q, k, v, g, t_known)[:6]
        _, vjp = jax.vjp(prep, gather(q_ref), gather(k_ref), gather(v_ref), [gb_ref[rows, :] for rows in chunk_rows])
        ss = [st_ref[r, h] for r, h in units]
        dss = [dst_ref[r, h] for r, h in units]
        dvns, d_os, v_new = gather(dvn_ref), gather(do_ref), gather(vn_ref)
        both = [_dot_nt(jnp.concatenate([dvn, d_o], axis=0), s.astype(MM_DTYPE)) for dvn, d_o, s in zip(dvns, d_os, ss)]
        d_w = [-m[0:CHUNK] for m in both]
        d_qd = [m[CHUNK:2 * CHUNK] for m in both]
        d_qk = [_dot_nt(d_o, vn) for d_o, vn in zip(d_os, v_new)]
        d_kd = [_dot_nt(vn, ds.astype(MM_DTYPE)) for vn, ds in zip(v_new, dss)]
        d_gam = [_sum_all(ds * s) for ds, s in zip(dss, ss)]
        dq, dk, dv, dgb = vjp(([d.astype(F32) for d in dvns], d_w, d_qk, d_qd, d_kd, d_gam))
        for i, (r, h) in enumerate(units):
            rows = chunk_rows[r]
            for part, d in enumerate((dq, dk, dv)):
                dqkv_ref[rows, part * 512 + h * 128:part * 512 + (h + 1) * 128] = d[i]
        for r, rows in enumerate(chunk_rows):
            dgb_ref[rows, :] = dgb[r]

    tokp = lambda j: pl.BlockSpec((tm, 512), lambda i: (i, j))
    st4 = pl.BlockSpec((r_per, GDN_HEADS, 128, 128), lambda i: (i, 0, 0, 0))
    dqkv, dgb = pl.pallas_call(
        prep_body, name="gdn_prep_bwd", grid=(n // tm,),
        in_specs=[tokp(0), tokp(1), tokp(2), pl.BlockSpec((tm, 128), lambda i: (i, 0)),
                  pl.BlockSpec((r_per, GDN_HEADS, CHUNK, CHUNK), lambda i: (i, 0, 0, 0)), st4, st4,
                  tokp(0), tokp(0), tokp(0)],
        out_specs=[pl.BlockSpec((tm, 1536), lambda i: (i, 0)), pl.BlockSpec((tm, 128), lambda i: (i, 0))],
        out_shape=[jax.ShapeDtypeStruct((n, 1536), F32), jax.ShapeDtypeStruct((n, 128), F32)],
        compiler_params=_params("parallel"),
    )(qkv, qkv, qkv, gb, tinv, st.reshape(bl * nc, GDN_HEADS, 128, 128), dst.reshape(bl * nc, GDN_HEADS, 128, 128),
      dvn.reshape(n, 512), d_o.reshape(n, 512), vn.reshape(n, 512))
    return dqkv, dog.reshape(n, 512), dgb, dnw


def _out_block(x2, tgt2, ya, yb, g1p3, wo, lnw, lnb, seq, tm=256):
    n = x2.shape[0]
    tpe = seq // tm
    bl = n // seq

    def body(x_ref, t_ref, ya_ref, yb_ref, g_ref, wo_ref, lnw_ref, lnb_ref,
             dz_ref, dya_ref, dyb_ref, dwo_ref, dg_ref, glw_ref, glb_ref, loss_ref):
        i = pl.program_id(0)

        @pl.when(i == 0)
        def _():
            dwo_ref[...] = jnp.zeros_like(dwo_ref)
            glw_ref[...] = jnp.zeros_like(glw_ref)
            glb_ref[...] = jnp.zeros_like(glb_ref)
            loss_ref[...] = jnp.zeros_like(loss_ref)

        @pl.when(i % tpe == 0)
        def _():
            dg_ref[...] = jnp.zeros_like(dg_ref)

        ya16 = ya_ref[...].astype(wo.dtype)
        yb16 = yb_ref[...].astype(wo.dtype)
        wa = wo_ref[0:GLA_WIDTH, :]
        wb = wo_ref[GLA_WIDTH:, :]
        y = _dot(ya16, wa) + _dot(yb16, wb)
        g1p = g_ref[0]
        z = ALPHA * x_ref[...] + g1p * y
        mu = jnp.mean(z, axis=-1, keepdims=True)
        zc = z - mu
        rstd = lax.rsqrt(jnp.mean(zc * zc, axis=-1, keepdims=True) + LN_EPS)
        xhat = zc * rstd
        diff = xhat * lnw_ref[...] + lnb_ref[...] - t_ref[...]
        loss_ref[...] += (0.5 / D_MODEL) * jnp.sum(jnp.sum(diff * diff, axis=-1, keepdims=True), axis=0, keepdims=True)
        dout = diff * (1.0 / D_MODEL)
        glw_ref[...] += jnp.sum(dout * xhat, axis=0, keepdims=True)
        glb_ref[...] += jnp.sum(dout, axis=0, keepdims=True)
        dxh = dout * lnw_ref[...]
        dz = rstd * (dxh - jnp.mean(dxh, axis=-1, keepdims=True)
                     - xhat * jnp.mean(dxh * xhat, axis=-1, keepdims=True))
        dz_ref[...] = dz
        dg_ref[0] += jnp.sum(dz * y, axis=0, keepdims=True)
        dy = (g1p * dz).astype(wo.dtype)
        dya_ref[...] = _dot_nt(dy, wa)
        dyb_ref[...] = _dot_nt(dy, wb)
        dwo_ref[0:GLA_WIDTH, :] += _dot_tn(ya16, dy)
        dwo_ref[GLA_WIDTH:, :] += _dot_tn(yb16, dy)

    row = lambda i: (i, 0)
    const = lambda i: (0, 0)
    per_ex = pl.BlockSpec((1, 1, D_MODEL), lambda i: (i // tpe, 0, 0))
    return pl.pallas_call(
        body, name="out_block", grid=(n // tm,),
        in_specs=[pl.BlockSpec((tm, D_MODEL), row), pl.BlockSpec((tm, D_MODEL), row),
                  pl.BlockSpec((tm, 512), row), pl.BlockSpec((tm, 512), row), per_ex,
                  pl.BlockSpec((D_MODEL, D_MODEL), const), pl.BlockSpec((1, D_MODEL), const),
                  pl.BlockSpec((1, D_MODEL), const)],
        out_specs=[pl.BlockSpec((tm, D_MODEL), row), pl.BlockSpec((tm, 512), row), pl.BlockSpec((tm, 512), row),
                   pl.BlockSpec((D_MODEL, D_MODEL), const), per_ex,
                   pl.BlockSpec((1, D_MODEL), const), pl.BlockSpec((1, D_MODEL), const),
                   pl.BlockSpec((1, 1), const)],
        out_shape=[jax.ShapeDtypeStruct((n, D_MODEL), F32), jax.ShapeDtypeStruct((n, 512), F32),
                   jax.ShapeDtypeStruct((n, 512), F32), jax.ShapeDtypeStruct((D_MODEL, D_MODEL), F32),
                   jax.ShapeDtypeStruct((bl, 1, D_MODEL), F32), jax.ShapeDtypeStruct((1, D_MODEL), F32),
                   jax.ShapeDtypeStruct((1, D_MODEL), F32), jax.ShapeDtypeStruct((1, 1), F32)],
        compiler_params=_params("arbitrary"),
    )(x2, tgt2, ya, yb, g1p3, wo, lnw, lnb)


def _proj_bwd_x(ds, ws, x2, dz, sc3, seq, tm=256):
    n = x2.shape[0]
    tpe = seq // tm
    bl = n // seq

    def body(da_ref, db_ref, dc_ref, dd1_ref, dd2_ref, wa_ref, wb_ref, wc_ref, wd_ref, x_ref, dz_ref, sc_ref,
             gx_ref, dsh_ref, dsc_ref):
        i = pl.program_id(0)

        @pl.when(i % tpe == 0)
        def _():
            dsh_ref[...] = jnp.zeros_like(dsh_ref)
            dsc_ref[...] = jnp.zeros_like(dsc_ref)

        cdt = ws[0].dtype
        dh = _dot(da_ref[...].astype(cdt), wa_ref[...])
        dh += _dot(db_ref[...].astype(cdt), wb_ref[...])
        dh += _dot(dc_ref[...].astype(cdt), wc_ref[...])
        dh += _dot((dd1_ref[...] + dd2_ref[...]).astype(cdt), wd_ref[...])
        gx_ref[...] = dh * sc_ref[0] + ALPHA * dz_ref[...]
        dsh_ref[0] += jnp.sum(dh, axis=0, keepdims=True)
        dsc_ref[0] += jnp.sum(dh * x_ref[...], axis=0, keepdims=True)

    row = lambda i: (i, 0)
    const = lambda i: (0, 0)
    per_ex = pl.BlockSpec((1, 1, D_MODEL), lambda i: (i // tpe, 0, 0))
    da, db, dc, (dd1, dd2) = ds
    return pl.pallas_call(
        body, name="proj_bwd_x", grid=(n // tm,),
        in_specs=[pl.BlockSpec((tm, d.shape[1]), row) for d in (da, db, dc, dd1, dd2)]
        + [pl.BlockSpec(w.shape, const) for w in ws]
        + [pl.BlockSpec((tm, D_MODEL), row), pl.BlockSpec((tm, D_MODEL), row), per_ex],
        out_specs=[pl.BlockSpec((tm, D_MODEL), row), per_ex, per_ex],
        out_shape=[jax.ShapeDtypeStruct((n, D_MODEL), F32), jax.ShapeDtypeStruct((bl, 1, D_MODEL), F32),
                   jax.ShapeDtypeStruct((bl, 1, D_MODEL), F32)],
        compiler_params=_params("arbitrary"),
    )(da, db, dc, dd1, dd2, *ws, x2, dz, sc3)


def _proj_bwd_w(x2, sc3, sh3, ds, seq, cdt, name, tm=256):
    n = x2.shape[0]
    tpe = seq // tm
    flat, groups = [], []
    for d in ds:
        parts = d if isinstance(d, tuple) else (d,)
        groups.append(len(parts))
        flat.extend(parts)
    nin = len(flat)

    def body(x_ref, sc_ref, sh_ref, *refs):
        i = pl.program_id(0)
        outs = refs[nin:]

        @pl.when(i == 0)
        def _():
            for o in outs:
                o[...] = jnp.zeros_like(o)

        h = (x_ref[...] * sc_ref[0] + sh_ref[0]).astype(cdt)
        pos = 0
        for o, cnt in zip(outs, groups):
            d = refs[pos][...]
            for extra in refs[pos + 1:pos + cnt]:
                d = d + extra[...]
            pos += cnt
            o[...] += _dot_tn(d.astype(cdt), h)

    row = lambda i: (i, 0)
    const = lambda i: (0, 0)
    per_ex = pl.BlockSpec((1, 1, D_MODEL), lambda i: (i // tpe, 0, 0))
    widths = [(d[0] if isinstance(d, tuple) else d).shape[1] for d in ds]
    return pl.pallas_call(
        body, name=name, grid=(n // tm,),
        in_specs=[pl.BlockSpec((tm, D_MODEL), row), per_ex, per_ex]
        + [pl.BlockSpec((tm, d.shape[1]), row) for d in flat],
        out_specs=[pl.BlockSpec((w, D_MODEL), const) for w in widths],
        out_shape=[jax.ShapeDtypeStruct((w, D_MODEL), F32) for w in widths],
        compiler_params=_params("arbitrary"),
    )(x2, sc3, sh3, *flat)


def _mod_block(c_all, w_ada_sh, b_blk):
    def body(c_ref, w_ref, b_ref, o_ref):
        o_ref[...] = _dot(c_ref[...], w_ref[...]) + b_ref[...]

    return pl.pallas_call(
        body, name="mod_block",
        out_shape=jax.ShapeDtypeStruct((c_all.shape[0], w_ada_sh.shape[1]), F32),
        compiler_params=pltpu.CompilerParams(vmem_limit_bytes=VMEM_LIMIT),
    )(c_all, w_ada_sh, b_blk)


def _ada_grads(c_all, dmod_all, dmod_blk):
    def body(c_ref, da_ref, db_ref, gw_ref, gb_ref):
        gw_ref[...] = _dot_tn(c_ref[...], db_ref[...])
        gb_ref[...] = jnp.sum(da_ref[...], axis=0, keepdims=True)

    return pl.pallas_call(
        body, name="ada_grads",
        out_shape=[jax.ShapeDtypeStruct((c_all.shape[1], dmod_blk.shape[1]), F32),
                   jax.ShapeDtypeStruct((1, dmod_all.shape[1]), F32)],
        compiler_params=pltpu.CompilerParams(vmem_limit_bytes=VMEM_LIMIT),
    )(c_all, dmod_all, dmod_blk)


def _sum_leading(parts, name):
    def body(p_ref, o_ref):
        acc = p_ref[0]
        for d in range(1, parts.shape[0]):
            acc = acc + p_ref[d]
        o_ref[...] = acc

    return pl.pallas_call(
        body, name=name, out_shape=jax.ShapeDtypeStruct(parts.shape[1:], F32),
        compiler_params=pltpu.CompilerParams(vmem_limit_bytes=VMEM_LIMIT),
    )(parts)


ELEMENTWISE_BLOCK_BYTES = 2 * 1024 * 1024


def _tile2d(rows, cols, row_align=8):
    if rows * cols * 4 <= ELEMENTWISE_BLOCK_BYTES:
        return rows, cols
    fits = [t for t in range(row_align, rows, row_align) if rows % t == 0 and t * cols * 4 <= ELEMENTWISE_BLOCK_BYTES]
    if fits:
        return fits[-1], cols
    fits = [t for t in range(128, cols, 128) if cols % t == 0 and rows * t * 4 <= ELEMENTWISE_BLOCK_BYTES]
    assert fits, (rows, cols)
    return rows, fits[-1]


def _add_n(arrs, name, out_dtypes=(F32,)):
    rows, cols = arrs[0].shape
    narrow = any(jnp.dtype(dt).itemsize < 4 for dt in tuple(out_dtypes) + tuple(a.dtype for a in arrs))
    tr, tc = _tile2d(rows, cols, 16 if narrow else 8)
    n_in = len(arrs)

    def body(*refs):
        acc = refs[0][...].astype(F32)
        for r in refs[1:n_in]:
            acc = acc + r[...].astype(F32)
        for o in refs[n_in:]:
            o[...] = acc.astype(o.dtype)

    spec = pl.BlockSpec((tr, tc), lambda i, j: (i, j))
    return pl.pallas_call(
        body, name=name, grid=(rows // tr, cols // tc), in_specs=[spec] * n_in, out_specs=[spec] * len(out_dtypes),
        out_shape=[jax.ShapeDtypeStruct((rows, cols), dt) for dt in out_dtypes],
        compiler_params=_params("parallel", "parallel"),
    )(*arrs)


def _chip_sum_blocks(a, b, per, blocks, name, chunk=128):
    rows, cols = a.shape
    padded = -(-per // 16) * 16
    assert rows >= (blocks - 1) * per + padded, (rows, per, blocks)

    def body(a_ref, b_ref, o_ref, o16_ref):
        for j in range(blocks):
            for r0 in range(0, padded, chunk):
                n_rows = min(chunk, padded - r0)
                src = pl.ds(j * per + r0, n_rows)
                s = a_ref[src, :] + b_ref[src, :]
                if per - r0 < n_rows:
                    s = jnp.where(_iota((n_rows, 1), 0) < per - r0, s, 0.0)
                o_ref[j, r0:r0 + n_rows, :] = s
                o16_ref[j, r0:r0 + n_rows, :] = s.astype(BF16)

    return pl.pallas_call(
        body, name=name,
        out_shape=[jax.ShapeDtypeStruct((blocks, padded, cols), F32), jax.ShapeDtypeStruct((blocks, padded, cols), BF16)],
        compiler_params=pltpu.CompilerParams(vmem_limit_bytes=VMEM_LIMIT),
    )(a, b)


GRAD_PAD_ROWS = 16


def _adamw(w, g, m, v, name):
    rows, cols = w.shape
    tr, tc = _tile2d(rows, cols)
    c1 = 1.0 / (1.0 - ADAM_B1 ** ADAM_STEP)
    c2 = 1.0 / (1.0 - ADAM_B2 ** ADAM_STEP)

    def body(w_ref, g_ref, m_ref, v_ref, d_ref, nm_ref, nv_ref):
        gg = g_ref[...]
        nm = ADAM_B1 * m_ref[...] + (1.0 - ADAM_B1) * gg
        nv = ADAM_B2 * v_ref[...] + (1.0 - ADAM_B2) * (gg * gg)
        nm_ref[...] = nm
        nv_ref[...] = nv
        d_ref[...] = -ADAM_LR * ((nm * c1) / (jnp.sqrt(nv * c2) + ADAM_EPS) + ADAM_WD * w_ref[...])

    spec = pl.BlockSpec((tr, tc), lambda i, j: (i, j))
    shp = jax.ShapeDtypeStruct((rows, cols), F32)
    return pl.pallas_call(
        body, name=name, grid=(rows // tr, cols // tc), in_specs=[spec] * 4, out_specs=[spec] * 3,
        out_shape=[shp, shp, shp], compiler_params=_params("parallel", "parallel"),
    )(w, g, m, v)


def _coords():
    return lax.axis_index("x"), lax.axis_index("y"), lax.axis_index("c")


def _all_gather8(blk, name):
    m_per, n = blk.shape

    def body(x_ref, out_ref, send_sems, recv_sems, local_sem):
        x, y, c = _coords()
        me, sibling = (x, y, c), (x, y, 1 - c)
        chips = [(1 - x, y), (x, 1 - y), (1 - x, 1 - y)]

        def rows(px, py, pc):
            return out_ref.at[pl.ds((4 * px + 2 * py + pc) * m_per, m_per), :]

        def copy(k, block, to, src=None):
            return pltpu.make_async_remote_copy(
                src_ref=rows(*block) if src is None else src, dst_ref=rows(*block),
                send_sem=send_sems.at[k], recv_sem=recv_sems.at[k], device_id=to, device_id_type=MESH)

        mine = pltpu.make_async_copy(x_ref, rows(*me), local_sem)
        mine.start()
        first = [copy(0, me, sibling, src=x_ref)]
        first += [copy(1 + j, me, (*chip, c), src=x_ref) for j, chip in enumerate(chips)]
        for cp in first:
            cp.start()
        passed = [copy(4 + j, (*chip, c), sibling) for j, chip in enumerate(chips)]
        for j, chip in enumerate(chips):
            copy(1 + j, (*chip, c), me).wait_recv()
            passed[j].start()
        copy(0, sibling, me).wait_recv()
        for j, chip in enumerate(chips):
            copy(4 + j, (*chip, 1 - c), me).wait_recv()
        for cp in first + passed:
            cp.wait_send()
        mine.wait()

    return pl.pallas_call(
        body, name=name,
        out_shape=jax.ShapeDtypeStruct((8 * m_per, n), blk.dtype),
        in_specs=[pl.BlockSpec(memory_space=pltpu.VMEM)],
        out_specs=pl.BlockSpec(memory_space=pltpu.VMEM),
        scratch_shapes=[pltpu.SemaphoreType.DMA((7,)), pltpu.SemaphoreType.DMA((7,)), pltpu.SemaphoreType.DMA],
        compiler_params=pltpu.CompilerParams(vmem_limit_bytes=VMEM_LIMIT),
    )(blk)


def _chip_gather(shards, split, name):
    k_arr = len(shards)

    def body(*refs):
        srcs, dsts = refs[:k_arr], refs[k_arr:2 * k_arr]
        send_sems, recv_sems, fwd_send_sems, fwd_recv_sems, local_sems = refs[2 * k_arr:]
        x, y, c = _coords()
        peers = [(1 - x, y, c), (x, 1 - y, c), (1 - x, 1 - y, c)]
        sibling = (x, y, 1 - c)
        me_chip = 2 * x + y

        def part(ref, a, core):
            if not split[a]:
                return ref
            half = shards[a].shape[1] // 2
            return ref.at[:, pl.ds(core * half, half)]

        def ici(a, j, src_chip, dst_dev):
            return pltpu.make_async_remote_copy(
                src_ref=part(srcs[a], a, c), dst_ref=part(dsts[a].at[src_chip], a, c),
                send_sem=send_sems.at[a, j], recv_sem=recv_sems.at[a, j], device_id=dst_dev, device_id_type=MESH)

        def d2d(a, j, src_chip, core):
            return pltpu.make_async_remote_copy(
                src_ref=part(dsts[a].at[src_chip], a, core), dst_ref=part(dsts[a].at[src_chip], a, core),
                send_sem=fwd_send_sems.at[a, j], recv_sem=fwd_recv_sems.at[a, j],
                device_id=sibling, device_id_type=MESH)

        local = [pltpu.make_async_copy(srcs[a], dsts[a].at[me_chip], local_sems.at[a]) for a in range(k_arr)]
        for cp in local:
            cp.start()
        sends = [ici(a, j, me_chip, peer) for a in range(k_arr) for j, peer in enumerate(peers)]
        for cp in sends:
            cp.start()
        forwards = []
        for a in range(k_arr):
            for j, peer in enumerate(peers):
                peer_chip = 2 * peer[0] + peer[1]
                ici(a, j, peer_chip, peer).wait_recv()
                if split[a]:
                    forwards.append(d2d(a, j, peer_chip, c))
                    forwards[-1].start()
        for a in range(k_arr):
            for j, peer in enumerate(peers):
                if split[a]:
                    d2d(a, j, 2 * peer[0] + peer[1], 1 - c).wait_recv()
        for cp in sends + forwards:
            cp.wait_send()
        for cp in local:
            cp.wait()

    any_spec = pl.BlockSpec(memory_space=pl.ANY)
    return pl.pallas_call(
        body, name=name,
        out_shape=[jax.ShapeDtypeStruct((4,) + s.shape, s.dtype) for s in shards],
        in_specs=[any_spec] * k_arr, out_specs=[any_spec] * k_arr,
        scratch_shapes=[pltpu.SemaphoreType.DMA((k_arr, 3))] * 4 + [pltpu.SemaphoreType.DMA((k_arr,))],
    )(*shards)


def _chip_scatter(pieces, name):
    k_arr = len(pieces)

    def body(*refs):
        srcs, dsts = refs[:k_arr], refs[k_arr:2 * k_arr]
        send_sems, recv_sems = refs[2 * k_arr:]
        x, y, c = _coords()
        peers = [(1 - x, y, c), (x, 1 - y, c), (1 - x, 1 - y, c)]
        copies = []
        for a in range(k_arr):
            for j, peer in enumerate(peers):
                copies.append(pltpu.make_async_remote_copy(
                    src_ref=srcs[a].at[2 * peer[0] + peer[1]], dst_ref=dsts[a].at[j],
                    send_sem=send_sems.at[a, j], recv_sem=recv_sems.at[a, j], device_id=peer, device_id_type=MESH))
        for cp in copies:
            cp.start()
        for cp in copies:
            cp.wait_recv()
        for cp in copies:
            cp.wait_send()

    any_spec = pl.BlockSpec(memory_space=pl.ANY)
    return pl.pallas_call(
        body, name=name,
        out_shape=[jax.ShapeDtypeStruct((3,) + p.shape[1:], p.dtype) for p in pieces],
        in_specs=[any_spec] * k_arr, out_specs=[any_spec] * k_arr,
        scratch_shapes=[pltpu.SemaphoreType.DMA((k_arr, 3)), pltpu.SemaphoreType.DMA((k_arr, 3))],
    )(*pieces)


def _sibling_swap(arrs, name):
    k_arr = len(arrs)

    def body(*refs):
        srcs, dsts = refs[:k_arr], refs[k_arr:2 * k_arr]
        send_sems, recv_sems = refs[2 * k_arr:]
        x, y, c = _coords()
        copies = [pltpu.make_async_remote_copy(
            src_ref=srcs[a], dst_ref=dsts[a], send_sem=send_sems.at[a], recv_sem=recv_sems.at[a],
            device_id=(x, y, 1 - c), device_id_type=MESH) for a in range(k_arr)]
        for cp in copies:
            cp.start()
        for cp in copies:
            cp.wait_recv()
        for cp in copies:
            cp.wait_send()

    any_spec = pl.BlockSpec(memory_space=pl.ANY)
    return pl.pallas_call(
        body, name=name,
        out_shape=[jax.ShapeDtypeStruct(a.shape, a.dtype) for a in arrs],
        in_specs=[any_spec] * k_arr, out_specs=[any_spec] * k_arr,
        scratch_shapes=[pltpu.SemaphoreType.DMA((k_arr,)), pltpu.SemaphoreType.DMA((k_arr,))],
    )(*arrs)


def _split_w_in(w_in_t):
    wa = jnp.concatenate([w_in_t[0:1024], w_in_t[1040:1552]], axis=0)
    wb = w_in_t[1552:3088]
    wc = w_in_t[3096:3608]
    wd = jnp.concatenate([w_in_t[1024:1040], w_in_t[3088:3096],
                          jnp.zeros((128 - SMALL_USED, w_in_t.shape[1]), w_in_t.dtype)], axis=0)
    return wa, wb, wc, wd


def _merge_dw_in(dwa, dwb, dwc, dwd):
    return jnp.concatenate([dwa[0:1024], dwd[0:GLA_RANK], dwa[1024:1536], dwb, dwd[GLA_RANK:SMALL_USED], dwc,
                            jnp.zeros((GRAD_PAD_ROWS, dwa.shape[1]), dwa.dtype)], axis=0)


def _local_step(x, mod, w_in16, w_out16, gla_wg, gla_bg, gla_nw, conv_w, a_log, dt_bias, gdn_nw, ln_w, ln_b, tgt):
    bl, seq, _ = x.shape
    n = bl * seq
    x2 = x.reshape(n, D_MODEL)
    tgt2 = tgt.reshape(n, D_MODEL)
    sh3 = mod[:, None, 0:D_MODEL]
    sc3 = 1.0 + mod[:, None, D_MODEL:2 * D_MODEL]
    g1p3 = 1.0 + mod[:, None, 2 * D_MODEL:]
    ws = _split_w_in(w_in16)
    wg = jnp.concatenate([gla_wg, jnp.zeros((128 - GLA_RANK, GLA_QK), F32)], axis=0)
    cw8 = jnp.concatenate([conv_w, jnp.zeros((8 - CONV_K, conv_w.shape[1]), F32)], axis=0)
    alog_v = jnp.zeros((1, 128), F32).at[:, LANE_A:LANE_A + GDN_HEADS].set(a_log)
    dtb_v = jnp.zeros((1, 128), F32).at[:, LANE_A:LANE_A + GDN_HEADS].set(dt_bias)

    pa, pb, pc, pd = _proj_fwd(x2, sc3, sh3, ws, seq)
    ya, st_a = _gla_fwd(pa, pd, wg, gla_bg, gla_nw, bl, seq)
    qkv, gb, conv_out = _gdn_pre_fwd(pb, pd, cw8, alog_v, dtb_v, bl, seq)
    yb, st_b = _gdn_fwd(qkv, gb, pc, gdn_nw, bl, seq)
    dz, dya, dyb, d_wo, d_gate, d_lnw, d_lnb, loss = _out_block(x2, tgt2, ya, yb, g1p3, w_out16, ln_w, ln_b, seq)
    da, dd1, d_wg, d_bg, d_nwa = _gla_bwd(pa, pd, st_a, dya, wg, gla_bg, gla_nw, bl, seq)
    dqkv, dc, dgb, d_nwb = _gdn_bwd(qkv, gb, pc, st_b, dyb, gdn_nw, bl, seq)
    db, dd2, d_cw8, d_alog, d_dtb = _gdn_pre_bwd(pb, conv_out, pd, dqkv, dgb, cw8, alog_v, dtb_v, bl, seq)
    gx, d_sh, d_sc = _proj_bwd_x((da, db, dc, (dd1, dd2)), ws, x2, dz, sc3, seq)
    (dwa,) = _proj_bwd_w(x2, sc3, sh3, [da], seq, w_in16.dtype, "proj_bwd_w_a")
    dwb, dwc, dwd = _proj_bwd_w(x2, sc3, sh3, [db, dc, (dd1, dd2)], seq, w_in16.dtype, "proj_bwd_w_bcd")
    grads = dict(
        w_in=_merge_dw_in(dwa, dwb, dwc, dwd),
        w_out=d_wo,
        gla_w_gate_up=d_wg[0:GLA_RANK, :],
        gla_b_gate=d_bg,
        gla_norm_w=d_nwa,
        gdn_conv_w=d_cw8[0:CONV_K, :],
        gdn_a_log=d_alog[:, LANE_A:LANE_A + GDN_HEADS],
        gdn_dt_bias=d_dtb[:, LANE_A:LANE_A + GDN_HEADS],
        gdn_norm_w=d_nwb,
        ln_w=d_lnw,
        ln_b=d_lnb,
        mod=jnp.concatenate([d_sh[:, 0, :], d_sc[:, 0, :], d_gate[:, 0, :]], axis=1),
    )
    return loss, gx.reshape(bl, seq, D_MODEL), grads


_SMALL = (("gla_b_gate", 256), ("gla_norm_w", 128), ("gdn_a_log", 4), ("gdn_dt_bias", 4), ("gdn_norm_w", 128),
          ("ln_w", 1024), ("ln_b", 1024), ("gla_w_gate_up", 16 * 256), ("gdn_conv_w", 4 * 1536), ("loss", 1),
          ("mod", 2 * 3072))


def _pack_small(grads):
    flat = jnp.concatenate([grads[k].reshape(-1) for k, _ in _SMALL])
    total = sum(sz for _, sz in _SMALL)
    rows = -(-total // 1024) * 8
    return jnp.concatenate([flat, jnp.zeros((rows * 128 - total,), F32)]).reshape(rows, 128)


def _unpack_small(flat):
    out, pos = {}, 0
    for k, sz in _SMALL:
        out[k] = flat[pos:pos + sz]
        pos += sz
    return out


def kernel(x, c, w_ada, b_ada, w_in, gla_w_gate_up, gla_b_gate, gla_norm_w, gdn_conv_w, gdn_a_log, gdn_dt_bias, gdn_norm_w, w_out, ln_w, ln_b, loss_target, m_w_ada, m_b_ada, m_w_in, m_gla_w_gate_up, m_gla_b_gate, m_gla_norm_w, m_gdn_conv_w, m_gdn_a_log, m_gdn_dt_bias, m_gdn_norm_w, m_w_out, m_ln_w, m_ln_b, v_w_ada, v_b_ada, v_w_in, v_gla_w_gate_up, v_gla_b_gate, v_gla_norm_w, v_gdn_conv_w, v_gdn_a_log, v_gdn_dt_bias, v_gdn_norm_w, v_w_out, v_ln_w, v_ln_b):
    ix, iy, ic = _coords()
    chip = 2 * ix + iy
    dev = 4 * ix + 2 * iy + ic
    bl = x.shape[0]
    ndev = 8

    c_all = _all_gather8(c.reshape(8, -1), "gather_c").reshape(ndev * bl, D_MODEL)
    ada_cols = w_ada.shape[2]
    b_blk = lax.dynamic_slice_in_dim(b_ada, chip * ada_cols, ada_cols, axis=1)
    mod_blk = _mod_block(c_all, w_ada[0], b_blk)
    mod_g = _all_gather8(mod_blk, "gather_mod").reshape(ndev, ndev * bl, ada_cols)
    mod_all = jnp.concatenate([mod_g[2 * j] for j in range(4)], axis=1)
    mod = lax.dynamic_slice_in_dim(mod_all, dev * bl, bl, axis=0)

    w_in_g, w_out_g, wg_g, cw_g = _chip_gather(
        [jnp.transpose(w_in[0]).astype(BF16), w_out[0].astype(BF16), gla_w_gate_up[0], gdn_conv_w[0]],
        [True, True, False, False], "gather_weights")
    w_in16 = w_in_g.reshape(IN_COLS, D_MODEL)
    w_out16 = w_out_g.reshape(D_MODEL, D_MODEL)
    gla_wg = jnp.concatenate([wg_g[j] for j in range(4)], axis=1)
    conv_w = jnp.concatenate([cw_g[j] for j in range(4)], axis=1)

    loss, grad_x, gr = _local_step(x, mod, w_in16, w_out16, gla_wg, gla_b_gate, gla_norm_w, conv_w,
                                   gdn_a_log, gdn_dt_bias, gdn_norm_w, ln_w, ln_b, loss_target)

    gr["loss"] = loss
    packed = _pack_small(gr)
    prow = packed.shape[0]
    gathered = _all_gather8(packed, "gather_small").reshape(ndev, prow, 128)
    small = _unpack_small(_sum_leading(gathered, "sum_small").reshape(-1))
    loss = small["loss"][0]
    mod_rows = gathered.reshape(ndev, prow * 128)[:, sum(sz for _, sz in _SMALL[:-1]):][:, :bl * 3 * D_MODEL]
    dmod_all = mod_rows.reshape(ndev * bl, 3 * D_MODEL)
    dmod_blk = lax.dynamic_slice_in_dim(dmod_all, chip * ada_cols, ada_cols, axis=1)
    g_w_ada, g_b_ada = _ada_grads(c_all, dmod_all, dmod_blk)
    wg_cols = gla_w_gate_up.shape[2]
    g_wg = lax.dynamic_slice_in_dim(small["gla_w_gate_up"].reshape(GLA_RANK, GLA_QK), chip * wg_cols, wg_cols, axis=1)
    cw_cols = gdn_conv_w.shape[2]
    g_cw = lax.dynamic_slice_in_dim(small["gdn_conv_w"].reshape(CONV_K, 3 * GDN_WIDTH), chip * cw_cols, cw_cols, axis=1)

    in_feats = w_in.shape[2]
    out_rows = w_out.shape[1]
    p_in = gr["w_in"]
    p_out = gr["w_out"].reshape(4, out_rows, D_MODEL)
    h_in, h_out = D_MODEL // 2, out_rows // 2
    mine_in = lax.dynamic_slice_in_dim(p_in, ic * h_in, h_in, axis=1)
    mine_out = lax.dynamic_slice_in_dim(p_out, ic * h_out, h_out, axis=1)
    theirs_in = lax.dynamic_slice_in_dim(p_in, (1 - ic) * h_in, h_in, axis=1)
    theirs_out = lax.dynamic_slice_in_dim(p_out, (1 - ic) * h_out, h_out, axis=1)
    got_in, got_out = _sibling_swap([theirs_in, theirs_out], "swap_halves")
    chip_in, chip_in16 = _chip_sum_blocks(mine_in, got_in, in_feats, 4, "chip_sum_in")
    chip_out, chip_out16 = _add_n([mine_out.reshape(4 * h_out, D_MODEL), got_out.reshape(4 * h_out, D_MODEL)],
                                  "chip_sum_out", (F32, BF16))
    chip_out = chip_out.reshape(4, h_out, D_MODEL)
    rs_in, rs_out = _chip_scatter([chip_in16, chip_out16.reshape(4, h_out, D_MODEL)], "scatter_grads")
    own_in = lax.dynamic_index_in_dim(chip_in, chip, axis=0, keepdims=False)
    own_out = lax.dynamic_index_in_dim(chip_out, chip, axis=0, keepdims=False)
    (half_in,) = _add_n([own_in, rs_in[0], rs_in[1], rs_in[2]], "reduce_in")
    (half_out,) = _add_n([own_out, rs_out[0], rs_out[1], rs_out[2]], "reduce_out")
    sib_in, sib_out = _sibling_swap([half_in, half_out], "swap_result")
    g_w_in_t = jnp.where(ic == 0, jnp.concatenate([half_in, sib_in], axis=1),
                         jnp.concatenate([sib_in, half_in], axis=1))[0:in_feats]
    g_w_out = jnp.where(ic == 0, jnp.concatenate([half_out, sib_out], axis=0),
                        jnp.concatenate([sib_out, half_out], axis=0))

    grads = dict(
        w_ada=g_w_ada[None], b_ada=g_b_ada, w_in=g_w_in_t, gla_w_gate_up=g_wg[None],
        gla_b_gate=small["gla_b_gate"].reshape(1, -1), gla_norm_w=small["gla_norm_w"].reshape(1, -1),
        gdn_conv_w=g_cw[None], gdn_a_log=small["gdn_a_log"].reshape(1, -1),
        gdn_dt_bias=small["gdn_dt_bias"].reshape(1, -1), gdn_norm_w=small["gdn_norm_w"].reshape(1, -1),
        w_out=g_w_out[None], ln_w=small["ln_w"].reshape(1, -1), ln_b=small["ln_b"].reshape(1, -1))
    weights = dict(w_ada=w_ada, b_ada=b_ada, w_in=w_in, gla_w_gate_up=gla_w_gate_up, gla_b_gate=gla_b_gate,
                   gla_norm_w=gla_norm_w, gdn_conv_w=gdn_conv_w, gdn_a_log=gdn_a_log, gdn_dt_bias=gdn_dt_bias,
                   gdn_norm_w=gdn_norm_w, w_out=w_out, ln_w=ln_w, ln_b=ln_b)
    m_in = dict(w_ada=m_w_ada, b_ada=m_b_ada, w_in=m_w_in, gla_w_gate_up=m_gla_w_gate_up, gla_b_gate=m_gla_b_gate,
                gla_norm_w=m_gla_norm_w, gdn_conv_w=m_gdn_conv_w, gdn_a_log=m_gdn_a_log, gdn_dt_bias=m_gdn_dt_bias,
                gdn_norm_w=m_gdn_norm_w, w_out=m_w_out, ln_w=m_ln_w, ln_b=m_ln_b)
    v_in = dict(w_ada=v_w_ada, b_ada=v_b_ada, w_in=v_w_in, gla_w_gate_up=v_gla_w_gate_up, gla_b_gate=v_gla_b_gate,
                gla_norm_w=v_gla_norm_w, gdn_conv_w=v_gdn_conv_w, gdn_a_log=v_gdn_a_log, gdn_dt_bias=v_gdn_dt_bias,
                gdn_norm_w=v_gdn_norm_w, w_out=v_w_out, ln_w=v_ln_w, ln_b=v_ln_b)
    names = list(weights)
    delta, new_m, new_v = {}, {}, {}
    for nm in names:
        shp = weights[nm].shape
        if nm == "w_in":
            to2d = lambda t: jnp.transpose(t[0])
            from2d = lambda t: jnp.transpose(t)[None]
            g2d = grads[nm]
        else:
            to2d = lambda t: t.reshape(-1, shp[-1])
            from2d = lambda t: t.reshape(shp)
            g2d = to2d(grads[nm])
        d, a, b = _adamw(to2d(weights[nm]), g2d, to2d(m_in[nm]), to2d(v_in[nm]), "adamw_" + nm)
        delta[nm], new_m[nm], new_v[nm] = from2d(d), from2d(a), from2d(b)
        grads[nm] = from2d(g2d)
    return (loss, grad_x, *[grads[k] for k in names], *[delta[k] for k in names],
            *[new_m[k] for k in names], *[new_v[k] for k in names])
```

```python
import functools

import jax
import jax.numpy as jnp
from jax import lax
from jax.experimental import pallas as pl
from jax.experimental.pallas import tpu as pltpu

F32 = jnp.float32
BF16 = jnp.bfloat16
HI = lax.Precision.HIGH
INV_PREC = None
MESH = pl.DeviceIdType.MESH

D_MODEL = 1024
GLA_HEADS = 4
GLA_DK = 64
GLA_DV = 128
GLA_QK = 256
GLA_WIDTH = 512
GLA_RANK = 16
GLA_GATE_NORM = 16.0
GDN_HEADS = 4
GDN_DK = 128
GDN_WIDTH = 512
CONV_K = 4
CHUNK = 64
LN_EPS = 1e-5
RMS_EPS = 1e-6
ALPHA = 2.0 ** 0.25
IN_COLS = 3608

LANE_A = GLA_RANK
LANE_B = GLA_RANK + GDN_HEADS
SMALL_USED = GLA_RANK + 2 * GDN_HEADS

ADAM_LR = 0.001
ADAM_B1 = 0.9
ADAM_B2 = 0.999
ADAM_EPS = 1e-08
ADAM_WD = 0.01
ADAM_STEP = 10

VMEM_LIMIT = 56 * 1024 * 1024


def _iota(shape, dim):
    return lax.broadcasted_iota(jnp.int32, shape, dim)


def _dot(a, b, prec=None):
    return lax.dot_general(a, b, (((1,), (0,)), ((), ())), precision=prec, preferred_element_type=F32)


def _dot_nt(a, b, prec=None):
    return lax.dot_general(a, b, (((1,), (1,)), ((), ())), precision=prec, preferred_element_type=F32)


def _dot_tn(a, b, prec=None):
    return lax.dot_general(a, b, (((0,), (0,)), ((), ())), precision=prec, preferred_element_type=F32)


def _log_sigmoid(z):
    return jnp.minimum(z, 0.0) - jnp.log1p(jnp.exp(-jnp.abs(z)))


def _softplus(z):
    return jnp.maximum(z, 0.0) + jnp.log1p(jnp.exp(-jnp.abs(z)))


def _silu(z):
    return z * jax.nn.sigmoid(z)


def _rms_gate(o, nw, og):
    return o * lax.rsqrt(jnp.mean(o * o, axis=-1, keepdims=True) + RMS_EPS) * nw * _silu(og)


def _params(*sem):
    return pltpu.CompilerParams(dimension_semantics=sem, vmem_limit_bytes=VMEM_LIMIT)


GLA_PAIRS = GLA_HEADS // 2


def _gla_chunk(qs, ks, lrs, vs, ogs, ss, wgs, bgs, nw):
    c = qs[0].shape[0]
    n_ep = len(ss)
    n_ex = n_ep // GLA_PAIRS
    n_chunks = len(qs) // n_ep
    pair_units = [(i // n_ep * n_ex + i % n_ep // GLA_PAIRS, i % GLA_PAIRS) for i in range(len(qs))]
    head_units = [(i // GLA_HEADS * GLA_PAIRS + i % GLA_HEADS // 2, i % 2) for i in range(len(vs))]
    row, col = _iota((c, c), 0), _iota((c, c), 1)
    causal = row >= col
    first_half = (_iota((c, 1), 0) < c // 2).astype(F32)
    lane = _iota((1, 128), 1)
    masks = [(lane < GLA_DK).astype(F32), (lane >= GLA_DK).astype(F32)]
    gs = [_log_sigmoid(_dot(lrs[ce], wgs[p]) + bgs[p]) * (1.0 / GLA_GATE_NORM) for ce, p in pair_units]
    bs = [_dot(causal.astype(F32), g, HI) for g in gs]
    b_ref = [jnp.sum(g * first_half, axis=0, keepdims=True) for g in gs]
    b_last = [jnp.sum(g, axis=0, keepdims=True) for g in gs]
    qsc = [q * (GLA_DK ** -0.5) for q in qs]
    qe = [q * jnp.exp(b - br) for q, b, br in zip(qsc, bs, b_ref)]
    ke = [k * jnp.exp(br - b) for k, b, br in zip(ks, bs, b_ref)]
    qb = [q * jnp.exp(b) for q, b in zip(qsc, bs)]
    kd = [k * jnp.exp(bl_ - b) for k, b, bl_ in zip(ks, bs, b_last)]
    decay = [jnp.exp(bl_) for bl_ in b_last]
    att = [jnp.where(causal, _dot_nt(qe[u] * masks[half], ke[u]), 0.0) for u, half in head_units]
    o_intra = [_dot(a, v) for a, v in zip(att, vs)]
    qbm = [qb[u] * masks[half] for u, half in head_units]
    kdm = [kd[u] * masks[half] for u, half in head_units]
    ys = []
    for r in range(n_chunks):
        heads_r = range(r * n_ex * GLA_HEADS, (r + 1) * n_ex * GLA_HEADS)
        o_inter = [_dot_nt(qbm[i], ss[head_units[i][0] - r * n_ep]) for i in heads_r]
        upd = [_dot_tn(vs[i], kdm[i]) for i in heads_r]
        ss = [s * decay[r * n_ep + j] + upd[2 * j] + upd[2 * j + 1] for j, s in enumerate(ss)]
        ys += [_rms_gate(o_intra[i] + oi, nw, ogs[i]) for i, oi in zip(heads_r, o_inter)]
    return ys, ss


def _unit_lower_inverse_chain(a_list):
    c = a_list[0].shape[0]
    eye = (_iota((c, c), 0) == _iota((c, c), 1)).astype(F32)
    ps = [-a for a in a_list]
    ts = [eye + p for p in ps]
    levels = max(c.bit_length() - 2, 0)
    if levels:
        ps = [_dot(p, p, INV_PREC) for p in ps]
    for level in range(levels):
        last = level == levels - 1
        both = [_dot(t if last else jnp.concatenate([t, p], axis=0), p, INV_PREC) for t, p in zip(ts, ps)]
        ts = [t + m[0:c] for t, m in zip(ts, both)]
        if not last:
            ps = [m[c:2 * c] for m in both]
    return ts


@jax.custom_vjp
def _unit_lower_inverse(a_list):
    return _unit_lower_inverse_chain(a_list)


def _unit_lower_inverse_fwd(a_list):
    ts = _unit_lower_inverse_chain(a_list)
    return ts, ts


def _unit_lower_inverse_bwd(ts, dts):
    xs = [_dot_nt(dt, t, INV_PREC) for dt, t in zip(dts, ts)]
    return ([-_dot_tn(t, x, INV_PREC) for t, x in zip(ts, xs)],)


_unit_lower_inverse.defvjp(_unit_lower_inverse_fwd, _unit_lower_inverse_bwd)


@jax.custom_vjp
def _unit_lower_inverse_known(a_list, ts):
    return ts


def _unit_lower_inverse_known_fwd(a_list, ts):
    return ts, ts


def _unit_lower_inverse_known_bwd(ts, dts):
    return _unit_lower_inverse_bwd(ts, dts) + ([jnp.zeros_like(t) for t in ts],)


_unit_lower_inverse_known.defvjp(_unit_lower_inverse_known_fwd, _unit_lower_inverse_known_bwd)


def _gdn_prep_units(qs, ks, vs, gbs, t_known=None):
    c = qs[0].shape[0]
    units = [divmod(i, GDN_HEADS) for i in range(len(qs))]
    row, col = _iota((c, c), 0), _iota((c, c), 1)
    causal, strict = row >= col, row > col
    if t_known is None:
        lane = _iota((1, 128), 1)
        d_alls = [_dot(causal.astype(F32), gb, HI) for gb in gbs]
        g_c, beta_c, d_c = [], [], []
        for r, h in units:
            sel_a = (lane == LANE_A + h).astype(F32)
            g_c.append(jnp.sum(gbs[r] * sel_a, axis=-1, keepdims=True))
            beta_c.append(jnp.sum(gbs[r] * (lane == LANE_B + h).astype(F32), axis=-1, keepdims=True))
            d_c.append(jnp.sum(d_alls[r] * sel_a, axis=-1, keepdims=True))
        d_diff = [jnp.broadcast_to(d, (c, c)) - jnp.broadcast_to(d, (c, c)).T for d in d_c]
    else:
        src = _iota((128, 128), 0)
        spread = jnp.concatenate([(src == base + h).astype(F32) for base in (LANE_A, LANE_B)
                                  for h in range(GDN_HEADS)], axis=1)
        width = GDN_HEADS * 128
        g_beta = [_dot(gb, spread, HI) for gb in gbs]
        d_alls = [_dot(causal.astype(F32), gbv[:, 0:width], HI) for gbv in g_beta]
        g_c = [g_beta[r][:, h * 128:(h + 1) * 128] for r, h in units]
        beta_c = [g_beta[r][:, width + h * 128:width + (h + 1) * 128] for r, h in units]
        d_c = [d_alls[r][:, h * 128:(h + 1) * 128] for r, h in units]
        d_diff = [d[:, 0:c] - d.T[0:c, :] for d in d_c]
    d_last = [jnp.sum(g, axis=0, keepdims=True) for g in g_c]
    decay_mat = [jnp.where(causal, jnp.exp(jnp.where(causal, dd, 0.0)), 0.0) for dd in d_diff]
    kb = [k * b for k, b in zip(ks, beta_c)]
    kbk_qk = [_dot_nt(jnp.concatenate([kbi, q], axis=0), k) for kbi, q, k in zip(kb, qs, ks)]
    a = [jnp.where(strict, m[0:c] * dm, 0.0) for m, dm in zip(kbk_qk, decay_mat)]
    qk = [jnp.where(causal, m[c:2 * c] * dm, 0.0) for m, dm in zip(kbk_qk, decay_mat)]
    t = _unit_lower_inverse(a) if t_known is None else _unit_lower_inverse_known(a, t_known)
    uw = [_dot(ti, jnp.concatenate([v * b, kbi * jnp.exp(d)], axis=1))
          for ti, v, b, kbi, d in zip(t, vs, beta_c, kb, d_c)]
    u = [m[:, 0:128] for m in uw]
    w = [m[:, 128:256] for m in uw]
    q_dec = [q * jnp.exp(d) for q, d in zip(qs, d_c)]
    k_dec = [k * jnp.exp(dl - d) for k, dl, d in zip(ks, d_last, d_c)]
    gamma = [jnp.exp(dl) for dl in d_last]
    return u, w, qk, q_dec, k_dec, gamma, t


def _sum_rows(t):
    return jnp.sum(t, axis=0, keepdims=True)


def _gdn_pre_elem(ps, ab, alog_v, dtb_v):
    outs = []
    for j, p in enumerate(ps):
        s = _silu(p)
        if j < 2 * GDN_HEADS:
            s = s * lax.rsqrt(jnp.sum(s * s, axis=-1, keepdims=True) + RMS_EPS)
        if j < GDN_HEADS:
            s = s * (GDN_DK ** -0.5)
        outs.append(s)
    lane = _iota((1, 128), 1)
    is_a = (lane >= LANE_A) & (lane < LANE_A + GDN_HEADS)
    is_b = (lane >= LANE_B) & (lane < LANE_B + GDN_HEADS)
    g = -jnp.exp(alog_v) * _softplus(ab + dtb_v)
    gb = jnp.where(is_a, g, jnp.where(is_b, jax.nn.sigmoid(ab), 0.0))
    return tuple(outs) + (gb,)


def _proj_fwd(x2, sc3, sh3, ws, seq, tm=256):
    n = x2.shape[0]
    tpe = seq // tm
    nw = len(ws)

    def body(x_ref, sc_ref, sh_ref, *refs):
        h = (x_ref[...] * sc_ref[0] + sh_ref[0]).astype(ws[0].dtype)
        for w_ref, o_ref in zip(refs[:nw], refs[nw:]):
            o_ref[...] = _dot_nt(h, w_ref[...])

    row = lambda i: (i, 0)
    per_ex = pl.BlockSpec((1, 1, D_MODEL), lambda i: (i // tpe, 0, 0))
    return pl.pallas_call(
        body, name="proj_fwd", grid=(n // tm,),
        in_specs=[pl.BlockSpec((tm, D_MODEL), row), per_ex, per_ex]
        + [pl.BlockSpec(w.shape, lambda i: (0, 0)) for w in ws],
        out_specs=[pl.BlockSpec((tm, w.shape[0]), row) for w in ws],
        out_shape=[jax.ShapeDtypeStruct((n, w.shape[0]), F32) for w in ws],
        compiler_params=_params("parallel"),
    )(x2, sc3, sh3, *ws)


GLA_SCAN_CHUNKS = 4


def _gla_operands(q_ref, k_ref, v_ref, og_ref, lr_ref, wg_ref, bg_ref, bl, r_per):
    chunks = [slice(r * CHUNK, (r + 1) * CHUNK) for r in range(r_per)]
    pair_cols = [slice(p * 128, (p + 1) * 128) for p in range(GLA_PAIRS)]
    head_cols = [slice(h * 128, (h + 1) * 128) for h in range(GLA_HEADS)]
    per_pair = lambda ref: [ref[e, rows, cols] for rows in chunks for e in range(bl) for cols in pair_cols]
    per_head = lambda ref: [ref[e, rows, cols] for rows in chunks for e in range(bl) for cols in head_cols]
    return (per_pair(q_ref), per_pair(k_ref), [lr_ref[e, rows, :] for rows in chunks for e in range(bl)],
            per_head(v_ref), per_head(og_ref)), ([wg_ref[:, cols] for cols in pair_cols],
                                                 [bg_ref[:, cols] for cols in pair_cols])


def _gla_fwd(pa, pd, wg, bg, nw, bl, seq):
    n = pa.shape[0]
    nc = seq // CHUNK
    r_per = GLA_SCAN_CHUNKS
    pairs = [(e, p) for e in range(bl) for p in range(GLA_PAIRS)]
    head_slots = [(slice(r * CHUNK, (r + 1) * CHUNK), e, slice(h * 128, (h + 1) * 128))
                  for r in range(r_per) for e in range(bl) for h in range(GLA_HEADS)]

    def body(q_ref, k_ref, v_ref, og_ref, lr_ref, wg_ref, bg_ref, nw_ref, y_ref, st_ref, s_scr):
        @pl.when(pl.program_id(0) == 0)
        def _():
            s_scr[...] = jnp.zeros_like(s_scr)

        ss = [s_scr[e, p] for e, p in pairs]
        for (e, p), s in zip(pairs, ss):
            st_ref[e, 0, p] = s
        acts, gate = _gla_operands(q_ref, k_ref, v_ref, og_ref, lr_ref, wg_ref, bg_ref, bl, r_per)
        ys, s_new = _gla_chunk(*acts, ss, *gate, nw_ref[...])
        for (rows, e, cols), y in zip(head_slots, ys):
            y_ref[e, rows, cols] = y.astype(y_ref.dtype)
        for (e, p), s in zip(pairs, s_new):
            s_scr[e, p] = s

    tok = lambda w, j: pl.BlockSpec((bl, r_per * CHUNK, w), lambda i: (0, i, j))
    const = lambda i: (0, 0)
    pa3 = pa.reshape(bl, seq, 1536)
    y, st = pl.pallas_call(
        body, name="gla_fwd", grid=(nc // r_per,),
        in_specs=[tok(256, 0), tok(256, 1), tok(512, 1), tok(512, 2), tok(128, 0),
                  pl.BlockSpec(wg.shape, const), pl.BlockSpec(bg.shape, const), pl.BlockSpec(nw.shape, const)],
        out_specs=[tok(512, 0), pl.BlockSpec((bl, 1, GLA_PAIRS, 128, 128), lambda i: (0, i, 0, 0, 0))],
        out_shape=[jax.ShapeDtypeStruct((bl, seq, 512), MM_DTYPE),
                   jax.ShapeDtypeStruct((bl, nc // r_per, GLA_PAIRS, 128, 128), F32)],
        scratch_shapes=[pltpu.VMEM((bl, GLA_PAIRS, 128, 128), F32)],
        compiler_params=_params("arbitrary"),
    )(pa3, pa3, pa3, pa3, pd.reshape(bl, seq, 128), wg, bg, nw)
    return y.reshape(n, 512), st


def _gla_bwd(pa, pd, st, dya, wg, bg, nw, bl, seq):
    n = pa.shape[0]
    nc = seq // CHUNK
    r_per = GLA_SCAN_CHUNKS
    steps = nc // r_per
    pairs = [(e, p) for e in range(bl) for p in range(GLA_PAIRS)]
    pair_cols = [slice(p * 128, (p + 1) * 128) for p in range(GLA_PAIRS)]
    chunks = [slice(r * CHUNK, (r + 1) * CHUNK) for r in range(r_per)]
    pair_slots = [(rows, e, p) for rows in chunks for e in range(bl) for p in range(GLA_PAIRS)]
    head_slots = [(rows, e, h) for rows in chunks for e in range(bl) for h in range(GLA_HEADS)]

    def body(q_ref, k_ref, v_ref, og_ref, lr_ref, st_ref, dy_ref, wg_ref, bg_ref, nw_ref,
             da_ref, dd_ref, dwg_ref, dbg_ref, dnw_ref, ds_scr):
        @pl.when(pl.program_id(0) == 0)
        def _():
            dwg_ref[...] = jnp.zeros_like(dwg_ref)
            dbg_ref[...] = jnp.zeros_like(dbg_ref)
            dnw_ref[...] = jnp.zeros_like(dnw_ref)
            ds_scr[...] = jnp.zeros_like(ds_scr)

        acts, gate = _gla_operands(q_ref, k_ref, v_ref, og_ref, lr_ref, wg_ref, bg_ref, bl, r_per)
        _, vjp = jax.vjp(_gla_chunk, *acts, [st_ref[e, 0, p] for e, p in pairs], *gate, nw_ref[...])
        dq, dk, dlr, dv, dog, ds, dwg, dbg, dnw = vjp(
            ([dy_ref[e, rows, h * 128:(h + 1) * 128] for rows, e, h in head_slots], [ds_scr[e, p] for e, p in pairs]))
        for i, (rows, e) in enumerate((rows, e) for rows in chunks for e in range(bl)):
            dd_ref[e, rows, :] = dlr[i]
        for i, (rows, e, p) in enumerate(pair_slots):
            da_ref[e, rows, pair_cols[p]] = dq[i].astype(da_ref.dtype)
            da_ref[e, rows, GLA_QK + p * 128:GLA_QK + (p + 1) * 128] = dk[i].astype(da_ref.dtype)
        for i, (rows, e, h) in enumerate(head_slots):
            da_ref[e, rows, 512 + h * 128:512 + (h + 1) * 128] = dv[i].astype(da_ref.dtype)
            da_ref[e, rows, 1024 + h * 128:1024 + (h + 1) * 128] = dog[i].astype(da_ref.dtype)
        for (e, p), d in zip(pairs, ds):
            ds_scr[e, p] = d
        for p, cols in enumerate(pair_cols):
            dwg_ref[:, cols] += dwg[p]
            dbg_ref[:, cols] += dbg[p]
        dnw_ref[...] += dnw

    tok = lambda w, j: pl.BlockSpec((bl, r_per * CHUNK, w), lambda i: (0, steps - 1 - i, j))
    const = lambda i: (0, 0)
    pa3 = pa.reshape(bl, seq, 1536)
    da, dd, dwg, dbg, dnw = pl.pallas_call(
        body, name="gla_bwd", grid=(steps,),
        in_specs=[tok(256, 0), tok(256, 1), tok(512, 1), tok(512, 2), tok(128, 0),
                  pl.BlockSpec((bl, 1, GLA_PAIRS, 128, 128), lambda i: (0, steps - 1 - i, 0, 0, 0)), tok(512, 0),
                  pl.BlockSpec(wg.shape, const), pl.BlockSpec(bg.shape, const), pl.BlockSpec(nw.shape, const)],
        out_specs=[tok(1536, 0), tok(128, 0),
                   pl.BlockSpec(wg.shape, const), pl.BlockSpec(bg.shape, const), pl.BlockSpec(nw.shape, const)],
        out_shape=[jax.ShapeDtypeStruct((bl, seq, 1536), MM_DTYPE), jax.ShapeDtypeStruct((bl, seq, 128), F32),
                   jax.ShapeDtypeStruct(wg.shape, F32), jax.ShapeDtypeStruct(bg.shape, F32),
                   jax.ShapeDtypeStruct(nw.shape, F32)],
        scratch_shapes=[pltpu.VMEM((bl, GLA_PAIRS, 128, 128), F32)],
        compiler_params=_params("arbitrary"),
    )(pa3, pa3, pa3, pa3, pd.reshape(bl, seq, 128), st, dya.reshape(bl, seq, 512), wg, bg, nw)
    return da.reshape(n, 1536), dd.reshape(n, 128), dwg, dbg, dnw


def _conv_taps(buf_ref, w_ref, base, rows):
    acc = w_ref[0:1, :] * buf_ref[pl.ds(base, rows), :]
    for k in range(1, CONV_K):
        acc = acc + w_ref[k:k + 1, :] * buf_ref[pl.ds(base + k, rows), :]
    return acc


def _gdn_pre_fwd(pb, pd, cw8, alog_v, dtb_v, bl, seq, tm=256):
    n = pb.shape[0]
    tpe = seq // tm
    t8 = tm // 8

    def body(u_ref, prev_ref, ab_ref, w_ref, al_ref, dt_ref, qkv_ref, gb_ref, p_ref, buf):
        i = pl.program_id(0)
        keep = (i % tpe != 0).astype(F32)
        buf[0:8, :] = prev_ref[...] * keep
        buf[8:8 + tm, :] = u_ref[...]
        p = _conv_taps(buf, w_ref, 8 - (CONV_K - 1), tm)
        p_ref[...] = p
        ps = [p[:, j * 128:(j + 1) * 128] for j in range(12)]
        outs = _gdn_pre_elem(ps, ab_ref[...], al_ref[...], dt_ref[...])
        for j in range(12):
            qkv_ref[:, j * 128:(j + 1) * 128] = outs[j]
        gb_ref[...] = outs[12]

    row = lambda i: (i, 0)
    const = lambda i: (0, 0)
    return pl.pallas_call(
        body, name="gdn_pre_fwd", grid=(n // tm,),
        in_specs=[pl.BlockSpec((tm, 1536), row),
                  pl.BlockSpec((8, 1536), lambda i: (jnp.maximum(i * t8 - 1, 0), 0)),
                  pl.BlockSpec((tm, 128), row),
                  pl.BlockSpec((8, 1536), const), pl.BlockSpec((1, 128), const), pl.BlockSpec((1, 128), const)],
        out_specs=[pl.BlockSpec((tm, 1536), row), pl.BlockSpec((tm, 128), row), pl.BlockSpec((tm, 1536), row)],
        out_shape=[jax.ShapeDtypeStruct((n, 1536), F32), jax.ShapeDtypeStruct((n, 128), F32),
                   jax.ShapeDtypeStruct((n, 1536), F32)],
        scratch_shapes=[pltpu.VMEM((tm + 8, 1536), F32)],
        compiler_params=_params("parallel"),
    )(pb, pb, pd, cw8, alog_v, dtb_v)


def _gdn_pre_bwd(pb, conv_out, pd, dqkv, dgb, cw8, alog_v, dtb_v, bl, seq, tm=256):
    n = pb.shape[0]
    tpe = seq // tm
    t8 = tm // 8
    nb8 = n // 8
    ext = tm + 8

    def body(u_ref, p_ref, pn_ref, ab_ref, abn_ref, dq_ref, dqn_ref, dgb_ref, w_ref, al_ref, dt_ref,
             du_ref, dab_ref, dw_ref, dal_ref, ddt_ref, dpbuf):
        i = pl.program_id(0)

        @pl.when(i == 0)
        def _():
            dw_ref[...] = jnp.zeros_like(dw_ref)
            dal_ref[...] = jnp.zeros_like(dal_ref)
            ddt_ref[...] = jnp.zeros_like(ddt_ref)

        keep_next = (i % tpe != tpe - 1).astype(F32)
        ps = [jnp.concatenate([p_ref[:, j * 128:(j + 1) * 128], pn_ref[:, j * 128:(j + 1) * 128]], axis=0)
              for j in range(12)]
        ab = jnp.concatenate([ab_ref[...], abn_ref[...]], axis=0)
        _, vjp = jax.vjp(_gdn_pre_elem, ps, ab, al_ref[...], dt_ref[...])
        zeros8 = jnp.zeros((8, 128), F32)
        cts = tuple(jnp.concatenate([dq_ref[:, j * 128:(j + 1) * 128],
                                     dqn_ref[:, j * 128:(j + 1) * 128] * keep_next], axis=0) for j in range(12))
        cts += (jnp.concatenate([dgb_ref[...], zeros8], axis=0),)
        dps, dab, dal, ddt = vjp(cts)
        for j in range(12):
            dpbuf[:, j * 128:(j + 1) * 128] = dps[j]
        dab_ref[...] = dab[0:tm, :]
        dal_ref[...] += dal
        ddt_ref[...] += ddt
        u = u_ref[...]
        du = None
        for k in range(CONV_K):
            dp_k = dpbuf[pl.ds(CONV_K - 1 - k, tm), :]
            term = w_ref[k:k + 1, :] * dp_k
            du = term if du is None else du + term
            dw_ref[k:k + 1, :] += jnp.sum(u * dp_k, axis=0, keepdims=True)
        du_ref[...] = du.astype(du_ref.dtype)

    row = lambda i: (i, 0)
    next8 = lambda i: (jnp.minimum((i + 1) * t8, nb8 - 1), 0)
    const = lambda i: (0, 0)
    return pl.pallas_call(
        body, name="gdn_pre_bwd", grid=(n // tm,),
        in_specs=[pl.BlockSpec((tm, 1536), row), pl.BlockSpec((tm, 1536), row), pl.BlockSpec((8, 1536), next8),
                  pl.BlockSpec((tm, 128), row), pl.BlockSpec((8, 128), next8),
                  pl.BlockSpec((tm, 1536), row), pl.BlockSpec((8, 1536), next8),
                  pl.BlockSpec((tm, 128), row),
                  pl.BlockSpec((8, 1536), const), pl.BlockSpec((1, 128), const), pl.BlockSpec((1, 128), const)],
        out_specs=[pl.BlockSpec((tm, 1536), row), pl.BlockSpec((tm, 128), row),
                   pl.BlockSpec((8, 1536), const), pl.BlockSpec((1, 128), const), pl.BlockSpec((1, 128), const)],
        out_shape=[jax.ShapeDtypeStruct((n, 1536), MM_DTYPE), jax.ShapeDtypeStruct((n, 128), F32),
                   jax.ShapeDtypeStruct((8, 1536), F32), jax.ShapeDtypeStruct((1, 128), F32),
                   jax.ShapeDtypeStruct((1, 128), F32)],
        scratch_shapes=[pltpu.VMEM((ext, 1536), F32)],
        compiler_params=_params("arbitrary"),
    )(pb, conv_out, conv_out, pd, pd, dqkv, dqkv, dgb, cw8, alog_v, dtb_v)


GDN_PREP_CHUNKS = 4
GDN_PREP_BWD_CHUNKS = 4
GDN_SCAN_CHUNKS = 2
MM_DTYPE = BF16


def _head_cols(ref, rows, base=0):
    return [ref[rows, base + h * 128:base + (h + 1) * 128] for h in range(GDN_HEADS)]


def _gdn_prep(qkv, gb):
    n = qkv.shape[0]
    r_per = GDN_PREP_CHUNKS
    tm = r_per * CHUNK

    def body(q_ref, k_ref, v_ref, gb_ref, u_ref, w_ref, qd_ref, kd_ref, qk_ref, t_ref, gam_ref):
        rowid = _iota((8, 128), 0)
        chunk_rows = [slice(r * CHUNK, (r + 1) * CHUNK) for r in range(r_per)]
        gather = lambda ref: [t for rows in chunk_rows for t in _head_cols(ref, rows)]
        u, w, qk, qd, kd, gamma, tinv = _gdn_prep_units(gather(q_ref), gather(k_ref), gather(v_ref),
                                                        [gb_ref[rows, :] for rows in chunk_rows])
        for r, rows in enumerate(chunk_rows):
            gam = jnp.zeros((8, 128), F32)
            for h in range(GDN_HEADS):
                i = r * GDN_HEADS + h
                cols = slice(h * 128, (h + 1) * 128)
                u_ref[rows, cols] = u[i]
                w_ref[rows, cols] = w[i].astype(MM_DTYPE)
                qd_ref[rows, cols] = qd[i].astype(MM_DTYPE)
                kd_ref[rows, cols] = kd[i].astype(MM_DTYPE)
                qk_ref[r, h] = qk[i].astype(MM_DTYPE)
                t_ref[r, h] = tinv[i].astype(MM_DTYPE)
                gam = jnp.where(rowid == h, gamma[i], gam)
            gam_ref[r] = gam

    tok = lambda j: pl.BlockSpec((tm, 512), lambda i: (i, j))
    return pl.pallas_call(
        body, name="gdn_prep", grid=(n // tm,),
        in_specs=[tok(0), tok(1), tok(2), pl.BlockSpec((tm, 128), lambda i: (i, 0))],
        out_specs=[tok(0)] * 4 + [pl.BlockSpec((r_per, GDN_HEADS, CHUNK, CHUNK), lambda i: (i, 0, 0, 0))] * 2
        + [pl.BlockSpec((r_per, 8, 128), lambda i: (i, 0, 0))],
        out_shape=[jax.ShapeDtypeStruct((n, 512), F32)] + [jax.ShapeDtypeStruct((n, 512), MM_DTYPE)] * 3
        + [jax.ShapeDtypeStruct((n // CHUNK, GDN_HEADS, CHUNK, CHUNK), MM_DTYPE)] * 2
        + [jax.ShapeDtypeStruct((n // CHUNK, 8, 128), F32)],
        compiler_params=_params("parallel"),
    )(qkv, qkv, qkv, gb)


def _gdn_fwd(qkv, gb, pc, nw, bl, seq):
    n = qkv.shape[0]
    nc = seq // CHUNK
    u, w, qd, kd, qk, tinv, gam = _gdn_prep(qkv, gb)
    tok3 = lambda t: t.reshape(bl, seq, 512)
    qk5 = qk.reshape(bl, nc, GDN_HEADS, CHUNK, CHUNK)
    gam4 = gam.reshape(bl, nc, 8, 128)

    r_per = GDN_SCAN_CHUNKS
    mm = lambda t: t.astype(MM_DTYPE)

    def body(u_ref, w_ref, qd_ref, kd_ref, qk_ref, gam_ref, og_ref, nw_ref, o_ref, y_ref, vn_ref, st_ref, s_scr):
        @pl.when(pl.program_id(0) == 0)
        def _():
            s_scr[...] = jnp.zeros_like(s_scr)

        units = [(b, h, slice(h * 128, (h + 1) * 128)) for b in range(bl) for h in range(GDN_HEADS)]
        ss = [s_scr[b, h] for b, h, _ in units]
        for r in range(r_per):
            rows = slice(r * CHUNK, (r + 1) * CHUNK)
            for (b, h, _), s in zip(units, ss):
                st_ref[b, r, h] = s
            ws_qs = [_dot(jnp.concatenate([w_ref[b, rows, cols], qd_ref[b, rows, cols]], axis=0), mm(s))
                     for (b, h, cols), s in zip(units, ss)]
            v_new = [u_ref[b, rows, cols] - m[0:CHUNK] for (b, h, cols), m in zip(units, ws_qs)]
            os_ = [m[CHUNK:2 * CHUNK] + _dot(qk_ref[b, r, h], mm(vn))
                   for (b, h, cols), m, vn in zip(units, ws_qs, v_new)]
            ss = [s * gam_ref[b, r, h:h + 1, :] + _dot_tn(kd_ref[b, rows, cols], mm(vn))
                  for (b, h, cols), s, vn in zip(units, ss, v_new)]
            for (b, h, cols), vn, o in zip(units, v_new, os_):
                vn_ref[b, rows, cols] = mm(vn)
                o_ref[b, rows, cols] = o
                y_ref[b, rows, cols] = mm(_rms_gate(o, nw_ref[...], og_ref[b, rows, cols]))
        for (b, h, _), s in zip(units, ss):
            s_scr[b, h] = s

    tok = pl.BlockSpec((bl, r_per * CHUNK, 512), lambda i: (0, i, 0))
    st_spec = pl.BlockSpec((bl, r_per, GDN_HEADS, 128, 128), lambda i: (0, i, 0, 0, 0))
    tok_shape = jax.ShapeDtypeStruct((bl, seq, 512), F32)
    o, y, vn, st = pl.pallas_call(
        body, name="gdn_scan_fwd", grid=(nc // r_per,),
        in_specs=[tok, tok, tok, tok,
                  pl.BlockSpec((bl, r_per, GDN_HEADS, CHUNK, CHUNK), lambda i: (0, i, 0, 0, 0)),
                  pl.BlockSpec((bl, r_per, 8, 128), lambda i: (0, i, 0, 0)), tok,
                  pl.BlockSpec(nw.shape, lambda i: (0, 0))],
        out_specs=[tok, tok, tok, st_spec],
        out_shape=[tok_shape, jax.ShapeDtypeStruct((bl, seq, 512), MM_DTYPE), jax.ShapeDtypeStruct((bl, seq, 512), MM_DTYPE),
                   jax.ShapeDtypeStruct((bl, nc, GDN_HEADS, 128, 128), F32)],
        scratch_shapes=[pltpu.VMEM((bl, GDN_HEADS, 128, 128), F32)],
        compiler_params=_params("arbitrary"),
    )(tok3(u), tok3(w), tok3(qd), tok3(kd), qk5, gam4, tok3(pc), nw)
    return y.reshape(n, 512), (o, st, w, qd, kd, qk5, gam4, tinv, vn)


def _gdn_bwd(qkv, gb, pc, res, dyb, nw, bl, seq):
    n = qkv.shape[0]
    nc = seq // CHUNK
    o, st, w, qd, kd, qk5, gam4, tinv, vn = res
    tok3 = lambda t: t.reshape(bl, seq, 512)

    def scan_body(dy_ref, o_ref, og_ref, w_ref, qd_ref, kd_ref, qk_ref, gam_ref, nw_ref,
                  do_ref, dog_ref, dvn_ref, dst_ref, dnw_ref, ds_scr):
        @pl.when(pl.program_id(0) == 0)
        def _():
            ds_scr[...] = jnp.zeros_like(ds_scr)
            dnw_ref[...] = jnp.zeros_like(dnw_ref)

        units = [(b, h, slice(h * 128, (h + 1) * 128)) for b in range(bl) for h in range(GDN_HEADS)]
        dnw = jnp.zeros(nw.shape, F32)
        dss = [ds_scr[b, h] for b, h, _ in units]
        for r in reversed(range(r_scan)):
            rows = slice(r * CHUNK, (r + 1) * CHUNK)
            d_os = []
            for b, h, cols in units:
                _, vjp = jax.vjp(_rms_gate, o_ref[b, rows, cols], nw_ref[...], og_ref[b, rows, cols])
                d_o, dnw_h, dog = vjp(dy_ref[b, rows, cols])
                do_ref[b, rows, cols] = mm(d_o)
                dog_ref[b, rows, cols] = mm(dog)
                dnw = dnw + dnw_h
                d_os.append(mm(d_o))
            for (b, h, _), ds in zip(units, dss):
                dst_ref[b, r, h] = ds
            dvn_a = [_dot(kd_ref[b, rows, cols], mm(ds)) for (b, h, cols), ds in zip(units, dss)]
            dvns = [a + _dot_tn(qk_ref[b, r, h], d_o) for (b, h, cols), a, d_o in zip(units, dvn_a, d_os)]
            for (b, h, cols), dvn in zip(units, dvns):
                dvn_ref[b, rows, cols] = mm(dvn)
            dss = [ds * gam_ref[b, r, h:h + 1, :] + _dot_tn(
                jnp.concatenate([qd_ref[b, rows, cols], w_ref[b, rows, cols]], axis=0),
                jnp.concatenate([d_o, mm(-dvn)], axis=0))
                for (b, h, cols), d_o, ds, dvn in zip(units, d_os, dss, dvns)]
        dnw_ref[...] += dnw
        for (b, h, _), ds in zip(units, dss):
            ds_scr[b, h] = ds

    r_scan = GDN_SCAN_CHUNKS
    mm = lambda t: t.astype(MM_DTYPE)
    rev = lambda i: nc // r_scan - 1 - i
    tok = pl.BlockSpec((bl, r_scan * CHUNK, 512), lambda i: (0, rev(i), 0))
    st_spec = pl.BlockSpec((bl, r_scan, GDN_HEADS, 128, 128), lambda i: (0, rev(i), 0, 0, 0))
    tok_shape = jax.ShapeDtypeStruct((bl, seq, 512), F32)
    tok_mm = jax.ShapeDtypeStruct((bl, seq, 512), MM_DTYPE)
    d_o, dog, dvn, dst, dnw = pl.pallas_call(
        scan_body, name="gdn_scan_bwd", grid=(nc // r_scan,),
        in_specs=[tok] * 6 + [pl.BlockSpec((bl, r_scan, GDN_HEADS, CHUNK, CHUNK), lambda i: (0, rev(i), 0, 0, 0)),
                              pl.BlockSpec((bl, r_scan, 8, 128), lambda i: (0, rev(i), 0, 0)),
                              pl.BlockSpec(nw.shape, lambda i: (0, 0))],
        out_specs=[tok, tok, tok, st_spec, pl.BlockSpec(nw.shape, lambda i: (0, 0))],
        out_shape=[tok_mm, tok_mm, tok_mm, jax.ShapeDtypeStruct(st.shape, F32),
                   jax.ShapeDtypeStruct(nw.shape, F32)],
        scratch_shapes=[pltpu.VMEM((bl, GDN_HEADS, 128, 128), F32)],
        compiler_params=_params("arbitrary"),
    )(tok3(dyb), o, tok3(pc), tok3(w), tok3(qd), tok3(kd), qk5, gam4, nw)

    r_per = GDN_PREP_BWD_CHUNKS
    tm = r_per * CHUNK

    def prep_body(q_ref, k_ref, v_ref, gb_ref, t_ref, st_ref, dst_ref, dvn_ref, do_ref, vn_ref, dqkv_ref, dgb_ref):
        chunk_rows = [slice(r * CHUNK, (r + 1) * CHUNK) for r in range(r_per)]
        gather = lambda ref: [t for rows in chunk_rows for t in _head_cols(ref, rows)]
        units = [(r, h) for r in range(r_per) for h in range(GDN_HEADS)]
        t_known = [t_ref[r, h].astype(F32) for r, h in units]
        prep = lambda q, k, v, g: _gdn_prep_units(q, k, v, g, t_known)[:6]
        _, vjp = jax.vjp(prep, gather(q_ref), gather(k_ref), gather(v_ref), [gb_ref[rows, :] for rows in chunk_rows])
        ss = [st_ref[r, h] for r, h in units]
        dss = [dst_ref[r, h] for r, h in units]
        dvns, d_os, v_new = gather(dvn_ref), gather(do_ref), gather(vn_ref)
        both = [_dot_nt(jnp.concatenate([dvn, d_o], axis=0), s.astype(MM_DTYPE)) for dvn, d_o, s in zip(dvns, d_os, ss)]
        d_w = [-m[0:CHUNK] for m in both]
        d_qd = [m[CHUNK:2 * CHUNK] for m in both]
        d_qk = [_dot_nt(d_o, vn) for d_o, vn in zip(d_os, v_new)]
        d_kd = [_dot_nt(vn, ds.astype(MM_DTYPE)) for vn, ds in zip(v_new, dss)]
        d_gam = [_sum_rows(ds * s) for ds, s in zip(dss, ss)]
        dq, dk, dv, dgb = vjp(([d.astype(F32) for d in dvns], d_w, d_qk, d_qd, d_kd, d_gam))
        for i, (r, h) in enumerate(units):
            rows = chunk_rows[r]
            for part, d in enumerate((dq, dk, dv)):
                dqkv_ref[rows, part * 512 + h * 128:part * 512 + (h + 1) * 128] = d[i]
        for r, rows in enumerate(chunk_rows):
            dgb_ref[rows, :] = dgb[r]

    tokp = lambda j: pl.BlockSpec((tm, 512), lambda i: (i, j))
    st4 = pl.BlockSpec((r_per, GDN_HEADS, 128, 128), lambda i: (i, 0, 0, 0))
    dqkv, dgb = pl.pallas_call(
        prep_body, name="gdn_prep_bwd", grid=(n // tm,),
        in_specs=[tokp(0), tokp(1), tokp(2), pl.BlockSpec((tm, 128), lambda i: (i, 0)),
                  pl.BlockSpec((r_per, GDN_HEADS, CHUNK, CHUNK), lambda i: (i, 0, 0, 0)), st4, st4,
                  tokp(0), tokp(0), tokp(0)],
        out_specs=[pl.BlockSpec((tm, 1536), lambda i: (i, 0)), pl.BlockSpec((tm, 128), lambda i: (i, 0))],
        out_shape=[jax.ShapeDtypeStruct((n, 1536), F32), jax.ShapeDtypeStruct((n, 128), F32)],
        compiler_params=_params("parallel"),
    )(qkv, qkv, qkv, gb, tinv, st.reshape(bl * nc, GDN_HEADS, 128, 128), dst.reshape(bl * nc, GDN_HEADS, 128, 128),
      dvn.reshape(n, 512), d_o.reshape(n, 512), vn.reshape(n, 512))
    return dqkv, dog.reshape(n, 512), dgb, dnw


def _out_block(x2, tgt2, ya, yb, g1p3, wo, lnw, lnb, seq, tm=256):
    n = x2.shape[0]
    tpe = seq // tm
    bl = n // seq

    def body(x_ref, t_ref, ya_ref, yb_ref, g_ref, wo_ref, lnw_ref, lnb_ref,
             dz_ref, dya_ref, dyb_ref, dwo_ref, dg_ref, glw_ref, glb_ref, loss_ref):
        i = pl.program_id(0)

        @pl.when(i == 0)
        def _():
            dwo_ref[...] = jnp.zeros_like(dwo_ref)
            glw_ref[...] = jnp.zeros_like(glw_ref)
            glb_ref[...] = jnp.zeros_like(glb_ref)
            loss_ref[...] = jnp.zeros_like(loss_ref)

        @pl.when(i % tpe == 0)
        def _():
            dg_ref[...] = jnp.zeros_like(dg_ref)

        ya16 = ya_ref[...].astype(wo.dtype)
        yb16 = yb_ref[...].astype(wo.dtype)
        wa = wo_ref[0:GLA_WIDTH, :]
        wb = wo_ref[GLA_WIDTH:, :]
        y = _dot(ya16, wa) + _dot(yb16, wb)
        g1p = g_ref[0]
        z = ALPHA * x_ref[...] + g1p * y
        mu = jnp.mean(z, axis=-1, keepdims=True)
        zc = z - mu
        rstd = lax.rsqrt(jnp.mean(zc * zc, axis=-1, keepdims=True) + LN_EPS)
        xhat = zc * rstd
        diff = xhat * lnw_ref[...] + lnb_ref[...] - t_ref[...]
        loss_ref[...] += (0.5 / D_MODEL) * jnp.sum(jnp.sum(diff * diff, axis=-1, keepdims=True), axis=0, keepdims=True)
        dout = diff * (1.0 / D_MODEL)
        glw_ref[...] += jnp.sum(dout * xhat, axis=0, keepdims=True)
        glb_ref[...] += jnp.sum(dout, axis=0, keepdims=True)
        dxh = dout * lnw_ref[...]
        dz = rstd * (dxh - jnp.mean(dxh, axis=-1, keepdims=True)
                     - xhat * jnp.mean(dxh * xhat, axis=-1, keepdims=True))
        dz_ref[...] = dz
        dg_ref[0] += jnp.sum(dz * y, axis=0, keepdims=True)
        dy = (g1p * dz).astype(wo.dtype)
        dya_ref[...] = _dot_nt(dy, wa)
        dyb_ref[...] = _dot_nt(dy, wb)
        dwo_ref[0:GLA_WIDTH, :] += _dot_tn(ya16, dy)
        dwo_ref[GLA_WIDTH:, :] += _dot_tn(yb16, dy)

    row = lambda i: (i, 0)
    const = lambda i: (0, 0)
    per_ex = pl.BlockSpec((1, 1, D_MODEL), lambda i: (i // tpe, 0, 0))
    return pl.pallas_call(
        body, name="out_block", grid=(n // tm,),
        in_specs=[pl.BlockSpec((tm, D_MODEL), row), pl.BlockSpec((tm, D_MODEL), row),
                  pl.BlockSpec((tm, 512), row), pl.BlockSpec((tm, 512), row), per_ex,
                  pl.BlockSpec((D_MODEL, D_MODEL), const), pl.BlockSpec((1, D_MODEL), const),
                  pl.BlockSpec((1, D_MODEL), const)],
        out_specs=[pl.BlockSpec((tm, D_MODEL), row), pl.BlockSpec((tm, 512), row), pl.BlockSpec((tm, 512), row),
                   pl.BlockSpec((D_MODEL, D_MODEL), const), per_ex,
                   pl.BlockSpec((1, D_MODEL), const), pl.BlockSpec((1, D_MODEL), const),
                   pl.BlockSpec((1, 1), const)],
        out_shape=[jax.ShapeDtypeStruct((n, D_MODEL), F32), jax.ShapeDtypeStruct((n, 512), F32),
                   jax.ShapeDtypeStruct((n, 512), F32), jax.ShapeDtypeStruct((D_MODEL, D_MODEL), F32),
                   jax.ShapeDtypeStruct((bl, 1, D_MODEL), F32), jax.ShapeDtypeStruct((1, D_MODEL), F32),
                   jax.ShapeDtypeStruct((1, D_MODEL), F32), jax.ShapeDtypeStruct((1, 1), F32)],
        compiler_params=_params("arbitrary"),
    )(x2, tgt2, ya, yb, g1p3, wo, lnw, lnb)


def _proj_bwd_x(ds, ws, x2, dz, sc3, seq, tm=256):
    n = x2.shape[0]
    tpe = seq // tm
    bl = n // seq

    def body(da_ref, db_ref, dc_ref, dd1_ref, dd2_ref, wa_ref, wb_ref, wc_ref, wd_ref, x_ref, dz_ref, sc_ref,
             gx_ref, dsh_ref, dsc_ref):
        i = pl.program_id(0)

        @pl.when(i % tpe == 0)
        def _():
            dsh_ref[...] = jnp.zeros_like(dsh_ref)
            dsc_ref[...] = jnp.zeros_like(dsc_ref)

        cdt = ws[0].dtype
        dh = _dot(da_ref[...].astype(cdt), wa_ref[...])
        dh += _dot(db_ref[...].astype(cdt), wb_ref[...])
        dh += _dot(dc_ref[...].astype(cdt), wc_ref[...])
        dh += _dot((dd1_ref[...] + dd2_ref[...]).astype(cdt), wd_ref[...])
        gx_ref[...] = dh * sc_ref[0] + ALPHA * dz_ref[...]
        dsh_ref[0] += jnp.sum(dh, axis=0, keepdims=True)
        dsc_ref[0] += jnp.sum(dh * x_ref[...], axis=0, keepdims=True)

    row = lambda i: (i, 0)
    const = lambda i: (0, 0)
    per_ex = pl.BlockSpec((1, 1, D_MODEL), lambda i: (i // tpe, 0, 0))
    da, db, dc, (dd1, dd2) = ds
    return pl.pallas_call(
        body, name="proj_bwd_x", grid=(n // tm,),
        in_specs=[pl.BlockSpec((tm, d.shape[1]), row) for d in (da, db, dc, dd1, dd2)]
        + [pl.BlockSpec(w.shape, const) for w in ws]
        + [pl.BlockSpec((tm, D_MODEL), row), pl.BlockSpec((tm, D_MODEL), row), per_ex],
        out_specs=[pl.BlockSpec((tm, D_MODEL), row), per_ex, per_ex],
        out_shape=[jax.ShapeDtypeStruct((n, D_MODEL), F32), jax.ShapeDtypeStruct((bl, 1, D_MODEL), F32),
                   jax.ShapeDtypeStruct((bl, 1, D_MODEL), F32)],
        compiler_params=_params("arbitrary"),
    )(da, db, dc, dd1, dd2, *ws, x2, dz, sc3)


def _proj_bwd_w(x2, sc3, sh3, ds, seq, cdt, name, tm=256):
    n = x2.shape[0]
    tpe = seq // tm
    flat, groups = [], []
    for d in ds:
        parts = d if isinstance(d, tuple) else (d,)
        groups.append(len(parts))
        flat.extend(parts)
    nin = len(flat)

    def body(x_ref, sc_ref, sh_ref, *refs):
        i = pl.program_id(0)
        outs = refs[nin:]

        @pl.when(i == 0)
        def _():
            for o in outs:
                o[...] = jnp.zeros_like(o)

        h = (x_ref[...] * sc_ref[0] + sh_ref[0]).astype(cdt)
        pos = 0
        for o, cnt in zip(outs, groups):
            d = refs[pos][...]
            for extra in refs[pos + 1:pos + cnt]:
                d = d + extra[...]
            pos += cnt
            o[...] += _dot_tn(d.astype(cdt), h)

    row = lambda i: (i, 0)
    const = lambda i: (0, 0)
    per_ex = pl.BlockSpec((1, 1, D_MODEL), lambda i: (i // tpe, 0, 0))
    widths = [(d[0] if isinstance(d, tuple) else d).shape[1] for d in ds]
    return pl.pallas_call(
        body, name=name, grid=(n // tm,),
        in_specs=[pl.BlockSpec((tm, D_MODEL), row), per_ex, per_ex]
        + [pl.BlockSpec((tm, d.shape[1]), row) for d in flat],
        out_specs=[pl.BlockSpec((w, D_MODEL), const) for w in widths],
        out_shape=[jax.ShapeDtypeStruct((w, D_MODEL), F32) for w in widths],
        compiler_params=_params("arbitrary"),
    )(x2, sc3, sh3, *flat)


def _mod_block(c_all, w_ada_sh, b_blk):
    def body(c_ref, w_ref, b_ref, o_ref):
        o_ref[...] = _dot(c_ref[...], w_ref[...]) + b_ref[...]

    return pl.pallas_call(
        body, name="mod_block",
        out_shape=jax.ShapeDtypeStruct((c_all.shape[0], w_ada_sh.shape[1]), F32),
        compiler_params=pltpu.CompilerParams(vmem_limit_bytes=VMEM_LIMIT),
    )(c_all, w_ada_sh, b_blk)


def _ada_grads(c_all, dmod_all, dmod_blk):
    def body(c_ref, da_ref, db_ref, gw_ref, gb_ref):
        gw_ref[...] = _dot_tn(c_ref[...], db_ref[...])
        gb_ref[...] = jnp.sum(da_ref[...], axis=0, keepdims=True)

    return pl.pallas_call(
        body, name="ada_grads",
        out_shape=[jax.ShapeDtypeStruct((c_all.shape[1], dmod_blk.shape[1]), F32),
                   jax.ShapeDtypeStruct((1, dmod_all.shape[1]), F32)],
        compiler_params=pltpu.CompilerParams(vmem_limit_bytes=VMEM_LIMIT),
    )(c_all, dmod_all, dmod_blk)


def _sum_leading(parts, name):
    def body(p_ref, o_ref):
        acc = p_ref[0]
        for d in range(1, parts.shape[0]):
            acc = acc + p_ref[d]
        o_ref[...] = acc

    return pl.pallas_call(
        body, name=name, out_shape=jax.ShapeDtypeStruct(parts.shape[1:], F32),
        compiler_params=pltpu.CompilerParams(vmem_limit_bytes=VMEM_LIMIT),
    )(parts)


ELEMENTWISE_BLOCK_BYTES = 2 * 1024 * 1024


def _tile2d(rows, cols, row_align=8):
    if rows * cols * 4 <= ELEMENTWISE_BLOCK_BYTES:
        return rows, cols
    fits = [t for t in range(row_align, rows, row_align) if rows % t == 0 and t * cols * 4 <= ELEMENTWISE_BLOCK_BYTES]
    if fits:
        return fits[-1], cols
    fits = [t for t in range(128, cols, 128) if cols % t == 0 and rows * t * 4 <= ELEMENTWISE_BLOCK_BYTES]
    assert fits, (rows, cols)
    return rows, fits[-1]


def _add_n(arrs, name, out_dtypes=(F32,)):
    rows, cols = arrs[0].shape
    narrow = any(jnp.dtype(dt).itemsize < 4 for dt in tuple(out_dtypes) + tuple(a.dtype for a in arrs))
    tr, tc = _tile2d(rows, cols, 16 if narrow else 8)
    n_in = len(arrs)

    def body(*refs):
        acc = refs[0][...].astype(F32)
        for r in refs[1:n_in]:
            acc = acc + r[...].astype(F32)
        for o in refs[n_in:]:
            o[...] = acc.astype(o.dtype)

    spec = pl.BlockSpec((tr, tc), lambda i, j: (i, j))
    return pl.pallas_call(
        body, name=name, grid=(rows // tr, cols // tc), in_specs=[spec] * n_in, out_specs=[spec] * len(out_dtypes),
        out_shape=[jax.ShapeDtypeStruct((rows, cols), dt) for dt in out_dtypes],
        compiler_params=_params("parallel", "parallel"),
    )(*arrs)


def _chip_sum_blocks(a, b, per, blocks, name, chunk=128):
    rows, cols = a.shape
    padded = -(-per // 16) * 16
    assert rows >= (blocks - 1) * per + padded, (rows, per, blocks)

    def body(a_ref, b_ref, o_ref, o16_ref):
        for j in range(blocks):
            for r0 in range(0, padded, chunk):
                n_rows = min(chunk, padded - r0)
                src = pl.ds(j * per + r0, n_rows)
                s = a_ref[src, :] + b_ref[src, :]
                if per - r0 < n_rows:
                    s = jnp.where(_iota((n_rows, 1), 0) < per - r0, s, 0.0)
                o_ref[j, r0:r0 + n_rows, :] = s
                o16_ref[j, r0:r0 + n_rows, :] = s.astype(BF16)

    return pl.pallas_call(
        body, name=name,
        out_shape=[jax.ShapeDtypeStruct((blocks, padded, cols), F32), jax.ShapeDtypeStruct((blocks, padded, cols), BF16)],
        compiler_params=pltpu.CompilerParams(vmem_limit_bytes=VMEM_LIMIT),
    )(a, b)


GRAD_PAD_ROWS = 16


def _adamw(w, g, m, v, name):
    rows, cols = w.shape
    tr, tc = _tile2d(rows, cols)
    c1 = 1.0 / (1.0 - ADAM_B1 ** ADAM_STEP)
    c2 = 1.0 / (1.0 - ADAM_B2 ** ADAM_STEP)

    def body(w_ref, g_ref, m_ref, v_ref, d_ref, nm_ref, nv_ref):
        gg = g_ref[...]
        nm = ADAM_B1 * m_ref[...] + (1.0 - ADAM_B1) * gg
        nv = ADAM_B2 * v_ref[...] + (1.0 - ADAM_B2) * (gg * gg)
        nm_ref[...] = nm
        nv_ref[...] = nv
        d_ref[...] = -ADAM_LR * ((nm * c1) / (jnp.sqrt(nv * c2) + ADAM_EPS) + ADAM_WD * w_ref[...])

    spec = pl.BlockSpec((tr, tc), lambda i, j: (i, j))
    shp = jax.ShapeDtypeStruct((rows, cols), F32)
    return pl.pallas_call(
        body, name=name, grid=(rows // tr, cols // tc), in_specs=[spec] * 4, out_specs=[spec] * 3,
        out_shape=[shp, shp, shp], compiler_params=_params("parallel", "parallel"),
    )(w, g, m, v)


def _coords():
    return lax.axis_index("x"), lax.axis_index("y"), lax.axis_index("c")


def _all_gather8(blk, name):
    m_per, n = blk.shape

    def body(x_ref, out_ref, send_sems, recv_sems, local_sem):
        x, y, c = _coords()
        me, sibling = (x, y, c), (x, y, 1 - c)
        chips = [(1 - x, y), (x, 1 - y), (1 - x, 1 - y)]

        def rows(px, py, pc):
            return out_ref.at[pl.ds((4 * px + 2 * py + pc) * m_per, m_per), :]

        def copy(k, block, to, src=None):
            return pltpu.make_async_remote_copy(
                src_ref=rows(*block) if src is None else src, dst_ref=rows(*block),
                send_sem=send_sems.at[k], recv_sem=recv_sems.at[k], device_id=to, device_id_type=MESH)

        mine = pltpu.make_async_copy(x_ref, rows(*me), local_sem)
        mine.start()
        first = [copy(0, me, sibling, src=x_ref)]
        first += [copy(1 + j, me, (*chip, c), src=x_ref) for j, chip in enumerate(chips)]
        for cp in first:
            cp.start()
        passed = [copy(4 + j, (*chip, c), sibling) for j, chip in enumerate(chips)]
        for j, chip in enumerate(chips):
            copy(1 + j, (*chip, c), me).wait_recv()
            passed[j].start()
        copy(0, sibling, me).wait_recv()
        for j, chip in enumerate(chips):
            copy(4 + j, (*chip, 1 - c), me).wait_recv()
        for cp in first + passed:
            cp.wait_send()
        mine.wait()

    return pl.pallas_call(
        body, name=name,
        out_shape=jax.ShapeDtypeStruct((8 * m_per, n), blk.dtype),
        in_specs=[pl.BlockSpec(memory_space=pltpu.VMEM)],
        out_specs=pl.BlockSpec(memory_space=pltpu.VMEM),
        scratch_shapes=[pltpu.SemaphoreType.DMA((7,)), pltpu.SemaphoreType.DMA((7,)), pltpu.SemaphoreType.DMA],
        compiler_params=pltpu.CompilerParams(vmem_limit_bytes=VMEM_LIMIT),
    )(blk)


def _chip_gather(shards, split, name):
    k_arr = len(shards)

    def body(*refs):
        srcs, dsts = refs[:k_arr], refs[k_arr:2 * k_arr]
        send_sems, recv_sems, fwd_send_sems, fwd_recv_sems, local_sems = refs[2 * k_arr:]
        x, y, c = _coords()
        peers = [(1 - x, y, c), (x, 1 - y, c), (1 - x, 1 - y, c)]
        sibling = (x, y, 1 - c)
        me_chip = 2 * x + y

        def part(ref, a, core):
            if not split[a]:
                return ref
            half = shards[a].shape[1] // 2
            return ref.at[:, pl.ds(core * half, half)]

        def ici(a, j, src_chip, dst_dev):
            return pltpu.make_async_remote_copy(
                src_ref=part(srcs[a], a, c), dst_ref=part(dsts[a].at[src_chip], a, c),
                send_sem=send_sems.at[a, j], recv_sem=recv_sems.at[a, j], device_id=dst_dev, device_id_type=MESH)

        def d2d(a, j, src_chip, core):
            return pltpu.make_async_remote_copy(
                src_ref=part(dsts[a].at[src_chip], a, core), dst_ref=part(dsts[a].at[src_chip], a, core),
                send_sem=fwd_send_sems.at[a, j], recv_sem=fwd_recv_sems.at[a, j],
                device_id=sibling, device_id_type=MESH)

        local = [pltpu.make_async_copy(srcs[a], dsts[a].at[me_chip], local_sems.at[a]) for a in range(k_arr)]
        for cp in local:
            cp.start()
        sends = [ici(a, j, me_chip, peer) for a in range(k_arr) for j, peer in enumerate(peers)]
        for cp in sends:
            cp.start()
        forwards = []
        for a in range(k_arr):
            for j, peer in enumerate(peers):
                peer_chip = 2 * peer[0] + peer[1]
                ici(a, j, peer_chip, peer).wait_recv()
                if split[a]:
                    forwards.append(d2d(a, j, peer_chip, c))
                    forwards[-1].start()
        for a in range(k_arr):
            for j, peer in enumerate(peers):
                if split[a]:
                    d2d(a, j, 2 * peer[0] + peer[1], 1 - c).wait_recv()
        for cp in sends + forwards:
            cp.wait_send()
        for cp in local:
            cp.wait()

    any_spec = pl.BlockSpec(memory_space=pl.ANY)
    return pl.pallas_call(
        body, name=name,
        out_shape=[jax.ShapeDtypeStruct((4,) + s.shape, s.dtype) for s in shards],
        in_specs=[any_spec] * k_arr, out_specs=[any_spec] * k_arr,
        scratch_shapes=[pltpu.SemaphoreType.DMA((k_arr, 3))] * 4 + [pltpu.SemaphoreType.DMA((k_arr,))],
    )(*shards)


def _chip_scatter(pieces, name):
    k_arr = len(pieces)

    def body(*refs):
        srcs, dsts = refs[:k_arr], refs[k_arr:2 * k_arr]
        send_sems, recv_sems = refs[2 * k_arr:]
        x, y, c = _coords()
        peers = [(1 - x, y, c), (x, 1 - y, c), (1 - x, 1 - y, c)]
        copies = []
        for a in range(k_arr):
            for j, peer in enumerate(peers):
                copies.append(pltpu.make_async_remote_copy(
                    src_ref=srcs[a].at[2 * peer[0] + peer[1]], dst_ref=dsts[a].at[j],
                    send_sem=send_sems.at[a, j], recv_sem=recv_sems.at[a, j], device_id=peer, device_id_type=MESH))
        for cp in copies:
            cp.start()
        for cp in copies:
            cp.wait_recv()
        for cp in copies:
            cp.wait_send()

    any_spec = pl.BlockSpec(memory_space=pl.ANY)
    return pl.pallas_call(
        body, name=name,
        out_shape=[jax.ShapeDtypeStruct((3,) + p.shape[1:], p.dtype) for p in pieces],
        in_specs=[any_spec] * k_arr, out_specs=[any_spec] * k_arr,
        scratch_shapes=[pltpu.SemaphoreType.DMA((k_arr, 3)), pltpu.SemaphoreType.DMA((k_arr, 3))],
    )(*pieces)


def _sibling_swap(arrs, name):
    k_arr = len(arrs)

    def body(*refs):
        srcs, dsts = refs[:k_arr], refs[k_arr:2 * k_arr]
        send_sems, recv_sems = refs[2 * k_arr:]
        x, y, c = _coords()
        copies = [pltpu.make_async_remote_copy(
            src_ref=srcs[a], dst_ref=dsts[a], send_sem=send_sems.at[a], recv_sem=recv_sems.at[a],
            device_id=(x, y, 1 - c), device_id_type=MESH) for a in range(k_arr)]
        for cp in copies:
            cp.start()
        for cp in copies:
            cp.wait_recv()
        for cp in copies:
            cp.wait_send()

    any_spec = pl.BlockSpec(memory_space=pl.ANY)
    return pl.pallas_call(
        body, name=name,
        out_shape=[jax.ShapeDtypeStruct(a.shape, a.dtype) for a in arrs],
        in_specs=[any_spec] * k_arr, out_specs=[any_spec] * k_arr,
        scratch_shapes=[pltpu.SemaphoreType.DMA((k_arr,)), pltpu.SemaphoreType.DMA((k_arr,))],
    )(*arrs)


def _split_w_in(w_in_t):
    wa = jnp.concatenate([w_in_t[0:1024], w_in_t[1040:1552]], axis=0)
    wb = w_in_t[1552:3088]
    wc = w_in_t[3096:3608]
    wd = jnp.concatenate([w_in_t[1024:1040], w_in_t[3088:3096],
                          jnp.zeros((128 - SMALL_USED, w_in_t.shape[1]), w_in_t.dtype)], axis=0)
    return wa, wb, wc, wd


def _merge_dw_in(dwa, dwb, dwc, dwd):
    return jnp.concatenate([dwa[0:1024], dwd[0:GLA_RANK], dwa[1024:1536], dwb, dwd[GLA_RANK:SMALL_USED], dwc,
                            jnp.zeros((GRAD_PAD_ROWS, dwa.shape[1]), dwa.dtype)], axis=0)


def _local_step(x, mod, w_in16, w_out16, gla_wg, gla_bg, gla_nw, conv_w, a_log, dt_bias, gdn_nw, ln_w, ln_b, tgt):
    bl, seq, _ = x.shape
    n = bl * seq
    x2 = x.reshape(n, D_MODEL)
    tgt2 = tgt.reshape(n, D_MODEL)
    sh3 = mod[:, None, 0:D_MODEL]
    sc3 = 1.0 + mod[:, None, D_MODEL:2 * D_MODEL]
    g1p3 = 1.0 + mod[:, None, 2 * D_MODEL:]
    ws = _split_w_in(w_in16)
    wg = jnp.concatenate([gla_wg, jnp.zeros((128 - GLA_RANK, GLA_QK), F32)], axis=0)
    cw8 = jnp.concatenate([conv_w, jnp.zeros((8 - CONV_K, conv_w.shape[1]), F32)], axis=0)
    alog_v = jnp.zeros((1, 128), F32).at[:, LANE_A:LANE_A + GDN_HEADS].set(a_log)
    dtb_v = jnp.zeros((1, 128), F32).at[:, LANE_A:LANE_A + GDN_HEADS].set(dt_bias)

    pa, pb, pc, pd = _proj_fwd(x2, sc3, sh3, ws, seq)
    ya, st_a = _gla_fwd(pa, pd, wg, gla_bg, gla_nw, bl, seq)
    qkv, gb, conv_out = _gdn_pre_fwd(pb, pd, cw8, alog_v, dtb_v, bl, seq)
    yb, st_b = _gdn_fwd(qkv, gb, pc, gdn_nw, bl, seq)
    dz, dya, dyb, d_wo, d_gate, d_lnw, d_lnb, loss = _out_block(x2, tgt2, ya, yb, g1p3, w_out16, ln_w, ln_b, seq)
    da, dd1, d_wg, d_bg, d_nwa = _gla_bwd(pa, pd, st_a, dya, wg, gla_bg, gla_nw, bl, seq)
    dqkv, dc, dgb, d_nwb = _gdn_bwd(qkv, gb, pc, st_b, dyb, gdn_nw, bl, seq)
    db, dd2, d_cw8, d_alog, d_dtb = _gdn_pre_bwd(pb, conv_out, pd, dqkv, dgb, cw8, alog_v, dtb_v, bl, seq)
    gx, d_sh, d_sc = _proj_bwd_x((da, db, dc, (dd1, dd2)), ws, x2, dz, sc3, seq)
    (dwa,) = _proj_bwd_w(x2, sc3, sh3, [da], seq, w_in16.dtype, "proj_bwd_w_a")
    dwb, dwc, dwd = _proj_bwd_w(x2, sc3, sh3, [db, dc, (dd1, dd2)], seq, w_in16.dtype, "proj_bwd_w_bcd")
    grads = dict(
        w_in=_merge_dw_in(dwa, dwb, dwc, dwd),
        w_out=d_wo,
        gla_w_gate_up=d_wg[0:GLA_RANK, :],
        gla_b_gate=d_bg,
        gla_norm_w=d_nwa,
        gdn_conv_w=d_cw8[0:CONV_K, :],
        gdn_a_log=d_alog[:, LANE_A:LANE_A + GDN_HEADS],
        gdn_dt_bias=d_dtb[:, LANE_A:LANE_A + GDN_HEADS],
        gdn_norm_w=d_nwb,
        ln_w=d_lnw,
        ln_b=d_lnb,
        mod=jnp.concatenate([d_sh[:, 0, :], d_sc[:, 0, :], d_gate[:, 0, :]], axis=1),
    )
    return loss, gx.reshape(bl, seq, D_MODEL), grads


_SMALL = (("gla_b_gate", 256), ("gla_norm_w", 128), ("gdn_a_log", 4), ("gdn_dt_bias", 4), ("gdn_norm_w", 128),
          ("ln_w", 1024), ("ln_b", 1024), ("gla_w_gate_up", 16 * 256), ("gdn_conv_w", 4 * 1536), ("loss", 1),
          ("mod", 2 * 3072))


def _pack_small(grads):
    flat = jnp.concatenate([grads[k].reshape(-1) for k, _ in _SMALL])
    total = sum(sz for _, sz in _SMALL)
    rows = -(-total // 1024) * 8
    return jnp.concatenate([flat, jnp.zeros((rows * 128 - total,), F32)]).reshape(rows, 128)


def _unpack_small(flat):
    out, pos = {}, 0
    for k, sz in _SMALL:
        out[k] = flat[pos:pos + sz]
        pos += sz
    return out


def kernel(x, c, w_ada, b_ada, w_in, gla_w_gate_up, gla_b_gate, gla_norm_w, gdn_conv_w, gdn_a_log, gdn_dt_bias, gdn_norm_w, w_out, ln_w, ln_b, loss_target, m_w_ada, m_b_ada, m_w_in, m_gla_w_gate_up, m_gla_b_gate, m_gla_norm_w, m_gdn_conv_w, m_gdn_a_log, m_gdn_dt_bias, m_gdn_norm_w, m_w_out, m_ln_w, m_ln_b, v_w_ada, v_b_ada, v_w_in, v_gla_w_gate_up, v_gla_b_gate, v_gla_norm_w, v_gdn_conv_w, v_gdn_a_log, v_gdn_dt_bias, v_gdn_norm_w, v_w_out, v_ln_w, v_ln_b):
    ix, iy, ic = _coords()
    chip = 2 * ix + iy
    dev = 4 * ix + 2 * iy + ic
    bl = x.shape[0]
    ndev = 8

    c_all = _all_gather8(c.reshape(8, -1), "gather_c").reshape(ndev * bl, D_MODEL)
    ada_cols = w_ada.shape[2]
    b_blk = lax.dynamic_slice_in_dim(b_ada, chip * ada_cols, ada_cols, axis=1)
    mod_blk = _mod_block(c_all, w_ada[0], b_blk)
    mod_g = _all_gather8(mod_blk, "gather_mod").reshape(ndev, ndev * bl, ada_cols)
    mod_all = jnp.concatenate([mod_g[2 * j] for j in range(4)], axis=1)
    mod = lax.dynamic_slice_in_dim(mod_all, dev * bl, bl, axis=0)

    w_in_g, w_out_g, wg_g, cw_g = _chip_gather(
        [jnp.transpose(w_in[0]).astype(BF16), w_out[0].astype(BF16), gla_w_gate_up[0], gdn_conv_w[0]],
        [True, True, False, False], "gather_weights")
    w_in16 = w_in_g.reshape(IN_COLS, D_MODEL)
    w_out16 = w_out_g.reshape(D_MODEL, D_MODEL)
    gla_wg = jnp.concatenate([wg_g[j] for j in range(4)], axis=1)
    conv_w = jnp.concatenate([cw_g[j] for j in range(4)], axis=1)

    loss, grad_x, gr = _local_step(x, mod, w_in16, w_out16, gla_wg, gla_b_gate, gla_norm_w, conv_w,
                                   gdn_a_log, gdn_dt_bias, gdn_norm_w, ln_w, ln_b, loss_target)

    gr["loss"] = loss
    packed = _pack_small(gr)
    prow = packed.shape[0]
    gathered = _all_gather8(packed, "gather_small").reshape(ndev, prow, 128)
    small = _unpack_small(_sum_leading(gathered, "sum_small").reshape(-1))
    loss = small["loss"][0]
    mod_rows = gathered.reshape(ndev, prow * 128)[:, sum(sz for _, sz in _SMALL[:-1]):][:, :bl * 3 * D_MODEL]
    dmod_all = mod_rows.reshape(ndev * bl, 3 * D_MODEL)
    dmod_blk = lax.dynamic_slice_in_dim(dmod_all, chip * ada_cols, ada_cols, axis=1)
    g_w_ada, g_b_ada = _ada_grads(c_all, dmod_all, dmod_blk)
    wg_cols = gla_w_gate_up.shape[2]
    g_wg = lax.dynamic_slice_in_dim(small["gla_w_gate_up"].reshape(GLA_RANK, GLA_QK), chip * wg_cols, wg_cols, axis=1)
    cw_cols = gdn_conv_w.shape[2]
    g_cw = lax.dynamic_slice_in_dim(small["gdn_conv_w"].reshape(CONV_K, 3 * GDN_WIDTH), chip * cw_cols, cw_cols, axis=1)

    in_feats = w_in.shape[2]
    out_rows = w_out.shape[1]
    p_in = gr["w_in"]
    p_out = gr["w_out"].reshape(4, out_rows, D_MODEL)
    h_in, h_out = D_MODEL // 2, out_rows // 2
    mine_in = lax.dynamic_slice_in_dim(p_in, ic * h_in, h_in, axis=1)
    mine_out = lax.dynamic_slice_in_dim(p_out, ic * h_out, h_out, axis=1)
    theirs_in = lax.dynamic_slice_in_dim(p_in, (1 - ic) * h_in, h_in, axis=1)
    theirs_out = lax.dynamic_slice_in_dim(p_out, (1 - ic) * h_out, h_out, axis=1)
    got_in, got_out = _sibling_swap([theirs_in, theirs_out], "swap_halves")
    chip_in, chip_in16 = _chip_sum_blocks(mine_in, got_in, in_feats, 4, "chip_sum_in")
    chip_out, chip_out16 = _add_n([mine_out.reshape(4 * h_out, D_MODEL), got_out.reshape(4 * h_out, D_MODEL)],
                                  "chip_sum_out", (F32, BF16))
    chip_out = chip_out.reshape(4, h_out, D_MODEL)
    rs_in, rs_out = _chip_scatter([chip_in16, chip_out16.reshape(4, h_out, D_MODEL)], "scatter_grads")
    own_in = lax.dynamic_index_in_dim(chip_in, chip, axis=0, keepdims=False)
    own_out = lax.dynamic_index_in_dim(chip_out, chip, axis=0, keepdims=False)
    (half_in,) = _add_n([own_in, rs_in[0], rs_in[1], rs_in[2]], "reduce_in")
    (half_out,) = _add_n([own_out, rs_out[0], rs_out[1], rs_out[2]], "reduce_out")
    sib_in, sib_out = _sibling_swap([half_in, half_out], "swap_result")
    g_w_in_t = jnp.where(ic == 0, jnp.concatenate([half_in, sib_in], axis=1),
                         jnp.concatenate([sib_in, half_in], axis=1))[0:in_feats]
    g_w_out = jnp.where(ic == 0, jnp.concatenate([half_out, sib_out], axis=0),
                        jnp.concatenate([sib_out, half_out], axis=0))

    grads = dict(
        w_ada=g_w_ada[None], b_ada=g_b_ada, w_in=g_w_in_t, gla_w_gate_up=g_wg[None],
        gla_b_gate=small["gla_b_gate"].reshape(1, -1), gla_norm_w=small["gla_norm_w"].reshape(1, -1),
        gdn_conv_w=g_cw[None], gdn_a_log=small["gdn_a_log"].reshape(1, -1),
        gdn_dt_bias=small["gdn_dt_bias"].reshape(1, -1), gdn_norm_w=small["gdn_norm_w"].reshape(1, -1),
        w_out=g_w_out[None], ln_w=small["ln_w"].reshape(1, -1), ln_b=small["ln_b"].reshape(1, -1))
    weights = dict(w_ada=w_ada, b_ada=b_ada, w_in=w_in, gla_w_gate_up=gla_w_gate_up, gla_b_gate=gla_b_gate,
                   gla_norm_w=gla_norm_w, gdn_conv_w=gdn_conv_w, gdn_a_log=gdn_a_log, gdn_dt_bias=gdn_dt_bias,
                   gdn_norm_w=gdn_norm_w, w_out=w_out, ln_w=ln_w, ln_b=ln_b)
    m_in = dict(w_ada=m_w_ada, b_ada=m_b_ada, w_in=m_w_in, gla_w_gate_up=m_gla_w_gate_up, gla_b_gate=m_gla_b_gate,
                gla_norm_w=m_gla_norm_w, gdn_conv_w=m_gdn_conv_w, gdn_a_log=m_gdn_a_log, gdn_dt_bias=m_gdn_dt_bias,
                gdn_norm_w=m_gdn_norm_w, w_out=m_w_out, ln_w=m_ln_w, ln_b=m_ln_b)
    v_in = dict(w_ada=v_w_ada, b_ada=v_b_ada, w_in=v_w_in, gla_w_gate_up=v_gla_w_gate_up, gla_b_gate=v_gla_b_gate,
                gla_norm_w=v_gla_norm_w, gdn_conv_w=v_gdn_conv_w, gdn_a_log=v_gdn_a_log, gdn_dt_bias=v_gdn_dt_bias,
                gdn_norm_w=v_gdn_norm_w, w_out=v_w_out, ln_w=v_ln_w, ln_b=v_ln_b)
    names = list(weights)
    delta, new_m, new_v = {}, {}, {}
    for nm in names:
        shp = weights[nm].shape
        if nm == "w_in":
            to2d = lambda t: jnp.transpose(t[0])
            from2d = lambda t: jnp.transpose(t)[None]
            g2d = grads[nm]
        else:
            to2d = lambda t: t.reshape(-1, shp[-1])
            from2d = lambda t: t.reshape(shp)
            g2d = to2d(grads[nm])
        d, a, b = _adamw(to2d(weights[nm]), g2d, to2d(m_in[nm]), to2d(v_in[nm]), "adamw_" + nm)
        delta[nm], new_m[nm], new_v[nm] = from2d(d), from2d(a), from2d(b)
        grads[nm] = from2d(g2d)
    return (loss, grad_x, *[grads[k] for k in names], *[delta[k] for k in names],
            *[new_m[k] for k in names], *[new_v[k] for k in names])
```

```python
import functools

import jax
import jax.numpy as jnp
from jax import lax
from jax.experimental import pallas as pl
from jax.experimental.pallas import tpu as pltpu

F32 = jnp.float32
BF16 = jnp.bfloat16
HI = lax.Precision.HIGH
INV_PREC = None
MESH = pl.DeviceIdType.MESH

D_MODEL = 1024
GLA_HEADS = 4
GLA_DK = 64
GLA_DV = 128
GLA_QK = 256
GLA_WIDTH = 512
GLA_RANK = 16
GLA_GATE_NORM = 16.0
GDN_HEADS = 4
GDN_DK = 128
GDN_WIDTH = 512
CONV_K = 4
CHUNK = 64
LN_EPS = 1e-5
RMS_EPS = 1e-6
ALPHA = 2.0 ** 0.25
IN_COLS = 3608

LANE_A = GLA_RANK
LANE_B = GLA_RANK + GDN_HEADS
SMALL_USED = GLA_RANK + 2 * GDN_HEADS

ADAM_LR = 0.001
ADAM_B1 = 0.9
ADAM_B2 = 0.999
ADAM_EPS = 1e-08
ADAM_WD = 0.01
ADAM_STEP = 10

VMEM_LIMIT = 56 * 1024 * 1024


def _iota(shape, dim):
    return lax.broadcasted_iota(jnp.int32, shape, dim)


def _dot(a, b, prec=None):
    return lax.dot_general(a, b, (((1,), (0,)), ((), ())), precision=prec, preferred_element_type=F32)


def _dot_nt(a, b, prec=None):
    return lax.dot_general(a, b, (((1,), (1,)), ((), ())), precision=prec, preferred_element_type=F32)


def _dot_tn(a, b, prec=None):
    return lax.dot_general(a, b, (((0,), (0,)), ((), ())), precision=prec, preferred_element_type=F32)


def _log_sigmoid(z):
    return jnp.minimum(z, 0.0) - jnp.log1p(jnp.exp(-jnp.abs(z)))


def _softplus(z):
    return jnp.maximum(z, 0.0) + jnp.log1p(jnp.exp(-jnp.abs(z)))


def _silu(z):
    return z * jax.nn.sigmoid(z)


def _rms_gate(o, nw, og):
    return o * lax.rsqrt(jnp.mean(o * o, axis=-1, keepdims=True) + RMS_EPS) * nw * _silu(og)


def _params(*sem):
    return pltpu.CompilerParams(dimension_semantics=sem, vmem_limit_bytes=VMEM_LIMIT)


GLA_PAIRS = GLA_HEADS // 2


def _gla_chunk(qs, ks, lrs, vs, ogs, ss, wgs, bgs, nw):
    c = qs[0].shape[0]
    n_ep = len(ss)
    n_ex = n_ep // GLA_PAIRS
    n_chunks = len(qs) // n_ep
    pair_units = [(i // n_ep * n_ex + i % n_ep // GLA_PAIRS, i % GLA_PAIRS) for i in range(len(qs))]
    head_units = [(i // GLA_HEADS * GLA_PAIRS + i % GLA_HEADS // 2, i % 2) for i in range(len(vs))]
    row, col = _iota((c, c), 0), _iota((c, c), 1)
    causal = row >= col
    first_half = (_iota((c, 1), 0) < c // 2).astype(F32)
    lane = _iota((1, 128), 1)
    masks = [(lane < GLA_DK).astype(F32), (lane >= GLA_DK).astype(F32)]
    gs = [_log_sigmoid(_dot(lrs[ce], wgs[p]) + bgs[p]) * (1.0 / GLA_GATE_NORM) for ce, p in pair_units]
    bs = [_dot(causal.astype(F32), g, HI) for g in gs]
    b_ref = [jnp.sum(g * first_half, axis=0, keepdims=True) for g in gs]
    b_last = [jnp.sum(g, axis=0, keepdims=True) for g in gs]
    qsc = [q * (GLA_DK ** -0.5) for q in qs]
    qe = [q * jnp.exp(b - br) for q, b, br in zip(qsc, bs, b_ref)]
    ke = [k * jnp.exp(br - b) for k, b, br in zip(ks, bs, b_ref)]
    qb = [q * jnp.exp(b) for q, b in zip(qsc, bs)]
    kd = [k * jnp.exp(bl_ - b) for k, b, bl_ in zip(ks, bs, b_last)]
    decay = [jnp.exp(bl_) for bl_ in b_last]
    att = [jnp.where(causal, _dot_nt(qe[u] * masks[half], ke[u]), 0.0) for u, half in head_units]
    o_intra = [_dot(a, v) for a, v in zip(att, vs)]
    qbm = [qb[u] * masks[half] for u, half in head_units]
    kdm = [kd[u] * masks[half] for u, half in head_units]
    ys = []
    for r in range(n_chunks):
        heads_r = range(r * n_ex * GLA_HEADS, (r + 1) * n_ex * GLA_HEADS)
        o_inter = [_dot_nt(qbm[i], ss[head_units[i][0] - r * n_ep]) for i in heads_r]
        upd = [_dot_tn(vs[i], kdm[i]) for i in heads_r]
        ss = [s * decay[r * n_ep + j] + upd[2 * j] + upd[2 * j + 1] for j, s in enumerate(ss)]
        ys += [_rms_gate(o_intra[i] + oi, nw, ogs[i]) for i, oi in zip(heads_r, o_inter)]
    return ys, ss


def _unit_lower_inverse_chain(a_list):
    c = a_list[0].shape[0]
    eye = (_iota((c, c), 0) == _iota((c, c), 1)).astype(F32)
    ps = [-a for a in a_list]
    ts = [eye + p for p in ps]
    levels = max(c.bit_length() - 2, 0)
    if levels:
        ps = [_dot(p, p, INV_PREC) for p in ps]
    for level in range(levels):
        last = level == levels - 1
        both = [_dot(t if last else jnp.concatenate([t, p], axis=0), p, INV_PREC) for t, p in zip(ts, ps)]
        ts = [t + m[0:c] for t, m in zip(ts, both)]
        if not last:
            ps = [m[c:2 * c] for m in both]
    return ts


@jax.custom_vjp
def _unit_lower_inverse(a_list):
    return _unit_lower_inverse_chain(a_list)


def _unit_lower_inverse_fwd(a_list):
    ts = _unit_lower_inverse_chain(a_list)
    return ts, ts


def _unit_lower_inverse_bwd(ts, dts):
    xs = [_dot_nt(dt, t, INV_PREC) for dt, t in zip(dts, ts)]
    return ([-_dot_tn(t, x, INV_PREC) for t, x in zip(ts, xs)],)


_unit_lower_inverse.defvjp(_unit_lower_inverse_fwd, _unit_lower_inverse_bwd)


@jax.custom_vjp
def _unit_lower_inverse_known(a_list, ts):
    return ts


def _unit_lower_inverse_known_fwd(a_list, ts):
    return ts, ts


def _unit_lower_inverse_known_bwd(ts, dts):
    return _unit_lower_inverse_bwd(ts, dts) + ([jnp.zeros_like(t) for t in ts],)


_unit_lower_inverse_known.defvjp(_unit_lower_inverse_known_fwd, _unit_lower_inverse_known_bwd)


def _gdn_prep_units(qs, ks, vs, gbs, t_known=None):
    c = qs[0].shape[0]
    units = [divmod(i, GDN_HEADS) for i in range(len(qs))]
    row, col = _iota((c, c), 0), _iota((c, c), 1)
    causal, strict = row >= col, row > col
    if t_known is None:
        lane = _iota((1, 128), 1)
        d_alls = [_dot(causal.astype(F32), gb, HI) for gb in gbs]
        g_c, beta_c, d_c = [], [], []
        for r, h in units:
            sel_a = (lane == LANE_A + h).astype(F32)
            g_c.append(jnp.sum(gbs[r] * sel_a, axis=-1, keepdims=True))
            beta_c.append(jnp.sum(gbs[r] * (lane == LANE_B + h).astype(F32), axis=-1, keepdims=True))
            d_c.append(jnp.sum(d_alls[r] * sel_a, axis=-1, keepdims=True))
        d_diff = [jnp.broadcast_to(d, (c, c)) - jnp.broadcast_to(d, (c, c)).T for d in d_c]
    else:
        src = _iota((128, 128), 0)
        spread = jnp.concatenate([(src == base + h).astype(F32) for base in (LANE_A, LANE_B)
                                  for h in range(GDN_HEADS)], axis=1)
        width = GDN_HEADS * 128
        g_beta = [_dot(gb, spread, HI) for gb in gbs]
        d_alls = [_dot(causal.astype(F32), gbv[:, 0:width], HI) for gbv in g_beta]
        g_c = [g_beta[r][:, h * 128:(h + 1) * 128] for r, h in units]
        beta_c = [g_beta[r][:, width + h * 128:width + (h + 1) * 128] for r, h in units]
        d_c = [d_alls[r][:, h * 128:(h + 1) * 128] for r, h in units]
        d_diff = [d[:, 0:c] - d.T[0:c, :] for d in d_c]
    d_last = [jnp.sum(g, axis=0, keepdims=True) for g in g_c]
    decay_mat = [jnp.where(causal, jnp.exp(jnp.where(causal, dd, 0.0)), 0.0) for dd in d_diff]
    kb = [k * b for k, b in zip(ks, beta_c)]
    kbk_qk = [_dot_nt(jnp.concatenate([kbi, q], axis=0), k) for kbi, q, k in zip(kb, qs, ks)]
    a = [jnp.where(strict, m[0:c] * dm, 0.0) for m, dm in zip(kbk_qk, decay_mat)]
    qk = [jnp.where(causal, m[c:2 * c] * dm, 0.0) for m, dm in zip(kbk_qk, decay_mat)]
    t = _unit_lower_inverse(a) if t_known is None else _unit_lower_inverse_known(a, t_known)
    uw = [_dot(ti, jnp.concatenate([v * b, kbi * jnp.exp(d)], axis=1))
          for ti, v, b, kbi, d in zip(t, vs, beta_c, kb, d_c)]
    u = [m[:, 0:128] for m in uw]
    w = [m[:, 128:256] for m in uw]
    q_dec = [q * jnp.exp(d) for q, d in zip(qs, d_c)]
    k_dec = [k * jnp.exp(dl - d) for k, dl, d in zip(ks, d_last, d_c)]
    gamma = [jnp.exp(dl) for dl in d_last]
    return u, w, qk, q_dec, k_dec, gamma, t


def _sum_rows(t):
    return jnp.sum(t, axis=0, keepdims=True)


def _gdn_pre_elem(ps, ab, alog_v, dtb_v):
    outs = []
    for j, p in enumerate(ps):
        s = _silu(p)
        if j < 2 * GDN_HEADS:
            s = s * lax.rsqrt(jnp.sum(s * s, axis=-1, keepdims=True) + RMS_EPS)
        if j < GDN_HEADS:
            s = s * (GDN_DK ** -0.5)
        outs.append(s)
    lane = _iota((1, 128), 1)
    is_a = (lane >= LANE_A) & (lane < LANE_A + GDN_HEADS)
    is_b = (lane >= LANE_B) & (lane < LANE_B + GDN_HEADS)
    g = -jnp.exp(alog_v) * _softplus(ab + dtb_v)
    gb = jnp.where(is_a, g, jnp.where(is_b, jax.nn.sigmoid(ab), 0.0))
    return tuple(outs) + (gb,)


def _proj_fwd(x2, sc3, sh3, ws, seq, tm=256):
    n = x2.shape[0]
    tpe = seq // tm
    nw = len(ws)

    def body(x_ref, sc_ref, sh_ref, *refs):
        h = (x_ref[...] * sc_ref[0] + sh_ref[0]).astype(ws[0].dtype)
        for w_ref, o_ref in zip(refs[:nw], refs[nw:]):
            o_ref[...] = _dot_nt(h, w_ref[...])

    row = lambda i: (i, 0)
    per_ex = pl.BlockSpec((1, 1, D_MODEL), lambda i: (i // tpe, 0, 0))
    return pl.pallas_call(
        body, name="proj_fwd", grid=(n // tm,),
        in_specs=[pl.BlockSpec((tm, D_MODEL), row), per_ex, per_ex]
        + [pl.BlockSpec(w.shape, lambda i: (0, 0)) for w in ws],
        out_specs=[pl.BlockSpec((tm, w.shape[0]), row) for w in ws],
        out_shape=[jax.ShapeDtypeStruct((n, w.shape[0]), F32) for w in ws],
        compiler_params=_params("parallel"),
    )(x2, sc3, sh3, *ws)


GLA_SCAN_CHUNKS = 4


def _gla_operands(q_ref, k_ref, v_ref, og_ref, lr_ref, wg_ref, bg_ref, bl, r_per):
    chunks = [slice(r * CHUNK, (r + 1) * CHUNK) for r in range(r_per)]
    pair_cols = [slice(p * 128, (p + 1) * 128) for p in range(GLA_PAIRS)]
    head_cols = [slice(h * 128, (h + 1) * 128) for h in range(GLA_HEADS)]
    per_pair = lambda ref: [ref[e, rows, cols] for rows in chunks for e in range(bl) for cols in pair_cols]
    per_head = lambda ref: [ref[e, rows, cols] for rows in chunks for e in range(bl) for cols in head_cols]
    return (per_pair(q_ref), per_pair(k_ref), [lr_ref[e, rows, :] for rows in chunks for e in range(bl)],
            per_head(v_ref), per_head(og_ref)), ([wg_ref[:, cols] for cols in pair_cols],
                                                 [bg_ref[:, cols] for cols in pair_cols])


def _gla_fwd(pa, pd, wg, bg, nw, bl, seq):
    n = pa.shape[0]
    nc = seq // CHUNK
    r_per = GLA_SCAN_CHUNKS
    pairs = [(e, p) for e in range(bl) for p in range(GLA_PAIRS)]
    head_slots = [(slice(r * CHUNK, (r + 1) * CHUNK), e, slice(h * 128, (h + 1) * 128))
                  for r in range(r_per) for e in range(bl) for h in range(GLA_HEADS)]

    def body(q_ref, k_ref, v_ref, og_ref, lr_ref, wg_ref, bg_ref, nw_ref, y_ref, st_ref, s_scr):
        @pl.when(pl.program_id(0) == 0)
        def _():
            s_scr[...] = jnp.zeros_like(s_scr)

        ss = [s_scr[e, p] for e, p in pairs]
        for (e, p), s in zip(pairs, ss):
            st_ref[e, 0, p] = s
        acts, gate = _gla_operands(q_ref, k_ref, v_ref, og_ref, lr_ref, wg_ref, bg_ref, bl, r_per)
        ys, s_new = _gla_chunk(*acts, ss, *gate, nw_ref[...])
        for (rows, e, cols), y in zip(head_slots, ys):
            y_ref[e, rows, cols] = y.astype(y_ref.dtype)
        for (e, p), s in zip(pairs, s_new):
            s_scr[e, p] = s

    tok = lambda w, j: pl.BlockSpec((bl, r_per * CHUNK, w), lambda i: (0, i, j))
    const = lambda i: (0, 0)
    pa3 = pa.reshape(bl, seq, 1536)
    y, st = pl.pallas_call(
        body, name="gla_fwd", grid=(nc // r_per,),
        in_specs=[tok(256, 0), tok(256, 1), tok(512, 1), tok(512, 2), tok(128, 0),
                  pl.BlockSpec(wg.shape, const), pl.BlockSpec(bg.shape, const), pl.BlockSpec(nw.shape, const)],
        out_specs=[tok(512, 0), pl.BlockSpec((bl, 1, GLA_PAIRS, 128, 128), lambda i: (0, i, 0, 0, 0))],
        out_shape=[jax.ShapeDtypeStruct((bl, seq, 512), MM_DTYPE),
                   jax.ShapeDtypeStruct((bl, nc // r_per, GLA_PAIRS, 128, 128), F32)],
        scratch_shapes=[pltpu.VMEM((bl, GLA_PAIRS, 128, 128), F32)],
        compiler_params=_params("arbitrary"),
    )(pa3, pa3, pa3, pa3, pd.reshape(bl, seq, 128), wg, bg, nw)
    return y.reshape(n, 512), st


def _gla_bwd(pa, pd, st, dya, wg, bg, nw, bl, seq):
    n = pa.shape[0]
    nc = seq // CHUNK
    r_per = GLA_SCAN_CHUNKS
    steps = nc // r_per
    pairs = [(e, p) for e in range(bl) for p in range(GLA_PAIRS)]
    pair_cols = [slice(p * 128, (p + 1) * 128) for p in range(GLA_PAIRS)]
    chunks = [slice(r * CHUNK, (r + 1) * CHUNK) for r in range(r_per)]
    pair_slots = [(rows, e, p) for rows in chunks for e in range(bl) for p in range(GLA_PAIRS)]
    head_slots = [(rows, e, h) for rows in chunks for e in range(bl) for h in range(GLA_HEADS)]

    def body(q_ref, k_ref, v_ref, og_ref, lr_ref, st_ref, dy_ref, wg_ref, bg_ref, nw_ref,
             da_ref, dd_ref, dwg_ref, dbg_ref, dnw_ref, ds_scr):
        @pl.when(pl.program_id(0) == 0)
        def _():
            dwg_ref[...] = jnp.zeros_like(dwg_ref)
            dbg_ref[...] = jnp.zeros_like(dbg_ref)
            dnw_ref[...] = jnp.zeros_like(dnw_ref)
            ds_scr[...] = jnp.zeros_like(ds_scr)

        acts, gate = _gla_operands(q_ref, k_ref, v_ref, og_ref, lr_ref, wg_ref, bg_ref, bl, r_per)
        _, vjp = jax.vjp(_gla_chunk, *acts, [st_ref[e, 0, p] for e, p in pairs], *gate, nw_ref[...])
        dq, dk, dlr, dv, dog, ds, dwg, dbg, dnw = vjp(
            ([dy_ref[e, rows, h * 128:(h + 1) * 128] for rows, e, h in head_slots], [ds_scr[e, p] for e, p in pairs]))
        for i, (rows, e) in enumerate((rows, e) for rows in chunks for e in range(bl)):
            dd_ref[e, rows, :] = dlr[i]
        for i, (rows, e, p) in enumerate(pair_slots):
            da_ref[e, rows, pair_cols[p]] = dq[i].astype(da_ref.dtype)
            da_ref[e, rows, GLA_QK + p * 128:GLA_QK + (p + 1) * 128] = dk[i].astype(da_ref.dtype)
        for i, (rows, e, h) in enumerate(head_slots):
            da_ref[e, rows, 512 + h * 128:512 + (h + 1) * 128] = dv[i].astype(da_ref.dtype)
            da_ref[e, rows, 1024 + h * 128:1024 + (h + 1) * 128] = dog[i].astype(da_ref.dtype)
        for (e, p), d in zip(pairs, ds):
            ds_scr[e, p] = d
        for p, cols in enumerate(pair_cols):
            dwg_ref[:, cols] += dwg[p]
            dbg_ref[:, cols] += dbg[p]
        dnw_ref[...] += dnw

    tok = lambda w, j: pl.BlockSpec((bl, r_per * CHUNK, w), lambda i: (0, steps - 1 - i, j))
    const = lambda i: (0, 0)
    pa3 = pa.reshape(bl, seq, 1536)
    da, dd, dwg, dbg, dnw = pl.pallas_call(
        body, name="gla_bwd", grid=(steps,),
        in_specs=[tok(256, 0), tok(256, 1), tok(512, 1), tok(512, 2), tok(128, 0),
                  pl.BlockSpec((bl, 1, GLA_PAIRS, 128, 128), lambda i: (0, steps - 1 - i, 0, 0, 0)), tok(512, 0),
                  pl.BlockSpec(wg.shape, const), pl.BlockSpec(bg.shape, const), pl.BlockSpec(nw.shape, const)],
        out_specs=[tok(1536, 0), tok(128, 0),
                   pl.BlockSpec(wg.shape, const), pl.BlockSpec(bg.shape, const), pl.BlockSpec(nw.shape, const)],
        out_shape=[jax.ShapeDtypeStruct((bl, seq, 1536), MM_DTYPE), jax.ShapeDtypeStruct((bl, seq, 128), F32),
                   jax.ShapeDtypeStruct(wg.shape, F32), jax.ShapeDtypeStruct(bg.shape, F32),
                   jax.ShapeDtypeStruct(nw.shape, F32)],
        scratch_shapes=[pltpu.VMEM((bl, GLA_PAIRS, 128, 128), F32)],
        compiler_params=_params("arbitrary"),
    )(pa3, pa3, pa3, pa3, pd.reshape(bl, seq, 128), st, dya.reshape(bl, seq, 512), wg, bg, nw)
    return da.reshape(n, 1536), dd.reshape(n, 128), dwg, dbg, dnw


PRE_ROWS = 64
PRE_PIECES = [slice(j * 128, (j + 1) * 128) for j in range(3 * GDN_HEADS)]


def _rows_from(ref, start, rows, cols):
    lo = start // 8 * 8
    if lo == start:
        return ref[start:start + rows, cols]
    window = ref[lo:lo + rows + 8, cols]
    return pltpu.roll(window, rows + 8 - (start - lo), 0)[0:rows]


def _conv_taps(buf_ref, w_ref, base, rows, cols):
    acc = w_ref[0:1, cols] * _rows_from(buf_ref, base, rows, cols)
    for k in range(1, CONV_K):
        acc = acc + w_ref[k:k + 1, cols] * _rows_from(buf_ref, base + k, rows, cols)
    return acc


def _gdn_pre_fwd(pb, pd, cw8, alog_v, dtb_v, bl, seq, tm=256):
    n = pb.shape[0]
    tpe = seq // tm
    t8 = tm // 8

    def body(u_ref, prev_ref, ab_ref, w_ref, al_ref, dt_ref, qkv_ref, gb_ref, p_ref, buf):
        i = pl.program_id(0)
        keep = (i % tpe != 0).astype(F32)
        buf[0:8, :] = prev_ref[...] * keep
        buf[8:8 + tm, :] = u_ref[...]
        for r0 in range(0, tm, PRE_ROWS):
            rows = slice(r0, r0 + PRE_ROWS)
            ps = [_conv_taps(buf, w_ref, 8 - (CONV_K - 1) + r0, PRE_ROWS, cols) for cols in PRE_PIECES]
            outs = _gdn_pre_elem(ps, ab_ref[rows, :], al_ref[...], dt_ref[...])
            for cols, p, out in zip(PRE_PIECES, ps, outs):
                p_ref[rows, cols] = p
                qkv_ref[rows, cols] = out
            gb_ref[rows, :] = outs[len(PRE_PIECES)]

    row = lambda i: (i, 0)
    const = lambda i: (0, 0)
    return pl.pallas_call(
        body, name="gdn_pre_fwd", grid=(n // tm,),
        in_specs=[pl.BlockSpec((tm, 1536), row),
                  pl.BlockSpec((8, 1536), lambda i: (jnp.maximum(i * t8 - 1, 0), 0)),
                  pl.BlockSpec((tm, 128), row),
                  pl.BlockSpec((8, 1536), const), pl.BlockSpec((1, 128), const), pl.BlockSpec((1, 128), const)],
        out_specs=[pl.BlockSpec((tm, 1536), row), pl.BlockSpec((tm, 128), row), pl.BlockSpec((tm, 1536), row)],
        out_shape=[jax.ShapeDtypeStruct((n, 1536), F32), jax.ShapeDtypeStruct((n, 128), F32),
                   jax.ShapeDtypeStruct((n, 1536), F32)],
        scratch_shapes=[pltpu.VMEM((tm + 8, 1536), F32)],
        compiler_params=_params("parallel"),
    )(pb, pb, pd, cw8, alog_v, dtb_v)


def _gdn_pre_bwd(pb, conv_out, pd, dqkv, dgb, cw8, alog_v, dtb_v, bl, seq, tm=256):
    n = pb.shape[0]
    tpe = seq // tm
    t8 = tm // 8
    nb8 = n // 8
    ext = tm + 8

    def body(u_ref, p_ref, pn_ref, ab_ref, abn_ref, dq_ref, dqn_ref, dgb_ref, w_ref, al_ref, dt_ref,
             du_ref, dab_ref, dw_ref, dal_ref, ddt_ref, dpbuf):
        i = pl.program_id(0)

        @pl.when(i == 0)
        def _():
            dw_ref[...] = jnp.zeros_like(dw_ref)
            dal_ref[...] = jnp.zeros_like(dal_ref)
            ddt_ref[...] = jnp.zeros_like(ddt_ref)

        keep_next = (i % tpe != tpe - 1).astype(F32)
        zeros8 = jnp.zeros((8, 128), F32)
        dal, ddt = jnp.zeros((1, 128), F32), jnp.zeros((1, 128), F32)
        for r0 in range(0, tm, PRE_ROWS):
            rows = slice(r0, r0 + PRE_ROWS)
            last = r0 + PRE_ROWS == tm
            along = lambda own, extra: jnp.concatenate([own, extra], axis=0) if last else own
            ps = [along(p_ref[rows, cols], pn_ref[:, cols]) for cols in PRE_PIECES]
            ab = along(ab_ref[rows, :], abn_ref[...])
            _, vjp = jax.vjp(_gdn_pre_elem, ps, ab, al_ref[...], dt_ref[...])
            cts = tuple(along(dq_ref[rows, cols], dqn_ref[:, cols] * keep_next) for cols in PRE_PIECES)
            cts += (along(dgb_ref[rows, :], zeros8),)
            dps, dab, dal_r, ddt_r = vjp(cts)
            out_rows = slice(r0, r0 + PRE_ROWS + (8 if last else 0))
            for cols, dp in zip(PRE_PIECES, dps):
                dpbuf[out_rows, cols] = dp
            dab_ref[rows, :] = dab[0:PRE_ROWS, :]
            dal, ddt = dal + dal_r, ddt + ddt_r
        dal_ref[...] += dal
        ddt_ref[...] += ddt
        for cols in PRE_PIECES:
            dw = [jnp.zeros((1, 128), F32) for _ in range(CONV_K)]
            for r0 in range(0, tm, PRE_ROWS):
                u = u_ref[r0:r0 + PRE_ROWS, cols]
                du = None
                for k in range(CONV_K):
                    dp_k = _rows_from(dpbuf, r0 + CONV_K - 1 - k, PRE_ROWS, cols)
                    term = w_ref[k:k + 1, cols] * dp_k
                    du = term if du is None else du + term
                    dw[k] = dw[k] + jnp.sum(u * dp_k, axis=0, keepdims=True)
                du_ref[r0:r0 + PRE_ROWS, cols] = du.astype(du_ref.dtype)
            for k in range(CONV_K):
                dw_ref[k:k + 1, cols] += dw[k]

    row = lambda i: (i, 0)
    next8 = lambda i: (jnp.minimum((i + 1) * t8, nb8 - 1), 0)
    const = lambda i: (0, 0)
    return pl.pallas_call(
        body, name="gdn_pre_bwd", grid=(n // tm,),
        in_specs=[pl.BlockSpec((tm, 1536), row), pl.BlockSpec((tm, 1536), row), pl.BlockSpec((8, 1536), next8),
                  pl.BlockSpec((tm, 128), row), pl.BlockSpec((8, 128), next8),
                  pl.BlockSpec((tm, 1536), row), pl.BlockSpec((8, 1536), next8),
                  pl.BlockSpec((tm, 128), row),
                  pl.BlockSpec((8, 1536), const), pl.BlockSpec((1, 128), const), pl.BlockSpec((1, 128), const)],
        out_specs=[pl.BlockSpec((tm, 1536), row), pl.BlockSpec((tm, 128), row),
                   pl.BlockSpec((8, 1536), const), pl.BlockSpec((1, 128), const), pl.BlockSpec((1, 128), const)],
        out_shape=[jax.ShapeDtypeStruct((n, 1536), MM_DTYPE), jax.ShapeDtypeStruct((n, 128), F32),
                   jax.ShapeDtypeStruct((8, 1536), F32), jax.ShapeDtypeStruct((1, 128), F32),
                   jax.ShapeDtypeStruct((1, 128), F32)],
        scratch_shapes=[pltpu.VMEM((ext, 1536), F32)],
        compiler_params=_params("arbitrary"),
    )(pb, conv_out, conv_out, pd, pd, dqkv, dqkv, dgb, cw8, alog_v, dtb_v)


GDN_PREP_CHUNKS = 4
GDN_PREP_BWD_CHUNKS = 4
GDN_SCAN_CHUNKS = 2
MM_DTYPE = BF16


def _head_cols(ref, rows, base=0):
    return [ref[rows, base + h * 128:base + (h + 1) * 128] for h in range(GDN_HEADS)]


def _gdn_prep(qkv, gb):
    n = qkv.shape[0]
    r_per = GDN_PREP_CHUNKS
    tm = r_per * CHUNK

    def body(q_ref, k_ref, v_ref, gb_ref, u_ref, w_ref, qd_ref, kd_ref, qk_ref, t_ref, gam_ref):
        rowid = _iota((8, 128), 0)
        chunk_rows = [slice(r * CHUNK, (r + 1) * CHUNK) for r in range(r_per)]
        gather = lambda ref: [t for rows in chunk_rows for t in _head_cols(ref, rows)]
        u, w, qk, qd, kd, gamma, tinv = _gdn_prep_units(gather(q_ref), gather(k_ref), gather(v_ref),
                                                        [gb_ref[rows, :] for rows in chunk_rows])
        for r, rows in enumerate(chunk_rows):
            gam = jnp.zeros((8, 128), F32)
            for h in range(GDN_HEADS):
                i = r * GDN_HEADS + h
                cols = slice(h * 128, (h + 1) * 128)
                u_ref[rows, cols] = u[i]
                w_ref[rows, cols] = w[i].astype(MM_DTYPE)
                qd_ref[rows, cols] = qd[i].astype(MM_DTYPE)
                kd_ref[rows, cols] = kd[i].astype(MM_DTYPE)
                qk_ref[r, h] = qk[i].astype(MM_DTYPE)
                t_ref[r, h] = tinv[i].astype(MM_DTYPE)
                gam = jnp.where(rowid == h, gamma[i], gam)
            gam_ref[r] = gam

    tok = lambda j: pl.BlockSpec((tm, 512), lambda i: (i, j))
    return pl.pallas_call(
        body, name="gdn_prep", grid=(n // tm,),
        in_specs=[tok(0), tok(1), tok(2), pl.BlockSpec((tm, 128), lambda i: (i, 0))],
        out_specs=[tok(0)] * 4 + [pl.BlockSpec((r_per, GDN_HEADS, CHUNK, CHUNK), lambda i: (i, 0, 0, 0))] * 2
        + [pl.BlockSpec((r_per, 8, 128), lambda i: (i, 0, 0))],
        out_shape=[jax.ShapeDtypeStruct((n, 512), F32)] + [jax.ShapeDtypeStruct((n, 512), MM_DTYPE)] * 3
        + [jax.ShapeDtypeStruct((n // CHUNK, GDN_HEADS, CHUNK, CHUNK), MM_DTYPE)] * 2
        + [jax.ShapeDtypeStruct((n // CHUNK, 8, 128), F32)],
        compiler_params=_params("parallel"),
    )(qkv, qkv, qkv, gb)


def _gdn_fwd(qkv, gb, pc, nw, bl, seq):
    n = qkv.shape[0]
    nc = seq // CHUNK
    u, w, qd, kd, qk, tinv, gam = _gdn_prep(qkv, gb)
    tok3 = lambda t: t.reshape(bl, seq, 512)
    qk5 = qk.reshape(bl, nc, GDN_HEADS, CHUNK, CHUNK)
    gam4 = gam.reshape(bl, nc, 8, 128)

    r_per = GDN_SCAN_CHUNKS
    mm = lambda t: t.astype(MM_DTYPE)

    def body(u_ref, w_ref, qd_ref, kd_ref, qk_ref, gam_ref, og_ref, nw_ref, o_ref, y_ref, vn_ref, st_ref, s_scr):
        @pl.when(pl.program_id(0) == 0)
        def _():
            s_scr[...] = jnp.zeros_like(s_scr)

        units = [(b, h, slice(h * 128, (h + 1) * 128)) for b in range(bl) for h in range(GDN_HEADS)]
        ss = [s_scr[b, h] for b, h, _ in units]
        for r in range(r_per):
            rows = slice(r * CHUNK, (r + 1) * CHUNK)
            for (b, h, _), s in zip(units, ss):
                st_ref[b, r, h] = s
            ws_qs = [_dot(jnp.concatenate([w_ref[b, rows, cols], qd_ref[b, rows, cols]], axis=0), mm(s))
                     for (b, h, cols), s in zip(units, ss)]
            v_new = [u_ref[b, rows, cols] - m[0:CHUNK] for (b, h, cols), m in zip(units, ws_qs)]
            os_ = [m[CHUNK:2 * CHUNK] + _dot(qk_ref[b, r, h], mm(vn))
                   for (b, h, cols), m, vn in zip(units, ws_qs, v_new)]
            ss = [s * gam_ref[b, r, h:h + 1, :] + _dot_tn(kd_ref[b, rows, cols], mm(vn))
                  for (b, h, cols), s, vn in zip(units, ss, v_new)]
            for (b, h, cols), vn, o in zip(units, v_new, os_):
                vn_ref[b, rows, cols] = mm(vn)
                o_ref[b, rows, cols] = o
                y_ref[b, rows, cols] = mm(_rms_gate(o, nw_ref[...], og_ref[b, rows, cols]))
        for (b, h, _), s in zip(units, ss):
            s_scr[b, h] = s

    tok = pl.BlockSpec((bl, r_per * CHUNK, 512), lambda i: (0, i, 0))
    st_spec = pl.BlockSpec((bl, r_per, GDN_HEADS, 128, 128), lambda i: (0, i, 0, 0, 0))
    tok_shape = jax.ShapeDtypeStruct((bl, seq, 512), F32)
    o, y, vn, st = pl.pallas_call(
        body, name="gdn_scan_fwd", grid=(nc // r_per,),
        in_specs=[tok, tok, tok, tok,
                  pl.BlockSpec((bl, r_per, GDN_HEADS, CHUNK, CHUNK), lambda i: (0, i, 0, 0, 0)),
                  pl.BlockSpec((bl, r_per, 8, 128), lambda i: (0, i, 0, 0)), tok,
                  pl.BlockSpec(nw.shape, lambda i: (0, 0))],
        out_specs=[tok, tok, tok, st_spec],
        out_shape=[tok_shape, jax.ShapeDtypeStruct((bl, seq, 512), MM_DTYPE), jax.ShapeDtypeStruct((bl, seq, 512), MM_DTYPE),
                   jax.ShapeDtypeStruct((bl, nc, GDN_HEADS, 128, 128), F32)],
        scratch_shapes=[pltpu.VMEM((bl, GDN_HEADS, 128, 128), F32)],
        compiler_params=_params("arbitrary"),
    )(tok3(u), tok3(w), tok3(qd), tok3(kd), qk5, gam4, tok3(pc), nw)
    return y.reshape(n, 512), (o, st, w, qd, kd, qk5, gam4, tinv, vn)


def _gdn_bwd(qkv, gb, pc, res, dyb, nw, bl, seq):
    n = qkv.shape[0]
    nc = seq // CHUNK
    o, st, w, qd, kd, qk5, gam4, tinv, vn = res
    tok3 = lambda t: t.reshape(bl, seq, 512)

    def scan_body(dy_ref, o_ref, og_ref, w_ref, qd_ref, kd_ref, qk_ref, gam_ref, nw_ref,
                  do_ref, dog_ref, dvn_ref, dst_ref, dnw_ref, ds_scr):
        @pl.when(pl.program_id(0) == 0)
        def _():
            ds_scr[...] = jnp.zeros_like(ds_scr)
            dnw_ref[...] = jnp.zeros_like(dnw_ref)

        units = [(b, h, slice(h * 128, (h + 1) * 128)) for b in range(bl) for h in range(GDN_HEADS)]
        dnw = jnp.zeros(nw.shape, F32)
        dss = [ds_scr[b, h] for b, h, _ in units]
        for r in reversed(range(r_scan)):
            rows = slice(r * CHUNK, (r + 1) * CHUNK)
            d_os = []
            for b, h, cols in units:
                _, vjp = jax.vjp(_rms_gate, o_ref[b, rows, cols], nw_ref[...], og_ref[b, rows, cols])
                d_o, dnw_h, dog = vjp(dy_ref[b, rows, cols])
                do_ref[b, rows, cols] = mm(d_o)
                dog_ref[b, rows, cols] = mm(dog)
                dnw = dnw + dnw_h
                d_os.append(mm(d_o))
            for (b, h, _), ds in zip(units, dss):
                dst_ref[b, r, h] = ds
            dvn_a = [_dot(kd_ref[b, rows, cols], mm(ds)) for (b, h, cols), ds in zip(units, dss)]
            dvns = [a + _dot_tn(qk_ref[b, r, h], d_o) for (b, h, cols), a, d_o in zip(units, dvn_a, d_os)]
            for (b, h, cols), dvn in zip(units, dvns):
                dvn_ref[b, rows, cols] = mm(dvn)
            dss = [ds * gam_ref[b, r, h:h + 1, :] + _dot_tn(
                jnp.concatenate([qd_ref[b, rows, cols], w_ref[b, rows, cols]], axis=0),
                jnp.concatenate([d_o, mm(-dvn)], axis=0))
                for (b, h, cols), d_o, ds, dvn in zip(units, d_os, dss, dvns)]
        dnw_ref[...] += dnw
        for (b, h, _), ds in zip(units, dss):
            ds_scr[b, h] = ds

    r_scan = GDN_SCAN_CHUNKS
    mm = lambda t: t.astype(MM_DTYPE)
    rev = lambda i: nc // r_scan - 1 - i
    tok = pl.BlockSpec((bl, r_scan * CHUNK, 512), lambda i: (0, rev(i), 0))
    st_spec = pl.BlockSpec((bl, r_scan, GDN_HEADS, 128, 128), lambda i: (0, rev(i), 0, 0, 0))
    tok_shape = jax.ShapeDtypeStruct((bl, seq, 512), F32)
    tok_mm = jax.ShapeDtypeStruct((bl, seq, 512), MM_DTYPE)
    d_o, dog, dvn, dst, dnw = pl.pallas_call(
        scan_body, name="gdn_scan_bwd", grid=(nc // r_scan,),
        in_specs=[tok] * 6 + [pl.BlockSpec((bl, r_scan, GDN_HEADS, CHUNK, CHUNK), lambda i: (0, rev(i), 0, 0, 0)),
                              pl.BlockSpec((bl, r_scan, 8, 128), lambda i: (0, rev(i), 0, 0)),
                              pl.BlockSpec(nw.shape, lambda i: (0, 0))],
        out_specs=[tok, tok, tok, st_spec, pl.BlockSpec(nw.shape, lambda i: (0, 0))],
        out_shape=[tok_mm, tok_mm, tok_mm, jax.ShapeDtypeStruct(st.shape, F32),
                   jax.ShapeDtypeStruct(nw.shape, F32)],
        scratch_shapes=[pltpu.VMEM((bl, GDN_HEADS, 128, 128), F32)],
        compiler_params=_params("arbitrary"),
    )(tok3(dyb), o, tok3(pc), tok3(w), tok3(qd), tok3(kd), qk5, gam4, nw)

    r_per = GDN_PREP_BWD_CHUNKS
    tm = r_per * CHUNK

    def prep_body(q_ref, k_ref, v_ref, gb_ref, t_ref, st_ref, dst_ref, dvn_ref, do_ref, vn_ref, dqkv_ref, dgb_ref):
        chunk_rows = [slice(r * CHUNK, (r + 1) * CHUNK) for r in range(r_per)]
        gather = lambda ref: [t for rows in chunk_rows for t in _head_cols(ref, rows)]
        units = [(r, h) for r in range(r_per) for h in range(GDN_HEADS)]
        t_known = [t_ref[r, h].astype(F32) for r, h in units]
        prep = lambda q, k, v, g: _gdn_prep_units(q, k, v, g, t_known)[:6]
        _, vjp = jax.vjp(prep, gather(q_ref), gather(k_ref), gather(v_ref), [gb_ref[rows, :] for rows in chunk_rows])
        ss = [st_ref[r, h] for r, h in units]
        dss = [dst_ref[r, h] for r, h in units]
        dvns, d_os, v_new = gather(dvn_ref), gather(do_ref), gather(vn_ref)
        both = [_dot_nt(jnp.concatenate([dvn, d_o], axis=0), s.astype(MM_DTYPE)) for dvn, d_o, s in zip(dvns, d_os, ss)]
        d_w = [-m[0:CHUNK] for m in both]
        d_qd = [m[CHUNK:2 * CHUNK] for m in both]
        d_qk = [_dot_nt(d_o, vn) for d_o, vn in zip(d_os, v_new)]
        d_kd = [_dot_nt(vn, ds.astype(MM_DTYPE)) for vn, ds in zip(v_new, dss)]
        d_gam = [_sum_rows(ds * s) for ds, s in zip(dss, ss)]
        dq, dk, dv, dgb = vjp(([d.astype(F32) for d in dvns], d_w, d_qk, d_qd, d_kd, d_gam))
        for i, (r, h) in enumerate(units):
            rows = chunk_rows[r]
            for part, d in enumerate((dq, dk, dv)):
                dqkv_ref[rows, part * 512 + h * 128:part * 512 + (h + 1) * 128] = d[i]
        for r, rows in enumerate(chunk_rows):
            dgb_ref[rows, :] = dgb[r]

    tokp = lambda j: pl.BlockSpec((tm, 512), lambda i: (i, j))
    st4 = pl.BlockSpec((r_per, GDN_HEADS, 128, 128), lambda i: (i, 0, 0, 0))
    dqkv, dgb = pl.pallas_call(
        prep_body, name="gdn_prep_bwd", grid=(n // tm,),
        in_specs=[tokp(0), tokp(1), tokp(2), pl.BlockSpec((tm, 128), lambda i: (i, 0)),
                  pl.BlockSpec((r_per, GDN_HEADS, CHUNK, CHUNK), lambda i: (i, 0, 0, 0)), st4, st4,
                  tokp(0), tokp(0), tokp(0)],
        out_specs=[pl.BlockSpec((tm, 1536), lambda i: (i, 0)), pl.BlockSpec((tm, 128), lambda i: (i, 0))],
        out_shape=[jax.ShapeDtypeStruct((n, 1536), F32), jax.ShapeDtypeStruct((n, 128), F32)],
        compiler_params=_params("parallel"),
    )(qkv, qkv, qkv, gb, tinv, st.reshape(bl * nc, GDN_HEADS, 128, 128), dst.reshape(bl * nc, GDN_HEADS, 128, 128),
      dvn.reshape(n, 512), d_o.reshape(n, 512), vn.reshape(n, 512))
    return dqkv, dog.reshape(n, 512), dgb, dnw


def _out_block(x2, tgt2, ya, yb, g1p3, wo, lnw, lnb, seq, tm=256):
    n = x2.shape[0]
    tpe = seq // tm
    bl = n // seq

    def body(x_ref, t_ref, ya_ref, yb_ref, g_ref, wo_ref, lnw_ref, lnb_ref,
             dz_ref, dya_ref, dyb_ref, dwo_ref, dg_ref, glw_ref, glb_ref, loss_ref):
        i = pl.program_id(0)

        @pl.when(i == 0)
        def _():
            dwo_ref[...] = jnp.zeros_like(dwo_ref)
            glw_ref[...] = jnp.zeros_like(glw_ref)
            glb_ref[...] = jnp.zeros_like(glb_ref)
            loss_ref[...] = jnp.zeros_like(loss_ref)

        @pl.when(i % tpe == 0)
        def _():
            dg_ref[...] = jnp.zeros_like(dg_ref)

        ya16 = ya_ref[...].astype(wo.dtype)
        yb16 = yb_ref[...].astype(wo.dtype)
        wa = wo_ref[0:GLA_WIDTH, :]
        wb = wo_ref[GLA_WIDTH:, :]
        y = _dot(ya16, wa) + _dot(yb16, wb)
        g1p = g_ref[0]
        z = ALPHA * x_ref[...] + g1p * y
        mu = jnp.mean(z, axis=-1, keepdims=True)
        zc = z - mu
        rstd = lax.rsqrt(jnp.mean(zc * zc, axis=-1, keepdims=True) + LN_EPS)
        xhat = zc * rstd
        diff = xhat * lnw_ref[...] + lnb_ref[...] - t_ref[...]
        loss_ref[...] += (0.5 / D_MODEL) * jnp.sum(jnp.sum(diff * diff, axis=-1, keepdims=True), axis=0, keepdims=True)
        dout = diff * (1.0 / D_MODEL)
        glw_ref[...] += jnp.sum(dout * xhat, axis=0, keepdims=True)
        glb_ref[...] += jnp.sum(dout, axis=0, keepdims=True)
        dxh = dout * lnw_ref[...]
        dz = rstd * (dxh - jnp.mean(dxh, axis=-1, keepdims=True)
                     - xhat * jnp.mean(dxh * xhat, axis=-1, keepdims=True))
        dz_ref[...] = dz
        dg_ref[0] += jnp.sum(dz * y, axis=0, keepdims=True)
        dy = (g1p * dz).astype(wo.dtype)
        dya_ref[...] = _dot_nt(dy, wa)
        dyb_ref[...] = _dot_nt(dy, wb)
        dwo_ref[0:GLA_WIDTH, :] += _dot_tn(ya16, dy)
        dwo_ref[GLA_WIDTH:, :] += _dot_tn(yb16, dy)

    row = lambda i: (i, 0)
    const = lambda i: (0, 0)
    per_ex = pl.BlockSpec((1, 1, D_MODEL), lambda i: (i // tpe, 0, 0))
    return pl.pallas_call(
        body, name="out_block", grid=(n // tm,),
        in_specs=[pl.BlockSpec((tm, D_MODEL), row), pl.BlockSpec((tm, D_MODEL), row),
                  pl.BlockSpec((tm, 512), row), pl.BlockSpec((tm, 512), row), per_ex,
                  pl.BlockSpec((D_MODEL, D_MODEL), const), pl.BlockSpec((1, D_MODEL), const),
                  pl.BlockSpec((1, D_MODEL), const)],
        out_specs=[pl.BlockSpec((tm, D_MODEL), row), pl.BlockSpec((tm, 512), row), pl.BlockSpec((tm, 512), row),
                   pl.BlockSpec((D_MODEL, D_MODEL), const), per_ex,
                   pl.BlockSpec((1, D_MODEL), const), pl.BlockSpec((1, D_MODEL), const),
                   pl.BlockSpec((1, 1), const)],
        out_shape=[jax.ShapeDtypeStruct((n, D_MODEL), F32), jax.ShapeDtypeStruct((n, 512), F32),
                   jax.ShapeDtypeStruct((n, 512), F32), jax.ShapeDtypeStruct((D_MODEL, D_MODEL), F32),
                   jax.ShapeDtypeStruct((bl, 1, D_MODEL), F32), jax.ShapeDtypeStruct((1, D_MODEL), F32),
                   jax.ShapeDtypeStruct((1, D_MODEL), F32), jax.ShapeDtypeStruct((1, 1), F32)],
        compiler_params=_params("arbitrary"),
    )(x2, tgt2, ya, yb, g1p3, wo, lnw, lnb)


def _proj_bwd_x(ds, ws, x2, dz, sc3, seq, tm=256):
    n = x2.shape[0]
    tpe = seq // tm
    bl = n // seq

    def body(da_ref, db_ref, dc_ref, dd1_ref, dd2_ref, wa_ref, wb_ref, wc_ref, wd_ref, x_ref, dz_ref, sc_ref,
             gx_ref, dsh_ref, dsc_ref):
        i = pl.program_id(0)

        @pl.when(i % tpe == 0)
        def _():
            dsh_ref[...] = jnp.zeros_like(dsh_ref)
            dsc_ref[...] = jnp.zeros_like(dsc_ref)

        cdt = ws[0].dtype
        dh = _dot(da_ref[...].astype(cdt), wa_ref[...])
        dh += _dot(db_ref[...].astype(cdt), wb_ref[...])
        dh += _dot(dc_ref[...].astype(cdt), wc_ref[...])
        dh += _dot((dd1_ref[...] + dd2_ref[...]).astype(cdt), wd_ref[...])
        gx_ref[...] = dh * sc_ref[0] + ALPHA * dz_ref[...]
        dsh_ref[0] += jnp.sum(dh, axis=0, keepdims=True)
        dsc_ref[0] += jnp.sum(dh * x_ref[...], axis=0, keepdims=True)

    row = lambda i: (i, 0)
    const = lambda i: (0, 0)
    per_ex = pl.BlockSpec((1, 1, D_MODEL), lambda i: (i // tpe, 0, 0))
    da, db, dc, (dd1, dd2) = ds
    return pl.pallas_call(
        body, name="proj_bwd_x", grid=(n // tm,),
        in_specs=[pl.BlockSpec((tm, d.shape[1]), row) for d in (da, db, dc, dd1, dd2)]
        + [pl.BlockSpec(w.shape, const) for w in ws]
        + [pl.BlockSpec((tm, D_MODEL), row), pl.BlockSpec((tm, D_MODEL), row), per_ex],
        out_specs=[pl.BlockSpec((tm, D_MODEL), row), per_ex, per_ex],
        out_shape=[jax.ShapeDtypeStruct((n, D_MODEL), F32), jax.ShapeDtypeStruct((bl, 1, D_MODEL), F32),
                   jax.ShapeDtypeStruct((bl, 1, D_MODEL), F32)],
        compiler_params=_params("arbitrary"),
    )(da, db, dc, dd1, dd2, *ws, x2, dz, sc3)


def _proj_bwd_w(x2, sc3, sh3, ds, seq, cdt, name, tm=256):
    n = x2.shape[0]
    tpe = seq // tm
    flat, groups = [], []
    for d in ds:
        parts = d if isinstance(d, tuple) else (d,)
        groups.append(len(parts))
        flat.extend(parts)
    nin = len(flat)

    def body(x_ref, sc_ref, sh_ref, *refs):
        i = pl.program_id(0)
        outs = refs[nin:]

        @pl.when(i == 0)
        def _():
            for o in outs:
                o[...] = jnp.zeros_like(o)

        h = (x_ref[...] * sc_ref[0] + sh_ref[0]).astype(cdt)
        pos = 0
        for o, cnt in zip(outs, groups):
            d = refs[pos][...]
            for extra in refs[pos + 1:pos + cnt]:
                d = d + extra[...]
            pos += cnt
            o[...] += _dot_tn(d.astype(cdt), h)

    row = lambda i: (i, 0)
    const = lambda i: (0, 0)
    per_ex = pl.BlockSpec((1, 1, D_MODEL), lambda i: (i // tpe, 0, 0))
    widths = [(d[0] if isinstance(d, tuple) else d).shape[1] for d in ds]
    return pl.pallas_call(
        body, name=name, grid=(n // tm,),
        in_specs=[pl.BlockSpec((tm, D_MODEL), row), per_ex, per_ex]
        + [pl.BlockSpec((tm, d.shape[1]), row) for d in flat],
        out_specs=[pl.BlockSpec((w, D_MODEL), const) for w in widths],
        out_shape=[jax.ShapeDtypeStruct((w, D_MODEL), F32) for w in widths],
        compiler_params=_params("arbitrary"),
    )(x2, sc3, sh3, *flat)


def _mod_block(c_all, w_ada_sh, b_blk):
    def body(c_ref, w_ref, b_ref, o_ref):
        o_ref[...] = _dot(c_ref[...], w_ref[...]) + b_ref[...]

    return pl.pallas_call(
        body, name="mod_block",
        out_shape=jax.ShapeDtypeStruct((c_all.shape[0], w_ada_sh.shape[1]), F32),
        compiler_params=pltpu.CompilerParams(vmem_limit_bytes=VMEM_LIMIT),
    )(c_all, w_ada_sh, b_blk)


def _ada_grads(c_all, dmod_all, dmod_blk):
    def body(c_ref, da_ref, db_ref, gw_ref, gb_ref):
        gw_ref[...] = _dot_tn(c_ref[...], db_ref[...])
        gb_ref[...] = jnp.sum(da_ref[...], axis=0, keepdims=True)

    return pl.pallas_call(
        body, name="ada_grads",
        out_shape=[jax.ShapeDtypeStruct((c_all.shape[1], dmod_blk.shape[1]), F32),
                   jax.ShapeDtypeStruct((1, dmod_all.shape[1]), F32)],
        compiler_params=pltpu.CompilerParams(vmem_limit_bytes=VMEM_LIMIT),
    )(c_all, dmod_all, dmod_blk)


def _sum_leading(parts, name):
    def body(p_ref, o_ref):
        acc = p_ref[0]
        for d in range(1, parts.shape[0]):
            acc = acc + p_ref[d]
        o_ref[...] = acc

    return pl.pallas_call(
        body, name=name, out_shape=jax.ShapeDtypeStruct(parts.shape[1:], F32),
        compiler_params=pltpu.CompilerParams(vmem_limit_bytes=VMEM_LIMIT),
    )(parts)


ELEMENTWISE_BLOCK_BYTES = 2 * 1024 * 1024


def _tile2d(rows, cols, row_align=8):
    if rows * cols * 4 <= ELEMENTWISE_BLOCK_BYTES:
        return rows, cols
    fits = [t for t in range(row_align, rows, row_align) if rows % t == 0 and t * cols * 4 <= ELEMENTWISE_BLOCK_BYTES]
    if fits:
        return fits[-1], cols
    fits = [t for t in range(128, cols, 128) if cols % t == 0 and rows * t * 4 <= ELEMENTWISE_BLOCK_BYTES]
    assert fits, (rows, cols)
    return rows, fits[-1]


def _add_n(arrs, name, out_dtypes=(F32,)):
    rows, cols = arrs[0].shape
    narrow = any(jnp.dtype(dt).itemsize < 4 for dt in tuple(out_dtypes) + tuple(a.dtype for a in arrs))
    tr, tc = _tile2d(rows, cols, 16 if narrow else 8)
    n_in = len(arrs)

    def body(*refs):
        acc = refs[0][...].astype(F32)
        for r in refs[1:n_in]:
            acc = acc + r[...].astype(F32)
        for o in refs[n_in:]:
            o[...] = acc.astype(o.dtype)

    spec = pl.BlockSpec((tr, tc), lambda i, j: (i, j))
    return pl.pallas_call(
        body, name=name, grid=(rows // tr, cols // tc), in_specs=[spec] * n_in, out_specs=[spec] * len(out_dtypes),
        out_shape=[jax.ShapeDtypeStruct((rows, cols), dt) for dt in out_dtypes],
        compiler_params=_params("parallel", "parallel"),
    )(*arrs)


def _chip_sum_blocks(a, b, per, blocks, name, chunk=128):
    rows, cols = a.shape
    padded = -(-per // 16) * 16
    assert rows >= (blocks - 1) * per + padded, (rows, per, blocks)

    def body(a_ref, b_ref, o_ref, o16_ref):
        for j in range(blocks):
            for r0 in range(0, padded, chunk):
                n_rows = min(chunk, padded - r0)
                src = pl.ds(j * per + r0, n_rows)
                s = a_ref[src, :] + b_ref[src, :]
                if per - r0 < n_rows:
                    s = jnp.where(_iota((n_rows, 1), 0) < per - r0, s, 0.0)
                o_ref[j, r0:r0 + n_rows, :] = s
                o16_ref[j, r0:r0 + n_rows, :] = s.astype(BF16)

    return pl.pallas_call(
        body, name=name,
        out_shape=[jax.ShapeDtypeStruct((blocks, padded, cols), F32), jax.ShapeDtypeStruct((blocks, padded, cols), BF16)],
        compiler_params=pltpu.CompilerParams(vmem_limit_bytes=VMEM_LIMIT),
    )(a, b)


GRAD_PAD_ROWS = 16


def _adamw(w, g, m, v, name):
    rows, cols = w.shape
    tr, tc = _tile2d(rows, cols)
    c1 = 1.0 / (1.0 - ADAM_B1 ** ADAM_STEP)
    c2 = 1.0 / (1.0 - ADAM_B2 ** ADAM_STEP)

    def body(w_ref, g_ref, m_ref, v_ref, d_ref, nm_ref, nv_ref):
        gg = g_ref[...]
        nm = ADAM_B1 * m_ref[...] + (1.0 - ADAM_B1) * gg
        nv = ADAM_B2 * v_ref[...] + (1.0 - ADAM_B2) * (gg * gg)
        nm_ref[...] = nm
        nv_ref[...] = nv
        d_ref[...] = -ADAM_LR * ((nm * c1) / (jnp.sqrt(nv * c2) + ADAM_EPS) + ADAM_WD * w_ref[...])

    spec = pl.BlockSpec((tr, tc), lambda i, j: (i, j))
    shp = jax.ShapeDtypeStruct((rows, cols), F32)
    return pl.pallas_call(
        body, name=name, grid=(rows // tr, cols // tc), in_specs=[spec] * 4, out_specs=[spec] * 3,
        out_shape=[shp, shp, shp], compiler_params=_params("parallel", "parallel"),
    )(w, g, m, v)


def _coords():
    return lax.axis_index("x"), lax.axis_index("y"), lax.axis_index("c")


def _all_gather8(blk, name):
    m_per, n = blk.shape

    def body(x_ref, out_ref, send_sems, recv_sems, local_sem):
        x, y, c = _coords()
        me, sibling = (x, y, c), (x, y, 1 - c)
        chips = [(1 - x, y), (x, 1 - y), (1 - x, 1 - y)]

        def rows(px, py, pc):
            return out_ref.at[pl.ds((4 * px + 2 * py + pc) * m_per, m_per), :]

        def copy(k, block, to, src=None):
            return pltpu.make_async_remote_copy(
                src_ref=rows(*block) if src is None else src, dst_ref=rows(*block),
                send_sem=send_sems.at[k], recv_sem=recv_sems.at[k], device_id=to, device_id_type=MESH)

        mine = pltpu.make_async_copy(x_ref, rows(*me), local_sem)
        mine.start()
        first = [copy(0, me, sibling, src=x_ref)]
        first += [copy(1 + j, me, (*chip, c), src=x_ref) for j, chip in enumerate(chips)]
        for cp in first:
            cp.start()
        passed = [copy(4 + j, (*chip, c), sibling) for j, chip in enumerate(chips)]
        for j, chip in enumerate(chips):
            copy(1 + j, (*chip, c), me).wait_recv()
            passed[j].start()
        copy(0, sibling, me).wait_recv()
        for j, chip in enumerate(chips):
            copy(4 + j, (*chip, 1 - c), me).wait_recv()
        for cp in first + passed:
            cp.wait_send()
        mine.wait()

    return pl.pallas_call(
        body, name=name,
        out_shape=jax.ShapeDtypeStruct((8 * m_per, n), blk.dtype),
        in_specs=[pl.BlockSpec(memory_space=pltpu.VMEM)],
        out_specs=pl.BlockSpec(memory_space=pltpu.VMEM),
        scratch_shapes=[pltpu.SemaphoreType.DMA((7,)), pltpu.SemaphoreType.DMA((7,)), pltpu.SemaphoreType.DMA],
        compiler_params=pltpu.CompilerParams(vmem_limit_bytes=VMEM_LIMIT),
    )(blk)


def _chip_gather(shards, split, name):
    k_arr = len(shards)

    def body(*refs):
        srcs, dsts = refs[:k_arr], refs[k_arr:2 * k_arr]
        send_sems, recv_sems, fwd_send_sems, fwd_recv_sems, local_sems = refs[2 * k_arr:]
        x, y, c = _coords()
        peers = [(1 - x, y, c), (x, 1 - y, c), (1 - x, 1 - y, c)]
        sibling = (x, y, 1 - c)
        me_chip = 2 * x + y

        def part(ref, a, core):
            if not split[a]:
                return ref
            half = shards[a].shape[1] // 2
            return ref.at[:, pl.ds(core * half, half)]

        def ici(a, j, src_chip, dst_dev):
            return pltpu.make_async_remote_copy(
                src_ref=part(srcs[a], a, c), dst_ref=part(dsts[a].at[src_chip], a, c),
                send_sem=send_sems.at[a, j], recv_sem=recv_sems.at[a, j], device_id=dst_dev, device_id_type=MESH)

        def d2d(a, j, src_chip, core):
            return pltpu.make_async_remote_copy(
                src_ref=part(dsts[a].at[src_chip], a, core), dst_ref=part(dsts[a].at[src_chip], a, core),
                send_sem=fwd_send_sems.at[a, j], recv_sem=fwd_recv_sems.at[a, j],
                device_id=sibling, device_id_type=MESH)

        local = [pltpu.make_async_copy(srcs[a], dsts[a].at[me_chip], local_sems.at[a]) for a in range(k_arr)]
        for cp in local:
            cp.start()
        sends = [ici(a, j, me_chip, peer) for a in range(k_arr) for j, peer in enumerate(peers)]
        for cp in sends:
            cp.start()
        forwards = []
        for a in range(k_arr):
            for j, peer in enumerate(peers):
                peer_chip = 2 * peer[0] + peer[1]
                ici(a, j, peer_chip, peer).wait_recv()
                if split[a]:
                    forwards.append(d2d(a, j, peer_chip, c))
                    forwards[-1].start()
        for a in range(k_arr):
            for j, peer in enumerate(peers):
                if split[a]:
                    d2d(a, j, 2 * peer[0] + peer[1], 1 - c).wait_recv()
        for cp in sends + forwards:
            cp.wait_send()
        for cp in local:
            cp.wait()

    any_spec = pl.BlockSpec(memory_space=pl.ANY)
    return pl.pallas_call(
        body, name=name,
        out_shape=[jax.ShapeDtypeStruct((4,) + s.shape, s.dtype) for s in shards],
        in_specs=[any_spec] * k_arr, out_specs=[any_spec] * k_arr,
        scratch_shapes=[pltpu.SemaphoreType.DMA((k_arr, 3))] * 4 + [pltpu.SemaphoreType.DMA((k_arr,))],
    )(*shards)


def _chip_scatter(pieces, name):
    k_arr = len(pieces)

    def body(*refs):
        srcs, dsts = refs[:k_arr], refs[k_arr:2 * k_arr]
        send_sems, recv_sems = refs[2 * k_arr:]
        x, y, c = _coords()
        peers = [(1 - x, y, c), (x, 1 - y, c), (1 - x, 1 - y, c)]
        copies = []
        for a in range(k_arr):
            for j, peer in enumerate(peers):
                copies.append(pltpu.make_async_remote_copy(
                    src_ref=srcs[a].at[2 * peer[0] + peer[1]], dst_ref=dsts[a].at[j],
                    send_sem=send_sems.at[a, j], recv_sem=recv_sems.at[a, j], device_id=peer, device_id_type=MESH))
        for cp in copies:
            cp.start()
        for cp in copies:
            cp.wait_recv()
        for cp in copies:
            cp.wait_send()

    any_spec = pl.BlockSpec(memory_space=pl.ANY)
    return pl.pallas_call(
        body, name=name,
        out_shape=[jax.ShapeDtypeStruct((3,) + p.shape[1:], p.dtype) for p in pieces],
        in_specs=[any_spec] * k_arr, out_specs=[any_spec] * k_arr,
        scratch_shapes=[pltpu.SemaphoreType.DMA((k_arr, 3)), pltpu.SemaphoreType.DMA((k_arr, 3))],
    )(*pieces)


def _sibling_swap(arrs, name):
    k_arr = len(arrs)

    def body(*refs):
        srcs, dsts = refs[:k_arr], refs[k_arr:2 * k_arr]
        send_sems, recv_sems = refs[2 * k_arr:]
        x, y, c = _coords()
        copies = [pltpu.make_async_remote_copy(
            src_ref=srcs[a], dst_ref=dsts[a], send_sem=send_sems.at[a], recv_sem=recv_sems.at[a],
            device_id=(x, y, 1 - c), device_id_type=MESH) for a in range(k_arr)]
        for cp in copies:
            cp.start()
        for cp in copies:
            cp.wait_recv()
        for cp in copies:
            cp.wait_send()

    any_spec = pl.BlockSpec(memory_space=pl.ANY)
    return pl.pallas_call(
        body, name=name,
        out_shape=[jax.ShapeDtypeStruct(a.shape, a.dtype) for a in arrs],
        in_specs=[any_spec] * k_arr, out_specs=[any_spec] * k_arr,
        scratch_shapes=[pltpu.SemaphoreType.DMA((k_arr,)), pltpu.SemaphoreType.DMA((k_arr,))],
    )(*arrs)


def _split_w_in(w_in_t):
    wa = jnp.concatenate([w_in_t[0:1024], w_in_t[1040:1552]], axis=0)
    wb = w_in_t[1552:3088]
    wc = w_in_t[3096:3608]
    wd = jnp.concatenate([w_in_t[1024:1040], w_in_t[3088:3096],
                          jnp.zeros((128 - SMALL_USED, w_in_t.shape[1]), w_in_t.dtype)], axis=0)
    return wa, wb, wc, wd


def _merge_dw_in(dwa, dwb, dwc, dwd):
    return jnp.concatenate([dwa[0:1024], dwd[0:GLA_RANK], dwa[1024:1536], dwb, dwd[GLA_RANK:SMALL_USED], dwc,
                            jnp.zeros((GRAD_PAD_ROWS, dwa.shape[1]), dwa.dtype)], axis=0)


def _local_step(x, mod, w_in16, w_out16, gla_wg, gla_bg, gla_nw, conv_w, a_log, dt_bias, gdn_nw, ln_w, ln_b, tgt):
    bl, seq, _ = x.shape
    n = bl * seq
    x2 = x.reshape(n, D_MODEL)
    tgt2 = tgt.reshape(n, D_MODEL)
    sh3 = mod[:, None, 0:D_MODEL]
    sc3 = 1.0 + mod[:, None, D_MODEL:2 * D_MODEL]
    g1p3 = 1.0 + mod[:, None, 2 * D_MODEL:]
    ws = _split_w_in(w_in16)
    wg = jnp.concatenate([gla_wg, jnp.zeros((128 - GLA_RANK, GLA_QK), F32)], axis=0)
    cw8 = jnp.concatenate([conv_w, jnp.zeros((8 - CONV_K, conv_w.shape[1]), F32)], axis=0)
    alog_v = jnp.zeros((1, 128), F32).at[:, LANE_A:LANE_A + GDN_HEADS].set(a_log)
    dtb_v = jnp.zeros((1, 128), F32).at[:, LANE_A:LANE_A + GDN_HEADS].set(dt_bias)

    pa, pb, pc, pd = _proj_fwd(x2, sc3, sh3, ws, seq)
    ya, st_a = _gla_fwd(pa, pd, wg, gla_bg, gla_nw, bl, seq)
    qkv, gb, conv_out = _gdn_pre_fwd(pb, pd, cw8, alog_v, dtb_v, bl, seq)
    yb, st_b = _gdn_fwd(qkv, gb, pc, gdn_nw, bl, seq)
    dz, dya, dyb, d_wo, d_gate, d_lnw, d_lnb, loss = _out_block(x2, tgt2, ya, yb, g1p3, w_out16, ln_w, ln_b, seq)
    da, dd1, d_wg, d_bg, d_nwa = _gla_bwd(pa, pd, st_a, dya, wg, gla_bg, gla_nw, bl, seq)
    dqkv, dc, dgb, d_nwb = _gdn_bwd(qkv, gb, pc, st_b, dyb, gdn_nw, bl, seq)
    db, dd2, d_cw8, d_alog, d_dtb = _gdn_pre_bwd(pb, conv_out, pd, dqkv, dgb, cw8, alog_v, dtb_v, bl, seq)
    gx, d_sh, d_sc = _proj_bwd_x((da, db, dc, (dd1, dd2)), ws, x2, dz, sc3, seq)
    (dwa,) = _proj_bwd_w(x2, sc3, sh3, [da], seq, w_in16.dtype, "proj_bwd_w_a")
    dwb, dwc, dwd = _proj_bwd_w(x2, sc3, sh3, [db, dc, (dd1, dd2)], seq, w_in16.dtype, "proj_bwd_w_bcd")
    grads = dict(
        w_in=_merge_dw_in(dwa, dwb, dwc, dwd),
        w_out=d_wo,
        gla_w_gate_up=d_wg[0:GLA_RANK, :],
        gla_b_gate=d_bg,
        gla_norm_w=d_nwa,
        gdn_conv_w=d_cw8[0:CONV_K, :],
        gdn_a_log=d_alog[:, LANE_A:LANE_A + GDN_HEADS],
        gdn_dt_bias=d_dtb[:, LANE_A:LANE_A + GDN_HEADS],
        gdn_norm_w=d_nwb,
        ln_w=d_lnw,
        ln_b=d_lnb,
        mod=jnp.concatenate([d_sh[:, 0, :], d_sc[:, 0, :], d_gate[:, 0, :]], axis=1),
    )
    return loss, gx.reshape(bl, seq, D_MODEL), grads


_SMALL = (("gla_b_gate", 256), ("gla_norm_w", 128), ("gdn_a_log", 4), ("gdn_dt_bias", 4), ("gdn_norm_w", 128),
          ("ln_w", 1024), ("ln_b", 1024), ("gla_w_gate_up", 16 * 256), ("gdn_conv_w", 4 * 1536), ("loss", 1),
          ("mod", 2 * 3072))


def _pack_small(grads):
    flat = jnp.concatenate([grads[k].reshape(-1) for k, _ in _SMALL])
    total = sum(sz for _, sz in _SMALL)
    rows = -(-total // 1024) * 8
    return jnp.concatenate([flat, jnp.zeros((rows * 128 - total,), F32)]).reshape(rows, 128)


def _unpack_small(flat):
    out, pos = {}, 0
    for k, sz in _SMALL:
        out[k] = flat[pos:pos + sz]
        pos += sz
    return out


def kernel(x, c, w_ada, b_ada, w_in, gla_w_gate_up, gla_b_gate, gla_norm_w, gdn_conv_w, gdn_a_log, gdn_dt_bias, gdn_norm_w, w_out, ln_w, ln_b, loss_target, m_w_ada, m_b_ada, m_w_in, m_gla_w_gate_up, m_gla_b_gate, m_gla_norm_w, m_gdn_conv_w, m_gdn_a_log, m_gdn_dt_bias, m_gdn_norm_w, m_w_out, m_ln_w, m_ln_b, v_w_ada, v_b_ada, v_w_in, v_gla_w_gate_up, v_gla_b_gate, v_gla_norm_w, v_gdn_conv_w, v_gdn_a_log, v_gdn_dt_bias, v_gdn_norm_w, v_w_out, v_ln_w, v_ln_b):
    ix, iy, ic = _coords()
    chip = 2 * ix + iy
    dev = 4 * ix + 2 * iy + ic
    bl = x.shape[0]
    ndev = 8

    c_all = _all_gather8(c.reshape(8, -1), "gather_c").reshape(ndev * bl, D_MODEL)
    ada_cols = w_ada.shape[2]
    b_blk = lax.dynamic_slice_in_dim(b_ada, chip * ada_cols, ada_cols, axis=1)
    mod_blk = _mod_block(c_all, w_ada[0], b_blk)
    mod_g = _all_gather8(mod_blk, "gather_mod").reshape(ndev, ndev * bl, ada_cols)
    mod_all = jnp.concatenate([mod_g[2 * j] for j in range(4)], axis=1)
    mod = lax.dynamic_slice_in_dim(mod_all, dev * bl, bl, axis=0)

    w_in_g, w_out_g, wg_g, cw_g = _chip_gather(
        [jnp.transpose(w_in[0]).astype(BF16), w_out[0].astype(BF16), gla_w_gate_up[0], gdn_conv_w[0]],
        [True, True, False, False], "gather_weights")
    w_in16 = w_in_g.reshape(IN_COLS, D_MODEL)
    w_out16 = w_out_g.reshape(D_MODEL, D_MODEL)
    gla_wg = jnp.concatenate([wg_g[j] for j in range(4)], axis=1)
    conv_w = jnp.concatenate([cw_g[j] for j in range(4)], axis=1)

    loss, grad_x, gr = _local_step(x, mod, w_in16, w_out16, gla_wg, gla_b_gate, gla_norm_w, conv_w,
                                   gdn_a_log, gdn_dt_bias, gdn_norm_w, ln_w, ln_b, loss_target)

    gr["loss"] = loss
    packed = _pack_small(gr)
    prow = packed.shape[0]
    gathered = _all_gather8(packed, "gather_small").reshape(ndev, prow, 128)
    small = _unpack_small(_sum_leading(gathered, "sum_small").reshape(-1))
    loss = small["loss"][0]
    mod_rows = gathered.reshape(ndev, prow * 128)[:, sum(sz for _, sz in _SMALL[:-1]):][:, :bl * 3 * D_MODEL]
    dmod_all = mod_rows.reshape(ndev * bl, 3 * D_MODEL)
    dmod_blk = lax.dynamic_slice_in_dim(dmod_all, chip * ada_cols, ada_cols, axis=1)
    g_w_ada, g_b_ada = _ada_grads(c_all, dmod_all, dmod_blk)
    wg_cols = gla_w_gate_up.shape[2]
    g_wg = lax.dynamic_slice_in_dim(small["gla_w_gate_up"].reshape(GLA_RANK, GLA_QK), chip * wg_cols, wg_cols, axis=1)
    cw_cols = gdn_conv_w.shape[2]
    g_cw = lax.dynamic_slice_in_dim(small["gdn_conv_w"].reshape(CONV_K, 3 * GDN_WIDTH), chip * cw_cols, cw_cols, axis=1)

    in_feats = w_in.shape[2]
    out_rows = w_out.shape[1]
    p_in = gr["w_in"]
    p_out = gr["w_out"].reshape(4, out_rows, D_MODEL)
    h_in, h_out = D_MODEL // 2, out_rows // 2
    mine_in = lax.dynamic_slice_in_dim(p_in, ic * h_in, h_in, axis=1)
    mine_out = lax.dynamic_slice_in_dim(p_out, ic * h_out, h_out, axis=1)
    theirs_in = lax.dynamic_slice_in_dim(p_in, (1 - ic) * h_in, h_in, axis=1)
    theirs_out = lax.dynamic_slice_in_dim(p_out, (1 - ic) * h_out, h_out, axis=1)
    got_in, got_out = _sibling_swap([theirs_in, theirs_out], "swap_halves")
    chip_in, chip_in16 = _chip_sum_blocks(mine_in, got_in, in_feats, 4, "chip_sum_in")
    chip_out, chip_out16 = _add_n([mine_out.reshape(4 * h_out, D_MODEL), got_out.reshape(4 * h_out, D_MODEL)],
                                  "chip_sum_out", (F32, BF16))
    chip_out = chip_out.reshape(4, h_out, D_MODEL)
    rs_in, rs_out = _chip_scatter([chip_in16, chip_out16.reshape(4, h_out, D_MODEL)], "scatter_grads")
    own_in = lax.dynamic_index_in_dim(chip_in, chip, axis=0, keepdims=False)
    own_out = lax.dynamic_index_in_dim(chip_out, chip, axis=0, keepdims=False)
    (half_in,) = _add_n([own_in, rs_in[0], rs_in[1], rs_in[2]], "reduce_in")
    (half_out,) = _add_n([own_out, rs_out[0], rs_out[1], rs_out[2]], "reduce_out")
    sib_in, sib_out = _sibling_swap([half_in, half_out], "swap_result")
    g_w_in_t = jnp.where(ic == 0, jnp.concatenate([half_in, sib_in], axis=1),
                         jnp.concatenate([sib_in, half_in], axis=1))[0:in_feats]
    g_w_out = jnp.where(ic == 0, jnp.concatenate([half_out, sib_out], axis=0),
                        jnp.concatenate([sib_out, half_out], axis=0))

    grads = dict(
        w_ada=g_w_ada[None], b_ada=g_b_ada, w_in=g_w_in_t, gla_w_gate_up=g_wg[None],
        gla_b_gate=small["gla_b_gate"].reshape(1, -1), gla_norm_w=small["gla_norm_w"].reshape(1, -1),
        gdn_conv_w=g_cw[None], gdn_a_log=small["gdn_a_log"].reshape(1, -1),
        gdn_dt_bias=small["gdn_dt_bias"].reshape(1, -1), gdn_norm_w=small["gdn_norm_w"].reshape(1, -1),
        w_out=g_w_out[None], ln_w=small["ln_w"].reshape(1, -1), ln_b=small["ln_b"].reshape(1, -1))
    weights = dict(w_ada=w_ada, b_ada=b_ada, w_in=w_in, gla_w_gate_up=gla_w_gate_up, gla_b_gate=gla_b_gate,
                   gla_norm_w=gla_norm_w, gdn_conv_w=gdn_conv_w, gdn_a_log=gdn_a_log, gdn_dt_bias=gdn_dt_bias,
                   gdn_norm_w=gdn_norm_w, w_out=w_out, ln_w=ln_w, ln_b=ln_b)
    m_in = dict(w_ada=m_w_ada, b_ada=m_b_ada, w_in=m_w_in, gla_w_gate_up=m_gla_w_gate_up, gla_b_gate=m_gla_b_gate,
                gla_norm_w=m_gla_norm_w, gdn_conv_w=m_gdn_conv_w, gdn_a_log=m_gdn_a_log, gdn_dt_bias=m_gdn_dt_bias,
                gdn_norm_w=m_gdn_norm_w, w_out=m_w_out, ln_w=m_ln_w, ln_b=m_ln_b)
    v_in = dict(w_ada=v_w_ada, b_ada=v_b_ada, w_in=v_w_in, gla_w_gate_up=v_gla_w_gate_up, gla_b_gate=v_gla_b_gate,
                gla_norm_w=v_gla_norm_w, gdn_conv_w=v_gdn_conv_w, gdn_a_log=v_gdn_a_log, gdn_dt_bias=v_gdn_dt_bias,
                gdn_norm_w=v_gdn_norm_w, w_out=v_w_out, ln_w=v_ln_w, ln_b=v_ln_b)
    names = list(weights)
    delta, new_m, new_v = {}, {}, {}
    for nm in names:
        shp = weights[nm].shape
        if nm == "w_in":
            to2d = lambda t: jnp.transpose(t[0])
            from2d = lambda t: jnp.transpose(t)[None]
            g2d = grads[nm]
        else:
            to2d = lambda t: t.reshape(-1, shp[-1])
            from2d = lambda t: t.reshape(shp)
            g2d = to2d(grads[nm])
        d, a, b = _adamw(to2d(weights[nm]), g2d, to2d(m_in[nm]), to2d(v_in[nm]), "adamw_" + nm)
        delta[nm], new_m[nm], new_v[nm] = from2d(d), from2d(a), from2d(b)
        grads[nm] = from2d(g2d)
    return (loss, grad_x, *[grads[k] for k in names], *[delta[k] for k in names],
            *[new_m[k] for k in names], *[new_v[k] for k in names])
```

```python
import functools

import jax
import jax.numpy as jnp
from jax import lax
from jax.experimental import pallas as pl
from jax.experimental.pallas import tpu as pltpu

F32 = jnp.float32
BF16 = jnp.bfloat16
HI = lax.Precision.HIGH
INV_PREC = None
MESH = pl.DeviceIdType.MESH

D_MODEL = 1024
GLA_HEADS = 4
GLA_DK = 64
GLA_DV = 128
GLA_QK = 256
GLA_WIDTH = 512
GLA_RANK = 16
GLA_GATE_NORM = 16.0
GDN_HEADS = 4
GDN_DK = 128
GDN_WIDTH = 512
CONV_K = 4
CHUNK = 64
LN_EPS = 1e-5
RMS_EPS = 1e-6
ALPHA = 2.0 ** 0.25
IN_COLS = 3608

LANE_A = GLA_RANK
LANE_B = GLA_RANK + GDN_HEADS
SMALL_USED = GLA_RANK + 2 * GDN_HEADS

ADAM_LR = 0.001
ADAM_B1 = 0.9
ADAM_B2 = 0.999
ADAM_EPS = 1e-08
ADAM_WD = 0.01
ADAM_STEP = 10

VMEM_LIMIT = 56 * 1024 * 1024


def _iota(shape, dim):
    return lax.broadcasted_iota(jnp.int32, shape, dim)


def _dot(a, b, prec=None):
    return lax.dot_general(a, b, (((1,), (0,)), ((), ())), precision=prec, preferred_element_type=F32)


def _dot_nt(a, b, prec=None):
    return lax.dot_general(a, b, (((1,), (1,)), ((), ())), precision=prec, preferred_element_type=F32)


def _dot_tn(a, b, prec=None):
    return lax.dot_general(a, b, (((0,), (0,)), ((), ())), precision=prec, preferred_element_type=F32)


def _log_sigmoid(z):
    return jnp.minimum(z, 0.0) - jnp.log1p(jnp.exp(-jnp.abs(z)))


def _softplus(z):
    return jnp.maximum(z, 0.0) + jnp.log1p(jnp.exp(-jnp.abs(z)))


def _silu(z):
    return z * jax.nn.sigmoid(z)


def _rms_gate(o, nw, og):
    return o * lax.rsqrt(jnp.mean(o * o, axis=-1, keepdims=True) + RMS_EPS) * nw * _silu(og)


def _params(*sem):
    return pltpu.CompilerParams(dimension_semantics=sem, vmem_limit_bytes=VMEM_LIMIT)


GLA_PAIRS = GLA_HEADS // 2


def _gla_chunk(qs, ks, lrs, vs, ogs, ss, wgs, bgs, nw):
    c = qs[0].shape[0]
    n_ep = len(ss)
    n_ex = n_ep // GLA_PAIRS
    n_chunks = len(qs) // n_ep
    pair_units = [(i // n_ep * n_ex + i % n_ep // GLA_PAIRS, i % GLA_PAIRS) for i in range(len(qs))]
    head_units = [(i // GLA_HEADS * GLA_PAIRS + i % GLA_HEADS // 2, i % 2) for i in range(len(vs))]
    row, col = _iota((c, c), 0), _iota((c, c), 1)
    causal = row >= col
    first_half = (_iota((c, 1), 0) < c // 2).astype(F32)
    lane = _iota((1, 128), 1)
    masks = [(lane < GLA_DK).astype(F32), (lane >= GLA_DK).astype(F32)]
    gs = [_log_sigmoid(_dot(lrs[ce], wgs[p]) + bgs[p]) * (1.0 / GLA_GATE_NORM) for ce, p in pair_units]
    bs = [_dot(causal.astype(F32), g, HI) for g in gs]
    b_ref = [jnp.sum(g * first_half, axis=0, keepdims=True) for g in gs]
    b_last = [jnp.sum(g, axis=0, keepdims=True) for g in gs]
    qsc = [q * (GLA_DK ** -0.5) for q in qs]
    qe = [q * jnp.exp(b - br) for q, b, br in zip(qsc, bs, b_ref)]
    ke = [k * jnp.exp(br - b) for k, b, br in zip(ks, bs, b_ref)]
    qb = [q * jnp.exp(b) for q, b in zip(qsc, bs)]
    kd = [k * jnp.exp(bl_ - b) for k, b, bl_ in zip(ks, bs, b_last)]
    decay = [jnp.exp(bl_) for bl_ in b_last]
    att = [jnp.where(causal, _dot_nt(qe[u] * masks[half], ke[u]), 0.0) for u, half in head_units]
    o_intra = [_dot(a, v) for a, v in zip(att, vs)]
    qbm = [qb[u] * masks[half] for u, half in head_units]
    kdm = [kd[u] * masks[half] for u, half in head_units]
    ys = []
    for r in range(n_chunks):
        heads_r = range(r * n_ex * GLA_HEADS, (r + 1) * n_ex * GLA_HEADS)
        o_inter = [_dot_nt(qbm[i], ss[head_units[i][0] - r * n_ep]) for i in heads_r]
        upd = [_dot_tn(vs[i], kdm[i]) for i in heads_r]
        ss = [s * decay[r * n_ep + j] + upd[2 * j] + upd[2 * j + 1] for j, s in enumerate(ss)]
        ys += [_rms_gate(o_intra[i] + oi, nw, ogs[i]) for i, oi in zip(heads_r, o_inter)]
    return ys, ss


def _unit_lower_inverse_chain(a_list):
    c = a_list[0].shape[0]
    eye = (_iota((c, c), 0) == _iota((c, c), 1)).astype(F32)
    ps = [-a for a in a_list]
    ts = [eye + p for p in ps]
    levels = max(c.bit_length() - 2, 0)
    if levels:
        ps = [_dot(p, p, INV_PREC) for p in ps]
    for level in range(levels):
        last = level == levels - 1
        both = [_dot(t if last else jnp.concatenate([t, p], axis=0), p, INV_PREC) for t, p in zip(ts, ps)]
        ts = [t + m[0:c] for t, m in zip(ts, both)]
        if not last:
            ps = [m[c:2 * c] for m in both]
    return ts


@jax.custom_vjp
def _unit_lower_inverse(a_list):
    return _unit_lower_inverse_chain(a_list)


def _unit_lower_inverse_fwd(a_list):
    ts = _unit_lower_inverse_chain(a_list)
    return ts, ts


def _unit_lower_inverse_bwd(ts, dts):
    xs = [_dot_nt(dt, t, INV_PREC) for dt, t in zip(dts, ts)]
    return ([-_dot_tn(t, x, INV_PREC) for t, x in zip(ts, xs)],)


_unit_lower_inverse.defvjp(_unit_lower_inverse_fwd, _unit_lower_inverse_bwd)


@jax.custom_vjp
def _unit_lower_inverse_known(a_list, ts):
    return ts


def _unit_lower_inverse_known_fwd(a_list, ts):
    return ts, ts


def _unit_lower_inverse_known_bwd(ts, dts):
    return _unit_lower_inverse_bwd(ts, dts) + ([jnp.zeros_like(t) for t in ts],)


_unit_lower_inverse_known.defvjp(_unit_lower_inverse_known_fwd, _unit_lower_inverse_known_bwd)


def _gdn_prep_units(qs, ks, vs, gbs, t_known=None):
    c = qs[0].shape[0]
    units = [divmod(i, GDN_HEADS) for i in range(len(qs))]
    row, col = _iota((c, c), 0), _iota((c, c), 1)
    causal, strict = row >= col, row > col
    if t_known is None:
        lane = _iota((1, 128), 1)
        d_alls = [_dot(causal.astype(F32), gb, HI) for gb in gbs]
        g_c, beta_c, d_c = [], [], []
        for r, h in units:
            sel_a = (lane == LANE_A + h).astype(F32)
            g_c.append(jnp.sum(gbs[r] * sel_a, axis=-1, keepdims=True))
            beta_c.append(jnp.sum(gbs[r] * (lane == LANE_B + h).astype(F32), axis=-1, keepdims=True))
            d_c.append(jnp.sum(d_alls[r] * sel_a, axis=-1, keepdims=True))
        d_diff = [jnp.broadcast_to(d, (c, c)) - jnp.broadcast_to(d, (c, c)).T for d in d_c]
    else:
        src = _iota((128, 128), 0)
        spread = jnp.concatenate([(src == base + h).astype(F32) for base in (LANE_A, LANE_B)
                                  for h in range(GDN_HEADS)], axis=1)
        width = GDN_HEADS * 128
        g_beta = [_dot(gb, spread, HI) for gb in gbs]
        d_alls = [_dot(causal.astype(F32), gbv[:, 0:width], HI) for gbv in g_beta]
        g_c = [g_beta[r][:, h * 128:(h + 1) * 128] for r, h in units]
        beta_c = [g_beta[r][:, width + h * 128:width + (h + 1) * 128] for r, h in units]
        d_c = [d_alls[r][:, h * 128:(h + 1) * 128] for r, h in units]
        d_diff = [d[:, 0:c] - d.T[0:c, :] for d in d_c]
    d_last = [jnp.sum(g, axis=0, keepdims=True) for g in g_c]
    decay_mat = [jnp.where(causal, jnp.exp(jnp.where(causal, dd, 0.0)), 0.0) for dd in d_diff]
    kb = [k * b for k, b in zip(ks, beta_c)]
    kbk_qk = [_dot_nt(jnp.concatenate([kbi, q], axis=0), k) for kbi, q, k in zip(kb, qs, ks)]
    a = [jnp.where(strict, m[0:c] * dm, 0.0) for m, dm in zip(kbk_qk, decay_mat)]
    qk = [jnp.where(causal, m[c:2 * c] * dm, 0.0) for m, dm in zip(kbk_qk, decay_mat)]
    t = _unit_lower_inverse(a) if t_known is None else _unit_lower_inverse_known(a, t_known)
    uw = [_dot(ti, jnp.concatenate([v * b, kbi * jnp.exp(d)], axis=1))
          for ti, v, b, kbi, d in zip(t, vs, beta_c, kb, d_c)]
    u = [m[:, 0:128] for m in uw]
    w = [m[:, 128:256] for m in uw]
    q_dec = [q * jnp.exp(d) for q, d in zip(qs, d_c)]
    k_dec = [k * jnp.exp(dl - d) for k, dl, d in zip(ks, d_last, d_c)]
    gamma = [jnp.exp(dl) for dl in d_last]
    return u, w, qk, q_dec, k_dec, gamma, t


def _sum_rows(t):
    return jnp.sum(t, axis=0, keepdims=True)


def _gdn_pre_elem(ps, ab, alog_v, dtb_v):
    outs = []
    for j, p in enumerate(ps):
        s = _silu(p)
        if j < 2 * GDN_HEADS:
            s = s * lax.rsqrt(jnp.sum(s * s, axis=-1, keepdims=True) + RMS_EPS)
        if j < GDN_HEADS:
            s = s * (GDN_DK ** -0.5)
        outs.append(s)
    lane = _iota((1, 128), 1)
    is_a = (lane >= LANE_A) & (lane < LANE_A + GDN_HEADS)
    is_b = (lane >= LANE_B) & (lane < LANE_B + GDN_HEADS)
    g = -jnp.exp(alog_v) * _softplus(ab + dtb_v)
    gb = jnp.where(is_a, g, jnp.where(is_b, jax.nn.sigmoid(ab), 0.0))
    return tuple(outs) + (gb,)


def _proj_fwd(x2, sc3, sh3, ws, seq, tm=512):
    n = x2.shape[0]
    tpe = seq // tm
    nw = len(ws)

    def body(x_ref, sc_ref, sh_ref, *refs):
        h = (x_ref[...] * sc_ref[0] + sh_ref[0]).astype(ws[0].dtype)
        for w_ref, o_ref in zip(refs[:nw], refs[nw:]):
            o_ref[...] = _dot_nt(h, w_ref[...])

    row = lambda i: (i, 0)
    per_ex = pl.BlockSpec((1, 1, D_MODEL), lambda i: (i // tpe, 0, 0))
    return pl.pallas_call(
        body, name="proj_fwd", grid=(n // tm,),
        in_specs=[pl.BlockSpec((tm, D_MODEL), row), per_ex, per_ex]
        + [pl.BlockSpec(w.shape, lambda i: (0, 0)) for w in ws],
        out_specs=[pl.BlockSpec((tm, w.shape[0]), row) for w in ws],
        out_shape=[jax.ShapeDtypeStruct((n, w.shape[0]), F32) for w in ws],
        compiler_params=_params("parallel"),
    )(x2, sc3, sh3, *ws)


GLA_SCAN_CHUNKS = 4


def _gla_operands(q_ref, k_ref, v_ref, og_ref, lr_ref, wg_ref, bg_ref, bl, r_per):
    chunks = [slice(r * CHUNK, (r + 1) * CHUNK) for r in range(r_per)]
    pair_cols = [slice(p * 128, (p + 1) * 128) for p in range(GLA_PAIRS)]
    head_cols = [slice(h * 128, (h + 1) * 128) for h in range(GLA_HEADS)]
    per_pair = lambda ref: [ref[e, rows, cols] for rows in chunks for e in range(bl) for cols in pair_cols]
    per_head = lambda ref: [ref[e, rows, cols] for rows in chunks for e in range(bl) for cols in head_cols]
    return (per_pair(q_ref), per_pair(k_ref), [lr_ref[e, rows, :] for rows in chunks for e in range(bl)],
            per_head(v_ref), per_head(og_ref)), ([wg_ref[:, cols] for cols in pair_cols],
                                                 [bg_ref[:, cols] for cols in pair_cols])


def _gla_fwd(pa, pd, wg, bg, nw, bl, seq):
    n = pa.shape[0]
    nc = seq // CHUNK
    r_per = GLA_SCAN_CHUNKS
    pairs = [(e, p) for e in range(bl) for p in range(GLA_PAIRS)]
    head_slots = [(slice(r * CHUNK, (r + 1) * CHUNK), e, slice(h * 128, (h + 1) * 128))
                  for r in range(r_per) for e in range(bl) for h in range(GLA_HEADS)]

    def body(q_ref, k_ref, v_ref, og_ref, lr_ref, wg_ref, bg_ref, nw_ref, y_ref, st_ref, s_scr):
        @pl.when(pl.program_id(0) == 0)
        def _():
            s_scr[...] = jnp.zeros_like(s_scr)

        ss = [s_scr[e, p] for e, p in pairs]
        for (e, p), s in zip(pairs, ss):
            st_ref[e, 0, p] = s
        acts, gate = _gla_operands(q_ref, k_ref, v_ref, og_ref, lr_ref, wg_ref, bg_ref, bl, r_per)
        ys, s_new = _gla_chunk(*acts, ss, *gate, nw_ref[...])
        for (rows, e, cols), y in zip(head_slots, ys):
            y_ref[e, rows, cols] = y.astype(y_ref.dtype)
        for (e, p), s in zip(pairs, s_new):
            s_scr[e, p] = s

    tok = lambda w, j: pl.BlockSpec((bl, r_per * CHUNK, w), lambda i: (0, i, j))
    const = lambda i: (0, 0)
    pa3 = pa.reshape(bl, seq, 1536)
    y, st = pl.pallas_call(
        body, name="gla_fwd", grid=(nc // r_per,),
        in_specs=[tok(256, 0), tok(256, 1), tok(512, 1), tok(512, 2), tok(128, 0),
                  pl.BlockSpec(wg.shape, const), pl.BlockSpec(bg.shape, const), pl.BlockSpec(nw.shape, const)],
        out_specs=[tok(512, 0), pl.BlockSpec((bl, 1, GLA_PAIRS, 128, 128), lambda i: (0, i, 0, 0, 0))],
        out_shape=[jax.ShapeDtypeStruct((bl, seq, 512), MM_DTYPE),
                   jax.ShapeDtypeStruct((bl, nc // r_per, GLA_PAIRS, 128, 128), F32)],
        scratch_shapes=[pltpu.VMEM((bl, GLA_PAIRS, 128, 128), F32)],
        compiler_params=_params("arbitrary"),
    )(pa3, pa3, pa3, pa3, pd.reshape(bl, seq, 128), wg, bg, nw)
    return y.reshape(n, 512), st


def _gla_bwd(pa, pd, st, dya, wg, bg, nw, bl, seq):
    n = pa.shape[0]
    nc = seq // CHUNK
    r_per = GLA_SCAN_CHUNKS
    steps = nc // r_per
    pairs = [(e, p) for e in range(bl) for p in range(GLA_PAIRS)]
    pair_cols = [slice(p * 128, (p + 1) * 128) for p in range(GLA_PAIRS)]
    chunks = [slice(r * CHUNK, (r + 1) * CHUNK) for r in range(r_per)]
    pair_slots = [(rows, e, p) for rows in chunks for e in range(bl) for p in range(GLA_PAIRS)]
    head_slots = [(rows, e, h) for rows in chunks for e in range(bl) for h in range(GLA_HEADS)]

    def body(q_ref, k_ref, v_ref, og_ref, lr_ref, st_ref, dy_ref, wg_ref, bg_ref, nw_ref,
             da_ref, dd_ref, dwg_ref, dbg_ref, dnw_ref, ds_scr):
        @pl.when(pl.program_id(0) == 0)
        def _():
            dwg_ref[...] = jnp.zeros_like(dwg_ref)
            dbg_ref[...] = jnp.zeros_like(dbg_ref)
            dnw_ref[...] = jnp.zeros_like(dnw_ref)
            ds_scr[...] = jnp.zeros_like(ds_scr)

        acts, gate = _gla_operands(q_ref, k_ref, v_ref, og_ref, lr_ref, wg_ref, bg_ref, bl, r_per)
        _, vjp = jax.vjp(_gla_chunk, *acts, [st_ref[e, 0, p] for e, p in pairs], *gate, nw_ref[...])
        dq, dk, dlr, dv, dog, ds, dwg, dbg, dnw = vjp(
            ([dy_ref[e, rows, h * 128:(h + 1) * 128] for rows, e, h in head_slots], [ds_scr[e, p] for e, p in pairs]))
        for i, (rows, e) in enumerate((rows, e) for rows in chunks for e in range(bl)):
            dd_ref[e, rows, :] = dlr[i]
        for i, (rows, e, p) in enumerate(pair_slots):
            da_ref[e, rows, pair_cols[p]] = dq[i].astype(da_ref.dtype)
            da_ref[e, rows, GLA_QK + p * 128:GLA_QK + (p + 1) * 128] = dk[i].astype(da_ref.dtype)
        for i, (rows, e, h) in enumerate(head_slots):
            da_ref[e, rows, 512 + h * 128:512 + (h + 1) * 128] = dv[i].astype(da_ref.dtype)
            da_ref[e, rows, 1024 + h * 128:1024 + (h + 1) * 128] = dog[i].astype(da_ref.dtype)
        for (e, p), d in zip(pairs, ds):
            ds_scr[e, p] = d
        for p, cols in enumerate(pair_cols):
            dwg_ref[:, cols] += dwg[p]
            dbg_ref[:, cols] += dbg[p]
        dnw_ref[...] += dnw

    tok = lambda w, j: pl.BlockSpec((bl, r_per * CHUNK, w), lambda i: (0, steps - 1 - i, j))
    const = lambda i: (0, 0)
    pa3 = pa.reshape(bl, seq, 1536)
    da, dd, dwg, dbg, dnw = pl.pallas_call(
        body, name="gla_bwd", grid=(steps,),
        in_specs=[tok(256, 0), tok(256, 1), tok(512, 1), tok(512, 2), tok(128, 0),
                  pl.BlockSpec((bl, 1, GLA_PAIRS, 128, 128), lambda i: (0, steps - 1 - i, 0, 0, 0)), tok(512, 0),
                  pl.BlockSpec(wg.shape, const), pl.BlockSpec(bg.shape, const), pl.BlockSpec(nw.shape, const)],
        out_specs=[tok(1536, 0), tok(128, 0),
                   pl.BlockSpec(wg.shape, const), pl.BlockSpec(bg.shape, const), pl.BlockSpec(nw.shape, const)],
        out_shape=[jax.ShapeDtypeStruct((bl, seq, 1536), MM_DTYPE), jax.ShapeDtypeStruct((bl, seq, 128), F32),
                   jax.ShapeDtypeStruct(wg.shape, F32), jax.ShapeDtypeStruct(bg.shape, F32),
                   jax.ShapeDtypeStruct(nw.shape, F32)],
        scratch_shapes=[pltpu.VMEM((bl, GLA_PAIRS, 128, 128), F32)],
        compiler_params=_params("arbitrary"),
    )(pa3, pa3, pa3, pa3, pd.reshape(bl, seq, 128), st, dya.reshape(bl, seq, 512), wg, bg, nw)
    return da.reshape(n, 1536), dd.reshape(n, 128), dwg, dbg, dnw


PRE_ROWS = 64
PRE_PIECES = [slice(j * 128, (j + 1) * 128) for j in range(3 * GDN_HEADS)]


def _rows_from(ref, start, rows, cols):
    lo = start // 8 * 8
    if lo == start:
        return ref[start:start + rows, cols]
    window = ref[lo:lo + rows + 8, cols]
    return pltpu.roll(window, rows + 8 - (start - lo), 0)[0:rows]


def _conv_taps(buf_ref, w_ref, base, rows, cols):
    acc = w_ref[0:1, cols] * _rows_from(buf_ref, base, rows, cols)
    for k in range(1, CONV_K):
        acc = acc + w_ref[k:k + 1, cols] * _rows_from(buf_ref, base + k, rows, cols)
    return acc


def _gdn_pre_fwd(pb, pd, cw8, alog_v, dtb_v, bl, seq, tm=256):
    n = pb.shape[0]
    tpe = seq // tm
    t8 = tm // 8

    def body(u_ref, prev_ref, ab_ref, w_ref, al_ref, dt_ref, qkv_ref, gb_ref, p_ref, buf):
        i = pl.program_id(0)
        keep = (i % tpe != 0).astype(F32)
        buf[0:8, :] = prev_ref[...] * keep
        buf[8:8 + tm, :] = u_ref[...]
        for r0 in range(0, tm, PRE_ROWS):
            rows = slice(r0, r0 + PRE_ROWS)
            ps = [_conv_taps(buf, w_ref, 8 - (CONV_K - 1) + r0, PRE_ROWS, cols) for cols in PRE_PIECES]
            outs = _gdn_pre_elem(ps, ab_ref[rows, :], al_ref[...], dt_ref[...])
            for cols, p, out in zip(PRE_PIECES, ps, outs):
                p_ref[rows, cols] = p
                qkv_ref[rows, cols] = out
            gb_ref[rows, :] = outs[len(PRE_PIECES)]

    row = lambda i: (i, 0)
    const = lambda i: (0, 0)
    return pl.pallas_call(
        body, name="gdn_pre_fwd", grid=(n // tm,),
        in_specs=[pl.BlockSpec((tm, 1536), row),
                  pl.BlockSpec((8, 1536), lambda i: (jnp.maximum(i * t8 - 1, 0), 0)),
                  pl.BlockSpec((tm, 128), row),
                  pl.BlockSpec((8, 1536), const), pl.BlockSpec((1, 128), const), pl.BlockSpec((1, 128), const)],
        out_specs=[pl.BlockSpec((tm, 1536), row), pl.BlockSpec((tm, 128), row), pl.BlockSpec((tm, 1536), row)],
        out_shape=[jax.ShapeDtypeStruct((n, 1536), F32), jax.ShapeDtypeStruct((n, 128), F32),
                   jax.ShapeDtypeStruct((n, 1536), F32)],
        scratch_shapes=[pltpu.VMEM((tm + 8, 1536), F32)],
        compiler_params=_params("parallel"),
    )(pb, pb, pd, cw8, alog_v, dtb_v)


def _gdn_pre_bwd(pb, conv_out, pd, dqkv, dgb, cw8, alog_v, dtb_v, bl, seq, tm=256):
    n = pb.shape[0]
    tpe = seq // tm
    t8 = tm // 8
    nb8 = n // 8
    ext = tm + 8

    def body(u_ref, p_ref, pn_ref, ab_ref, abn_ref, dq_ref, dqn_ref, dgb_ref, w_ref, al_ref, dt_ref,
             du_ref, dab_ref, dw_ref, dal_ref, ddt_ref, dpbuf):
        i = pl.program_id(0)

        @pl.when(i == 0)
        def _():
            dw_ref[...] = jnp.zeros_like(dw_ref)
            dal_ref[...] = jnp.zeros_like(dal_ref)
            ddt_ref[...] = jnp.zeros_like(ddt_ref)

        keep_next = (i % tpe != tpe - 1).astype(F32)
        zeros8 = jnp.zeros((8, 128), F32)
        dal, ddt = jnp.zeros((1, 128), F32), jnp.zeros((1, 128), F32)
        for r0 in range(0, tm, PRE_ROWS):
            rows = slice(r0, r0 + PRE_ROWS)
            last = r0 + PRE_ROWS == tm
            along = lambda own, extra: jnp.concatenate([own, extra], axis=0) if last else own
            ps = [along(p_ref[rows, cols], pn_ref[:, cols]) for cols in PRE_PIECES]
            ab = along(ab_ref[rows, :], abn_ref[...])
            _, vjp = jax.vjp(_gdn_pre_elem, ps, ab, al_ref[...], dt_ref[...])
            cts = tuple(along(dq_ref[rows, cols], dqn_ref[:, cols] * keep_next) for cols in PRE_PIECES)
            cts += (along(dgb_ref[rows, :], zeros8),)
            dps, dab, dal_r, ddt_r = vjp(cts)
            out_rows = slice(r0, r0 + PRE_ROWS + (8 if last else 0))
            for cols, dp in zip(PRE_PIECES, dps):
                dpbuf[out_rows, cols] = dp
            dab_ref[rows, :] = dab[0:PRE_ROWS, :]
            dal, ddt = dal + dal_r, ddt + ddt_r
        dal_ref[...] += dal
        ddt_ref[...] += ddt
        for cols in PRE_PIECES:
            dw = [jnp.zeros((1, 128), F32) for _ in range(CONV_K)]
            for r0 in range(0, tm, PRE_ROWS):
                u = u_ref[r0:r0 + PRE_ROWS, cols]
                du = None
                for k in range(CONV_K):
                    dp_k = _rows_from(dpbuf, r0 + CONV_K - 1 - k, PRE_ROWS, cols)
                    term = w_ref[k:k + 1, cols] * dp_k
                    du = term if du is None else du + term
                    dw[k] = dw[k] + jnp.sum(u * dp_k, axis=0, keepdims=True)
                du_ref[r0:r0 + PRE_ROWS, cols] = du.astype(du_ref.dtype)
            for k in range(CONV_K):
                dw_ref[k:k + 1, cols] += dw[k]

    row = lambda i: (i, 0)
    next8 = lambda i: (jnp.minimum((i + 1) * t8, nb8 - 1), 0)
    const = lambda i: (0, 0)
    return pl.pallas_call(
        body, name="gdn_pre_bwd", grid=(n // tm,),
        in_specs=[pl.BlockSpec((tm, 1536), row), pl.BlockSpec((tm, 1536), row), pl.BlockSpec((8, 1536), next8),
                  pl.BlockSpec((tm, 128), row), pl.BlockSpec((8, 128), next8),
                  pl.BlockSpec((tm, 1536), row), pl.BlockSpec((8, 1536), next8),
                  pl.BlockSpec((tm, 128), row),
                  pl.BlockSpec((8, 1536), const), pl.BlockSpec((1, 128), const), pl.BlockSpec((1, 128), const)],
        out_specs=[pl.BlockSpec((tm, 1536), row), pl.BlockSpec((tm, 128), row),
                   pl.BlockSpec((8, 1536), const), pl.BlockSpec((1, 128), const), pl.BlockSpec((1, 128), const)],
        out_shape=[jax.ShapeDtypeStruct((n, 1536), MM_DTYPE), jax.ShapeDtypeStruct((n, 128), F32),
                   jax.ShapeDtypeStruct((8, 1536), F32), jax.ShapeDtypeStruct((1, 128), F32),
                   jax.ShapeDtypeStruct((1, 128), F32)],
        scratch_shapes=[pltpu.VMEM((ext, 1536), F32)],
        compiler_params=_params("arbitrary"),
    )(pb, conv_out, conv_out, pd, pd, dqkv, dqkv, dgb, cw8, alog_v, dtb_v)


GDN_PREP_CHUNKS = 4
GDN_PREP_BWD_CHUNKS = 4
GDN_SCAN_CHUNKS = 2
MM_DTYPE = BF16


def _head_cols(ref, rows, base=0):
    return [ref[rows, base + h * 128:base + (h + 1) * 128] for h in range(GDN_HEADS)]


def _gdn_prep(qkv, gb):
    n = qkv.shape[0]
    r_per = GDN_PREP_CHUNKS
    tm = r_per * CHUNK

    def body(q_ref, k_ref, v_ref, gb_ref, u_ref, w_ref, qd_ref, kd_ref, qk_ref, t_ref, gam_ref):
        rowid = _iota((8, 128), 0)
        chunk_rows = [slice(r * CHUNK, (r + 1) * CHUNK) for r in range(r_per)]
        gather = lambda ref: [t for rows in chunk_rows for t in _head_cols(ref, rows)]
        u, w, qk, qd, kd, gamma, tinv = _gdn_prep_units(gather(q_ref), gather(k_ref), gather(v_ref),
                                                        [gb_ref[rows, :] for rows in chunk_rows])
        for r, rows in enumerate(chunk_rows):
            gam = jnp.zeros((8, 128), F32)
            for h in range(GDN_HEADS):
                i = r * GDN_HEADS + h
                cols = slice(h * 128, (h + 1) * 128)
                u_ref[rows, cols] = u[i]
                w_ref[rows, cols] = w[i].astype(MM_DTYPE)
                qd_ref[rows, cols] = qd[i].astype(MM_DTYPE)
                kd_ref[rows, cols] = kd[i].astype(MM_DTYPE)
                qk_ref[r, h] = qk[i].astype(MM_DTYPE)
                t_ref[r, h] = tinv[i].astype(MM_DTYPE)
                gam = jnp.where(rowid == h, gamma[i], gam)
            gam_ref[r] = gam

    tok = lambda j: pl.BlockSpec((tm, 512), lambda i: (i, j))
    return pl.pallas_call(
        body, name="gdn_prep", grid=(n // tm,),
        in_specs=[tok(0), tok(1), tok(2), pl.BlockSpec((tm, 128), lambda i: (i, 0))],
        out_specs=[tok(0)] * 4 + [pl.BlockSpec((r_per, GDN_HEADS, CHUNK, CHUNK), lambda i: (i, 0, 0, 0))] * 2
        + [pl.BlockSpec((r_per, 8, 128), lambda i: (i, 0, 0))],
        out_shape=[jax.ShapeDtypeStruct((n, 512), F32)] + [jax.ShapeDtypeStruct((n, 512), MM_DTYPE)] * 3
        + [jax.ShapeDtypeStruct((n // CHUNK, GDN_HEADS, CHUNK, CHUNK), MM_DTYPE)] * 2
        + [jax.ShapeDtypeStruct((n // CHUNK, 8, 128), F32)],
        compiler_params=_params("parallel"),
    )(qkv, qkv, qkv, gb)


def _gdn_fwd(qkv, gb, pc, nw, bl, seq):
    n = qkv.shape[0]
    nc = seq // CHUNK
    u, w, qd, kd, qk, tinv, gam = _gdn_prep(qkv, gb)
    tok3 = lambda t: t.reshape(bl, seq, 512)
    qk5 = qk.reshape(bl, nc, GDN_HEADS, CHUNK, CHUNK)
    gam4 = gam.reshape(bl, nc, 8, 128)

    r_per = GDN_SCAN_CHUNKS
    mm = lambda t: t.astype(MM_DTYPE)

    def body(u_ref, w_ref, qd_ref, kd_ref, qk_ref, gam_ref, og_ref, nw_ref, o_ref, y_ref, vn_ref, st_ref, s_scr):
        @pl.when(pl.program_id(0) == 0)
        def _():
            s_scr[...] = jnp.zeros_like(s_scr)

        units = [(b, h, slice(h * 128, (h + 1) * 128)) for b in range(bl) for h in range(GDN_HEADS)]
        ss = [s_scr[b, h] for b, h, _ in units]
        for r in range(r_per):
            rows = slice(r * CHUNK, (r + 1) * CHUNK)
            for (b, h, _), s in zip(units, ss):
                st_ref[b, r, h] = s
            ws_qs = [_dot(jnp.concatenate([w_ref[b, rows, cols], qd_ref[b, rows, cols]], axis=0), mm(s))
                     for (b, h, cols), s in zip(units, ss)]
            v_new = [u_ref[b, rows, cols] - m[0:CHUNK] for (b, h, cols), m in zip(units, ws_qs)]
            os_ = [m[CHUNK:2 * CHUNK] + _dot(qk_ref[b, r, h], mm(vn))
                   for (b, h, cols), m, vn in zip(units, ws_qs, v_new)]
            ss = [s * gam_ref[b, r, h:h + 1, :] + _dot_tn(kd_ref[b, rows, cols], mm(vn))
                  for (b, h, cols), s, vn in zip(units, ss, v_new)]
            for (b, h, cols), vn, o in zip(units, v_new, os_):
                vn_ref[b, rows, cols] = mm(vn)
                o_ref[b, rows, cols] = o
                y_ref[b, rows, cols] = mm(_rms_gate(o, nw_ref[...], og_ref[b, rows, cols]))
        for (b, h, _), s in zip(units, ss):
            s_scr[b, h] = s

    tok = pl.BlockSpec((bl, r_per * CHUNK, 512), lambda i: (0, i, 0))
    st_spec = pl.BlockSpec((bl, r_per, GDN_HEADS, 128, 128), lambda i: (0, i, 0, 0, 0))
    tok_shape = jax.ShapeDtypeStruct((bl, seq, 512), F32)
    o, y, vn, st = pl.pallas_call(
        body, name="gdn_scan_fwd", grid=(nc // r_per,),
        in_specs=[tok, tok, tok, tok,
                  pl.BlockSpec((bl, r_per, GDN_HEADS, CHUNK, CHUNK), lambda i: (0, i, 0, 0, 0)),
                  pl.BlockSpec((bl, r_per, 8, 128), lambda i: (0, i, 0, 0)), tok,
                  pl.BlockSpec(nw.shape, lambda i: (0, 0))],
        out_specs=[tok, tok, tok, st_spec],
        out_shape=[tok_shape, jax.ShapeDtypeStruct((bl, seq, 512), MM_DTYPE), jax.ShapeDtypeStruct((bl, seq, 512), MM_DTYPE),
                   jax.ShapeDtypeStruct((bl, nc, GDN_HEADS, 128, 128), F32)],
        scratch_shapes=[pltpu.VMEM((bl, GDN_HEADS, 128, 128), F32)],
        compiler_params=_params("arbitrary"),
    )(tok3(u), tok3(w), tok3(qd), tok3(kd), qk5, gam4, tok3(pc), nw)
    return y.reshape(n, 512), (o, st, w, qd, kd, qk5, gam4, tinv, vn)


def _gdn_bwd(qkv, gb, pc, res, dyb, nw, bl, seq):
    n = qkv.shape[0]
    nc = seq // CHUNK
    o, st, w, qd, kd, qk5, gam4, tinv, vn = res
    tok3 = lambda t: t.reshape(bl, seq, 512)

    def scan_body(dy_ref, o_ref, og_ref, w_ref, qd_ref, kd_ref, qk_ref, gam_ref, nw_ref,
                  do_ref, dog_ref, dvn_ref, dst_ref, dnw_ref, ds_scr):
        @pl.when(pl.program_id(0) == 0)
        def _():
            ds_scr[...] = jnp.zeros_like(ds_scr)
            dnw_ref[...] = jnp.zeros_like(dnw_ref)

        units = [(b, h, slice(h * 128, (h + 1) * 128)) for b in range(bl) for h in range(GDN_HEADS)]
        dnw = jnp.zeros(nw.shape, F32)
        dss = [ds_scr[b, h] for b, h, _ in units]
        for r in reversed(range(r_scan)):
            rows = slice(r * CHUNK, (r + 1) * CHUNK)
            d_os = []
            for b, h, cols in units:
                _, vjp = jax.vjp(_rms_gate, o_ref[b, rows, cols], nw_ref[...], og_ref[b, rows, cols])
                d_o, dnw_h, dog = vjp(dy_ref[b, rows, cols])
                do_ref[b, rows, cols] = mm(d_o)
                dog_ref[b, rows, cols] = mm(dog)
                dnw = dnw + dnw_h
                d_os.append(mm(d_o))
            for (b, h, _), ds in zip(units, dss):
                dst_ref[b, r, h] = ds
            dvn_a = [_dot(kd_ref[b, rows, cols], mm(ds)) for (b, h, cols), ds in zip(units, dss)]
            dvns = [a + _dot_tn(qk_ref[b, r, h], d_o) for (b, h, cols), a, d_o in zip(units, dvn_a, d_os)]
            for (b, h, cols), dvn in zip(units, dvns):
                dvn_ref[b, rows, cols] = mm(dvn)
            dss = [ds * gam_ref[b, r, h:h + 1, :] + _dot_tn(
                jnp.concatenate([qd_ref[b, rows, cols], w_ref[b, rows, cols]], axis=0),
                jnp.concatenate([d_o, mm(-dvn)], axis=0))
                for (b, h, cols), d_o, ds, dvn in zip(units, d_os, dss, dvns)]
        dnw_ref[...] += dnw
        for (b, h, _), ds in zip(units, dss):
            ds_scr[b, h] = ds

    r_scan = GDN_SCAN_CHUNKS
    mm = lambda t: t.astype(MM_DTYPE)
    rev = lambda i: nc // r_scan - 1 - i
    tok = pl.BlockSpec((bl, r_scan * CHUNK, 512), lambda i: (0, rev(i), 0))
    st_spec = pl.BlockSpec((bl, r_scan, GDN_HEADS, 128, 128), lambda i: (0, rev(i), 0, 0, 0))
    tok_shape = jax.ShapeDtypeStruct((bl, seq, 512), F32)
    tok_mm = jax.ShapeDtypeStruct((bl, seq, 512), MM_DTYPE)
    d_o, dog, dvn, dst, dnw = pl.pallas_call(
        scan_body, name="gdn_scan_bwd", grid=(nc // r_scan,),
        in_specs=[tok] * 6 + [pl.BlockSpec((bl, r_scan, GDN_HEADS, CHUNK, CHUNK), lambda i: (0, rev(i), 0, 0, 0)),
                              pl.BlockSpec((bl, r_scan, 8, 128), lambda i: (0, rev(i), 0, 0)),
                              pl.BlockSpec(nw.shape, lambda i: (0, 0))],
        out_specs=[tok, tok, tok, st_spec, pl.BlockSpec(nw.shape, lambda i: (0, 0))],
        out_shape=[tok_mm, tok_mm, tok_mm, jax.ShapeDtypeStruct(st.shape, F32),
                   jax.ShapeDtypeStruct(nw.shape, F32)],
        scratch_shapes=[pltpu.VMEM((bl, GDN_HEADS, 128, 128), F32)],
        compiler_params=_params("arbitrary"),
    )(tok3(dyb), o, tok3(pc), tok3(w), tok3(qd), tok3(kd), qk5, gam4, nw)

    r_per = GDN_PREP_BWD_CHUNKS
    tm = r_per * CHUNK

    def prep_body(q_ref, k_ref, v_ref, gb_ref, t_ref, st_ref, dst_ref, dvn_ref, do_ref, vn_ref, dqkv_ref, dgb_ref):
        chunk_rows = [slice(r * CHUNK, (r + 1) * CHUNK) for r in range(r_per)]
        gather = lambda ref: [t for rows in chunk_rows for t in _head_cols(ref, rows)]
        units = [(r, h) for r in range(r_per) for h in range(GDN_HEADS)]
        t_known = [t_ref[r, h].astype(F32) for r, h in units]
        prep = lambda q, k, v, g: _gdn_prep_units(q, k, v, g, t_known)[:6]
        _, vjp = jax.vjp(prep, gather(q_ref), gather(k_ref), gather(v_ref), [gb_ref[rows, :] for rows in chunk_rows])
        ss = [st_ref[r, h] for r, h in units]
        dss = [dst_ref[r, h] for r, h in units]
        dvns, d_os, v_new = gather(dvn_ref), gather(do_ref), gather(vn_ref)
        both = [_dot_nt(jnp.concatenate([dvn, d_o], axis=0), s.astype(MM_DTYPE)) for dvn, d_o, s in zip(dvns, d_os, ss)]
        d_w = [-m[0:CHUNK] for m in both]
        d_qd = [m[CHUNK:2 * CHUNK] for m in both]
        d_qk = [_dot_nt(d_o, vn) for d_o, vn in zip(d_os, v_new)]
        d_kd = [_dot_nt(vn, ds.astype(MM_DTYPE)) for vn, ds in zip(v_new, dss)]
        d_gam = [_sum_rows(ds * s) for ds, s in zip(dss, ss)]
        dq, dk, dv, dgb = vjp(([d.astype(F32) for d in dvns], d_w, d_qk, d_qd, d_kd, d_gam))
        for i, (r, h) in enumerate(units):
            rows = chunk_rows[r]
            for part, d in enumerate((dq, dk, dv)):
                dqkv_ref[rows, part * 512 + h * 128:part * 512 + (h + 1) * 128] = d[i]
        for r, rows in enumerate(chunk_rows):
            dgb_ref[rows, :] = dgb[r]

    tokp = lambda j: pl.BlockSpec((tm, 512), lambda i: (i, j))
    st4 = pl.BlockSpec((r_per, GDN_HEADS, 128, 128), lambda i: (i, 0, 0, 0))
    dqkv, dgb = pl.pallas_call(
        prep_body, name="gdn_prep_bwd", grid=(n // tm,),
        in_specs=[tokp(0), tokp(1), tokp(2), pl.BlockSpec((tm, 128), lambda i: (i, 0)),
                  pl.BlockSpec((r_per, GDN_HEADS, CHUNK, CHUNK), lambda i: (i, 0, 0, 0)), st4, st4,
                  tokp(0), tokp(0), tokp(0)],
        out_specs=[pl.BlockSpec((tm, 1536), lambda i: (i, 0)), pl.BlockSpec((tm, 128), lambda i: (i, 0))],
        out_shape=[jax.ShapeDtypeStruct((n, 1536), F32), jax.ShapeDtypeStruct((n, 128), F32)],
        compiler_params=_params("parallel"),
    )(qkv, qkv, qkv, gb, tinv, st.reshape(bl * nc, GDN_HEADS, 128, 128), dst.reshape(bl * nc, GDN_HEADS, 128, 128),
      dvn.reshape(n, 512), d_o.reshape(n, 512), vn.reshape(n, 512))
    return dqkv, dog.reshape(n, 512), dgb, dnw


def _out_block(x2, tgt2, ya, yb, g1p3, wo, lnw, lnb, seq, tm=512):
    n = x2.shape[0]
    tpe = seq // tm
    bl = n // seq

    def body(x_ref, t_ref, ya_ref, yb_ref, g_ref, wo_ref, lnw_ref, lnb_ref,
             dz_ref, dya_ref, dyb_ref, dwo_ref, dg_ref, glw_ref, glb_ref, loss_ref):
        i = pl.program_id(0)

        @pl.when(i == 0)
        def _():
            dwo_ref[...] = jnp.zeros_like(dwo_ref)
            glw_ref[...] = jnp.zeros_like(glw_ref)
            glb_ref[...] = jnp.zeros_like(glb_ref)
            loss_ref[...] = jnp.zeros_like(loss_ref)

        @pl.when(i % tpe == 0)
        def _():
            dg_ref[...] = jnp.zeros_like(dg_ref)

        ya16 = ya_ref[...].astype(wo.dtype)
        yb16 = yb_ref[...].astype(wo.dtype)
        wa = wo_ref[0:GLA_WIDTH, :]
        wb = wo_ref[GLA_WIDTH:, :]
        y = _dot(ya16, wa) + _dot(yb16, wb)
        g1p = g_ref[0]
        z = ALPHA * x_ref[...] + g1p * y
        mu = jnp.mean(z, axis=-1, keepdims=True)
        zc = z - mu
        rstd = lax.rsqrt(jnp.mean(zc * zc, axis=-1, keepdims=True) + LN_EPS)
        xhat = zc * rstd
        diff = xhat * lnw_ref[...] + lnb_ref[...] - t_ref[...]
        loss_ref[...] += (0.5 / D_MODEL) * jnp.sum(jnp.sum(diff * diff, axis=-1, keepdims=True), axis=0, keepdims=True)
        dout = diff * (1.0 / D_MODEL)
        glw_ref[...] += jnp.sum(dout * xhat, axis=0, keepdims=True)
        glb_ref[...] += jnp.sum(dout, axis=0, keepdims=True)
        dxh = dout * lnw_ref[...]
        dz = rstd * (dxh - jnp.mean(dxh, axis=-1, keepdims=True)
                     - xhat * jnp.mean(dxh * xhat, axis=-1, keepdims=True))
        dz_ref[...] = dz
        dg_ref[0] += jnp.sum(dz * y, axis=0, keepdims=True)
        dy = (g1p * dz).astype(wo.dtype)
        dya_ref[...] = _dot_nt(dy, wa)
        dyb_ref[...] = _dot_nt(dy, wb)
        dwo_ref[0:GLA_WIDTH, :] += _dot_tn(ya16, dy)
        dwo_ref[GLA_WIDTH:, :] += _dot_tn(yb16, dy)

    row = lambda i: (i, 0)
    const = lambda i: (0, 0)
    per_ex = pl.BlockSpec((1, 1, D_MODEL), lambda i: (i // tpe, 0, 0))
    return pl.pallas_call(
        body, name="out_block", grid=(n // tm,),
        in_specs=[pl.BlockSpec((tm, D_MODEL), row), pl.BlockSpec((tm, D_MODEL), row),
                  pl.BlockSpec((tm, 512), row), pl.BlockSpec((tm, 512), row), per_ex,
                  pl.BlockSpec((D_MODEL, D_MODEL), const), pl.BlockSpec((1, D_MODEL), const),
                  pl.BlockSpec((1, D_MODEL), const)],
        out_specs=[pl.BlockSpec((tm, D_MODEL), row), pl.BlockSpec((tm, 512), row), pl.BlockSpec((tm, 512), row),
                   pl.BlockSpec((D_MODEL, D_MODEL), const), per_ex,
                   pl.BlockSpec((1, D_MODEL), const), pl.BlockSpec((1, D_MODEL), const),
                   pl.BlockSpec((1, 1), const)],
        out_shape=[jax.ShapeDtypeStruct((n, D_MODEL), F32), jax.ShapeDtypeStruct((n, 512), F32),
                   jax.ShapeDtypeStruct((n, 512), F32), jax.ShapeDtypeStruct((D_MODEL, D_MODEL), F32),
                   jax.ShapeDtypeStruct((bl, 1, D_MODEL), F32), jax.ShapeDtypeStruct((1, D_MODEL), F32),
                   jax.ShapeDtypeStruct((1, D_MODEL), F32), jax.ShapeDtypeStruct((1, 1), F32)],
        compiler_params=_params("arbitrary"),
    )(x2, tgt2, ya, yb, g1p3, wo, lnw, lnb)


def _proj_bwd_x(ds, ws, x2, dz, sc3, seq, tm=512):
    n = x2.shape[0]
    tpe = seq // tm
    bl = n // seq

    def body(da_ref, db_ref, dc_ref, dd1_ref, dd2_ref, wa_ref, wb_ref, wc_ref, wd_ref, x_ref, dz_ref, sc_ref,
             gx_ref, dsh_ref, dsc_ref):
        i = pl.program_id(0)

        @pl.when(i % tpe == 0)
        def _():
            dsh_ref[...] = jnp.zeros_like(dsh_ref)
            dsc_ref[...] = jnp.zeros_like(dsc_ref)

        cdt = ws[0].dtype
        dh = _dot(da_ref[...].astype(cdt), wa_ref[...])
        dh += _dot(db_ref[...].astype(cdt), wb_ref[...])
        dh += _dot(dc_ref[...].astype(cdt), wc_ref[...])
        dh += _dot((dd1_ref[...] + dd2_ref[...]).astype(cdt), wd_ref[...])
        gx_ref[...] = dh * sc_ref[0] + ALPHA * dz_ref[...]
        dsh_ref[0] += jnp.sum(dh, axis=0, keepdims=True)
        dsc_ref[0] += jnp.sum(dh * x_ref[...], axis=0, keepdims=True)

    row = lambda i: (i, 0)
    const = lambda i: (0, 0)
    per_ex = pl.BlockSpec((1, 1, D_MODEL), lambda i: (i // tpe, 0, 0))
    da, db, dc, (dd1, dd2) = ds
    return pl.pallas_call(
        body, name="proj_bwd_x", grid=(n // tm,),
        in_specs=[pl.BlockSpec((tm, d.shape[1]), row) for d in (da, db, dc, dd1, dd2)]
        + [pl.BlockSpec(w.shape, const) for w in ws]
        + [pl.BlockSpec((tm, D_MODEL), row), pl.BlockSpec((tm, D_MODEL), row), per_ex],
        out_specs=[pl.BlockSpec((tm, D_MODEL), row), per_ex, per_ex],
        out_shape=[jax.ShapeDtypeStruct((n, D_MODEL), F32), jax.ShapeDtypeStruct((bl, 1, D_MODEL), F32),
                   jax.ShapeDtypeStruct((bl, 1, D_MODEL), F32)],
        compiler_params=_params("arbitrary"),
    )(da, db, dc, dd1, dd2, *ws, x2, dz, sc3)


def _proj_bwd_w(x2, sc3, sh3, ds, seq, cdt, name, tm=512):
    n = x2.shape[0]
    tpe = seq // tm
    flat, groups = [], []
    for d in ds:
        parts = d if isinstance(d, tuple) else (d,)
        groups.append(len(parts))
        flat.extend(parts)
    nin = len(flat)

    def body(x_ref, sc_ref, sh_ref, *refs):
        i = pl.program_id(0)
        outs = refs[nin:]

        @pl.when(i == 0)
        def _():
            for o in outs:
                o[...] = jnp.zeros_like(o)

        h = (x_ref[...] * sc_ref[0] + sh_ref[0]).astype(cdt)
        pos = 0
        for o, cnt in zip(outs, groups):
            d = refs[pos][...]
            for extra in refs[pos + 1:pos + cnt]:
                d = d + extra[...]
            pos += cnt
            o[...] += _dot_tn(d.astype(cdt), h)

    row = lambda i: (i, 0)
    const = lambda i: (0, 0)
    per_ex = pl.BlockSpec((1, 1, D_MODEL), lambda i: (i // tpe, 0, 0))
    widths = [(d[0] if isinstance(d, tuple) else d).shape[1] for d in ds]
    return pl.pallas_call(
        body, name=name, grid=(n // tm,),
        in_specs=[pl.BlockSpec((tm, D_MODEL), row), per_ex, per_ex]
        + [pl.BlockSpec((tm, d.shape[1]), row) for d in flat],
        out_specs=[pl.BlockSpec((w, D_MODEL), const) for w in widths],
        out_shape=[jax.ShapeDtypeStruct((w, D_MODEL), F32) for w in widths],
        compiler_params=_params("arbitrary"),
    )(x2, sc3, sh3, *flat)


def _mod_block(c_all, w_ada_sh, b_blk):
    def body(c_ref, w_ref, b_ref, o_ref):
        o_ref[...] = _dot(c_ref[...], w_ref[...]) + b_ref[...]

    return pl.pallas_call(
        body, name="mod_block",
        out_shape=jax.ShapeDtypeStruct((c_all.shape[0], w_ada_sh.shape[1]), F32),
        compiler_params=pltpu.CompilerParams(vmem_limit_bytes=VMEM_LIMIT),
    )(c_all, w_ada_sh, b_blk)


def _ada_grads(c_all, dmod_all, dmod_blk):
    def body(c_ref, da_ref, db_ref, gw_ref, gb_ref):
        gw_ref[...] = _dot_tn(c_ref[...], db_ref[...])
        gb_ref[...] = jnp.sum(da_ref[...], axis=0, keepdims=True)

    return pl.pallas_call(
        body, name="ada_grads",
        out_shape=[jax.ShapeDtypeStruct((c_all.shape[1], dmod_blk.shape[1]), F32),
                   jax.ShapeDtypeStruct((1, dmod_all.shape[1]), F32)],
        compiler_params=pltpu.CompilerParams(vmem_limit_bytes=VMEM_LIMIT),
    )(c_all, dmod_all, dmod_blk)


def _sum_leading(parts, name):
    def body(p_ref, o_ref):
        acc = p_ref[0]
        for d in range(1, parts.shape[0]):
            acc = acc + p_ref[d]
        o_ref[...] = acc

    return pl.pallas_call(
        body, name=name, out_shape=jax.ShapeDtypeStruct(parts.shape[1:], F32),
        compiler_params=pltpu.CompilerParams(vmem_limit_bytes=VMEM_LIMIT),
    )(parts)


ELEMENTWISE_BLOCK_BYTES = 2 * 1024 * 1024


def _tile2d(rows, cols, row_align=8):
    if rows * cols * 4 <= ELEMENTWISE_BLOCK_BYTES:
        return rows, cols
    fits = [t for t in range(row_align, rows, row_align) if rows % t == 0 and t * cols * 4 <= ELEMENTWISE_BLOCK_BYTES]
    if fits:
        return fits[-1], cols
    fits = [t for t in range(128, cols, 128) if cols % t == 0 and rows * t * 4 <= ELEMENTWISE_BLOCK_BYTES]
    assert fits, (rows, cols)
    return rows, fits[-1]


def _add_n(arrs, name, out_dtypes=(F32,)):
    rows, cols = arrs[0].shape
    narrow = any(jnp.dtype(dt).itemsize < 4 for dt in tuple(out_dtypes) + tuple(a.dtype for a in arrs))
    tr, tc = _tile2d(rows, cols, 16 if narrow else 8)
    n_in = len(arrs)

    def body(*refs):
        acc = refs[0][...].astype(F32)
        for r in refs[1:n_in]:
            acc = acc + r[...].astype(F32)
        for o in refs[n_in:]:
            o[...] = acc.astype(o.dtype)

    spec = pl.BlockSpec((tr, tc), lambda i, j: (i, j))
    return pl.pallas_call(
        body, name=name, grid=(rows // tr, cols // tc), in_specs=[spec] * n_in, out_specs=[spec] * len(out_dtypes),
        out_shape=[jax.ShapeDtypeStruct((rows, cols), dt) for dt in out_dtypes],
        compiler_params=_params("parallel", "parallel"),
    )(*arrs)


def _chip_sum_blocks(a, b, per, blocks, name, chunk=128):
    rows, cols = a.shape
    padded = -(-per // 16) * 16
    assert rows >= (blocks - 1) * per + padded, (rows, per, blocks)

    def body(a_ref, b_ref, o_ref, o16_ref):
        for j in range(blocks):
            for r0 in range(0, padded, chunk):
                n_rows = min(chunk, padded - r0)
                src = pl.ds(j * per + r0, n_rows)
                s = a_ref[src, :] + b_ref[src, :]
                if per - r0 < n_rows:
                    s = jnp.where(_iota((n_rows, 1), 0) < per - r0, s, 0.0)
                o_ref[j, r0:r0 + n_rows, :] = s
                o16_ref[j, r0:r0 + n_rows, :] = s.astype(BF16)

    return pl.pallas_call(
        body, name=name,
        out_shape=[jax.ShapeDtypeStruct((blocks, padded, cols), F32), jax.ShapeDtypeStruct((blocks, padded, cols), BF16)],
        compiler_params=pltpu.CompilerParams(vmem_limit_bytes=VMEM_LIMIT),
    )(a, b)


GRAD_PAD_ROWS = 16


def _adamw(w, g, m, v, name):
    rows, cols = w.shape
    tr, tc = _tile2d(rows, cols)
    c1 = 1.0 / (1.0 - ADAM_B1 ** ADAM_STEP)
    c2 = 1.0 / (1.0 - ADAM_B2 ** ADAM_STEP)

    def body(w_ref, g_ref, m_ref, v_ref, d_ref, nm_ref, nv_ref):
        gg = g_ref[...]
        nm = ADAM_B1 * m_ref[...] + (1.0 - ADAM_B1) * gg
        nv = ADAM_B2 * v_ref[...] + (1.0 - ADAM_B2) * (gg * gg)
        nm_ref[...] = nm
        nv_ref[...] = nv
        d_ref[...] = -ADAM_LR * ((nm * c1) / (jnp.sqrt(nv * c2) + ADAM_EPS) + ADAM_WD * w_ref[...])

    spec = pl.BlockSpec((tr, tc), lambda i, j: (i, j))
    shp = jax.ShapeDtypeStruct((rows, cols), F32)
    return pl.pallas_call(
        body, name=name, grid=(rows // tr, cols // tc), in_specs=[spec] * 4, out_specs=[spec] * 3,
        out_shape=[shp, shp, shp], compiler_params=_params("parallel", "parallel"),
    )(w, g, m, v)


def _coords():
    return lax.axis_index("x"), lax.axis_index("y"), lax.axis_index("c")


def _all_gather8(blk, name):
    m_per, n = blk.shape

    def body(x_ref, out_ref, send_sems, recv_sems, local_sem):
        x, y, c = _coords()
        me, sibling = (x, y, c), (x, y, 1 - c)
        chips = [(1 - x, y), (x, 1 - y), (1 - x, 1 - y)]

        def rows(px, py, pc):
            return out_ref.at[pl.ds((4 * px + 2 * py + pc) * m_per, m_per), :]

        def copy(k, block, to, src=None):
            return pltpu.make_async_remote_copy(
                src_ref=rows(*block) if src is None else src, dst_ref=rows(*block),
                send_sem=send_sems.at[k], recv_sem=recv_sems.at[k], device_id=to, device_id_type=MESH)

        mine = pltpu.make_async_copy(x_ref, rows(*me), local_sem)
        mine.start()
        first = [copy(0, me, sibling, src=x_ref)]
        first += [copy(1 + j, me, (*chip, c), src=x_ref) for j, chip in enumerate(chips)]
        for cp in first:
            cp.start()
        passed = [copy(4 + j, (*chip, c), sibling) for j, chip in enumerate(chips)]
        for j, chip in enumerate(chips):
            copy(1 + j, (*chip, c), me).wait_recv()
            passed[j].start()
        copy(0, sibling, me).wait_recv()
        for j, chip in enumerate(chips):
            copy(4 + j, (*chip, 1 - c), me).wait_recv()
        for cp in first + passed:
            cp.wait_send()
        mine.wait()

    return pl.pallas_call(
        body, name=name,
        out_shape=jax.ShapeDtypeStruct((8 * m_per, n), blk.dtype),
        in_specs=[pl.BlockSpec(memory_space=pltpu.VMEM)],
        out_specs=pl.BlockSpec(memory_space=pltpu.VMEM),
        scratch_shapes=[pltpu.SemaphoreType.DMA((7,)), pltpu.SemaphoreType.DMA((7,)), pltpu.SemaphoreType.DMA],
        compiler_params=pltpu.CompilerParams(vmem_limit_bytes=VMEM_LIMIT),
    )(blk)


def _chip_gather(shards, split, name):
    k_arr = len(shards)

    def body(*refs):
        srcs, dsts = refs[:k_arr], refs[k_arr:2 * k_arr]
        send_sems, recv_sems, fwd_send_sems, fwd_recv_sems, local_sems = refs[2 * k_arr:]
        x, y, c = _coords()
        peers = [(1 - x, y, c), (x, 1 - y, c), (1 - x, 1 - y, c)]
        sibling = (x, y, 1 - c)
        me_chip = 2 * x + y

        def part(ref, a, core):
            if not split[a]:
                return ref
            half = shards[a].shape[1] // 2
            return ref.at[:, pl.ds(core * half, half)]

        def ici(a, j, src_chip, dst_dev):
            return pltpu.make_async_remote_copy(
                src_ref=part(srcs[a], a, c), dst_ref=part(dsts[a].at[src_chip], a, c),
                send_sem=send_sems.at[a, j], recv_sem=recv_sems.at[a, j], device_id=dst_dev, device_id_type=MESH)

        def d2d(a, j, src_chip, core):
            return pltpu.make_async_remote_copy(
                src_ref=part(dsts[a].at[src_chip], a, core), dst_ref=part(dsts[a].at[src_chip], a, core),
                send_sem=fwd_send_sems.at[a, j], recv_sem=fwd_recv_sems.at[a, j],
                device_id=sibling, device_id_type=MESH)

        local = [pltpu.make_async_copy(srcs[a], dsts[a].at[me_chip], local_sems.at[a]) for a in range(k_arr)]
        for cp in local:
            cp.start()
        sends = [ici(a, j, me_chip, peer) for a in range(k_arr) for j, peer in enumerate(peers)]
        for cp in sends:
            cp.start()
        forwards = []
        for a in range(k_arr):
            for j, peer in enumerate(peers):
                peer_chip = 2 * peer[0] + peer[1]
                ici(a, j, peer_chip, peer).wait_recv()
                if split[a]:
                    forwards.append(d2d(a, j, peer_chip, c))
                    forwards[-1].start()
        for a in range(k_arr):
            for j, peer in enumerate(peers):
                if split[a]:
                    d2d(a, j, 2 * peer[0] + peer[1], 1 - c).wait_recv()
        for cp in sends + forwards:
            cp.wait_send()
        for cp in local:
            cp.wait()

    any_spec = pl.BlockSpec(memory_space=pl.ANY)
    return pl.pallas_call(
        body, name=name,
        out_shape=[jax.ShapeDtypeStruct((4,) + s.shape, s.dtype) for s in shards],
        in_specs=[any_spec] * k_arr, out_specs=[any_spec] * k_arr,
        scratch_shapes=[pltpu.SemaphoreType.DMA((k_arr, 3))] * 4 + [pltpu.SemaphoreType.DMA((k_arr,))],
    )(*shards)


def _chip_scatter(pieces, name):
    k_arr = len(pieces)

    def body(*refs):
        srcs, dsts = refs[:k_arr], refs[k_arr:2 * k_arr]
        send_sems, recv_sems = refs[2 * k_arr:]
        x, y, c = _coords()
        peers = [(1 - x, y, c), (x, 1 - y, c), (1 - x, 1 - y, c)]
        copies = []
        for a in range(k_arr):
            for j, peer in enumerate(peers):
                copies.append(pltpu.make_async_remote_copy(
                    src_ref=srcs[a].at[2 * peer[0] + peer[1]], dst_ref=dsts[a].at[j],
                    send_sem=send_sems.at[a, j], recv_sem=recv_sems.at[a, j], device_id=peer, device_id_type=MESH))
        for cp in copies:
            cp.start()
        for cp in copies:
            cp.wait_recv()
        for cp in copies:
            cp.wait_send()

    any_spec = pl.BlockSpec(memory_space=pl.ANY)
    return pl.pallas_call(
        body, name=name,
        out_shape=[jax.ShapeDtypeStruct((3,) + p.shape[1:], p.dtype) for p in pieces],
        in_specs=[any_spec] * k_arr, out_specs=[any_spec] * k_arr,
        scratch_shapes=[pltpu.SemaphoreType.DMA((k_arr, 3)), pltpu.SemaphoreType.DMA((k_arr, 3))],
    )(*pieces)


def _sibling_swap(arrs, name):
    k_arr = len(arrs)

    def body(*refs):
        srcs, dsts = refs[:k_arr], refs[k_arr:2 * k_arr]
        send_sems, recv_sems = refs[2 * k_arr:]
        x, y, c = _coords()
        copies = [pltpu.make_async_remote_copy(
            src_ref=srcs[a], dst_ref=dsts[a], send_sem=send_sems.at[a], recv_sem=recv_sems.at[a],
            device_id=(x, y, 1 - c), device_id_type=MESH) for a in range(k_arr)]
        for cp in copies:
            cp.start()
        for cp in copies:
            cp.wait_recv()
        for cp in copies:
            cp.wait_send()

    any_spec = pl.BlockSpec(memory_space=pl.ANY)
    return pl.pallas_call(
        body, name=name,
        out_shape=[jax.ShapeDtypeStruct(a.shape, a.dtype) for a in arrs],
        in_specs=[any_spec] * k_arr, out_specs=[any_spec] * k_arr,
        scratch_shapes=[pltpu.SemaphoreType.DMA((k_arr,)), pltpu.SemaphoreType.DMA((k_arr,))],
    )(*arrs)


def _split_w_in(w_in_t):
    wa = jnp.concatenate([w_in_t[0:1024], w_in_t[1040:1552]], axis=0)
    wb = w_in_t[1552:3088]
    wc = w_in_t[3096:3608]
    wd = jnp.concatenate([w_in_t[1024:1040], w_in_t[3088:3096],
                          jnp.zeros((128 - SMALL_USED, w_in_t.shape[1]), w_in_t.dtype)], axis=0)
    return wa, wb, wc, wd


def _merge_dw_in(dwa, dwb, dwc, dwd):
    return jnp.concatenate([dwa[0:1024], dwd[0:GLA_RANK], dwa[1024:1536], dwb, dwd[GLA_RANK:SMALL_USED], dwc,
                            jnp.zeros((GRAD_PAD_ROWS, dwa.shape[1]), dwa.dtype)], axis=0)


def _local_step(x, mod, w_in16, w_out16, gla_wg, gla_bg, gla_nw, conv_w, a_log, dt_bias, gdn_nw, ln_w, ln_b, tgt):
    bl, seq, _ = x.shape
    n = bl * seq
    x2 = x.reshape(n, D_MODEL)
    tgt2 = tgt.reshape(n, D_MODEL)
    sh3 = mod[:, None, 0:D_MODEL]
    sc3 = 1.0 + mod[:, None, D_MODEL:2 * D_MODEL]
    g1p3 = 1.0 + mod[:, None, 2 * D_MODEL:]
    ws = _split_w_in(w_in16)
    wg = jnp.concatenate([gla_wg, jnp.zeros((128 - GLA_RANK, GLA_QK), F32)], axis=0)
    cw8 = jnp.concatenate([conv_w, jnp.zeros((8 - CONV_K, conv_w.shape[1]), F32)], axis=0)
    alog_v = jnp.zeros((1, 128), F32).at[:, LANE_A:LANE_A + GDN_HEADS].set(a_log)
    dtb_v = jnp.zeros((1, 128), F32).at[:, LANE_A:LANE_A + GDN_HEADS].set(dt_bias)

    pa, pb, pc, pd = _proj_fwd(x2, sc3, sh3, ws, seq)
    ya, st_a = _gla_fwd(pa, pd, wg, gla_bg, gla_nw, bl, seq)
    qkv, gb, conv_out = _gdn_pre_fwd(pb, pd, cw8, alog_v, dtb_v, bl, seq)
    yb, st_b = _gdn_fwd(qkv, gb, pc, gdn_nw, bl, seq)
    dz, dya, dyb, d_wo, d_gate, d_lnw, d_lnb, loss = _out_block(x2, tgt2, ya, yb, g1p3, w_out16, ln_w, ln_b, seq)
    da, dd1, d_wg, d_bg, d_nwa = _gla_bwd(pa, pd, st_a, dya, wg, gla_bg, gla_nw, bl, seq)
    dqkv, dc, dgb, d_nwb = _gdn_bwd(qkv, gb, pc, st_b, dyb, gdn_nw, bl, seq)
    db, dd2, d_cw8, d_alog, d_dtb = _gdn_pre_bwd(pb, conv_out, pd, dqkv, dgb, cw8, alog_v, dtb_v, bl, seq)
    gx, d_sh, d_sc = _proj_bwd_x((da, db, dc, (dd1, dd2)), ws, x2, dz, sc3, seq)
    (dwa,) = _proj_bwd_w(x2, sc3, sh3, [da], seq, w_in16.dtype, "proj_bwd_w_a")
    dwb, dwc, dwd = _proj_bwd_w(x2, sc3, sh3, [db, dc, (dd1, dd2)], seq, w_in16.dtype, "proj_bwd_w_bcd")
    grads = dict(
        w_in=_merge_dw_in(dwa, dwb, dwc, dwd),
        w_out=d_wo,
        gla_w_gate_up=d_wg[0:GLA_RANK, :],
        gla_b_gate=d_bg,
        gla_norm_w=d_nwa,
        gdn_conv_w=d_cw8[0:CONV_K, :],
        gdn_a_log=d_alog[:, LANE_A:LANE_A + GDN_HEADS],
        gdn_dt_bias=d_dtb[:, LANE_A:LANE_A + GDN_HEADS],
        gdn_norm_w=d_nwb,
        ln_w=d_lnw,
        ln_b=d_lnb,
        mod=jnp.concatenate([d_sh[:, 0, :], d_sc[:, 0, :], d_gate[:, 0, :]], axis=1),
    )
    return loss, gx.reshape(bl, seq, D_MODEL), grads


_SMALL = (("gla_b_gate", 256), ("gla_norm_w", 128), ("gdn_a_log", 4), ("gdn_dt_bias", 4), ("gdn_norm_w", 128),
          ("ln_w", 1024), ("ln_b", 1024), ("gla_w_gate_up", 16 * 256), ("gdn_conv_w", 4 * 1536), ("loss", 1),
          ("mod", 2 * 3072))


def _pack_small(grads):
    flat = jnp.concatenate([grads[k].reshape(-1) for k, _ in _SMALL])
    total = sum(sz for _, sz in _SMALL)
    rows = -(-total // 1024) * 8
    return jnp.concatenate([flat, jnp.zeros((rows * 128 - total,), F32)]).reshape(rows, 128)


def _unpack_small(flat):
    out, pos = {}, 0
    for k, sz in _SMALL:
        out[k] = flat[pos:pos + sz]
        pos += sz
    return out


def kernel(x, c, w_ada, b_ada, w_in, gla_w_gate_up, gla_b_gate, gla_norm_w, gdn_conv_w, gdn_a_log, gdn_dt_bias, gdn_norm_w, w_out, ln_w, ln_b, loss_target, m_w_ada, m_b_ada, m_w_in, m_gla_w_gate_up, m_gla_b_gate, m_gla_norm_w, m_gdn_conv_w, m_gdn_a_log, m_gdn_dt_bias, m_gdn_norm_w, m_w_out, m_ln_w, m_ln_b, v_w_ada, v_b_ada, v_w_in, v_gla_w_gate_up, v_gla_b_gate, v_gla_norm_w, v_gdn_conv_w, v_gdn_a_log, v_gdn_dt_bias, v_gdn_norm_w, v_w_out, v_ln_w, v_ln_b):
    ix, iy, ic = _coords()
    chip = 2 * ix + iy
    dev = 4 * ix + 2 * iy + ic
    bl = x.shape[0]
    ndev = 8

    c_all = _all_gather8(c.reshape(8, -1), "gather_c").reshape(ndev * bl, D_MODEL)
    ada_cols = w_ada.shape[2]
    b_blk = lax.dynamic_slice_in_dim(b_ada, chip * ada_cols, ada_cols, axis=1)
    mod_blk = _mod_block(c_all, w_ada[0], b_blk)
    mod_g = _all_gather8(mod_blk, "gather_mod").reshape(ndev, ndev * bl, ada_cols)
    mod_all = jnp.concatenate([mod_g[2 * j] for j in range(4)], axis=1)
    mod = lax.dynamic_slice_in_dim(mod_all, dev * bl, bl, axis=0)

    w_in_g, w_out_g, wg_g, cw_g = _chip_gather(
        [jnp.transpose(w_in[0]).astype(BF16), w_out[0].astype(BF16), gla_w_gate_up[0], gdn_conv_w[0]],
        [True, True, False, False], "gather_weights")
    w_in16 = w_in_g.reshape(IN_COLS, D_MODEL)
    w_out16 = w_out_g.reshape(D_MODEL, D_MODEL)
    gla_wg = jnp.concatenate([wg_g[j] for j in range(4)], axis=1)
    conv_w = jnp.concatenate([cw_g[j] for j in range(4)], axis=1)

    loss, grad_x, gr = _local_step(x, mod, w_in16, w_out16, gla_wg, gla_b_gate, gla_norm_w, conv_w,
                                   gdn_a_log, gdn_dt_bias, gdn_norm_w, ln_w, ln_b, loss_target)

    gr["loss"] = loss
    packed = _pack_small(gr)
    prow = packed.shape[0]
    gathered = _all_gather8(packed, "gather_small").reshape(ndev, prow, 128)
    small = _unpack_small(_sum_leading(gathered, "sum_small").reshape(-1))
    loss = small["loss"][0]
    mod_rows = gathered.reshape(ndev, prow * 128)[:, sum(sz for _, sz in _SMALL[:-1]):][:, :bl * 3 * D_MODEL]
    dmod_all = mod_rows.reshape(ndev * bl, 3 * D_MODEL)
    dmod_blk = lax.dynamic_slice_in_dim(dmod_all, chip * ada_cols, ada_cols, axis=1)
    g_w_ada, g_b_ada = _ada_grads(c_all, dmod_all, dmod_blk)
    wg_cols = gla_w_gate_up.shape[2]
    g_wg = lax.dynamic_slice_in_dim(small["gla_w_gate_up"].reshape(GLA_RANK, GLA_QK), chip * wg_cols, wg_cols, axis=1)
    cw_cols = gdn_conv_w.shape[2]
    g_cw = lax.dynamic_slice_in_dim(small["gdn_conv_w"].reshape(CONV_K, 3 * GDN_WIDTH), chip * cw_cols, cw_cols, axis=1)

    in_feats = w_in.shape[2]
    out_rows = w_out.shape[1]
    p_in = gr["w_in"]
    p_out = gr["w_out"].reshape(4, out_rows, D_MODEL)
    h_in, h_out = D_MODEL // 2, out_rows // 2
    mine_in = lax.dynamic_slice_in_dim(p_in, ic * h_in, h_in, axis=1)
    mine_out = lax.dynamic_slice_in_dim(p_out, ic * h_out, h_out, axis=1)
    theirs_in = lax.dynamic_slice_in_dim(p_in, (1 - ic) * h_in, h_in, axis=1)
    theirs_out = lax.dynamic_slice_in_dim(p_out, (1 - ic) * h_out, h_out, axis=1)
    got_in, got_out = _sibling_swap([theirs_in, theirs_out], "swap_halves")
    chip_in, chip_in16 = _chip_sum_blocks(mine_in, got_in, in_feats, 4, "chip_sum_in")
    chip_out, chip_out16 = _add_n([mine_out.reshape(4 * h_out, D_MODEL), got_out.reshape(4 * h_out, D_MODEL)],
                                  "chip_sum_out", (F32, BF16))
    chip_out = chip_out.reshape(4, h_out, D_MODEL)
    rs_in, rs_out = _chip_scatter([chip_in16, chip_out16.reshape(4, h_out, D_MODEL)], "scatter_grads")
    own_in = lax.dynamic_index_in_dim(chip_in, chip, axis=0, keepdims=False)
    own_out = lax.dynamic_index_in_dim(chip_out, chip, axis=0, keepdims=False)
    (half_in,) = _add_n([own_in, rs_in[0], rs_in[1], rs_in[2]], "reduce_in")
    (half_out,) = _add_n([own_out, rs_out[0], rs_out[1], rs_out[2]], "reduce_out")
    sib_in, sib_out = _sibling_swap([half_in, half_out], "swap_result")
    g_w_in_t = jnp.where(ic == 0, jnp.concatenate([half_in, sib_in], axis=1),
                         jnp.concatenate([sib_in, half_in], axis=1))[0:in_feats]
    g_w_out = jnp.where(ic == 0, jnp.concatenate([half_out, sib_out], axis=0),
                        jnp.concatenate([sib_out, half_out], axis=0))

    grads = dict(
        w_ada=g_w_ada[None], b_ada=g_b_ada, w_in=g_w_in_t, gla_w_gate_up=g_wg[None],
        gla_b_gate=small["gla_b_gate"].reshape(1, -1), gla_norm_w=small["gla_norm_w"].reshape(1, -1),
        gdn_conv_w=g_cw[None], gdn_a_log=small["gdn_a_log"].reshape(1, -1),
        gdn_dt_bias=small["gdn_dt_bias"].reshape(1, -1), gdn_norm_w=small["gdn_norm_w"].reshape(1, -1),
        w_out=g_w_out[None], ln_w=small["ln_w"].reshape(1, -1), ln_b=small["ln_b"].reshape(1, -1))
    weights = dict(w_ada=w_ada, b_ada=b_ada, w_in=w_in, gla_w_gate_up=gla_w_gate_up, gla_b_gate=gla_b_gate,
                   gla_norm_w=gla_norm_w, gdn_conv_w=gdn_conv_w, gdn_a_log=gdn_a_log, gdn_dt_bias=gdn_dt_bias,
                   gdn_norm_w=gdn_norm_w, w_out=w_out, ln_w=ln_w, ln_b=ln_b)
    m_in = dict(w_ada=m_w_ada, b_ada=m_b_ada, w_in=m_w_in, gla_w_gate_up=m_gla_w_gate_up, gla_b_gate=m_gla_b_gate,
                gla_norm_w=m_gla_norm_w, gdn_conv_w=m_gdn_conv_w, gdn_a_log=m_gdn_a_log, gdn_dt_bias=m_gdn_dt_bias,
                gdn_norm_w=m_gdn_norm_w, w_out=m_w_out, ln_w=m_ln_w, ln_b=m_ln_b)
    v_in = dict(w_ada=v_w_ada, b_ada=v_b_ada, w_in=v_w_in, gla_w_gate_up=v_gla_w_gate_up, gla_b_gate=v_gla_b_gate,
                gla_norm_w=v_gla_norm_w, gdn_conv_w=v_gdn_conv_w, gdn_a_log=v_gdn_a_log, gdn_dt_bias=v_gdn_dt_bias,
                gdn_norm_w=v_gdn_norm_w, w_out=v_w_out, ln_w=v_ln_w, ln_b=v_ln_b)
    names = list(weights)
    delta, new_m, new_v = {}, {}, {}
    for nm in names:
        shp = weights[nm].shape
        if nm == "w_in":
            to2d = lambda t: jnp.transpose(t[0])
            from2d = lambda t: jnp.transpose(t)[None]
            g2d = grads[nm]
        else:
            to2d = lambda t: t.reshape(-1, shp[-1])
            from2d = lambda t: t.reshape(shp)
            g2d = to2d(grads[nm])
        d, a, b = _adamw(to2d(weights[nm]), g2d, to2d(m_in[nm]), to2d(v_in[nm]), "adamw_" + nm)
        delta[nm], new_m[nm], new_v[nm] = from2d(d), from2d(a), from2d(b)
        grads[nm] = from2d(g2d)
    return (loss, grad_x, *[grads[k] for k in names], *[delta[k] for k in names],
            *[new_m[k] for k in names], *[new_v[k] for k in names])
```

```python
import functools

import jax
import jax.numpy as jnp
from jax import lax
from jax.experimental import pallas as pl
from jax.experimental.pallas import tpu as pltpu

F32 = jnp.float32
BF16 = jnp.bfloat16
HI = lax.Precision.HIGH
INV_PREC = None
MESH = pl.DeviceIdType.MESH

D_MODEL = 1024
GLA_HEADS = 4
GLA_DK = 64
GLA_DV = 128
GLA_QK = 256
GLA_WIDTH = 512
GLA_RANK = 16
GLA_GATE_NORM = 16.0
GDN_HEADS = 4
GDN_DK = 128
GDN_WIDTH = 512
CONV_K = 4
CHUNK = 64
LN_EPS = 1e-5
RMS_EPS = 1e-6
ALPHA = 2.0 ** 0.25
IN_COLS = 3608

LANE_A = GLA_RANK
LANE_B = GLA_RANK + GDN_HEADS
SMALL_USED = GLA_RANK + 2 * GDN_HEADS

ADAM_LR = 0.001
ADAM_B1 = 0.9
ADAM_B2 = 0.999
ADAM_EPS = 1e-08
ADAM_WD = 0.01
ADAM_STEP = 10

VMEM_LIMIT = 56 * 1024 * 1024


def _iota(shape, dim):
    return lax.broadcasted_iota(jnp.int32, shape, dim)


def _dot(a, b, prec=None):
    return lax.dot_general(a, b, (((1,), (0,)), ((), ())), precision=prec, preferred_element_type=F32)


def _dot_nt(a, b, prec=None):
    return lax.dot_general(a, b, (((1,), (1,)), ((), ())), precision=prec, preferred_element_type=F32)


def _dot_tn(a, b, prec=None):
    return lax.dot_general(a, b, (((0,), (0,)), ((), ())), precision=prec, preferred_element_type=F32)


def _log_sigmoid(z):
    return jnp.minimum(z, 0.0) - jnp.log1p(jnp.exp(-jnp.abs(z)))


def _softplus(z):
    return jnp.maximum(z, 0.0) + jnp.log1p(jnp.exp(-jnp.abs(z)))


def _silu(z):
    return z * jax.nn.sigmoid(z)


def _rms_gate(o, nw, og):
    return o * lax.rsqrt(jnp.mean(o * o, axis=-1, keepdims=True) + RMS_EPS) * nw * _silu(og)


def _params(*sem):
    return pltpu.CompilerParams(dimension_semantics=sem, vmem_limit_bytes=VMEM_LIMIT)


GLA_PAIRS = GLA_HEADS // 2


def _gla_chunk(qs, ks, lrs, vs, ogs, ss, wgs, bgs, nw):
    c = qs[0].shape[0]
    n_ep = len(ss)
    n_ex = n_ep // GLA_PAIRS
    n_chunks = len(qs) // n_ep
    pair_units = [(i // n_ep * n_ex + i % n_ep // GLA_PAIRS, i % GLA_PAIRS) for i in range(len(qs))]
    head_units = [(i // GLA_HEADS * GLA_PAIRS + i % GLA_HEADS // 2, i % 2) for i in range(len(vs))]
    row, col = _iota((c, c), 0), _iota((c, c), 1)
    causal = row >= col
    first_half = (_iota((c, 1), 0) < c // 2).astype(F32)
    lane = _iota((1, 128), 1)
    masks = [(lane < GLA_DK).astype(F32), (lane >= GLA_DK).astype(F32)]
    gs = [_log_sigmoid(_dot(lrs[ce], wgs[p]) + bgs[p]) * (1.0 / GLA_GATE_NORM) for ce, p in pair_units]
    bs = [_dot(causal.astype(F32), g, HI) for g in gs]
    b_ref = [jnp.sum(g * first_half, axis=0, keepdims=True) for g in gs]
    b_last = [jnp.sum(g, axis=0, keepdims=True) for g in gs]
    qsc = [q * (GLA_DK ** -0.5) for q in qs]
    qe = [q * jnp.exp(b - br) for q, b, br in zip(qsc, bs, b_ref)]
    ke = [k * jnp.exp(br - b) for k, b, br in zip(ks, bs, b_ref)]
    qb = [q * jnp.exp(b) for q, b in zip(qsc, bs)]
    kd = [k * jnp.exp(bl_ - b) for k, b, bl_ in zip(ks, bs, b_last)]
    decay = [jnp.exp(bl_) for bl_ in b_last]
    att = [jnp.where(causal, _dot_nt(qe[u] * masks[half], ke[u]), 0.0) for u, half in head_units]
    o_intra = [_dot(a, v) for a, v in zip(att, vs)]
    qbm = [qb[u] * masks[half] for u, half in head_units]
    kdm = [kd[u] * masks[half] for u, half in head_units]
    ys = []
    for r in range(n_chunks):
        heads_r = range(r * n_ex * GLA_HEADS, (r + 1) * n_ex * GLA_HEADS)
        o_inter = [_dot_nt(qbm[i], ss[head_units[i][0] - r * n_ep]) for i in heads_r]
        upd = [_dot_tn(vs[i], kdm[i]) for i in heads_r]
        ss = [s * decay[r * n_ep + j] + upd[2 * j] + upd[2 * j + 1] for j, s in enumerate(ss)]
        ys += [_rms_gate(o_intra[i] + oi, nw, ogs[i]) for i, oi in zip(heads_r, o_inter)]
    return ys, ss


def _unit_lower_inverse_chain(a_list):
    c = a_list[0].shape[0]
    eye = (_iota((c, c), 0) == _iota((c, c), 1)).astype(F32)
    ps = [-a for a in a_list]
    ts = [eye + p for p in ps]
    levels = max(c.bit_length() - 2, 0)
    if levels:
        ps = [_dot(p, p, INV_PREC) for p in ps]
    for level in range(levels):
        last = level == levels - 1
        both = [_dot(t if last else jnp.concatenate([t, p], axis=0), p, INV_PREC) for t, p in zip(ts, ps)]
        ts = [t + m[0:c] for t, m in zip(ts, both)]
        if not last:
            ps = [m[c:2 * c] for m in both]
    return ts


@jax.custom_vjp
def _unit_lower_inverse(a_list):
    return _unit_lower_inverse_chain(a_list)


def _unit_lower_inverse_fwd(a_list):
    ts = _unit_lower_inverse_chain(a_list)
    return ts, ts


def _unit_lower_inverse_bwd(ts, dts):
    xs = [_dot_nt(dt, t, INV_PREC) for dt, t in zip(dts, ts)]
    return ([-_dot_tn(t, x, INV_PREC) for t, x in zip(ts, xs)],)


_unit_lower_inverse.defvjp(_unit_lower_inverse_fwd, _unit_lower_inverse_bwd)


@jax.custom_vjp
def _unit_lower_inverse_known(a_list, ts):
    return ts


def _unit_lower_inverse_known_fwd(a_list, ts):
    return ts, ts


def _unit_lower_inverse_known_bwd(ts, dts):
    return _unit_lower_inverse_bwd(ts, dts) + ([jnp.zeros_like(t) for t in ts],)


_unit_lower_inverse_known.defvjp(_unit_lower_inverse_known_fwd, _unit_lower_inverse_known_bwd)


def _gdn_prep_units(qs, ks, vs, gbs, t_known=None):
    c = qs[0].shape[0]
    units = [divmod(i, GDN_HEADS) for i in range(len(qs))]
    row, col = _iota((c, c), 0), _iota((c, c), 1)
    causal, strict = row >= col, row > col
    if t_known is None:
        lane = _iota((1, 128), 1)
        d_alls = [_dot(causal.astype(F32), gb, HI) for gb in gbs]
        g_c, beta_c, d_c = [], [], []
        for r, h in units:
            sel_a = (lane == LANE_A + h).astype(F32)
            g_c.append(jnp.sum(gbs[r] * sel_a, axis=-1, keepdims=True))
            beta_c.append(jnp.sum(gbs[r] * (lane == LANE_B + h).astype(F32), axis=-1, keepdims=True))
            d_c.append(jnp.sum(d_alls[r] * sel_a, axis=-1, keepdims=True))
        d_diff = [jnp.broadcast_to(d, (c, c)) - jnp.broadcast_to(d, (c, c)).T for d in d_c]
    else:
        src = _iota((128, 128), 0)
        spread = jnp.concatenate([(src == base + h).astype(F32) for base in (LANE_A, LANE_B)
                                  for h in range(GDN_HEADS)], axis=1)
        width = GDN_HEADS * 128
        g_beta = [_dot(gb, spread, HI) for gb in gbs]
        d_alls = [_dot(causal.astype(F32), gbv[:, 0:width], HI) for gbv in g_beta]
        g_c = [g_beta[r][:, h * 128:(h + 1) * 128] for r, h in units]
        beta_c = [g_beta[r][:, width + h * 128:width + (h + 1) * 128] for r, h in units]
        d_c = [d_alls[r][:, h * 128:(h + 1) * 128] for r, h in units]
        d_diff = [d[:, 0:c] - d.T[0:c, :] for d in d_c]
    d_last = [jnp.sum(g, axis=0, keepdims=True) for g in g_c]
    decay_mat = [jnp.where(causal, jnp.exp(jnp.where(causal, dd, 0.0)), 0.0) for dd in d_diff]
    kb = [k * b for k, b in zip(ks, beta_c)]
    kbk_qk = [_dot_nt(jnp.concatenate([kbi, q], axis=0), k) for kbi, q, k in zip(kb, qs, ks)]
    a = [jnp.where(strict, m[0:c] * dm, 0.0) for m, dm in zip(kbk_qk, decay_mat)]
    qk = [jnp.where(causal, m[c:2 * c] * dm, 0.0) for m, dm in zip(kbk_qk, decay_mat)]
    t = _unit_lower_inverse(a) if t_known is None else _unit_lower_inverse_known(a, t_known)
    uw = [_dot(ti, jnp.concatenate([v * b, kbi * jnp.exp(d)], axis=1))
          for ti, v, b, kbi, d in zip(t, vs, beta_c, kb, d_c)]
    u = [m[:, 0:128] for m in uw]
    w = [m[:, 128:256] for m in uw]
    q_dec = [q * jnp.exp(d) for q, d in zip(qs, d_c)]
    k_dec = [k * jnp.exp(dl - d) for k, dl, d in zip(ks, d_last, d_c)]
    gamma = [jnp.exp(dl) for dl in d_last]
    return u, w, qk, q_dec, k_dec, gamma, t


def _sum_rows(t):
    return jnp.sum(t, axis=0, keepdims=True)


def _gdn_pre_elem(ps, ab, alog_v, dtb_v):
    outs = []
    for j, p in enumerate(ps):
        s = _silu(p)
        if j < 2 * GDN_HEADS:
            s = s * lax.rsqrt(jnp.sum(s * s, axis=-1, keepdims=True) + RMS_EPS)
        if j < GDN_HEADS:
            s = s * (GDN_DK ** -0.5)
        outs.append(s)
    lane = _iota((1, 128), 1)
    is_a = (lane >= LANE_A) & (lane < LANE_A + GDN_HEADS)
    is_b = (lane >= LANE_B) & (lane < LANE_B + GDN_HEADS)
    g = -jnp.exp(alog_v) * _softplus(ab + dtb_v)
    gb = jnp.where(is_a, g, jnp.where(is_b, jax.nn.sigmoid(ab), 0.0))
    return tuple(outs) + (gb,)


def _proj_fwd(x2, sc3, sh3, ws, seq, tm=512):
    n = x2.shape[0]
    tpe = seq // tm
    nw = len(ws)

    def body(x_ref, sc_ref, sh_ref, *refs):
        h = (x_ref[...] * sc_ref[0] + sh_ref[0]).astype(ws[0].dtype)
        for w_ref, o_ref in zip(refs[:nw], refs[nw:]):
            o_ref[...] = _dot_nt(h, w_ref[...])

    row = lambda i: (i, 0)
    per_ex = pl.BlockSpec((1, 1, D_MODEL), lambda i: (i // tpe, 0, 0))
    return pl.pallas_call(
        body, name="proj_fwd", grid=(n // tm,),
        in_specs=[pl.BlockSpec((tm, D_MODEL), row), per_ex, per_ex]
        + [pl.BlockSpec(w.shape, lambda i: (0, 0)) for w in ws],
        out_specs=[pl.BlockSpec((tm, w.shape[0]), row) for w in ws],
        out_shape=[jax.ShapeDtypeStruct((n, w.shape[0]), F32) for w in ws],
        compiler_params=_params("parallel"),
    )(x2, sc3, sh3, *ws)


GLA_SCAN_CHUNKS = 4


def _gla_operands(q_ref, k_ref, v_ref, og_ref, lr_ref, wg_ref, bg_ref, bl, r_per):
    chunks = [slice(r * CHUNK, (r + 1) * CHUNK) for r in range(r_per)]
    pair_cols = [slice(p * 128, (p + 1) * 128) for p in range(GLA_PAIRS)]
    head_cols = [slice(h * 128, (h + 1) * 128) for h in range(GLA_HEADS)]
    per_pair = lambda ref: [ref[e, rows, cols] for rows in chunks for e in range(bl) for cols in pair_cols]
    per_head = lambda ref: [ref[e, rows, cols] for rows in chunks for e in range(bl) for cols in head_cols]
    return (per_pair(q_ref), per_pair(k_ref), [lr_ref[e, rows, :] for rows in chunks for e in range(bl)],
            per_head(v_ref), per_head(og_ref)), ([wg_ref[:, cols] for cols in pair_cols],
                                                 [bg_ref[:, cols] for cols in pair_cols])


def _gla_fwd(pa, pd, wg, bg, nw, bl, seq):
    n = pa.shape[0]
    nc = seq // CHUNK
    r_per = GLA_SCAN_CHUNKS
    pairs = [(e, p) for e in range(bl) for p in range(GLA_PAIRS)]
    head_slots = [(slice(r * CHUNK, (r + 1) * CHUNK), e, slice(h * 128, (h + 1) * 128))
                  for r in range(r_per) for e in range(bl) for h in range(GLA_HEADS)]

    def body(q_ref, k_ref, v_ref, og_ref, lr_ref, wg_ref, bg_ref, nw_ref, y_ref, st_ref, s_scr):
        @pl.when(pl.program_id(0) == 0)
        def _():
            s_scr[...] = jnp.zeros_like(s_scr)

        ss = [s_scr[e, p] for e, p in pairs]
        for (e, p), s in zip(pairs, ss):
            st_ref[e, 0, p] = s
        acts, gate = _gla_operands(q_ref, k_ref, v_ref, og_ref, lr_ref, wg_ref, bg_ref, bl, r_per)
        ys, s_new = _gla_chunk(*acts, ss, *gate, nw_ref[...])
        for (rows, e, cols), y in zip(head_slots, ys):
            y_ref[e, rows, cols] = y.astype(y_ref.dtype)
        for (e, p), s in zip(pairs, s_new):
            s_scr[e, p] = s

    tok = lambda w, j: pl.BlockSpec((bl, r_per * CHUNK, w), lambda i: (0, i, j))
    const = lambda i: (0, 0)
    pa3 = pa.reshape(bl, seq, 1536)
    y, st = pl.pallas_call(
        body, name="gla_fwd", grid=(nc // r_per,),
        in_specs=[tok(256, 0), tok(256, 1), tok(512, 1), tok(512, 2), tok(128, 0),
                  pl.BlockSpec(wg.shape, const), pl.BlockSpec(bg.shape, const), pl.BlockSpec(nw.shape, const)],
        out_specs=[tok(512, 0), pl.BlockSpec((bl, 1, GLA_PAIRS, 128, 128), lambda i: (0, i, 0, 0, 0))],
        out_shape=[jax.ShapeDtypeStruct((bl, seq, 512), MM_DTYPE),
                   jax.ShapeDtypeStruct((bl, nc // r_per, GLA_PAIRS, 128, 128), F32)],
        scratch_shapes=[pltpu.VMEM((bl, GLA_PAIRS, 128, 128), F32)],
        compiler_params=_params("arbitrary"),
    )(pa3, pa3, pa3, pa3, pd.reshape(bl, seq, 128), wg, bg, nw)
    return y.reshape(n, 512), st


def _gla_bwd(pa, pd, st, dya, wg, bg, nw, bl, seq):
    n = pa.shape[0]
    nc = seq // CHUNK
    r_per = GLA_SCAN_CHUNKS
    steps = nc // r_per
    pairs = [(e, p) for e in range(bl) for p in range(GLA_PAIRS)]
    pair_cols = [slice(p * 128, (p + 1) * 128) for p in range(GLA_PAIRS)]
    chunks = [slice(r * CHUNK, (r + 1) * CHUNK) for r in range(r_per)]
    pair_slots = [(rows, e, p) for rows in chunks for e in range(bl) for p in range(GLA_PAIRS)]
    head_slots = [(rows, e, h) for rows in chunks for e in range(bl) for h in range(GLA_HEADS)]

    def body(q_ref, k_ref, v_ref, og_ref, lr_ref, st_ref, dy_ref, wg_ref, bg_ref, nw_ref,
             da_ref, dd_ref, dwg_ref, dbg_ref, dnw_ref, ds_scr):
        @pl.when(pl.program_id(0) == 0)
        def _():
            dwg_ref[...] = jnp.zeros_like(dwg_ref)
            dbg_ref[...] = jnp.zeros_like(dbg_ref)
            dnw_ref[...] = jnp.zeros_like(dnw_ref)
            ds_scr[...] = jnp.zeros_like(ds_scr)

        acts, gate = _gla_operands(q_ref, k_ref, v_ref, og_ref, lr_ref, wg_ref, bg_ref, bl, r_per)
        _, vjp = jax.vjp(_gla_chunk, *acts, [st_ref[e, 0, p] for e, p in pairs], *gate, nw_ref[...])
        dq, dk, dlr, dv, dog, ds, dwg, dbg, dnw = vjp(
            ([dy_ref[e, rows, h * 128:(h + 1) * 128] for rows, e, h in head_slots], [ds_scr[e, p] for e, p in pairs]))
        for i, (rows, e) in enumerate((rows, e) for rows in chunks for e in range(bl)):
            dd_ref[e, rows, :] = dlr[i]
        for i, (rows, e, p) in enumerate(pair_slots):
            da_ref[e, rows, pair_cols[p]] = dq[i].astype(da_ref.dtype)
            da_ref[e, rows, GLA_QK + p * 128:GLA_QK + (p + 1) * 128] = dk[i].astype(da_ref.dtype)
        for i, (rows, e, h) in enumerate(head_slots):
            da_ref[e, rows, 512 + h * 128:512 + (h + 1) * 128] = dv[i].astype(da_ref.dtype)
            da_ref[e, rows, 1024 + h * 128:1024 + (h + 1) * 128] = dog[i].astype(da_ref.dtype)
        for (e, p), d in zip(pairs, ds):
            ds_scr[e, p] = d
        for p, cols in enumerate(pair_cols):
            dwg_ref[:, cols] += dwg[p]
            dbg_ref[:, cols] += dbg[p]
        dnw_ref[...] += dnw

    tok = lambda w, j: pl.BlockSpec((bl, r_per * CHUNK, w), lambda i: (0, steps - 1 - i, j))
    const = lambda i: (0, 0)
    pa3 = pa.reshape(bl, seq, 1536)
    da, dd, dwg, dbg, dnw = pl.pallas_call(
        body, name="gla_bwd", grid=(steps,),
        in_specs=[tok(256, 0), tok(256, 1), tok(512, 1), tok(512, 2), tok(128, 0),
                  pl.BlockSpec((bl, 1, GLA_PAIRS, 128, 128), lambda i: (0, steps - 1 - i, 0, 0, 0)), tok(512, 0),
                  pl.BlockSpec(wg.shape, const), pl.BlockSpec(bg.shape, const), pl.BlockSpec(nw.shape, const)],
        out_specs=[tok(1536, 0), tok(128, 0),
                   pl.BlockSpec(wg.shape, const), pl.BlockSpec(bg.shape, const), pl.BlockSpec(nw.shape, const)],
        out_shape=[jax.ShapeDtypeStruct((bl, seq, 1536), MM_DTYPE), jax.ShapeDtypeStruct((bl, seq, 128), F32),
                   jax.ShapeDtypeStruct(wg.shape, F32), jax.ShapeDtypeStruct(bg.shape, F32),
                   jax.ShapeDtypeStruct(nw.shape, F32)],
        scratch_shapes=[pltpu.VMEM((bl, GLA_PAIRS, 128, 128), F32)],
        compiler_params=_params("arbitrary"),
    )(pa3, pa3, pa3, pa3, pd.reshape(bl, seq, 128), st, dya.reshape(bl, seq, 512), wg, bg, nw)
    return da.reshape(n, 1536), dd.reshape(n, 128), dwg, dbg, dnw


PRE_ROWS = 64
PRE_PIECES = [slice(j * 128, (j + 1) * 128) for j in range(3 * GDN_HEADS)]


def _rows_from(ref, start, rows, cols):
    lo = start // 8 * 8
    if lo == start:
        return ref[start:start + rows, cols]
    window = ref[lo:lo + rows + 8, cols]
    return pltpu.roll(window, rows + 8 - (start - lo), 0)[0:rows]


def _conv_taps(buf_ref, w_ref, base, rows, cols):
    acc = w_ref[0:1, cols] * _rows_from(buf_ref, base, rows, cols)
    for k in range(1, CONV_K):
        acc = acc + w_ref[k:k + 1, cols] * _rows_from(buf_ref, base + k, rows, cols)
    return acc


def _gdn_pre_fwd(pb, pd, cw8, alog_v, dtb_v, bl, seq, tm=512):
    n = pb.shape[0]
    tpe = seq // tm
    t8 = tm // 8

    def body(u_ref, prev_ref, ab_ref, w_ref, al_ref, dt_ref, qkv_ref, gb_ref, p_ref, buf):
        i = pl.program_id(0)
        keep = (i % tpe != 0).astype(F32)
        buf[0:8, :] = prev_ref[...] * keep
        buf[8:8 + tm, :] = u_ref[...]
        for r0 in range(0, tm, PRE_ROWS):
            rows = slice(r0, r0 + PRE_ROWS)
            ps = [_conv_taps(buf, w_ref, 8 - (CONV_K - 1) + r0, PRE_ROWS, cols) for cols in PRE_PIECES]
            outs = _gdn_pre_elem(ps, ab_ref[rows, :], al_ref[...], dt_ref[...])
            for cols, p, out in zip(PRE_PIECES, ps, outs):
                p_ref[rows, cols] = p
                qkv_ref[rows, cols] = out
            gb_ref[rows, :] = outs[len(PRE_PIECES)]

    row = lambda i: (i, 0)
    const = lambda i: (0, 0)
    return pl.pallas_call(
        body, name="gdn_pre_fwd", grid=(n // tm,),
        in_specs=[pl.BlockSpec((tm, 1536), row),
                  pl.BlockSpec((8, 1536), lambda i: (jnp.maximum(i * t8 - 1, 0), 0)),
                  pl.BlockSpec((tm, 128), row),
                  pl.BlockSpec((8, 1536), const), pl.BlockSpec((1, 128), const), pl.BlockSpec((1, 128), const)],
        out_specs=[pl.BlockSpec((tm, 1536), row), pl.BlockSpec((tm, 128), row), pl.BlockSpec((tm, 1536), row)],
        out_shape=[jax.ShapeDtypeStruct((n, 1536), F32), jax.ShapeDtypeStruct((n, 128), F32),
                   jax.ShapeDtypeStruct((n, 1536), F32)],
        scratch_shapes=[pltpu.VMEM((tm + 8, 1536), F32)],
        compiler_params=_params("parallel"),
    )(pb, pb, pd, cw8, alog_v, dtb_v)


def _gdn_pre_bwd(pb, conv_out, pd, dqkv, dgb, cw8, alog_v, dtb_v, bl, seq, tm=512):
    n = pb.shape[0]
    tpe = seq // tm
    t8 = tm // 8
    nb8 = n // 8
    ext = tm + 8

    def body(u_ref, p_ref, pn_ref, ab_ref, abn_ref, dq_ref, dqn_ref, dgb_ref, w_ref, al_ref, dt_ref,
             du_ref, dab_ref, dw_ref, dal_ref, ddt_ref, dpbuf):
        i = pl.program_id(0)

        @pl.when(i == 0)
        def _():
            dw_ref[...] = jnp.zeros_like(dw_ref)
            dal_ref[...] = jnp.zeros_like(dal_ref)
            ddt_ref[...] = jnp.zeros_like(ddt_ref)

        keep_next = (i % tpe != tpe - 1).astype(F32)
        zeros8 = jnp.zeros((8, 128), F32)
        dal, ddt = jnp.zeros((1, 128), F32), jnp.zeros((1, 128), F32)
        for r0 in range(0, tm, PRE_ROWS):
            rows = slice(r0, r0 + PRE_ROWS)
            last = r0 + PRE_ROWS == tm
            along = lambda own, extra: jnp.concatenate([own, extra], axis=0) if last else own
            ps = [along(p_ref[rows, cols], pn_ref[:, cols]) for cols in PRE_PIECES]
            ab = along(ab_ref[rows, :], abn_ref[...])
            _, vjp = jax.vjp(_gdn_pre_elem, ps, ab, al_ref[...], dt_ref[...])
            cts = tuple(along(dq_ref[rows, cols], dqn_ref[:, cols] * keep_next) for cols in PRE_PIECES)
            cts += (along(dgb_ref[rows, :], zeros8),)
            dps, dab, dal_r, ddt_r = vjp(cts)
            out_rows = slice(r0, r0 + PRE_ROWS + (8 if last else 0))
            for cols, dp in zip(PRE_PIECES, dps):
                dpbuf[out_rows, cols] = dp
            dab_ref[rows, :] = dab[0:PRE_ROWS, :]
            dal, ddt = dal + dal_r, ddt + ddt_r
        dal_ref[...] += dal
        ddt_ref[...] += ddt
        for cols in PRE_PIECES:
            dw = [jnp.zeros((1, 128), F32) for _ in range(CONV_K)]
            for r0 in range(0, tm, PRE_ROWS):
                u = u_ref[r0:r0 + PRE_ROWS, cols]
                du = None
                for k in range(CONV_K):
                    dp_k = _rows_from(dpbuf, r0 + CONV_K - 1 - k, PRE_ROWS, cols)
                    term = w_ref[k:k + 1, cols] * dp_k
                    du = term if du is None else du + term
                    dw[k] = dw[k] + jnp.sum(u * dp_k, axis=0, keepdims=True)
                du_ref[r0:r0 + PRE_ROWS, cols] = du.astype(du_ref.dtype)
            for k in range(CONV_K):
                dw_ref[k:k + 1, cols] += dw[k]

    row = lambda i: (i, 0)
    next8 = lambda i: (jnp.minimum((i + 1) * t8, nb8 - 1), 0)
    const = lambda i: (0, 0)
    return pl.pallas_call(
        body, name="gdn_pre_bwd", grid=(n // tm,),
        in_specs=[pl.BlockSpec((tm, 1536), row), pl.BlockSpec((tm, 1536), row), pl.BlockSpec((8, 1536), next8),
                  pl.BlockSpec((tm, 128), row), pl.BlockSpec((8, 128), next8),
                  pl.BlockSpec((tm, 1536), row), pl.BlockSpec((8, 1536), next8),
                  pl.BlockSpec((tm, 128), row),
                  pl.BlockSpec((8, 1536), const), pl.BlockSpec((1, 128), const), pl.BlockSpec((1, 128), const)],
        out_specs=[pl.BlockSpec((tm, 1536), row), pl.BlockSpec((tm, 128), row),
                   pl.BlockSpec((8, 1536), const), pl.BlockSpec((1, 128), const), pl.BlockSpec((1, 128), const)],
        out_shape=[jax.ShapeDtypeStruct((n, 1536), MM_DTYPE), jax.ShapeDtypeStruct((n, 128), F32),
                   jax.ShapeDtypeStruct((8, 1536), F32), jax.ShapeDtypeStruct((1, 128), F32),
                   jax.ShapeDtypeStruct((1, 128), F32)],
        scratch_shapes=[pltpu.VMEM((ext, 1536), F32)],
        compiler_params=_params("arbitrary"),
    )(pb, conv_out, conv_out, pd, pd, dqkv, dqkv, dgb, cw8, alog_v, dtb_v)


GDN_PREP_CHUNKS = 4
GDN_PREP_BWD_CHUNKS = 4
GDN_SCAN_CHUNKS = 4
MM_DTYPE = BF16


def _head_cols(ref, rows, base=0):
    return [ref[rows, base + h * 128:base + (h + 1) * 128] for h in range(GDN_HEADS)]


def _gdn_prep(qkv, gb):
    n = qkv.shape[0]
    r_per = GDN_PREP_CHUNKS
    tm = r_per * CHUNK

    def body(q_ref, k_ref, v_ref, gb_ref, u_ref, w_ref, qd_ref, kd_ref, qk_ref, t_ref, gam_ref):
        rowid = _iota((8, 128), 0)
        chunk_rows = [slice(r * CHUNK, (r + 1) * CHUNK) for r in range(r_per)]
        gather = lambda ref: [t for rows in chunk_rows for t in _head_cols(ref, rows)]
        u, w, qk, qd, kd, gamma, tinv = _gdn_prep_units(gather(q_ref), gather(k_ref), gather(v_ref),
                                                        [gb_ref[rows, :] for rows in chunk_rows])
        for r, rows in enumerate(chunk_rows):
            gam = jnp.zeros((8, 128), F32)
            for h in range(GDN_HEADS):
                i = r * GDN_HEADS + h
                cols = slice(h * 128, (h + 1) * 128)
                u_ref[rows, cols] = u[i]
                w_ref[rows, cols] = w[i].astype(MM_DTYPE)
                qd_ref[rows, cols] = qd[i].astype(MM_DTYPE)
                kd_ref[rows, cols] = kd[i].astype(MM_DTYPE)
                qk_ref[r, h] = qk[i].astype(MM_DTYPE)
                t_ref[r, h] = tinv[i].astype(MM_DTYPE)
                gam = jnp.where(rowid == h, gamma[i], gam)
            gam_ref[r] = gam

    tok = lambda j: pl.BlockSpec((tm, 512), lambda i: (i, j))
    return pl.pallas_call(
        body, name="gdn_prep", grid=(n // tm,),
        in_specs=[tok(0), tok(1), tok(2), pl.BlockSpec((tm, 128), lambda i: (i, 0))],
        out_specs=[tok(0)] * 4 + [pl.BlockSpec((r_per, GDN_HEADS, CHUNK, CHUNK), lambda i: (i, 0, 0, 0))] * 2
        + [pl.BlockSpec((r_per, 8, 128), lambda i: (i, 0, 0))],
        out_shape=[jax.ShapeDtypeStruct((n, 512), F32)] + [jax.ShapeDtypeStruct((n, 512), MM_DTYPE)] * 3
        + [jax.ShapeDtypeStruct((n // CHUNK, GDN_HEADS, CHUNK, CHUNK), MM_DTYPE)] * 2
        + [jax.ShapeDtypeStruct((n // CHUNK, 8, 128), F32)],
        compiler_params=_params("parallel"),
    )(qkv, qkv, qkv, gb)


def _gdn_fwd(qkv, gb, pc, nw, bl, seq):
    n = qkv.shape[0]
    nc = seq // CHUNK
    u, w, qd, kd, qk, tinv, gam = _gdn_prep(qkv, gb)
    tok3 = lambda t: t.reshape(bl, seq, 512)
    qk5 = qk.reshape(bl, nc, GDN_HEADS, CHUNK, CHUNK)
    gam4 = gam.reshape(bl, nc, 8, 128)

    r_per = GDN_SCAN_CHUNKS
    mm = lambda t: t.astype(MM_DTYPE)

    def body(u_ref, w_ref, qd_ref, kd_ref, qk_ref, gam_ref, og_ref, nw_ref, o_ref, y_ref, vn_ref, st_ref, s_scr):
        @pl.when(pl.program_id(0) == 0)
        def _():
            s_scr[...] = jnp.zeros_like(s_scr)

        units = [(b, h, slice(h * 128, (h + 1) * 128)) for b in range(bl) for h in range(GDN_HEADS)]
        ss = [s_scr[b, h] for b, h, _ in units]
        for r in range(r_per):
            rows = slice(r * CHUNK, (r + 1) * CHUNK)
            for (b, h, _), s in zip(units, ss):
                st_ref[b, r, h] = s
            ws_qs = [_dot(jnp.concatenate([w_ref[b, rows, cols], qd_ref[b, rows, cols]], axis=0), mm(s))
                     for (b, h, cols), s in zip(units, ss)]
            v_new = [u_ref[b, rows, cols] - m[0:CHUNK] for (b, h, cols), m in zip(units, ws_qs)]
            os_ = [m[CHUNK:2 * CHUNK] + _dot(qk_ref[b, r, h], mm(vn))
                   for (b, h, cols), m, vn in zip(units, ws_qs, v_new)]
            ss = [s * gam_ref[b, r, h:h + 1, :] + _dot_tn(kd_ref[b, rows, cols], mm(vn))
                  for (b, h, cols), s, vn in zip(units, ss, v_new)]
            for (b, h, cols), vn, o in zip(units, v_new, os_):
                vn_ref[b, rows, cols] = mm(vn)
                o_ref[b, rows, cols] = o
                y_ref[b, rows, cols] = mm(_rms_gate(o, nw_ref[...], og_ref[b, rows, cols]))
        for (b, h, _), s in zip(units, ss):
            s_scr[b, h] = s

    tok = pl.BlockSpec((bl, r_per * CHUNK, 512), lambda i: (0, i, 0))
    st_spec = pl.BlockSpec((bl, r_per, GDN_HEADS, 128, 128), lambda i: (0, i, 0, 0, 0))
    tok_shape = jax.ShapeDtypeStruct((bl, seq, 512), F32)
    o, y, vn, st = pl.pallas_call(
        body, name="gdn_scan_fwd", grid=(nc // r_per,),
        in_specs=[tok, tok, tok, tok,
                  pl.BlockSpec((bl, r_per, GDN_HEADS, CHUNK, CHUNK), lambda i: (0, i, 0, 0, 0)),
                  pl.BlockSpec((bl, r_per, 8, 128), lambda i: (0, i, 0, 0)), tok,
                  pl.BlockSpec(nw.shape, lambda i: (0, 0))],
        out_specs=[tok, tok, tok, st_spec],
        out_shape=[tok_shape, jax.ShapeDtypeStruct((bl, seq, 512), MM_DTYPE), jax.ShapeDtypeStruct((bl, seq, 512), MM_DTYPE),
                   jax.ShapeDtypeStruct((bl, nc, GDN_HEADS, 128, 128), F32)],
        scratch_shapes=[pltpu.VMEM((bl, GDN_HEADS, 128, 128), F32)],
        compiler_params=_params("arbitrary"),
    )(tok3(u), tok3(w), tok3(qd), tok3(kd), qk5, gam4, tok3(pc), nw)
    return y.reshape(n, 512), (o, st, w, qd, kd, qk5, gam4, tinv, vn)


def _gdn_bwd(qkv, gb, pc, res, dyb, nw, bl, seq):
    n = qkv.shape[0]
    nc = seq // CHUNK
    o, st, w, qd, kd, qk5, gam4, tinv, vn = res
    tok3 = lambda t: t.reshape(bl, seq, 512)

    def scan_body(dy_ref, o_ref, og_ref, w_ref, qd_ref, kd_ref, qk_ref, gam_ref, nw_ref,
                  do_ref, dog_ref, dvn_ref, dst_ref, dnw_ref, ds_scr):
        @pl.when(pl.program_id(0) == 0)
        def _():
            ds_scr[...] = jnp.zeros_like(ds_scr)
            dnw_ref[...] = jnp.zeros_like(dnw_ref)

        units = [(b, h, slice(h * 128, (h + 1) * 128)) for b in range(bl) for h in range(GDN_HEADS)]
        dnw = jnp.zeros(nw.shape, F32)
        dss = [ds_scr[b, h] for b, h, _ in units]
        for r in reversed(range(r_scan)):
            rows = slice(r * CHUNK, (r + 1) * CHUNK)
            d_os = []
            for b, h, cols in units:
                _, vjp = jax.vjp(_rms_gate, o_ref[b, rows, cols], nw_ref[...], og_ref[b, rows, cols])
                d_o, dnw_h, dog = vjp(dy_ref[b, rows, cols])
                do_ref[b, rows, cols] = mm(d_o)
                dog_ref[b, rows, cols] = mm(dog)
                dnw = dnw + dnw_h
                d_os.append(mm(d_o))
            for (b, h, _), ds in zip(units, dss):
                dst_ref[b, r, h] = ds
            dvn_a = [_dot(kd_ref[b, rows, cols], mm(ds)) for (b, h, cols), ds in zip(units, dss)]
            dvns = [a + _dot_tn(qk_ref[b, r, h], d_o) for (b, h, cols), a, d_o in zip(units, dvn_a, d_os)]
            for (b, h, cols), dvn in zip(units, dvns):
                dvn_ref[b, rows, cols] = mm(dvn)
            dss = [ds * gam_ref[b, r, h:h + 1, :] + _dot_tn(
                jnp.concatenate([qd_ref[b, rows, cols], w_ref[b, rows, cols]], axis=0),
                jnp.concatenate([d_o, mm(-dvn)], axis=0))
                for (b, h, cols), d_o, ds, dvn in zip(units, d_os, dss, dvns)]
        dnw_ref[...] += dnw
        for (b, h, _), ds in zip(units, dss):
            ds_scr[b, h] = ds

    r_scan = GDN_SCAN_CHUNKS
    mm = lambda t: t.astype(MM_DTYPE)
    rev = lambda i: nc // r_scan - 1 - i
    tok = pl.BlockSpec((bl, r_scan * CHUNK, 512), lambda i: (0, rev(i), 0))
    st_spec = pl.BlockSpec((bl, r_scan, GDN_HEADS, 128, 128), lambda i: (0, rev(i), 0, 0, 0))
    tok_shape = jax.ShapeDtypeStruct((bl, seq, 512), F32)
    tok_mm = jax.ShapeDtypeStruct((bl, seq, 512), MM_DTYPE)
    d_o, dog, dvn, dst, dnw = pl.pallas_call(
        scan_body, name="gdn_scan_bwd", grid=(nc // r_scan,),
        in_specs=[tok] * 6 + [pl.BlockSpec((bl, r_scan, GDN_HEADS, CHUNK, CHUNK), lambda i: (0, rev(i), 0, 0, 0)),
                              pl.BlockSpec((bl, r_scan, 8, 128), lambda i: (0, rev(i), 0, 0)),
                              pl.BlockSpec(nw.shape, lambda i: (0, 0))],
        out_specs=[tok, tok, tok, st_spec, pl.BlockSpec(nw.shape, lambda i: (0, 0))],
        out_shape=[tok_mm, tok_mm, tok_mm, jax.ShapeDtypeStruct(st.shape, F32),
                   jax.ShapeDtypeStruct(nw.shape, F32)],
        scratch_shapes=[pltpu.VMEM((bl, GDN_HEADS, 128, 128), F32)],
        compiler_params=_params("arbitrary"),
    )(tok3(dyb), o, tok3(pc), tok3(w), tok3(qd), tok3(kd), qk5, gam4, nw)

    r_per = GDN_PREP_BWD_CHUNKS
    tm = r_per * CHUNK

    def prep_body(q_ref, k_ref, v_ref, gb_ref, t_ref, st_ref, dst_ref, dvn_ref, do_ref, vn_ref, dqkv_ref, dgb_ref):
        chunk_rows = [slice(r * CHUNK, (r + 1) * CHUNK) for r in range(r_per)]
        gather = lambda ref: [t for rows in chunk_rows for t in _head_cols(ref, rows)]
        units = [(r, h) for r in range(r_per) for h in range(GDN_HEADS)]
        t_known = [t_ref[r, h].astype(F32) for r, h in units]
        prep = lambda q, k, v, g: _gdn_prep_units(q, k, v, g, t_known)[:6]
        _, vjp = jax.vjp(prep, gather(q_ref), gather(k_ref), gather(v_ref), [gb_ref[rows, :] for rows in chunk_rows])
        ss = [st_ref[r, h] for r, h in units]
        dss = [dst_ref[r, h] for r, h in units]
        dvns, d_os, v_new = gather(dvn_ref), gather(do_ref), gather(vn_ref)
        both = [_dot_nt(jnp.concatenate([dvn, d_o], axis=0), s.astype(MM_DTYPE)) for dvn, d_o, s in zip(dvns, d_os, ss)]
        d_w = [-m[0:CHUNK] for m in both]
        d_qd = [m[CHUNK:2 * CHUNK] for m in both]
        d_qk = [_dot_nt(d_o, vn) for d_o, vn in zip(d_os, v_new)]
        d_kd = [_dot_nt(vn, ds.astype(MM_DTYPE)) for vn, ds in zip(v_new, dss)]
        d_gam = [_sum_rows(ds * s) for ds, s in zip(dss, ss)]
        dq, dk, dv, dgb = vjp(([d.astype(F32) for d in dvns], d_w, d_qk, d_qd, d_kd, d_gam))
        for i, (r, h) in enumerate(units):
            rows = chunk_rows[r]
            for part, d in enumerate((dq, dk, dv)):
                dqkv_ref[rows, part * 512 + h * 128:part * 512 + (h + 1) * 128] = d[i]
        for r, rows in enumerate(chunk_rows):
            dgb_ref[rows, :] = dgb[r]

    tokp = lambda j: pl.BlockSpec((tm, 512), lambda i: (i, j))
    st4 = pl.BlockSpec((r_per, GDN_HEADS, 128, 128), lambda i: (i, 0, 0, 0))
    dqkv, dgb = pl.pallas_call(
        prep_body, name="gdn_prep_bwd", grid=(n // tm,),
        in_specs=[tokp(0), tokp(1), tokp(2), pl.BlockSpec((tm, 128), lambda i: (i, 0)),
                  pl.BlockSpec((r_per, GDN_HEADS, CHUNK, CHUNK), lambda i: (i, 0, 0, 0)), st4, st4,
                  tokp(0), tokp(0), tokp(0)],
        out_specs=[pl.BlockSpec((tm, 1536), lambda i: (i, 0)), pl.BlockSpec((tm, 128), lambda i: (i, 0))],
        out_shape=[jax.ShapeDtypeStruct((n, 1536), F32), jax.ShapeDtypeStruct((n, 128), F32)],
        compiler_params=_params("parallel"),
    )(qkv, qkv, qkv, gb, tinv, st.reshape(bl * nc, GDN_HEADS, 128, 128), dst.reshape(bl * nc, GDN_HEADS, 128, 128),
      dvn.reshape(n, 512), d_o.reshape(n, 512), vn.reshape(n, 512))
    return dqkv, dog.reshape(n, 512), dgb, dnw


def _out_block(x2, tgt2, ya, yb, g1p3, wo, lnw, lnb, seq, tm=512):
    n = x2.shape[0]
    tpe = seq // tm
    bl = n // seq

    def body(x_ref, t_ref, ya_ref, yb_ref, g_ref, wo_ref, lnw_ref, lnb_ref,
             dz_ref, dya_ref, dyb_ref, dwo_ref, dg_ref, glw_ref, glb_ref, loss_ref):
        i = pl.program_id(0)

        @pl.when(i == 0)
        def _():
            dwo_ref[...] = jnp.zeros_like(dwo_ref)
            glw_ref[...] = jnp.zeros_like(glw_ref)
            glb_ref[...] = jnp.zeros_like(glb_ref)
            loss_ref[...] = jnp.zeros_like(loss_ref)

        @pl.when(i % tpe == 0)
        def _():
            dg_ref[...] = jnp.zeros_like(dg_ref)

        ya16 = ya_ref[...].astype(wo.dtype)
        yb16 = yb_ref[...].astype(wo.dtype)
        wa = wo_ref[0:GLA_WIDTH, :]
        wb = wo_ref[GLA_WIDTH:, :]
        y = _dot(ya16, wa) + _dot(yb16, wb)
        g1p = g_ref[0]
        z = ALPHA * x_ref[...] + g1p * y
        mu = jnp.mean(z, axis=-1, keepdims=True)
        zc = z - mu
        rstd = lax.rsqrt(jnp.mean(zc * zc, axis=-1, keepdims=True) + LN_EPS)
        xhat = zc * rstd
        diff = xhat * lnw_ref[...] + lnb_ref[...] - t_ref[...]
        loss_ref[...] += (0.5 / D_MODEL) * jnp.sum(jnp.sum(diff * diff, axis=-1, keepdims=True), axis=0, keepdims=True)
        dout = diff * (1.0 / D_MODEL)
        glw_ref[...] += jnp.sum(dout * xhat, axis=0, keepdims=True)
        glb_ref[...] += jnp.sum(dout, axis=0, keepdims=True)
        dxh = dout * lnw_ref[...]
        dz = rstd * (dxh - jnp.mean(dxh, axis=-1, keepdims=True)
                     - xhat * jnp.mean(dxh * xhat, axis=-1, keepdims=True))
        dz_ref[...] = dz
        dg_ref[0] += jnp.sum(dz * y, axis=0, keepdims=True)
        dy = (g1p * dz).astype(wo.dtype)
        dya_ref[...] = _dot_nt(dy, wa)
        dyb_ref[...] = _dot_nt(dy, wb)
        dwo_ref[0:GLA_WIDTH, :] += _dot_tn(ya16, dy)
        dwo_ref[GLA_WIDTH:, :] += _dot_tn(yb16, dy)

    row = lambda i: (i, 0)
    const = lambda i: (0, 0)
    per_ex = pl.BlockSpec((1, 1, D_MODEL), lambda i: (i // tpe, 0, 0))
    return pl.pallas_call(
        body, name="out_block", grid=(n // tm,),
        in_specs=[pl.BlockSpec((tm, D_MODEL), row), pl.BlockSpec((tm, D_MODEL), row),
                  pl.BlockSpec((tm, 512), row), pl.BlockSpec((tm, 512), row), per_ex,
                  pl.BlockSpec((D_MODEL, D_MODEL), const), pl.BlockSpec((1, D_MODEL), const),
                  pl.BlockSpec((1, D_MODEL), const)],
        out_specs=[pl.BlockSpec((tm, D_MODEL), row), pl.BlockSpec((tm, 512), row), pl.BlockSpec((tm, 512), row),
                   pl.BlockSpec((D_MODEL, D_MODEL), const), per_ex,
                   pl.BlockSpec((1, D_MODEL), const), pl.BlockSpec((1, D_MODEL), const),
                   pl.BlockSpec((1, 1), const)],
        out_shape=[jax.ShapeDtypeStruct((n, D_MODEL), F32), jax.ShapeDtypeStruct((n, 512), F32),
                   jax.ShapeDtypeStruct((n, 512), F32), jax.ShapeDtypeStruct((D_MODEL, D_MODEL), F32),
                   jax.ShapeDtypeStruct((bl, 1, D_MODEL), F32), jax.ShapeDtypeStruct((1, D_MODEL), F32),
                   jax.ShapeDtypeStruct((1, D_MODEL), F32), jax.ShapeDtypeStruct((1, 1), F32)],
        compiler_params=_params("arbitrary"),
    )(x2, tgt2, ya, yb, g1p3, wo, lnw, lnb)


def _proj_bwd_x(ds, ws, x2, dz, sc3, seq, tm=512):
    n = x2.shape[0]
    tpe = seq // tm
    bl = n // seq

    def body(da_ref, db_ref, dc_ref, dd1_ref, dd2_ref, wa_ref, wb_ref, wc_ref, wd_ref, x_ref, dz_ref, sc_ref,
             gx_ref, dsh_ref, dsc_ref):
        i = pl.program_id(0)

        @pl.when(i % tpe == 0)
        def _():
            dsh_ref[...] = jnp.zeros_like(dsh_ref)
            dsc_ref[...] = jnp.zeros_like(dsc_ref)

        cdt = ws[0].dtype
        dh = _dot(da_ref[...].astype(cdt), wa_ref[...])
        dh += _dot(db_ref[...].astype(cdt), wb_ref[...])
        dh += _dot(dc_ref[...].astype(cdt), wc_ref[...])
        dh += _dot((dd1_ref[...] + dd2_ref[...]).astype(cdt), wd_ref[...])
        gx_ref[...] = dh * sc_ref[0] + ALPHA * dz_ref[...]
        dsh_ref[0] += jnp.sum(dh, axis=0, keepdims=True)
        dsc_ref[0] += jnp.sum(dh * x_ref[...], axis=0, keepdims=True)

    row = lambda i: (i, 0)
    const = lambda i: (0, 0)
    per_ex = pl.BlockSpec((1, 1, D_MODEL), lambda i: (i // tpe, 0, 0))
    da, db, dc, (dd1, dd2) = ds
    return pl.pallas_call(
        body, name="proj_bwd_x", grid=(n // tm,),
        in_specs=[pl.BlockSpec((tm, d.shape[1]), row) for d in (da, db, dc, dd1, dd2)]
        + [pl.BlockSpec(w.shape, const) for w in ws]
        + [pl.BlockSpec((tm, D_MODEL), row), pl.BlockSpec((tm, D_MODEL), row), per_ex],
        out_specs=[pl.BlockSpec((tm, D_MODEL), row), per_ex, per_ex],
        out_shape=[jax.ShapeDtypeStruct((n, D_MODEL), F32), jax.ShapeDtypeStruct((bl, 1, D_MODEL), F32),
                   jax.ShapeDtypeStruct((bl, 1, D_MODEL), F32)],
        compiler_params=_params("arbitrary"),
    )(da, db, dc, dd1, dd2, *ws, x2, dz, sc3)


def _proj_bwd_w(x2, sc3, sh3, ds, seq, cdt, name, tm=1024):
    n = x2.shape[0]
    tpe = seq // tm
    flat, groups = [], []
    for d in ds:
        parts = d if isinstance(d, tuple) else (d,)
        groups.append(len(parts))
        flat.extend(parts)
    nin = len(flat)

    def body(x_ref, sc_ref, sh_ref, *refs):
        i = pl.program_id(0)
        outs = refs[nin:]

        @pl.when(i == 0)
        def _():
            for o in outs:
                o[...] = jnp.zeros_like(o)

        h = (x_ref[...] * sc_ref[0] + sh_ref[0]).astype(cdt)
        pos = 0
        for o, cnt in zip(outs, groups):
            d = refs[pos][...]
            for extra in refs[pos + 1:pos + cnt]:
                d = d + extra[...]
            pos += cnt
            o[...] += _dot_tn(d.astype(cdt), h)

    row = lambda i: (i, 0)
    const = lambda i: (0, 0)
    per_ex = pl.BlockSpec((1, 1, D_MODEL), lambda i: (i // tpe, 0, 0))
    widths = [(d[0] if isinstance(d, tuple) else d).shape[1] for d in ds]
    return pl.pallas_call(
        body, name=name, grid=(n // tm,),
        in_specs=[pl.BlockSpec((tm, D_MODEL), row), per_ex, per_ex]
        + [pl.BlockSpec((tm, d.shape[1]), row) for d in flat],
        out_specs=[pl.BlockSpec((w, D_MODEL), const) for w in widths],
        out_shape=[jax.ShapeDtypeStruct((w, D_MODEL), F32) for w in widths],
        compiler_params=_params("arbitrary"),
    )(x2, sc3, sh3, *flat)


def _mod_block(c_all, w_ada_sh, b_blk):
    def body(c_ref, w_ref, b_ref, o_ref):
        o_ref[...] = _dot(c_ref[...], w_ref[...]) + b_ref[...]

    return pl.pallas_call(
        body, name="mod_block",
        out_shape=jax.ShapeDtypeStruct((c_all.shape[0], w_ada_sh.shape[1]), F32),
        compiler_params=pltpu.CompilerParams(vmem_limit_bytes=VMEM_LIMIT),
    )(c_all, w_ada_sh, b_blk)


def _ada_grads(c_all, dmod_all, dmod_blk):
    def body(c_ref, da_ref, db_ref, gw_ref, gb_ref):
        gw_ref[...] = _dot_tn(c_ref[...], db_ref[...])
        gb_ref[...] = jnp.sum(da_ref[...], axis=0, keepdims=True)

    return pl.pallas_call(
        body, name="ada_grads",
        out_shape=[jax.ShapeDtypeStruct((c_all.shape[1], dmod_blk.shape[1]), F32),
                   jax.ShapeDtypeStruct((1, dmod_all.shape[1]), F32)],
        compiler_params=pltpu.CompilerParams(vmem_limit_bytes=VMEM_LIMIT),
    )(c_all, dmod_all, dmod_blk)


def _sum_leading(parts, name):
    def body(p_ref, o_ref):
        acc = p_ref[0]
        for d in range(1, parts.shape[0]):
            acc = acc + p_ref[d]
        o_ref[...] = acc

    return pl.pallas_call(
        body, name=name, out_shape=jax.ShapeDtypeStruct(parts.shape[1:], F32),
        compiler_params=pltpu.CompilerParams(vmem_limit_bytes=VMEM_LIMIT),
    )(parts)


ELEMENTWISE_BLOCK_BYTES = 2 * 1024 * 1024


def _tile2d(rows, cols, row_align=8):
    if rows * cols * 4 <= ELEMENTWISE_BLOCK_BYTES:
        return rows, cols
    fits = [t for t in range(row_align, rows, row_align) if rows % t == 0 and t * cols * 4 <= ELEMENTWISE_BLOCK_BYTES]
    if fits:
        return fits[-1], cols
    fits = [t for t in range(128, cols, 128) if cols % t == 0 and rows * t * 4 <= ELEMENTWISE_BLOCK_BYTES]
    assert fits, (rows, cols)
    return rows, fits[-1]


def _add_n(arrs, name, out_dtypes=(F32,)):
    rows, cols = arrs[0].shape
    narrow = any(jnp.dtype(dt).itemsize < 4 for dt in tuple(out_dtypes) + tuple(a.dtype for a in arrs))
    tr, tc = _tile2d(rows, cols, 16 if narrow else 8)
    n_in = len(arrs)

    def body(*refs):
        acc = refs[0][...].astype(F32)
        for r in refs[1:n_in]:
            acc = acc + r[...].astype(F32)
        for o in refs[n_in:]:
            o[...] = acc.astype(o.dtype)

    spec = pl.BlockSpec((tr, tc), lambda i, j: (i, j))
    return pl.pallas_call(
        body, name=name, grid=(rows // tr, cols // tc), in_specs=[spec] * n_in, out_specs=[spec] * len(out_dtypes),
        out_shape=[jax.ShapeDtypeStruct((rows, cols), dt) for dt in out_dtypes],
        compiler_params=_params("parallel", "parallel"),
    )(*arrs)


def _chip_sum_blocks(a, b, per, blocks, name, chunk=128):
    rows, cols = a.shape
    padded = -(-per // 16) * 16
    assert rows >= (blocks - 1) * per + padded, (rows, per, blocks)

    def body(a_ref, b_ref, o_ref, o16_ref):
        for j in range(blocks):
            for r0 in range(0, padded, chunk):
                n_rows = min(chunk, padded - r0)
                src = pl.ds(j * per + r0, n_rows)
                s = a_ref[src, :] + b_ref[src, :]
                if per - r0 < n_rows:
                    s = jnp.where(_iota((n_rows, 1), 0) < per - r0, s, 0.0)
                o_ref[j, r0:r0 + n_rows, :] = s
                o16_ref[j, r0:r0 + n_rows, :] = s.astype(BF16)

    return pl.pallas_call(
        body, name=name,
        out_shape=[jax.ShapeDtypeStruct((blocks, padded, cols), F32), jax.ShapeDtypeStruct((blocks, padded, cols), BF16)],
        compiler_params=pltpu.CompilerParams(vmem_limit_bytes=VMEM_LIMIT),
    )(a, b)


GRAD_PAD_ROWS = 16


def _adamw(w, g, m, v, name):
    rows, cols = w.shape
    tr, tc = _tile2d(rows, cols)
    c1 = 1.0 / (1.0 - ADAM_B1 ** ADAM_STEP)
    c2 = 1.0 / (1.0 - ADAM_B2 ** ADAM_STEP)

    def body(w_ref, g_ref, m_ref, v_ref, d_ref, nm_ref, nv_ref):
        gg = g_ref[...]
        nm = ADAM_B1 * m_ref[...] + (1.0 - ADAM_B1) * gg
        nv = ADAM_B2 * v_ref[...] + (1.0 - ADAM_B2) * (gg * gg)
        nm_ref[...] = nm
        nv_ref[...] = nv
        d_ref[...] = -ADAM_LR * ((nm * c1) / (jnp.sqrt(nv * c2) + ADAM_EPS) + ADAM_WD * w_ref[...])

    spec = pl.BlockSpec((tr, tc), lambda i, j: (i, j))
    shp = jax.ShapeDtypeStruct((rows, cols), F32)
    return pl.pallas_call(
        body, name=name, grid=(rows // tr, cols // tc), in_specs=[spec] * 4, out_specs=[spec] * 3,
        out_shape=[shp, shp, shp], compiler_params=_params("parallel", "parallel"),
    )(w, g, m, v)


def _coords():
    return lax.axis_index("x"), lax.axis_index("y"), lax.axis_index("c")


def _all_gather8(blk, name):
    m_per, n = blk.shape

    def body(x_ref, out_ref, send_sems, recv_sems, local_sem):
        x, y, c = _coords()
        me, sibling = (x, y, c), (x, y, 1 - c)
        chips = [(1 - x, y), (x, 1 - y), (1 - x, 1 - y)]

        def rows(px, py, pc):
            return out_ref.at[pl.ds((4 * px + 2 * py + pc) * m_per, m_per), :]

        def copy(k, block, to, src=None):
            return pltpu.make_async_remote_copy(
                src_ref=rows(*block) if src is None else src, dst_ref=rows(*block),
                send_sem=send_sems.at[k], recv_sem=recv_sems.at[k], device_id=to, device_id_type=MESH)

        mine = pltpu.make_async_copy(x_ref, rows(*me), local_sem)
        mine.start()
        first = [copy(0, me, sibling, src=x_ref)]
        first += [copy(1 + j, me, (*chip, c), src=x_ref) for j, chip in enumerate(chips)]
        for cp in first:
            cp.start()
        passed = [copy(4 + j, (*chip, c), sibling) for j, chip in enumerate(chips)]
        for j, chip in enumerate(chips):
            copy(1 + j, (*chip, c), me).wait_recv()
            passed[j].start()
        copy(0, sibling, me).wait_recv()
        for j, chip in enumerate(chips):
            copy(4 + j, (*chip, 1 - c), me).wait_recv()
        for cp in first + passed:
            cp.wait_send()
        mine.wait()

    return pl.pallas_call(
        body, name=name,
        out_shape=jax.ShapeDtypeStruct((8 * m_per, n), blk.dtype),
        in_specs=[pl.BlockSpec(memory_space=pltpu.VMEM)],
        out_specs=pl.BlockSpec(memory_space=pltpu.VMEM),
        scratch_shapes=[pltpu.SemaphoreType.DMA((7,)), pltpu.SemaphoreType.DMA((7,)), pltpu.SemaphoreType.DMA],
        compiler_params=pltpu.CompilerParams(vmem_limit_bytes=VMEM_LIMIT),
    )(blk)


def _chip_gather(shards, split, name):
    k_arr = len(shards)

    def body(*refs):
        srcs, dsts = refs[:k_arr], refs[k_arr:2 * k_arr]
        send_sems, recv_sems, fwd_send_sems, fwd_recv_sems, local_sems = refs[2 * k_arr:]
        x, y, c = _coords()
        peers = [(1 - x, y, c), (x, 1 - y, c), (1 - x, 1 - y, c)]
        sibling = (x, y, 1 - c)
        me_chip = 2 * x + y

        def part(ref, a, core):
            if not split[a]:
                return ref
            half = shards[a].shape[1] // 2
            return ref.at[:, pl.ds(core * half, half)]

        def ici(a, j, src_chip, dst_dev):
            return pltpu.make_async_remote_copy(
                src_ref=part(srcs[a], a, c), dst_ref=part(dsts[a].at[src_chip], a, c),
                send_sem=send_sems.at[a, j], recv_sem=recv_sems.at[a, j], device_id=dst_dev, device_id_type=MESH)

        def d2d(a, j, src_chip, core):
            return pltpu.make_async_remote_copy(
                src_ref=part(dsts[a].at[src_chip], a, core), dst_ref=part(dsts[a].at[src_chip], a, core),
                send_sem=fwd_send_sems.at[a, j], recv_sem=fwd_recv_sems.at[a, j],
                device_id=sibling, device_id_type=MESH)

        local = [pltpu.make_async_copy(srcs[a], dsts[a].at[me_chip], local_sems.at[a]) for a in range(k_arr)]
        for cp in local:
            cp.start()
        sends = [ici(a, j, me_chip, peer) for a in range(k_arr) for j, peer in enumerate(peers)]
        for cp in sends:
            cp.start()
        forwards = []
        for a in range(k_arr):
            for j, peer in enumerate(peers):
                peer_chip = 2 * peer[0] + peer[1]
                ici(a, j, peer_chip, peer).wait_recv()
                if split[a]:
                    forwards.append(d2d(a, j, peer_chip, c))
                    forwards[-1].start()
        for a in range(k_arr):
            for j, peer in enumerate(peers):
                if split[a]:
                    d2d(a, j, 2 * peer[0] + peer[1], 1 - c).wait_recv()
        for cp in sends + forwards:
            cp.wait_send()
        for cp in local:
            cp.wait()

    any_spec = pl.BlockSpec(memory_space=pl.ANY)
    return pl.pallas_call(
        body, name=name,
        out_shape=[jax.ShapeDtypeStruct((4,) + s.shape, s.dtype) for s in shards],
        in_specs=[any_spec] * k_arr, out_specs=[any_spec] * k_arr,
        scratch_shapes=[pltpu.SemaphoreType.DMA((k_arr, 3))] * 4 + [pltpu.SemaphoreType.DMA((k_arr,))],
    )(*shards)


def _chip_scatter(pieces, name):
    k_arr = len(pieces)

    def body(*refs):
        srcs, dsts = refs[:k_arr], refs[k_arr:2 * k_arr]
        send_sems, recv_sems = refs[2 * k_arr:]
        x, y, c = _coords()
        peers = [(1 - x, y, c), (x, 1 - y, c), (1 - x, 1 - y, c)]
        copies = []
        for a in range(k_arr):
            for j, peer in enumerate(peers):
                copies.append(pltpu.make_async_remote_copy(
                    src_ref=srcs[a].at[2 * peer[0] + peer[1]], dst_ref=dsts[a].at[j],
                    send_sem=send_sems.at[a, j], recv_sem=recv_sems.at[a, j], device_id=peer, device_id_type=MESH))
        for cp in copies:
            cp.start()
        for cp in copies:
            cp.wait_recv()
        for cp in copies:
            cp.wait_send()

    any_spec = pl.BlockSpec(memory_space=pl.ANY)
    return pl.pallas_call(
        body, name=name,
        out_shape=[jax.ShapeDtypeStruct((3,) + p.shape[1:], p.dtype) for p in pieces],
        in_specs=[any_spec] * k_arr, out_specs=[any_spec] * k_arr,
        scratch_shapes=[pltpu.SemaphoreType.DMA((k_arr, 3)), pltpu.SemaphoreType.DMA((k_arr, 3))],
    )(*pieces)


def _sibling_swap(arrs, name):
    k_arr = len(arrs)

    def body(*refs):
        srcs, dsts = refs[:k_arr], refs[k_arr:2 * k_arr]
        send_sems, recv_sems = refs[2 * k_arr:]
        x, y, c = _coords()
        copies = [pltpu.make_async_remote_copy(
            src_ref=srcs[a], dst_ref=dsts[a], send_sem=send_sems.at[a], recv_sem=recv_sems.at[a],
            device_id=(x, y, 1 - c), device_id_type=MESH) for a in range(k_arr)]
        for cp in copies:
            cp.start()
        for cp in copies:
            cp.wait_recv()
        for cp in copies:
            cp.wait_send()

    any_spec = pl.BlockSpec(memory_space=pl.ANY)
    return pl.pallas_call(
        body, name=name,
        out_shape=[jax.ShapeDtypeStruct(a.shape, a.dtype) for a in arrs],
        in_specs=[any_spec] * k_arr, out_specs=[any_spec] * k_arr,
        scratch_shapes=[pltpu.SemaphoreType.DMA((k_arr,)), pltpu.SemaphoreType.DMA((k_arr,))],
    )(*arrs)


def _split_w_in(w_in_t):
    wa = jnp.concatenate([w_in_t[0:1024], w_in_t[1040:1552]], axis=0)
    wb = w_in_t[1552:3088]
    wc = w_in_t[3096:3608]
    wd = jnp.concatenate([w_in_t[1024:1040], w_in_t[3088:3096],
                          jnp.zeros((128 - SMALL_USED, w_in_t.shape[1]), w_in_t.dtype)], axis=0)
    return wa, wb, wc, wd


def _merge_dw_in(dwa, dwb, dwc, dwd):
    return jnp.concatenate([dwa[0:1024], dwd[0:GLA_RANK], dwa[1024:1536], dwb, dwd[GLA_RANK:SMALL_USED], dwc,
                            jnp.zeros((GRAD_PAD_ROWS, dwa.shape[1]), dwa.dtype)], axis=0)


def _local_step(x, mod, w_in16, w_out16, gla_wg, gla_bg, gla_nw, conv_w, a_log, dt_bias, gdn_nw, ln_w, ln_b, tgt):
    bl, seq, _ = x.shape
    n = bl * seq
    x2 = x.reshape(n, D_MODEL)
    tgt2 = tgt.reshape(n, D_MODEL)
    sh3 = mod[:, None, 0:D_MODEL]
    sc3 = 1.0 + mod[:, None, D_MODEL:2 * D_MODEL]
    g1p3 = 1.0 + mod[:, None, 2 * D_MODEL:]
    ws = _split_w_in(w_in16)
    wg = jnp.concatenate([gla_wg, jnp.zeros((128 - GLA_RANK, GLA_QK), F32)], axis=0)
    cw8 = jnp.concatenate([conv_w, jnp.zeros((8 - CONV_K, conv_w.shape[1]), F32)], axis=0)
    alog_v = jnp.zeros((1, 128), F32).at[:, LANE_A:LANE_A + GDN_HEADS].set(a_log)
    dtb_v = jnp.zeros((1, 128), F32).at[:, LANE_A:LANE_A + GDN_HEADS].set(dt_bias)

    pa, pb, pc, pd = _proj_fwd(x2, sc3, sh3, ws, seq)
    ya, st_a = _gla_fwd(pa, pd, wg, gla_bg, gla_nw, bl, seq)
    qkv, gb, conv_out = _gdn_pre_fwd(pb, pd, cw8, alog_v, dtb_v, bl, seq)
    yb, st_b = _gdn_fwd(qkv, gb, pc, gdn_nw, bl, seq)
    dz, dya, dyb, d_wo, d_gate, d_lnw, d_lnb, loss = _out_block(x2, tgt2, ya, yb, g1p3, w_out16, ln_w, ln_b, seq)
    da, dd1, d_wg, d_bg, d_nwa = _gla_bwd(pa, pd, st_a, dya, wg, gla_bg, gla_nw, bl, seq)
    dqkv, dc, dgb, d_nwb = _gdn_bwd(qkv, gb, pc, st_b, dyb, gdn_nw, bl, seq)
    db, dd2, d_cw8, d_alog, d_dtb = _gdn_pre_bwd(pb, conv_out, pd, dqkv, dgb, cw8, alog_v, dtb_v, bl, seq)
    gx, d_sh, d_sc = _proj_bwd_x((da, db, dc, (dd1, dd2)), ws, x2, dz, sc3, seq)
    (dwa,) = _proj_bwd_w(x2, sc3, sh3, [da], seq, w_in16.dtype, "proj_bwd_w_a")
    dwb, dwc, dwd = _proj_bwd_w(x2, sc3, sh3, [db, dc, (dd1, dd2)], seq, w_in16.dtype, "proj_bwd_w_bcd")
    grads = dict(
        w_in=_merge_dw_in(dwa, dwb, dwc, dwd),
        w_out=d_wo,
        gla_w_gate_up=d_wg[0:GLA_RANK, :],
        gla_b_gate=d_bg,
        gla_norm_w=d_nwa,
        gdn_conv_w=d_cw8[0:CONV_K, :],
        gdn_a_log=d_alog[:, LANE_A:LANE_A + GDN_HEADS],
        gdn_dt_bias=d_dtb[:, LANE_A:LANE_A + GDN_HEADS],
        gdn_norm_w=d_nwb,
        ln_w=d_lnw,
        ln_b=d_lnb,
        mod=jnp.concatenate([d_sh[:, 0, :], d_sc[:, 0, :], d_gate[:, 0, :]], axis=1),
    )
    return loss, gx.reshape(bl, seq, D_MODEL), grads


_SMALL = (("gla_b_gate", 256), ("gla_norm_w", 128), ("gdn_a_log", 4), ("gdn_dt_bias", 4), ("gdn_norm_w", 128),
          ("ln_w", 1024), ("ln_b", 1024), ("gla_w_gate_up", 16 * 256), ("gdn_conv_w", 4 * 1536), ("loss", 1),
          ("mod", 2 * 3072))


def _pack_small(grads):
    flat = jnp.concatenate([grads[k].reshape(-1) for k, _ in _SMALL])
    total = sum(sz for _, sz in _SMALL)
    rows = -(-total // 1024) * 8
    return jnp.concatenate([flat, jnp.zeros((rows * 128 - total,), F32)]).reshape(rows, 128)


def _unpack_small(flat):
    out, pos = {}, 0
    for k, sz in _SMALL:
        out[k] = flat[pos:pos + sz]
        pos += sz
    return out


def kernel(x, c, w_ada, b_ada, w_in, gla_w_gate_up, gla_b_gate, gla_norm_w, gdn_conv_w, gdn_a_log, gdn_dt_bias, gdn_norm_w, w_out, ln_w, ln_b, loss_target, m_w_ada, m_b_ada, m_w_in, m_gla_w_gate_up, m_gla_b_gate, m_gla_norm_w, m_gdn_conv_w, m_gdn_a_log, m_gdn_dt_bias, m_gdn_norm_w, m_w_out, m_ln_w, m_ln_b, v_w_ada, v_b_ada, v_w_in, v_gla_w_gate_up, v_gla_b_gate, v_gla_norm_w, v_gdn_conv_w, v_gdn_a_log, v_gdn_dt_bias, v_gdn_norm_w, v_w_out, v_ln_w, v_ln_b):
    ix, iy, ic = _coords()
    chip = 2 * ix + iy
    dev = 4 * ix + 2 * iy + ic
    bl = x.shape[0]
    ndev = 8

    c_all = _all_gather8(c.reshape(8, -1), "gather_c").reshape(ndev * bl, D_MODEL)
    ada_cols = w_ada.shape[2]
    b_blk = lax.dynamic_slice_in_dim(b_ada, chip * ada_cols, ada_cols, axis=1)
    mod_blk = _mod_block(c_all, w_ada[0], b_blk)
    mod_g = _all_gather8(mod_blk, "gather_mod").reshape(ndev, ndev * bl, ada_cols)
    mod_all = jnp.concatenate([mod_g[2 * j] for j in range(4)], axis=1)
    mod = lax.dynamic_slice_in_dim(mod_all, dev * bl, bl, axis=0)

    w_in_g, w_out_g, wg_g, cw_g = _chip_gather(
        [jnp.transpose(w_in[0]).astype(BF16), w_out[0].astype(BF16), gla_w_gate_up[0], gdn_conv_w[0]],
        [True, True, False, False], "gather_weights")
    w_in16 = w_in_g.reshape(IN_COLS, D_MODEL)
    w_out16 = w_out_g.reshape(D_MODEL, D_MODEL)
    gla_wg = jnp.concatenate([wg_g[j] for j in range(4)], axis=1)
    conv_w = jnp.concatenate([cw_g[j] for j in range(4)], axis=1)

    loss, grad_x, gr = _local_step(x, mod, w_in16, w_out16, gla_wg, gla_b_gate, gla_norm_w, conv_w,
                                   gdn_a_log, gdn_dt_bias, gdn_norm_w, ln_w, ln_b, loss_target)

    gr["loss"] = loss
    packed = _pack_small(gr)
    prow = packed.shape[0]
    gathered = _all_gather8(packed, "gather_small").reshape(ndev, prow, 128)
    small = _unpack_small(_sum_leading(gathered, "sum_small").reshape(-1))
    loss = small["loss"][0]
    mod_rows = gathered.reshape(ndev, prow * 128)[:, sum(sz for _, sz in _SMALL[:-1]):][:, :bl * 3 * D_MODEL]
    dmod_all = mod_rows.reshape(ndev * bl, 3 * D_MODEL)
    dmod_blk = lax.dynamic_slice_in_dim(dmod_all, chip * ada_cols, ada_cols, axis=1)
    g_w_ada, g_b_ada = _ada_grads(c_all, dmod_all, dmod_blk)
    wg_cols = gla_w_gate_up.shape[2]
    g_wg = lax.dynamic_slice_in_dim(small["gla_w_gate_up"].reshape(GLA_RANK, GLA_QK), chip * wg_cols, wg_cols, axis=1)
    cw_cols = gdn_conv_w.shape[2]
    g_cw = lax.dynamic_slice_in_dim(small["gdn_conv_w"].reshape(CONV_K, 3 * GDN_WIDTH), chip * cw_cols, cw_cols, axis=1)

    in_feats = w_in.shape[2]
    out_rows = w_out.shape[1]
    p_in = gr["w_in"]
    p_out = gr["w_out"].reshape(4, out_rows, D_MODEL)
    h_in, h_out = D_MODEL // 2, out_rows // 2
    mine_in = lax.dynamic_slice_in_dim(p_in, ic * h_in, h_in, axis=1)
    mine_out = lax.dynamic_slice_in_dim(p_out, ic * h_out, h_out, axis=1)
    theirs_in = lax.dynamic_slice_in_dim(p_in, (1 - ic) * h_in, h_in, axis=1)
    theirs_out = lax.dynamic_slice_in_dim(p_out, (1 - ic) * h_out, h_out, axis=1)
    got_in, got_out = _sibling_swap([theirs_in, theirs_out], "swap_halves")
    chip_in, chip_in16 = _chip_sum_blocks(mine_in, got_in, in_feats, 4, "chip_sum_in")
    chip_out, chip_out16 = _add_n([mine_out.reshape(4 * h_out, D_MODEL), got_out.reshape(4 * h_out, D_MODEL)],
                                  "chip_sum_out", (F32, BF16))
    chip_out = chip_out.reshape(4, h_out, D_MODEL)
    rs_in, rs_out = _chip_scatter([chip_in16, chip_out16.reshape(4, h_out, D_MODEL)], "scatter_grads")
    own_in = lax.dynamic_index_in_dim(chip_in, chip, axis=0, keepdims=False)
    own_out = lax.dynamic_index_in_dim(chip_out, chip, axis=0, keepdims=False)
    (half_in,) = _add_n([own_in, rs_in[0], rs_in[1], rs_in[2]], "reduce_in")
    (half_out,) = _add_n([own_out, rs_out[0], rs_out[1], rs_out[2]], "reduce_out")
    sib_in, sib_out = _sibling_swap([half_in, half_out], "swap_result")
    g_w_in_t = jnp.where(ic == 0, jnp.concatenate([half_in, sib_in], axis=1),
                         jnp.concatenate([sib_in, half_in], axis=1))[0:in_feats]
    g_w_out = jnp.where(ic == 0, jnp.concatenate([half_out, sib_out], axis=0),
                        jnp.concatenate([sib_out, half_out], axis=0))

    grads = dict(
        w_ada=g_w_ada[None], b_ada=g_b_ada, w_in=g_w_in_t, gla_w_gate_up=g_wg[None],
        gla_b_gate=small["gla_b_gate"].reshape(1, -1), gla_norm_w=small["gla_norm_w"].reshape(1, -1),
        gdn_conv_w=g_cw[None], gdn_a_log=small["gdn_a_log"].reshape(1, -1),
        gdn_dt_bias=small["gdn_dt_bias"].reshape(1, -1), gdn_norm_w=small["gdn_norm_w"].reshape(1, -1),
        w_out=g_w_out[None], ln_w=small["ln_w"].reshape(1, -1), ln_b=small["ln_b"].reshape(1, -1))
    weights = dict(w_ada=w_ada, b_ada=b_ada, w_in=w_in, gla_w_gate_up=gla_w_gate_up, gla_b_gate=gla_b_gate,
                   gla_norm_w=gla_norm_w, gdn_conv_w=gdn_conv_w, gdn_a_log=gdn_a_log, gdn_dt_bias=gdn_dt_bias,
                   gdn_norm_w=gdn_norm_w, w_out=w_out, ln_w=ln_w, ln_b=ln_b)
    m_in = dict(w_ada=m_w_ada, b_ada=m_b_ada, w_in=m_w_in, gla_w_gate_up=m_gla_w_gate_up, gla_b_gate=m_gla_b_gate,
                gla_norm_w=m_gla_norm_w, gdn_conv_w=m_gdn_conv_w, gdn_a_log=m_gdn_a_log, gdn_dt_bias=m_gdn_dt_bias,
                gdn_norm_w=m_gdn_norm_w, w_out=m_w_out, ln_w=m_ln_w, ln_b=m_ln_b)
    v_in = dict(w_ada=v_w_ada, b_ada=v_b_ada, w_in=v_w_in, gla_w_gate_up=v_gla_w_gate_up, gla_b_gate=v_gla_b_gate,
                gla_norm_w=v_gla_norm_w, gdn_conv_w=v_gdn_conv_w, gdn_a_log=v_gdn_a_log, gdn_dt_bias=v_gdn_dt_bias,
                gdn_norm_w=v_gdn_norm_w, w_out=v_w_out, ln_w=v_ln_w, ln_b=v_ln_b)
    names = list(weights)
    delta, new_m, new_v = {}, {}, {}
    for nm in names:
        shp = weights[nm].shape
        if nm == "w_in":
            to2d = lambda t: jnp.transpose(t[0])
            from2d = lambda t: jnp.transpose(t)[None]
            g2d = grads[nm]
        else:
            to2d = lambda t: t.reshape(-1, shp[-1])
            from2d = lambda t: t.reshape(shp)
            g2d = to2d(grads[nm])
        d, a, b = _adamw(to2d(weights[nm]), g2d, to2d(m_in[nm]), to2d(v_in[nm]), "adamw_" + nm)
        delta[nm], new_m[nm], new_v[nm] = from2d(d), from2d(a), from2d(b)
        grads[nm] = from2d(g2d)
    return (loss, grad_x, *[grads[k] for k in names], *[delta[k] for k in names],
            *[new_m[k] for k in names], *[new_v[k] for k in names])
```

```python
import functools

import jax
import jax.numpy as jnp
from jax import lax
from jax.experimental import pallas as pl
from jax.experimental.pallas import tpu as pltpu

F32 = jnp.float32
BF16 = jnp.bfloat16
HI = lax.Precision.HIGH
INV_PREC = None
MESH = pl.DeviceIdType.MESH

D_MODEL = 1024
GLA_HEADS = 4
GLA_DK = 64
GLA_DV = 128
GLA_QK = 256
GLA_WIDTH = 512
GLA_RANK = 16
GLA_GATE_NORM = 16.0
GDN_HEADS = 4
GDN_DK = 128
GDN_WIDTH = 512
CONV_K = 4
CHUNK = 64
LN_EPS = 1e-5
RMS_EPS = 1e-6
ALPHA = 2.0 ** 0.25
IN_COLS = 3608

LANE_A = GLA_RANK
LANE_B = GLA_RANK + GDN_HEADS
SMALL_USED = GLA_RANK + 2 * GDN_HEADS

ADAM_LR = 0.001
ADAM_B1 = 0.9
ADAM_B2 = 0.999
ADAM_EPS = 1e-08
ADAM_WD = 0.01
ADAM_STEP = 10

VMEM_LIMIT = 56 * 1024 * 1024


def _iota(shape, dim):
    return lax.broadcasted_iota(jnp.int32, shape, dim)


def _dot(a, b, prec=None):
    return lax.dot_general(a, b, (((1,), (0,)), ((), ())), precision=prec, preferred_element_type=F32)


def _dot_nt(a, b, prec=None):
    return lax.dot_general(a, b, (((1,), (1,)), ((), ())), precision=prec, preferred_element_type=F32)


def _dot_tn(a, b, prec=None):
    return lax.dot_general(a, b, (((0,), (0,)), ((), ())), precision=prec, preferred_element_type=F32)


def _log_sigmoid(z):
    return jnp.minimum(z, 0.0) - jnp.log1p(jnp.exp(-jnp.abs(z)))


def _softplus(z):
    return jnp.maximum(z, 0.0) + jnp.log1p(jnp.exp(-jnp.abs(z)))


def _silu(z):
    return z * jax.nn.sigmoid(z)


def _rms_gate(o, nw, og):
    return o * lax.rsqrt(jnp.mean(o * o, axis=-1, keepdims=True) + RMS_EPS) * nw * _silu(og)


def _params(*sem):
    return pltpu.CompilerParams(dimension_semantics=sem, vmem_limit_bytes=VMEM_LIMIT)


GLA_PAIRS = GLA_HEADS // 2


def _gla_chunk(qs, ks, lrs, vs, ogs, ss, wgs, bgs, nw):
    c = qs[0].shape[0]
    n_ep = len(ss)
    n_ex = n_ep // GLA_PAIRS
    n_chunks = len(qs) // n_ep
    pair_units = [(i // n_ep * n_ex + i % n_ep // GLA_PAIRS, i % GLA_PAIRS) for i in range(len(qs))]
    head_units = [(i // GLA_HEADS * GLA_PAIRS + i % GLA_HEADS // 2, i % 2) for i in range(len(vs))]
    row, col = _iota((c, c), 0), _iota((c, c), 1)
    causal = row >= col
    first_half = (_iota((c, 1), 0) < c // 2).astype(F32)
    lane = _iota((1, 128), 1)
    masks = [(lane < GLA_DK).astype(F32), (lane >= GLA_DK).astype(F32)]
    gs = [_log_sigmoid(_dot(lrs[ce], wgs[p]) + bgs[p]) * (1.0 / GLA_GATE_NORM) for ce, p in pair_units]
    bs = [_dot(causal.astype(F32), g, HI) for g in gs]
    b_ref = [jnp.sum(g * first_half, axis=0, keepdims=True) for g in gs]
    b_last = [jnp.sum(g, axis=0, keepdims=True) for g in gs]
    qsc = [q * (GLA_DK ** -0.5) for q in qs]
    qe = [q * jnp.exp(b - br) for q, b, br in zip(qsc, bs, b_ref)]
    ke = [k * jnp.exp(br - b) for k, b, br in zip(ks, bs, b_ref)]
    qb = [q * jnp.exp(b) for q, b in zip(qsc, bs)]
    kd = [k * jnp.exp(bl_ - b) for k, b, bl_ in zip(ks, bs, b_last)]
    decay = [jnp.exp(bl_) for bl_ in b_last]
    att = [jnp.where(causal, _dot_nt(qe[u] * masks[half], ke[u]), 0.0) for u, half in head_units]
    o_intra = [_dot(a, v) for a, v in zip(att, vs)]
    qbm = [qb[u] * masks[half] for u, half in head_units]
    kdm = [kd[u] * masks[half] for u, half in head_units]
    ys = []
    for r in range(n_chunks):
        heads_r = range(r * n_ex * GLA_HEADS, (r + 1) * n_ex * GLA_HEADS)
        o_inter = [_dot_nt(qbm[i], ss[head_units[i][0] - r * n_ep]) for i in heads_r]
        upd = [_dot_tn(vs[i], kdm[i]) for i in heads_r]
        ss = [s * decay[r * n_ep + j] + upd[2 * j] + upd[2 * j + 1] for j, s in enumerate(ss)]
        ys += [_rms_gate(o_intra[i] + oi, nw, ogs[i]) for i, oi in zip(heads_r, o_inter)]
    return ys, ss


def _unit_lower_inverse_chain(a_list):
    c = a_list[0].shape[0]
    eye = (_iota((c, c), 0) == _iota((c, c), 1)).astype(F32)
    ps = [-a for a in a_list]
    ts = [eye + p for p in ps]
    levels = max(c.bit_length() - 2, 0)
    if levels:
        ps = [_dot(p, p, INV_PREC) for p in ps]
    for level in range(levels):
        last = level == levels - 1
        both = [_dot(t if last else jnp.concatenate([t, p], axis=0), p, INV_PREC) for t, p in zip(ts, ps)]
        ts = [t + m[0:c] for t, m in zip(ts, both)]
        if not last:
            ps = [m[c:2 * c] for m in both]
    return ts


@jax.custom_vjp
def _unit_lower_inverse(a_list):
    return _unit_lower_inverse_chain(a_list)


def _unit_lower_inverse_fwd(a_list):
    ts = _unit_lower_inverse_chain(a_list)
    return ts, ts


def _unit_lower_inverse_bwd(ts, dts):
    xs = [_dot_nt(dt, t, INV_PREC) for dt, t in zip(dts, ts)]
    return ([-_dot_tn(t, x, INV_PREC) for t, x in zip(ts, xs)],)


_unit_lower_inverse.defvjp(_unit_lower_inverse_fwd, _unit_lower_inverse_bwd)


@jax.custom_vjp
def _unit_lower_inverse_known(a_list, ts):
    return ts


def _unit_lower_inverse_known_fwd(a_list, ts):
    return ts, ts


def _unit_lower_inverse_known_bwd(ts, dts):
    return _unit_lower_inverse_bwd(ts, dts) + ([jnp.zeros_like(t) for t in ts],)


_unit_lower_inverse_known.defvjp(_unit_lower_inverse_known_fwd, _unit_lower_inverse_known_bwd)


def _gdn_prep_units(qs, ks, vs, gbs, t_known=None):
    c = qs[0].shape[0]
    units = [divmod(i, GDN_HEADS) for i in range(len(qs))]
    row, col = _iota((c, c), 0), _iota((c, c), 1)
    causal, strict = row >= col, row > col
    if t_known is None:
        lane = _iota((1, 128), 1)
        d_alls = [_dot(causal.astype(F32), gb, HI) for gb in gbs]
        g_c, beta_c, d_c = [], [], []
        for r, h in units:
            sel_a = (lane == LANE_A + h).astype(F32)
            g_c.append(jnp.sum(gbs[r] * sel_a, axis=-1, keepdims=True))
            beta_c.append(jnp.sum(gbs[r] * (lane == LANE_B + h).astype(F32), axis=-1, keepdims=True))
            d_c.append(jnp.sum(d_alls[r] * sel_a, axis=-1, keepdims=True))
        d_diff = [jnp.broadcast_to(d, (c, c)) - jnp.broadcast_to(d, (c, c)).T for d in d_c]
    else:
        src = _iota((128, 128), 0)
        spread = jnp.concatenate([(src == base + h).astype(F32) for base in (LANE_A, LANE_B)
                                  for h in range(GDN_HEADS)], axis=1)
        width = GDN_HEADS * 128
        g_beta = [_dot(gb, spread, HI) for gb in gbs]
        d_alls = [_dot(causal.astype(F32), gbv[:, 0:width], HI) for gbv in g_beta]
        g_c = [g_beta[r][:, h * 128:(h + 1) * 128] for r, h in units]
        beta_c = [g_beta[r][:, width + h * 128:width + (h + 1) * 128] for r, h in units]
        d_c = [d_alls[r][:, h * 128:(h + 1) * 128] for r, h in units]
        d_diff = [d[:, 0:c] - d.T[0:c, :] for d in d_c]
    d_last = [jnp.sum(g, axis=0, keepdims=True) for g in g_c]
    decay_mat = [jnp.where(causal, jnp.exp(jnp.where(causal, dd, 0.0)), 0.0) for dd in d_diff]
    kb = [k * b for k, b in zip(ks, beta_c)]
    kbk_qk = [_dot_nt(jnp.concatenate([kbi, q], axis=0), k) for kbi, q, k in zip(kb, qs, ks)]
    a = [jnp.where(strict, m[0:c] * dm, 0.0) for m, dm in zip(kbk_qk, decay_mat)]
    qk = [jnp.where(causal, m[c:2 * c] * dm, 0.0) for m, dm in zip(kbk_qk, decay_mat)]
    t = _unit_lower_inverse(a) if t_known is None else _unit_lower_inverse_known(a, t_known)
    uw = [_dot(ti, jnp.concatenate([v * b, kbi * jnp.exp(d)], axis=1))
          for ti, v, b, kbi, d in zip(t, vs, beta_c, kb, d_c)]
    u = [m[:, 0:128] for m in uw]
    w = [m[:, 128:256] for m in uw]
    q_dec = [q * jnp.exp(d) for q, d in zip(qs, d_c)]
    k_dec = [k * jnp.exp(dl - d) for k, dl, d in zip(ks, d_last, d_c)]
    gamma = [jnp.exp(dl) for dl in d_last]
    return u, w, qk, q_dec, k_dec, gamma, t


def _sum_rows(t):
    return jnp.sum(t, axis=0, keepdims=True)


def _gdn_pre_elem(ps, ab, alog_v, dtb_v):
    outs = []
    for j, p in enumerate(ps):
        s = _silu(p)
        if j < 2 * GDN_HEADS:
            s = s * lax.rsqrt(jnp.sum(s * s, axis=-1, keepdims=True) + RMS_EPS)
        if j < GDN_HEADS:
            s = s * (GDN_DK ** -0.5)
        outs.append(s)
    lane = _iota((1, 128), 1)
    is_a = (lane >= LANE_A) & (lane < LANE_A + GDN_HEADS)
    is_b = (lane >= LANE_B) & (lane < LANE_B + GDN_HEADS)
    g = -jnp.exp(alog_v) * _softplus(ab + dtb_v)
    gb = jnp.where(is_a, g, jnp.where(is_b, jax.nn.sigmoid(ab), 0.0))
    return tuple(outs) + (gb,)


def _proj_fwd(x2, sc3, sh3, ws, seq, tm=512):
    n = x2.shape[0]
    tpe = seq // tm
    nw = len(ws)

    def body(x_ref, sc_ref, sh_ref, *refs):
        h = (x_ref[...] * sc_ref[0] + sh_ref[0]).astype(ws[0].dtype)
        for w_ref, o_ref in zip(refs[:nw], refs[nw:]):
            o_ref[...] = _dot_nt(h, w_ref[...])

    row = lambda i: (i, 0)
    per_ex = pl.BlockSpec((1, 1, D_MODEL), lambda i: (i // tpe, 0, 0))
    return pl.pallas_call(
        body, name="proj_fwd", grid=(n // tm,),
        in_specs=[pl.BlockSpec((tm, D_MODEL), row), per_ex, per_ex]
        + [pl.BlockSpec(w.shape, lambda i: (0, 0)) for w in ws],
        out_specs=[pl.BlockSpec((tm, w.shape[0]), row) for w in ws],
        out_shape=[jax.ShapeDtypeStruct((n, w.shape[0]), F32) for w in ws],
        compiler_params=_params("parallel"),
    )(x2, sc3, sh3, *ws)


GLA_SCAN_CHUNKS = 4


def _gla_operands(q_ref, k_ref, v_ref, og_ref, lr_ref, wg_ref, bg_ref, bl, r_per):
    chunks = [slice(r * CHUNK, (r + 1) * CHUNK) for r in range(r_per)]
    pair_cols = [slice(p * 128, (p + 1) * 128) for p in range(GLA_PAIRS)]
    head_cols = [slice(h * 128, (h + 1) * 128) for h in range(GLA_HEADS)]
    per_pair = lambda ref: [ref[e, rows, cols] for rows in chunks for e in range(bl) for cols in pair_cols]
    per_head = lambda ref: [ref[e, rows, cols] for rows in chunks for e in range(bl) for cols in head_cols]
    return (per_pair(q_ref), per_pair(k_ref), [lr_ref[e, rows, :] for rows in chunks for e in range(bl)],
            per_head(v_ref), per_head(og_ref)), ([wg_ref[:, cols] for cols in pair_cols],
                                                 [bg_ref[:, cols] for cols in pair_cols])


def _gla_fwd(pa, pd, wg, bg, nw, bl, seq):
    n = pa.shape[0]
    nc = seq // CHUNK
    r_per = GLA_SCAN_CHUNKS
    pairs = [(e, p) for e in range(bl) for p in range(GLA_PAIRS)]
    head_slots = [(slice(r * CHUNK, (r + 1) * CHUNK), e, slice(h * 128, (h + 1) * 128))
                  for r in range(r_per) for e in range(bl) for h in range(GLA_HEADS)]

    def body(q_ref, k_ref, v_ref, og_ref, lr_ref, wg_ref, bg_ref, nw_ref, y_ref, st_ref, s_scr):
        @pl.when(pl.program_id(0) == 0)
        def _():
            s_scr[...] = jnp.zeros_like(s_scr)

        ss = [s_scr[e, p] for e, p in pairs]
        for (e, p), s in zip(pairs, ss):
            st_ref[e, 0, p] = s
        acts, gate = _gla_operands(q_ref, k_ref, v_ref, og_ref, lr_ref, wg_ref, bg_ref, bl, r_per)
        ys, s_new = _gla_chunk(*acts, ss, *gate, nw_ref[...])
        for (rows, e, cols), y in zip(head_slots, ys):
            y_ref[e, rows, cols] = y.astype(y_ref.dtype)
        for (e, p), s in zip(pairs, s_new):
            s_scr[e, p] = s

    tok = lambda w, j: pl.BlockSpec((bl, r_per * CHUNK, w), lambda i: (0, i, j))
    const = lambda i: (0, 0)
    pa3 = pa.reshape(bl, seq, 1536)
    y, st = pl.pallas_call(
        body, name="gla_fwd", grid=(nc // r_per,),
        in_specs=[tok(256, 0), tok(256, 1), tok(512, 1), tok(512, 2), tok(128, 0),
                  pl.BlockSpec(wg.shape, const), pl.BlockSpec(bg.shape, const), pl.BlockSpec(nw.shape, const)],
        out_specs=[tok(512, 0), pl.BlockSpec((bl, 1, GLA_PAIRS, 128, 128), lambda i: (0, i, 0, 0, 0))],
        out_shape=[jax.ShapeDtypeStruct((bl, seq, 512), MM_DTYPE),
                   jax.ShapeDtypeStruct((bl, nc // r_per, GLA_PAIRS, 128, 128), F32)],
        scratch_shapes=[pltpu.VMEM((bl, GLA_PAIRS, 128, 128), F32)],
        compiler_params=_params("arbitrary"),
    )(pa3, pa3, pa3, pa3, pd.reshape(bl, seq, 128), wg, bg, nw)
    return y.reshape(n, 512), st


def _gla_bwd(pa, pd, st, dya, wg, bg, nw, bl, seq):
    n = pa.shape[0]
    nc = seq // CHUNK
    r_per = GLA_SCAN_CHUNKS
    steps = nc // r_per
    pairs = [(e, p) for e in range(bl) for p in range(GLA_PAIRS)]
    pair_cols = [slice(p * 128, (p + 1) * 128) for p in range(GLA_PAIRS)]
    chunks = [slice(r * CHUNK, (r + 1) * CHUNK) for r in range(r_per)]
    pair_slots = [(rows, e, p) for rows in chunks for e in range(bl) for p in range(GLA_PAIRS)]
    head_slots = [(rows, e, h) for rows in chunks for e in range(bl) for h in range(GLA_HEADS)]

    def body(q_ref, k_ref, v_ref, og_ref, lr_ref, st_ref, dy_ref, wg_ref, bg_ref, nw_ref,
             da_ref, dd_ref, dwg_ref, dbg_ref, dnw_ref, ds_scr):
        @pl.when(pl.program_id(0) == 0)
        def _():
            dwg_ref[...] = jnp.zeros_like(dwg_ref)
            dbg_ref[...] = jnp.zeros_like(dbg_ref)
            dnw_ref[...] = jnp.zeros_like(dnw_ref)
            ds_scr[...] = jnp.zeros_like(ds_scr)

        acts, gate = _gla_operands(q_ref, k_ref, v_ref, og_ref, lr_ref, wg_ref, bg_ref, bl, r_per)
        _, vjp = jax.vjp(_gla_chunk, *acts, [st_ref[e, 0, p] for e, p in pairs], *gate, nw_ref[...])
        dq, dk, dlr, dv, dog, ds, dwg, dbg, dnw = vjp(
            ([dy_ref[e, rows, h * 128:(h + 1) * 128] for rows, e, h in head_slots], [ds_scr[e, p] for e, p in pairs]))
        for i, (rows, e) in enumerate((rows, e) for rows in chunks for e in range(bl)):
            dd_ref[e, rows, :] = dlr[i]
        for i, (rows, e, p) in enumerate(pair_slots):
            da_ref[e, rows, pair_cols[p]] = dq[i].astype(da_ref.dtype)
            da_ref[e, rows, GLA_QK + p * 128:GLA_QK + (p + 1) * 128] = dk[i].astype(da_ref.dtype)
        for i, (rows, e, h) in enumerate(head_slots):
            da_ref[e, rows, 512 + h * 128:512 + (h + 1) * 128] = dv[i].astype(da_ref.dtype)
            da_ref[e, rows, 1024 + h * 128:1024 + (h + 1) * 128] = dog[i].astype(da_ref.dtype)
        for (e, p), d in zip(pairs, ds):
            ds_scr[e, p] = d
        for p, cols in enumerate(pair_cols):
            dwg_ref[:, cols] += dwg[p]
            dbg_ref[:, cols] += dbg[p]
        dnw_ref[...] += dnw

    tok = lambda w, j: pl.BlockSpec((bl, r_per * CHUNK, w), lambda i: (0, steps - 1 - i, j))
    const = lambda i: (0, 0)
    pa3 = pa.reshape(bl, seq, 1536)
    da, dd, dwg, dbg, dnw = pl.pallas_call(
        body, name="gla_bwd", grid=(steps,),
        in_specs=[tok(256, 0), tok(256, 1), tok(512, 1), tok(512, 2), tok(128, 0),
                  pl.BlockSpec((bl, 1, GLA_PAIRS, 128, 128), lambda i: (0, steps - 1 - i, 0, 0, 0)), tok(512, 0),
                  pl.BlockSpec(wg.shape, const), pl.BlockSpec(bg.shape, const), pl.BlockSpec(nw.shape, const)],
        out_specs=[tok(1536, 0), tok(128, 0),
                   pl.BlockSpec(wg.shape, const), pl.BlockSpec(bg.shape, const), pl.BlockSpec(nw.shape, const)],
        out_shape=[jax.ShapeDtypeStruct((bl, seq, 1536), MM_DTYPE), jax.ShapeDtypeStruct((bl, seq, 128), F32),
                   jax.ShapeDtypeStruct(wg.shape, F32), jax.ShapeDtypeStruct(bg.shape, F32),
                   jax.ShapeDtypeStruct(nw.shape, F32)],
        scratch_shapes=[pltpu.VMEM((bl, GLA_PAIRS, 128, 128), F32)],
        compiler_params=_params("arbitrary"),
    )(pa3, pa3, pa3, pa3, pd.reshape(bl, seq, 128), st, dya.reshape(bl, seq, 512), wg, bg, nw)
    return da.reshape(n, 1536), dd.reshape(n, 128), dwg, dbg, dnw


PRE_ROWS = 64
PRE_PIECES = [slice(j * 128, (j + 1) * 128) for j in range(3 * GDN_HEADS)]


def _rows_from(ref, start, rows, cols):
    lo = start // 8 * 8
    if lo == start:
        return ref[start:start + rows, cols]
    window = ref[lo:lo + rows + 8, cols]
    return pltpu.roll(window, rows + 8 - (start - lo), 0)[0:rows]


def _conv_taps(buf_ref, w_ref, base, rows, cols):
    acc = w_ref[0:1, cols] * _rows_from(buf_ref, base, rows, cols)
    for k in range(1, CONV_K):
        acc = acc + w_ref[k:k + 1, cols] * _rows_from(buf_ref, base + k, rows, cols)
    return acc


def _gdn_pre_fwd(pb, pd, cw8, alog_v, dtb_v, bl, seq, tm=512):
    n = pb.shape[0]
    tpe = seq // tm
    t8 = tm // 8

    def body(u_ref, prev_ref, ab_ref, w_ref, al_ref, dt_ref, qkv_ref, gb_ref, p_ref, buf):
        i = pl.program_id(0)
        keep = (i % tpe != 0).astype(F32)
        buf[0:8, :] = prev_ref[...] * keep
        buf[8:8 + tm, :] = u_ref[...]
        for r0 in range(0, tm, PRE_ROWS):
            rows = slice(r0, r0 + PRE_ROWS)
            ps = [_conv_taps(buf, w_ref, 8 - (CONV_K - 1) + r0, PRE_ROWS, cols) for cols in PRE_PIECES]
            outs = _gdn_pre_elem(ps, ab_ref[rows, :], al_ref[...], dt_ref[...])
            for cols, p, out in zip(PRE_PIECES, ps, outs):
                p_ref[rows, cols] = p
                qkv_ref[rows, cols] = out
            gb_ref[rows, :] = outs[len(PRE_PIECES)]

    row = lambda i: (i, 0)
    const = lambda i: (0, 0)
    return pl.pallas_call(
        body, name="gdn_pre_fwd", grid=(n // tm,),
        in_specs=[pl.BlockSpec((tm, 1536), row),
                  pl.BlockSpec((8, 1536), lambda i: (jnp.maximum(i * t8 - 1, 0), 0)),
                  pl.BlockSpec((tm, 128), row),
                  pl.BlockSpec((8, 1536), const), pl.BlockSpec((1, 128), const), pl.BlockSpec((1, 128), const)],
        out_specs=[pl.BlockSpec((tm, 1536), row), pl.BlockSpec((tm, 128), row), pl.BlockSpec((tm, 1536), row)],
        out_shape=[jax.ShapeDtypeStruct((n, 1536), F32), jax.ShapeDtypeStruct((n, 128), F32),
                   jax.ShapeDtypeStruct((n, 1536), F32)],
        scratch_shapes=[pltpu.VMEM((tm + 8, 1536), F32)],
        compiler_params=_params("parallel"),
    )(pb, pb, pd, cw8, alog_v, dtb_v)


def _gdn_pre_bwd(pb, conv_out, pd, dqkv, dgb, cw8, alog_v, dtb_v, bl, seq, tm=512):
    n = pb.shape[0]
    tpe = seq // tm
    t8 = tm // 8
    nb8 = n // 8
    ext = tm + 8

    def body(u_ref, p_ref, pn_ref, ab_ref, abn_ref, dq_ref, dqn_ref, dgb_ref, w_ref, al_ref, dt_ref,
             du_ref, dab_ref, dw_ref, dal_ref, ddt_ref, dpbuf):
        i = pl.program_id(0)

        @pl.when(i == 0)
        def _():
            dw_ref[...] = jnp.zeros_like(dw_ref)
            dal_ref[...] = jnp.zeros_like(dal_ref)
            ddt_ref[...] = jnp.zeros_like(ddt_ref)

        keep_next = (i % tpe != tpe - 1).astype(F32)
        zeros8 = jnp.zeros((8, 128), F32)
        dal, ddt = jnp.zeros((1, 128), F32), jnp.zeros((1, 128), F32)
        for r0 in range(0, tm, PRE_ROWS):
            rows = slice(r0, r0 + PRE_ROWS)
            last = r0 + PRE_ROWS == tm
            along = lambda own, extra: jnp.concatenate([own, extra], axis=0) if last else own
            ps = [along(p_ref[rows, cols], pn_ref[:, cols]) for cols in PRE_PIECES]
            ab = along(ab_ref[rows, :], abn_ref[...])
            _, vjp = jax.vjp(_gdn_pre_elem, ps, ab, al_ref[...], dt_ref[...])
            cts = tuple(along(dq_ref[rows, cols], dqn_ref[:, cols] * keep_next) for cols in PRE_PIECES)
            cts += (along(dgb_ref[rows, :], zeros8),)
            dps, dab, dal_r, ddt_r = vjp(cts)
            out_rows = slice(r0, r0 + PRE_ROWS + (8 if last else 0))
            for cols, dp in zip(PRE_PIECES, dps):
                dpbuf[out_rows, cols] = dp
            dab_ref[rows, :] = dab[0:PRE_ROWS, :]
            dal, ddt = dal + dal_r, ddt + ddt_r
        dal_ref[...] += dal
        ddt_ref[...] += ddt
        for cols in PRE_PIECES:
            dw = [jnp.zeros((1, 128), F32) for _ in range(CONV_K)]
            for r0 in range(0, tm, PRE_ROWS):
                u = u_ref[r0:r0 + PRE_ROWS, cols]
                du = None
                for k in range(CONV_K):
                    dp_k = _rows_from(dpbuf, r0 + CONV_K - 1 - k, PRE_ROWS, cols)
                    term = w_ref[k:k + 1, cols] * dp_k
                    du = term if du is None else du + term
                    dw[k] = dw[k] + jnp.sum(u * dp_k, axis=0, keepdims=True)
                du_ref[r0:r0 + PRE_ROWS, cols] = du.astype(du_ref.dtype)
            for k in range(CONV_K):
                dw_ref[k:k + 1, cols] += dw[k]

    row = lambda i: (i, 0)
    next8 = lambda i: (jnp.minimum((i + 1) * t8, nb8 - 1), 0)
    const = lambda i: (0, 0)
    return pl.pallas_call(
        body, name="gdn_pre_bwd", grid=(n // tm,),
        in_specs=[pl.BlockSpec((tm, 1536), row), pl.BlockSpec((tm, 1536), row), pl.BlockSpec((8, 1536), next8),
                  pl.BlockSpec((tm, 128), row), pl.BlockSpec((8, 128), next8),
                  pl.BlockSpec((tm, 1536), row), pl.BlockSpec((8, 1536), next8),
                  pl.BlockSpec((tm, 128), row),
                  pl.BlockSpec((8, 1536), const), pl.BlockSpec((1, 128), const), pl.BlockSpec((1, 128), const)],
        out_specs=[pl.BlockSpec((tm, 1536), row), pl.BlockSpec((tm, 128), row),
                   pl.BlockSpec((8, 1536), const), pl.BlockSpec((1, 128), const), pl.BlockSpec((1, 128), const)],
        out_shape=[jax.ShapeDtypeStruct((n, 1536), MM_DTYPE), jax.ShapeDtypeStruct((n, 128), F32),
                   jax.ShapeDtypeStruct((8, 1536), F32), jax.ShapeDtypeStruct((1, 128), F32),
                   jax.ShapeDtypeStruct((1, 128), F32)],
        scratch_shapes=[pltpu.VMEM((ext, 1536), F32)],
        compiler_params=_params("arbitrary"),
    )(pb, conv_out, conv_out, pd, pd, dqkv, dqkv, dgb, cw8, alog_v, dtb_v)


GDN_PREP_CHUNKS = 4
GDN_PREP_BWD_CHUNKS = 4
GDN_SCAN_CHUNKS = 8
MM_DTYPE = BF16


def _head_cols(ref, rows, base=0):
    return [ref[rows, base + h * 128:base + (h + 1) * 128] for h in range(GDN_HEADS)]


def _gdn_prep(qkv, gb):
    n = qkv.shape[0]
    r_per = GDN_PREP_CHUNKS
    tm = r_per * CHUNK

    def body(q_ref, k_ref, v_ref, gb_ref, u_ref, w_ref, qd_ref, kd_ref, qk_ref, t_ref, gam_ref):
        rowid = _iota((8, 128), 0)
        chunk_rows = [slice(r * CHUNK, (r + 1) * CHUNK) for r in range(r_per)]
        gather = lambda ref: [t for rows in chunk_rows for t in _head_cols(ref, rows)]
        u, w, qk, qd, kd, gamma, tinv = _gdn_prep_units(gather(q_ref), gather(k_ref), gather(v_ref),
                                                        [gb_ref[rows, :] for rows in chunk_rows])
        for r, rows in enumerate(chunk_rows):
            gam = jnp.zeros((8, 128), F32)
            for h in range(GDN_HEADS):
                i = r * GDN_HEADS + h
                cols = slice(h * 128, (h + 1) * 128)
                u_ref[rows, cols] = u[i]
                w_ref[rows, cols] = w[i].astype(MM_DTYPE)
                qd_ref[rows, cols] = qd[i].astype(MM_DTYPE)
                kd_ref[rows, cols] = kd[i].astype(MM_DTYPE)
                qk_ref[r, h] = qk[i].astype(MM_DTYPE)
                t_ref[r, h] = tinv[i].astype(MM_DTYPE)
                gam = jnp.where(rowid == h, gamma[i], gam)
            gam_ref[r] = gam

    tok = lambda j: pl.BlockSpec((tm, 512), lambda i: (i, j))
    return pl.pallas_call(
        body, name="gdn_prep", grid=(n // tm,),
        in_specs=[tok(0), tok(1), tok(2), pl.BlockSpec((tm, 128), lambda i: (i, 0))],
        out_specs=[tok(0)] * 4 + [pl.BlockSpec((r_per, GDN_HEADS, CHUNK, CHUNK), lambda i: (i, 0, 0, 0))] * 2
        + [pl.BlockSpec((r_per, 8, 128), lambda i: (i, 0, 0))],
        out_shape=[jax.ShapeDtypeStruct((n, 512), F32)] + [jax.ShapeDtypeStruct((n, 512), MM_DTYPE)] * 3
        + [jax.ShapeDtypeStruct((n // CHUNK, GDN_HEADS, CHUNK, CHUNK), MM_DTYPE)] * 2
        + [jax.ShapeDtypeStruct((n // CHUNK, 8, 128), F32)],
        compiler_params=_params("parallel"),
    )(qkv, qkv, qkv, gb)


def _gdn_fwd(qkv, gb, pc, nw, bl, seq):
    n = qkv.shape[0]
    nc = seq // CHUNK
    u, w, qd, kd, qk, tinv, gam = _gdn_prep(qkv, gb)
    tok3 = lambda t: t.reshape(bl, seq, 512)
    qk5 = qk.reshape(bl, nc, GDN_HEADS, CHUNK, CHUNK)
    gam4 = gam.reshape(bl, nc, 8, 128)

    r_per = GDN_SCAN_CHUNKS
    mm = lambda t: t.astype(MM_DTYPE)

    def body(u_ref, w_ref, qd_ref, kd_ref, qk_ref, gam_ref, og_ref, nw_ref, o_ref, y_ref, vn_ref, st_ref, s_scr):
        @pl.when(pl.program_id(0) == 0)
        def _():
            s_scr[...] = jnp.zeros_like(s_scr)

        units = [(b, h, slice(h * 128, (h + 1) * 128)) for b in range(bl) for h in range(GDN_HEADS)]
        ss = [s_scr[b, h] for b, h, _ in units]
        for r in range(r_per):
            rows = slice(r * CHUNK, (r + 1) * CHUNK)
            for (b, h, _), s in zip(units, ss):
                st_ref[b, r, h] = s
            ws_qs = [_dot(jnp.concatenate([w_ref[b, rows, cols], qd_ref[b, rows, cols]], axis=0), mm(s))
                     for (b, h, cols), s in zip(units, ss)]
            v_new = [u_ref[b, rows, cols] - m[0:CHUNK] for (b, h, cols), m in zip(units, ws_qs)]
            os_ = [m[CHUNK:2 * CHUNK] + _dot(qk_ref[b, r, h], mm(vn))
                   for (b, h, cols), m, vn in zip(units, ws_qs, v_new)]
            ss = [s * gam_ref[b, r, h:h + 1, :] + _dot_tn(kd_ref[b, rows, cols], mm(vn))
                  for (b, h, cols), s, vn in zip(units, ss, v_new)]
            for (b, h, cols), vn, o in zip(units, v_new, os_):
                vn_ref[b, rows, cols] = mm(vn)
                o_ref[b, rows, cols] = o
                y_ref[b, rows, cols] = mm(_rms_gate(o, nw_ref[...], og_ref[b, rows, cols]))
        for (b, h, _), s in zip(units, ss):
            s_scr[b, h] = s

    tok = pl.BlockSpec((bl, r_per * CHUNK, 512), lambda i: (0, i, 0))
    st_spec = pl.BlockSpec((bl, r_per, GDN_HEADS, 128, 128), lambda i: (0, i, 0, 0, 0))
    tok_shape = jax.ShapeDtypeStruct((bl, seq, 512), F32)
    o, y, vn, st = pl.pallas_call(
        body, name="gdn_scan_fwd", grid=(nc // r_per,),
        in_specs=[tok, tok, tok, tok,
                  pl.BlockSpec((bl, r_per, GDN_HEADS, CHUNK, CHUNK), lambda i: (0, i, 0, 0, 0)),
                  pl.BlockSpec((bl, r_per, 8, 128), lambda i: (0, i, 0, 0)), tok,
                  pl.BlockSpec(nw.shape, lambda i: (0, 0))],
        out_specs=[tok, tok, tok, st_spec],
        out_shape=[tok_shape, jax.ShapeDtypeStruct((bl, seq, 512), MM_DTYPE), jax.ShapeDtypeStruct((bl, seq, 512), MM_DTYPE),
                   jax.ShapeDtypeStruct((bl, nc, GDN_HEADS, 128, 128), F32)],
        scratch_shapes=[pltpu.VMEM((bl, GDN_HEADS, 128, 128), F32)],
        compiler_params=_params("arbitrary"),
    )(tok3(u), tok3(w), tok3(qd), tok3(kd), qk5, gam4, tok3(pc), nw)
    return y.reshape(n, 512), (o, st, w, qd, kd, qk5, gam4, tinv, vn)


def _gdn_bwd(qkv, gb, pc, res, dyb, nw, bl, seq):
    n = qkv.shape[0]
    nc = seq // CHUNK
    o, st, w, qd, kd, qk5, gam4, tinv, vn = res
    tok3 = lambda t: t.reshape(bl, seq, 512)

    def scan_body(dy_ref, o_ref, og_ref, w_ref, qd_ref, kd_ref, qk_ref, gam_ref, nw_ref,
                  do_ref, dog_ref, dvn_ref, dst_ref, dnw_ref, ds_scr):
        @pl.when(pl.program_id(0) == 0)
        def _():
            ds_scr[...] = jnp.zeros_like(ds_scr)
            dnw_ref[...] = jnp.zeros_like(dnw_ref)

        units = [(b, h, slice(h * 128, (h + 1) * 128)) for b in range(bl) for h in range(GDN_HEADS)]
        dnw = jnp.zeros(nw.shape, F32)
        dss = [ds_scr[b, h] for b, h, _ in units]
        for r in reversed(range(r_scan)):
            rows = slice(r * CHUNK, (r + 1) * CHUNK)
            d_os = []
            for b, h, cols in units:
                _, vjp = jax.vjp(_rms_gate, o_ref[b, rows, cols], nw_ref[...], og_ref[b, rows, cols])
                d_o, dnw_h, dog = vjp(dy_ref[b, rows, cols])
                do_ref[b, rows, cols] = mm(d_o)
                dog_ref[b, rows, cols] = mm(dog)
                dnw = dnw + dnw_h
                d_os.append(mm(d_o))
            for (b, h, _), ds in zip(units, dss):
                dst_ref[b, r, h] = ds
            dvn_a = [_dot(kd_ref[b, rows, cols], mm(ds)) for (b, h, cols), ds in zip(units, dss)]
            dvns = [a + _dot_tn(qk_ref[b, r, h], d_o) for (b, h, cols), a, d_o in zip(units, dvn_a, d_os)]
            for (b, h, cols), dvn in zip(units, dvns):
                dvn_ref[b, rows, cols] = mm(dvn)
            dss = [ds * gam_ref[b, r, h:h + 1, :] + _dot_tn(
                jnp.concatenate([qd_ref[b, rows, cols], w_ref[b, rows, cols]], axis=0),
                jnp.concatenate([d_o, mm(-dvn)], axis=0))
                for (b, h, cols), d_o, ds, dvn in zip(units, d_os, dss, dvns)]
        dnw_ref[...] += dnw
        for (b, h, _), ds in zip(units, dss):
            ds_scr[b, h] = ds

    r_scan = GDN_SCAN_CHUNKS
    mm = lambda t: t.astype(MM_DTYPE)
    rev = lambda i: nc // r_scan - 1 - i
    tok = pl.BlockSpec((bl, r_scan * CHUNK, 512), lambda i: (0, rev(i), 0))
    st_spec = pl.BlockSpec((bl, r_scan, GDN_HEADS, 128, 128), lambda i: (0, rev(i), 0, 0, 0))
    tok_shape = jax.ShapeDtypeStruct((bl, seq, 512), F32)
    tok_mm = jax.ShapeDtypeStruct((bl, seq, 512), MM_DTYPE)
    d_o, dog, dvn, dst, dnw = pl.pallas_call(
        scan_body, name="gdn_scan_bwd", grid=(nc // r_scan,),
        in_specs=[tok] * 6 + [pl.BlockSpec((bl, r_scan, GDN_HEADS, CHUNK, CHUNK), lambda i: (0, rev(i), 0, 0, 0)),
                              pl.BlockSpec((bl, r_scan, 8, 128), lambda i: (0, rev(i), 0, 0)),
                              pl.BlockSpec(nw.shape, lambda i: (0, 0))],
        out_specs=[tok, tok, tok, st_spec, pl.BlockSpec(nw.shape, lambda i: (0, 0))],
        out_shape=[tok_mm, tok_mm, tok_mm, jax.ShapeDtypeStruct(st.shape, F32),
                   jax.ShapeDtypeStruct(nw.shape, F32)],
        scratch_shapes=[pltpu.VMEM((bl, GDN_HEADS, 128, 128), F32)],
        compiler_params=_params("arbitrary"),
    )(tok3(dyb), o, tok3(pc), tok3(w), tok3(qd), tok3(kd), qk5, gam4, nw)

    r_per = GDN_PREP_BWD_CHUNKS
    tm = r_per * CHUNK

    def prep_body(q_ref, k_ref, v_ref, gb_ref, t_ref, st_ref, dst_ref, dvn_ref, do_ref, vn_ref, dqkv_ref, dgb_ref):
        chunk_rows = [slice(r * CHUNK, (r + 1) * CHUNK) for r in range(r_per)]
        gather = lambda ref: [t for rows in chunk_rows for t in _head_cols(ref, rows)]
        units = [(r, h) for r in range(r_per) for h in range(GDN_HEADS)]
        t_known = [t_ref[r, h].astype(F32) for r, h in units]
        prep = lambda q, k, v, g: _gdn_prep_units(q, k, v, g, t_known)[:6]
        _, vjp = jax.vjp(prep, gather(q_ref), gather(k_ref), gather(v_ref), [gb_ref[rows, :] for rows in chunk_rows])
        ss = [st_ref[r, h] for r, h in units]
        dss = [dst_ref[r, h] for r, h in units]
        dvns, d_os, v_new = gather(dvn_ref), gather(do_ref), gather(vn_ref)
        both = [_dot_nt(jnp.concatenate([dvn, d_o], axis=0), s.astype(MM_DTYPE)) for dvn, d_o, s in zip(dvns, d_os, ss)]
        d_w = [-m[0:CHUNK] for m in both]
        d_qd = [m[CHUNK:2 * CHUNK] for m in both]
        d_qk = [_dot_nt(d_o, vn) for d_o, vn in zip(d_os, v_new)]
        d_kd = [_dot_nt(vn, ds.astype(MM_DTYPE)) for vn, ds in zip(v_new, dss)]
        d_gam = [_sum_rows(ds * s) for ds, s in zip(dss, ss)]
        dq, dk, dv, dgb = vjp(([d.astype(F32) for d in dvns], d_w, d_qk, d_qd, d_kd, d_gam))
        for i, (r, h) in enumerate(units):
            rows = chunk_rows[r]
            for part, d in enumerate((dq, dk, dv)):
                dqkv_ref[rows, part * 512 + h * 128:part * 512 + (h + 1) * 128] = d[i]
        for r, rows in enumerate(chunk_rows):
            dgb_ref[rows, :] = dgb[r]

    tokp = lambda j: pl.BlockSpec((tm, 512), lambda i: (i, j))
    st4 = pl.BlockSpec((r_per, GDN_HEADS, 128, 128), lambda i: (i, 0, 0, 0))
    dqkv, dgb = pl.pallas_call(
        prep_body, name="gdn_prep_bwd", grid=(n // tm,),
        in_specs=[tokp(0), tokp(1), tokp(2), pl.BlockSpec((tm, 128), lambda i: (i, 0)),
                  pl.BlockSpec((r_per, GDN_HEADS, CHUNK, CHUNK), lambda i: (i, 0, 0, 0)), st4, st4,
                  tokp(0), tokp(0), tokp(0)],
        out_specs=[pl.BlockSpec((tm, 1536), lambda i: (i, 0)), pl.BlockSpec((tm, 128), lambda i: (i, 0))],
        out_shape=[jax.ShapeDtypeStruct((n, 1536), F32), jax.ShapeDtypeStruct((n, 128), F32)],
        compiler_params=_params("parallel"),
    )(qkv, qkv, qkv, gb, tinv, st.reshape(bl * nc, GDN_HEADS, 128, 128), dst.reshape(bl * nc, GDN_HEADS, 128, 128),
      dvn.reshape(n, 512), d_o.reshape(n, 512), vn.reshape(n, 512))
    return dqkv, dog.reshape(n, 512), dgb, dnw


def _out_block(x2, tgt2, ya, yb, g1p3, wo, lnw, lnb, seq, tm=512):
    n = x2.shape[0]
    tpe = seq // tm
    bl = n // seq

    def body(x_ref, t_ref, ya_ref, yb_ref, g_ref, wo_ref, lnw_ref, lnb_ref,
             dz_ref, dya_ref, dyb_ref, dwo_ref, dg_ref, glw_ref, glb_ref, loss_ref):
        i = pl.program_id(0)

        @pl.when(i == 0)
        def _():
            dwo_ref[...] = jnp.zeros_like(dwo_ref)
            glw_ref[...] = jnp.zeros_like(glw_ref)
            glb_ref[...] = jnp.zeros_like(glb_ref)
            loss_ref[...] = jnp.zeros_like(loss_ref)

        @pl.when(i % tpe == 0)
        def _():
            dg_ref[...] = jnp.zeros_like(dg_ref)

        ya16 = ya_ref[...].astype(wo.dtype)
        yb16 = yb_ref[...].astype(wo.dtype)
        wa = wo_ref[0:GLA_WIDTH, :]
        wb = wo_ref[GLA_WIDTH:, :]
        y = _dot(ya16, wa) + _dot(yb16, wb)
        g1p = g_ref[0]
        z = ALPHA * x_ref[...] + g1p * y
        mu = jnp.mean(z, axis=-1, keepdims=True)
        zc = z - mu
        rstd = lax.rsqrt(jnp.mean(zc * zc, axis=-1, keepdims=True) + LN_EPS)
        xhat = zc * rstd
        diff = xhat * lnw_ref[...] + lnb_ref[...] - t_ref[...]
        loss_ref[...] += (0.5 / D_MODEL) * jnp.sum(jnp.sum(diff * diff, axis=-1, keepdims=True), axis=0, keepdims=True)
        dout = diff * (1.0 / D_MODEL)
        glw_ref[...] += jnp.sum(dout * xhat, axis=0, keepdims=True)
        glb_ref[...] += jnp.sum(dout, axis=0, keepdims=True)
        dxh = dout * lnw_ref[...]
        dz = rstd * (dxh - jnp.mean(dxh, axis=-1, keepdims=True)
                     - xhat * jnp.mean(dxh * xhat, axis=-1, keepdims=True))
        dz_ref[...] = dz
        dg_ref[0] += jnp.sum(dz * y, axis=0, keepdims=True)
        dy = (g1p * dz).astype(wo.dtype)
        dya_ref[...] = _dot_nt(dy, wa)
        dyb_ref[...] = _dot_nt(dy, wb)
        dwo_ref[0:GLA_WIDTH, :] += _dot_tn(ya16, dy)
        dwo_ref[GLA_WIDTH:, :] += _dot_tn(yb16, dy)

    row = lambda i: (i, 0)
    const = lambda i: (0, 0)
    per_ex = pl.BlockSpec((1, 1, D_MODEL), lambda i: (i // tpe, 0, 0))
    return pl.pallas_call(
        body, name="out_block", grid=(n // tm,),
        in_specs=[pl.BlockSpec((tm, D_MODEL), row), pl.BlockSpec((tm, D_MODEL), row),
                  pl.BlockSpec((tm, 512), row), pl.BlockSpec((tm, 512), row), per_ex,
                  pl.BlockSpec((D_MODEL, D_MODEL), const), pl.BlockSpec((1, D_MODEL), const),
                  pl.BlockSpec((1, D_MODEL), const)],
        out_specs=[pl.BlockSpec((tm, D_MODEL), row), pl.BlockSpec((tm, 512), row), pl.BlockSpec((tm, 512), row),
                   pl.BlockSpec((D_MODEL, D_MODEL), const), per_ex,
                   pl.BlockSpec((1, D_MODEL), const), pl.BlockSpec((1, D_MODEL), const),
                   pl.BlockSpec((1, 1), const)],
        out_shape=[jax.ShapeDtypeStruct((n, D_MODEL), F32), jax.ShapeDtypeStruct((n, 512), F32),
                   jax.ShapeDtypeStruct((n, 512), F32), jax.ShapeDtypeStruct((D_MODEL, D_MODEL), F32),
                   jax.ShapeDtypeStruct((bl, 1, D_MODEL), F32), jax.ShapeDtypeStruct((1, D_MODEL), F32),
                   jax.ShapeDtypeStruct((1, D_MODEL), F32), jax.ShapeDtypeStruct((1, 1), F32)],
        compiler_params=_params("arbitrary"),
    )(x2, tgt2, ya, yb, g1p3, wo, lnw, lnb)


def _proj_bwd_x(ds, ws, x2, dz, sc3, seq, tm=512):
    n = x2.shape[0]
    tpe = seq // tm
    bl = n // seq

    def body(da_ref, db_ref, dc_ref, dd1_ref, dd2_ref, wa_ref, wb_ref, wc_ref, wd_ref, x_ref, dz_ref, sc_ref,
             gx_ref, dsh_ref, dsc_ref):
        i = pl.program_id(0)

        @pl.when(i % tpe == 0)
        def _():
            dsh_ref[...] = jnp.zeros_like(dsh_ref)
            dsc_ref[...] = jnp.zeros_like(dsc_ref)

        cdt = ws[0].dtype
        dh = _dot(da_ref[...].astype(cdt), wa_ref[...])
        dh += _dot(db_ref[...].astype(cdt), wb_ref[...])
        dh += _dot(dc_ref[...].astype(cdt), wc_ref[...])
        dh += _dot((dd1_ref[...] + dd2_ref[...]).astype(cdt), wd_ref[...])
        gx_ref[...] = dh * sc_ref[0] + ALPHA * dz_ref[...]
        dsh_ref[0] += jnp.sum(dh, axis=0, keepdims=True)
        dsc_ref[0] += jnp.sum(dh * x_ref[...], axis=0, keepdims=True)

    row = lambda i: (i, 0)
    const = lambda i: (0, 0)
    per_ex = pl.BlockSpec((1, 1, D_MODEL), lambda i: (i // tpe, 0, 0))
    da, db, dc, (dd1, dd2) = ds
    return pl.pallas_call(
        body, name="proj_bwd_x", grid=(n // tm,),
        in_specs=[pl.BlockSpec((tm, d.shape[1]), row) for d in (da, db, dc, dd1, dd2)]
        + [pl.BlockSpec(w.shape, const) for w in ws]
        + [pl.BlockSpec((tm, D_MODEL), row), pl.BlockSpec((tm, D_MODEL), row), per_ex],
        out_specs=[pl.BlockSpec((tm, D_MODEL), row), per_ex, per_ex],
        out_shape=[jax.ShapeDtypeStruct((n, D_MODEL), F32), jax.ShapeDtypeStruct((bl, 1, D_MODEL), F32),
                   jax.ShapeDtypeStruct((bl, 1, D_MODEL), F32)],
        compiler_params=_params("arbitrary"),
    )(da, db, dc, dd1, dd2, *ws, x2, dz, sc3)


def _proj_bwd_w(x2, sc3, sh3, ds, seq, cdt, name, tm=1024):
    n = x2.shape[0]
    tpe = seq // tm
    flat, groups = [], []
    for d in ds:
        parts = d if isinstance(d, tuple) else (d,)
        groups.append(len(parts))
        flat.extend(parts)
    nin = len(flat)

    def body(x_ref, sc_ref, sh_ref, *refs):
        i = pl.program_id(0)
        outs = refs[nin:]

        @pl.when(i == 0)
        def _():
            for o in outs:
                o[...] = jnp.zeros_like(o)

        h = (x_ref[...] * sc_ref[0] + sh_ref[0]).astype(cdt)
        pos = 0
        for o, cnt in zip(outs, groups):
            d = refs[pos][...]
            for extra in refs[pos + 1:pos + cnt]:
                d = d + extra[...]
            pos += cnt
            o[...] += _dot_tn(d.astype(cdt), h)

    row = lambda i: (i, 0)
    const = lambda i: (0, 0)
    per_ex = pl.BlockSpec((1, 1, D_MODEL), lambda i: (i // tpe, 0, 0))
    widths = [(d[0] if isinstance(d, tuple) else d).shape[1] for d in ds]
    return pl.pallas_call(
        body, name=name, grid=(n // tm,),
        in_specs=[pl.BlockSpec((tm, D_MODEL), row), per_ex, per_ex]
        + [pl.BlockSpec((tm, d.shape[1]), row) for d in flat],
        out_specs=[pl.BlockSpec((w, D_MODEL), const) for w in widths],
        out_shape=[jax.ShapeDtypeStruct((w, D_MODEL), F32) for w in widths],
        compiler_params=_params("arbitrary"),
    )(x2, sc3, sh3, *flat)


def _mod_block(c_all, w_ada_sh, b_blk):
    def body(c_ref, w_ref, b_ref, o_ref):
        o_ref[...] = _dot(c_ref[...], w_ref[...]) + b_ref[...]

    return pl.pallas_call(
        body, name="mod_block",
        out_shape=jax.ShapeDtypeStruct((c_all.shape[0], w_ada_sh.shape[1]), F32),
        compiler_params=pltpu.CompilerParams(vmem_limit_bytes=VMEM_LIMIT),
    )(c_all, w_ada_sh, b_blk)


def _ada_grads(c_all, dmod_all, dmod_blk):
    def body(c_ref, da_ref, db_ref, gw_ref, gb_ref):
        gw_ref[...] = _dot_tn(c_ref[...], db_ref[...])
        gb_ref[...] = jnp.sum(da_ref[...], axis=0, keepdims=True)

    return pl.pallas_call(
        body, name="ada_grads",
        out_shape=[jax.ShapeDtypeStruct((c_all.shape[1], dmod_blk.shape[1]), F32),
                   jax.ShapeDtypeStruct((1, dmod_all.shape[1]), F32)],
        compiler_params=pltpu.CompilerParams(vmem_limit_bytes=VMEM_LIMIT),
    )(c_all, dmod_all, dmod_blk)


def _sum_leading(parts, name):
    def body(p_ref, o_ref):
        acc = p_ref[0]
        for d in range(1, parts.shape[0]):
            acc = acc + p_ref[d]
        o_ref[...] = acc

    return pl.pallas_call(
        body, name=name, out_shape=jax.ShapeDtypeStruct(parts.shape[1:], F32),
        compiler_params=pltpu.CompilerParams(vmem_limit_bytes=VMEM_LIMIT),
    )(parts)


ELEMENTWISE_BLOCK_BYTES = 2 * 1024 * 1024


def _tile2d(rows, cols, row_align=8):
    if rows * cols * 4 <= ELEMENTWISE_BLOCK_BYTES:
        return rows, cols
    fits = [t for t in range(row_align, rows, row_align) if rows % t == 0 and t * cols * 4 <= ELEMENTWISE_BLOCK_BYTES]
    if fits:
        return fits[-1], cols
    fits = [t for t in range(128, cols, 128) if cols % t == 0 and rows * t * 4 <= ELEMENTWISE_BLOCK_BYTES]
    assert fits, (rows, cols)
    return rows, fits[-1]


def _add_n(arrs, name, out_dtypes=(F32,)):
    rows, cols = arrs[0].shape
    narrow = any(jnp.dtype(dt).itemsize < 4 for dt in tuple(out_dtypes) + tuple(a.dtype for a in arrs))
    tr, tc = _tile2d(rows, cols, 16 if narrow else 8)
    n_in = len(arrs)

    def body(*refs):
        acc = refs[0][...].astype(F32)
        for r in refs[1:n_in]:
            acc = acc + r[...].astype(F32)
        for o in refs[n_in:]:
            o[...] = acc.astype(o.dtype)

    spec = pl.BlockSpec((tr, tc), lambda i, j: (i, j))
    return pl.pallas_call(
        body, name=name, grid=(rows // tr, cols // tc), in_specs=[spec] * n_in, out_specs=[spec] * len(out_dtypes),
        out_shape=[jax.ShapeDtypeStruct((rows, cols), dt) for dt in out_dtypes],
        compiler_params=_params("parallel", "parallel"),
    )(*arrs)


def _chip_sum_blocks(a, b, per, blocks, name, chunk=128):
    rows, cols = a.shape
    padded = -(-per // 16) * 16
    assert rows >= (blocks - 1) * per + padded, (rows, per, blocks)

    def body(a_ref, b_ref, o_ref, o16_ref):
        for j in range(blocks):
            for r0 in range(0, padded, chunk):
                n_rows = min(chunk, padded - r0)
                src = pl.ds(j * per + r0, n_rows)
                s = a_ref[src, :] + b_ref[src, :]
                if per - r0 < n_rows:
                    s = jnp.where(_iota((n_rows, 1), 0) < per - r0, s, 0.0)
                o_ref[j, r0:r0 + n_rows, :] = s
                o16_ref[j, r0:r0 + n_rows, :] = s.astype(BF16)

    return pl.pallas_call(
        body, name=name,
        out_shape=[jax.ShapeDtypeStruct((blocks, padded, cols), F32), jax.ShapeDtypeStruct((blocks, padded, cols), BF16)],
        compiler_params=pltpu.CompilerParams(vmem_limit_bytes=VMEM_LIMIT),
    )(a, b)


GRAD_PAD_ROWS = 16


def _adamw(w, g, m, v, name):
    rows, cols = w.shape
    tr, tc = _tile2d(rows, cols)
    c1 = 1.0 / (1.0 - ADAM_B1 ** ADAM_STEP)
    c2 = 1.0 / (1.0 - ADAM_B2 ** ADAM_STEP)

    def body(w_ref, g_ref, m_ref, v_ref, d_ref, nm_ref, nv_ref):
        gg = g_ref[...]
        nm = ADAM_B1 * m_ref[...] + (1.0 - ADAM_B1) * gg
        nv = ADAM_B2 * v_ref[...] + (1.0 - ADAM_B2) * (gg * gg)
        nm_ref[...] = nm
        nv_ref[...] = nv
        d_ref[...] = -ADAM_LR * ((nm * c1) / (jnp.sqrt(nv * c2) + ADAM_EPS) + ADAM_WD * w_ref[...])

    spec = pl.BlockSpec((tr, tc), lambda i, j: (i, j))
    shp = jax.ShapeDtypeStruct((rows, cols), F32)
    return pl.pallas_call(
        body, name=name, grid=(rows // tr, cols // tc), in_specs=[spec] * 4, out_specs=[spec] * 3,
        out_shape=[shp, shp, shp], compiler_params=_params("parallel", "parallel"),
    )(w, g, m, v)


def _coords():
    return lax.axis_index("x"), lax.axis_index("y"), lax.axis_index("c")


def _all_gather8(blk, name):
    m_per, n = blk.shape

    def body(x_ref, out_ref, send_sems, recv_sems, local_sem):
        x, y, c = _coords()
        me, sibling = (x, y, c), (x, y, 1 - c)
        chips = [(1 - x, y), (x, 1 - y), (1 - x, 1 - y)]

        def rows(px, py, pc):
            return out_ref.at[pl.ds((4 * px + 2 * py + pc) * m_per, m_per), :]

        def copy(k, block, to, src=None):
            return pltpu.make_async_remote_copy(
                src_ref=rows(*block) if src is None else src, dst_ref=rows(*block),
                send_sem=send_sems.at[k], recv_sem=recv_sems.at[k], device_id=to, device_id_type=MESH)

        mine = pltpu.make_async_copy(x_ref, rows(*me), local_sem)
        mine.start()
        first = [copy(0, me, sibling, src=x_ref)]
        first += [copy(1 + j, me, (*chip, c), src=x_ref) for j, chip in enumerate(chips)]
        for cp in first:
            cp.start()
        passed = [copy(4 + j, (*chip, c), sibling) for j, chip in enumerate(chips)]
        for j, chip in enumerate(chips):
            copy(1 + j, (*chip, c), me).wait_recv()
            passed[j].start()
        copy(0, sibling, me).wait_recv()
        for j, chip in enumerate(chips):
            copy(4 + j, (*chip, 1 - c), me).wait_recv()
        for cp in first + passed:
            cp.wait_send()
        mine.wait()

    return pl.pallas_call(
        body, name=name,
        out_shape=jax.ShapeDtypeStruct((8 * m_per, n), blk.dtype),
        in_specs=[pl.BlockSpec(memory_space=pltpu.VMEM)],
        out_specs=pl.BlockSpec(memory_space=pltpu.VMEM),
        scratch_shapes=[pltpu.SemaphoreType.DMA((7,)), pltpu.SemaphoreType.DMA((7,)), pltpu.SemaphoreType.DMA],
        compiler_params=pltpu.CompilerParams(vmem_limit_bytes=VMEM_LIMIT),
    )(blk)


def _chip_gather(shards, split, name):
    k_arr = len(shards)

    def body(*refs):
        srcs, dsts = refs[:k_arr], refs[k_arr:2 * k_arr]
        send_sems, recv_sems, fwd_send_sems, fwd_recv_sems, local_sems = refs[2 * k_arr:]
        x, y, c = _coords()
        peers = [(1 - x, y, c), (x, 1 - y, c), (1 - x, 1 - y, c)]
        sibling = (x, y, 1 - c)
        me_chip = 2 * x + y

        def part(ref, a, core):
            if not split[a]:
                return ref
            half = shards[a].shape[1] // 2
            return ref.at[:, pl.ds(core * half, half)]

        def ici(a, j, src_chip, dst_dev):
            return pltpu.make_async_remote_copy(
                src_ref=part(srcs[a], a, c), dst_ref=part(dsts[a].at[src_chip], a, c),
                send_sem=send_sems.at[a, j], recv_sem=recv_sems.at[a, j], device_id=dst_dev, device_id_type=MESH)

        def d2d(a, j, src_chip, core):
            return pltpu.make_async_remote_copy(
                src_ref=part(dsts[a].at[src_chip], a, core), dst_ref=part(dsts[a].at[src_chip], a, core),
                send_sem=fwd_send_sems.at[a, j], recv_sem=fwd_recv_sems.at[a, j],
                device_id=sibling, device_id_type=MESH)

        local = [pltpu.make_async_copy(srcs[a], dsts[a].at[me_chip], local_sems.at[a]) for a in range(k_arr)]
        for cp in local:
            cp.start()
        sends = [ici(a, j, me_chip, peer) for a in range(k_arr) for j, peer in enumerate(peers)]
        for cp in sends:
            cp.start()
        forwards = []
        for a in range(k_arr):
            for j, peer in enumerate(peers):
                peer_chip = 2 * peer[0] + peer[1]
                ici(a, j, peer_chip, peer).wait_recv()
                if split[a]:
                    forwards.append(d2d(a, j, peer_chip, c))
                    forwards[-1].start()
        for a in range(k_arr):
            for j, peer in enumerate(peers):
                if split[a]:
                    d2d(a, j, 2 * peer[0] + peer[1], 1 - c).wait_recv()
        for cp in sends + forwards:
            cp.wait_send()
        for cp in local:
            cp.wait()

    any_spec = pl.BlockSpec(memory_space=pl.ANY)
    return pl.pallas_call(
        body, name=name,
        out_shape=[jax.ShapeDtypeStruct((4,) + s.shape, s.dtype) for s in shards],
        in_specs=[any_spec] * k_arr, out_specs=[any_spec] * k_arr,
        scratch_shapes=[pltpu.SemaphoreType.DMA((k_arr, 3))] * 4 + [pltpu.SemaphoreType.DMA((k_arr,))],
    )(*shards)


def _chip_scatter(pieces, name):
    k_arr = len(pieces)

    def body(*refs):
        srcs, dsts = refs[:k_arr], refs[k_arr:2 * k_arr]
        send_sems, recv_sems = refs[2 * k_arr:]
        x, y, c = _coords()
        peers = [(1 - x, y, c), (x, 1 - y, c), (1 - x, 1 - y, c)]
        copies = []
        for a in range(k_arr):
            for j, peer in enumerate(peers):
                copies.append(pltpu.make_async_remote_copy(
                    src_ref=srcs[a].at[2 * peer[0] + peer[1]], dst_ref=dsts[a].at[j],
                    send_sem=send_sems.at[a, j], recv_sem=recv_sems.at[a, j], device_id=peer, device_id_type=MESH))
        for cp in copies:
            cp.start()
        for cp in copies:
            cp.wait_recv()
        for cp in copies:
            cp.wait_send()

    any_spec = pl.BlockSpec(memory_space=pl.ANY)
    return pl.pallas_call(
        body, name=name,
        out_shape=[jax.ShapeDtypeStruct((3,) + p.shape[1:], p.dtype) for p in pieces],
        in_specs=[any_spec] * k_arr, out_specs=[any_spec] * k_arr,
        scratch_shapes=[pltpu.SemaphoreType.DMA((k_arr, 3)), pltpu.SemaphoreType.DMA((k_arr, 3))],
    )(*pieces)


def _sibling_swap(arrs, name):
    k_arr = len(arrs)

    def body(*refs):
        srcs, dsts = refs[:k_arr], refs[k_arr:2 * k_arr]
        send_sems, recv_sems = refs[2 * k_arr:]
        x, y, c = _coords()
        copies = [pltpu.make_async_remote_copy(
            src_ref=srcs[a], dst_ref=dsts[a], send_sem=send_sems.at[a], recv_sem=recv_sems.at[a],
            device_id=(x, y, 1 - c), device_id_type=MESH) for a in range(k_arr)]
        for cp in copies:
            cp.start()
        for cp in copies:
            cp.wait_recv()
        for cp in copies:
            cp.wait_send()

    any_spec = pl.BlockSpec(memory_space=pl.ANY)
    return pl.pallas_call(
        body, name=name,
        out_shape=[jax.ShapeDtypeStruct(a.shape, a.dtype) for a in arrs],
        in_specs=[any_spec] * k_arr, out_specs=[any_spec] * k_arr,
        scratch_shapes=[pltpu.SemaphoreType.DMA((k_arr,)), pltpu.SemaphoreType.DMA((k_arr,))],
    )(*arrs)


def _split_w_in(w_in_t):
    wa = jnp.concatenate([w_in_t[0:1024], w_in_t[1040:1552]], axis=0)
    wb = w_in_t[1552:3088]
    wc = w_in_t[3096:3608]
    wd = jnp.concatenate([w_in_t[1024:1040], w_in_t[3088:3096],
                          jnp.zeros((128 - SMALL_USED, w_in_t.shape[1]), w_in_t.dtype)], axis=0)
    return wa, wb, wc, wd


def _merge_dw_in(dwa, dwb, dwc, dwd):
    return jnp.concatenate([dwa[0:1024], dwd[0:GLA_RANK], dwa[1024:1536], dwb, dwd[GLA_RANK:SMALL_USED], dwc,
                            jnp.zeros((GRAD_PAD_ROWS, dwa.shape[1]), dwa.dtype)], axis=0)


def _local_step(x, mod, w_in16, w_out16, gla_wg, gla_bg, gla_nw, conv_w, a_log, dt_bias, gdn_nw, ln_w, ln_b, tgt):
    bl, seq, _ = x.shape
    n = bl * seq
    x2 = x.reshape(n, D_MODEL)
    tgt2 = tgt.reshape(n, D_MODEL)
    sh3 = mod[:, None, 0:D_MODEL]
    sc3 = 1.0 + mod[:, None, D_MODEL:2 * D_MODEL]
    g1p3 = 1.0 + mod[:, None, 2 * D_MODEL:]
    ws = _split_w_in(w_in16)
    wg = jnp.concatenate([gla_wg, jnp.zeros((128 - GLA_RANK, GLA_QK), F32)], axis=0)
    cw8 = jnp.concatenate([conv_w, jnp.zeros((8 - CONV_K, conv_w.shape[1]), F32)], axis=0)
    alog_v = jnp.zeros((1, 128), F32).at[:, LANE_A:LANE_A + GDN_HEADS].set(a_log)
    dtb_v = jnp.zeros((1, 128), F32).at[:, LANE_A:LANE_A + GDN_HEADS].set(dt_bias)

    pa, pb, pc, pd = _proj_fwd(x2, sc3, sh3, ws, seq)
    ya, st_a = _gla_fwd(pa, pd, wg, gla_bg, gla_nw, bl, seq)
    qkv, gb, conv_out = _gdn_pre_fwd(pb, pd, cw8, alog_v, dtb_v, bl, seq)
    yb, st_b = _gdn_fwd(qkv, gb, pc, gdn_nw, bl, seq)
    dz, dya, dyb, d_wo, d_gate, d_lnw, d_lnb, loss = _out_block(x2, tgt2, ya, yb, g1p3, w_out16, ln_w, ln_b, seq)
    da, dd1, d_wg, d_bg, d_nwa = _gla_bwd(pa, pd, st_a, dya, wg, gla_bg, gla_nw, bl, seq)
    dqkv, dc, dgb, d_nwb = _gdn_bwd(qkv, gb, pc, st_b, dyb, gdn_nw, bl, seq)
    db, dd2, d_cw8, d_alog, d_dtb = _gdn_pre_bwd(pb, conv_out, pd, dqkv, dgb, cw8, alog_v, dtb_v, bl, seq)
    gx, d_sh, d_sc = _proj_bwd_x((da, db, dc, (dd1, dd2)), ws, x2, dz, sc3, seq)
    (dwa,) = _proj_bwd_w(x2, sc3, sh3, [da], seq, w_in16.dtype, "proj_bwd_w_a")
    dwb, dwc, dwd = _proj_bwd_w(x2, sc3, sh3, [db, dc, (dd1, dd2)], seq, w_in16.dtype, "proj_bwd_w_bcd")
    grads = dict(
        w_in=_merge_dw_in(dwa, dwb, dwc, dwd),
        w_out=d_wo,
        gla_w_gate_up=d_wg[0:GLA_RANK, :],
        gla_b_gate=d_bg,
        gla_norm_w=d_nwa,
        gdn_conv_w=d_cw8[0:CONV_K, :],
        gdn_a_log=d_alog[:, LANE_A:LANE_A + GDN_HEADS],
        gdn_dt_bias=d_dtb[:, LANE_A:LANE_A + GDN_HEADS],
        gdn_norm_w=d_nwb,
        ln_w=d_lnw,
        ln_b=d_lnb,
        mod=jnp.concatenate([d_sh[:, 0, :], d_sc[:, 0, :], d_gate[:, 0, :]], axis=1),
    )
    return loss, gx.reshape(bl, seq, D_MODEL), grads


_SMALL = (("gla_b_gate", 256), ("gla_norm_w", 128), ("gdn_a_log", 4), ("gdn_dt_bias", 4), ("gdn_norm_w", 128),
          ("ln_w", 1024), ("ln_b", 1024), ("gla_w_gate_up", 16 * 256), ("gdn_conv_w", 4 * 1536), ("loss", 1),
          ("mod", 2 * 3072))


def _pack_small(grads):
    flat = jnp.concatenate([grads[k].reshape(-1) for k, _ in _SMALL])
    total = sum(sz for _, sz in _SMALL)
    rows = -(-total // 1024) * 8
    return jnp.concatenate([flat, jnp.zeros((rows * 128 - total,), F32)]).reshape(rows, 128)


def _unpack_small(flat):
    out, pos = {}, 0
    for k, sz in _SMALL:
        out[k] = flat[pos:pos + sz]
        pos += sz
    return out


def kernel(x, c, w_ada, b_ada, w_in, gla_w_gate_up, gla_b_gate, gla_norm_w, gdn_conv_w, gdn_a_log, gdn_dt_bias, gdn_norm_w, w_out, ln_w, ln_b, loss_target, m_w_ada, m_b_ada, m_w_in, m_gla_w_gate_up, m_gla_b_gate, m_gla_norm_w, m_gdn_conv_w, m_gdn_a_log, m_gdn_dt_bias, m_gdn_norm_w, m_w_out, m_ln_w, m_ln_b, v_w_ada, v_b_ada, v_w_in, v_gla_w_gate_up, v_gla_b_gate, v_gla_norm_w, v_gdn_conv_w, v_gdn_a_log, v_gdn_dt_bias, v_gdn_norm_w, v_w_out, v_ln_w, v_ln_b):
    ix, iy, ic = _coords()
    chip = 2 * ix + iy
    dev = 4 * ix + 2 * iy + ic
    bl = x.shape[0]
    ndev = 8

    c_all = _all_gather8(c.reshape(8, -1), "gather_c").reshape(ndev * bl, D_MODEL)
    ada_cols = w_ada.shape[2]
    b_blk = lax.dynamic_slice_in_dim(b_ada, chip * ada_cols, ada_cols, axis=1)
    mod_blk = _mod_block(c_all, w_ada[0], b_blk)
    mod_g = _all_gather8(mod_blk, "gather_mod").reshape(ndev, ndev * bl, ada_cols)
    mod_all = jnp.concatenate([mod_g[2 * j] for j in range(4)], axis=1)
    mod = lax.dynamic_slice_in_dim(mod_all, dev * bl, bl, axis=0)

    w_in_g, w_out_g, wg_g, cw_g = _chip_gather(
        [jnp.transpose(w_in[0]).astype(BF16), w_out[0].astype(BF16), gla_w_gate_up[0], gdn_conv_w[0]],
        [True, True, False, False], "gather_weights")
    w_in16 = w_in_g.reshape(IN_COLS, D_MODEL)
    w_out16 = w_out_g.reshape(D_MODEL, D_MODEL)
    gla_wg = jnp.concatenate([wg_g[j] for j in range(4)], axis=1)
    conv_w = jnp.concatenate([cw_g[j] for j in range(4)], axis=1)

    loss, grad_x, gr = _local_step(x, mod, w_in16, w_out16, gla_wg, gla_b_gate, gla_norm_w, conv_w,
                                   gdn_a_log, gdn_dt_bias, gdn_norm_w, ln_w, ln_b, loss_target)

    gr["loss"] = loss
    packed = _pack_small(gr)
    prow = packed.shape[0]
    gathered = _all_gather8(packed, "gather_small").reshape(ndev, prow, 128)
    small = _unpack_small(_sum_leading(gathered, "sum_small").reshape(-1))
    loss = small["loss"][0]
    mod_rows = gathered.reshape(ndev, prow * 128)[:, sum(sz for _, sz in _SMALL[:-1]):][:, :bl * 3 * D_MODEL]
    dmod_all = mod_rows.reshape(ndev * bl, 3 * D_MODEL)
    dmod_blk = lax.dynamic_slice_in_dim(dmod_all, chip * ada_cols, ada_cols, axis=1)
    g_w_ada, g_b_ada = _ada_grads(c_all, dmod_all, dmod_blk)
    wg_cols = gla_w_gate_up.shape[2]
    g_wg = lax.dynamic_slice_in_dim(small["gla_w_gate_up"].reshape(GLA_RANK, GLA_QK), chip * wg_cols, wg_cols, axis=1)
    cw_cols = gdn_conv_w.shape[2]
    g_cw = lax.dynamic_slice_in_dim(small["gdn_conv_w"].reshape(CONV_K, 3 * GDN_WIDTH), chip * cw_cols, cw_cols, axis=1)

    in_feats = w_in.shape[2]
    out_rows = w_out.shape[1]
    p_in = gr["w_in"]
    p_out = gr["w_out"].reshape(4, out_rows, D_MODEL)
    h_in, h_out = D_MODEL // 2, out_rows // 2
    mine_in = lax.dynamic_slice_in_dim(p_in, ic * h_in, h_in, axis=1)
    mine_out = lax.dynamic_slice_in_dim(p_out, ic * h_out, h_out, axis=1)
    theirs_in = lax.dynamic_slice_in_dim(p_in, (1 - ic) * h_in, h_in, axis=1)
    theirs_out = lax.dynamic_slice_in_dim(p_out, (1 - ic) * h_out, h_out, axis=1)
    got_in, got_out = _sibling_swap([theirs_in, theirs_out], "swap_halves")
    chip_in, chip_in16 = _chip_sum_blocks(mine_in, got_in, in_feats, 4, "chip_sum_in")
    chip_out, chip_out16 = _add_n([mine_out.reshape(4 * h_out, D_MODEL), got_out.reshape(4 * h_out, D_MODEL)],
                                  "chip_sum_out", (F32, BF16))
    chip_out = chip_out.reshape(4, h_out, D_MODEL)
    rs_in, rs_out = _chip_scatter([chip_in16, chip_out16.reshape(4, h_out, D_MODEL)], "scatter_grads")
    own_in = lax.dynamic_index_in_dim(chip_in, chip, axis=0, keepdims=False)
    own_out = lax.dynamic_index_in_dim(chip_out, chip, axis=0, keepdims=False)
    (half_in,) = _add_n([own_in, rs_in[0], rs_in[1], rs_in[2]], "reduce_in")
    (half_out,) = _add_n([own_out, rs_out[0], rs_out[1], rs_out[2]], "reduce_out")
    sib_in, sib_out = _sibling_swap([half_in, half_out], "swap_result")
    g_w_in_t = jnp.where(ic == 0, jnp.concatenate([half_in, sib_in], axis=1),
                         jnp.concatenate([sib_in, half_in], axis=1))[0:in_feats]
    g_w_out = jnp.where(ic == 0, jnp.concatenate([half_out, sib_out], axis=0),
                        jnp.concatenate([sib_out, half_out], axis=0))

    grads = dict(
        w_ada=g_w_ada[None], b_ada=g_b_ada, w_in=g_w_in_t, gla_w_gate_up=g_wg[None],
        gla_b_gate=small["gla_b_gate"].reshape(1, -1), gla_norm_w=small["gla_norm_w"].reshape(1, -1),
        gdn_conv_w=g_cw[None], gdn_a_log=small["gdn_a_log"].reshape(1, -1),
        gdn_dt_bias=small["gdn_dt_bias"].reshape(1, -1), gdn_norm_w=small["gdn_norm_w"].reshape(1, -1),
        w_out=g_w_out[None], ln_w=small["ln_w"].reshape(1, -1), ln_b=small["ln_b"].reshape(1, -1))
    weights = dict(w_ada=w_ada, b_ada=b_ada, w_in=w_in, gla_w_gate_up=gla_w_gate_up, gla_b_gate=gla_b_gate,
                   gla_norm_w=gla_norm_w, gdn_conv_w=gdn_conv_w, gdn_a_log=gdn_a_log, gdn_dt_bias=gdn_dt_bias,
                   gdn_norm_w=gdn_norm_w, w_out=w_out, ln_w=ln_w, ln_b=ln_b)
    m_in = dict(w_ada=m_w_ada, b_ada=m_b_ada, w_in=m_w_in, gla_w_gate_up=m_gla_w_gate_up, gla_b_gate=m_gla_b_gate,
                gla_norm_w=m_gla_norm_w, gdn_conv_w=m_gdn_conv_w, gdn_a_log=m_gdn_a_log, gdn_dt_bias=m_gdn_dt_bias,
                gdn_norm_w=m_gdn_norm_w, w_out=m_w_out, ln_w=m_ln_w, ln_b=m_ln_b)
    v_in = dict(w_ada=v_w_ada, b_ada=v_b_ada, w_in=v_w_in, gla_w_gate_up=v_gla_w_gate_up, gla_b_gate=v_gla_b_gate,
                gla_norm_w=v_gla_norm_w, gdn_conv_w=v_gdn_conv_w, gdn_a_log=v_gdn_a_log, gdn_dt_bias=v_gdn_dt_bias,
                gdn_norm_w=v_gdn_norm_w, w_out=v_w_out, ln_w=v_ln_w, ln_b=v_ln_b)
    names = list(weights)
    delta, new_m, new_v = {}, {}, {}
    for nm in names:
        shp = weights[nm].shape
        if nm == "w_in":
            to2d = lambda t: jnp.transpose(t[0])
            from2d = lambda t: jnp.transpose(t)[None]
            g2d = grads[nm]
        else:
            to2d = lambda t: t.reshape(-1, shp[-1])
            from2d = lambda t: t.reshape(shp)
            g2d = to2d(grads[nm])
        d, a, b = _adamw(to2d(weights[nm]), g2d, to2d(m_in[nm]), to2d(v_in[nm]), "adamw_" + nm)
        delta[nm], new_m[nm], new_v[nm] = from2d(d), from2d(a), from2d(b)
        grads[nm] = from2d(g2d)
    return (loss, grad_x, *[grads[k] for k in names], *[delta[k] for k in names],
            *[new_m[k] for k in names], *[new_v[k] for k in names])
```

```python
import functools

import jax
import jax.numpy as jnp
from jax import lax
from jax.experimental import pallas as pl
from jax.experimental.pallas import tpu as pltpu

F32 = jnp.float32
BF16 = jnp.bfloat16
HI = lax.Precision.HIGH
INV_PREC = None
MESH = pl.DeviceIdType.MESH

D_MODEL = 1024
GLA_HEADS = 4
GLA_DK = 64
GLA_DV = 128
GLA_QK = 256
GLA_WIDTH = 512
GLA_RANK = 16
GLA_GATE_NORM = 16.0
GDN_HEADS = 4
GDN_DK = 128
GDN_WIDTH = 512
CONV_K = 4
CHUNK = 64
LN_EPS = 1e-5
RMS_EPS = 1e-6
ALPHA = 2.0 ** 0.25
IN_COLS = 3608

LANE_A = GLA_RANK
LANE_B = GLA_RANK + GDN_HEADS
SMALL_USED = GLA_RANK + 2 * GDN_HEADS

ADAM_LR = 0.001
ADAM_B1 = 0.9
ADAM_B2 = 0.999
ADAM_EPS = 1e-08
ADAM_WD = 0.01
ADAM_STEP = 10

VMEM_LIMIT = 56 * 1024 * 1024


def _iota(shape, dim):
    return lax.broadcasted_iota(jnp.int32, shape, dim)


def _dot(a, b, prec=None):
    return lax.dot_general(a, b, (((1,), (0,)), ((), ())), precision=prec, preferred_element_type=F32)


def _dot_nt(a, b, prec=None):
    return lax.dot_general(a, b, (((1,), (1,)), ((), ())), precision=prec, preferred_element_type=F32)


def _dot_tn(a, b, prec=None):
    return lax.dot_general(a, b, (((0,), (0,)), ((), ())), precision=prec, preferred_element_type=F32)


def _log_sigmoid(z):
    return jnp.minimum(z, 0.0) - jnp.log1p(jnp.exp(-jnp.abs(z)))


def _softplus(z):
    return jnp.maximum(z, 0.0) + jnp.log1p(jnp.exp(-jnp.abs(z)))


def _silu(z):
    return z * jax.nn.sigmoid(z)


def _rms_gate(o, nw, og):
    return o * lax.rsqrt(jnp.mean(o * o, axis=-1, keepdims=True) + RMS_EPS) * nw * _silu(og)


def _params(*sem):
    return pltpu.CompilerParams(dimension_semantics=sem, vmem_limit_bytes=VMEM_LIMIT)


GLA_PAIRS = GLA_HEADS // 2


def _gla_chunk(qs, ks, lrs, vs, ogs, ss, wgs, bgs, nw):
    c = qs[0].shape[0]
    n_ep = len(ss)
    n_ex = n_ep // GLA_PAIRS
    n_chunks = len(qs) // n_ep
    pair_units = [(i // n_ep * n_ex + i % n_ep // GLA_PAIRS, i % GLA_PAIRS) for i in range(len(qs))]
    head_units = [(i // GLA_HEADS * GLA_PAIRS + i % GLA_HEADS // 2, i % 2) for i in range(len(vs))]
    row, col = _iota((c, c), 0), _iota((c, c), 1)
    causal = row >= col
    first_half = (_iota((c, 1), 0) < c // 2).astype(F32)
    lane = _iota((1, 128), 1)
    masks = [(lane < GLA_DK).astype(F32), (lane >= GLA_DK).astype(F32)]
    gs = [_log_sigmoid(_dot(lrs[ce], wgs[p]) + bgs[p]) * (1.0 / GLA_GATE_NORM) for ce, p in pair_units]
    bs = [_dot(causal.astype(F32), g, HI) for g in gs]
    b_ref = [jnp.sum(g * first_half, axis=0, keepdims=True) for g in gs]
    b_last = [jnp.sum(g, axis=0, keepdims=True) for g in gs]
    qsc = [q * (GLA_DK ** -0.5) for q in qs]
    qe = [q * jnp.exp(b - br) for q, b, br in zip(qsc, bs, b_ref)]
    ke = [k * jnp.exp(br - b) for k, b, br in zip(ks, bs, b_ref)]
    qb = [q * jnp.exp(b) for q, b in zip(qsc, bs)]
    kd = [k * jnp.exp(bl_ - b) for k, b, bl_ in zip(ks, bs, b_last)]
    decay = [jnp.exp(bl_) for bl_ in b_last]
    att = [jnp.where(causal, _dot_nt(qe[u] * masks[half], ke[u]), 0.0) for u, half in head_units]
    o_intra = [_dot(a, v) for a, v in zip(att, vs)]
    qbm = [qb[u] * masks[half] for u, half in head_units]
    kdm = [kd[u] * masks[half] for u, half in head_units]
    ys = []
    for r in range(n_chunks):
        heads_r = range(r * n_ex * GLA_HEADS, (r + 1) * n_ex * GLA_HEADS)
        o_inter = [_dot_nt(qbm[i], ss[head_units[i][0] - r * n_ep]) for i in heads_r]
        upd = [_dot_tn(vs[i], kdm[i]) for i in heads_r]
        ss = [s * decay[r * n_ep + j] + upd[2 * j] + upd[2 * j + 1] for j, s in enumerate(ss)]
        ys += [_rms_gate(o_intra[i] + oi, nw, ogs[i]) for i, oi in zip(heads_r, o_inter)]
    return ys, ss


def _unit_lower_inverse_chain(a_list):
    c = a_list[0].shape[0]
    eye = (_iota((c, c), 0) == _iota((c, c), 1)).astype(F32)
    ps = [-a for a in a_list]
    ts = [eye + p for p in ps]
    levels = max(c.bit_length() - 2, 0)
    if levels:
        ps = [_dot(p, p, INV_PREC) for p in ps]
    for level in range(levels):
        last = level == levels - 1
        both = [_dot(t if last else jnp.concatenate([t, p], axis=0), p, INV_PREC) for t, p in zip(ts, ps)]
        ts = [t + m[0:c] for t, m in zip(ts, both)]
        if not last:
            ps = [m[c:2 * c] for m in both]
    return ts


@jax.custom_vjp
def _unit_lower_inverse(a_list):
    return _unit_lower_inverse_chain(a_list)


def _unit_lower_inverse_fwd(a_list):
    ts = _unit_lower_inverse_chain(a_list)
    return ts, ts


def _unit_lower_inverse_bwd(ts, dts):
    xs = [_dot_nt(dt, t, INV_PREC) for dt, t in zip(dts, ts)]
    return ([-_dot_tn(t, x, INV_PREC) for t, x in zip(ts, xs)],)


_unit_lower_inverse.defvjp(_unit_lower_inverse_fwd, _unit_lower_inverse_bwd)


@jax.custom_vjp
def _unit_lower_inverse_known(a_list, ts):
    return ts


def _unit_lower_inverse_known_fwd(a_list, ts):
    return ts, ts


def _unit_lower_inverse_known_bwd(ts, dts):
    return _unit_lower_inverse_bwd(ts, dts) + ([jnp.zeros_like(t) for t in ts],)


_unit_lower_inverse_known.defvjp(_unit_lower_inverse_known_fwd, _unit_lower_inverse_known_bwd)


def _gdn_prep_units(qs, ks, vs, gbs, t_known=None):
    c = qs[0].shape[0]
    units = [divmod(i, GDN_HEADS) for i in range(len(qs))]
    row, col = _iota((c, c), 0), _iota((c, c), 1)
    causal, strict = row >= col, row > col
    if t_known is None:
        lane = _iota((1, 128), 1)
        d_alls = [_dot(causal.astype(F32), gb, HI) for gb in gbs]
        g_c, beta_c, d_c = [], [], []
        for r, h in units:
            sel_a = (lane == LANE_A + h).astype(F32)
            g_c.append(jnp.sum(gbs[r] * sel_a, axis=-1, keepdims=True))
            beta_c.append(jnp.sum(gbs[r] * (lane == LANE_B + h).astype(F32), axis=-1, keepdims=True))
            d_c.append(jnp.sum(d_alls[r] * sel_a, axis=-1, keepdims=True))
        d_diff = [jnp.broadcast_to(d, (c, c)) - jnp.broadcast_to(d, (c, c)).T for d in d_c]
    else:
        src = _iota((128, 128), 0)
        spread = jnp.concatenate([(src == base + h).astype(F32) for base in (LANE_A, LANE_B)
                                  for h in range(GDN_HEADS)], axis=1)
        width = GDN_HEADS * 128
        g_beta = [_dot(gb, spread, HI) for gb in gbs]
        d_alls = [_dot(causal.astype(F32), gbv[:, 0:width], HI) for gbv in g_beta]
        g_c = [g_beta[r][:, h * 128:(h + 1) * 128] for r, h in units]
        beta_c = [g_beta[r][:, width + h * 128:width + (h + 1) * 128] for r, h in units]
        d_c = [d_alls[r][:, h * 128:(h + 1) * 128] for r, h in units]
        d_diff = [d[:, 0:c] - d.T[0:c, :] for d in d_c]
    d_last = [jnp.sum(g, axis=0, keepdims=True) for g in g_c]
    decay_mat = [jnp.where(causal, jnp.exp(jnp.where(causal, dd, 0.0)), 0.0) for dd in d_diff]
    kb = [k * b for k, b in zip(ks, beta_c)]
    kbk_qk = [_dot_nt(jnp.concatenate([kbi, q], axis=0), k) for kbi, q, k in zip(kb, qs, ks)]
    a = [jnp.where(strict, m[0:c] * dm, 0.0) for m, dm in zip(kbk_qk, decay_mat)]
    qk = [jnp.where(causal, m[c:2 * c] * dm, 0.0) for m, dm in zip(kbk_qk, decay_mat)]
    t = _unit_lower_inverse(a) if t_known is None else _unit_lower_inverse_known(a, t_known)
    uw = [_dot(ti, jnp.concatenate([v * b, kbi * jnp.exp(d)], axis=1))
          for ti, v, b, kbi, d in zip(t, vs, beta_c, kb, d_c)]
    u = [m[:, 0:128] for m in uw]
    w = [m[:, 128:256] for m in uw]
    q_dec = [q * jnp.exp(d) for q, d in zip(qs, d_c)]
    k_dec = [k * jnp.exp(dl - d) for k, dl, d in zip(ks, d_last, d_c)]
    gamma = [jnp.exp(dl) for dl in d_last]
    return u, w, qk, q_dec, k_dec, gamma, t


def _sum_rows(t):
    return jnp.sum(t, axis=0, keepdims=True)


def _gdn_pre_elem(ps, ab, alog_v, dtb_v):
    outs = []
    for j, p in enumerate(ps):
        s = _silu(p)
        if j < 2 * GDN_HEADS:
            s = s * lax.rsqrt(jnp.sum(s * s, axis=-1, keepdims=True) + RMS_EPS)
        if j < GDN_HEADS:
            s = s * (GDN_DK ** -0.5)
        outs.append(s)
    lane = _iota((1, 128), 1)
    is_a = (lane >= LANE_A) & (lane < LANE_A + GDN_HEADS)
    is_b = (lane >= LANE_B) & (lane < LANE_B + GDN_HEADS)
    g = -jnp.exp(alog_v) * _softplus(ab + dtb_v)
    gb = jnp.where(is_a, g, jnp.where(is_b, jax.nn.sigmoid(ab), 0.0))
    return tuple(outs) + (gb,)


def _proj_fwd(x2, sc3, sh3, ws, seq, tm=512):
    n = x2.shape[0]
    tpe = seq // tm
    nw = len(ws)

    def body(x_ref, sc_ref, sh_ref, *refs):
        h = (x_ref[...] * sc_ref[0] + sh_ref[0]).astype(ws[0].dtype)
        for w_ref, o_ref in zip(refs[:nw], refs[nw:]):
            o_ref[...] = _dot_nt(h, w_ref[...])

    row = lambda i: (i, 0)
    per_ex = pl.BlockSpec((1, 1, D_MODEL), lambda i: (i // tpe, 0, 0))
    return pl.pallas_call(
        body, name="proj_fwd", grid=(n // tm,),
        in_specs=[pl.BlockSpec((tm, D_MODEL), row), per_ex, per_ex]
        + [pl.BlockSpec(w.shape, lambda i: (0, 0)) for w in ws],
        out_specs=[pl.BlockSpec((tm, w.shape[0]), row) for w in ws],
        out_shape=[jax.ShapeDtypeStruct((n, w.shape[0]), F32) for w in ws],
        compiler_params=_params("parallel"),
    )(x2, sc3, sh3, *ws)


GLA_SCAN_CHUNKS = 4


def _gla_operands(q_ref, k_ref, v_ref, og_ref, lr_ref, wg_ref, bg_ref, bl, r_per):
    chunks = [slice(r * CHUNK, (r + 1) * CHUNK) for r in range(r_per)]
    pair_cols = [slice(p * 128, (p + 1) * 128) for p in range(GLA_PAIRS)]
    head_cols = [slice(h * 128, (h + 1) * 128) for h in range(GLA_HEADS)]
    per_pair = lambda ref: [ref[e, rows, cols] for rows in chunks for e in range(bl) for cols in pair_cols]
    per_head = lambda ref: [ref[e, rows, cols] for rows in chunks for e in range(bl) for cols in head_cols]
    return (per_pair(q_ref), per_pair(k_ref), [lr_ref[e, rows, :] for rows in chunks for e in range(bl)],
            per_head(v_ref), per_head(og_ref)), ([wg_ref[:, cols] for cols in pair_cols],
                                                 [bg_ref[:, cols] for cols in pair_cols])


def _gla_fwd(pa, pd, wg, bg, nw, bl, seq):
    n = pa.shape[0]
    nc = seq // CHUNK
    r_per = GLA_SCAN_CHUNKS
    pairs = [(e, p) for e in range(bl) for p in range(GLA_PAIRS)]
    head_slots = [(slice(r * CHUNK, (r + 1) * CHUNK), e, slice(h * 128, (h + 1) * 128))
                  for r in range(r_per) for e in range(bl) for h in range(GLA_HEADS)]

    def body(q_ref, k_ref, v_ref, og_ref, lr_ref, wg_ref, bg_ref, nw_ref, y_ref, st_ref, s_scr):
        @pl.when(pl.program_id(0) == 0)
        def _():
            s_scr[...] = jnp.zeros_like(s_scr)

        ss = [s_scr[e, p] for e, p in pairs]
        for (e, p), s in zip(pairs, ss):
            st_ref[e, 0, p] = s
        acts, gate = _gla_operands(q_ref, k_ref, v_ref, og_ref, lr_ref, wg_ref, bg_ref, bl, r_per)
        ys, s_new = _gla_chunk(*acts, ss, *gate, nw_ref[...])
        for (rows, e, cols), y in zip(head_slots, ys):
            y_ref[e, rows, cols] = y.astype(y_ref.dtype)
        for (e, p), s in zip(pairs, s_new):
            s_scr[e, p] = s

    tok = lambda w, j: pl.BlockSpec((bl, r_per * CHUNK, w), lambda i: (0, i, j))
    const = lambda i: (0, 0)
    pa3 = pa.reshape(bl, seq, 1536)
    y, st = pl.pallas_call(
        body, name="gla_fwd", grid=(nc // r_per,),
        in_specs=[tok(256, 0), tok(256, 1), tok(512, 1), tok(512, 2), tok(128, 0),
                  pl.BlockSpec(wg.shape, const), pl.BlockSpec(bg.shape, const), pl.BlockSpec(nw.shape, const)],
        out_specs=[tok(512, 0), pl.BlockSpec((bl, 1, GLA_PAIRS, 128, 128), lambda i: (0, i, 0, 0, 0))],
        out_shape=[jax.ShapeDtypeStruct((bl, seq, 512), MM_DTYPE),
                   jax.ShapeDtypeStruct((bl, nc // r_per, GLA_PAIRS, 128, 128), F32)],
        scratch_shapes=[pltpu.VMEM((bl, GLA_PAIRS, 128, 128), F32)],
        compiler_params=_params("arbitrary"),
    )(pa3, pa3, pa3, pa3, pd.reshape(bl, seq, 128), wg, bg, nw)
    return y.reshape(n, 512), st


def _gla_bwd(pa, pd, st, dya, wg, bg, nw, bl, seq):
    n = pa.shape[0]
    nc = seq // CHUNK
    r_per = GLA_SCAN_CHUNKS
    steps = nc // r_per
    pairs = [(e, p) for e in range(bl) for p in range(GLA_PAIRS)]
    pair_cols = [slice(p * 128, (p + 1) * 128) for p in range(GLA_PAIRS)]
    chunks = [slice(r * CHUNK, (r + 1) * CHUNK) for r in range(r_per)]
    pair_slots = [(rows, e, p) for rows in chunks for e in range(bl) for p in range(GLA_PAIRS)]
    head_slots = [(rows, e, h) for rows in chunks for e in range(bl) for h in range(GLA_HEADS)]

    def body(q_ref, k_ref, v_ref, og_ref, lr_ref, st_ref, dy_ref, wg_ref, bg_ref, nw_ref,
             da_ref, dd_ref, dwg_ref, dbg_ref, dnw_ref, ds_scr):
        @pl.when(pl.program_id(0) == 0)
        def _():
            dwg_ref[...] = jnp.zeros_like(dwg_ref)
            dbg_ref[...] = jnp.zeros_like(dbg_ref)
            dnw_ref[...] = jnp.zeros_like(dnw_ref)
            ds_scr[...] = jnp.zeros_like(ds_scr)

        acts, gate = _gla_operands(q_ref, k_ref, v_ref, og_ref, lr_ref, wg_ref, bg_ref, bl, r_per)
        _, vjp = jax.vjp(_gla_chunk, *acts, [st_ref[e, 0, p] for e, p in pairs], *gate, nw_ref[...])
        dq, dk, dlr, dv, dog, ds, dwg, dbg, dnw = vjp(
            ([dy_ref[e, rows, h * 128:(h + 1) * 128] for rows, e, h in head_slots], [ds_scr[e, p] for e, p in pairs]))
        for i, (rows, e) in enumerate((rows, e) for rows in chunks for e in range(bl)):
            dd_ref[e, rows, :] = dlr[i]
        for i, (rows, e, p) in enumerate(pair_slots):
            da_ref[e, rows, pair_cols[p]] = dq[i].astype(da_ref.dtype)
            da_ref[e, rows, GLA_QK + p * 128:GLA_QK + (p + 1) * 128] = dk[i].astype(da_ref.dtype)
        for i, (rows, e, h) in enumerate(head_slots):
            da_ref[e, rows, 512 + h * 128:512 + (h + 1) * 128] = dv[i].astype(da_ref.dtype)
            da_ref[e, rows, 1024 + h * 128:1024 + (h + 1) * 128] = dog[i].astype(da_ref.dtype)
        for (e, p), d in zip(pairs, ds):
            ds_scr[e, p] = d
        for p, cols in enumerate(pair_cols):
            dwg_ref[:, cols] += dwg[p]
            dbg_ref[:, cols] += dbg[p]
        dnw_ref[...] += dnw

    tok = lambda w, j: pl.BlockSpec((bl, r_per * CHUNK, w), lambda i: (0, steps - 1 - i, j))
    const = lambda i: (0, 0)
    pa3 = pa.reshape(bl, seq, 1536)
    da, dd, dwg, dbg, dnw = pl.pallas_call(
        body, name="gla_bwd", grid=(steps,),
        in_specs=[tok(256, 0), tok(256, 1), tok(512, 1), tok(512, 2), tok(128, 0),
                  pl.BlockSpec((bl, 1, GLA_PAIRS, 128, 128), lambda i: (0, steps - 1 - i, 0, 0, 0)), tok(512, 0),
                  pl.BlockSpec(wg.shape, const), pl.BlockSpec(bg.shape, const), pl.BlockSpec(nw.shape, const)],
        out_specs=[tok(1536, 0), tok(128, 0),
                   pl.BlockSpec(wg.shape, const), pl.BlockSpec(bg.shape, const), pl.BlockSpec(nw.shape, const)],
        out_shape=[jax.ShapeDtypeStruct((bl, seq, 1536), MM_DTYPE), jax.ShapeDtypeStruct((bl, seq, 128), F32),
                   jax.ShapeDtypeStruct(wg.shape, F32), jax.ShapeDtypeStruct(bg.shape, F32),
                   jax.ShapeDtypeStruct(nw.shape, F32)],
        scratch_shapes=[pltpu.VMEM((bl, GLA_PAIRS, 128, 128), F32)],
        compiler_params=_params("arbitrary"),
    )(pa3, pa3, pa3, pa3, pd.reshape(bl, seq, 128), st, dya.reshape(bl, seq, 512), wg, bg, nw)
    return da.reshape(n, 1536), dd.reshape(n, 128), dwg, dbg, dnw


PRE_ROWS = 64
PRE_PIECES = [slice(j * 128, (j + 1) * 128) for j in range(3 * GDN_HEADS)]


def _rows_from(ref, start, rows, cols):
    lo = start // 8 * 8
    if lo == start:
        return ref[start:start + rows, cols]
    window = ref[lo:lo + rows + 8, cols]
    return pltpu.roll(window, rows + 8 - (start - lo), 0)[0:rows]


def _conv_taps(buf_ref, w_ref, base, rows, cols):
    acc = w_ref[0:1, cols] * _rows_from(buf_ref, base, rows, cols)
    for k in range(1, CONV_K):
        acc = acc + w_ref[k:k + 1, cols] * _rows_from(buf_ref, base + k, rows, cols)
    return acc


def _gdn_pre_fwd(pb, pd, cw8, alog_v, dtb_v, bl, seq, tm=512):
    n = pb.shape[0]
    tpe = seq // tm
    t8 = tm // 8

    def body(u_ref, prev_ref, ab_ref, w_ref, al_ref, dt_ref, qkv_ref, gb_ref, p_ref, buf):
        i = pl.program_id(0)
        keep = (i % tpe != 0).astype(F32)
        buf[0:8, :] = prev_ref[...] * keep
        buf[8:8 + tm, :] = u_ref[...]
        for r0 in range(0, tm, PRE_ROWS):
            rows = slice(r0, r0 + PRE_ROWS)
            ps = [_conv_taps(buf, w_ref, 8 - (CONV_K - 1) + r0, PRE_ROWS, cols) for cols in PRE_PIECES]
            outs = _gdn_pre_elem(ps, ab_ref[rows, :], al_ref[...], dt_ref[...])
            for cols, p, out in zip(PRE_PIECES, ps, outs):
                p_ref[rows, cols] = p
                qkv_ref[rows, cols] = out
            gb_ref[rows, :] = outs[len(PRE_PIECES)]

    row = lambda i: (i, 0)
    const = lambda i: (0, 0)
    return pl.pallas_call(
        body, name="gdn_pre_fwd", grid=(n // tm,),
        in_specs=[pl.BlockSpec((tm, 1536), row),
                  pl.BlockSpec((8, 1536), lambda i: (jnp.maximum(i * t8 - 1, 0), 0)),
                  pl.BlockSpec((tm, 128), row),
                  pl.BlockSpec((8, 1536), const), pl.BlockSpec((1, 128), const), pl.BlockSpec((1, 128), const)],
        out_specs=[pl.BlockSpec((tm, 1536), row), pl.BlockSpec((tm, 128), row), pl.BlockSpec((tm, 1536), row)],
        out_shape=[jax.ShapeDtypeStruct((n, 1536), F32), jax.ShapeDtypeStruct((n, 128), F32),
                   jax.ShapeDtypeStruct((n, 1536), F32)],
        scratch_shapes=[pltpu.VMEM((tm + 8, 1536), F32)],
        compiler_params=_params("parallel"),
    )(pb, pb, pd, cw8, alog_v, dtb_v)


def _gdn_pre_bwd(pb, conv_out, pd, dqkv, dgb, cw8, alog_v, dtb_v, bl, seq, tm=512):
    n = pb.shape[0]
    tpe = seq // tm
    t8 = tm // 8
    nb8 = n // 8
    ext = tm + 8

    def body(u_ref, p_ref, pn_ref, ab_ref, abn_ref, dq_ref, dqn_ref, dgb_ref, w_ref, al_ref, dt_ref,
             du_ref, dab_ref, dw_ref, dal_ref, ddt_ref, dpbuf):
        i = pl.program_id(0)

        @pl.when(i == 0)
        def _():
            dw_ref[...] = jnp.zeros_like(dw_ref)
            dal_ref[...] = jnp.zeros_like(dal_ref)
            ddt_ref[...] = jnp.zeros_like(ddt_ref)

        keep_next = (i % tpe != tpe - 1).astype(F32)
        zeros8 = jnp.zeros((8, 128), F32)
        dal, ddt = jnp.zeros((1, 128), F32), jnp.zeros((1, 128), F32)
        for r0 in range(0, tm, PRE_ROWS):
            rows = slice(r0, r0 + PRE_ROWS)
            last = r0 + PRE_ROWS == tm
            along = lambda own, extra: jnp.concatenate([own, extra], axis=0) if last else own
            ps = [along(p_ref[rows, cols], pn_ref[:, cols]) for cols in PRE_PIECES]
            ab = along(ab_ref[rows, :], abn_ref[...])
            _, vjp = jax.vjp(_gdn_pre_elem, ps, ab, al_ref[...], dt_ref[...])
            cts = tuple(along(dq_ref[rows, cols], dqn_ref[:, cols] * keep_next) for cols in PRE_PIECES)
            cts += (along(dgb_ref[rows, :], zeros8),)
            dps, dab, dal_r, ddt_r = vjp(cts)
            out_rows = slice(r0, r0 + PRE_ROWS + (8 if last else 0))
            for cols, dp in zip(PRE_PIECES, dps):
                dpbuf[out_rows, cols] = dp
            dab_ref[rows, :] = dab[0:PRE_ROWS, :]
            dal, ddt = dal + dal_r, ddt + ddt_r
        dal_ref[...] += dal
        ddt_ref[...] += ddt
        for cols in PRE_PIECES:
            dw = [jnp.zeros((1, 128), F32) for _ in range(CONV_K)]
            for r0 in range(0, tm, PRE_ROWS):
                u = u_ref[r0:r0 + PRE_ROWS, cols]
                du = None
                for k in range(CONV_K):
                    dp_k = _rows_from(dpbuf, r0 + CONV_K - 1 - k, PRE_ROWS, cols)
                    term = w_ref[k:k + 1, cols] * dp_k
                    du = term if du is None else du + term
                    dw[k] = dw[k] + jnp.sum(u * dp_k, axis=0, keepdims=True)
                du_ref[r0:r0 + PRE_ROWS, cols] = du.astype(du_ref.dtype)
            for k in range(CONV_K):
                dw_ref[k:k + 1, cols] += dw[k]

    row = lambda i: (i, 0)
    next8 = lambda i: (jnp.minimum((i + 1) * t8, nb8 - 1), 0)
    const = lambda i: (0, 0)
    return pl.pallas_call(
        body, name="gdn_pre_bwd", grid=(n // tm,),
        in_specs=[pl.BlockSpec((tm, 1536), row), pl.BlockSpec((tm, 1536), row), pl.BlockSpec((8, 1536), next8),
                  pl.BlockSpec((tm, 128), row), pl.BlockSpec((8, 128), next8),
                  pl.BlockSpec((tm, 1536), row), pl.BlockSpec((8, 1536), next8),
                  pl.BlockSpec((tm, 128), row),
                  pl.BlockSpec((8, 1536), const), pl.BlockSpec((1, 128), const), pl.BlockSpec((1, 128), const)],
        out_specs=[pl.BlockSpec((tm, 1536), row), pl.BlockSpec((tm, 128), row),
                   pl.BlockSpec((8, 1536), const), pl.BlockSpec((1, 128), const), pl.BlockSpec((1, 128), const)],
        out_shape=[jax.ShapeDtypeStruct((n, 1536), MM_DTYPE), jax.ShapeDtypeStruct((n, 128), F32),
                   jax.ShapeDtypeStruct((8, 1536), F32), jax.ShapeDtypeStruct((1, 128), F32),
                   jax.ShapeDtypeStruct((1, 128), F32)],
        scratch_shapes=[pltpu.VMEM((ext, 1536), F32)],
        compiler_params=_params("arbitrary"),
    )(pb, conv_out, conv_out, pd, pd, dqkv, dqkv, dgb, cw8, alog_v, dtb_v)


GDN_PREP_CHUNKS = 4
GDN_PREP_BWD_CHUNKS = 4
GDN_SCAN_CHUNKS = 8
MM_DTYPE = BF16


def _head_cols(ref, rows, base=0):
    return [ref[rows, base + h * 128:base + (h + 1) * 128] for h in range(GDN_HEADS)]


def _gdn_prep(qkv, gb):
    n = qkv.shape[0]
    r_per = GDN_PREP_CHUNKS
    tm = r_per * CHUNK

    def body(q_ref, k_ref, v_ref, gb_ref, u_ref, w_ref, qd_ref, kd_ref, qk_ref, t_ref, gam_ref):
        rowid = _iota((8, 128), 0)
        chunk_rows = [slice(r * CHUNK, (r + 1) * CHUNK) for r in range(r_per)]
        gather = lambda ref: [t for rows in chunk_rows for t in _head_cols(ref, rows)]
        u, w, qk, qd, kd, gamma, tinv = _gdn_prep_units(gather(q_ref), gather(k_ref), gather(v_ref),
                                                        [gb_ref[rows, :] for rows in chunk_rows])
        for r, rows in enumerate(chunk_rows):
            gam = jnp.zeros((8, 128), F32)
            for h in range(GDN_HEADS):
                i = r * GDN_HEADS + h
                cols = slice(h * 128, (h + 1) * 128)
                u_ref[rows, cols] = u[i]
                w_ref[rows, cols] = w[i].astype(MM_DTYPE)
                qd_ref[rows, cols] = qd[i].astype(MM_DTYPE)
                kd_ref[rows, cols] = kd[i].astype(MM_DTYPE)
                qk_ref[r, h] = qk[i].astype(MM_DTYPE)
                t_ref[r, h] = tinv[i].astype(MM_DTYPE)
                gam = jnp.where(rowid == h, gamma[i], gam)
            gam_ref[r] = gam

    tok = lambda j: pl.BlockSpec((tm, 512), lambda i: (i, j))
    return pl.pallas_call(
        body, name="gdn_prep", grid=(n // tm,),
        in_specs=[tok(0), tok(1), tok(2), pl.BlockSpec((tm, 128), lambda i: (i, 0))],
        out_specs=[tok(0)] * 4 + [pl.BlockSpec((r_per, GDN_HEADS, CHUNK, CHUNK), lambda i: (i, 0, 0, 0))] * 2
        + [pl.BlockSpec((r_per, 8, 128), lambda i: (i, 0, 0))],
        out_shape=[jax.ShapeDtypeStruct((n, 512), F32)] + [jax.ShapeDtypeStruct((n, 512), MM_DTYPE)] * 3
        + [jax.ShapeDtypeStruct((n // CHUNK, GDN_HEADS, CHUNK, CHUNK), MM_DTYPE)] * 2
        + [jax.ShapeDtypeStruct((n // CHUNK, 8, 128), F32)],
        compiler_params=_params("parallel"),
    )(qkv, qkv, qkv, gb)


def _gdn_fwd(qkv, gb, pc, nw, bl, seq):
    n = qkv.shape[0]
    nc = seq // CHUNK
    u, w, qd, kd, qk, tinv, gam = _gdn_prep(qkv, gb)
    tok3 = lambda t: t.reshape(bl, seq, 512)
    qk5 = qk.reshape(bl, nc, GDN_HEADS, CHUNK, CHUNK)
    gam4 = gam.reshape(bl, nc, 8, 128)

    r_per = GDN_SCAN_CHUNKS
    mm = lambda t: t.astype(MM_DTYPE)

    def body(u_ref, w_ref, qd_ref, kd_ref, qk_ref, gam_ref, og_ref, nw_ref, o_ref, y_ref, vn_ref, st_ref, s_scr):
        @pl.when(pl.program_id(0) == 0)
        def _():
            s_scr[...] = jnp.zeros_like(s_scr)

        units = [(b, h, slice(h * 128, (h + 1) * 128)) for b in range(bl) for h in range(GDN_HEADS)]
        ss = [s_scr[b, h] for b, h, _ in units]
        for r in range(r_per):
            rows = slice(r * CHUNK, (r + 1) * CHUNK)
            for (b, h, _), s in zip(units, ss):
                st_ref[b, r, h] = s
            ws_qs = [_dot(jnp.concatenate([w_ref[b, rows, cols], qd_ref[b, rows, cols]], axis=0), mm(s))
                     for (b, h, cols), s in zip(units, ss)]
            v_new = [u_ref[b, rows, cols] - m[0:CHUNK] for (b, h, cols), m in zip(units, ws_qs)]
            os_ = [m[CHUNK:2 * CHUNK] + _dot(qk_ref[b, r, h], mm(vn))
                   for (b, h, cols), m, vn in zip(units, ws_qs, v_new)]
            ss = [s * gam_ref[b, r, h:h + 1, :] + _dot_tn(kd_ref[b, rows, cols], mm(vn))
                  for (b, h, cols), s, vn in zip(units, ss, v_new)]
            for (b, h, cols), vn, o in zip(units, v_new, os_):
                vn_ref[b, rows, cols] = mm(vn)
                o_ref[b, rows, cols] = o
                y_ref[b, rows, cols] = mm(_rms_gate(o, nw_ref[...], og_ref[b, rows, cols]))
        for (b, h, _), s in zip(units, ss):
            s_scr[b, h] = s

    tok = pl.BlockSpec((bl, r_per * CHUNK, 512), lambda i: (0, i, 0))
    st_spec = pl.BlockSpec((bl, r_per, GDN_HEADS, 128, 128), lambda i: (0, i, 0, 0, 0))
    tok_shape = jax.ShapeDtypeStruct((bl, seq, 512), F32)
    o, y, vn, st = pl.pallas_call(
        body, name="gdn_scan_fwd", grid=(nc // r_per,),
        in_specs=[tok, tok, tok, tok,
                  pl.BlockSpec((bl, r_per, GDN_HEADS, CHUNK, CHUNK), lambda i: (0, i, 0, 0, 0)),
                  pl.BlockSpec((bl, r_per, 8, 128), lambda i: (0, i, 0, 0)), tok,
                  pl.BlockSpec(nw.shape, lambda i: (0, 0))],
        out_specs=[tok, tok, tok, st_spec],
        out_shape=[tok_shape, jax.ShapeDtypeStruct((bl, seq, 512), MM_DTYPE), jax.ShapeDtypeStruct((bl, seq, 512), MM_DTYPE),
                   jax.ShapeDtypeStruct((bl, nc, GDN_HEADS, 128, 128), F32)],
        scratch_shapes=[pltpu.VMEM((bl, GDN_HEADS, 128, 128), F32)],
        compiler_params=_params("arbitrary"),
    )(tok3(u), tok3(w), tok3(qd), tok3(kd), qk5, gam4, tok3(pc), nw)
    return y.reshape(n, 512), (o, st, w, qd, kd, qk5, gam4, tinv, vn)


def _gdn_bwd(qkv, gb, pc, res, dyb, nw, bl, seq):
    n = qkv.shape[0]
    nc = seq // CHUNK
    o, st, w, qd, kd, qk5, gam4, tinv, vn = res
    tok3 = lambda t: t.reshape(bl, seq, 512)

    def scan_body(dy_ref, o_ref, og_ref, w_ref, qd_ref, kd_ref, qk_ref, gam_ref, nw_ref,
                  do_ref, dog_ref, dvn_ref, dst_ref, dnw_ref, ds_scr):
        @pl.when(pl.program_id(0) == 0)
        def _():
            ds_scr[...] = jnp.zeros_like(ds_scr)
            dnw_ref[...] = jnp.zeros_like(dnw_ref)

        units = [(b, h, slice(h * 128, (h + 1) * 128)) for b in range(bl) for h in range(GDN_HEADS)]
        dnw = jnp.zeros(nw.shape, F32)
        dss = [ds_scr[b, h] for b, h, _ in units]
        for r in reversed(range(r_scan)):
            rows = slice(r * CHUNK, (r + 1) * CHUNK)
            d_os = []
            for b, h, cols in units:
                _, vjp = jax.vjp(_rms_gate, o_ref[b, rows, cols], nw_ref[...], og_ref[b, rows, cols])
                d_o, dnw_h, dog = vjp(dy_ref[b, rows, cols])
                do_ref[b, rows, cols] = mm(d_o)
                dog_ref[b, rows, cols] = mm(dog)
                dnw = dnw + dnw_h
                d_os.append(mm(d_o))
            for (b, h, _), ds in zip(units, dss):
                dst_ref[b, r, h] = ds
            dvn_a = [_dot(kd_ref[b, rows, cols], mm(ds)) for (b, h, cols), ds in zip(units, dss)]
            dvns = [a + _dot_tn(qk_ref[b, r, h], d_o) for (b, h, cols), a, d_o in zip(units, dvn_a, d_os)]
            for (b, h, cols), dvn in zip(units, dvns):
                dvn_ref[b, rows, cols] = mm(dvn)
            dss = [ds * gam_ref[b, r, h:h + 1, :] + _dot_tn(
                jnp.concatenate([qd_ref[b, rows, cols], w_ref[b, rows, cols]], axis=0),
                jnp.concatenate([d_o, mm(-dvn)], axis=0))
                for (b, h, cols), d_o, ds, dvn in zip(units, d_os, dss, dvns)]
        dnw_ref[...] += dnw
        for (b, h, _), ds in zip(units, dss):
            ds_scr[b, h] = ds

    r_scan = GDN_SCAN_CHUNKS
    mm = lambda t: t.astype(MM_DTYPE)
    rev = lambda i: nc // r_scan - 1 - i
    tok = pl.BlockSpec((bl, r_scan * CHUNK, 512), lambda i: (0, rev(i), 0))
    st_spec = pl.BlockSpec((bl, r_scan, GDN_HEADS, 128, 128), lambda i: (0, rev(i), 0, 0, 0))
    tok_shape = jax.ShapeDtypeStruct((bl, seq, 512), F32)
    tok_mm = jax.ShapeDtypeStruct((bl, seq, 512), MM_DTYPE)
    d_o, dog, dvn, dst, dnw = pl.pallas_call(
        scan_body, name="gdn_scan_bwd", grid=(nc // r_scan,),
        in_specs=[tok] * 6 + [pl.BlockSpec((bl, r_scan, GDN_HEADS, CHUNK, CHUNK), lambda i: (0, rev(i), 0, 0, 0)),
                              pl.BlockSpec((bl, r_scan, 8, 128), lambda i: (0, rev(i), 0, 0)),
                              pl.BlockSpec(nw.shape, lambda i: (0, 0))],
        out_specs=[tok, tok, tok, st_spec, pl.BlockSpec(nw.shape, lambda i: (0, 0))],
        out_shape=[tok_mm, tok_mm, tok_mm, jax.ShapeDtypeStruct(st.shape, F32),
                   jax.ShapeDtypeStruct(nw.shape, F32)],
        scratch_shapes=[pltpu.VMEM((bl, GDN_HEADS, 128, 128), F32)],
        compiler_params=_params("arbitrary"),
    )(tok3(dyb), o, tok3(pc), tok3(w), tok3(qd), tok3(kd), qk5, gam4, nw)

    r_per = GDN_PREP_BWD_CHUNKS
    tm = r_per * CHUNK

    def prep_body(q_ref, k_ref, v_ref, gb_ref, t_ref, st_ref, dst_ref, dvn_ref, do_ref, vn_ref, dqkv_ref, dgb_ref):
        chunk_rows = [slice(r * CHUNK, (r + 1) * CHUNK) for r in range(r_per)]
        gather = lambda ref: [t for rows in chunk_rows for t in _head_cols(ref, rows)]
        units = [(r, h) for r in range(r_per) for h in range(GDN_HEADS)]
        t_known = [t_ref[r, h].astype(F32) for r, h in units]
        prep = lambda q, k, v, g: _gdn_prep_units(q, k, v, g, t_known)[:6]
        _, vjp = jax.vjp(prep, gather(q_ref), gather(k_ref), gather(v_ref), [gb_ref[rows, :] for rows in chunk_rows])
        ss = [st_ref[r, h] for r, h in units]
        dss = [dst_ref[r, h] for r, h in units]
        dvns, d_os, v_new = gather(dvn_ref), gather(do_ref), gather(vn_ref)
        both = [_dot_nt(jnp.concatenate([dvn, d_o], axis=0), s.astype(MM_DTYPE)) for dvn, d_o, s in zip(dvns, d_os, ss)]
        d_w = [-m[0:CHUNK] for m in both]
        d_qd = [m[CHUNK:2 * CHUNK] for m in both]
        d_qk = [_dot_nt(d_o, vn) for d_o, vn in zip(d_os, v_new)]
        d_kd = [_dot_nt(vn, ds.astype(MM_DTYPE)) for vn, ds in zip(v_new, dss)]
        d_gam = [_sum_rows(ds * s) for ds, s in zip(dss, ss)]
        dq, dk, dv, dgb = vjp(([d.astype(F32) for d in dvns], d_w, d_qk, d_qd, d_kd, d_gam))
        for i, (r, h) in enumerate(units):
            rows = chunk_rows[r]
            for part, d in enumerate((dq, dk, dv)):
                dqkv_ref[rows, part * 512 + h * 128:part * 512 + (h + 1) * 128] = d[i]
        for r, rows in enumerate(chunk_rows):
            dgb_ref[rows, :] = dgb[r]

    tokp = lambda j: pl.BlockSpec((tm, 512), lambda i: (i, j))
    st4 = pl.BlockSpec((r_per, GDN_HEADS, 128, 128), lambda i: (i, 0, 0, 0))
    dqkv, dgb = pl.pallas_call(
        prep_body, name="gdn_prep_bwd", grid=(n // tm,),
        in_specs=[tokp(0), tokp(1), tokp(2), pl.BlockSpec((tm, 128), lambda i: (i, 0)),
                  pl.BlockSpec((r_per, GDN_HEADS, CHUNK, CHUNK), lambda i: (i, 0, 0, 0)), st4, st4,
                  tokp(0), tokp(0), tokp(0)],
        out_specs=[pl.BlockSpec((tm, 1536), lambda i: (i, 0)), pl.BlockSpec((tm, 128), lambda i: (i, 0))],
        out_shape=[jax.ShapeDtypeStruct((n, 1536), F32), jax.ShapeDtypeStruct((n, 128), F32)],
        compiler_params=_params("parallel"),
    )(qkv, qkv, qkv, gb, tinv, st.reshape(bl * nc, GDN_HEADS, 128, 128), dst.reshape(bl * nc, GDN_HEADS, 128, 128),
      dvn.reshape(n, 512), d_o.reshape(n, 512), vn.reshape(n, 512))
    return dqkv, dog.reshape(n, 512), dgb, dnw


def _out_block(x2, tgt2, ya, yb, g1p3, wo, lnw, lnb, seq, tm=512):
    n = x2.shape[0]
    tpe = seq // tm
    bl = n // seq

    def body(x_ref, t_ref, ya_ref, yb_ref, g_ref, wo_ref, lnw_ref, lnb_ref,
             dz_ref, dya_ref, dyb_ref, dwo_ref, dg_ref, glw_ref, glb_ref, loss_ref):
        i = pl.program_id(0)

        @pl.when(i == 0)
        def _():
            dwo_ref[...] = jnp.zeros_like(dwo_ref)
            glw_ref[...] = jnp.zeros_like(glw_ref)
            glb_ref[...] = jnp.zeros_like(glb_ref)
            loss_ref[...] = jnp.zeros_like(loss_ref)

        @pl.when(i % tpe == 0)
        def _():
            dg_ref[...] = jnp.zeros_like(dg_ref)

        ya16 = ya_ref[...].astype(wo.dtype)
        yb16 = yb_ref[...].astype(wo.dtype)
        wa = wo_ref[0:GLA_WIDTH, :]
        wb = wo_ref[GLA_WIDTH:, :]
        y = _dot(ya16, wa) + _dot(yb16, wb)
        g1p = g_ref[0]
        z = ALPHA * x_ref[...] + g1p * y
        mu = jnp.mean(z, axis=-1, keepdims=True)
        zc = z - mu
        rstd = lax.rsqrt(jnp.mean(zc * zc, axis=-1, keepdims=True) + LN_EPS)
        xhat = zc * rstd
        diff = xhat * lnw_ref[...] + lnb_ref[...] - t_ref[...]
        loss_ref[...] += (0.5 / D_MODEL) * jnp.sum(jnp.sum(diff * diff, axis=-1, keepdims=True), axis=0, keepdims=True)
        dout = diff * (1.0 / D_MODEL)
        glw_ref[...] += jnp.sum(dout * xhat, axis=0, keepdims=True)
        glb_ref[...] += jnp.sum(dout, axis=0, keepdims=True)
        dxh = dout * lnw_ref[...]
        dz = rstd * (dxh - jnp.mean(dxh, axis=-1, keepdims=True)
                     - xhat * jnp.mean(dxh * xhat, axis=-1, keepdims=True))
        dz_ref[...] = dz
        dg_ref[0] += jnp.sum(dz * y, axis=0, keepdims=True)
        dy = (g1p * dz).astype(wo.dtype)
        dya_ref[...] = _dot_nt(dy, wa)
        dyb_ref[...] = _dot_nt(dy, wb)
        dwo_ref[0:GLA_WIDTH, :] += _dot_tn(ya16, dy)
        dwo_ref[GLA_WIDTH:, :] += _dot_tn(yb16, dy)

    row = lambda i: (i, 0)
    const = lambda i: (0, 0)
    per_ex = pl.BlockSpec((1, 1, D_MODEL), lambda i: (i // tpe, 0, 0))
    return pl.pallas_call(
        body, name="out_block", grid=(n // tm,),
        in_specs=[pl.BlockSpec((tm, D_MODEL), row), pl.BlockSpec((tm, D_MODEL), row),
                  pl.BlockSpec((tm, 512), row), pl.BlockSpec((tm, 512), row), per_ex,
                  pl.BlockSpec((D_MODEL, D_MODEL), const), pl.BlockSpec((1, D_MODEL), const),
                  pl.BlockSpec((1, D_MODEL), const)],
        out_specs=[pl.BlockSpec((tm, D_MODEL), row), pl.BlockSpec((tm, 512), row), pl.BlockSpec((tm, 512), row),
                   pl.BlockSpec((D_MODEL, D_MODEL), const), per_ex,
                   pl.BlockSpec((1, D_MODEL), const), pl.BlockSpec((1, D_MODEL), const),
                   pl.BlockSpec((1, 1), const)],
        out_shape=[jax.ShapeDtypeStruct((n, D_MODEL), F32), jax.ShapeDtypeStruct((n, 512), F32),
                   jax.ShapeDtypeStruct((n, 512), F32), jax.ShapeDtypeStruct((D_MODEL, D_MODEL), F32),
                   jax.ShapeDtypeStruct((bl, 1, D_MODEL), F32), jax.ShapeDtypeStruct((1, D_MODEL), F32),
                   jax.ShapeDtypeStruct((1, D_MODEL), F32), jax.ShapeDtypeStruct((1, 1), F32)],
        compiler_params=_params("arbitrary"),
    )(x2, tgt2, ya, yb, g1p3, wo, lnw, lnb)


def _proj_bwd_x(ds, ws, x2, dz, sc3, seq, tm=512):
    n = x2.shape[0]
    tpe = seq // tm
    bl = n // seq

    def body(da_ref, db_ref, dc_ref, dd1_ref, dd2_ref, wa_ref, wb_ref, wc_ref, wd_ref, x_ref, dz_ref, sc_ref,
             gx_ref, dsh_ref, dsc_ref):
        i = pl.program_id(0)

        @pl.when(i % tpe == 0)
        def _():
            dsh_ref[...] = jnp.zeros_like(dsh_ref)
            dsc_ref[...] = jnp.zeros_like(dsc_ref)

        cdt = ws[0].dtype
        dh = _dot(da_ref[...].astype(cdt), wa_ref[...])
        dh += _dot(db_ref[...].astype(cdt), wb_ref[...])
        dh += _dot(dc_ref[...].astype(cdt), wc_ref[...])
        dh += _dot((dd1_ref[...] + dd2_ref[...]).astype(cdt), wd_ref[...])
        gx_ref[...] = dh * sc_ref[0] + ALPHA * dz_ref[...]
        dsh_ref[0] += jnp.sum(dh, axis=0, keepdims=True)
        dsc_ref[0] += jnp.sum(dh * x_ref[...], axis=0, keepdims=True)

    row = lambda i: (i, 0)
    const = lambda i: (0, 0)
    per_ex = pl.BlockSpec((1, 1, D_MODEL), lambda i: (i // tpe, 0, 0))
    da, db, dc, (dd1, dd2) = ds
    return pl.pallas_call(
        body, name="proj_bwd_x", grid=(n // tm,),
        in_specs=[pl.BlockSpec((tm, d.shape[1]), row) for d in (da, db, dc, dd1, dd2)]
        + [pl.BlockSpec(w.shape, const) for w in ws]
        + [pl.BlockSpec((tm, D_MODEL), row), pl.BlockSpec((tm, D_MODEL), row), per_ex],
        out_specs=[pl.BlockSpec((tm, D_MODEL), row), per_ex, per_ex],
        out_shape=[jax.ShapeDtypeStruct((n, D_MODEL), F32), jax.ShapeDtypeStruct((bl, 1, D_MODEL), F32),
                   jax.ShapeDtypeStruct((bl, 1, D_MODEL), F32)],
        compiler_params=_params("arbitrary"),
    )(da, db, dc, dd1, dd2, *ws, x2, dz, sc3)


def _proj_bwd_w(x2, sc3, sh3, ds, seq, cdt, name, tm=1024):
    n = x2.shape[0]
    tpe = seq // tm
    flat, groups = [], []
    for d in ds:
        parts = d if isinstance(d, tuple) else (d,)
        groups.append(len(parts))
        flat.extend(parts)
    nin = len(flat)

    def body(x_ref, sc_ref, sh_ref, *refs):
        i = pl.program_id(0)
        outs = refs[nin:]

        @pl.when(i == 0)
        def _():
            for o in outs:
                o[...] = jnp.zeros_like(o)

        h = (x_ref[...] * sc_ref[0] + sh_ref[0]).astype(cdt)
        pos = 0
        for o, cnt in zip(outs, groups):
            d = refs[pos][...]
            for extra in refs[pos + 1:pos + cnt]:
                d = d + extra[...]
            pos += cnt
            o[...] += _dot_tn(d.astype(cdt), h)

    row = lambda i: (i, 0)
    const = lambda i: (0, 0)
    per_ex = pl.BlockSpec((1, 1, D_MODEL), lambda i: (i // tpe, 0, 0))
    widths = [(d[0] if isinstance(d, tuple) else d).shape[1] for d in ds]
    return pl.pallas_call(
        body, name=name, grid=(n // tm,),
        in_specs=[pl.BlockSpec((tm, D_MODEL), row), per_ex, per_ex]
        + [pl.BlockSpec((tm, d.shape[1]), row) for d in flat],
        out_specs=[pl.BlockSpec((w, D_MODEL), const) for w in widths],
        out_shape=[jax.ShapeDtypeStruct((w, D_MODEL), F32) for w in widths],
        compiler_params=_params("arbitrary"),
    )(x2, sc3, sh3, *flat)


def _mod_block(c_all, w_ada_sh, b_blk):
    def body(c_ref, w_ref, b_ref, o_ref):
        o_ref[...] = _dot(c_ref[...], w_ref[...]) + b_ref[...]

    return pl.pallas_call(
        body, name="mod_block",
        out_shape=jax.ShapeDtypeStruct((c_all.shape[0], w_ada_sh.shape[1]), F32),
        compiler_params=pltpu.CompilerParams(vmem_limit_bytes=VMEM_LIMIT),
    )(c_all, w_ada_sh, b_blk)


def _ada_grads(c_all, dmod_all, dmod_blk):
    def body(c_ref, da_ref, db_ref, gw_ref, gb_ref):
        gw_ref[...] = _dot_tn(c_ref[...], db_ref[...])
        gb_ref[...] = jnp.sum(da_ref[...], axis=0, keepdims=True)

    return pl.pallas_call(
        body, name="ada_grads",
        out_shape=[jax.ShapeDtypeStruct((c_all.shape[1], dmod_blk.shape[1]), F32),
                   jax.ShapeDtypeStruct((1, dmod_all.shape[1]), F32)],
        compiler_params=pltpu.CompilerParams(vmem_limit_bytes=VMEM_LIMIT),
    )(c_all, dmod_all, dmod_blk)


def _sum_leading(parts, name):
    def body(p_ref, o_ref):
        acc = p_ref[0]
        for d in range(1, parts.shape[0]):
            acc = acc + p_ref[d]
        o_ref[...] = acc

    return pl.pallas_call(
        body, name=name, out_shape=jax.ShapeDtypeStruct(parts.shape[1:], F32),
        compiler_params=pltpu.CompilerParams(vmem_limit_bytes=VMEM_LIMIT),
    )(parts)


ELEMENTWISE_BLOCK_BYTES = 2 * 1024 * 1024


def _tile2d(rows, cols, row_align=8):
    if rows * cols * 4 <= ELEMENTWISE_BLOCK_BYTES:
        return rows, cols
    fits = [t for t in range(row_align, rows, row_align) if rows % t == 0 and t * cols * 4 <= ELEMENTWISE_BLOCK_BYTES]
    if fits:
        return fits[-1], cols
    fits = [t for t in range(128, cols, 128) if cols % t == 0 and rows * t * 4 <= ELEMENTWISE_BLOCK_BYTES]
    assert fits, (rows, cols)
    return rows, fits[-1]


def _add_n(arrs, name, out_dtypes=(F32,)):
    rows, cols = arrs[0].shape
    narrow = any(jnp.dtype(dt).itemsize < 4 for dt in tuple(out_dtypes) + tuple(a.dtype for a in arrs))
    tr, tc = _tile2d(rows, cols, 16 if narrow else 8)
    n_in = len(arrs)

    def body(*refs):
        acc = refs[0][...].astype(F32)
        for r in refs[1:n_in]:
            acc = acc + r[...].astype(F32)
        for o in refs[n_in:]:
            o[...] = acc.astype(o.dtype)

    spec = pl.BlockSpec((tr, tc), lambda i, j: (i, j))
    return pl.pallas_call(
        body, name=name, grid=(rows // tr, cols // tc), in_specs=[spec] * n_in, out_specs=[spec] * len(out_dtypes),
        out_shape=[jax.ShapeDtypeStruct((rows, cols), dt) for dt in out_dtypes],
        compiler_params=_params("parallel", "parallel"),
    )(*arrs)


def _chip_sum_blocks(a, b, per, blocks, name, chunk=128):
    rows, cols = a.shape
    padded = -(-per // 16) * 16
    assert rows >= (blocks - 1) * per + padded, (rows, per, blocks)

    def body(a_ref, b_ref, o_ref, o16_ref):
        for j in range(blocks):
            for r0 in range(0, padded, chunk):
                n_rows = min(chunk, padded - r0)
                src = pl.ds(j * per + r0, n_rows)
                s = a_ref[src, :] + b_ref[src, :]
                if per - r0 < n_rows:
                    s = jnp.where(_iota((n_rows, 1), 0) < per - r0, s, 0.0)
                o_ref[j, r0:r0 + n_rows, :] = s
                o16_ref[j, r0:r0 + n_rows, :] = s.astype(BF16)

    return pl.pallas_call(
        body, name=name,
        out_shape=[jax.ShapeDtypeStruct((blocks, padded, cols), F32), jax.ShapeDtypeStruct((blocks, padded, cols), BF16)],
        compiler_params=pltpu.CompilerParams(vmem_limit_bytes=VMEM_LIMIT),
    )(a, b)


GRAD_PAD_ROWS = 16


def _adamw_rule(w, g, m, v):
    c1 = 1.0 / (1.0 - ADAM_B1 ** ADAM_STEP)
    c2 = 1.0 / (1.0 - ADAM_B2 ** ADAM_STEP)
    nm = ADAM_B1 * m + (1.0 - ADAM_B1) * g
    nv = ADAM_B2 * v + (1.0 - ADAM_B2) * (g * g)
    return -ADAM_LR * ((nm * c1) / (jnp.sqrt(nv * c2) + ADAM_EPS) + ADAM_WD * w), nm, nv


def _adamw_many(ws, gs, ms, vs, name):
    k = len(ws)

    def body(*refs):
        ins, outs = refs[:4 * k], refs[4 * k:]
        for i in range(k):
            res = _adamw_rule(ins[i][...], ins[k + i][...], ins[2 * k + i][...], ins[3 * k + i][...])
            for j in range(3):
                outs[3 * i + j][...] = res[j]

    return pl.pallas_call(
        body, name=name,
        out_shape=[jax.ShapeDtypeStruct(w.shape, F32) for w in ws for _ in range(3)],
        compiler_params=pltpu.CompilerParams(vmem_limit_bytes=VMEM_LIMIT),
    )(*ws, *gs, *ms, *vs)


def _adamw(w, g, m, v, name):
    rows, cols = w.shape
    tr, tc = _tile2d(rows, cols)
    c1 = 1.0 / (1.0 - ADAM_B1 ** ADAM_STEP)
    c2 = 1.0 / (1.0 - ADAM_B2 ** ADAM_STEP)

    def body(w_ref, g_ref, m_ref, v_ref, d_ref, nm_ref, nv_ref):
        gg = g_ref[...]
        nm = ADAM_B1 * m_ref[...] + (1.0 - ADAM_B1) * gg
        nv = ADAM_B2 * v_ref[...] + (1.0 - ADAM_B2) * (gg * gg)
        nm_ref[...] = nm
        nv_ref[...] = nv
        d_ref[...] = -ADAM_LR * ((nm * c1) / (jnp.sqrt(nv * c2) + ADAM_EPS) + ADAM_WD * w_ref[...])

    spec = pl.BlockSpec((tr, tc), lambda i, j: (i, j))
    shp = jax.ShapeDtypeStruct((rows, cols), F32)
    return pl.pallas_call(
        body, name=name, grid=(rows // tr, cols // tc), in_specs=[spec] * 4, out_specs=[spec] * 3,
        out_shape=[shp, shp, shp], compiler_params=_params("parallel", "parallel"),
    )(w, g, m, v)


def _coords():
    return lax.axis_index("x"), lax.axis_index("y"), lax.axis_index("c")


def _all_gather8(blk, name):
    m_per, n = blk.shape

    def body(x_ref, out_ref, send_sems, recv_sems, local_sem):
        x, y, c = _coords()
        me, sibling = (x, y, c), (x, y, 1 - c)
        chips = [(1 - x, y), (x, 1 - y), (1 - x, 1 - y)]

        def rows(px, py, pc):
            return out_ref.at[pl.ds((4 * px + 2 * py + pc) * m_per, m_per), :]

        def copy(k, block, to, src=None):
            return pltpu.make_async_remote_copy(
                src_ref=rows(*block) if src is None else src, dst_ref=rows(*block),
                send_sem=send_sems.at[k], recv_sem=recv_sems.at[k], device_id=to, device_id_type=MESH)

        mine = pltpu.make_async_copy(x_ref, rows(*me), local_sem)
        mine.start()
        first = [copy(0, me, sibling, src=x_ref)]
        first += [copy(1 + j, me, (*chip, c), src=x_ref) for j, chip in enumerate(chips)]
        for cp in first:
            cp.start()
        passed = [copy(4 + j, (*chip, c), sibling) for j, chip in enumerate(chips)]
        for j, chip in enumerate(chips):
            copy(1 + j, (*chip, c), me).wait_recv()
            passed[j].start()
        copy(0, sibling, me).wait_recv()
        for j, chip in enumerate(chips):
            copy(4 + j, (*chip, 1 - c), me).wait_recv()
        for cp in first + passed:
            cp.wait_send()
        mine.wait()

    return pl.pallas_call(
        body, name=name,
        out_shape=jax.ShapeDtypeStruct((8 * m_per, n), blk.dtype),
        in_specs=[pl.BlockSpec(memory_space=pltpu.VMEM)],
        out_specs=pl.BlockSpec(memory_space=pltpu.VMEM),
        scratch_shapes=[pltpu.SemaphoreType.DMA((7,)), pltpu.SemaphoreType.DMA((7,)), pltpu.SemaphoreType.DMA],
        compiler_params=pltpu.CompilerParams(vmem_limit_bytes=VMEM_LIMIT),
    )(blk)


def _chip_gather(shards, split, name):
    k_arr = len(shards)

    def body(*refs):
        srcs, dsts = refs[:k_arr], refs[k_arr:2 * k_arr]
        send_sems, recv_sems, fwd_send_sems, fwd_recv_sems, local_sems = refs[2 * k_arr:]
        x, y, c = _coords()
        peers = [(1 - x, y, c), (x, 1 - y, c), (1 - x, 1 - y, c)]
        sibling = (x, y, 1 - c)
        me_chip = 2 * x + y

        def part(ref, a, core):
            if not split[a]:
                return ref
            half = shards[a].shape[1] // 2
            return ref.at[:, pl.ds(core * half, half)]

        def ici(a, j, src_chip, dst_dev):
            return pltpu.make_async_remote_copy(
                src_ref=part(srcs[a], a, c), dst_ref=part(dsts[a].at[src_chip], a, c),
                send_sem=send_sems.at[a, j], recv_sem=recv_sems.at[a, j], device_id=dst_dev, device_id_type=MESH)

        def d2d(a, j, src_chip, core):
            return pltpu.make_async_remote_copy(
                src_ref=part(dsts[a].at[src_chip], a, core), dst_ref=part(dsts[a].at[src_chip], a, core),
                send_sem=fwd_send_sems.at[a, j], recv_sem=fwd_recv_sems.at[a, j],
                device_id=sibling, device_id_type=MESH)

        local = [pltpu.make_async_copy(srcs[a], dsts[a].at[me_chip], local_sems.at[a]) for a in range(k_arr)]
        for cp in local:
            cp.start()
        sends = [ici(a, j, me_chip, peer) for a in range(k_arr) for j, peer in enumerate(peers)]
        for cp in sends:
            cp.start()
        forwards = []
        for a in range(k_arr):
            for j, peer in enumerate(peers):
                peer_chip = 2 * peer[0] + peer[1]
                ici(a, j, peer_chip, peer).wait_recv()
                if split[a]:
                    forwards.append(d2d(a, j, peer_chip, c))
                    forwards[-1].start()
        for a in range(k_arr):
            for j, peer in enumerate(peers):
                if split[a]:
                    d2d(a, j, 2 * peer[0] + peer[1], 1 - c).wait_recv()
        for cp in sends + forwards:
            cp.wait_send()
        for cp in local:
            cp.wait()

    any_spec = pl.BlockSpec(memory_space=pl.ANY)
    return pl.pallas_call(
        body, name=name,
        out_shape=[jax.ShapeDtypeStruct((4,) + s.shape, s.dtype) for s in shards],
        in_specs=[any_spec] * k_arr, out_specs=[any_spec] * k_arr,
        scratch_shapes=[pltpu.SemaphoreType.DMA((k_arr, 3))] * 4 + [pltpu.SemaphoreType.DMA((k_arr,))],
    )(*shards)


def _chip_scatter(pieces, name):
    k_arr = len(pieces)

    def body(*refs):
        srcs, dsts = refs[:k_arr], refs[k_arr:2 * k_arr]
        send_sems, recv_sems = refs[2 * k_arr:]
        x, y, c = _coords()
        peers = [(1 - x, y, c), (x, 1 - y, c), (1 - x, 1 - y, c)]
        copies = []
        for a in range(k_arr):
            for j, peer in enumerate(peers):
                copies.append(pltpu.make_async_remote_copy(
                    src_ref=srcs[a].at[2 * peer[0] + peer[1]], dst_ref=dsts[a].at[j],
                    send_sem=send_sems.at[a, j], recv_sem=recv_sems.at[a, j], device_id=peer, device_id_type=MESH))
        for cp in copies:
            cp.start()
        for cp in copies:
            cp.wait_recv()
        for cp in copies:
            cp.wait_send()

    any_spec = pl.BlockSpec(memory_space=pl.ANY)
    return pl.pallas_call(
        body, name=name,
        out_shape=[jax.ShapeDtypeStruct((3,) + p.shape[1:], p.dtype) for p in pieces],
        in_specs=[any_spec] * k_arr, out_specs=[any_spec] * k_arr,
        scratch_shapes=[pltpu.SemaphoreType.DMA((k_arr, 3)), pltpu.SemaphoreType.DMA((k_arr, 3))],
    )(*pieces)


def _sibling_swap(arrs, name):
    k_arr = len(arrs)

    def body(*refs):
        srcs, dsts = refs[:k_arr], refs[k_arr:2 * k_arr]
        send_sems, recv_sems = refs[2 * k_arr:]
        x, y, c = _coords()
        copies = [pltpu.make_async_remote_copy(
            src_ref=srcs[a], dst_ref=dsts[a], send_sem=send_sems.at[a], recv_sem=recv_sems.at[a],
            device_id=(x, y, 1 - c), device_id_type=MESH) for a in range(k_arr)]
        for cp in copies:
            cp.start()
        for cp in copies:
            cp.wait_recv()
        for cp in copies:
            cp.wait_send()

    any_spec = pl.BlockSpec(memory_space=pl.ANY)
    return pl.pallas_call(
        body, name=name,
        out_shape=[jax.ShapeDtypeStruct(a.shape, a.dtype) for a in arrs],
        in_specs=[any_spec] * k_arr, out_specs=[any_spec] * k_arr,
        scratch_shapes=[pltpu.SemaphoreType.DMA((k_arr,)), pltpu.SemaphoreType.DMA((k_arr,))],
    )(*arrs)


def _split_w_in(w_in_t):
    wa = jnp.concatenate([w_in_t[0:1024], w_in_t[1040:1552]], axis=0)
    wb = w_in_t[1552:3088]
    wc = w_in_t[3096:3608]
    wd = jnp.concatenate([w_in_t[1024:1040], w_in_t[3088:3096],
                          jnp.zeros((128 - SMALL_USED, w_in_t.shape[1]), w_in_t.dtype)], axis=0)
    return wa, wb, wc, wd


def _merge_dw_in(dwa, dwb, dwc, dwd):
    return jnp.concatenate([dwa[0:1024], dwd[0:GLA_RANK], dwa[1024:1536], dwb, dwd[GLA_RANK:SMALL_USED], dwc,
                            jnp.zeros((GRAD_PAD_ROWS, dwa.shape[1]), dwa.dtype)], axis=0)


def _local_step(x, mod, w_in16, w_out16, gla_wg, gla_bg, gla_nw, conv_w, a_log, dt_bias, gdn_nw, ln_w, ln_b, tgt):
    bl, seq, _ = x.shape
    n = bl * seq
    x2 = x.reshape(n, D_MODEL)
    tgt2 = tgt.reshape(n, D_MODEL)
    sh3 = mod[:, None, 0:D_MODEL]
    sc3 = 1.0 + mod[:, None, D_MODEL:2 * D_MODEL]
    g1p3 = 1.0 + mod[:, None, 2 * D_MODEL:]
    ws = _split_w_in(w_in16)
    wg = jnp.concatenate([gla_wg, jnp.zeros((128 - GLA_RANK, GLA_QK), F32)], axis=0)
    cw8 = jnp.concatenate([conv_w, jnp.zeros((8 - CONV_K, conv_w.shape[1]), F32)], axis=0)
    alog_v = jnp.zeros((1, 128), F32).at[:, LANE_A:LANE_A + GDN_HEADS].set(a_log)
    dtb_v = jnp.zeros((1, 128), F32).at[:, LANE_A:LANE_A + GDN_HEADS].set(dt_bias)

    pa, pb, pc, pd = _proj_fwd(x2, sc3, sh3, ws, seq)
    ya, st_a = _gla_fwd(pa, pd, wg, gla_bg, gla_nw, bl, seq)
    qkv, gb, conv_out = _gdn_pre_fwd(pb, pd, cw8, alog_v, dtb_v, bl, seq)
    yb, st_b = _gdn_fwd(qkv, gb, pc, gdn_nw, bl, seq)
    dz, dya, dyb, d_wo, d_gate, d_lnw, d_lnb, loss = _out_block(x2, tgt2, ya, yb, g1p3, w_out16, ln_w, ln_b, seq)
    da, dd1, d_wg, d_bg, d_nwa = _gla_bwd(pa, pd, st_a, dya, wg, gla_bg, gla_nw, bl, seq)
    dqkv, dc, dgb, d_nwb = _gdn_bwd(qkv, gb, pc, st_b, dyb, gdn_nw, bl, seq)
    db, dd2, d_cw8, d_alog, d_dtb = _gdn_pre_bwd(pb, conv_out, pd, dqkv, dgb, cw8, alog_v, dtb_v, bl, seq)
    gx, d_sh, d_sc = _proj_bwd_x((da, db, dc, (dd1, dd2)), ws, x2, dz, sc3, seq)
    (dwa,) = _proj_bwd_w(x2, sc3, sh3, [da], seq, w_in16.dtype, "proj_bwd_w_a")
    dwb, dwc, dwd = _proj_bwd_w(x2, sc3, sh3, [db, dc, (dd1, dd2)], seq, w_in16.dtype, "proj_bwd_w_bcd")
    grads = dict(
        w_in=_merge_dw_in(dwa, dwb, dwc, dwd),
        w_out=d_wo,
        gla_w_gate_up=d_wg[0:GLA_RANK, :],
        gla_b_gate=d_bg,
        gla_norm_w=d_nwa,
        gdn_conv_w=d_cw8[0:CONV_K, :],
        gdn_a_log=d_alog[:, LANE_A:LANE_A + GDN_HEADS],
        gdn_dt_bias=d_dtb[:, LANE_A:LANE_A + GDN_HEADS],
        gdn_norm_w=d_nwb,
        ln_w=d_lnw,
        ln_b=d_lnb,
        mod=jnp.concatenate([d_sh[:, 0, :], d_sc[:, 0, :], d_gate[:, 0, :]], axis=1),
    )
    return loss, gx.reshape(bl, seq, D_MODEL), grads


_SMALL = (("gla_b_gate", 256), ("gla_norm_w", 128), ("gdn_a_log", 4), ("gdn_dt_bias", 4), ("gdn_norm_w", 128),
          ("ln_w", 1024), ("ln_b", 1024), ("gla_w_gate_up", 16 * 256), ("gdn_conv_w", 4 * 1536), ("loss", 1),
          ("mod", 2 * 3072))


def _pack_small(grads):
    flat = jnp.concatenate([grads[k].reshape(-1) for k, _ in _SMALL])
    total = sum(sz for _, sz in _SMALL)
    rows = -(-total // 1024) * 8
    return jnp.concatenate([flat, jnp.zeros((rows * 128 - total,), F32)]).reshape(rows, 128)


def _unpack_small(flat):
    out, pos = {}, 0
    for k, sz in _SMALL:
        out[k] = flat[pos:pos + sz]
        pos += sz
    return out


def kernel(x, c, w_ada, b_ada, w_in, gla_w_gate_up, gla_b_gate, gla_norm_w, gdn_conv_w, gdn_a_log, gdn_dt_bias, gdn_norm_w, w_out, ln_w, ln_b, loss_target, m_w_ada, m_b_ada, m_w_in, m_gla_w_gate_up, m_gla_b_gate, m_gla_norm_w, m_gdn_conv_w, m_gdn_a_log, m_gdn_dt_bias, m_gdn_norm_w, m_w_out, m_ln_w, m_ln_b, v_w_ada, v_b_ada, v_w_in, v_gla_w_gate_up, v_gla_b_gate, v_gla_norm_w, v_gdn_conv_w, v_gdn_a_log, v_gdn_dt_bias, v_gdn_norm_w, v_w_out, v_ln_w, v_ln_b):
    ix, iy, ic = _coords()
    chip = 2 * ix + iy
    dev = 4 * ix + 2 * iy + ic
    bl = x.shape[0]
    ndev = 8

    c_all = _all_gather8(c.reshape(8, -1), "gather_c").reshape(ndev * bl, D_MODEL)
    ada_cols = w_ada.shape[2]
    b_blk = lax.dynamic_slice_in_dim(b_ada, chip * ada_cols, ada_cols, axis=1)
    mod_blk = _mod_block(c_all, w_ada[0], b_blk)
    mod_g = _all_gather8(mod_blk, "gather_mod").reshape(ndev, ndev * bl, ada_cols)
    mod_all = jnp.concatenate([mod_g[2 * j] for j in range(4)], axis=1)
    mod = lax.dynamic_slice_in_dim(mod_all, dev * bl, bl, axis=0)

    w_in_g, w_out_g, wg_g, cw_g = _chip_gather(
        [jnp.transpose(w_in[0]).astype(BF16), w_out[0].astype(BF16), gla_w_gate_up[0], gdn_conv_w[0]],
        [True, True, False, False], "gather_weights")
    w_in16 = w_in_g.reshape(IN_COLS, D_MODEL)
    w_out16 = w_out_g.reshape(D_MODEL, D_MODEL)
    gla_wg = jnp.concatenate([wg_g[j] for j in range(4)], axis=1)
    conv_w = jnp.concatenate([cw_g[j] for j in range(4)], axis=1)

    loss, grad_x, gr = _local_step(x, mod, w_in16, w_out16, gla_wg, gla_b_gate, gla_norm_w, conv_w,
                                   gdn_a_log, gdn_dt_bias, gdn_norm_w, ln_w, ln_b, loss_target)

    gr["loss"] = loss
    packed = _pack_small(gr)
    prow = packed.shape[0]
    gathered = _all_gather8(packed, "gather_small").reshape(ndev, prow, 128)
    small = _unpack_small(_sum_leading(gathered, "sum_small").reshape(-1))
    loss = small["loss"][0]
    mod_rows = gathered.reshape(ndev, prow * 128)[:, sum(sz for _, sz in _SMALL[:-1]):][:, :bl * 3 * D_MODEL]
    dmod_all = mod_rows.reshape(ndev * bl, 3 * D_MODEL)
    dmod_blk = lax.dynamic_slice_in_dim(dmod_all, chip * ada_cols, ada_cols, axis=1)
    g_w_ada, g_b_ada = _ada_grads(c_all, dmod_all, dmod_blk)
    wg_cols = gla_w_gate_up.shape[2]
    g_wg = lax.dynamic_slice_in_dim(small["gla_w_gate_up"].reshape(GLA_RANK, GLA_QK), chip * wg_cols, wg_cols, axis=1)
    cw_cols = gdn_conv_w.shape[2]
    g_cw = lax.dynamic_slice_in_dim(small["gdn_conv_w"].reshape(CONV_K, 3 * GDN_WIDTH), chip * cw_cols, cw_cols, axis=1)

    in_feats = w_in.shape[2]
    out_rows = w_out.shape[1]
    p_in = gr["w_in"]
    p_out = gr["w_out"].reshape(4, out_rows, D_MODEL)
    h_in, h_out = D_MODEL // 2, out_rows // 2
    mine_in = lax.dynamic_slice_in_dim(p_in, ic * h_in, h_in, axis=1)
    mine_out = lax.dynamic_slice_in_dim(p_out, ic * h_out, h_out, axis=1)
    theirs_in = lax.dynamic_slice_in_dim(p_in, (1 - ic) * h_in, h_in, axis=1)
    theirs_out = lax.dynamic_slice_in_dim(p_out, (1 - ic) * h_out, h_out, axis=1)
    got_in, got_out = _sibling_swap([theirs_in, theirs_out], "swap_halves")
    chip_in, chip_in16 = _chip_sum_blocks(mine_in, got_in, in_feats, 4, "chip_sum_in")
    chip_out, chip_out16 = _add_n([mine_out.reshape(4 * h_out, D_MODEL), got_out.reshape(4 * h_out, D_MODEL)],
                                  "chip_sum_out", (F32, BF16))
    chip_out = chip_out.reshape(4, h_out, D_MODEL)
    rs_in, rs_out = _chip_scatter([chip_in16, chip_out16.reshape(4, h_out, D_MODEL)], "scatter_grads")
    own_in = lax.dynamic_index_in_dim(chip_in, chip, axis=0, keepdims=False)
    own_out = lax.dynamic_index_in_dim(chip_out, chip, axis=0, keepdims=False)
    (half_in,) = _add_n([own_in, rs_in[0], rs_in[1], rs_in[2]], "reduce_in")
    (half_out,) = _add_n([own_out, rs_out[0], rs_out[1], rs_out[2]], "reduce_out")
    sib_in, sib_out = _sibling_swap([half_in, half_out], "swap_result")
    g_w_in_t = jnp.where(ic == 0, jnp.concatenate([half_in, sib_in], axis=1),
                         jnp.concatenate([sib_in, half_in], axis=1))[0:in_feats]
    g_w_out = jnp.where(ic == 0, jnp.concatenate([half_out, sib_out], axis=0),
                        jnp.concatenate([sib_out, half_out], axis=0))

    grads = dict(
        w_ada=g_w_ada[None], b_ada=g_b_ada, w_in=g_w_in_t, gla_w_gate_up=g_wg[None],
        gla_b_gate=small["gla_b_gate"].reshape(1, -1), gla_norm_w=small["gla_norm_w"].reshape(1, -1),
        gdn_conv_w=g_cw[None], gdn_a_log=small["gdn_a_log"].reshape(1, -1),
        gdn_dt_bias=small["gdn_dt_bias"].reshape(1, -1), gdn_norm_w=small["gdn_norm_w"].reshape(1, -1),
        w_out=g_w_out[None], ln_w=small["ln_w"].reshape(1, -1), ln_b=small["ln_b"].reshape(1, -1))
    weights = dict(w_ada=w_ada, b_ada=b_ada, w_in=w_in, gla_w_gate_up=gla_w_gate_up, gla_b_gate=gla_b_gate,
                   gla_norm_w=gla_norm_w, gdn_conv_w=gdn_conv_w, gdn_a_log=gdn_a_log, gdn_dt_bias=gdn_dt_bias,
                   gdn_norm_w=gdn_norm_w, w_out=w_out, ln_w=ln_w, ln_b=ln_b)
    m_in = dict(w_ada=m_w_ada, b_ada=m_b_ada, w_in=m_w_in, gla_w_gate_up=m_gla_w_gate_up, gla_b_gate=m_gla_b_gate,
                gla_norm_w=m_gla_norm_w, gdn_conv_w=m_gdn_conv_w, gdn_a_log=m_gdn_a_log, gdn_dt_bias=m_gdn_dt_bias,
                gdn_norm_w=m_gdn_norm_w, w_out=m_w_out, ln_w=m_ln_w, ln_b=m_ln_b)
    v_in = dict(w_ada=v_w_ada, b_ada=v_b_ada, w_in=v_w_in, gla_w_gate_up=v_gla_w_gate_up, gla_b_gate=v_gla_b_gate,
                gla_norm_w=v_gla_norm_w, gdn_conv_w=v_gdn_conv_w, gdn_a_log=v_gdn_a_log, gdn_dt_bias=v_gdn_dt_bias,
                gdn_norm_w=v_gdn_norm_w, w_out=v_w_out, ln_w=v_ln_w, ln_b=v_ln_b)
    names = list(weights)
    delta, new_m, new_v = {}, {}, {}
    small_names = [nm for nm in names if weights[nm].size * 4 <= ELEMENTWISE_BLOCK_BYTES // 16]
    flat2d = lambda t: t.reshape(-1, t.shape[-1])
    outs = _adamw_many([flat2d(weights[nm]) for nm in small_names], [flat2d(grads[nm]) for nm in small_names],
                       [flat2d(m_in[nm]) for nm in small_names], [flat2d(v_in[nm]) for nm in small_names], "adamw_small")
    for i, nm in enumerate(small_names):
        shp = weights[nm].shape
        delta[nm], new_m[nm], new_v[nm] = (o.reshape(shp) for o in outs[3 * i:3 * i + 3])
        grads[nm] = grads[nm].reshape(shp)
    for nm in names:
        if nm in small_names:
            continue
        shp = weights[nm].shape
        if nm == "w_in":
            to2d = lambda t: jnp.transpose(t[0])
            from2d = lambda t: jnp.transpose(t)[None]
            g2d = grads[nm]
        else:
            to2d = lambda t: t.reshape(-1, shp[-1])
            from2d = lambda t: t.reshape(shp)
            g2d = to2d(grads[nm])
        d, a, b = _adamw(to2d(weights[nm]), g2d, to2d(m_in[nm]), to2d(v_in[nm]), "adamw_" + nm)
        delta[nm], new_m[nm], new_v[nm] = from2d(d), from2d(a), from2d(b)
        grads[nm] = from2d(g2d)
    return (loss, grad_x, *[grads[k] for k in names], *[delta[k] for k in names],
            *[new_m[k] for k in names], *[new_v[k] for k in names])
```

```python
import functools

import jax
import jax.numpy as jnp
from jax import lax
from jax.experimental import pallas as pl
from jax.experimental.pallas import tpu as pltpu

F32 = jnp.float32
BF16 = jnp.bfloat16
HI = lax.Precision.HIGH
INV_PREC = None
MESH = pl.DeviceIdType.MESH

D_MODEL = 1024
GLA_HEADS = 4
GLA_DK = 64
GLA_DV = 128
GLA_QK = 256
GLA_WIDTH = 512
GLA_RANK = 16
GLA_GATE_NORM = 16.0
GDN_HEADS = 4
GDN_DK = 128
GDN_WIDTH = 512
CONV_K = 4
CHUNK = 64
LN_EPS = 1e-5
RMS_EPS = 1e-6
ALPHA = 2.0 ** 0.25
IN_COLS = 3608

LANE_A = GLA_RANK
LANE_B = GLA_RANK + GDN_HEADS
SMALL_USED = GLA_RANK + 2 * GDN_HEADS

ADAM_LR = 0.001
ADAM_B1 = 0.9
ADAM_B2 = 0.999
ADAM_EPS = 1e-08
ADAM_WD = 0.01
ADAM_STEP = 10

VMEM_LIMIT = 56 * 1024 * 1024


def _iota(shape, dim):
    return lax.broadcasted_iota(jnp.int32, shape, dim)


def _dot(a, b, prec=None):
    return lax.dot_general(a, b, (((1,), (0,)), ((), ())), precision=prec, preferred_element_type=F32)


def _dot_nt(a, b, prec=None):
    return lax.dot_general(a, b, (((1,), (1,)), ((), ())), precision=prec, preferred_element_type=F32)


def _dot_tn(a, b, prec=None):
    return lax.dot_general(a, b, (((0,), (0,)), ((), ())), precision=prec, preferred_element_type=F32)


def _log_sigmoid(z):
    return jnp.minimum(z, 0.0) - jnp.log1p(jnp.exp(-jnp.abs(z)))


def _softplus(z):
    return jnp.maximum(z, 0.0) + jnp.log1p(jnp.exp(-jnp.abs(z)))


def _silu(z):
    return z * jax.nn.sigmoid(z)


def _rms_gate(o, nw, og):
    return o * lax.rsqrt(jnp.mean(o * o, axis=-1, keepdims=True) + RMS_EPS) * nw * _silu(og)


def _params(*sem):
    return pltpu.CompilerParams(dimension_semantics=sem, vmem_limit_bytes=VMEM_LIMIT)


GLA_PAIRS = GLA_HEADS // 2


def _gla_chunk(qs, ks, lrs, vs, ogs, ss, wgs, bgs, nw):
    c = qs[0].shape[0]
    n_ep = len(ss)
    n_ex = n_ep // GLA_PAIRS
    n_chunks = len(qs) // n_ep
    pair_units = [(i // n_ep * n_ex + i % n_ep // GLA_PAIRS, i % GLA_PAIRS) for i in range(len(qs))]
    head_units = [(i // GLA_HEADS * GLA_PAIRS + i % GLA_HEADS // 2, i % 2) for i in range(len(vs))]
    row, col = _iota((c, c), 0), _iota((c, c), 1)
    causal = row >= col
    first_half = (_iota((c, 1), 0) < c // 2).astype(F32)
    lane = _iota((1, 128), 1)
    masks = [(lane < GLA_DK).astype(F32), (lane >= GLA_DK).astype(F32)]
    gs = [_log_sigmoid(_dot(lrs[ce], wgs[p]) + bgs[p]) * (1.0 / GLA_GATE_NORM) for ce, p in pair_units]
    bs = [_dot(causal.astype(F32), g, HI) for g in gs]
    b_ref = [jnp.sum(g * first_half, axis=0, keepdims=True) for g in gs]
    b_last = [jnp.sum(g, axis=0, keepdims=True) for g in gs]
    qsc = [q * (GLA_DK ** -0.5) for q in qs]
    qe = [q * jnp.exp(b - br) for q, b, br in zip(qsc, bs, b_ref)]
    ke = [k * jnp.exp(br - b) for k, b, br in zip(ks, bs, b_ref)]
    qb = [q * jnp.exp(b) for q, b in zip(qsc, bs)]
    kd = [k * jnp.exp(bl_ - b) for k, b, bl_ in zip(ks, bs, b_last)]
    decay = [jnp.exp(bl_) for bl_ in b_last]
    att = [jnp.where(causal, _dot_nt(qe[u] * masks[half], ke[u]), 0.0) for u, half in head_units]
    o_intra = [_dot(a, v) for a, v in zip(att, vs)]
    qbm = [qb[u] * masks[half] for u, half in head_units]
    kdm = [kd[u] * masks[half] for u, half in head_units]
    ys = []
    for r in range(n_chunks):
        heads_r = range(r * n_ex * GLA_HEADS, (r + 1) * n_ex * GLA_HEADS)
        o_inter = [_dot_nt(qbm[i], ss[head_units[i][0] - r * n_ep]) for i in heads_r]
        upd = [_dot_tn(vs[i], kdm[i]) for i in heads_r]
        ss = [s * decay[r * n_ep + j] + upd[2 * j] + upd[2 * j + 1] for j, s in enumerate(ss)]
        ys += [_rms_gate(o_intra[i] + oi, nw, ogs[i]) for i, oi in zip(heads_r, o_inter)]
    return ys, ss


def _unit_lower_inverse_chain(a_list):
    c = a_list[0].shape[0]
    eye = (_iota((c, c), 0) == _iota((c, c), 1)).astype(F32)
    ps = [-a for a in a_list]
    ts = [eye + p for p in ps]
    levels = max(c.bit_length() - 2, 0)
    if levels:
        ps = [_dot(p, p, INV_PREC) for p in ps]
    for level in range(levels):
        last = level == levels - 1
        both = [_dot(t if last else jnp.concatenate([t, p], axis=0), p, INV_PREC) for t, p in zip(ts, ps)]
        ts = [t + m[0:c] for t, m in zip(ts, both)]
        if not last:
            ps = [m[c:2 * c] for m in both]
    return ts


@jax.custom_vjp
def _unit_lower_inverse(a_list):
    return _unit_lower_inverse_chain(a_list)


def _unit_lower_inverse_fwd(a_list):
    ts = _unit_lower_inverse_chain(a_list)
    return ts, ts


def _unit_lower_inverse_bwd(ts, dts):
    xs = [_dot_nt(dt, t, INV_PREC) for dt, t in zip(dts, ts)]
    return ([-_dot_tn(t, x, INV_PREC) for t, x in zip(ts, xs)],)


_unit_lower_inverse.defvjp(_unit_lower_inverse_fwd, _unit_lower_inverse_bwd)


@jax.custom_vjp
def _unit_lower_inverse_known(a_list, ts):
    return ts


def _unit_lower_inverse_known_fwd(a_list, ts):
    return ts, ts


def _unit_lower_inverse_known_bwd(ts, dts):
    return _unit_lower_inverse_bwd(ts, dts) + ([jnp.zeros_like(t) for t in ts],)


_unit_lower_inverse_known.defvjp(_unit_lower_inverse_known_fwd, _unit_lower_inverse_known_bwd)


def _gdn_prep_units(qs, ks, vs, gbs, t_known=None):
    c = qs[0].shape[0]
    units = [divmod(i, GDN_HEADS) for i in range(len(qs))]
    row, col = _iota((c, c), 0), _iota((c, c), 1)
    causal, strict = row >= col, row > col
    if t_known is None:
        lane = _iota((1, 128), 1)
        d_alls = [_dot(causal.astype(F32), gb, HI) for gb in gbs]
        g_c, beta_c, d_c = [], [], []
        for r, h in units:
            sel_a = (lane == LANE_A + h).astype(F32)
            g_c.append(jnp.sum(gbs[r] * sel_a, axis=-1, keepdims=True))
            beta_c.append(jnp.sum(gbs[r] * (lane == LANE_B + h).astype(F32), axis=-1, keepdims=True))
            d_c.append(jnp.sum(d_alls[r] * sel_a, axis=-1, keepdims=True))
        d_diff = [jnp.broadcast_to(d, (c, c)) - jnp.broadcast_to(d, (c, c)).T for d in d_c]
    else:
        src = _iota((128, 128), 0)
        spread = jnp.concatenate([(src == base + h).astype(F32) for base in (LANE_A, LANE_B)
                                  for h in range(GDN_HEADS)], axis=1)
        width = GDN_HEADS * 128
        g_beta = [_dot(gb, spread, HI) for gb in gbs]
        d_alls = [_dot(causal.astype(F32), gbv[:, 0:width], HI) for gbv in g_beta]
        g_c = [g_beta[r][:, h * 128:(h + 1) * 128] for r, h in units]
        beta_c = [g_beta[r][:, width + h * 128:width + (h + 1) * 128] for r, h in units]
        d_c = [d_alls[r][:, h * 128:(h + 1) * 128] for r, h in units]
        d_diff = [d[:, 0:c] - d.T[0:c, :] for d in d_c]
    d_last = [jnp.sum(g, axis=0, keepdims=True) for g in g_c]
    decay_mat = [jnp.where(causal, jnp.exp(jnp.where(causal, dd, 0.0)), 0.0) for dd in d_diff]
    kb = [k * b for k, b in zip(ks, beta_c)]
    kbk_qk = [_dot_nt(jnp.concatenate([kbi, q], axis=0), k) for kbi, q, k in zip(kb, qs, ks)]
    a = [jnp.where(strict, m[0:c] * dm, 0.0) for m, dm in zip(kbk_qk, decay_mat)]
    qk = [jnp.where(causal, m[c:2 * c] * dm, 0.0) for m, dm in zip(kbk_qk, decay_mat)]
    t = _unit_lower_inverse(a) if t_known is None else _unit_lower_inverse_known(a, t_known)
    uw = [_dot(ti, jnp.concatenate([v * b, kbi * jnp.exp(d)], axis=1))
          for ti, v, b, kbi, d in zip(t, vs, beta_c, kb, d_c)]
    u = [m[:, 0:128] for m in uw]
    w = [m[:, 128:256] for m in uw]
    q_dec = [q * jnp.exp(d) for q, d in zip(qs, d_c)]
    k_dec = [k * jnp.exp(dl - d) for k, dl, d in zip(ks, d_last, d_c)]
    gamma = [jnp.exp(dl) for dl in d_last]
    return u, w, qk, q_dec, k_dec, gamma, t


def _sum_rows(t):
    return jnp.sum(t, axis=0, keepdims=True)


def _gdn_pre_elem(ps, ab, alog_v, dtb_v):
    outs = []
    for j, p in enumerate(ps):
        s = _silu(p)
        if j < 2 * GDN_HEADS:
            s = s * lax.rsqrt(jnp.sum(s * s, axis=-1, keepdims=True) + RMS_EPS)
        if j < GDN_HEADS:
            s = s * (GDN_DK ** -0.5)
        outs.append(s)
    lane = _iota((1, 128), 1)
    is_a = (lane >= LANE_A) & (lane < LANE_A + GDN_HEADS)
    is_b = (lane >= LANE_B) & (lane < LANE_B + GDN_HEADS)
    g = -jnp.exp(alog_v) * _softplus(ab + dtb_v)
    gb = jnp.where(is_a, g, jnp.where(is_b, jax.nn.sigmoid(ab), 0.0))
    return tuple(outs) + (gb,)


def _proj_fwd(x2, sc3, sh3, ws, seq, tm=512):
    n = x2.shape[0]
    tpe = seq // tm
    nw = len(ws)

    def body(x_ref, sc_ref, sh_ref, *refs):
        h = (x_ref[...] * sc_ref[0] + sh_ref[0]).astype(ws[0].dtype)
        for w_ref, o_ref in zip(refs[:nw], refs[nw:]):
            o_ref[...] = _dot_nt(h, w_ref[...])

    row = lambda i: (i, 0)
    per_ex = pl.BlockSpec((1, 1, D_MODEL), lambda i: (i // tpe, 0, 0))
    return pl.pallas_call(
        body, name="proj_fwd", grid=(n // tm,),
        in_specs=[pl.BlockSpec((tm, D_MODEL), row), per_ex, per_ex]
        + [pl.BlockSpec(w.shape, lambda i: (0, 0)) for w in ws],
        out_specs=[pl.BlockSpec((tm, w.shape[0]), row) for w in ws],
        out_shape=[jax.ShapeDtypeStruct((n, w.shape[0]), F32) for w in ws],
        compiler_params=_params("parallel"),
    )(x2, sc3, sh3, *ws)


GLA_SCAN_CHUNKS = 4


def _gla_operands(q_ref, k_ref, v_ref, og_ref, lr_ref, wg_ref, bg_ref, bl, r_per):
    chunks = [slice(r * CHUNK, (r + 1) * CHUNK) for r in range(r_per)]
    pair_cols = [slice(p * 128, (p + 1) * 128) for p in range(GLA_PAIRS)]
    head_cols = [slice(h * 128, (h + 1) * 128) for h in range(GLA_HEADS)]
    per_pair = lambda ref: [ref[e, rows, cols] for rows in chunks for e in range(bl) for cols in pair_cols]
    per_head = lambda ref: [ref[e, rows, cols] for rows in chunks for e in range(bl) for cols in head_cols]
    return (per_pair(q_ref), per_pair(k_ref), [lr_ref[e, rows, :] for rows in chunks for e in range(bl)],
            per_head(v_ref), per_head(og_ref)), ([wg_ref[:, cols] for cols in pair_cols],
                                                 [bg_ref[:, cols] for cols in pair_cols])


def _gla_fwd(pa, pd, wg, bg, nw, bl, seq):
    n = pa.shape[0]
    nc = seq // CHUNK
    r_per = GLA_SCAN_CHUNKS
    pairs = [(e, p) for e in range(bl) for p in range(GLA_PAIRS)]
    head_slots = [(slice(r * CHUNK, (r + 1) * CHUNK), e, slice(h * 128, (h + 1) * 128))
                  for r in range(r_per) for e in range(bl) for h in range(GLA_HEADS)]

    def body(q_ref, k_ref, v_ref, og_ref, lr_ref, wg_ref, bg_ref, nw_ref, y_ref, st_ref, s_scr):
        @pl.when(pl.program_id(0) == 0)
        def _():
            s_scr[...] = jnp.zeros_like(s_scr)

        ss = [s_scr[e, p] for e, p in pairs]
        for (e, p), s in zip(pairs, ss):
            st_ref[e, 0, p] = s
        acts, gate = _gla_operands(q_ref, k_ref, v_ref, og_ref, lr_ref, wg_ref, bg_ref, bl, r_per)
        ys, s_new = _gla_chunk(*acts, ss, *gate, nw_ref[...])
        for (rows, e, cols), y in zip(head_slots, ys):
            y_ref[e, rows, cols] = y.astype(y_ref.dtype)
        for (e, p), s in zip(pairs, s_new):
            s_scr[e, p] = s

    tok = lambda w, j: pl.BlockSpec((bl, r_per * CHUNK, w), lambda i: (0, i, j))
    const = lambda i: (0, 0)
    pa3 = pa.reshape(bl, seq, 1536)
    y, st = pl.pallas_call(
        body, name="gla_fwd", grid=(nc // r_per,),
        in_specs=[tok(256, 0), tok(256, 1), tok(512, 1), tok(512, 2), tok(128, 0),
                  pl.BlockSpec(wg.shape, const), pl.BlockSpec(bg.shape, const), pl.BlockSpec(nw.shape, const)],
        out_specs=[tok(512, 0), pl.BlockSpec((bl, 1, GLA_PAIRS, 128, 128), lambda i: (0, i, 0, 0, 0))],
        out_shape=[jax.ShapeDtypeStruct((bl, seq, 512), MM_DTYPE),
                   jax.ShapeDtypeStruct((bl, nc // r_per, GLA_PAIRS, 128, 128), F32)],
        scratch_shapes=[pltpu.VMEM((bl, GLA_PAIRS, 128, 128), F32)],
        compiler_params=_params("arbitrary"),
    )(pa3, pa3, pa3, pa3, pd.reshape(bl, seq, 128), wg, bg, nw)
    return y.reshape(n, 512), st


def _gla_bwd(pa, pd, st, dya, wg, bg, nw, bl, seq):
    n = pa.shape[0]
    nc = seq // CHUNK
    r_per = GLA_SCAN_CHUNKS
    steps = nc // r_per
    pairs = [(e, p) for e in range(bl) for p in range(GLA_PAIRS)]
    pair_cols = [slice(p * 128, (p + 1) * 128) for p in range(GLA_PAIRS)]
    chunks = [slice(r * CHUNK, (r + 1) * CHUNK) for r in range(r_per)]
    pair_slots = [(rows, e, p) for rows in chunks for e in range(bl) for p in range(GLA_PAIRS)]
    head_slots = [(rows, e, h) for rows in chunks for e in range(bl) for h in range(GLA_HEADS)]

    def body(q_ref, k_ref, v_ref, og_ref, lr_ref, st_ref, dy_ref, wg_ref, bg_ref, nw_ref,
             da_ref, dd_ref, dwg_ref, dbg_ref, dnw_ref, ds_scr):
        @pl.when(pl.program_id(0) == 0)
        def _():
            dwg_ref[...] = jnp.zeros_like(dwg_ref)
            dbg_ref[...] = jnp.zeros_like(dbg_ref)
            dnw_ref[...] = jnp.zeros_like(dnw_ref)
            ds_scr[...] = jnp.zeros_like(ds_scr)

        acts, gate = _gla_operands(q_ref, k_ref, v_ref, og_ref, lr_ref, wg_ref, bg_ref, bl, r_per)
        _, vjp = jax.vjp(_gla_chunk, *acts, [st_ref[e, 0, p] for e, p in pairs], *gate, nw_ref[...])
        dq, dk, dlr, dv, dog, ds, dwg, dbg, dnw = vjp(
            ([dy_ref[e, rows, h * 128:(h + 1) * 128] for rows, e, h in head_slots], [ds_scr[e, p] for e, p in pairs]))
        for i, (rows, e) in enumerate((rows, e) for rows in chunks for e in range(bl)):
            dd_ref[e, rows, :] = dlr[i]
        for i, (rows, e, p) in enumerate(pair_slots):
            da_ref[e, rows, pair_cols[p]] = dq[i].astype(da_ref.dtype)
            da_ref[e, rows, GLA_QK + p * 128:GLA_QK + (p + 1) * 128] = dk[i].astype(da_ref.dtype)
        for i, (rows, e, h) in enumerate(head_slots):
            da_ref[e, rows, 512 + h * 128:512 + (h + 1) * 128] = dv[i].astype(da_ref.dtype)
            da_ref[e, rows, 1024 + h * 128:1024 + (h + 1) * 128] = dog[i].astype(da_ref.dtype)
        for (e, p), d in zip(pairs, ds):
            ds_scr[e, p] = d
        for p, cols in enumerate(pair_cols):
            dwg_ref[:, cols] += dwg[p]
            dbg_ref[:, cols] += dbg[p]
        dnw_ref[...] += dnw

    tok = lambda w, j: pl.BlockSpec((bl, r_per * CHUNK, w), lambda i: (0, steps - 1 - i, j))
    const = lambda i: (0, 0)
    pa3 = pa.reshape(bl, seq, 1536)
    da, dd, dwg, dbg, dnw = pl.pallas_call(
        body, name="gla_bwd", grid=(steps,),
        in_specs=[tok(256, 0), tok(256, 1), tok(512, 1), tok(512, 2), tok(128, 0),
                  pl.BlockSpec((bl, 1, GLA_PAIRS, 128, 128), lambda i: (0, steps - 1 - i, 0, 0, 0)), tok(512, 0),
                  pl.BlockSpec(wg.shape, const), pl.BlockSpec(bg.shape, const), pl.BlockSpec(nw.shape, const)],
        out_specs=[tok(1536, 0), tok(128, 0),
                   pl.BlockSpec(wg.shape, const), pl.BlockSpec(bg.shape, const), pl.BlockSpec(nw.shape, const)],
        out_shape=[jax.ShapeDtypeStruct((bl, seq, 1536), MM_DTYPE), jax.ShapeDtypeStruct((bl, seq, 128), F32),
                   jax.ShapeDtypeStruct(wg.shape, F32), jax.ShapeDtypeStruct(bg.shape, F32),
                   jax.ShapeDtypeStruct(nw.shape, F32)],
        scratch_shapes=[pltpu.VMEM((bl, GLA_PAIRS, 128, 128), F32)],
        compiler_params=_params("arbitrary"),
    )(pa3, pa3, pa3, pa3, pd.reshape(bl, seq, 128), st, dya.reshape(bl, seq, 512), wg, bg, nw)
    return da.reshape(n, 1536), dd.reshape(n, 128), dwg, dbg, dnw


PRE_ROWS = 64
PRE_PIECES = [slice(j * 128, (j + 1) * 128) for j in range(3 * GDN_HEADS)]


def _rows_from(ref, start, rows, cols):
    lo = start // 8 * 8
    if lo == start:
        return ref[start:start + rows, cols]
    window = ref[lo:lo + rows + 8, cols]
    return pltpu.roll(window, rows + 8 - (start - lo), 0)[0:rows]


def _conv_taps(buf_ref, w_ref, base, rows, cols):
    acc = w_ref[0:1, cols] * _rows_from(buf_ref, base, rows, cols)
    for k in range(1, CONV_K):
        acc = acc + w_ref[k:k + 1, cols] * _rows_from(buf_ref, base + k, rows, cols)
    return acc


def _gdn_pre_fwd(pb, pd, cw8, alog_v, dtb_v, bl, seq, tm=512):
    n = pb.shape[0]
    tpe = seq // tm
    t8 = tm // 8

    def body(u_ref, prev_ref, ab_ref, w_ref, al_ref, dt_ref, qkv_ref, gb_ref, p_ref, buf):
        i = pl.program_id(0)
        keep = (i % tpe != 0).astype(F32)
        buf[0:8, :] = prev_ref[...] * keep
        buf[8:8 + tm, :] = u_ref[...]
        for r0 in range(0, tm, PRE_ROWS):
            rows = slice(r0, r0 + PRE_ROWS)
            ps = [_conv_taps(buf, w_ref, 8 - (CONV_K - 1) + r0, PRE_ROWS, cols) for cols in PRE_PIECES]
            outs = _gdn_pre_elem(ps, ab_ref[rows, :], al_ref[...], dt_ref[...])
            for cols, p, out in zip(PRE_PIECES, ps, outs):
                p_ref[rows, cols] = p
                qkv_ref[rows, cols] = out
            gb_ref[rows, :] = outs[len(PRE_PIECES)]

    row = lambda i: (i, 0)
    const = lambda i: (0, 0)
    return pl.pallas_call(
        body, name="gdn_pre_fwd", grid=(n // tm,),
        in_specs=[pl.BlockSpec((tm, 1536), row),
                  pl.BlockSpec((8, 1536), lambda i: (jnp.maximum(i * t8 - 1, 0), 0)),
                  pl.BlockSpec((tm, 128), row),
                  pl.BlockSpec((8, 1536), const), pl.BlockSpec((1, 128), const), pl.BlockSpec((1, 128), const)],
        out_specs=[pl.BlockSpec((tm, 1536), row), pl.BlockSpec((tm, 128), row), pl.BlockSpec((tm, 1536), row)],
        out_shape=[jax.ShapeDtypeStruct((n, 1536), F32), jax.ShapeDtypeStruct((n, 128), F32),
                   jax.ShapeDtypeStruct((n, 1536), F32)],
        scratch_shapes=[pltpu.VMEM((tm + 8, 1536), F32)],
        compiler_params=_params("parallel"),
    )(pb, pb, pd, cw8, alog_v, dtb_v)


def _gdn_pre_bwd(pb, conv_out, pd, dqkv, dgb, cw8, alog_v, dtb_v, bl, seq, tm=512):
    n = pb.shape[0]
    tpe = seq // tm
    t8 = tm // 8
    nb8 = n // 8
    ext = tm + 8

    def body(u_ref, p_ref, pn_ref, ab_ref, abn_ref, dq_ref, dqn_ref, dgb_ref, w_ref, al_ref, dt_ref,
             du_ref, dab_ref, dw_ref, dal_ref, ddt_ref, dpbuf):
        i = pl.program_id(0)

        @pl.when(i == 0)
        def _():
            dw_ref[...] = jnp.zeros_like(dw_ref)
            dal_ref[...] = jnp.zeros_like(dal_ref)
            ddt_ref[...] = jnp.zeros_like(ddt_ref)

        keep_next = (i % tpe != tpe - 1).astype(F32)
        zeros8 = jnp.zeros((8, 128), F32)
        dal, ddt = jnp.zeros((1, 128), F32), jnp.zeros((1, 128), F32)
        for r0 in range(0, tm, PRE_ROWS):
            rows = slice(r0, r0 + PRE_ROWS)
            last = r0 + PRE_ROWS == tm
            along = lambda own, extra: jnp.concatenate([own, extra], axis=0) if last else own
            ps = [along(p_ref[rows, cols], pn_ref[:, cols]) for cols in PRE_PIECES]
            ab = along(ab_ref[rows, :], abn_ref[...])
            _, vjp = jax.vjp(_gdn_pre_elem, ps, ab, al_ref[...], dt_ref[...])
            cts = tuple(along(dq_ref[rows, cols], dqn_ref[:, cols] * keep_next) for cols in PRE_PIECES)
            cts += (along(dgb_ref[rows, :], zeros8),)
            dps, dab, dal_r, ddt_r = vjp(cts)
            out_rows = slice(r0, r0 + PRE_ROWS + (8 if last else 0))
            for cols, dp in zip(PRE_PIECES, dps):
                dpbuf[out_rows, cols] = dp
            dab_ref[rows, :] = dab[0:PRE_ROWS, :]
            dal, ddt = dal + dal_r, ddt + ddt_r
        dal_ref[...] += dal
        ddt_ref[...] += ddt
        for cols in PRE_PIECES:
            dw = [jnp.zeros((1, 128), F32) for _ in range(CONV_K)]
            for r0 in range(0, tm, PRE_ROWS):
                u = u_ref[r0:r0 + PRE_ROWS, cols]
                du = None
                for k in range(CONV_K):
                    dp_k = _rows_from(dpbuf, r0 + CONV_K - 1 - k, PRE_ROWS, cols)
                    term = w_ref[k:k + 1, cols] * dp_k
                    du = term if du is None else du + term
                    dw[k] = dw[k] + jnp.sum(u * dp_k, axis=0, keepdims=True)
                du_ref[r0:r0 + PRE_ROWS, cols] = du.astype(du_ref.dtype)
            for k in range(CONV_K):
                dw_ref[k:k + 1, cols] += dw[k]

    row = lambda i: (i, 0)
    next8 = lambda i: (jnp.minimum((i + 1) * t8, nb8 - 1), 0)
    const = lambda i: (0, 0)
    return pl.pallas_call(
        body, name="gdn_pre_bwd", grid=(n // tm,),
        in_specs=[pl.BlockSpec((tm, 1536), row), pl.BlockSpec((tm, 1536), row), pl.BlockSpec((8, 1536), next8),
                  pl.BlockSpec((tm, 128), row), pl.BlockSpec((8, 128), next8),
                  pl.BlockSpec((tm, 1536), row), pl.BlockSpec((8, 1536), next8),
                  pl.BlockSpec((tm, 128), row),
                  pl.BlockSpec((8, 1536), const), pl.BlockSpec((1, 128), const), pl.BlockSpec((1, 128), const)],
        out_specs=[pl.BlockSpec((tm, 1536), row), pl.BlockSpec((tm, 128), row),
                   pl.BlockSpec((8, 1536), const), pl.BlockSpec((1, 128), const), pl.BlockSpec((1, 128), const)],
        out_shape=[jax.ShapeDtypeStruct((n, 1536), MM_DTYPE), jax.ShapeDtypeStruct((n, 128), F32),
                   jax.ShapeDtypeStruct((8, 1536), F32), jax.ShapeDtypeStruct((1, 128), F32),
                   jax.ShapeDtypeStruct((1, 128), F32)],
        scratch_shapes=[pltpu.VMEM((ext, 1536), F32)],
        compiler_params=_params("arbitrary"),
    )(pb, conv_out, conv_out, pd, pd, dqkv, dqkv, dgb, cw8, alog_v, dtb_v)


GDN_PREP_CHUNKS = 4
GDN_PREP_BWD_CHUNKS = 8
GDN_SCAN_CHUNKS = 8
MM_DTYPE = BF16


def _head_cols(ref, rows, base=0):
    return [ref[rows, base + h * 128:base + (h + 1) * 128] for h in range(GDN_HEADS)]


def _gdn_prep(qkv, gb):
    n = qkv.shape[0]
    r_per = GDN_PREP_CHUNKS
    tm = r_per * CHUNK

    def body(q_ref, k_ref, v_ref, gb_ref, u_ref, w_ref, qd_ref, kd_ref, qk_ref, t_ref, gam_ref):
        rowid = _iota((8, 128), 0)
        chunk_rows = [slice(r * CHUNK, (r + 1) * CHUNK) for r in range(r_per)]
        gather = lambda ref: [t for rows in chunk_rows for t in _head_cols(ref, rows)]
        u, w, qk, qd, kd, gamma, tinv = _gdn_prep_units(gather(q_ref), gather(k_ref), gather(v_ref),
                                                        [gb_ref[rows, :] for rows in chunk_rows])
        for r, rows in enumerate(chunk_rows):
            gam = jnp.zeros((8, 128), F32)
            for h in range(GDN_HEADS):
                i = r * GDN_HEADS + h
                cols = slice(h * 128, (h + 1) * 128)
                u_ref[rows, cols] = u[i]
                w_ref[rows, cols] = w[i].astype(MM_DTYPE)
                qd_ref[rows, cols] = qd[i].astype(MM_DTYPE)
                kd_ref[rows, cols] = kd[i].astype(MM_DTYPE)
                qk_ref[r, h] = qk[i].astype(MM_DTYPE)
                t_ref[r, h] = tinv[i].astype(MM_DTYPE)
                gam = jnp.where(rowid == h, gamma[i], gam)
            gam_ref[r] = gam

    tok = lambda j: pl.BlockSpec((tm, 512), lambda i: (i, j))
    return pl.pallas_call(
        body, name="gdn_prep", grid=(n // tm,),
        in_specs=[tok(0), tok(1), tok(2), pl.BlockSpec((tm, 128), lambda i: (i, 0))],
        out_specs=[tok(0)] * 4 + [pl.BlockSpec((r_per, GDN_HEADS, CHUNK, CHUNK), lambda i: (i, 0, 0, 0))] * 2
        + [pl.BlockSpec((r_per, 8, 128), lambda i: (i, 0, 0))],
        out_shape=[jax.ShapeDtypeStruct((n, 512), F32)] + [jax.ShapeDtypeStruct((n, 512), MM_DTYPE)] * 3
        + [jax.ShapeDtypeStruct((n // CHUNK, GDN_HEADS, CHUNK, CHUNK), MM_DTYPE)] * 2
        + [jax.ShapeDtypeStruct((n // CHUNK, 8, 128), F32)],
        compiler_params=_params("parallel"),
    )(qkv, qkv, qkv, gb)


def _gdn_fwd(qkv, gb, pc, nw, bl, seq):
    n = qkv.shape[0]
    nc = seq // CHUNK
    u, w, qd, kd, qk, tinv, gam = _gdn_prep(qkv, gb)
    tok3 = lambda t: t.reshape(bl, seq, 512)
    qk5 = qk.reshape(bl, nc, GDN_HEADS, CHUNK, CHUNK)
    gam4 = gam.reshape(bl, nc, 8, 128)

    r_per = GDN_SCAN_CHUNKS
    mm = lambda t: t.astype(MM_DTYPE)

    def body(u_ref, w_ref, qd_ref, kd_ref, qk_ref, gam_ref, og_ref, nw_ref, o_ref, y_ref, vn_ref, st_ref, s_scr):
        @pl.when(pl.program_id(0) == 0)
        def _():
            s_scr[...] = jnp.zeros_like(s_scr)

        units = [(b, h, slice(h * 128, (h + 1) * 128)) for b in range(bl) for h in range(GDN_HEADS)]
        ss = [s_scr[b, h] for b, h, _ in units]
        for r in range(r_per):
            rows = slice(r * CHUNK, (r + 1) * CHUNK)
            for (b, h, _), s in zip(units, ss):
                st_ref[b, r, h] = s
            ws_qs = [_dot(jnp.concatenate([w_ref[b, rows, cols], qd_ref[b, rows, cols]], axis=0), mm(s))
                     for (b, h, cols), s in zip(units, ss)]
            v_new = [u_ref[b, rows, cols] - m[0:CHUNK] for (b, h, cols), m in zip(units, ws_qs)]
            os_ = [m[CHUNK:2 * CHUNK] + _dot(qk_ref[b, r, h], mm(vn))
                   for (b, h, cols), m, vn in zip(units, ws_qs, v_new)]
            ss = [s * gam_ref[b, r, h:h + 1, :] + _dot_tn(kd_ref[b, rows, cols], mm(vn))
                  for (b, h, cols), s, vn in zip(units, ss, v_new)]
            for (b, h, cols), vn, o in zip(units, v_new, os_):
                vn_ref[b, rows, cols] = mm(vn)
                o_ref[b, rows, cols] = o
                y_ref[b, rows, cols] = mm(_rms_gate(o, nw_ref[...], og_ref[b, rows, cols]))
        for (b, h, _), s in zip(units, ss):
            s_scr[b, h] = s

    tok = pl.BlockSpec((bl, r_per * CHUNK, 512), lambda i: (0, i, 0))
    st_spec = pl.BlockSpec((bl, r_per, GDN_HEADS, 128, 128), lambda i: (0, i, 0, 0, 0))
    tok_shape = jax.ShapeDtypeStruct((bl, seq, 512), F32)
    o, y, vn, st = pl.pallas_call(
        body, name="gdn_scan_fwd", grid=(nc // r_per,),
        in_specs=[tok, tok, tok, tok,
                  pl.BlockSpec((bl, r_per, GDN_HEADS, CHUNK, CHUNK), lambda i: (0, i, 0, 0, 0)),
                  pl.BlockSpec((bl, r_per, 8, 128), lambda i: (0, i, 0, 0)), tok,
                  pl.BlockSpec(nw.shape, lambda i: (0, 0))],
        out_specs=[tok, tok, tok, st_spec],
        out_shape=[tok_shape, jax.ShapeDtypeStruct((bl, seq, 512), MM_DTYPE), jax.ShapeDtypeStruct((bl, seq, 512), MM_DTYPE),
                   jax.ShapeDtypeStruct((bl, nc, GDN_HEADS, 128, 128), F32)],
        scratch_shapes=[pltpu.VMEM((bl, GDN_HEADS, 128, 128), F32)],
        compiler_params=_params("arbitrary"),
    )(tok3(u), tok3(w), tok3(qd), tok3(kd), qk5, gam4, tok3(pc), nw)
    return y.reshape(n, 512), (o, st, w, qd, kd, qk5, gam4, tinv, vn)


def _gdn_bwd(qkv, gb, pc, res, dyb, nw, bl, seq):
    n = qkv.shape[0]
    nc = seq // CHUNK
    o, st, w, qd, kd, qk5, gam4, tinv, vn = res
    tok3 = lambda t: t.reshape(bl, seq, 512)

    def scan_body(dy_ref, o_ref, og_ref, w_ref, qd_ref, kd_ref, qk_ref, gam_ref, nw_ref,
                  do_ref, dog_ref, dvn_ref, dst_ref, dnw_ref, ds_scr):
        @pl.when(pl.program_id(0) == 0)
        def _():
            ds_scr[...] = jnp.zeros_like(ds_scr)
            dnw_ref[...] = jnp.zeros_like(dnw_ref)

        units = [(b, h, slice(h * 128, (h + 1) * 128)) for b in range(bl) for h in range(GDN_HEADS)]
        dnw = jnp.zeros(nw.shape, F32)
        dss = [ds_scr[b, h] for b, h, _ in units]
        for r in reversed(range(r_scan)):
            rows = slice(r * CHUNK, (r + 1) * CHUNK)
            d_os = []
            for b, h, cols in units:
                _, vjp = jax.vjp(_rms_gate, o_ref[b, rows, cols], nw_ref[...], og_ref[b, rows, cols])
                d_o, dnw_h, dog = vjp(dy_ref[b, rows, cols])
                do_ref[b, rows, cols] = mm(d_o)
                dog_ref[b, rows, cols] = mm(dog)
                dnw = dnw + dnw_h
                d_os.append(mm(d_o))
            for (b, h, _), ds in zip(units, dss):
                dst_ref[b, r, h] = ds
            dvn_a = [_dot(kd_ref[b, rows, cols], mm(ds)) for (b, h, cols), ds in zip(units, dss)]
            dvns = [a + _dot_tn(qk_ref[b, r, h], d_o) for (b, h, cols), a, d_o in zip(units, dvn_a, d_os)]
            for (b, h, cols), dvn in zip(units, dvns):
                dvn_ref[b, rows, cols] = mm(dvn)
            dss = [ds * gam_ref[b, r, h:h + 1, :] + _dot_tn(
                jnp.concatenate([qd_ref[b, rows, cols], w_ref[b, rows, cols]], axis=0),
                jnp.concatenate([d_o, mm(-dvn)], axis=0))
                for (b, h, cols), d_o, ds, dvn in zip(units, d_os, dss, dvns)]
        dnw_ref[...] += dnw
        for (b, h, _), ds in zip(units, dss):
            ds_scr[b, h] = ds

    r_scan = GDN_SCAN_CHUNKS
    mm = lambda t: t.astype(MM_DTYPE)
    rev = lambda i: nc // r_scan - 1 - i
    tok = pl.BlockSpec((bl, r_scan * CHUNK, 512), lambda i: (0, rev(i), 0))
    st_spec = pl.BlockSpec((bl, r_scan, GDN_HEADS, 128, 128), lambda i: (0, rev(i), 0, 0, 0))
    tok_shape = jax.ShapeDtypeStruct((bl, seq, 512), F32)
    tok_mm = jax.ShapeDtypeStruct((bl, seq, 512), MM_DTYPE)
    d_o, dog, dvn, dst, dnw = pl.pallas_call(
        scan_body, name="gdn_scan_bwd", grid=(nc // r_scan,),
        in_specs=[tok] * 6 + [pl.BlockSpec((bl, r_scan, GDN_HEADS, CHUNK, CHUNK), lambda i: (0, rev(i), 0, 0, 0)),
                              pl.BlockSpec((bl, r_scan, 8, 128), lambda i: (0, rev(i), 0, 0)),
                              pl.BlockSpec(nw.shape, lambda i: (0, 0))],
        out_specs=[tok, tok, tok, st_spec, pl.BlockSpec(nw.shape, lambda i: (0, 0))],
        out_shape=[tok_mm, tok_mm, tok_mm, jax.ShapeDtypeStruct(st.shape, F32),
                   jax.ShapeDtypeStruct(nw.shape, F32)],
        scratch_shapes=[pltpu.VMEM((bl, GDN_HEADS, 128, 128), F32)],
        compiler_params=_params("arbitrary"),
    )(tok3(dyb), o, tok3(pc), tok3(w), tok3(qd), tok3(kd), qk5, gam4, nw)

    r_per = GDN_PREP_BWD_CHUNKS
    tm = r_per * CHUNK

    def prep_body(q_ref, k_ref, v_ref, gb_ref, t_ref, st_ref, dst_ref, dvn_ref, do_ref, vn_ref, dqkv_ref, dgb_ref):
        chunk_rows = [slice(r * CHUNK, (r + 1) * CHUNK) for r in range(r_per)]
        gather = lambda ref: [t for rows in chunk_rows for t in _head_cols(ref, rows)]
        units = [(r, h) for r in range(r_per) for h in range(GDN_HEADS)]
        t_known = [t_ref[r, h].astype(F32) for r, h in units]
        prep = lambda q, k, v, g: _gdn_prep_units(q, k, v, g, t_known)[:6]
        _, vjp = jax.vjp(prep, gather(q_ref), gather(k_ref), gather(v_ref), [gb_ref[rows, :] for rows in chunk_rows])
        ss = [st_ref[r, h] for r, h in units]
        dss = [dst_ref[r, h] for r, h in units]
        dvns, d_os, v_new = gather(dvn_ref), gather(do_ref), gather(vn_ref)
        both = [_dot_nt(jnp.concatenate([dvn, d_o], axis=0), s.astype(MM_DTYPE)) for dvn, d_o, s in zip(dvns, d_os, ss)]
        d_w = [-m[0:CHUNK] for m in both]
        d_qd = [m[CHUNK:2 * CHUNK] for m in both]
        d_qk = [_dot_nt(d_o, vn) for d_o, vn in zip(d_os, v_new)]
        d_kd = [_dot_nt(vn, ds.astype(MM_DTYPE)) for vn, ds in zip(v_new, dss)]
        d_gam = [_sum_rows(ds * s) for ds, s in zip(dss, ss)]
        dq, dk, dv, dgb = vjp(([d.astype(F32) for d in dvns], d_w, d_qk, d_qd, d_kd, d_gam))
        for i, (r, h) in enumerate(units):
            rows = chunk_rows[r]
            for part, d in enumerate((dq, dk, dv)):
                dqkv_ref[rows, part * 512 + h * 128:part * 512 + (h + 1) * 128] = d[i]
        for r, rows in enumerate(chunk_rows):
            dgb_ref[rows, :] = dgb[r]

    tokp = lambda j: pl.BlockSpec((tm, 512), lambda i: (i, j))
    st4 = pl.BlockSpec((r_per, GDN_HEADS, 128, 128), lambda i: (i, 0, 0, 0))
    dqkv, dgb = pl.pallas_call(
        prep_body, name="gdn_prep_bwd", grid=(n // tm,),
        in_specs=[tokp(0), tokp(1), tokp(2), pl.BlockSpec((tm, 128), lambda i: (i, 0)),
                  pl.BlockSpec((r_per, GDN_HEADS, CHUNK, CHUNK), lambda i: (i, 0, 0, 0)), st4, st4,
                  tokp(0), tokp(0), tokp(0)],
        out_specs=[pl.BlockSpec((tm, 1536), lambda i: (i, 0)), pl.BlockSpec((tm, 128), lambda i: (i, 0))],
        out_shape=[jax.ShapeDtypeStruct((n, 1536), F32), jax.ShapeDtypeStruct((n, 128), F32)],
        compiler_params=_params("parallel"),
    )(qkv, qkv, qkv, gb, tinv, st.reshape(bl * nc, GDN_HEADS, 128, 128), dst.reshape(bl * nc, GDN_HEADS, 128, 128),
      dvn.reshape(n, 512), d_o.reshape(n, 512), vn.reshape(n, 512))
    return dqkv, dog.reshape(n, 512), dgb, dnw


def _out_block(x2, tgt2, ya, yb, g1p3, wo, lnw, lnb, seq, tm=512):
    n = x2.shape[0]
    tpe = seq // tm
    bl = n // seq

    def body(x_ref, t_ref, ya_ref, yb_ref, g_ref, wo_ref, lnw_ref, lnb_ref,
             dz_ref, dya_ref, dyb_ref, dwo_ref, dg_ref, glw_ref, glb_ref, loss_ref):
        i = pl.program_id(0)

        @pl.when(i == 0)
        def _():
            dwo_ref[...] = jnp.zeros_like(dwo_ref)
            glw_ref[...] = jnp.zeros_like(glw_ref)
            glb_ref[...] = jnp.zeros_like(glb_ref)
            loss_ref[...] = jnp.zeros_like(loss_ref)

        @pl.when(i % tpe == 0)
        def _():
            dg_ref[...] = jnp.zeros_like(dg_ref)

        ya16 = ya_ref[...].astype(wo.dtype)
        yb16 = yb_ref[...].astype(wo.dtype)
        wa = wo_ref[0:GLA_WIDTH, :]
        wb = wo_ref[GLA_WIDTH:, :]
        y = _dot(ya16, wa) + _dot(yb16, wb)
        g1p = g_ref[0]
        z = ALPHA * x_ref[...] + g1p * y
        mu = jnp.mean(z, axis=-1, keepdims=True)
        zc = z - mu
        rstd = lax.rsqrt(jnp.mean(zc * zc, axis=-1, keepdims=True) + LN_EPS)
        xhat = zc * rstd
        diff = xhat * lnw_ref[...] + lnb_ref[...] - t_ref[...]
        loss_ref[...] += (0.5 / D_MODEL) * jnp.sum(jnp.sum(diff * diff, axis=-1, keepdims=True), axis=0, keepdims=True)
        dout = diff * (1.0 / D_MODEL)
        glw_ref[...] += jnp.sum(dout * xhat, axis=0, keepdims=True)
        glb_ref[...] += jnp.sum(dout, axis=0, keepdims=True)
        dxh = dout * lnw_ref[...]
        dz = rstd * (dxh - jnp.mean(dxh, axis=-1, keepdims=True)
                     - xhat * jnp.mean(dxh * xhat, axis=-1, keepdims=True))
        dz_ref[...] = dz
        dg_ref[0] += jnp.sum(dz * y, axis=0, keepdims=True)
        dy = (g1p * dz).astype(wo.dtype)
        dya_ref[...] = _dot_nt(dy, wa)
        dyb_ref[...] = _dot_nt(dy, wb)
        dwo_ref[0:GLA_WIDTH, :] += _dot_tn(ya16, dy)
        dwo_ref[GLA_WIDTH:, :] += _dot_tn(yb16, dy)

    row = lambda i: (i, 0)
    const = lambda i: (0, 0)
    per_ex = pl.BlockSpec((1, 1, D_MODEL), lambda i: (i // tpe, 0, 0))
    return pl.pallas_call(
        body, name="out_block", grid=(n // tm,),
        in_specs=[pl.BlockSpec((tm, D_MODEL), row), pl.BlockSpec((tm, D_MODEL), row),
                  pl.BlockSpec((tm, 512), row), pl.BlockSpec((tm, 512), row), per_ex,
                  pl.BlockSpec((D_MODEL, D_MODEL), const), pl.BlockSpec((1, D_MODEL), const),
                  pl.BlockSpec((1, D_MODEL), const)],
        out_specs=[pl.BlockSpec((tm, D_MODEL), row), pl.BlockSpec((tm, 512), row), pl.BlockSpec((tm, 512), row),
                   pl.BlockSpec((D_MODEL, D_MODEL), const), per_ex,
                   pl.BlockSpec((1, D_MODEL), const), pl.BlockSpec((1, D_MODEL), const),
                   pl.BlockSpec((1, 1), const)],
        out_shape=[jax.ShapeDtypeStruct((n, D_MODEL), F32), jax.ShapeDtypeStruct((n, 512), F32),
                   jax.ShapeDtypeStruct((n, 512), F32), jax.ShapeDtypeStruct((D_MODEL, D_MODEL), F32),
                   jax.ShapeDtypeStruct((bl, 1, D_MODEL), F32), jax.ShapeDtypeStruct((1, D_MODEL), F32),
                   jax.ShapeDtypeStruct((1, D_MODEL), F32), jax.ShapeDtypeStruct((1, 1), F32)],
        compiler_params=_params("arbitrary"),
    )(x2, tgt2, ya, yb, g1p3, wo, lnw, lnb)


def _proj_bwd_x(ds, ws, x2, dz, sc3, seq, tm=512):
    n = x2.shape[0]
    tpe = seq // tm
    bl = n // seq

    def body(da_ref, db_ref, dc_ref, dd1_ref, dd2_ref, wa_ref, wb_ref, wc_ref, wd_ref, x_ref, dz_ref, sc_ref,
             gx_ref, dsh_ref, dsc_ref):
        i = pl.program_id(0)

        @pl.when(i % tpe == 0)
        def _():
            dsh_ref[...] = jnp.zeros_like(dsh_ref)
            dsc_ref[...] = jnp.zeros_like(dsc_ref)

        cdt = ws[0].dtype
        dh = _dot(da_ref[...].astype(cdt), wa_ref[...])
        dh += _dot(db_ref[...].astype(cdt), wb_ref[...])
        dh += _dot(dc_ref[...].astype(cdt), wc_ref[...])
        dh += _dot((dd1_ref[...] + dd2_ref[...]).astype(cdt), wd_ref[...])
        gx_ref[...] = dh * sc_ref[0] + ALPHA * dz_ref[...]
        dsh_ref[0] += jnp.sum(dh, axis=0, keepdims=True)
        dsc_ref[0] += jnp.sum(dh * x_ref[...], axis=0, keepdims=True)

    row = lambda i: (i, 0)
    const = lambda i: (0, 0)
    per_ex = pl.BlockSpec((1, 1, D_MODEL), lambda i: (i // tpe, 0, 0))
    da, db, dc, (dd1, dd2) = ds
    return pl.pallas_call(
        body, name="proj_bwd_x", grid=(n // tm,),
        in_specs=[pl.BlockSpec((tm, d.shape[1]), row) for d in (da, db, dc, dd1, dd2)]
        + [pl.BlockSpec(w.shape, const) for w in ws]
        + [pl.BlockSpec((tm, D_MODEL), row), pl.BlockSpec((tm, D_MODEL), row), per_ex],
        out_specs=[pl.BlockSpec((tm, D_MODEL), row), per_ex, per_ex],
        out_shape=[jax.ShapeDtypeStruct((n, D_MODEL), F32), jax.ShapeDtypeStruct((bl, 1, D_MODEL), F32),
                   jax.ShapeDtypeStruct((bl, 1, D_MODEL), F32)],
        compiler_params=_params("arbitrary"),
    )(da, db, dc, dd1, dd2, *ws, x2, dz, sc3)


def _proj_bwd_w(x2, sc3, sh3, ds, seq, cdt, name, tm=1024):
    n = x2.shape[0]
    tpe = seq // tm
    flat, groups = [], []
    for d in ds:
        parts = d if isinstance(d, tuple) else (d,)
        groups.append(len(parts))
        flat.extend(parts)
    nin = len(flat)

    def body(x_ref, sc_ref, sh_ref, *refs):
        i = pl.program_id(0)
        outs = refs[nin:]

        @pl.when(i == 0)
        def _():
            for o in outs:
                o[...] = jnp.zeros_like(o)

        h = (x_ref[...] * sc_ref[0] + sh_ref[0]).astype(cdt)
        pos = 0
        for o, cnt in zip(outs, groups):
            d = refs[pos][...]
            for extra in refs[pos + 1:pos + cnt]:
                d = d + extra[...]
            pos += cnt
            o[...] += _dot_tn(d.astype(cdt), h)

    row = lambda i: (i, 0)
    const = lambda i: (0, 0)
    per_ex = pl.BlockSpec((1, 1, D_MODEL), lambda i: (i // tpe, 0, 0))
    widths = [(d[0] if isinstance(d, tuple) else d).shape[1] for d in ds]
    return pl.pallas_call(
        body, name=name, grid=(n // tm,),
        in_specs=[pl.BlockSpec((tm, D_MODEL), row), per_ex, per_ex]
        + [pl.BlockSpec((tm, d.shape[1]), row) for d in flat],
        out_specs=[pl.BlockSpec((w, D_MODEL), const) for w in widths],
        out_shape=[jax.ShapeDtypeStruct((w, D_MODEL), F32) for w in widths],
        compiler_params=_params("arbitrary"),
    )(x2, sc3, sh3, *flat)


def _mod_block(c_all, w_ada_sh, b_blk):
    def body(c_ref, w_ref, b_ref, o_ref):
        o_ref[...] = _dot(c_ref[...], w_ref[...]) + b_ref[...]

    return pl.pallas_call(
        body, name="mod_block",
        out_shape=jax.ShapeDtypeStruct((c_all.shape[0], w_ada_sh.shape[1]), F32),
        compiler_params=pltpu.CompilerParams(vmem_limit_bytes=VMEM_LIMIT),
    )(c_all, w_ada_sh, b_blk)


def _ada_grads(c_all, dmod_all, dmod_blk):
    def body(c_ref, da_ref, db_ref, gw_ref, gb_ref):
        gw_ref[...] = _dot_tn(c_ref[...], db_ref[...])
        gb_ref[...] = jnp.sum(da_ref[...], axis=0, keepdims=True)

    return pl.pallas_call(
        body, name="ada_grads",
        out_shape=[jax.ShapeDtypeStruct((c_all.shape[1], dmod_blk.shape[1]), F32),
                   jax.ShapeDtypeStruct((1, dmod_all.shape[1]), F32)],
        compiler_params=pltpu.CompilerParams(vmem_limit_bytes=VMEM_LIMIT),
    )(c_all, dmod_all, dmod_blk)


def _sum_leading(parts, name):
    def body(p_ref, o_ref):
        acc = p_ref[0]
        for d in range(1, parts.shape[0]):
            acc = acc + p_ref[d]
        o_ref[...] = acc

    return pl.pallas_call(
        body, name=name, out_shape=jax.ShapeDtypeStruct(parts.shape[1:], F32),
        compiler_params=pltpu.CompilerParams(vmem_limit_bytes=VMEM_LIMIT),
    )(parts)


ELEMENTWISE_BLOCK_BYTES = 2 * 1024 * 1024


def _tile2d(rows, cols, row_align=8):
    if rows * cols * 4 <= ELEMENTWISE_BLOCK_BYTES:
        return rows, cols
    fits = [t for t in range(row_align, rows, row_align) if rows % t == 0 and t * cols * 4 <= ELEMENTWISE_BLOCK_BYTES]
    if fits:
        return fits[-1], cols
    fits = [t for t in range(128, cols, 128) if cols % t == 0 and rows * t * 4 <= ELEMENTWISE_BLOCK_BYTES]
    assert fits, (rows, cols)
    return rows, fits[-1]


def _add_n(arrs, name, out_dtypes=(F32,)):
    rows, cols = arrs[0].shape
    narrow = any(jnp.dtype(dt).itemsize < 4 for dt in tuple(out_dtypes) + tuple(a.dtype for a in arrs))
    tr, tc = _tile2d(rows, cols, 16 if narrow else 8)
    n_in = len(arrs)

    def body(*refs):
        acc = refs[0][...].astype(F32)
        for r in refs[1:n_in]:
            acc = acc + r[...].astype(F32)
        for o in refs[n_in:]:
            o[...] = acc.astype(o.dtype)

    spec = pl.BlockSpec((tr, tc), lambda i, j: (i, j))
    return pl.pallas_call(
        body, name=name, grid=(rows // tr, cols // tc), in_specs=[spec] * n_in, out_specs=[spec] * len(out_dtypes),
        out_shape=[jax.ShapeDtypeStruct((rows, cols), dt) for dt in out_dtypes],
        compiler_params=_params("parallel", "parallel"),
    )(*arrs)


def _chip_sum_blocks(a, b, per, blocks, name, chunk=128):
    rows, cols = a.shape
    padded = -(-per // 16) * 16
    assert rows >= (blocks - 1) * per + padded, (rows, per, blocks)

    def body(a_ref, b_ref, o_ref, o16_ref):
        for j in range(blocks):
            for r0 in range(0, padded, chunk):
                n_rows = min(chunk, padded - r0)
                src = pl.ds(j * per + r0, n_rows)
                s = a_ref[src, :] + b_ref[src, :]
                if per - r0 < n_rows:
                    s = jnp.where(_iota((n_rows, 1), 0) < per - r0, s, 0.0)
                o_ref[j, r0:r0 + n_rows, :] = s
                o16_ref[j, r0:r0 + n_rows, :] = s.astype(BF16)

    return pl.pallas_call(
        body, name=name,
        out_shape=[jax.ShapeDtypeStruct((blocks, padded, cols), F32), jax.ShapeDtypeStruct((blocks, padded, cols), BF16)],
        compiler_params=pltpu.CompilerParams(vmem_limit_bytes=VMEM_LIMIT),
    )(a, b)


GRAD_PAD_ROWS = 16


def _adamw_rule(w, g, m, v):
    c1 = 1.0 / (1.0 - ADAM_B1 ** ADAM_STEP)
    c2 = 1.0 / (1.0 - ADAM_B2 ** ADAM_STEP)
    nm = ADAM_B1 * m + (1.0 - ADAM_B1) * g
    nv = ADAM_B2 * v + (1.0 - ADAM_B2) * (g * g)
    return -ADAM_LR * ((nm * c1) / (jnp.sqrt(nv * c2) + ADAM_EPS) + ADAM_WD * w), nm, nv


def _adamw_many(ws, gs, ms, vs, name):
    k = len(ws)

    def body(*refs):
        ins, outs = refs[:4 * k], refs[4 * k:]
        for i in range(k):
            res = _adamw_rule(ins[i][...], ins[k + i][...], ins[2 * k + i][...], ins[3 * k + i][...])
            for j in range(3):
                outs[3 * i + j][...] = res[j]

    return pl.pallas_call(
        body, name=name,
        out_shape=[jax.ShapeDtypeStruct(w.shape, F32) for w in ws for _ in range(3)],
        compiler_params=pltpu.CompilerParams(vmem_limit_bytes=VMEM_LIMIT),
    )(*ws, *gs, *ms, *vs)


def _adamw(w, g, m, v, name):
    rows, cols = w.shape
    tr, tc = _tile2d(rows, cols)
    c1 = 1.0 / (1.0 - ADAM_B1 ** ADAM_STEP)
    c2 = 1.0 / (1.0 - ADAM_B2 ** ADAM_STEP)

    def body(w_ref, g_ref, m_ref, v_ref, d_ref, nm_ref, nv_ref):
        gg = g_ref[...]
        nm = ADAM_B1 * m_ref[...] + (1.0 - ADAM_B1) * gg
        nv = ADAM_B2 * v_ref[...] + (1.0 - ADAM_B2) * (gg * gg)
        nm_ref[...] = nm
        nv_ref[...] = nv
        d_ref[...] = -ADAM_LR * ((nm * c1) / (jnp.sqrt(nv * c2) + ADAM_EPS) + ADAM_WD * w_ref[...])

    spec = pl.BlockSpec((tr, tc), lambda i, j: (i, j))
    shp = jax.ShapeDtypeStruct((rows, cols), F32)
    return pl.pallas_call(
        body, name=name, grid=(rows // tr, cols // tc), in_specs=[spec] * 4, out_specs=[spec] * 3,
        out_shape=[shp, shp, shp], compiler_params=_params("parallel", "parallel"),
    )(w, g, m, v)


def _coords():
    return lax.axis_index("x"), lax.axis_index("y"), lax.axis_index("c")


def _all_gather8(blk, name):
    m_per, n = blk.shape

    def body(x_ref, out_ref, send_sems, recv_sems, local_sem):
        x, y, c = _coords()
        me, sibling = (x, y, c), (x, y, 1 - c)
        chips = [(1 - x, y), (x, 1 - y), (1 - x, 1 - y)]

        def rows(px, py, pc):
            return out_ref.at[pl.ds((4 * px + 2 * py + pc) * m_per, m_per), :]

        def copy(k, block, to, src=None):
            return pltpu.make_async_remote_copy(
                src_ref=rows(*block) if src is None else src, dst_ref=rows(*block),
                send_sem=send_sems.at[k], recv_sem=recv_sems.at[k], device_id=to, device_id_type=MESH)

        mine = pltpu.make_async_copy(x_ref, rows(*me), local_sem)
        mine.start()
        first = [copy(0, me, sibling, src=x_ref)]
        first += [copy(1 + j, me, (*chip, c), src=x_ref) for j, chip in enumerate(chips)]
        for cp in first:
            cp.start()
        passed = [copy(4 + j, (*chip, c), sibling) for j, chip in enumerate(chips)]
        for j, chip in enumerate(chips):
            copy(1 + j, (*chip, c), me).wait_recv()
            passed[j].start()
        copy(0, sibling, me).wait_recv()
        for j, chip in enumerate(chips):
            copy(4 + j, (*chip, 1 - c), me).wait_recv()
        for cp in first + passed:
            cp.wait_send()
        mine.wait()

    return pl.pallas_call(
        body, name=name,
        out_shape=jax.ShapeDtypeStruct((8 * m_per, n), blk.dtype),
        in_specs=[pl.BlockSpec(memory_space=pltpu.VMEM)],
        out_specs=pl.BlockSpec(memory_space=pltpu.VMEM),
        scratch_shapes=[pltpu.SemaphoreType.DMA((7,)), pltpu.SemaphoreType.DMA((7,)), pltpu.SemaphoreType.DMA],
        compiler_params=pltpu.CompilerParams(vmem_limit_bytes=VMEM_LIMIT),
    )(blk)


def _chip_gather(shards, split, name):
    k_arr = len(shards)

    def body(*refs):
        srcs, dsts = refs[:k_arr], refs[k_arr:2 * k_arr]
        send_sems, recv_sems, fwd_send_sems, fwd_recv_sems, local_sems = refs[2 * k_arr:]
        x, y, c = _coords()
        peers = [(1 - x, y, c), (x, 1 - y, c), (1 - x, 1 - y, c)]
        sibling = (x, y, 1 - c)
        me_chip = 2 * x + y

        def part(ref, a, core):
            if not split[a]:
                return ref
            half = shards[a].shape[1] // 2
            return ref.at[:, pl.ds(core * half, half)]

        def ici(a, j, src_chip, dst_dev):
            return pltpu.make_async_remote_copy(
                src_ref=part(srcs[a], a, c), dst_ref=part(dsts[a].at[src_chip], a, c),
                send_sem=send_sems.at[a, j], recv_sem=recv_sems.at[a, j], device_id=dst_dev, device_id_type=MESH)

        def d2d(a, j, src_chip, core):
            return pltpu.make_async_remote_copy(
                src_ref=part(dsts[a].at[src_chip], a, core), dst_ref=part(dsts[a].at[src_chip], a, core),
                send_sem=fwd_send_sems.at[a, j], recv_sem=fwd_recv_sems.at[a, j],
                device_id=sibling, device_id_type=MESH)

        local = [pltpu.make_async_copy(srcs[a], dsts[a].at[me_chip], local_sems.at[a]) for a in range(k_arr)]
        for cp in local:
            cp.start()
        sends = [ici(a, j, me_chip, peer) for a in range(k_arr) for j, peer in enumerate(peers)]
        for cp in sends:
            cp.start()
        forwards = []
        for a in range(k_arr):
            for j, peer in enumerate(peers):
                peer_chip = 2 * peer[0] + peer[1]
                ici(a, j, peer_chip, peer).wait_recv()
                if split[a]:
                    forwards.append(d2d(a, j, peer_chip, c))
                    forwards[-1].start()
        for a in range(k_arr):
            for j, peer in enumerate(peers):
                if split[a]:
                    d2d(a, j, 2 * peer[0] + peer[1], 1 - c).wait_recv()
        for cp in sends + forwards:
            cp.wait_send()
        for cp in local:
            cp.wait()

    any_spec = pl.BlockSpec(memory_space=pl.ANY)
    return pl.pallas_call(
        body, name=name,
        out_shape=[jax.ShapeDtypeStruct((4,) + s.shape, s.dtype) for s in shards],
        in_specs=[any_spec] * k_arr, out_specs=[any_spec] * k_arr,
        scratch_shapes=[pltpu.SemaphoreType.DMA((k_arr, 3))] * 4 + [pltpu.SemaphoreType.DMA((k_arr,))],
    )(*shards)


def _chip_scatter(pieces, name):
    k_arr = len(pieces)

    def body(*refs):
        srcs, dsts = refs[:k_arr], refs[k_arr:2 * k_arr]
        send_sems, recv_sems = refs[2 * k_arr:]
        x, y, c = _coords()
        peers = [(1 - x, y, c), (x, 1 - y, c), (1 - x, 1 - y, c)]
        copies = []
        for a in range(k_arr):
            for j, peer in enumerate(peers):
                copies.append(pltpu.make_async_remote_copy(
                    src_ref=srcs[a].at[2 * peer[0] + peer[1]], dst_ref=dsts[a].at[j],
                    send_sem=send_sems.at[a, j], recv_sem=recv_sems.at[a, j], device_id=peer, device_id_type=MESH))
        for cp in copies:
            cp.start()
        for cp in copies:
            cp.wait_recv()
        for cp in copies:
            cp.wait_send()

    any_spec = pl.BlockSpec(memory_space=pl.ANY)
    return pl.pallas_call(
        body, name=name,
        out_shape=[jax.ShapeDtypeStruct((3,) + p.shape[1:], p.dtype) for p in pieces],
        in_specs=[any_spec] * k_arr, out_specs=[any_spec] * k_arr,
        scratch_shapes=[pltpu.SemaphoreType.DMA((k_arr, 3)), pltpu.SemaphoreType.DMA((k_arr, 3))],
    )(*pieces)


def _sibling_swap(arrs, name):
    k_arr = len(arrs)

    def body(*refs):
        srcs, dsts = refs[:k_arr], refs[k_arr:2 * k_arr]
        send_sems, recv_sems = refs[2 * k_arr:]
        x, y, c = _coords()
        copies = [pltpu.make_async_remote_copy(
            src_ref=srcs[a], dst_ref=dsts[a], send_sem=send_sems.at[a], recv_sem=recv_sems.at[a],
            device_id=(x, y, 1 - c), device_id_type=MESH) for a in range(k_arr)]
        for cp in copies:
            cp.start()
        for cp in copies:
            cp.wait_recv()
        for cp in copies:
            cp.wait_send()

    any_spec = pl.BlockSpec(memory_space=pl.ANY)
    return pl.pallas_call(
        body, name=name,
        out_shape=[jax.ShapeDtypeStruct(a.shape, a.dtype) for a in arrs],
        in_specs=[any_spec] * k_arr, out_specs=[any_spec] * k_arr,
        scratch_shapes=[pltpu.SemaphoreType.DMA((k_arr,)), pltpu.SemaphoreType.DMA((k_arr,))],
    )(*arrs)


def _split_w_in(w_in_t):
    wa = jnp.concatenate([w_in_t[0:1024], w_in_t[1040:1552]], axis=0)
    wb = w_in_t[1552:3088]
    wc = w_in_t[3096:3608]
    wd = jnp.concatenate([w_in_t[1024:1040], w_in_t[3088:3096],
                          jnp.zeros((128 - SMALL_USED, w_in_t.shape[1]), w_in_t.dtype)], axis=0)
    return wa, wb, wc, wd


def _merge_dw_in(dwa, dwb, dwc, dwd):
    return jnp.concatenate([dwa[0:1024], dwd[0:GLA_RANK], dwa[1024:1536], dwb, dwd[GLA_RANK:SMALL_USED], dwc,
                            jnp.zeros((GRAD_PAD_ROWS, dwa.shape[1]), dwa.dtype)], axis=0)


def _local_step(x, mod, w_in16, w_out16, gla_wg, gla_bg, gla_nw, conv_w, a_log, dt_bias, gdn_nw, ln_w, ln_b, tgt):
    bl, seq, _ = x.shape
    n = bl * seq
    x2 = x.reshape(n, D_MODEL)
    tgt2 = tgt.reshape(n, D_MODEL)
    sh3 = mod[:, None, 0:D_MODEL]
    sc3 = 1.0 + mod[:, None, D_MODEL:2 * D_MODEL]
    g1p3 = 1.0 + mod[:, None, 2 * D_MODEL:]
    ws = _split_w_in(w_in16)
    wg = jnp.concatenate([gla_wg, jnp.zeros((128 - GLA_RANK, GLA_QK), F32)], axis=0)
    cw8 = jnp.concatenate([conv_w, jnp.zeros((8 - CONV_K, conv_w.shape[1]), F32)], axis=0)
    alog_v = jnp.zeros((1, 128), F32).at[:, LANE_A:LANE_A + GDN_HEADS].set(a_log)
    dtb_v = jnp.zeros((1, 128), F32).at[:, LANE_A:LANE_A + GDN_HEADS].set(dt_bias)

    pa, pb, pc, pd = _proj_fwd(x2, sc3, sh3, ws, seq)
    ya, st_a = _gla_fwd(pa, pd, wg, gla_bg, gla_nw, bl, seq)
    qkv, gb, conv_out = _gdn_pre_fwd(pb, pd, cw8, alog_v, dtb_v, bl, seq)
    yb, st_b = _gdn_fwd(qkv, gb, pc, gdn_nw, bl, seq)
    dz, dya, dyb, d_wo, d_gate, d_lnw, d_lnb, loss = _out_block(x2, tgt2, ya, yb, g1p3, w_out16, ln_w, ln_b, seq)
    da, dd1, d_wg, d_bg, d_nwa = _gla_bwd(pa, pd, st_a, dya, wg, gla_bg, gla_nw, bl, seq)
    dqkv, dc, dgb, d_nwb = _gdn_bwd(qkv, gb, pc, st_b, dyb, gdn_nw, bl, seq)
    db, dd2, d_cw8, d_alog, d_dtb = _gdn_pre_bwd(pb, conv_out, pd, dqkv, dgb, cw8, alog_v, dtb_v, bl, seq)
    gx, d_sh, d_sc = _proj_bwd_x((da, db, dc, (dd1, dd2)), ws, x2, dz, sc3, seq)
    (dwa,) = _proj_bwd_w(x2, sc3, sh3, [da], seq, w_in16.dtype, "proj_bwd_w_a")
    dwb, dwc, dwd = _proj_bwd_w(x2, sc3, sh3, [db, dc, (dd1, dd2)], seq, w_in16.dtype, "proj_bwd_w_bcd")
    grads = dict(
        w_in=_merge_dw_in(dwa, dwb, dwc, dwd),
        w_out=d_wo,
        gla_w_gate_up=d_wg[0:GLA_RANK, :],
        gla_b_gate=d_bg,
        gla_norm_w=d_nwa,
        gdn_conv_w=d_cw8[0:CONV_K, :],
        gdn_a_log=d_alog[:, LANE_A:LANE_A + GDN_HEADS],
        gdn_dt_bias=d_dtb[:, LANE_A:LANE_A + GDN_HEADS],
        gdn_norm_w=d_nwb,
        ln_w=d_lnw,
        ln_b=d_lnb,
        mod=jnp.concatenate([d_sh[:, 0, :], d_sc[:, 0, :], d_gate[:, 0, :]], axis=1),
    )
    return loss, gx.reshape(bl, seq, D_MODEL), grads


_SMALL = (("gla_b_gate", 256), ("gla_norm_w", 128), ("gdn_a_log", 4), ("gdn_dt_bias", 4), ("gdn_norm_w", 128),
          ("ln_w", 1024), ("ln_b", 1024), ("gla_w_gate_up", 16 * 256), ("gdn_conv_w", 4 * 1536), ("loss", 1),
          ("mod", 2 * 3072))


def _pack_small(grads):
    flat = jnp.concatenate([grads[k].reshape(-1) for k, _ in _SMALL])
    total = sum(sz for _, sz in _SMALL)
    rows = -(-total // 1024) * 8
    return jnp.concatenate([flat, jnp.zeros((rows * 128 - total,), F32)]).reshape(rows, 128)


def _unpack_small(flat):
    out, pos = {}, 0
    for k, sz in _SMALL:
        out[k] = flat[pos:pos + sz]
        pos += sz
    return out


def kernel(x, c, w_ada, b_ada, w_in, gla_w_gate_up, gla_b_gate, gla_norm_w, gdn_conv_w, gdn_a_log, gdn_dt_bias, gdn_norm_w, w_out, ln_w, ln_b, loss_target, m_w_ada, m_b_ada, m_w_in, m_gla_w_gate_up, m_gla_b_gate, m_gla_norm_w, m_gdn_conv_w, m_gdn_a_log, m_gdn_dt_bias, m_gdn_norm_w, m_w_out, m_ln_w, m_ln_b, v_w_ada, v_b_ada, v_w_in, v_gla_w_gate_up, v_gla_b_gate, v_gla_norm_w, v_gdn_conv_w, v_gdn_a_log, v_gdn_dt_bias, v_gdn_norm_w, v_w_out, v_ln_w, v_ln_b):
    ix, iy, ic = _coords()
    chip = 2 * ix + iy
    dev = 4 * ix + 2 * iy + ic
    bl = x.shape[0]
    ndev = 8

    c_all = _all_gather8(c.reshape(8, -1), "gather_c").reshape(ndev * bl, D_MODEL)
    ada_cols = w_ada.shape[2]
    b_blk = lax.dynamic_slice_in_dim(b_ada, chip * ada_cols, ada_cols, axis=1)
    mod_blk = _mod_block(c_all, w_ada[0], b_blk)
    mod_g = _all_gather8(mod_blk, "gather_mod").reshape(ndev, ndev * bl, ada_cols)
    mod_all = jnp.concatenate([mod_g[2 * j] for j in range(4)], axis=1)
    mod = lax.dynamic_slice_in_dim(mod_all, dev * bl, bl, axis=0)

    w_in_g, w_out_g, wg_g, cw_g = _chip_gather(
        [jnp.transpose(w_in[0]).astype(BF16), w_out[0].astype(BF16), gla_w_gate_up[0], gdn_conv_w[0]],
        [True, True, False, False], "gather_weights")
    w_in16 = w_in_g.reshape(IN_COLS, D_MODEL)
    w_out16 = w_out_g.reshape(D_MODEL, D_MODEL)
    gla_wg = jnp.concatenate([wg_g[j] for j in range(4)], axis=1)
    conv_w = jnp.concatenate([cw_g[j] for j in range(4)], axis=1)

    loss, grad_x, gr = _local_step(x, mod, w_in16, w_out16, gla_wg, gla_b_gate, gla_norm_w, conv_w,
                                   gdn_a_log, gdn_dt_bias, gdn_norm_w, ln_w, ln_b, loss_target)

    gr["loss"] = loss
    packed = _pack_small(gr)
    prow = packed.shape[0]
    gathered = _all_gather8(packed, "gather_small").reshape(ndev, prow, 128)
    small = _unpack_small(_sum_leading(gathered, "sum_small").reshape(-1))
    loss = small["loss"][0]
    mod_rows = gathered.reshape(ndev, prow * 128)[:, sum(sz for _, sz in _SMALL[:-1]):][:, :bl * 3 * D_MODEL]
    dmod_all = mod_rows.reshape(ndev * bl, 3 * D_MODEL)
    dmod_blk = lax.dynamic_slice_in_dim(dmod_all, chip * ada_cols, ada_cols, axis=1)
    g_w_ada, g_b_ada = _ada_grads(c_all, dmod_all, dmod_blk)
    wg_cols = gla_w_gate_up.shape[2]
    g_wg = lax.dynamic_slice_in_dim(small["gla_w_gate_up"].reshape(GLA_RANK, GLA_QK), chip * wg_cols, wg_cols, axis=1)
    cw_cols = gdn_conv_w.shape[2]
    g_cw = lax.dynamic_slice_in_dim(small["gdn_conv_w"].reshape(CONV_K, 3 * GDN_WIDTH), chip * cw_cols, cw_cols, axis=1)

    in_feats = w_in.shape[2]
    out_rows = w_out.shape[1]
    p_in = gr["w_in"]
    p_out = gr["w_out"].reshape(4, out_rows, D_MODEL)
    h_in, h_out = D_MODEL // 2, out_rows // 2
    mine_in = lax.dynamic_slice_in_dim(p_in, ic * h_in, h_in, axis=1)
    mine_out = lax.dynamic_slice_in_dim(p_out, ic * h_out, h_out, axis=1)
    theirs_in = lax.dynamic_slice_in_dim(p_in, (1 - ic) * h_in, h_in, axis=1)
    theirs_out = lax.dynamic_slice_in_dim(p_out, (1 - ic) * h_out, h_out, axis=1)
    got_in, got_out = _sibling_swap([theirs_in, theirs_out], "swap_halves")
    chip_in, chip_in16 = _chip_sum_blocks(mine_in, got_in, in_feats, 4, "chip_sum_in")
    chip_out, chip_out16 = _add_n([mine_out.reshape(4 * h_out, D_MODEL), got_out.reshape(4 * h_out, D_MODEL)],
                                  "chip_sum_out", (F32, BF16))
    chip_out = chip_out.reshape(4, h_out, D_MODEL)
    rs_in, rs_out = _chip_scatter([chip_in16, chip_out16.reshape(4, h_out, D_MODEL)], "scatter_grads")
    own_in = lax.dynamic_index_in_dim(chip_in, chip, axis=0, keepdims=False)
    own_out = lax.dynamic_index_in_dim(chip_out, chip, axis=0, keepdims=False)
    (half_in,) = _add_n([own_in, rs_in[0], rs_in[1], rs_in[2]], "reduce_in")
    (half_out,) = _add_n([own_out, rs_out[0], rs_out[1], rs_out[2]], "reduce_out")
    sib_in, sib_out = _sibling_swap([half_in, half_out], "swap_result")
    g_w_in_t = jnp.where(ic == 0, jnp.concatenate([half_in, sib_in], axis=1),
                         jnp.concatenate([sib_in, half_in], axis=1))[0:in_feats]
    g_w_out = jnp.where(ic == 0, jnp.concatenate([half_out, sib_out], axis=0),
                        jnp.concatenate([sib_out, half_out], axis=0))

    grads = dict(
        w_ada=g_w_ada[None], b_ada=g_b_ada, w_in=g_w_in_t, gla_w_gate_up=g_wg[None],
        gla_b_gate=small["gla_b_gate"].reshape(1, -1), gla_norm_w=small["gla_norm_w"].reshape(1, -1),
        gdn_conv_w=g_cw[None], gdn_a_log=small["gdn_a_log"].reshape(1, -1),
        gdn_dt_bias=small["gdn_dt_bias"].reshape(1, -1), gdn_norm_w=small["gdn_norm_w"].reshape(1, -1),
        w_out=g_w_out[None], ln_w=small["ln_w"].reshape(1, -1), ln_b=small["ln_b"].reshape(1, -1))
    weights = dict(w_ada=w_ada, b_ada=b_ada, w_in=w_in, gla_w_gate_up=gla_w_gate_up, gla_b_gate=gla_b_gate,
                   gla_norm_w=gla_norm_w, gdn_conv_w=gdn_conv_w, gdn_a_log=gdn_a_log, gdn_dt_bias=gdn_dt_bias,
                   gdn_norm_w=gdn_norm_w, w_out=w_out, ln_w=ln_w, ln_b=ln_b)
    m_in = dict(w_ada=m_w_ada, b_ada=m_b_ada, w_in=m_w_in, gla_w_gate_up=m_gla_w_gate_up, gla_b_gate=m_gla_b_gate,
                gla_norm_w=m_gla_norm_w, gdn_conv_w=m_gdn_conv_w, gdn_a_log=m_gdn_a_log, gdn_dt_bias=m_gdn_dt_bias,
                gdn_norm_w=m_gdn_norm_w, w_out=m_w_out, ln_w=m_ln_w, ln_b=m_ln_b)
    v_in = dict(w_ada=v_w_ada, b_ada=v_b_ada, w_in=v_w_in, gla_w_gate_up=v_gla_w_gate_up, gla_b_gate=v_gla_b_gate,
                gla_norm_w=v_gla_norm_w, gdn_conv_w=v_gdn_conv_w, gdn_a_log=v_gdn_a_log, gdn_dt_bias=v_gdn_dt_bias,
                gdn_norm_w=v_gdn_norm_w, w_out=v_w_out, ln_w=v_ln_w, ln_b=v_ln_b)
    names = list(weights)
    delta, new_m, new_v = {}, {}, {}
    small_names = [nm for nm in names if weights[nm].size * 4 <= ELEMENTWISE_BLOCK_BYTES // 16]
    flat2d = lambda t: t.reshape(-1, t.shape[-1])
    outs = _adamw_many([flat2d(weights[nm]) for nm in small_names], [flat2d(grads[nm]) for nm in small_names],
                       [flat2d(m_in[nm]) for nm in small_names], [flat2d(v_in[nm]) for nm in small_names], "adamw_small")
    for i, nm in enumerate(small_names):
        shp = weights[nm].shape
        delta[nm], new_m[nm], new_v[nm] = (o.reshape(shp) for o in outs[3 * i:3 * i + 3])
        grads[nm] = grads[nm].reshape(shp)
    for nm in names:
        if nm in small_names:
            continue
        shp = weights[nm].shape
        if nm == "w_in":
            to2d = lambda t: jnp.transpose(t[0])
            from2d = lambda t: jnp.transpose(t)[None]
            g2d = grads[nm]
        else:
            to2d = lambda t: t.reshape(-1, shp[-1])
            from2d = lambda t: t.reshape(shp)
            g2d = to2d(grads[nm])
        d, a, b = _adamw(to2d(weights[nm]), g2d, to2d(m_in[nm]), to2d(v_in[nm]), "adamw_" + nm)
        delta[nm], new_m[nm], new_v[nm] = from2d(d), from2d(a), from2d(b)
        grads[nm] = from2d(g2d)
    return (loss, grad_x, *[grads[k] for k in names], *[delta[k] for k in names],
            *[new_m[k] for k in names], *[new_v[k] for k in names])
```
